```python
import math
import jax, jax.numpy as jnp
from jax import lax
import numpy as np

D_MODEL = 1024
BATCH = 8
SEQ = 2048
DEPTH = 1

PLE_DIM = 256
GLA_HEADS = 4
GLA_DK = 128
GLA_DV = 256
GLA_RANK = 16
GLA_TAU = 16.0
GLA_CHUNK = 64
GLA_QK = GLA_HEADS * GLA_DK
GLA_V = GLA_HEADS * GLA_DV
MOBA_HEADS = 8
MOBA_HD = 128
MOBA_BLOCK = 256
MOBA_TOPK = 3
MOBA_QCHUNK = 128
MOBA_W = MOBA_HEADS * MOBA_HD
REL_BUCKETS = 32
REL_MAX_DIST = 4096
N_EXPERTS = 32
TOP_K = 4
D_FF = 1024
SWIGLU_LIMIT = 7.0
SWIGLU_ALPHA = 1.702
MOE_BLOCK = 512
EPS = 1e-6
SPLIT_SIZES = (GLA_QK, GLA_QK, GLA_V, GLA_V, GLA_RANK, MOBA_W, MOBA_W, MOBA_W, D_MODEL, D_MODEL)
D_IN = sum(SPLIT_SIZES)

kernel_name = 'hybrid_gla_moba_moe_block'


def rmsnorm(x, g):
    xf = x.astype(jnp.float32)
    y = xf * lax.rsqrt(jnp.mean(xf * xf, axis=-1, keepdims=True) + EPS)
    return (y * g.astype(jnp.float32)).astype(x.dtype)


def rel_bucket(n):
    n = jnp.maximum(n, 0)
    max_exact = REL_BUCKETS // 2
    nf = jnp.maximum(n, 1).astype(jnp.float32)
    large = max_exact + (jnp.log(nf / max_exact) / math.log(REL_MAX_DIST / max_exact)
                         * (REL_BUCKETS - max_exact)).astype(jnp.int32)
    large = jnp.minimum(large, REL_BUCKETS - 1)
    return jnp.where(n < max_exact, n, large)


def gla_mixer(q, k, v, g):
    B, S, H, DK = q.shape
    DV = v.shape[-1]
    L = GLA_CHUNK
    n = S // L

    def chunks(t):
        return t.astype(jnp.float32).reshape(B, n, L, H, t.shape[-1]).transpose(1, 0, 3, 2, 4)

    qc = chunks(q) * (DK ** -0.5)
    kc, vc, gc = chunks(k), chunks(v), chunks(g)
    causal = jnp.tril(jnp.ones((L, L), dtype=bool))[None, None, :, :, None]

    def step(state, inp):
        qi, ki, vi, gi = inp
        G = jnp.cumsum(gi, axis=2)
        inter = jnp.einsum('bhld,bhde->bhle', qi * jnp.exp(G), state)
        diff = G[:, :, :, None, :] - G[:, :, None, :, :]
        decay = jnp.exp(jnp.where(causal, diff, -jnp.inf))
        A = jnp.einsum('bhtd,bhsd,bhtsd->bhts', qi, ki, decay)
        intra = jnp.einsum('bhts,bhse->bhte', A, vi)
        G_last = G[:, :, -1:, :]
        new_state = (jnp.exp(G_last[:, :, 0, :, None]) * state
                     + jnp.einsum('bhld,bhle->bhde', ki * jnp.exp(G_last - G), vi))
        return new_state, inter + intra

    state0 = jnp.zeros((B, H, DK, DV), jnp.float32)
    _, out = lax.scan(step, state0, (qc, kc, vc, gc))
    return out.transpose(1, 0, 3, 2, 4).reshape(B, S, H, DV).astype(v.dtype)


def moba_mixer(q, k, v, rel_table):
    B, S, H, HD = q.shape
    BLK, QC = MOBA_BLOCK, MOBA_QCHUNK
    s_pad = -(-S // BLK) * BLK
    pad = ((0, 0), (0, s_pad - S), (0, 0), (0, 0))
    q, k, v = jnp.pad(q, pad), jnp.pad(k, pad), jnp.pad(v, pad)
    nb = s_pad // BLK
    nc = s_pad // QC
    kk = min(MOBA_TOPK, nb)
    qh = (q * (HD ** -0.5)).transpose(0, 2, 1, 3)
    kb = k.transpose(0, 2, 1, 3).reshape(B, H, nb, BLK, HD)
    vb = v.transpose(0, 2, 1, 3).reshape(B, H, nb, BLK, HD)
    kmean = jnp.mean(kb.astype(jnp.float32), axis=3)
    pos = jnp.arange(s_pad)
    qblk = pos // BLK
    score = jnp.einsum('bhsd,bhnd->bhsn', qh.astype(jnp.float32), kmean)
    past = jnp.arange(nb)[None, :] < qblk[:, None]
    score = jnp.where(past, score, -jnp.inf)
    _, sel = lax.top_k(score, kk)
    rel_t = rel_table.astype(jnp.float32).T
    hix = jnp.arange(H)[:, None]
    blk_off = jnp.arange(BLK)

    def attend(args):
        qq, ss, bi, ci = args
        qpos = ci * QC + jnp.arange(QC)
        qb = qpos // BLK
        kbb, vbb = kb[bi], vb[bi]
        ob = (ci * QC) // BLK
        k_own = lax.dynamic_index_in_dim(kbb, ob, axis=1, keepdims=False)
        v_own = lax.dynamic_index_in_dim(vbb, ob, axis=1, keepdims=False)
        dist = qpos[:, None] - (ob * BLK + blk_off)[None, :]
        lg = jnp.einsum('hqd,hkd->hqk', qq, k_own).astype(jnp.float32) + rel_t[:, rel_bucket(dist)]
        logits = [jnp.where(dist[None] >= 0, lg, -jnp.inf)]
        for s in range(kk):
            idx = ss[:, :, s]
            kg = kbb[hix, idx]
            kpos_s = idx[:, :, None] * BLK + blk_off
            bias = rel_t[hix[:, :, None], rel_bucket(qpos[None, :, None] - kpos_s)]
            lg = jnp.einsum('hqd,hqkd->hqk', qq, kg).astype(jnp.float32) + bias
            logits.append(jnp.where((s < qb)[None, :, None], lg, -jnp.inf))
        probs = jax.nn.softmax(jnp.concatenate(logits, axis=-1), axis=-1).astype(v.dtype)
        out = jnp.einsum('hqk,hkd->hqd', probs[..., :BLK], v_own)
        for s in range(kk):
            vg = vbb[hix, ss[:, :, s]]
            out = out + jnp.einsum('hqk,hqkd->hqd', probs[..., (s + 1) * BLK:(s + 2) * BLK], vg)
        return out

    qx = qh.reshape(B, H, nc, QC, HD).transpose(0, 2, 1, 3, 4).reshape(B * nc, H, QC, HD)
    sx = sel.reshape(B, H, nc, QC, kk).transpose(0, 2, 1, 3, 4).reshape(B * nc, H, QC, kk)
    bx = jnp.repeat(jnp.arange(B), nc)
    cx = jnp.tile(jnp.arange(nc), B)
    out = lax.map(attend, (qx, sx, bx, cx))
    out = out.reshape(B, nc, H, QC, HD).transpose(0, 1, 3, 2, 4).reshape(B, s_pad, H * HD)
    return out[:, :S]


def moe_ffn(h, w_router, b_router, w_gate, b_gate, w_up, b_up, w_down, b_down):
    B, S, D = h.shape
    T = B * S
    M = MOE_BLOCK
    hf = h.reshape(T, D)
    logits = (hf @ w_router + b_router).astype(jnp.float32)
    top_v, top_i = lax.top_k(logits, TOP_K)
    wts = jax.nn.softmax(top_v, axis=-1)
    A = T * TOP_K
    e_flat = top_i.reshape(A)
    order = jnp.argsort(e_flat)
    e_sorted = e_flat[order]
    counts = jnp.bincount(e_flat, length=N_EXPERTS)
    padded = (counts + M - 1) // M * M
    pad_end = jnp.cumsum(padded)
    pad_start = pad_end - padded
    start = jnp.cumsum(counts) - counts
    dest_sorted = pad_start[e_sorted] + (jnp.arange(A) - start[e_sorted])
    n_pad = (-(-A // M)) * M + N_EXPERTS * M
    n_blk = n_pad // M
    row_tok = jnp.full((n_pad,), T, jnp.int32).at[dest_sorted].set(order // TOP_K)
    blk_exp = jnp.minimum(jnp.searchsorted(pad_end, jnp.arange(n_blk) * M, side='right'), N_EXPERTS - 1)
    x_rows = jnp.concatenate([hf, jnp.zeros((1, D), hf.dtype)], axis=0)[row_tok].reshape(n_blk, M, D)

    def expert_block(args):
        xb, e = args
        gate = xb @ w_gate[e] + b_gate[e]
        up = xb @ w_up[e] + b_up[e]
        gate = jnp.minimum(gate, SWIGLU_LIMIT)
        up = jnp.clip(up, -SWIGLU_LIMIT, SWIGLU_LIMIT)
        glu = gate * jax.nn.sigmoid(gate * SWIGLU_ALPHA)
        return ((up + 1.0) * glu) @ w_down[e] + b_down[e]

    y_rows = lax.map(expert_block, (x_rows, blk_exp)).reshape(n_pad, D)
    dest = jnp.zeros((A,), jnp.int32).at[order].set(dest_sorted).reshape(T, TOP_K)
    y = jnp.einsum('tk,tkd->td', wts.astype(h.dtype), y_rows[dest])
    return y.reshape(B, S, D)


def setup_inputs(seed: int = 0) -> dict:
    key = jax.random.key(seed)
    ks = jax.random.split(key, 32)
    f32 = jnp.float32
    L = DEPTH

    def nrm(k, shape, scale):
        return jax.random.normal(k, shape, f32) * scale

    def gain(k, shape):
        return 1.0 + 0.05 * jax.random.normal(k, shape, f32)

    return {
        'x': nrm(ks[0], (BATCH, SEQ, D_MODEL), 1.0),
        'p': nrm(ks[1], (DEPTH, BATCH, SEQ, PLE_DIM), 1.0),
        'rel_bias': nrm(ks[2], (REL_BUCKETS, MOBA_HEADS), 0.5),
        'g_mix': gain(ks[3], (L, D_MODEL)),
        'w_in': nrm(ks[4], (L, D_MODEL, D_IN), D_MODEL ** -0.5),
        'w_a2': nrm(ks[5], (L, GLA_RANK, GLA_QK), GLA_RANK ** -0.5),
        'b_a2': nrm(ks[6], (L, GLA_QK), 0.1),
        'g_gla_out': gain(ks[7], (L, GLA_DV)),
        'w_up_gla': nrm(ks[8], (L, GLA_V, D_MODEL), GLA_V ** -0.5),
        'w_up_moba': nrm(ks[9], (L, MOBA_W, D_MODEL), MOBA_W ** -0.5),
        'w_o': nrm(ks[10], (L, D_MODEL, D_MODEL), D_MODEL ** -0.5),
        'g_ffn': gain(ks[11], (L, D_MODEL)),
        'w_router': nrm(ks[12], (L, D_MODEL, N_EXPERTS), D_MODEL ** -0.5),
        'b_router': nrm(ks[13], (L, N_EXPERTS), 0.01),
        'w_e_gate': nrm(ks[14], (L, N_EXPERTS, D_MODEL, D_FF), D_MODEL ** -0.5),
        'b_e_gate': nrm(ks[15], (L, N_EXPERTS, D_FF), 0.02),
        'w_e_up': nrm(ks[16], (L, N_EXPERTS, D_MODEL, D_FF), D_MODEL ** -0.5),
        'b_e_up': nrm(ks[17], (L, N_EXPERTS, D_FF), 0.02),
        'w_e_down': nrm(ks[18], (L, N_EXPERTS, D_FF, D_MODEL), D_FF ** -0.5),
        'b_e_down': nrm(ks[19], (L, N_EXPERTS, D_MODEL), 0.02),
        'g_ple_gate': gain(ks[20], (L, D_MODEL)),
        'w_ple_gate': nrm(ks[21], (L, D_MODEL, D_MODEL), D_MODEL ** -0.5),
        'w_ple_proj': nrm(ks[22], (L, PLE_DIM, D_MODEL), PLE_DIM ** -0.5),
        'g_ple_proj': gain(ks[23], (L, D_MODEL)),
        'g_final': gain(ks[24], (D_MODEL,)),
    }


def reference(x, p, rel_bias, g_mix, w_in, w_a2, b_a2, g_gla_out, w_up_gla, w_up_moba, w_o,
              g_ffn, w_router, b_router, w_e_gate, b_e_gate, w_e_up, b_e_up, w_e_down, b_e_down,
              g_ple_gate, w_ple_gate, w_ple_proj, g_ple_proj, g_final):
    B, S, D = x.shape
    points = []
    acc = 0
    for sz in SPLIT_SIZES[:-1]:
        acc += sz
        points.append(acc)
    for i in range(DEPTH):
        h = rmsnorm(x, g_mix[i])
        proj = h @ w_in[i]
        qa, ka, va, ra, a_low, qb, kb, vb, gate_a, gate_b = jnp.split(proj, points, axis=-1)
        a_logit = (a_low @ w_a2[i] + b_a2[i]).astype(jnp.float32)
        g_log = (jax.nn.log_sigmoid(a_logit) / GLA_TAU).reshape(B, S, GLA_HEADS, GLA_DK)
        o_a = gla_mixer(qa.reshape(B, S, GLA_HEADS, GLA_DK), ka.reshape(B, S, GLA_HEADS, GLA_DK),
                        va.reshape(B, S, GLA_HEADS, GLA_DV), g_log)
        o_a = rmsnorm(o_a, g_gla_out[i]).reshape(B, S, GLA_V) * jax.nn.silu(ra)
        u_a = o_a @ w_up_gla[i]
        o_b = moba_mixer(qb.reshape(B, S, MOBA_HEADS, MOBA_HD), kb.reshape(B, S, MOBA_HEADS, MOBA_HD),
                         vb.reshape(B, S, MOBA_HEADS, MOBA_HD), rel_bias)
        u_b = o_b @ w_up_moba[i]
        y = jax.nn.sigmoid(gate_a) * u_a + jax.nn.sigmoid(gate_b) * u_b
        x = x + y @ w_o[i]
        h2 = rmsnorm(x, g_ffn[i])
        x = x + moe_ffn(h2, w_router[i], b_router[i], w_e_gate[i], b_e_gate[i], w_e_up[i], b_e_up[i],
                        w_e_down[i], b_e_down[i])
        pg = jax.nn.sigmoid(rmsnorm(x, g_ple_gate[i]) @ w_ple_gate[i])
        x = x + pg * rmsnorm(p[i] @ w_ple_proj[i], g_ple_proj[i])
    return rmsnorm(x, g_final)
```

```python
import functools
import math

import jax
import jax.numpy as jnp
from jax import lax
from jax.experimental import pallas as pl
from jax.experimental.pallas import tpu as pltpu

F32 = jnp.float32
BF16 = jnp.bfloat16

D_MODEL = 1024
PLE_DIM = 256
GLA_HEADS = 4
GLA_DK = 128
GLA_DV = 256
GLA_RANK = 16
GLA_TAU = 16.0
GLA_QK = GLA_HEADS * GLA_DK
GLA_V = GLA_HEADS * GLA_DV
MOBA_HEADS = 8
MOBA_HD = 128
MOBA_BLOCK = 256
MOBA_TOPK = 3
MOBA_W = MOBA_HEADS * MOBA_HD
REL_BUCKETS = 32
REL_MAX_DIST = 4096
N_EXPERTS = 32
TOP_K = 4
D_FF = 1024
SWIGLU_LIMIT = 7.0
SWIGLU_ALPHA = 1.702
EPS = 1e-6

LANES = 128
NEG = -1e30
VMEM_LIMIT = 56 * 1024 * 1024

OFF_QA, OFF_KA, OFF_VA, OFF_RA = 0, 512, 1024, 2048
OFF_QB, OFF_KB, OFF_VB, OFF_GA, OFF_GB = 3072, 4096, 5120, 6144, 7168
D_PROJ = 8192

GLA_CHUNK = 128
EXPERT_BLOCK = 512

NT = (((1,), (1,)), ((), ()))
TN = (((0,), (0,)), ((), ()))


def _params(sem, vmem=None):
    return pltpu.CompilerParams(dimension_semantics=sem, vmem_limit_bytes=vmem)


def _rms(x, g):
    return x * lax.rsqrt(jnp.mean(x * x, axis=-1, keepdims=True) + EPS) * g


def _sigmoid(x):
    return 1.0 / (1.0 + jnp.exp(-x))


def _bias_kernel(tab_ref, bkt_ref, o_ref):
    strip = 8

    def body(s, carry):
        r0 = pl.multiple_of(s * strip, strip)
        b = bkt_ref[0, pl.ds(r0, strip), :]
        accs = [jnp.zeros(b.shape, F32) for _ in range(MOBA_HEADS)]
        for bb in range(REL_BUCKETS):
            hit = b == bb
            for h in range(MOBA_HEADS):
                accs[h] = jnp.where(hit, tab_ref[h, bb], accs[h])
        for h in range(MOBA_HEADS):
            o_ref[h, 0, pl.ds(r0, strip), :] = jnp.where(b < 0, NEG, accs[h])
        return carry

    lax.fori_loop(0, MOBA_BLOCK // strip, body, 0)


def _bias_tiles(rel_bias, nblk):
    i = jnp.arange(MOBA_BLOCK, dtype=jnp.int32)
    dist = (jnp.arange(nblk, dtype=jnp.int32)[:, None, None] * MOBA_BLOCK + i[None, :, None] - i[None, None, :])
    n = jnp.maximum(dist, 0)
    max_exact = REL_BUCKETS // 2
    nf = jnp.maximum(n, 1).astype(F32)
    large = max_exact + (jnp.log(nf / max_exact) / math.log(REL_MAX_DIST / max_exact)
                         * (REL_BUCKETS - max_exact)).astype(jnp.int32)
    large = jnp.minimum(large, REL_BUCKETS - 1)
    bkt = jnp.where(dist < 0, -1, jnp.where(n < max_exact, n, large)).astype(jnp.int32)
    tab = rel_bias.astype(F32).T
    return pl.pallas_call(
        _bias_kernel,
        grid=(nblk,),
        in_specs=[pl.BlockSpec(memory_space=pltpu.SMEM),
                  pl.BlockSpec((1, MOBA_BLOCK, MOBA_BLOCK), lambda m: (m, 0, 0))],
        out_specs=pl.BlockSpec((MOBA_HEADS, 1, MOBA_BLOCK, MOBA_BLOCK), lambda m: (0, m, 0, 0)),
        out_shape=jax.ShapeDtypeStruct((MOBA_HEADS, nblk, MOBA_BLOCK, MOBA_BLOCK), F32),
        compiler_params=_params(("arbitrary",)),
        name="bias_tiles",
    )(tab, bkt)


def _inproj_kernel(x_ref, g_ref, w_ref, cs_ref, wal_ref, wa2_ref, ba2_ref, o_ref, glog_ref, h_scr):
    @pl.when(pl.program_id(1) == 0)
    def _():
        h = _rms(x_ref[...], g_ref[...]).astype(BF16)
        h_scr[...] = h
        a_low = jnp.dot(h, wal_ref[...], preferred_element_type=F32)
        a = jnp.dot(a_low.astype(BF16), wa2_ref[...], preferred_element_type=F32) + ba2_ref[...]
        log_sig = jnp.minimum(a, 0.0) - jnp.log1p(jnp.exp(-jnp.abs(a)))
        glog_ref[...] = log_sig * (1.0 / GLA_TAU)

    acc = jnp.dot(h_scr[...], w_ref[...], preferred_element_type=F32)
    o_ref[...] = (acc * cs_ref[...]).astype(BF16)


def _inproj(x2, g_mix, w_main, colscale, w_alow, w_a2p, b_a2, tm=1024, tn=1024):
    T = x2.shape[0]
    return pl.pallas_call(
        _inproj_kernel,
        grid=(T // tm, D_PROJ // tn),
        in_specs=[pl.BlockSpec((tm, D_MODEL), lambda i, j: (i, 0)),
                  pl.BlockSpec((1, D_MODEL), lambda i, j: (0, 0)),
                  pl.BlockSpec((D_MODEL, tn), lambda i, j: (0, j)),
                  pl.BlockSpec((1, tn), lambda i, j: (0, j)),
                  pl.BlockSpec((D_MODEL, LANES), lambda i, j: (0, 0)),
                  pl.BlockSpec((LANES, GLA_QK), lambda i, j: (0, 0)),
                  pl.BlockSpec((1, GLA_QK), lambda i, j: (0, 0))],
        out_specs=[pl.BlockSpec((tm, tn), lambda i, j: (i, j)),
                   pl.BlockSpec((tm, GLA_QK), lambda i, j: (i, 0))],
        out_shape=[jax.ShapeDtypeStruct((T, D_PROJ), BF16),
                   jax.ShapeDtypeStruct((T, GLA_QK), F32)],
        scratch_shapes=[pltpu.VMEM((tm, D_MODEL), BF16)],
        compiler_params=_params(("arbitrary", "arbitrary"), VMEM_LIMIT),
        name="inproj",
    )(x2, g_mix, w_main, colscale, w_alow, w_a2p, b_a2)


def _gla_kernel(q_ref, k_ref, v_ref, r_ref, g_ref, gout_ref, o_ref, st_ref):
    C = GLA_CHUNK

    @pl.when(pl.program_id(1) == 0)
    def _():
        st_ref[...] = jnp.zeros(st_ref.shape, F32)

    row = lax.broadcasted_iota(jnp.int32, (C, C), 0)
    col = lax.broadcasted_iota(jnp.int32, (C, C), 1)
    causal = col <= row
    ltri = causal.astype(BF16)
    g = g_ref[...]
    g_hi = g.astype(BF16)
    g_lo = (g - g_hi.astype(F32)).astype(BF16)
    G = jnp.dot(ltri, g_hi, preferred_element_type=F32) + jnp.dot(ltri, g_lo, preferred_element_type=F32)
    mid = C // 2
    for h in range(GLA_HEADS):
        ks = slice(h * GLA_DK, (h + 1) * GLA_DK)
        vs = slice(h * GLA_DV, (h + 1) * GLA_DV)
        Gh = G[:, ks]
        qh = q_ref[:, ks].astype(F32)
        kh = k_ref[:, ks].astype(F32)
        vh = v_ref[:, vs]
        g_mid = Gh[mid:mid + 1, :]
        g_last = Gh[C - 1:C, :]
        q_in = (qh * jnp.exp(Gh)).astype(BF16)
        q_a = (qh * jnp.exp(Gh - g_mid)).astype(BF16)
        k_a = (kh * jnp.exp(g_mid - Gh)).astype(BF16)
        k_d = (kh * jnp.exp(g_last - Gh)).astype(BF16)
        A = lax.dot_general(q_a, k_a, NT, preferred_element_type=F32)
        A = jnp.where(causal, A, 0.0).astype(BF16)
        intra = jnp.dot(A, vh, preferred_element_type=F32)
        st = st_ref[h]
        inter = lax.dot_general(q_in, st.astype(BF16), NT, preferred_element_type=F32)
        o = inter + intra
        st_ref[h] = jnp.exp(g_last) * st + lax.dot_general(vh, k_d, TN, preferred_element_type=F32)
        r = r_ref[:, vs].astype(F32)
        o_ref[:, vs] = (_rms(o, gout_ref[...]) * (r * _sigmoid(r))).astype(BF16)


def _gla(proj, glog, g_gla_out, B, S):
    C = GLA_CHUNK
    nc = S // C
    return pl.pallas_call(
        _gla_kernel,
        grid=(B, nc),
        in_specs=[pl.BlockSpec((C, GLA_QK), lambda b, c: (b * nc + c, OFF_QA // GLA_QK)),
                  pl.BlockSpec((C, GLA_QK), lambda b, c: (b * nc + c, OFF_KA // GLA_QK)),
                  pl.BlockSpec((C, GLA_V), lambda b, c: (b * nc + c, OFF_VA // GLA_V)),
                  pl.BlockSpec((C, GLA_V), lambda b, c: (b * nc + c, OFF_RA // GLA_V)),
                  pl.BlockSpec((C, GLA_QK), lambda b, c: (b * nc + c, 0)),
                  pl.BlockSpec((1, GLA_DV), lambda b, c: (0, 0))],
        out_specs=pl.BlockSpec((C, GLA_V), lambda b, c: (b * nc + c, 0)),
        out_shape=jax.ShapeDtypeStruct((B * S, GLA_V), BF16),
        scratch_shapes=[pltpu.VMEM((GLA_HEADS, GLA_DV, GLA_DK), F32)],
        compiler_params=_params(("arbitrary", "arbitrary")),
        name="gla",
    )(proj, proj, proj, proj, glog, g_gla_out)


def _moba_kernel(q_ref, k_ref, v_ref, bias_ref, o_ref, km_scr, *, nblk):
    BLK = MOBA_BLOCK
    c = pl.program_id(2)

    @pl.when(c == 0)
    def _():
        for j in range(nblk):
            kb = k_ref[j * BLK:(j + 1) * BLK, :].astype(F32)
            km_scr[j:j + 1, :] = jnp.mean(kb, axis=0, keepdims=True)

    q = q_ref[...]
    for cc in range(nblk):
        @pl.when(c == cc)
        def _(cc=cc):
            pen_t = None
            if cc > MOBA_TOPK:
                s = lax.dot_general(km_scr[...], q.astype(F32), NT, preferred_element_type=F32,
                                    precision=lax.Precision.HIGHEST)
                ji = lax.broadcasted_iota(jnp.int32, s.shape, 0)
                cnt = jnp.zeros(s.shape, F32)
                for jp in range(cc):
                    sj = s[jp:jp + 1, :]
                    beats = (sj > s) | ((sj == s) & (jp < ji))
                    cnt = cnt + beats.astype(F32)
                pen = jnp.where((ji < cc) & (cnt < MOBA_TOPK), 0.0, NEG)
                pen_t = pen.T
            logits = []
            for j in range(cc + 1):
                kj = k_ref[j * BLK:(j + 1) * BLK, :]
                lg = lax.dot_general(q, kj, NT, preferred_element_type=F32) + bias_ref[0, cc - j]
                if pen_t is not None and j < cc:
                    lg = lg + pen_t[:, j:j + 1]
                logits.append(lg)
            m = logits[0].max(axis=-1, keepdims=True)
            for lg in logits[1:]:
                m = jnp.maximum(m, lg.max(axis=-1, keepdims=True))
            den = jnp.zeros(m.shape, F32)
            acc = jnp.zeros((BLK, MOBA_HD), F32)
            for j, lg in enumerate(logits):
                p = jnp.exp(lg - m)
                den = den + p.sum(axis=-1, keepdims=True)
                acc = acc + jnp.dot(p.astype(BF16), v_ref[j * BLK:(j + 1) * BLK, :], preferred_element_type=F32)
            o_ref[...] = (acc / den).astype(BF16)


def _moba(proj, bias, B, S):
    BLK = MOBA_BLOCK
    nblk = S // BLK
    H = MOBA_HEADS
    return pl.pallas_call(
        functools.partial(_moba_kernel, nblk=nblk),
        grid=(H, B, nblk),
        in_specs=[pl.BlockSpec((BLK, MOBA_HD), lambda h, b, c: (b * nblk + c, OFF_QB // MOBA_HD + h)),
                  pl.BlockSpec((S, MOBA_HD), lambda h, b, c: (b, OFF_KB // MOBA_HD + h)),
                  pl.BlockSpec((S, MOBA_HD), lambda h, b, c: (b, OFF_VB // MOBA_HD + h)),
                  pl.BlockSpec((1, nblk, BLK, BLK), lambda h, b, c: (h, 0, 0, 0))],
        out_specs=pl.BlockSpec((BLK, MOBA_HD), lambda h, b, c: (b * nblk + c, h)),
        out_shape=jax.ShapeDtypeStruct((B * S, MOBA_W), BF16),
        scratch_shapes=[pltpu.VMEM((nblk, MOBA_HD), F32)],
        compiler_params=_params(("arbitrary", "arbitrary", "arbitrary"), VMEM_LIMIT),
        name="moba",
    )(proj, proj, proj, bias)


def _merge_kernel(oa_ref, ob_ref, ga_ref, gb_ref, x_ref, wua_ref, wub_ref, wo_ref, gffn_ref, wr_ref, br_ref,
                  x1_ref, h2_ref, eidx_ref, wts_ref, rank_ref, cnt_ref, cnt_scr):
    tm = x_ref.shape[0]

    @pl.when(pl.program_id(0) == 0)
    def _():
        cnt_scr[...] = jnp.zeros(cnt_scr.shape, F32)

    u_a = jnp.dot(oa_ref[...], wua_ref[...], preferred_element_type=F32)
    u_b = jnp.dot(ob_ref[...], wub_ref[...], preferred_element_type=F32)
    y = _sigmoid(ga_ref[...].astype(F32)) * u_a + _sigmoid(gb_ref[...].astype(F32)) * u_b
    x1 = x_ref[...] + jnp.dot(y.astype(BF16), wo_ref[...], preferred_element_type=F32)
    x1_ref[...] = x1
    h2 = _rms(x1, gffn_ref[...])
    h2_ref[...] = h2.astype(BF16)
    logits = jnp.dot(h2, wr_ref[...], preferred_element_type=F32, precision=lax.Precision.HIGHEST) + br_ref[...]
    lane = lax.broadcasted_iota(jnp.int32, (tm, LANES), 1)
    lane_f = lane.astype(F32)
    work = jnp.where(lane < N_EXPERTS, logits, NEG)
    vals, idxs, hots = [], [], []
    for _ in range(TOP_K):
        mx = work.max(axis=-1, keepdims=True)
        idx = jnp.min(jnp.where(work == mx, lane_f, float(LANES)), axis=-1, keepdims=True)
        hot = lane_f == idx
        vals.append(mx)
        idxs.append(idx)
        hots.append(hot)
        work = jnp.where(hot, 2.0 * NEG, work)
    exps = [jnp.exp(v - vals[0]) for v in vals]
    den = exps[0] + exps[1] + exps[2] + exps[3]
    sel = jnp.zeros((tm, LANES), F32)
    for hot in hots:
        sel = sel + hot.astype(F32)
    row = lax.broadcasted_iota(jnp.int32, (tm, tm), 0)
    col = lax.broadcasted_iota(jnp.int32, (tm, tm), 1)
    below = (col < row).astype(BF16)
    rank_all = jnp.dot(below, sel.astype(BF16), preferred_element_type=F32) + cnt_scr[...]
    cnt_new = cnt_scr[...] + sel.sum(axis=0, keepdims=True)
    cnt_scr[...] = cnt_new
    cnt_ref[...] = cnt_new
    e_out = jnp.zeros((tm, LANES), F32)
    w_out = jnp.zeros((tm, LANES), F32)
    r_out = jnp.zeros((tm, LANES), F32)
    for k in range(TOP_K):
        rk = jnp.sum(jnp.where(hots[k], rank_all, 0.0), axis=-1, keepdims=True)
        e_out = jnp.where(lane == k, idxs[k], e_out)
        w_out = jnp.where(lane == k, exps[k] / den, w_out)
        r_out = jnp.where(lane == k, rk, r_out)
    eidx_ref[...] = e_out.astype(jnp.int32)
    wts_ref[...] = w_out
    rank_ref[...] = r_out.astype(jnp.int32)


def _merge(o_a, o_b, proj, x2, w_ua, w_ub, w_o, g_ffn, w_r, b_r, tm=512):
    T = x2.shape[0]
    full = lambda shape: pl.BlockSpec(shape, lambda i: (0, 0))
    rowblk = lambda w: pl.BlockSpec((tm, w), lambda i: (i, 0))
    return pl.pallas_call(
        _merge_kernel,
        grid=(T // tm,),
        in_specs=[rowblk(GLA_V), rowblk(MOBA_W),
                  pl.BlockSpec((tm, D_MODEL), lambda i: (i, OFF_GA // D_MODEL)),
                  pl.BlockSpec((tm, D_MODEL), lambda i: (i, OFF_GB // D_MODEL)),
                  rowblk(D_MODEL),
                  full((GLA_V, D_MODEL)), full((MOBA_W, D_MODEL)), full((D_MODEL, D_MODEL)),
                  full((1, D_MODEL)), full((D_MODEL, LANES)), full((1, LANES))],
        out_specs=[rowblk(D_MODEL), rowblk(D_MODEL), rowblk(LANES), rowblk(LANES), rowblk(LANES),
                   full((1, LANES))],
        out_shape=[jax.ShapeDtypeStruct((T, D_MODEL), F32),
                   jax.ShapeDtypeStruct((T, D_MODEL), BF16),
                   jax.ShapeDtypeStruct((T, LANES), jnp.int32),
                   jax.ShapeDtypeStruct((T, LANES), F32),
                   jax.ShapeDtypeStruct((T, LANES), jnp.int32),
                   jax.ShapeDtypeStruct((1, LANES), F32)],
        scratch_shapes=[pltpu.VMEM((1, LANES), F32)],
        compiler_params=_params(("arbitrary",), VMEM_LIMIT),
        name="merge",
    )(o_a, o_b, proj, proj, x2, w_ua, w_ub, w_o, g_ffn, w_r, b_r)


def _expert_kernel(be_ref, nu_ref, x_ref, wg_ref, bg_ref, wu_ref, bu_ref, wd_ref, bd_ref, y_ref,
                   wg_s, wu_s, wd_s):
    i = pl.program_id(0)
    used = i < nu_ref[0]
    e = be_ref[i]
    prev = be_ref[jnp.maximum(i - 1, 0)]

    @pl.when(used & ((i == 0) | (e != prev)))
    def _():
        wg_s[...] = wg_ref[0].astype(BF16)
        wu_s[...] = wu_ref[0].astype(BF16)
        wd_s[...] = wd_ref[0].astype(BF16)

    @pl.when(used)
    def _():
        xb = x_ref[...]
        gate = jnp.dot(xb, wg_s[...], preferred_element_type=F32) + bg_ref[0]
        up = jnp.dot(xb, wu_s[...], preferred_element_type=F32) + bu_ref[0]
        gate = jnp.minimum(gate, SWIGLU_LIMIT)
        up = jnp.clip(up, -SWIGLU_LIMIT, SWIGLU_LIMIT)
        glu = gate * _sigmoid(gate * SWIGLU_ALPHA)
        act = ((up + 1.0) * glu).astype(BF16)
        y_ref[...] = (jnp.dot(act, wd_s[...], preferred_element_type=F32) + bd_ref[0]).astype(BF16)


def _experts(blk_exp, n_used, x_rows, w_g, b_g, w_u, b_u, w_d, b_d):
    n_pad = x_rows.shape[0]
    M = EXPERT_BLOCK
    wspec = lambda a, b: pl.BlockSpec((1, a, b), lambda i, be, nu: (be[i], 0, 0))
    return pl.pallas_call(
        _expert_kernel,
        grid_spec=pltpu.PrefetchScalarGridSpec(
            num_scalar_prefetch=2,
            grid=(n_pad // M,),
            in_specs=[pl.BlockSpec((M, D_MODEL), lambda i, be, nu: (i, 0)),
                      wspec(D_MODEL, D_FF), wspec(1, D_FF),
                      wspec(D_MODEL, D_FF), wspec(1, D_FF),
                      wspec(D_FF, D_MODEL), wspec(1, D_MODEL)],
            out_specs=pl.BlockSpec((M, D_MODEL), lambda i, be, nu: (i, 0)),
            scratch_shapes=[pltpu.VMEM((D_MODEL, D_FF), BF16),
                            pltpu.VMEM((D_MODEL, D_FF), BF16),
                            pltpu.VMEM((D_FF, D_MODEL), BF16)]),
        out_shape=jax.ShapeDtypeStruct((n_pad, D_MODEL), BF16),
        compiler_params=_params(("arbitrary",), VMEM_LIMIT),
        name="experts",
    )(blk_exp, n_used, x_rows, w_g, b_g, w_u, b_u, w_d, b_d)


def _final_kernel(x1_ref, yk_ref, wts_ref, p_ref, gpg_ref, wpg_ref, wpp_ref, gpp_ref, gfin_ref, o_ref):
    x = x1_ref[...]
    w = wts_ref[...]
    for k in range(TOP_K):
        x = x + w[:, k:k + 1] * yk_ref[:, k * D_MODEL:(k + 1) * D_MODEL].astype(F32)
    pg = _sigmoid(jnp.dot(_rms(x, gpg_ref[...]).astype(BF16), wpg_ref[...], preferred_element_type=F32))
    pp = jnp.dot(p_ref[...].astype(BF16), wpp_ref[...], preferred_element_type=F32)
    x = x + pg * _rms(pp, gpp_ref[...])
    o_ref[...] = _rms(x, gfin_ref[...])


def _final(x1, yk, wts, p2, g_pg, w_pg, w_pp, g_pp, g_fin, tm=512):
    T = x1.shape[0]
    full = lambda shape: pl.BlockSpec(shape, lambda i: (0, 0))
    rowblk = lambda w: pl.BlockSpec((tm, w), lambda i: (i, 0))
    return pl.pallas_call(
        _final_kernel,
        grid=(T // tm,),
        in_specs=[rowblk(D_MODEL), rowblk(TOP_K * D_MODEL), rowblk(LANES), rowblk(PLE_DIM),
                  full((1, D_MODEL)), full((D_MODEL, D_MODEL)), full((PLE_DIM, D_MODEL)),
                  full((1, D_MODEL)), full((1, D_MODEL))],
        out_specs=rowblk(D_MODEL),
        out_shape=jax.ShapeDtypeStruct((T, D_MODEL), F32),
        compiler_params=_params(("arbitrary",), VMEM_LIMIT),
        name="final",
    )(x1, yk, wts, p2, g_pg, w_pg, w_pp, g_pp, g_fin)


def _split_w_in(w_in):
    sizes = (GLA_QK, GLA_QK, GLA_V, GLA_V, GLA_RANK, MOBA_W, MOBA_W, MOBA_W, D_MODEL, D_MODEL)
    offs = [0]
    for s in sizes:
        offs.append(offs[-1] + s)
    parts = [w_in[:, offs[i]:offs[i + 1]] for i in range(len(sizes))]
    main = jnp.concatenate(parts[:4] + parts[5:], axis=1).astype(BF16)
    alow = jnp.pad(parts[4], ((0, 0), (0, LANES - GLA_RANK))).astype(BF16)
    return main, alow


def _layer(x2, p2, bias, B, S, g_mix, w_in, w_a2, b_a2, g_gla_out, w_up_gla, w_up_moba, w_o, g_ffn, w_router,
           b_router, w_e_gate, b_e_gate, w_e_up, b_e_up, w_e_down, b_e_down, g_ple_gate, w_ple_gate, w_ple_proj,
           g_ple_proj, g_final):
    T = B * S
    row = lambda v: v.reshape(1, -1).astype(F32)
    w_main, w_alow = _split_w_in(w_in)
    colscale = jnp.ones((D_PROJ,), F32)
    colscale = colscale.at[OFF_QA:OFF_QA + GLA_QK].set(GLA_DK ** -0.5)
    colscale = colscale.at[OFF_QB:OFF_QB + MOBA_W].set(MOBA_HD ** -0.5)
    w_a2p = jnp.pad(w_a2, ((0, LANES - GLA_RANK), (0, 0))).astype(BF16)
    proj, glog = _inproj(x2, row(g_mix), w_main, colscale.reshape(1, -1), w_alow, w_a2p, row(b_a2))

    o_a = _gla(proj, glog, row(g_gla_out), B, S)
    o_b = _moba(proj, bias, B, S)

    w_r = jnp.pad(w_router.astype(F32), ((0, 0), (0, LANES - N_EXPERTS)))
    b_r = jnp.pad(b_router.astype(F32), (0, LANES - N_EXPERTS)).reshape(1, -1)
    x1, h2, eidx, wts, rank, cnt = _merge(o_a, o_b, proj, x2, w_up_gla.astype(BF16), w_up_moba.astype(BF16),
                                          w_o.astype(BF16), row(g_ffn), w_r, b_r)

    M = EXPERT_BLOCK
    A = T * TOP_K
    n_pad = (-(-A // M)) * M + N_EXPERTS * M
    n_blk = n_pad // M
    counts = cnt[0, :N_EXPERTS].astype(jnp.int32)
    padded = (counts + M - 1) // M * M
    pad_end = jnp.cumsum(padded)
    pad_start = pad_end - padded
    blk_exp = jnp.minimum(jnp.searchsorted(pad_end, jnp.arange(n_blk, dtype=jnp.int32) * M, side='right'),
                          N_EXPERTS - 1).astype(jnp.int32)
    n_used = (pad_end[-1:] // M).astype(jnp.int32)
    dest = pad_start[eidx[:, :TOP_K]] + rank[:, :TOP_K]
    tok = jnp.broadcast_to(jnp.arange(T, dtype=jnp.int32)[:, None], (T, TOP_K))
    row_tok = jnp.zeros((n_pad,), jnp.int32).at[dest.reshape(-1)].set(tok.reshape(-1))
    x_rows = h2[row_tok]
    y_rows = _experts(blk_exp, n_used, x_rows, w_e_gate, b_e_gate.reshape(N_EXPERTS, 1, D_FF),
                      w_e_up, b_e_up.reshape(N_EXPERTS, 1, D_FF), w_e_down,
                      b_e_down.reshape(N_EXPERTS, 1, D_MODEL))
    yk = y_rows[dest.reshape(-1)].reshape(T, TOP_K * D_MODEL)
    return _final(x1, yk, wts, p2, row(g_ple_gate), w_ple_gate.astype(BF16), w_ple_proj.astype(BF16),
                  row(g_ple_proj), row(g_final))


def kernel(x, p, rel_bias, g_mix, w_in, w_a2, b_a2, g_gla_out, w_up_gla, w_up_moba, w_o, g_ffn, w_router, b_router,
           w_e_gate, b_e_gate, w_e_up, b_e_up, w_e_down, b_e_down, g_ple_gate, w_ple_gate, w_ple_proj, g_ple_proj,
           g_final):
    B, S, D = x.shape
    assert D == D_MODEL and S % MOBA_BLOCK == 0 and S % GLA_CHUNK == 0 and p.shape[0] == 1
    bias = _bias_tiles(rel_bias, S // MOBA_BLOCK)
    out = _layer(x.reshape(B * S, D), p[0].reshape(B * S, PLE_DIM), bias, B, S,
                 g_mix[0], w_in[0], w_a2[0], b_a2[0], g_gla_out[0], w_up_gla[0], w_up_moba[0], w_o[0], g_ffn[0],
                 w_router[0], b_router[0], w_e_gate[0], b_e_gate[0], w_e_up[0], b_e_up[0], w_e_down[0],
                 b_e_down[0], g_ple_gate[0], w_ple_gate[0], w_ple_proj[0], g_ple_proj[0], g_final)
    return out.reshape(B, S, D)
```

```python
import functools
import math

import jax
import jax.numpy as jnp
from jax import lax
from jax.experimental import pallas as pl
from jax.experimental.pallas import tpu as pltpu

F32 = jnp.float32
BF16 = jnp.bfloat16

D_MODEL = 1024
PLE_DIM = 256
GLA_HEADS = 4
GLA_DK = 128
GLA_DV = 256
GLA_RANK = 16
GLA_TAU = 16.0
GLA_QK = GLA_HEADS * GLA_DK
GLA_V = GLA_HEADS * GLA_DV
MOBA_HEADS = 8
MOBA_HD = 128
MOBA_BLOCK = 256
MOBA_TOPK = 3
MOBA_W = MOBA_HEADS * MOBA_HD
REL_BUCKETS = 32
REL_MAX_DIST = 4096
N_EXPERTS = 32
TOP_K = 4
D_FF = 1024
SWIGLU_LIMIT = 7.0
SWIGLU_ALPHA = 1.702
EPS = 1e-6

LANES = 128
NEG = -1e30
VMEM_LIMIT = 56 * 1024 * 1024

OFF_QA, OFF_KA, OFF_VA, OFF_RA = 0, 512, 1024, 2048
OFF_QB, OFF_KB, OFF_VB, OFF_GA, OFF_GB = 3072, 4096, 5120, 6144, 7168
D_PROJ = 8192

GLA_CHUNK = 128
EXPERT_BLOCK = 512
TOKEN_TILE = 256
RUN_ALIGN = 8
TILE_ROWS = TOP_K * TOKEN_TILE + N_EXPERTS * RUN_ALIGN

NT = (((1,), (1,)), ((), ()))
TN = (((0,), (0,)), ((), ()))


def _params(sem, vmem=None):
    return pltpu.CompilerParams(dimension_semantics=sem, vmem_limit_bytes=vmem)


def _rms(x, g):
    return x * lax.rsqrt(jnp.mean(x * x, axis=-1, keepdims=True) + EPS) * g


def _sigmoid(x):
    return 1.0 / (1.0 + jnp.exp(-x))


def _bias_kernel(tab_ref, bkt_ref, o_ref):
    strip = 8

    def body(s, carry):
        r0 = pl.multiple_of(s * strip, strip)
        b = bkt_ref[0, pl.ds(r0, strip), :]
        accs = [jnp.zeros(b.shape, F32) for _ in range(MOBA_HEADS)]
        for bb in range(REL_BUCKETS):
            hit = b == bb
            for h in range(MOBA_HEADS):
                accs[h] = jnp.where(hit, tab_ref[h, bb], accs[h])
        for h in range(MOBA_HEADS):
            o_ref[h, 0, pl.ds(r0, strip), :] = jnp.where(b < 0, NEG, accs[h])
        return carry

    lax.fori_loop(0, MOBA_BLOCK // strip, body, 0)


def _bias_tiles(rel_bias, nblk):
    i = jnp.arange(MOBA_BLOCK, dtype=jnp.int32)
    dist = (jnp.arange(nblk, dtype=jnp.int32)[:, None, None] * MOBA_BLOCK + i[None, :, None] - i[None, None, :])
    n = jnp.maximum(dist, 0)
    max_exact = REL_BUCKETS // 2
    nf = jnp.maximum(n, 1).astype(F32)
    large = max_exact + (jnp.log(nf / max_exact) / math.log(REL_MAX_DIST / max_exact)
                         * (REL_BUCKETS - max_exact)).astype(jnp.int32)
    large = jnp.minimum(large, REL_BUCKETS - 1)
    bkt = jnp.where(dist < 0, -1, jnp.where(n < max_exact, n, large)).astype(jnp.int32)
    tab = rel_bias.astype(F32).T
    return pl.pallas_call(
        _bias_kernel,
        grid=(nblk,),
        in_specs=[pl.BlockSpec(memory_space=pltpu.SMEM),
                  pl.BlockSpec((1, MOBA_BLOCK, MOBA_BLOCK), lambda m: (m, 0, 0))],
        out_specs=pl.BlockSpec((MOBA_HEADS, 1, MOBA_BLOCK, MOBA_BLOCK), lambda m: (0, m, 0, 0)),
        out_shape=jax.ShapeDtypeStruct((MOBA_HEADS, nblk, MOBA_BLOCK, MOBA_BLOCK), F32),
        compiler_params=_params(("arbitrary",)),
        name="bias_tiles",
    )(tab, bkt)


def _inproj_kernel(x_ref, g_ref, w_ref, cs_ref, wal_ref, wa2_ref, ba2_ref, o_ref, glog_ref, h_scr):
    @pl.when(pl.program_id(1) == 0)
    def _():
        h = _rms(x_ref[...], g_ref[...]).astype(BF16)
        h_scr[...] = h
        a_low = jnp.dot(h, wal_ref[...], preferred_element_type=F32)
        a = jnp.dot(a_low.astype(BF16), wa2_ref[...], preferred_element_type=F32) + ba2_ref[...]
        log_sig = jnp.minimum(a, 0.0) - jnp.log1p(jnp.exp(-jnp.abs(a)))
        glog_ref[...] = log_sig * (1.0 / GLA_TAU)

    acc = jnp.dot(h_scr[...], w_ref[...], preferred_element_type=F32)
    o_ref[...] = (acc * cs_ref[...]).astype(BF16)


def _inproj(x2, g_mix, w_main, colscale, w_alow, w_a2p, b_a2, tm=1024, tn=1024):
    T = x2.shape[0]
    return pl.pallas_call(
        _inproj_kernel,
        grid=(T // tm, D_PROJ // tn),
        in_specs=[pl.BlockSpec((tm, D_MODEL), lambda i, j: (i, 0)),
                  pl.BlockSpec((1, D_MODEL), lambda i, j: (0, 0)),
                  pl.BlockSpec((D_MODEL, tn), lambda i, j: (0, j)),
                  pl.BlockSpec((1, tn), lambda i, j: (0, j)),
                  pl.BlockSpec((D_MODEL, LANES), lambda i, j: (0, 0)),
                  pl.BlockSpec((LANES, GLA_QK), lambda i, j: (0, 0)),
                  pl.BlockSpec((1, GLA_QK), lambda i, j: (0, 0))],
        out_specs=[pl.BlockSpec((tm, tn), lambda i, j: (i, j)),
                   pl.BlockSpec((tm, GLA_QK), lambda i, j: (i, 0))],
        out_shape=[jax.ShapeDtypeStruct((T, D_PROJ), BF16),
                   jax.ShapeDtypeStruct((T, GLA_QK), F32)],
        scratch_shapes=[pltpu.VMEM((tm, D_MODEL), BF16)],
        compiler_params=_params(("arbitrary", "arbitrary"), VMEM_LIMIT),
        name="inproj",
    )(x2, g_mix, w_main, colscale, w_alow, w_a2p, b_a2)


def _gla_kernel(q_ref, k_ref, v_ref, r_ref, g_ref, gout_ref, o_ref, st_ref):
    C = GLA_CHUNK

    @pl.when(pl.program_id(1) == 0)
    def _():
        st_ref[...] = jnp.zeros(st_ref.shape, F32)

    row = lax.broadcasted_iota(jnp.int32, (C, C), 0)
    col = lax.broadcasted_iota(jnp.int32, (C, C), 1)
    causal = col <= row
    ltri = causal.astype(BF16)
    g = g_ref[...]
    g_hi = g.astype(BF16)
    g_lo = (g - g_hi.astype(F32)).astype(BF16)
    G = jnp.dot(ltri, g_hi, preferred_element_type=F32) + jnp.dot(ltri, g_lo, preferred_element_type=F32)
    mid = C // 2
    for h in range(GLA_HEADS):
        ks = slice(h * GLA_DK, (h + 1) * GLA_DK)
        vs = slice(h * GLA_DV, (h + 1) * GLA_DV)
        Gh = G[:, ks]
        qh = q_ref[:, ks].astype(F32)
        kh = k_ref[:, ks].astype(F32)
        vh = v_ref[:, vs]
        g_mid = Gh[mid:mid + 1, :]
        g_last = Gh[C - 1:C, :]
        q_in = (qh * jnp.exp(Gh)).astype(BF16)
        q_a = (qh * jnp.exp(Gh - g_mid)).astype(BF16)
        k_a = (kh * jnp.exp(g_mid - Gh)).astype(BF16)
        k_d = (kh * jnp.exp(g_last - Gh)).astype(BF16)
        A = lax.dot_general(q_a, k_a, NT, preferred_element_type=F32)
        A = jnp.where(causal, A, 0.0).astype(BF16)
        intra = jnp.dot(A, vh, preferred_element_type=F32)
        st = st_ref[h]
        inter = lax.dot_general(q_in, st.astype(BF16), NT, preferred_element_type=F32)
        o = inter + intra
        st_ref[h] = jnp.exp(g_last) * st + lax.dot_general(vh, k_d, TN, preferred_element_type=F32)
        r = r_ref[:, vs].astype(F32)
        o_ref[:, vs] = (_rms(o, gout_ref[...]) * (r * _sigmoid(r))).astype(BF16)


def _gla(proj, glog, g_gla_out, B, S):
    C = GLA_CHUNK
    nc = S // C
    return pl.pallas_call(
        _gla_kernel,
        grid=(B, nc),
        in_specs=[pl.BlockSpec((C, GLA_QK), lambda b, c: (b * nc + c, OFF_QA // GLA_QK)),
                  pl.BlockSpec((C, GLA_QK), lambda b, c: (b * nc + c, OFF_KA // GLA_QK)),
                  pl.BlockSpec((C, GLA_V), lambda b, c: (b * nc + c, OFF_VA // GLA_V)),
                  pl.BlockSpec((C, GLA_V), lambda b, c: (b * nc + c, OFF_RA // GLA_V)),
                  pl.BlockSpec((C, GLA_QK), lambda b, c: (b * nc + c, 0)),
                  pl.BlockSpec((1, GLA_DV), lambda b, c: (0, 0))],
        out_specs=pl.BlockSpec((C, GLA_V), lambda b, c: (b * nc + c, 0)),
        out_shape=jax.ShapeDtypeStruct((B * S, GLA_V), BF16),
        scratch_shapes=[pltpu.VMEM((GLA_HEADS, GLA_DV, GLA_DK), F32)],
        compiler_params=_params(("arbitrary", "arbitrary")),
        name="gla",
    )(proj, proj, proj, proj, glog, g_gla_out)


def _moba_kernel(q_ref, k_ref, v_ref, bias_ref, o_ref, km_scr, *, nblk):
    BLK = MOBA_BLOCK
    c = pl.program_id(2)

    @pl.when(c == 0)
    def _():
        for j in range(nblk):
            kb = k_ref[j * BLK:(j + 1) * BLK, :].astype(F32)
            km_scr[j:j + 1, :] = jnp.mean(kb, axis=0, keepdims=True)

    q = q_ref[...]
    for cc in range(nblk):
        @pl.when(c == cc)
        def _(cc=cc):
            pen_t = None
            if cc > MOBA_TOPK:
                s = lax.dot_general(km_scr[...], q.astype(F32), NT, preferred_element_type=F32,
                                    precision=lax.Precision.HIGHEST)
                ji = lax.broadcasted_iota(jnp.int32, s.shape, 0)
                cnt = jnp.zeros(s.shape, F32)
                for jp in range(cc):
                    sj = s[jp:jp + 1, :]
                    beats = (sj > s) | ((sj == s) & (jp < ji))
                    cnt = cnt + beats.astype(F32)
                pen = jnp.where((ji < cc) & (cnt < MOBA_TOPK), 0.0, NEG)
                pen_t = pen.T
            logits = []
            for j in range(cc + 1):
                kj = k_ref[j * BLK:(j + 1) * BLK, :]
                lg = lax.dot_general(q, kj, NT, preferred_element_type=F32) + bias_ref[0, cc - j]
                if pen_t is not None and j < cc:
                    lg = lg + pen_t[:, j:j + 1]
                logits.append(lg)
            m = logits[0].max(axis=-1, keepdims=True)
            for lg in logits[1:]:
                m = jnp.maximum(m, lg.max(axis=-1, keepdims=True))
            den = jnp.zeros(m.shape, F32)
            acc = jnp.zeros((BLK, MOBA_HD), F32)
            for j, lg in enumerate(logits):
                p = jnp.exp(lg - m)
                den = den + p.sum(axis=-1, keepdims=True)
                acc = acc + jnp.dot(p.astype(BF16), v_ref[j * BLK:(j + 1) * BLK, :], preferred_element_type=F32)
            o_ref[...] = (acc / den).astype(BF16)


def _moba(proj, bias, B, S):
    BLK = MOBA_BLOCK
    nblk = S // BLK
    H = MOBA_HEADS
    return pl.pallas_call(
        functools.partial(_moba_kernel, nblk=nblk),
        grid=(H, B, nblk),
        in_specs=[pl.BlockSpec((BLK, MOBA_HD), lambda h, b, c: (b * nblk + c, OFF_QB // MOBA_HD + h)),
                  pl.BlockSpec((S, MOBA_HD), lambda h, b, c: (b, OFF_KB // MOBA_HD + h)),
                  pl.BlockSpec((S, MOBA_HD), lambda h, b, c: (b, OFF_VB // MOBA_HD + h)),
                  pl.BlockSpec((1, nblk, BLK, BLK), lambda h, b, c: (h, 0, 0, 0))],
        out_specs=pl.BlockSpec((BLK, MOBA_HD), lambda h, b, c: (b * nblk + c, h)),
        out_shape=jax.ShapeDtypeStruct((B * S, MOBA_W), BF16),
        scratch_shapes=[pltpu.VMEM((nblk, MOBA_HD), F32)],
        compiler_params=_params(("arbitrary", "arbitrary", "arbitrary"), VMEM_LIMIT),
        name="moba",
    )(proj, proj, proj, bias)


def _pack(lo, hi):
    lo_b = lax.bitcast_convert_type(lo.astype(BF16).astype(F32), jnp.uint32)
    hi_b = lax.bitcast_convert_type(hi.astype(BF16).astype(F32), jnp.uint32)
    return (lo_b >> 16) | (hi_b & jnp.uint32(0xFFFF0000))


def _unpack(w):
    lo = lax.bitcast_convert_type(w << 16, F32)
    hi = lax.bitcast_convert_type(w & jnp.uint32(0xFFFF0000), F32)
    return lo.astype(BF16), hi.astype(BF16)


def _merge_kernel(oa_ref, ob_ref, ga_ref, gb_ref, x_ref, wua_ref, wub_ref, wo_ref, gffn_ref, wr_ref, br_ref,
                  x1_ref, xs_ref, posw_ref, cnt_ref, carry_ref, cnt_scr):
    tm = x_ref.shape[0]
    half = D_MODEL // 2

    @pl.when(pl.program_id(0) == 0)
    def _():
        cnt_scr[...] = jnp.zeros(cnt_scr.shape, F32)

    u_a = jnp.dot(oa_ref[...], wua_ref[...], preferred_element_type=F32)
    u_b = jnp.dot(ob_ref[...], wub_ref[...], preferred_element_type=F32)
    y = _sigmoid(ga_ref[...].astype(F32)) * u_a + _sigmoid(gb_ref[...].astype(F32)) * u_b
    x1 = x_ref[...] + jnp.dot(y.astype(BF16), wo_ref[...], preferred_element_type=F32)
    x1_ref[...] = x1
    h2 = _rms(x1, gffn_ref[...])
    logits = jnp.dot(h2, wr_ref[...], preferred_element_type=F32, precision=lax.Precision.HIGHEST) + br_ref[...]
    lane = lax.broadcasted_iota(jnp.int32, (tm, LANES), 1)
    lane_f = lane.astype(F32)
    work = jnp.where(lane < N_EXPERTS, logits, NEG)
    vals, hots = [], []
    for _ in range(TOP_K):
        mx = work.max(axis=-1, keepdims=True)
        idx = jnp.min(jnp.where(work == mx, lane_f, float(LANES)), axis=-1, keepdims=True)
        hot = lane_f == idx
        vals.append(mx)
        hots.append(hot)
        work = jnp.where(hot, 2.0 * NEG, work)
    exps = [jnp.exp(v - vals[0]) for v in vals]
    den = exps[0] + exps[1] + exps[2] + exps[3]
    sel = jnp.zeros((tm, LANES), F32)
    for hot in hots:
        sel = sel + hot.astype(F32)
    row = lax.broadcasted_iota(jnp.int32, (tm, tm), 0)
    col = lax.broadcasted_iota(jnp.int32, (tm, tm), 1)
    below = (col < row).astype(BF16)
    local_rank = jnp.dot(below, sel.astype(BF16), preferred_element_type=F32)
    cnt_t = sel.sum(axis=0, keepdims=True)
    cnt_t = jnp.floor((cnt_t + (RUN_ALIGN - 1.0)) * (1.0 / RUN_ALIGN)) * RUN_ALIGN
    er = lax.broadcasted_iota(jnp.int32, (LANES, LANES), 0)
    ec = lax.broadcasted_iota(jnp.int32, (LANES, LANES), 1)
    before = (er < ec).astype(F32)
    tile_off = jnp.dot(jnp.broadcast_to(cnt_t, (8, LANES)), before, preferred_element_type=F32,
                       precision=lax.Precision.HIGHEST)[0:1]
    pos_all = local_rank + tile_off
    posw = jnp.zeros((tm, LANES), F32)
    for k in range(TOP_K):
        pk = jnp.sum(jnp.where(hots[k], pos_all, 0.0), axis=-1, keepdims=True)
        posw = jnp.where(lane == k, pk, posw)
        posw = jnp.where(lane == TOP_K + k, exps[k] / den, posw)
    posw_ref[...] = posw
    carry_ref[0] = cnt_scr[...]
    cnt_ref[0] = cnt_t
    cnt_scr[...] = cnt_scr[...] + cnt_t
    pos_t = posw.T
    sub = lax.broadcasted_iota(jnp.int32, (TILE_ROWS, tm), 0).astype(F32)
    perm = jnp.zeros((TILE_ROWS, tm), F32)
    for k in range(TOP_K):
        perm = perm + (sub == pos_t[k:k + 1, :]).astype(F32)
    xs = jnp.dot(perm.astype(BF16), h2.astype(BF16), preferred_element_type=F32)
    xs_ref[...] = _pack(xs[:, :half], xs[:, half:])


def _merge(o_a, o_b, proj, x2, w_ua, w_ub, w_o, g_ffn, w_r, b_r, tm):
    T = x2.shape[0]
    nt = T // tm
    full = lambda shape: pl.BlockSpec(shape, lambda i: (0, 0))
    rowblk = lambda w: pl.BlockSpec((tm, w), lambda i: (i, 0))
    tilerow = pl.BlockSpec((1, 1, LANES), lambda i: (i, 0, 0))
    return pl.pallas_call(
        _merge_kernel,
        grid=(nt,),
        in_specs=[rowblk(GLA_V), rowblk(MOBA_W),
                  pl.BlockSpec((tm, D_MODEL), lambda i: (i, OFF_GA // D_MODEL)),
                  pl.BlockSpec((tm, D_MODEL), lambda i: (i, OFF_GB // D_MODEL)),
                  rowblk(D_MODEL),
                  full((GLA_V, D_MODEL)), full((MOBA_W, D_MODEL)), full((D_MODEL, D_MODEL)),
                  full((1, D_MODEL)), full((D_MODEL, LANES)), full((1, LANES))],
        out_specs=[rowblk(D_MODEL),
                   pl.BlockSpec((TILE_ROWS, D_MODEL // 2), lambda i: (i, 0)),
                   rowblk(LANES), tilerow, tilerow],
        out_shape=[jax.ShapeDtypeStruct((T, D_MODEL), F32),
                   jax.ShapeDtypeStruct((nt * TILE_ROWS, D_MODEL // 2), jnp.uint32),
                   jax.ShapeDtypeStruct((T, LANES), F32),
                   jax.ShapeDtypeStruct((nt, 1, LANES), F32),
                   jax.ShapeDtypeStruct((nt, 1, LANES), F32)],
        scratch_shapes=[pltpu.VMEM((1, LANES), F32)],
        compiler_params=_params(("arbitrary",), VMEM_LIMIT),
        name="merge",
    )(o_a, o_b, proj, proj, x2, w_ua, w_ub, w_o, g_ffn, w_r, b_r)


def _move_kernel(soff_ref, doff_ref, n_ref, zoff_ref, zn_ref, tail_ref, src_ref, zero_ref, dst_ref, sems,
                 *, n_groups, group, bits, n_zero, zbits):
    def pieces(src, s0, d0, n, nbits, sem, act):
        for b in reversed(range(RUN_ALIGN.bit_length() - 1, nbits)):
            size = 1 << b
            done = n & ~((2 << b) - 1)

            @pl.when((n & size) != 0)
            def _():
                s = pl.multiple_of(s0 + done, RUN_ALIGN)
                d = pl.multiple_of(d0 + done, RUN_ALIGN)
                act(pltpu.make_async_copy(src.at[pl.ds(s, size)], dst_ref.at[pl.ds(d, size)], sem))

    def group_runs(g, act):
        sem = sems.at[g % 2]

        def body(i, c):
            r = g * group + i
            pieces(src_ref, soff_ref[r], doff_ref[r], n_ref[r], bits, sem, act)
            return c

        lax.fori_loop(0, group, body, 0)

    def zero_runs(act):
        def body(r, c):
            pieces(zero_ref, 0, zoff_ref[r], zn_ref[r], zbits, sems.at[2], act)
            return c

        lax.fori_loop(0, n_zero, body, 0)

        zrows = zero_ref.shape[0]

        def tail(i, c):
            d = pl.multiple_of(i * zrows, zrows)
            act(pltpu.make_async_copy(zero_ref, dst_ref.at[pl.ds(d, zrows)], sems.at[2]))
            return c

        lax.fori_loop(tail_ref[0], dst_ref.shape[0] // zrows, tail, 0)

    start = lambda cp: cp.start()
    wait = lambda cp: cp.wait()
    g = pl.program_id(0)

    @pl.when(g == 0)
    def _():
        zero_runs(start)

    group_runs(g, start)

    @pl.when(g > 0)
    def _():
        group_runs(g - 1, wait)

    @pl.when(g == n_groups - 1)
    def _():
        group_runs(g, wait)
        zero_runs(wait)


def _move_rows(src, zeros, n_dst, soff, doff, n, zoff, zn, tail, group, max_n, max_zn):
    n_runs = soff.shape[0]
    assert n_dst % zeros.shape[0] == 0 and n_runs % group == 0
    any_spec = pl.BlockSpec(memory_space=pl.ANY)
    return pl.pallas_call(
        functools.partial(_move_kernel, n_groups=n_runs // group, group=group, bits=max_n.bit_length(),
                          n_zero=zoff.shape[0], zbits=max(max_zn, 1).bit_length()),
        grid_spec=pltpu.PrefetchScalarGridSpec(
            num_scalar_prefetch=6,
            grid=(n_runs // group,),
            in_specs=[any_spec, any_spec],
            out_specs=any_spec,
            scratch_shapes=[pltpu.SemaphoreType.DMA((3,))]),
        out_shape=jax.ShapeDtypeStruct((n_dst, src.shape[1]), src.dtype),
        compiler_params=_params(("arbitrary",)),
        name="move_rows",
    )(soff, doff, n, zoff, zn, tail, src, zeros)


def _expert_kernel(be_ref, nu_ref, x_ref, wg_ref, bg_ref, wu_ref, bu_ref, wd_ref, bd_ref, y_ref,
                   wg_s, wu_s, wd_s):
    i = pl.program_id(0)
    half = D_MODEL // 2
    used = i < nu_ref[0]
    e = be_ref[i]
    prev = be_ref[jnp.maximum(i - 1, 0)]

    @pl.when(used & ((i == 0) | (e != prev)))
    def _():
        wg_s[...] = wg_ref[0].astype(BF16)
        wu_s[...] = wu_ref[0].astype(BF16)
        wd_s[...] = wd_ref[0].astype(BF16)

    @pl.when(used)
    def _():
        x_lo, x_hi = _unpack(x_ref[...])

        def proj_in(w_s, b_ref):
            return (jnp.dot(x_lo, w_s[:half, :], preferred_element_type=F32)
                    + jnp.dot(x_hi, w_s[half:, :], preferred_element_type=F32) + b_ref[0])

        gate = jnp.minimum(proj_in(wg_s, bg_ref), SWIGLU_LIMIT)
        up = jnp.clip(proj_in(wu_s, bu_ref), -SWIGLU_LIMIT, SWIGLU_LIMIT)
        glu = gate * _sigmoid(gate * SWIGLU_ALPHA)
        act = ((up + 1.0) * glu).astype(BF16)
        y = jnp.dot(act, wd_s[...], preferred_element_type=F32) + bd_ref[0]
        y_ref[...] = _pack(y[:, :half], y[:, half:])

    @pl.when(jnp.logical_not(used))
    def _():
        y_ref[...] = jnp.zeros(y_ref.shape, y_ref.dtype)


def _experts(blk_exp, n_used, x_rows, n_pad, w_g, b_g, w_u, b_u, w_d, b_d):
    M = EXPERT_BLOCK
    wspec = lambda a, b: pl.BlockSpec((1, a, b), lambda i, be, nu: (be[i], 0, 0))
    return pl.pallas_call(
        _expert_kernel,
        grid_spec=pltpu.PrefetchScalarGridSpec(
            num_scalar_prefetch=2,
            grid=(n_pad // M,),
            in_specs=[pl.BlockSpec((M, D_MODEL // 2), lambda i, be, nu: (i, 0)),
                      wspec(D_MODEL, D_FF), wspec(1, D_FF),
                      wspec(D_MODEL, D_FF), wspec(1, D_FF),
                      wspec(D_FF, D_MODEL), wspec(1, D_MODEL)],
            out_specs=pl.BlockSpec((M, D_MODEL // 2), lambda i, be, nu: (i, 0)),
            scratch_shapes=[pltpu.VMEM((D_MODEL, D_FF), BF16),
                            pltpu.VMEM((D_MODEL, D_FF), BF16),
                            pltpu.VMEM((D_FF, D_MODEL), BF16)]),
        out_shape=jax.ShapeDtypeStruct((n_pad, D_MODEL // 2), jnp.uint32),
        compiler_params=_params(("arbitrary",), VMEM_LIMIT),
        name="experts",
    )(blk_exp, n_used, x_rows, w_g, b_g, w_u, b_u, w_d, b_d)


def _final_kernel(x1_ref, ys_ref, posw_ref, p_ref, gpg_ref, wpg_ref, wpp_ref, gpp_ref, gfin_ref, o_ref):
    tm = x1_ref.shape[0]
    posw = posw_ref[...]
    lane = lax.broadcasted_iota(jnp.int32, (tm, TILE_ROWS), 1).astype(F32)
    comb = jnp.zeros((tm, TILE_ROWS), F32)
    for k in range(TOP_K):
        comb = comb + jnp.where(lane == posw[:, k:k + 1], posw[:, TOP_K + k:TOP_K + k + 1], 0.0)
    comb = comb.astype(BF16)
    y_lo, y_hi = _unpack(ys_ref[...])
    moe = jnp.concatenate([jnp.dot(comb, y_lo, preferred_element_type=F32),
                           jnp.dot(comb, y_hi, preferred_element_type=F32)], axis=-1)
    x = x1_ref[...] + moe
    pg = _sigmoid(jnp.dot(_rms(x, gpg_ref[...]).astype(BF16), wpg_ref[...], preferred_element_type=F32))
    pp = jnp.dot(p_ref[...].astype(BF16), wpp_ref[...], preferred_element_type=F32)
    x = x + pg * _rms(pp, gpp_ref[...])
    o_ref[...] = _rms(x, gfin_ref[...])


def _final(x1, ys, posw, p2, g_pg, w_pg, w_pp, g_pp, g_fin, tm):
    T = x1.shape[0]
    full = lambda shape: pl.BlockSpec(shape, lambda i: (0, 0))
    rowblk = lambda w: pl.BlockSpec((tm, w), lambda i: (i, 0))
    return pl.pallas_call(
        _final_kernel,
        grid=(T // tm,),
        in_specs=[rowblk(D_MODEL), pl.BlockSpec((TILE_ROWS, D_MODEL // 2), lambda i: (i, 0)),
                  rowblk(LANES), rowblk(PLE_DIM),
                  full((1, D_MODEL)), full((D_MODEL, D_MODEL)), full((PLE_DIM, D_MODEL)),
                  full((1, D_MODEL)), full((1, D_MODEL))],
        out_specs=rowblk(D_MODEL),
        out_shape=jax.ShapeDtypeStruct((T, D_MODEL), F32),
        compiler_params=_params(("arbitrary",), VMEM_LIMIT),
        name="final",
    )(x1, ys, posw, p2, g_pg, w_pg, w_pp, g_pp, g_fin)


def _split_w_in(w_in):
    sizes = (GLA_QK, GLA_QK, GLA_V, GLA_V, GLA_RANK, MOBA_W, MOBA_W, MOBA_W, D_MODEL, D_MODEL)
    offs = [0]
    for s in sizes:
        offs.append(offs[-1] + s)
    parts = [w_in[:, offs[i]:offs[i + 1]] for i in range(len(sizes))]
    main = jnp.concatenate(parts[:4] + parts[5:], axis=1).astype(BF16)
    alow = jnp.pad(parts[4], ((0, 0), (0, LANES - GLA_RANK))).astype(BF16)
    return main, alow


def _layer(x2, p2, bias, B, S, g_mix, w_in, w_a2, b_a2, g_gla_out, w_up_gla, w_up_moba, w_o, g_ffn, w_router,
           b_router, w_e_gate, b_e_gate, w_e_up, b_e_up, w_e_down, b_e_down, g_ple_gate, w_ple_gate, w_ple_proj,
           g_ple_proj, g_final):
    T = B * S
    row = lambda v: v.reshape(1, -1).astype(F32)
    w_main, w_alow = _split_w_in(w_in)
    colscale = jnp.ones((D_PROJ,), F32)
    colscale = colscale.at[OFF_QA:OFF_QA + GLA_QK].set(GLA_DK ** -0.5)
    colscale = colscale.at[OFF_QB:OFF_QB + MOBA_W].set(MOBA_HD ** -0.5)
    w_a2p = jnp.pad(w_a2, ((0, LANES - GLA_RANK), (0, 0))).astype(BF16)
    proj, glog = _inproj(x2, row(g_mix), w_main, colscale.reshape(1, -1), w_alow, w_a2p, row(b_a2))

    o_a = _gla(proj, glog, row(g_gla_out), B, S)
    o_b = _moba(proj, bias, B, S)

    w_r = jnp.pad(w_router.astype(F32), ((0, 0), (0, LANES - N_EXPERTS)))
    b_r = jnp.pad(b_router.astype(F32), (0, LANES - N_EXPERTS)).reshape(1, -1)
    tm = TOKEN_TILE
    nt = T // tm
    x1, xs, posw, cnt_t, carry = _merge(o_a, o_b, proj, x2, w_up_gla.astype(BF16), w_up_moba.astype(BF16),
                                        w_o.astype(BF16), row(g_ffn), w_r, b_r, tm)

    M = EXPERT_BLOCK
    A = nt * TILE_ROWS
    n_pad = (-(-A // M)) * M + N_EXPERTS * M
    n_blk = n_pad // M
    cnt_t = cnt_t[:, 0, :N_EXPERTS].astype(jnp.int32)
    carry = carry[:, 0, :N_EXPERTS].astype(jnp.int32)
    counts = carry[-1] + cnt_t[-1]
    padded = (counts + M - 1) // M * M
    pad_end = jnp.cumsum(padded)
    pad_start = pad_end - padded
    blk_exp = jnp.minimum(jnp.sum(pad_end[None, :] <= (jnp.arange(n_blk, dtype=jnp.int32) * M)[:, None], axis=1),
                          N_EXPERTS - 1).astype(jnp.int32)
    n_used = (pad_end[-1:] // M).astype(jnp.int32)
    tile_off = jnp.cumsum(cnt_t, axis=1) - cnt_t
    sorted_off = (tile_off + (jnp.arange(nt, dtype=jnp.int32) * TILE_ROWS)[:, None]).reshape(-1)
    expert_off = (carry + pad_start[None, :]).reshape(-1)
    run_n = cnt_t.reshape(-1)
    zeros = jnp.zeros((M, D_MODEL // 2), jnp.uint32)
    x_rows = _move_rows(xs, zeros, n_pad, sorted_off, expert_off, run_n,
                        pad_start + counts, padded - counts, n_used, N_EXPERTS, tm, M - 1)
    y_rows = _experts(blk_exp, n_used, x_rows, n_pad, w_e_gate, b_e_gate.reshape(N_EXPERTS, 1, D_FF),
                      w_e_up, b_e_up.reshape(N_EXPERTS, 1, D_FF), w_e_down,
                      b_e_down.reshape(N_EXPERTS, 1, D_MODEL))
    tile_rows = jnp.sum(cnt_t, axis=1)
    ys = _move_rows(y_rows, zeros, A, expert_off, sorted_off, run_n,
                    jnp.arange(nt, dtype=jnp.int32) * TILE_ROWS + tile_rows, TILE_ROWS - tile_rows,
                    jnp.full((1,), A // M, jnp.int32), N_EXPERTS, tm, TILE_ROWS - TOP_K * tm)
    return _final(x1, ys, posw, p2, row(g_ple_gate), w_ple_gate.astype(BF16), w_ple_proj.astype(BF16),
                  row(g_ple_proj), row(g_final), tm)


def kernel(x, p, rel_bias, g_mix, w_in, w_a2, b_a2, g_gla_out, w_up_gla, w_up_moba, w_o, g_ffn, w_router, b_router,
           w_e_gate, b_e_gate, w_e_up, b_e_up, w_e_down, b_e_down, g_ple_gate, w_ple_gate, w_ple_proj, g_ple_proj,
           g_final):
    B, S, D = x.shape
    assert D == D_MODEL and S % MOBA_BLOCK == 0 and S % GLA_CHUNK == 0 and p.shape[0] == 1
    bias = _bias_tiles(rel_bias, S // MOBA_BLOCK)
    out = _layer(x.reshape(B * S, D), p[0].reshape(B * S, PLE_DIM), bias, B, S,
                 g_mix[0], w_in[0], w_a2[0], b_a2[0], g_gla_out[0], w_up_gla[0], w_up_moba[0], w_o[0], g_ffn[0],
                 w_router[0], b_router[0], w_e_gate[0], b_e_gate[0], w_e_up[0], b_e_up[0], w_e_down[0],
                 b_e_down[0], g_ple_gate[0], w_ple_gate[0], w_ple_proj[0], g_ple_proj[0], g_final)
    return out.reshape(B, S, D)
```

```python
import functools
import math

import jax
import jax.numpy as jnp
from jax import lax
from jax.experimental import pallas as pl
from jax.experimental.pallas import tpu as pltpu

F32 = jnp.float32
BF16 = jnp.bfloat16

D_MODEL = 1024
PLE_DIM = 256
GLA_HEADS = 4
GLA_DK = 128
GLA_DV = 256
GLA_RANK = 16
GLA_TAU = 16.0
GLA_QK = GLA_HEADS * GLA_DK
GLA_V = GLA_HEADS * GLA_DV
MOBA_HEADS = 8
MOBA_HD = 128
MOBA_BLOCK = 256
MOBA_TOPK = 3
MOBA_W = MOBA_HEADS * MOBA_HD
REL_BUCKETS = 32
REL_MAX_DIST = 4096
N_EXPERTS = 32
TOP_K = 4
D_FF = 1024
SWIGLU_LIMIT = 7.0
SWIGLU_ALPHA = 1.702
EPS = 1e-6

LANES = 128
NEG = -1e30
VMEM_LIMIT = 56 * 1024 * 1024

OFF_QA, OFF_KA, OFF_VA, OFF_RA = 0, 512, 1024, 2048
OFF_QB, OFF_KB, OFF_VB, OFF_GA, OFF_GB = 3072, 4096, 5120, 6144, 7168
D_PROJ = 8192

GLA_CHUNK = 128
EXPERT_BLOCK = 512
TOKEN_TILE = 256
RUN_ALIGN = 8
TILE_ROWS = TOP_K * TOKEN_TILE + N_EXPERTS * RUN_ALIGN

NT = (((1,), (1,)), ((), ()))
TN = (((0,), (0,)), ((), ()))


def _params(sem, vmem=None):
    return pltpu.CompilerParams(dimension_semantics=sem, vmem_limit_bytes=vmem)


def _rms(x, g):
    return x * lax.rsqrt(jnp.mean(x * x, axis=-1, keepdims=True) + EPS) * g


def _sigmoid(x):
    return 1.0 / (1.0 + jnp.exp(-x))


def _bias_kernel(tab_ref, bkt_ref, o_ref):
    strip = 8

    def body(s, carry):
        r0 = pl.multiple_of(s * strip, strip)
        b = bkt_ref[0, pl.ds(r0, strip), :]
        accs = [jnp.zeros(b.shape, F32) for _ in range(MOBA_HEADS)]
        for bb in range(REL_BUCKETS):
            hit = b == bb
            for h in range(MOBA_HEADS):
                accs[h] = jnp.where(hit, tab_ref[h, bb], accs[h])
        for h in range(MOBA_HEADS):
            o_ref[h, 0, pl.ds(r0, strip), :] = jnp.where(b < 0, NEG, accs[h])
        return carry

    lax.fori_loop(0, MOBA_BLOCK // strip, body, 0)


def _bias_tiles(rel_bias, nblk):
    i = jnp.arange(MOBA_BLOCK, dtype=jnp.int32)
    dist = (jnp.arange(nblk, dtype=jnp.int32)[:, None, None] * MOBA_BLOCK + i[None, :, None] - i[None, None, :])
    n = jnp.maximum(dist, 0)
    max_exact = REL_BUCKETS // 2
    nf = jnp.maximum(n, 1).astype(F32)
    large = max_exact + (jnp.log(nf / max_exact) / math.log(REL_MAX_DIST / max_exact)
                         * (REL_BUCKETS - max_exact)).astype(jnp.int32)
    large = jnp.minimum(large, REL_BUCKETS - 1)
    bkt = jnp.where(dist < 0, -1, jnp.where(n < max_exact, n, large)).astype(jnp.int32)
    tab = rel_bias.astype(F32).T
    return pl.pallas_call(
        _bias_kernel,
        grid=(nblk,),
        in_specs=[pl.BlockSpec(memory_space=pltpu.SMEM),
                  pl.BlockSpec((1, MOBA_BLOCK, MOBA_BLOCK), lambda m: (m, 0, 0))],
        out_specs=pl.BlockSpec((MOBA_HEADS, 1, MOBA_BLOCK, MOBA_BLOCK), lambda m: (0, m, 0, 0)),
        out_shape=jax.ShapeDtypeStruct((MOBA_HEADS, nblk, MOBA_BLOCK, MOBA_BLOCK), F32),
        compiler_params=_params(("arbitrary",)),
        name="bias_tiles",
    )(tab, bkt)


def _inproj_kernel(x_ref, g_ref, w_ref, cs_ref, wal_ref, wa2_ref, ba2_ref, o_ref, glog_ref, h_scr):
    @pl.when(pl.program_id(1) == 0)
    def _():
        h = _rms(x_ref[...], g_ref[...]).astype(BF16)
        h_scr[...] = h
        a_low = jnp.dot(h, wal_ref[...], preferred_element_type=F32)
        a = jnp.dot(a_low.astype(BF16), wa2_ref[...], preferred_element_type=F32) + ba2_ref[...]
        log_sig = jnp.minimum(a, 0.0) - jnp.log1p(jnp.exp(-jnp.abs(a)))
        glog_ref[...] = log_sig * (1.0 / GLA_TAU)

    acc = jnp.dot(h_scr[...], w_ref[...], preferred_element_type=F32)
    o_ref[...] = (acc * cs_ref[...]).astype(BF16)


def _inproj(x2, g_mix, w_main, colscale, w_alow, w_a2p, b_a2, tm=1024, tn=1024):
    T = x2.shape[0]
    return pl.pallas_call(
        _inproj_kernel,
        grid=(T // tm, D_PROJ // tn),
        in_specs=[pl.BlockSpec((tm, D_MODEL), lambda i, j: (i, 0)),
                  pl.BlockSpec((1, D_MODEL), lambda i, j: (0, 0)),
                  pl.BlockSpec((D_MODEL, tn), lambda i, j: (0, j)),
                  pl.BlockSpec((1, tn), lambda i, j: (0, j)),
                  pl.BlockSpec((D_MODEL, LANES), lambda i, j: (0, 0)),
                  pl.BlockSpec((LANES, GLA_QK), lambda i, j: (0, 0)),
                  pl.BlockSpec((1, GLA_QK), lambda i, j: (0, 0))],
        out_specs=[pl.BlockSpec((tm, tn), lambda i, j: (i, j)),
                   pl.BlockSpec((tm, GLA_QK), lambda i, j: (i, 0))],
        out_shape=[jax.ShapeDtypeStruct((T, D_PROJ), BF16),
                   jax.ShapeDtypeStruct((T, GLA_QK), F32)],
        scratch_shapes=[pltpu.VMEM((tm, D_MODEL), BF16)],
        compiler_params=_params(("arbitrary", "arbitrary"), VMEM_LIMIT),
        name="inproj",
    )(x2, g_mix, w_main, colscale, w_alow, w_a2p, b_a2)


def _gla_kernel(q_ref, k_ref, v_ref, r_ref, g_ref, gout_ref, o_ref, st_ref):
    C = GLA_CHUNK

    @pl.when(pl.program_id(1) == 0)
    def _():
        st_ref[...] = jnp.zeros(st_ref.shape, F32)

    row = lax.broadcasted_iota(jnp.int32, (C, C), 0)
    col = lax.broadcasted_iota(jnp.int32, (C, C), 1)
    causal = col <= row
    ltri = causal.astype(BF16)
    g = g_ref[...]
    g_hi = g.astype(BF16)
    g_lo = (g - g_hi.astype(F32)).astype(BF16)
    G = jnp.dot(ltri, g_hi, preferred_element_type=F32) + jnp.dot(ltri, g_lo, preferred_element_type=F32)
    mid = C // 2
    for h in range(GLA_HEADS):
        ks = slice(h * GLA_DK, (h + 1) * GLA_DK)
        vs = slice(h * GLA_DV, (h + 1) * GLA_DV)
        Gh = G[:, ks]
        qh = q_ref[:, ks].astype(F32)
        kh = k_ref[:, ks].astype(F32)
        vh = v_ref[:, vs]
        g_mid = Gh[mid:mid + 1, :]
        g_last = Gh[C - 1:C, :]
        q_in = (qh * jnp.exp(Gh)).astype(BF16)
        q_a = (qh * jnp.exp(Gh - g_mid)).astype(BF16)
        k_a = (kh * jnp.exp(g_mid - Gh)).astype(BF16)
        k_d = (kh * jnp.exp(g_last - Gh)).astype(BF16)
        A = lax.dot_general(q_a, k_a, NT, preferred_element_type=F32)
        A = jnp.where(causal, A, 0.0).astype(BF16)
        intra = jnp.dot(A, vh, preferred_element_type=F32)
        st = st_ref[h]
        inter = lax.dot_general(q_in, st.astype(BF16), NT, preferred_element_type=F32)
        o = inter + intra
        st_ref[h] = jnp.exp(g_last) * st + lax.dot_general(vh, k_d, TN, preferred_element_type=F32)
        r = r_ref[:, vs].astype(F32)
        o_ref[:, vs] = (_rms(o, gout_ref[...]) * (r * _sigmoid(r))).astype(BF16)


def _gla(proj, glog, g_gla_out, B, S):
    C = GLA_CHUNK
    nc = S // C
    return pl.pallas_call(
        _gla_kernel,
        grid=(B, nc),
        in_specs=[pl.BlockSpec((C, GLA_QK), lambda b, c: (b * nc + c, OFF_QA // GLA_QK)),
                  pl.BlockSpec((C, GLA_QK), lambda b, c: (b * nc + c, OFF_KA // GLA_QK)),
                  pl.BlockSpec((C, GLA_V), lambda b, c: (b * nc + c, OFF_VA // GLA_V)),
                  pl.BlockSpec((C, GLA_V), lambda b, c: (b * nc + c, OFF_RA // GLA_V)),
                  pl.BlockSpec((C, GLA_QK), lambda b, c: (b * nc + c, 0)),
                  pl.BlockSpec((1, GLA_DV), lambda b, c: (0, 0))],
        out_specs=pl.BlockSpec((C, GLA_V), lambda b, c: (b * nc + c, 0)),
        out_shape=jax.ShapeDtypeStruct((B * S, GLA_V), BF16),
        scratch_shapes=[pltpu.VMEM((GLA_HEADS, GLA_DV, GLA_DK), F32)],
        compiler_params=_params(("arbitrary", "arbitrary")),
        name="gla",
    )(proj, proj, proj, proj, glog, g_gla_out)


def _moba_kernel(q_ref, k_ref, v_ref, bias_ref, o_ref, km_scr, *, nblk):
    BLK = MOBA_BLOCK
    c = pl.program_id(2)

    @pl.when(c == 0)
    def _():
        for j in range(nblk):
            kb = k_ref[j * BLK:(j + 1) * BLK, :].astype(F32)
            km_scr[j:j + 1, :] = jnp.mean(kb, axis=0, keepdims=True)

    q = q_ref[...]
    for cc in range(nblk):
        @pl.when(c == cc)
        def _(cc=cc):
            pen_t = None
            if cc > MOBA_TOPK:
                s = lax.dot_general(km_scr[...], q.astype(F32), NT, preferred_element_type=F32,
                                    precision=lax.Precision.HIGHEST)
                ji = lax.broadcasted_iota(jnp.int32, s.shape, 0)
                cnt = jnp.zeros(s.shape, F32)
                for jp in range(cc):
                    sj = s[jp:jp + 1, :]
                    beats = (sj > s) | ((sj == s) & (jp < ji))
                    cnt = cnt + beats.astype(F32)
                pen = jnp.where((ji < cc) & (cnt < MOBA_TOPK), 0.0, NEG)
                pen_t = pen.T
            logits = []
            for j in range(cc + 1):
                kj = k_ref[j * BLK:(j + 1) * BLK, :]
                lg = lax.dot_general(q, kj, NT, preferred_element_type=F32) + bias_ref[0, cc - j]
                if pen_t is not None and j < cc:
                    lg = lg + pen_t[:, j:j + 1]
                logits.append(lg)
            m = logits[0].max(axis=-1, keepdims=True)
            for lg in logits[1:]:
                m = jnp.maximum(m, lg.max(axis=-1, keepdims=True))
            den = jnp.zeros(m.shape, F32)
            acc = jnp.zeros((BLK, MOBA_HD), F32)
            for j, lg in enumerate(logits):
                p = jnp.exp(lg - m)
                den = den + p.sum(axis=-1, keepdims=True)
                acc = acc + jnp.dot(p.astype(BF16), v_ref[j * BLK:(j + 1) * BLK, :], preferred_element_type=F32)
            o_ref[...] = (acc / den).astype(BF16)


def _moba(proj, bias, B, S):
    BLK = MOBA_BLOCK
    nblk = S // BLK
    H = MOBA_HEADS
    return pl.pallas_call(
        functools.partial(_moba_kernel, nblk=nblk),
        grid=(H, B, nblk),
        in_specs=[pl.BlockSpec((BLK, MOBA_HD), lambda h, b, c: (b * nblk + c, OFF_QB // MOBA_HD + h)),
                  pl.BlockSpec((S, MOBA_HD), lambda h, b, c: (b, OFF_KB // MOBA_HD + h)),
                  pl.BlockSpec((S, MOBA_HD), lambda h, b, c: (b, OFF_VB // MOBA_HD + h)),
                  pl.BlockSpec((1, nblk, BLK, BLK), lambda h, b, c: (h, 0, 0, 0))],
        out_specs=pl.BlockSpec((BLK, MOBA_HD), lambda h, b, c: (b * nblk + c, h)),
        out_shape=jax.ShapeDtypeStruct((B * S, MOBA_W), BF16),
        scratch_shapes=[pltpu.VMEM((nblk, MOBA_HD), F32)],
        compiler_params=_params(("arbitrary", "arbitrary", "arbitrary"), VMEM_LIMIT),
        name="moba",
    )(proj, proj, proj, bias)


def _pack(lo, hi):
    lo_b = lax.bitcast_convert_type(lo.astype(BF16).astype(F32), jnp.uint32)
    hi_b = lax.bitcast_convert_type(hi.astype(BF16).astype(F32), jnp.uint32)
    return (lo_b >> 16) | (hi_b & jnp.uint32(0xFFFF0000))


def _unpack(w):
    lo = lax.bitcast_convert_type(w << 16, F32)
    hi = lax.bitcast_convert_type(w & jnp.uint32(0xFFFF0000), F32)
    return lo.astype(BF16), hi.astype(BF16)


def _merge_kernel(oa_ref, ob_ref, ga_ref, gb_ref, x_ref, wua_ref, wub_ref, wo_ref, gffn_ref, wr_ref, br_ref,
                  x1_ref, h2_ref, posw_ref, cnt_ref, carry_ref, cnt_scr):
    tm = x_ref.shape[0]

    @pl.when(pl.program_id(0) == 0)
    def _():
        cnt_scr[...] = jnp.zeros(cnt_scr.shape, F32)

    u_a = jnp.dot(oa_ref[...], wua_ref[...], preferred_element_type=F32)
    u_b = jnp.dot(ob_ref[...], wub_ref[...], preferred_element_type=F32)
    y = _sigmoid(ga_ref[...].astype(F32)) * u_a + _sigmoid(gb_ref[...].astype(F32)) * u_b
    x1 = x_ref[...] + jnp.dot(y.astype(BF16), wo_ref[...], preferred_element_type=F32)
    x1_ref[...] = x1
    h2 = _rms(x1, gffn_ref[...])
    h2_ref[...] = h2.astype(BF16)
    logits = jnp.dot(h2, wr_ref[...], preferred_element_type=F32, precision=lax.Precision.HIGHEST) + br_ref[...]
    lane = lax.broadcasted_iota(jnp.int32, (tm, LANES), 1)
    lane_f = lane.astype(F32)
    work = jnp.where(lane < N_EXPERTS, logits, NEG)
    vals, hots = [], []
    for _ in range(TOP_K):
        mx = work.max(axis=-1, keepdims=True)
        idx = jnp.min(jnp.where(work == mx, lane_f, float(LANES)), axis=-1, keepdims=True)
        hot = lane_f == idx
        vals.append(mx)
        hots.append(hot)
        work = jnp.where(hot, 2.0 * NEG, work)
    exps = [jnp.exp(v - vals[0]) for v in vals]
    den = exps[0] + exps[1] + exps[2] + exps[3]
    sel = jnp.zeros((tm, LANES), F32)
    for hot in hots:
        sel = sel + hot.astype(F32)
    row = lax.broadcasted_iota(jnp.int32, (tm, tm), 0)
    col = lax.broadcasted_iota(jnp.int32, (tm, tm), 1)
    below = (col < row).astype(BF16)
    local_rank = jnp.dot(below, sel.astype(BF16), preferred_element_type=F32)
    cnt_t = sel.sum(axis=0, keepdims=True)
    cnt_t = jnp.floor((cnt_t + (RUN_ALIGN - 1.0)) * (1.0 / RUN_ALIGN)) * RUN_ALIGN
    er = lax.broadcasted_iota(jnp.int32, (LANES, LANES), 0)
    ec = lax.broadcasted_iota(jnp.int32, (LANES, LANES), 1)
    before = (er < ec).astype(F32)
    tile_off = jnp.dot(jnp.broadcast_to(cnt_t, (8, LANES)), before, preferred_element_type=F32,
                       precision=lax.Precision.HIGHEST)[0:1]
    pos_all = local_rank + tile_off
    posw = jnp.zeros((tm, LANES), F32)
    for k in range(TOP_K):
        pk = jnp.sum(jnp.where(hots[k], pos_all, 0.0), axis=-1, keepdims=True)
        posw = jnp.where(lane == k, pk, posw)
        posw = jnp.where(lane == TOP_K + k, exps[k] / den, posw)
    posw_ref[...] = posw
    carry_ref[0] = cnt_scr[...]
    cnt_ref[0] = cnt_t
    cnt_scr[...] = cnt_scr[...] + cnt_t


def _merge(o_a, o_b, proj, x2, w_ua, w_ub, w_o, g_ffn, w_r, b_r, tm):
    T = x2.shape[0]
    nt = T // tm
    full = lambda shape: pl.BlockSpec(shape, lambda i: (0, 0))
    rowblk = lambda w: pl.BlockSpec((tm, w), lambda i: (i, 0))
    tilerow = pl.BlockSpec((1, 1, LANES), lambda i: (i, 0, 0))
    return pl.pallas_call(
        _merge_kernel,
        grid=(nt,),
        in_specs=[rowblk(GLA_V), rowblk(MOBA_W),
                  pl.BlockSpec((tm, D_MODEL), lambda i: (i, OFF_GA // D_MODEL)),
                  pl.BlockSpec((tm, D_MODEL), lambda i: (i, OFF_GB // D_MODEL)),
                  rowblk(D_MODEL),
                  full((GLA_V, D_MODEL)), full((MOBA_W, D_MODEL)), full((D_MODEL, D_MODEL)),
                  full((1, D_MODEL)), full((D_MODEL, LANES)), full((1, LANES))],
        out_specs=[rowblk(D_MODEL), rowblk(D_MODEL), rowblk(LANES), tilerow, tilerow],
        out_shape=[jax.ShapeDtypeStruct((T, D_MODEL), F32),
                   jax.ShapeDtypeStruct((T, D_MODEL), BF16),
                   jax.ShapeDtypeStruct((T, LANES), F32),
                   jax.ShapeDtypeStruct((nt, 1, LANES), F32),
                   jax.ShapeDtypeStruct((nt, 1, LANES), F32)],
        scratch_shapes=[pltpu.VMEM((1, LANES), F32)],
        compiler_params=_params(("arbitrary",), VMEM_LIMIT),
        name="merge",
    )(o_a, o_b, proj, proj, x2, w_ua, w_ub, w_o, g_ffn, w_r, b_r)


def _run_pieces(n, max_rows, fn):
    for b in reversed(range(RUN_ALIGN.bit_length() - 1, max_rows.bit_length())):
        size = 1 << b
        done = n & ~((2 << b) - 1)

        @pl.when((n & size) != 0)
        def _():
            fn(done, size)


def _aligned(i):
    return pl.multiple_of(i, RUN_ALIGN)


def _dispatch_kernel(toff_ref, eoff_ref, n_ref, zoff_ref, zn_ref, tail_ref, h2_ref, posw_ref, x_ref,
                     buf, zbuf, sems, *, nt):
    tm = h2_ref.shape[0]
    half = D_MODEL // 2
    j = pl.program_id(0)
    zrows = zbuf.shape[0]

    def tile_runs(t, act):
        slot = t % 2

        def body(e, c):
            r = t * N_EXPERTS + e
            t0, d0 = toff_ref[r], eoff_ref[r]
            _run_pieces(n_ref[r], tm, lambda done, size: act(pltpu.make_async_copy(
                buf.at[slot, pl.ds(_aligned(t0 + done), size)], x_ref.at[pl.ds(_aligned(d0 + done), size)],
                sems.at[slot])))
            return c

        lax.fori_loop(0, N_EXPERTS, body, 0)

    def zero_fill(act):
        def body(e, c):
            d0 = zoff_ref[e]
            _run_pieces(zn_ref[e], zrows, lambda done, size: act(pltpu.make_async_copy(
                zbuf.at[pl.ds(0, size)], x_ref.at[pl.ds(_aligned(d0 + done), size)], sems.at[2])))
            return c

        lax.fori_loop(0, N_EXPERTS, body, 0)

        def tail(i, c):
            act(pltpu.make_async_copy(zbuf, x_ref.at[pl.ds(pl.multiple_of(i * zrows, zrows), zrows)], sems.at[2]))
            return c

        lax.fori_loop(tail_ref[0], x_ref.shape[0] // zrows, tail, 0)

    start = lambda cp: cp.start()
    wait = lambda cp: cp.wait()

    @pl.when(j == 0)
    def _():
        zbuf[...] = jnp.zeros(zbuf.shape, zbuf.dtype)
        zero_fill(start)

    @pl.when(j >= 2)
    def _():
        tile_runs(j - 2, wait)

    posw = posw_ref[...]
    pos_t = posw.T
    sub = lax.broadcasted_iota(jnp.int32, (TILE_ROWS, tm), 0).astype(F32)
    perm = jnp.zeros((TILE_ROWS, tm), F32)
    for k in range(TOP_K):
        perm = perm + (sub == pos_t[k:k + 1, :]).astype(F32)
    xs = jnp.dot(perm.astype(BF16), h2_ref[...], preferred_element_type=F32)
    buf[j % 2] = _pack(xs[:, :half], xs[:, half:])
    tile_runs(j, start)

    @pl.when(j == nt - 1)
    def _():
        if nt >= 2:
            tile_runs(j - 1, wait)
        tile_runs(j, wait)
        zero_fill(wait)


def _dispatch(h2, posw, n_rows, tile_off, expert_off, run_n, zoff, zn, tail, tm):
    T = h2.shape[0]
    nt = T // tm
    assert n_rows % EXPERT_BLOCK == 0
    return pl.pallas_call(
        functools.partial(_dispatch_kernel, nt=nt),
        grid_spec=pltpu.PrefetchScalarGridSpec(
            num_scalar_prefetch=6,
            grid=(nt,),
            in_specs=[pl.BlockSpec((tm, D_MODEL), lambda i, *_: (i, 0)),
                      pl.BlockSpec((tm, LANES), lambda i, *_: (i, 0))],
            out_specs=pl.BlockSpec(memory_space=pl.ANY),
            scratch_shapes=[pltpu.VMEM((2, TILE_ROWS, D_MODEL // 2), jnp.uint32),
                            pltpu.VMEM((EXPERT_BLOCK, D_MODEL // 2), jnp.uint32),
                            pltpu.SemaphoreType.DMA((3,))]),
        out_shape=jax.ShapeDtypeStruct((n_rows, D_MODEL // 2), jnp.uint32),
        compiler_params=_params(("arbitrary",), VMEM_LIMIT),
        name="dispatch",
    )(tile_off, expert_off, run_n, zoff, zn, tail, h2, posw)


def _expert_kernel(be_ref, nu_ref, x_ref, wg_ref, bg_ref, wu_ref, bu_ref, wd_ref, bd_ref, y_ref,
                   wg_s, wu_s, wd_s):
    i = pl.program_id(0)
    half = D_MODEL // 2
    used = i < nu_ref[0]
    e = be_ref[i]
    prev = be_ref[jnp.maximum(i - 1, 0)]

    @pl.when(used & ((i == 0) | (e != prev)))
    def _():
        wg_s[...] = wg_ref[0].astype(BF16)
        wu_s[...] = wu_ref[0].astype(BF16)
        wd_s[...] = wd_ref[0].astype(BF16)

    @pl.when(used)
    def _():
        x_lo, x_hi = _unpack(x_ref[...])

        def proj_in(w_s, b_ref):
            return (jnp.dot(x_lo, w_s[:half, :], preferred_element_type=F32)
                    + jnp.dot(x_hi, w_s[half:, :], preferred_element_type=F32) + b_ref[0])

        gate = jnp.minimum(proj_in(wg_s, bg_ref), SWIGLU_LIMIT)
        up = jnp.clip(proj_in(wu_s, bu_ref), -SWIGLU_LIMIT, SWIGLU_LIMIT)
        glu = gate * _sigmoid(gate * SWIGLU_ALPHA)
        act = ((up + 1.0) * glu).astype(BF16)
        y = jnp.dot(act, wd_s[...], preferred_element_type=F32) + bd_ref[0]
        y_ref[...] = _pack(y[:, :half], y[:, half:])

    @pl.when(jnp.logical_not(used))
    def _():
        y_ref[...] = jnp.zeros(y_ref.shape, y_ref.dtype)


def _experts(blk_exp, n_used, x_rows, n_pad, w_g, b_g, w_u, b_u, w_d, b_d):
    M = EXPERT_BLOCK
    wspec = lambda a, b: pl.BlockSpec((1, a, b), lambda i, be, nu: (be[i], 0, 0))
    return pl.pallas_call(
        _expert_kernel,
        grid_spec=pltpu.PrefetchScalarGridSpec(
            num_scalar_prefetch=2,
            grid=(n_pad // M,),
            in_specs=[pl.BlockSpec((M, D_MODEL // 2), lambda i, be, nu: (i, 0)),
                      wspec(D_MODEL, D_FF), wspec(1, D_FF),
                      wspec(D_MODEL, D_FF), wspec(1, D_FF),
                      wspec(D_FF, D_MODEL), wspec(1, D_MODEL)],
            out_specs=pl.BlockSpec((M, D_MODEL // 2), lambda i, be, nu: (i, 0)),
            scratch_shapes=[pltpu.VMEM((D_MODEL, D_FF), BF16),
                            pltpu.VMEM((D_MODEL, D_FF), BF16),
                            pltpu.VMEM((D_FF, D_MODEL), BF16)]),
        out_shape=jax.ShapeDtypeStruct((n_pad, D_MODEL // 2), jnp.uint32),
        compiler_params=_params(("arbitrary",), VMEM_LIMIT),
        name="experts",
    )(blk_exp, n_used, x_rows, w_g, b_g, w_u, b_u, w_d, b_d)


def _final_kernel(toff_ref, eoff_ref, n_ref, trows_ref, x1_ref, posw_ref, p_ref, gpg_ref, wpg_ref, wpp_ref, gpp_ref,
                  gfin_ref, y_ref, o_ref, buf, sems, *, nt):
    tm = x1_ref.shape[0]
    j = pl.program_id(0)

    def tile_runs(t, act):
        slot = t % 2

        def body(e, c):
            r = t * N_EXPERTS + e
            t0, s0 = toff_ref[r], eoff_ref[r]
            _run_pieces(n_ref[r], tm, lambda done, size: act(pltpu.make_async_copy(
                y_ref.at[pl.ds(_aligned(s0 + done), size)], buf.at[slot, pl.ds(_aligned(t0 + done), size)],
                sems.at[slot])))
            return c

        lax.fori_loop(0, N_EXPERTS, body, 0)

    @pl.when(j == 0)
    def _():
        buf[...] = jnp.zeros(buf.shape, buf.dtype)
        tile_runs(0, lambda cp: cp.start())

    @pl.when(j + 1 < nt)
    def _():
        tile_runs(j + 1, lambda cp: cp.start())

    tile_runs(j, lambda cp: cp.wait())
    posw = posw_ref[...]
    lane = lax.broadcasted_iota(jnp.int32, (tm, TILE_ROWS), 1).astype(F32)
    comb = jnp.zeros((tm, TILE_ROWS), F32)
    for k in range(TOP_K):
        comb = comb + jnp.where(lane == posw[:, k:k + 1], posw[:, TOP_K + k:TOP_K + k + 1], 0.0)
    comb = comb.astype(BF16)
    live = lax.broadcasted_iota(jnp.int32, (TILE_ROWS, D_MODEL // 2), 0) < trows_ref[j]
    y_lo, y_hi = _unpack(jnp.where(live, buf[j % 2], jnp.uint32(0)))
    moe = jnp.concatenate([jnp.dot(comb, y_lo, preferred_element_type=F32),
                           jnp.dot(comb, y_hi, preferred_element_type=F32)], axis=-1)
    x = x1_ref[...] + moe
    pg = _sigmoid(jnp.dot(_rms(x, gpg_ref[...]).astype(BF16), wpg_ref[...], preferred_element_type=F32))
    pp = jnp.dot(p_ref[...].astype(BF16), wpp_ref[...], preferred_element_type=F32)
    x = x + pg * _rms(pp, gpp_ref[...])
    o_ref[...] = _rms(x, gfin_ref[...])


def _final(tile_off, expert_off, run_n, tile_rows, x1, y_rows, posw, p2, g_pg, w_pg, w_pp, g_pp, g_fin, tm):
    T = x1.shape[0]
    nt = T // tm
    full = lambda shape: pl.BlockSpec(shape, lambda i, *_: (0, 0))
    rowblk = lambda w: pl.BlockSpec((tm, w), lambda i, *_: (i, 0))
    return pl.pallas_call(
        functools.partial(_final_kernel, nt=nt),
        grid_spec=pltpu.PrefetchScalarGridSpec(
            num_scalar_prefetch=4,
            grid=(nt,),
            in_specs=[rowblk(D_MODEL), rowblk(LANES), rowblk(PLE_DIM),
                      full((1, D_MODEL)), full((D_MODEL, D_MODEL)), full((PLE_DIM, D_MODEL)),
                      full((1, D_MODEL)), full((1, D_MODEL)),
                      pl.BlockSpec(memory_space=pl.ANY)],
            out_specs=rowblk(D_MODEL),
            scratch_shapes=[pltpu.VMEM((2, TILE_ROWS, D_MODEL // 2), jnp.uint32),
                            pltpu.SemaphoreType.DMA((2,))]),
        out_shape=jax.ShapeDtypeStruct((T, D_MODEL), F32),
        compiler_params=_params(("arbitrary",), VMEM_LIMIT),
        name="final",
    )(tile_off, expert_off, run_n, tile_rows, x1, posw, p2, g_pg, w_pg, w_pp, g_pp, g_fin, y_rows)


def _split_w_in(w_in):
    sizes = (GLA_QK, GLA_QK, GLA_V, GLA_V, GLA_RANK, MOBA_W, MOBA_W, MOBA_W, D_MODEL, D_MODEL)
    offs = [0]
    for s in sizes:
        offs.append(offs[-1] + s)
    parts = [w_in[:, offs[i]:offs[i + 1]] for i in range(len(sizes))]
    main = jnp.concatenate(parts[:4] + parts[5:], axis=1).astype(BF16)
    alow = jnp.pad(parts[4], ((0, 0), (0, LANES - GLA_RANK))).astype(BF16)
    return main, alow


def _layer(x2, p2, bias, B, S, g_mix, w_in, w_a2, b_a2, g_gla_out, w_up_gla, w_up_moba, w_o, g_ffn, w_router,
           b_router, w_e_gate, b_e_gate, w_e_up, b_e_up, w_e_down, b_e_down, g_ple_gate, w_ple_gate, w_ple_proj,
           g_ple_proj, g_final):
    T = B * S
    row = lambda v: v.reshape(1, -1).astype(F32)
    w_main, w_alow = _split_w_in(w_in)
    colscale = jnp.ones((D_PROJ,), F32)
    colscale = colscale.at[OFF_QA:OFF_QA + GLA_QK].set(GLA_DK ** -0.5)
    colscale = colscale.at[OFF_QB:OFF_QB + MOBA_W].set(MOBA_HD ** -0.5)
    w_a2p = jnp.pad(w_a2, ((0, LANES - GLA_RANK), (0, 0))).astype(BF16)
    proj, glog = _inproj(x2, row(g_mix), w_main, colscale.reshape(1, -1), w_alow, w_a2p, row(b_a2))

    o_a = _gla(proj, glog, row(g_gla_out), B, S)
    o_b = _moba(proj, bias, B, S)

    w_r = jnp.pad(w_router.astype(F32), ((0, 0), (0, LANES - N_EXPERTS)))
    b_r = jnp.pad(b_router.astype(F32), (0, LANES - N_EXPERTS)).reshape(1, -1)
    tm = TOKEN_TILE
    nt = T // tm
    x1, h2, posw, cnt_t, carry = _merge(o_a, o_b, proj, x2, w_up_gla.astype(BF16), w_up_moba.astype(BF16),
                                        w_o.astype(BF16), row(g_ffn), w_r, b_r, tm)

    M = EXPERT_BLOCK
    A = nt * TILE_ROWS
    n_pad = (-(-A // M)) * M + N_EXPERTS * M
    n_blk = n_pad // M
    cnt_t = cnt_t[:, 0, :N_EXPERTS].astype(jnp.int32)
    carry = carry[:, 0, :N_EXPERTS].astype(jnp.int32)
    counts = carry[-1] + cnt_t[-1]
    padded = (counts + M - 1) // M * M
    pad_end = jnp.cumsum(padded)
    pad_start = pad_end - padded
    blk_exp = jnp.minimum(jnp.sum(pad_end[None, :] <= (jnp.arange(n_blk, dtype=jnp.int32) * M)[:, None], axis=1),
                          N_EXPERTS - 1).astype(jnp.int32)
    n_used = (pad_end[-1:] // M).astype(jnp.int32)
    tile_off = (jnp.cumsum(cnt_t, axis=1) - cnt_t).reshape(-1)
    expert_off = (carry + pad_start[None, :]).reshape(-1)
    run_n = cnt_t.reshape(-1)
    x_rows = _dispatch(h2, posw, n_pad, tile_off, expert_off, run_n, pad_start + counts, padded - counts, n_used, tm)
    y_rows = _experts(blk_exp, n_used, x_rows, n_pad, w_e_gate, b_e_gate.reshape(N_EXPERTS, 1, D_FF),
                      w_e_up, b_e_up.reshape(N_EXPERTS, 1, D_FF), w_e_down,
                      b_e_down.reshape(N_EXPERTS, 1, D_MODEL))
    return _final(tile_off, expert_off, run_n, jnp.sum(cnt_t, axis=1), x1, y_rows, posw, p2, row(g_ple_gate),
                  w_ple_gate.astype(BF16), w_ple_proj.astype(BF16), row(g_ple_proj), row(g_final), tm)


def kernel(x, p, rel_bias, g_mix, w_in, w_a2, b_a2, g_gla_out, w_up_gla, w_up_moba, w_o, g_ffn, w_router, b_router,
           w_e_gate, b_e_gate, w_e_up, b_e_up, w_e_down, b_e_down, g_ple_gate, w_ple_gate, w_ple_proj, g_ple_proj,
           g_final):
    B, S, D = x.shape
    assert D == D_MODEL and S % MOBA_BLOCK == 0 and S % GLA_CHUNK == 0 and p.shape[0] == 1
    bias = _bias_tiles(rel_bias, S // MOBA_BLOCK)
    out = _layer(x.reshape(B * S, D), p[0].reshape(B * S, PLE_DIM), bias, B, S,
                 g_mix[0], w_in[0], w_a2[0], b_a2[0], g_gla_out[0], w_up_gla[0], w_up_moba[0], w_o[0], g_ffn[0],
                 w_router[0], b_router[0], w_e_gate[0], b_e_gate[0], w_e_up[0], b_e_up[0], w_e_down[0],
                 b_e_down[0], g_ple_gate[0], w_ple_gate[0], w_ple_proj[0], g_ple_proj[0], g_final)
    return out.reshape(B, S, D)
```

```python
import functools
import math

import jax
import jax.numpy as jnp
from jax import lax
from jax.experimental import pallas as pl
from jax.experimental.pallas import tpu as pltpu

F32 = jnp.float32
BF16 = jnp.bfloat16

D_MODEL = 1024
PLE_DIM = 256
GLA_HEADS = 4
GLA_DK = 128
GLA_DV = 256
GLA_RANK = 16
GLA_TAU = 16.0
GLA_QK = GLA_HEADS * GLA_DK
GLA_V = GLA_HEADS * GLA_DV
MOBA_HEADS = 8
MOBA_HD = 128
MOBA_BLOCK = 256
MOBA_TOPK = 3
MOBA_W = MOBA_HEADS * MOBA_HD
REL_BUCKETS = 32
REL_MAX_DIST = 4096
N_EXPERTS = 32
TOP_K = 4
D_FF = 1024
SWIGLU_LIMIT = 7.0
SWIGLU_ALPHA = 1.702
EPS = 1e-6

LANES = 128
NEG = -1e30
LOG2E = math.log2(math.e)
VMEM_LIMIT = 56 * 1024 * 1024

OFF_QA, OFF_KA, OFF_VA, OFF_RA = 0, 512, 1024, 2048
OFF_QB, OFF_KB, OFF_VB, OFF_GA, OFF_GB = 3072, 4096, 5120, 6144, 7168
D_PROJ = 8192

GLA_CHUNK = 128
EXPERT_BLOCK = 512
TOKEN_TILE = 256
RUN_ALIGN = 8
TILE_ROWS = TOP_K * TOKEN_TILE + N_EXPERTS * RUN_ALIGN

NT = (((1,), (1,)), ((), ()))
TN = (((0,), (0,)), ((), ()))


def _params(sem, vmem=None):
    return pltpu.CompilerParams(dimension_semantics=sem, vmem_limit_bytes=vmem)


def _rms(x, g):
    return x * lax.rsqrt(jnp.mean(x * x, axis=-1, keepdims=True) + EPS) * g


def _sigmoid(x):
    return 1.0 / (1.0 + jnp.exp(-x))


def _bias_kernel(tab_ref, bkt_ref, o_ref):
    strip = 8

    def body(s, carry):
        r0 = pl.multiple_of(s * strip, strip)
        b = bkt_ref[0, pl.ds(r0, strip), :]
        accs = [jnp.zeros(b.shape, F32) for _ in range(MOBA_HEADS)]
        for bb in range(REL_BUCKETS):
            hit = b == bb
            for h in range(MOBA_HEADS):
                accs[h] = jnp.where(hit, tab_ref[h, bb], accs[h])
        for h in range(MOBA_HEADS):
            o_ref[h, 0, pl.ds(r0, strip), :] = jnp.where(b < 0, NEG, accs[h])
        return carry

    lax.fori_loop(0, MOBA_BLOCK // strip, body, 0)


def _bias_tiles(rel_bias, nblk):
    i = jnp.arange(MOBA_BLOCK, dtype=jnp.int32)
    dist = (jnp.arange(nblk, dtype=jnp.int32)[:, None, None] * MOBA_BLOCK + i[None, :, None] - i[None, None, :])
    n = jnp.maximum(dist, 0)
    max_exact = REL_BUCKETS // 2
    nf = jnp.maximum(n, 1).astype(F32)
    large = max_exact + (jnp.log(nf / max_exact) / math.log(REL_MAX_DIST / max_exact)
                         * (REL_BUCKETS - max_exact)).astype(jnp.int32)
    large = jnp.minimum(large, REL_BUCKETS - 1)
    bkt = jnp.where(dist < 0, -1, jnp.where(n < max_exact, n, large)).astype(jnp.int32)
    tab = rel_bias.astype(F32).T * LOG2E
    return pl.pallas_call(
        _bias_kernel,
        grid=(nblk,),
        in_specs=[pl.BlockSpec(memory_space=pltpu.SMEM),
                  pl.BlockSpec((1, MOBA_BLOCK, MOBA_BLOCK), lambda m: (m, 0, 0))],
        out_specs=pl.BlockSpec((MOBA_HEADS, 1, MOBA_BLOCK, MOBA_BLOCK), lambda m: (0, m, 0, 0)),
        out_shape=jax.ShapeDtypeStruct((MOBA_HEADS, nblk, MOBA_BLOCK, MOBA_BLOCK), F32),
        compiler_params=_params(("arbitrary",)),
        name="bias_tiles",
    )(tab, bkt)


def _inproj_kernel(x_ref, g_ref, w_ref, cs_ref, wal_ref, wa2_ref, ba2_ref, o_ref, glog_ref, h_scr):
    @pl.when(pl.program_id(1) == 0)
    def _():
        h = _rms(x_ref[...], g_ref[...]).astype(BF16)
        h_scr[...] = h
        a_low = jnp.dot(h, wal_ref[...], preferred_element_type=F32)
        a = jnp.dot(a_low.astype(BF16), wa2_ref[...], preferred_element_type=F32) + ba2_ref[...]
        log_sig = jnp.minimum(a, 0.0) - jnp.log1p(jnp.exp(-jnp.abs(a)))
        glog_ref[...] = log_sig * (1.0 / GLA_TAU)

    acc = jnp.dot(h_scr[...], w_ref[...], preferred_element_type=F32)
    o_ref[...] = (acc * cs_ref[...]).astype(BF16)


def _inproj(x2, g_mix, w_main, colscale, w_alow, w_a2p, b_a2, tm=1024, tn=1024):
    T = x2.shape[0]
    return pl.pallas_call(
        _inproj_kernel,
        grid=(T // tm, D_PROJ // tn),
        in_specs=[pl.BlockSpec((tm, D_MODEL), lambda i, j: (i, 0)),
                  pl.BlockSpec((1, D_MODEL), lambda i, j: (0, 0)),
                  pl.BlockSpec((D_MODEL, tn), lambda i, j: (0, j)),
                  pl.BlockSpec((1, tn), lambda i, j: (0, j)),
                  pl.BlockSpec((D_MODEL, LANES), lambda i, j: (0, 0)),
                  pl.BlockSpec((LANES, GLA_QK), lambda i, j: (0, 0)),
                  pl.BlockSpec((1, GLA_QK), lambda i, j: (0, 0))],
        out_specs=[pl.BlockSpec((tm, tn), lambda i, j: (i, j)),
                   pl.BlockSpec((tm, GLA_QK), lambda i, j: (i, 0))],
        out_shape=[jax.ShapeDtypeStruct((T, D_PROJ), BF16),
                   jax.ShapeDtypeStruct((T, GLA_QK), F32)],
        scratch_shapes=[pltpu.VMEM((tm, D_MODEL), BF16)],
        compiler_params=_params(("arbitrary", "arbitrary"), VMEM_LIMIT),
        name="inproj",
    )(x2, g_mix, w_main, colscale, w_alow, w_a2p, b_a2)


def _gla_kernel(q_ref, k_ref, v_ref, r_ref, g_ref, gout_ref, o_ref, st_ref):
    C = GLA_CHUNK

    @pl.when(pl.program_id(1) == 0)
    def _():
        st_ref[...] = jnp.zeros(st_ref.shape, F32)

    row = lax.broadcasted_iota(jnp.int32, (C, C), 0)
    col = lax.broadcasted_iota(jnp.int32, (C, C), 1)
    causal = col <= row
    ltri = causal.astype(BF16)
    g = g_ref[...]
    g_hi = g.astype(BF16)
    g_lo = (g - g_hi.astype(F32)).astype(BF16)
    G = jnp.dot(ltri, g_hi, preferred_element_type=F32) + jnp.dot(ltri, g_lo, preferred_element_type=F32)
    mid = C // 2
    for h in range(GLA_HEADS):
        ks = slice(h * GLA_DK, (h + 1) * GLA_DK)
        vs = slice(h * GLA_DV, (h + 1) * GLA_DV)
        Gh = G[:, ks]
        qh = q_ref[:, ks].astype(F32)
        kh = k_ref[:, ks].astype(F32)
        vh = v_ref[:, vs]
        g_mid = Gh[mid:mid + 1, :]
        g_last = Gh[C - 1:C, :]
        q_in = (qh * jnp.exp(Gh)).astype(BF16)
        q_a = (qh * jnp.exp(Gh - g_mid)).astype(BF16)
        k_a = (kh * jnp.exp(g_mid - Gh)).astype(BF16)
        k_d = (kh * jnp.exp(g_last - Gh)).astype(BF16)
        A = lax.dot_general(q_a, k_a, NT, preferred_element_type=F32)
        A = jnp.where(causal, A, 0.0).astype(BF16)
        intra = jnp.dot(A, vh, preferred_element_type=F32)
        st = st_ref[h]
        inter = lax.dot_general(q_in, st.astype(BF16), NT, preferred_element_type=F32)
        o = inter + intra
        st_ref[h] = jnp.exp(g_last) * st + lax.dot_general(vh, k_d, TN, preferred_element_type=F32)
        r = r_ref[:, vs].astype(F32)
        o_ref[:, vs] = (_rms(o, gout_ref[...]) * (r * _sigmoid(r))).astype(BF16)


def _gla(proj, glog, g_gla_out, B, S):
    C = GLA_CHUNK
    nc = S // C
    return pl.pallas_call(
        _gla_kernel,
        grid=(B, nc),
        in_specs=[pl.BlockSpec((C, GLA_QK), lambda b, c: (b * nc + c, OFF_QA // GLA_QK)),
                  pl.BlockSpec((C, GLA_QK), lambda b, c: (b * nc + c, OFF_KA // GLA_QK)),
                  pl.BlockSpec((C, GLA_V), lambda b, c: (b * nc + c, OFF_VA // GLA_V)),
                  pl.BlockSpec((C, GLA_V), lambda b, c: (b * nc + c, OFF_RA // GLA_V)),
                  pl.BlockSpec((C, GLA_QK), lambda b, c: (b * nc + c, 0)),
                  pl.BlockSpec((1, GLA_DV), lambda b, c: (0, 0))],
        out_specs=pl.BlockSpec((C, GLA_V), lambda b, c: (b * nc + c, 0)),
        out_shape=jax.ShapeDtypeStruct((B * S, GLA_V), BF16),
        scratch_shapes=[pltpu.VMEM((GLA_HEADS, GLA_DV, GLA_DK), F32)],
        compiler_params=_params(("arbitrary", "arbitrary")),
        name="gla",
    )(proj, proj, proj, proj, glog, g_gla_out)


def _moba_kernel(q_ref, k_ref, v_ref, bias_ref, o_ref, ka_scr, va_scr, lg_scr, mx_scr, qa_scr, *, nblk):
    BLK, HD = MOBA_BLOCK, MOBA_HD
    S = nblk * BLK

    @pl.when((pl.program_id(0) == 0) & (pl.program_id(1) == 0))
    def _():
        blk = lax.broadcasted_iota(jnp.int32, (S, HD), 0) // BLK
        lane = lax.broadcasted_iota(jnp.int32, (S, HD), 1)
        ka_scr[:, HD:] = (lane == blk).astype(BF16)
        va_scr[:, HD:] = (lane == 0).astype(BF16)

    ka_scr[:, :HD] = k_ref[...]
    va_scr[:, :HD] = v_ref[...]
    n_plain = min(MOBA_TOPK + 1, nblk)

    def tile_id(cc, j):
        return cc * (cc + 1) // 2 + j

    def stage1(cc, q_in, keys):
        mx = None
        for j in range(cc + 1):
            lg = (lax.dot_general(q_in, keys[j * BLK:(j + 1) * BLK, :], NT, preferred_element_type=F32)
                  + bias_ref[0, cc - j])
            lg_scr[tile_id(cc, j)] = lg
            t = jnp.maximum(lg[:, :LANES], lg[:, LANES:])
            mx = t if mx is None else jnp.maximum(mx, t)
            if j == cc:
                mx_scr[cc] = mx
            yield

    def stage2(cc):
        m = mx_scr[cc].max(axis=-1, keepdims=True)
        acc = jnp.zeros((BLK, 2 * HD), F32)
        for j in range(cc + 1):
            p = jnp.exp2(lg_scr[tile_id(cc, j)] - m).astype(BF16)
            acc = acc + jnp.dot(p, va_scr[j * BLK:(j + 1) * BLK, :], preferred_element_type=F32)
            if j == cc:
                o_ref[cc * BLK:(cc + 1) * BLK, :] = (acc[:, :HD] / acc[:, HD:HD + 1]).astype(BF16)
            yield

    def drain(gen):
        for _ in gen:
            pass

    def interleave(main, side, n_main, n_side):
        side_steps = (s for g in side for s in g)
        done = 0
        for i, _ in enumerate(s for g in main for s in g):
            assert n_main > 0
            want = (i + 1) * n_side // n_main
            while done < want and next(side_steps, "end") != "end":
                done += 1
        drain(side_steps)

    plain = [stage1(cc, q_ref[cc * BLK:(cc + 1) * BLK, :], k_ref) for cc in range(n_plain)]
    if nblk > n_plain:
        next(plain[0])
        ksum = [k_ref[j * BLK:(j + 1) * BLK, :].astype(F32).reshape(BLK // 8, 8, HD).sum(axis=0).sum(
            axis=0, keepdims=True) for j in range(nblk)]
        kmean = jnp.concatenate(ksum, axis=0) * (1.0 / BLK)
        km_hi = kmean.astype(BF16)
        km_lo = (kmean - km_hi.astype(F32)).astype(BF16)
        km2 = jnp.concatenate([km_hi, km_lo], axis=0)
        pens = []
        for cc in range(n_plain, nblk):
            q = q_ref[cc * BLK:(cc + 1) * BLK, :]
            s2 = lax.dot_general(km2, q, NT, preferred_element_type=F32)
            pens.append((cc, q, s2[:nblk] + s2[nblk:]))
        for g in plain[:2]:
            drain(g)
        for cc, q, s in pens:
            ji = lax.broadcasted_iota(jnp.int32, s.shape, 0)
            cnt = jnp.zeros(s.shape, F32)
            for jp in range(cc):
                sj = s[jp:jp + 1, :]
                beats = (sj > s) | ((sj == s) & (jp < ji))
                cnt = cnt + beats.astype(F32)
            pen = jnp.where((ji < cc) & (cnt >= MOBA_TOPK), NEG, 0.0)
            pen_t = jnp.concatenate([pen, jnp.zeros((HD - nblk, BLK), F32)], axis=0).T
            qa_scr[cc - n_plain] = jnp.concatenate([q, pen_t.astype(BF16)], axis=1)
    for g in plain:
        drain(g)

    @pl.when(pl.program_id(0) >= 0)
    def _():
        late = [stage1(cc, qa_scr[cc - n_plain], ka_scr) for cc in range(n_plain, nblk)]
        early = [stage2(cc) for cc in range(n_plain)]
        interleave(late, early, tile_id(nblk, 0) - tile_id(n_plain, 0), tile_id(n_plain, 0))

    @pl.when(pl.program_id(0) >= 0)
    def _():
        for cc in range(n_plain, nblk):
            drain(stage2(cc))


def _moba(proj, bias, B, S):
    BLK = MOBA_BLOCK
    nblk = S // BLK
    H = MOBA_HEADS
    assert nblk <= MOBA_HD
    return pl.pallas_call(
        functools.partial(_moba_kernel, nblk=nblk),
        grid=(H, B),
        in_specs=[pl.BlockSpec((S, MOBA_HD), lambda h, b: (b, OFF_QB // MOBA_HD + h)),
                  pl.BlockSpec((S, MOBA_HD), lambda h, b: (b, OFF_KB // MOBA_HD + h)),
                  pl.BlockSpec((S, MOBA_HD), lambda h, b: (b, OFF_VB // MOBA_HD + h)),
                  pl.BlockSpec((1, nblk, BLK, BLK), lambda h, b: (h, 0, 0, 0))],
        out_specs=pl.BlockSpec((S, MOBA_HD), lambda h, b: (b, h)),
        out_shape=jax.ShapeDtypeStruct((B * S, MOBA_W), BF16),
        scratch_shapes=[pltpu.VMEM((S, 2 * MOBA_HD), BF16), pltpu.VMEM((S, 2 * MOBA_HD), BF16),
                        pltpu.VMEM((nblk * (nblk + 1) // 2, BLK, BLK), F32),
                        pltpu.VMEM((nblk, BLK, LANES), F32),
                        pltpu.VMEM((max(nblk - MOBA_TOPK - 1, 1), BLK, 2 * MOBA_HD), BF16)],
        compiler_params=_params(("arbitrary", "arbitrary"), VMEM_LIMIT),
        name="moba",
    )(proj, proj, proj, bias)


def _pack(lo, hi):
    lo_b = lax.bitcast_convert_type(lo.astype(BF16).astype(F32), jnp.uint32)
    hi_b = lax.bitcast_convert_type(hi.astype(BF16).astype(F32), jnp.uint32)
    return (lo_b >> 16) | (hi_b & jnp.uint32(0xFFFF0000))


def _unpack(w):
    lo = lax.bitcast_convert_type(w << 16, F32)
    hi = lax.bitcast_convert_type(w & jnp.uint32(0xFFFF0000), F32)
    return lo.astype(BF16), hi.astype(BF16)


def _merge_kernel(oa_ref, ob_ref, ga_ref, gb_ref, x_ref, wua_ref, wub_ref, wo_ref, gffn_ref, wr_ref, br_ref,
                  x1_ref, h2_ref, posw_ref, cnt_ref, carry_ref, cnt_scr):
    tm = x_ref.shape[0]

    @pl.when(pl.program_id(0) == 0)
    def _():
        cnt_scr[...] = jnp.zeros(cnt_scr.shape, F32)

    u_a = jnp.dot(oa_ref[...], wua_ref[...], preferred_element_type=F32)
    u_b = jnp.dot(ob_ref[...], wub_ref[...], preferred_element_type=F32)
    y = _sigmoid(ga_ref[...].astype(F32)) * u_a + _sigmoid(gb_ref[...].astype(F32)) * u_b
    x1 = x_ref[...] + jnp.dot(y.astype(BF16), wo_ref[...], preferred_element_type=F32)
    x1_ref[...] = x1
    h2 = _rms(x1, gffn_ref[...])
    h2_ref[...] = h2.astype(BF16)
    logits = jnp.dot(h2, wr_ref[...], preferred_element_type=F32, precision=lax.Precision.HIGHEST) + br_ref[...]
    lane = lax.broadcasted_iota(jnp.int32, (tm, LANES), 1)
    lane_f = lane.astype(F32)
    work = jnp.where(lane < N_EXPERTS, logits, NEG)
    vals, hots = [], []
    for _ in range(TOP_K):
        mx = work.max(axis=-1, keepdims=True)
        idx = jnp.min(jnp.where(work == mx, lane_f, float(LANES)), axis=-1, keepdims=True)
        hot = lane_f == idx
        vals.append(mx)
        hots.append(hot)
        work = jnp.where(hot, 2.0 * NEG, work)
    exps = [jnp.exp(v - vals[0]) for v in vals]
    den = exps[0] + exps[1] + exps[2] + exps[3]
    sel = jnp.zeros((tm, LANES), F32)
    for hot in hots:
        sel = sel + hot.astype(F32)
    row = lax.broadcasted_iota(jnp.int32, (tm, tm), 0)
    col = lax.broadcasted_iota(jnp.int32, (tm, tm), 1)
    below = (col < row).astype(BF16)
    local_rank = jnp.dot(below, sel.astype(BF16), preferred_element_type=F32)
    cnt_t = sel.sum(axis=0, keepdims=True)
    cnt_t = jnp.floor((cnt_t + (RUN_ALIGN - 1.0)) * (1.0 / RUN_ALIGN)) * RUN_ALIGN
    er = lax.broadcasted_iota(jnp.int32, (LANES, LANES), 0)
    ec = lax.broadcasted_iota(jnp.int32, (LANES, LANES), 1)
    before = (er < ec).astype(F32)
    tile_off = jnp.dot(jnp.broadcast_to(cnt_t, (8, LANES)), before, preferred_element_type=F32,
                       precision=lax.Precision.HIGHEST)[0:1]
    pos_all = local_rank + tile_off
    posw = jnp.zeros((tm, LANES), F32)
    for k in range(TOP_K):
        pk = jnp.sum(jnp.where(hots[k], pos_all, 0.0), axis=-1, keepdims=True)
        posw = jnp.where(lane == k, pk, posw)
        posw = jnp.where(lane == TOP_K + k, exps[k] / den, posw)
    posw_ref[...] = posw
    carry_ref[0] = cnt_scr[...]
    cnt_ref[0] = cnt_t
    cnt_scr[...] = cnt_scr[...] + cnt_t


def _merge(o_a, o_b, proj, x2, w_ua, w_ub, w_o, g_ffn, w_r, b_r, tm):
    T = x2.shape[0]
    nt = T // tm
    full = lambda shape: pl.BlockSpec(shape, lambda i: (0, 0))
    rowblk = lambda w: pl.BlockSpec((tm, w), lambda i: (i, 0))
    tilerow = pl.BlockSpec((1, 1, LANES), lambda i: (i, 0, 0))
    return pl.pallas_call(
        _merge_kernel,
        grid=(nt,),
        in_specs=[rowblk(GLA_V), rowblk(MOBA_W),
                  pl.BlockSpec((tm, D_MODEL), lambda i: (i, OFF_GA // D_MODEL)),
                  pl.BlockSpec((tm, D_MODEL), lambda i: (i, OFF_GB // D_MODEL)),
                  rowblk(D_MODEL),
                  full((GLA_V, D_MODEL)), full((MOBA_W, D_MODEL)), full((D_MODEL, D_MODEL)),
                  full((1, D_MODEL)), full((D_MODEL, LANES)), full((1, LANES))],
        out_specs=[rowblk(D_MODEL), rowblk(D_MODEL), rowblk(LANES), tilerow, tilerow],
        out_shape=[jax.ShapeDtypeStruct((T, D_MODEL), F32),
                   jax.ShapeDtypeStruct((T, D_MODEL), BF16),
                   jax.ShapeDtypeStruct((T, LANES), F32),
                   jax.ShapeDtypeStruct((nt, 1, LANES), F32),
                   jax.ShapeDtypeStruct((nt, 1, LANES), F32)],
        scratch_shapes=[pltpu.VMEM((1, LANES), F32)],
        compiler_params=_params(("arbitrary",), VMEM_LIMIT),
        name="merge",
    )(o_a, o_b, proj, proj, x2, w_ua, w_ub, w_o, g_ffn, w_r, b_r)


def _run_pieces(n, max_rows, fn):
    for b in reversed(range(RUN_ALIGN.bit_length() - 1, max_rows.bit_length())):
        size = 1 << b
        done = n & ~((2 << b) - 1)

        @pl.when((n & size) != 0)
        def _():
            fn(done, size)


def _aligned(i):
    return pl.multiple_of(i, RUN_ALIGN)


def _dispatch_kernel(toff_ref, eoff_ref, n_ref, zoff_ref, zn_ref, tail_ref, h2_ref, posw_ref, x_ref,
                     buf, zbuf, sems, *, nt):
    tm = h2_ref.shape[0]
    half = D_MODEL // 2
    j = pl.program_id(0)
    zrows = zbuf.shape[0]

    def tile_runs(t, act):
        slot = t % 2

        def body(e, c):
            r = t * N_EXPERTS + e
            t0, d0 = toff_ref[r], eoff_ref[r]
            _run_pieces(n_ref[r], tm, lambda done, size: act(pltpu.make_async_copy(
                buf.at[slot, pl.ds(_aligned(t0 + done), size)], x_ref.at[pl.ds(_aligned(d0 + done), size)],
                sems.at[slot])))
            return c

        lax.fori_loop(0, N_EXPERTS, body, 0)

    def zero_fill(act):
        def body(e, c):
            d0 = zoff_ref[e]
            _run_pieces(zn_ref[e], zrows, lambda done, size: act(pltpu.make_async_copy(
                zbuf.at[pl.ds(0, size)], x_ref.at[pl.ds(_aligned(d0 + done), size)], sems.at[2])))
            return c

        lax.fori_loop(0, N_EXPERTS, body, 0)

        def tail(i, c):
            act(pltpu.make_async_copy(zbuf, x_ref.at[pl.ds(pl.multiple_of(i * zrows, zrows), zrows)], sems.at[2]))
            return c

        lax.fori_loop(tail_ref[0], x_ref.shape[0] // zrows, tail, 0)

    start = lambda cp: cp.start()
    wait = lambda cp: cp.wait()

    @pl.when(j == 0)
    def _():
        zbuf[...] = jnp.zeros(zbuf.shape, zbuf.dtype)
        zero_fill(start)

    @pl.when(j >= 2)
    def _():
        tile_runs(j - 2, wait)

    posw = posw_ref[...]
    pos_t = posw.T
    sub = lax.broadcasted_iota(jnp.int32, (TILE_ROWS, tm), 0).astype(F32)
    perm = jnp.zeros((TILE_ROWS, tm), F32)
    for k in range(TOP_K):
        perm = perm + (sub == pos_t[k:k + 1, :]).astype(F32)
    xs = jnp.dot(perm.astype(BF16), h2_ref[...], preferred_element_type=F32)
    buf[j % 2] = _pack(xs[:, :half], xs[:, half:])
    tile_runs(j, start)

    @pl.when(j == nt - 1)
    def _():
        if nt >= 2:
            tile_runs(j - 1, wait)
        tile_runs(j, wait)
        zero_fill(wait)


def _dispatch(h2, posw, n_rows, tile_off, expert_off, run_n, zoff, zn, tail, tm):
    T = h2.shape[0]
    nt = T // tm
    assert n_rows % EXPERT_BLOCK == 0
    return pl.pallas_call(
        functools.partial(_dispatch_kernel, nt=nt),
        grid_spec=pltpu.PrefetchScalarGridSpec(
            num_scalar_prefetch=6,
            grid=(nt,),
            in_specs=[pl.BlockSpec((tm, D_MODEL), lambda i, *_: (i, 0)),
                      pl.BlockSpec((tm, LANES), lambda i, *_: (i, 0))],
            out_specs=pl.BlockSpec(memory_space=pl.ANY),
            scratch_shapes=[pltpu.VMEM((2, TILE_ROWS, D_MODEL // 2), jnp.uint32),
                            pltpu.VMEM((EXPERT_BLOCK, D_MODEL // 2), jnp.uint32),
                            pltpu.SemaphoreType.DMA((3,))]),
        out_shape=jax.ShapeDtypeStruct((n_rows, D_MODEL // 2), jnp.uint32),
        compiler_params=_params(("arbitrary",), VMEM_LIMIT),
        name="dispatch",
    )(tile_off, expert_off, run_n, zoff, zn, tail, h2, posw)


def _expert_kernel(be_ref, nu_ref, x_ref, wg_ref, bg_ref, wu_ref, bu_ref, wd_ref, bd_ref, y_ref,
                   wg_s, wu_s, wd_s):
    i = pl.program_id(0)
    half = D_MODEL // 2
    used = i < nu_ref[0]
    e = be_ref[i]
    prev = be_ref[jnp.maximum(i - 1, 0)]

    @pl.when(used & ((i == 0) | (e != prev)))
    def _():
        wg_s[...] = wg_ref[0].astype(BF16)
        wu_s[...] = wu_ref[0].astype(BF16)
        wd_s[...] = wd_ref[0].astype(BF16)

    @pl.when(used)
    def _():
        x_lo, x_hi = _unpack(x_ref[...])

        def proj_in(w_s, b_ref):
            return (jnp.dot(x_lo, w_s[:half, :], preferred_element_type=F32)
                    + jnp.dot(x_hi, w_s[half:, :], preferred_element_type=F32) + b_ref[0])

        gate = jnp.minimum(proj_in(wg_s, bg_ref), SWIGLU_LIMIT)
        up = jnp.clip(proj_in(wu_s, bu_ref), -SWIGLU_LIMIT, SWIGLU_LIMIT)
        glu = gate * _sigmoid(gate * SWIGLU_ALPHA)
        act = ((up + 1.0) * glu).astype(BF16)
        y = jnp.dot(act, wd_s[...], preferred_element_type=F32) + bd_ref[0]
        y_ref[...] = _pack(y[:, :half], y[:, half:])

    @pl.when(jnp.logical_not(used))
    def _():
        y_ref[...] = jnp.zeros(y_ref.shape, y_ref.dtype)


def _experts(blk_exp, n_used, x_rows, n_pad, w_g, b_g, w_u, b_u, w_d, b_d):
    M = EXPERT_BLOCK
    wspec = lambda a, b: pl.BlockSpec((1, a, b), lambda i, be, nu: (be[i], 0, 0))
    return pl.pallas_call(
        _expert_kernel,
        grid_spec=pltpu.PrefetchScalarGridSpec(
            num_scalar_prefetch=2,
            grid=(n_pad // M,),
            in_specs=[pl.BlockSpec((M, D_MODEL // 2), lambda i, be, nu: (i, 0)),
                      wspec(D_MODEL, D_FF), wspec(1, D_FF),
                      wspec(D_MODEL, D_FF), wspec(1, D_FF),
                      wspec(D_FF, D_MODEL), wspec(1, D_MODEL)],
            out_specs=pl.BlockSpec((M, D_MODEL // 2), lambda i, be, nu: (i, 0)),
            scratch_shapes=[pltpu.VMEM((D_MODEL, D_FF), BF16),
                            pltpu.VMEM((D_MODEL, D_FF), BF16),
                            pltpu.VMEM((D_FF, D_MODEL), BF16)]),
        out_shape=jax.ShapeDtypeStruct((n_pad, D_MODEL // 2), jnp.uint32),
        compiler_params=_params(("arbitrary",), VMEM_LIMIT),
        name="experts",
    )(blk_exp, n_used, x_rows, w_g, b_g, w_u, b_u, w_d, b_d)


def _final_kernel(toff_ref, eoff_ref, n_ref, trows_ref, x1_ref, posw_ref, p_ref, gpg_ref, wpg_ref, wpp_ref, gpp_ref,
                  gfin_ref, y_ref, o_ref, buf, sems, *, nt):
    tm = x1_ref.shape[0]
    j = pl.program_id(0)

    def tile_runs(t, act):
        slot = t % 2

        def body(e, c):
            r = t * N_EXPERTS + e
            t0, s0 = toff_ref[r], eoff_ref[r]
            _run_pieces(n_ref[r], tm, lambda done, size: act(pltpu.make_async_copy(
                y_ref.at[pl.ds(_aligned(s0 + done), size)], buf.at[slot, pl.ds(_aligned(t0 + done), size)],
                sems.at[slot])))
            return c

        lax.fori_loop(0, N_EXPERTS, body, 0)

    @pl.when(j == 0)
    def _():
        buf[...] = jnp.zeros(buf.shape, buf.dtype)
        tile_runs(0, lambda cp: cp.start())

    @pl.when(j + 1 < nt)
    def _():
        tile_runs(j + 1, lambda cp: cp.start())

    tile_runs(j, lambda cp: cp.wait())
    posw = posw_ref[...]
    lane = lax.broadcasted_iota(jnp.int32, (tm, TILE_ROWS), 1).astype(F32)
    comb = jnp.zeros((tm, TILE_ROWS), F32)
    for k in range(TOP_K):
        comb = comb + jnp.where(lane == posw[:, k:k + 1], posw[:, TOP_K + k:TOP_K + k + 1], 0.0)
    comb = comb.astype(BF16)
    live = lax.broadcasted_iota(jnp.int32, (TILE_ROWS, D_MODEL // 2), 0) < trows_ref[j]
    y_lo, y_hi = _unpack(jnp.where(live, buf[j % 2], jnp.uint32(0)))
    moe = jnp.concatenate([jnp.dot(comb, y_lo, preferred_element_type=F32),
                           jnp.dot(comb, y_hi, preferred_element_type=F32)], axis=-1)
    x = x1_ref[...] + moe
    pg = _sigmoid(jnp.dot(_rms(x, gpg_ref[...]).astype(BF16), wpg_ref[...], preferred_element_type=F32))
    pp = jnp.dot(p_ref[...].astype(BF16), wpp_ref[...], preferred_element_type=F32)
    x = x + pg * _rms(pp, gpp_ref[...])
    o_ref[...] = _rms(x, gfin_ref[...])


def _final(tile_off, expert_off, run_n, tile_rows, x1, y_rows, posw, p2, g_pg, w_pg, w_pp, g_pp, g_fin, tm):
    T = x1.shape[0]
    nt = T // tm
    full = lambda shape: pl.BlockSpec(shape, lambda i, *_: (0, 0))
    rowblk = lambda w: pl.BlockSpec((tm, w), lambda i, *_: (i, 0))
    return pl.pallas_call(
        functools.partial(_final_kernel, nt=nt),
        grid_spec=pltpu.PrefetchScalarGridSpec(
            num_scalar_prefetch=4,
            grid=(nt,),
            in_specs=[rowblk(D_MODEL), rowblk(LANES), rowblk(PLE_DIM),
                      full((1, D_MODEL)), full((D_MODEL, D_MODEL)), full((PLE_DIM, D_MODEL)),
                      full((1, D_MODEL)), full((1, D_MODEL)),
                      pl.BlockSpec(memory_space=pl.ANY)],
            out_specs=rowblk(D_MODEL),
            scratch_shapes=[pltpu.VMEM((2, TILE_ROWS, D_MODEL // 2), jnp.uint32),
                            pltpu.SemaphoreType.DMA((2,))]),
        out_shape=jax.ShapeDtypeStruct((T, D_MODEL), F32),
        compiler_params=_params(("arbitrary",), VMEM_LIMIT),
        name="final",
    )(tile_off, expert_off, run_n, tile_rows, x1, posw, p2, g_pg, w_pg, w_pp, g_pp, g_fin, y_rows)


def _split_w_in(w_in):
    sizes = (GLA_QK, GLA_QK, GLA_V, GLA_V, GLA_RANK, MOBA_W, MOBA_W, MOBA_W, D_MODEL, D_MODEL)
    offs = [0]
    for s in sizes:
        offs.append(offs[-1] + s)
    parts = [w_in[:, offs[i]:offs[i + 1]] for i in range(len(sizes))]
    main = jnp.concatenate(parts[:4] + parts[5:], axis=1).astype(BF16)
    alow = jnp.pad(parts[4], ((0, 0), (0, LANES - GLA_RANK))).astype(BF16)
    return main, alow


def _layer(x2, p2, bias, B, S, g_mix, w_in, w_a2, b_a2, g_gla_out, w_up_gla, w_up_moba, w_o, g_ffn, w_router,
           b_router, w_e_gate, b_e_gate, w_e_up, b_e_up, w_e_down, b_e_down, g_ple_gate, w_ple_gate, w_ple_proj,
           g_ple_proj, g_final):
    T = B * S
    row = lambda v: v.reshape(1, -1).astype(F32)
    w_main, w_alow = _split_w_in(w_in)
    colscale = jnp.ones((D_PROJ,), F32)
    colscale = colscale.at[OFF_QA:OFF_QA + GLA_QK].set(GLA_DK ** -0.5)
    colscale = colscale.at[OFF_QB:OFF_QB + MOBA_W].set(MOBA_HD ** -0.5 * LOG2E)
    w_a2p = jnp.pad(w_a2, ((0, LANES - GLA_RANK), (0, 0))).astype(BF16)
    proj, glog = _inproj(x2, row(g_mix), w_main, colscale.reshape(1, -1), w_alow, w_a2p, row(b_a2))

    o_a = _gla(proj, glog, row(g_gla_out), B, S)
    o_b = _moba(proj, bias, B, S)

    w_r = jnp.pad(w_router.astype(F32), ((0, 0), (0, LANES - N_EXPERTS)))
    b_r = jnp.pad(b_router.astype(F32), (0, LANES - N_EXPERTS)).reshape(1, -1)
    tm = TOKEN_TILE
    nt = T // tm
    x1, h2, posw, cnt_t, carry = _merge(o_a, o_b, proj, x2, w_up_gla.astype(BF16), w_up_moba.astype(BF16),
                                        w_o.astype(BF16), row(g_ffn), w_r, b_r, tm)

    M = EXPERT_BLOCK
    A = nt * TILE_ROWS
    n_pad = (-(-A // M)) * M + N_EXPERTS * M
    n_blk = n_pad // M
    cnt_t = cnt_t[:, 0, :N_EXPERTS].astype(jnp.int32)
    carry = carry[:, 0, :N_EXPERTS].astype(jnp.int32)
    counts = carry[-1] + cnt_t[-1]
    padded = (counts + M - 1) // M * M
    pad_end = jnp.cumsum(padded)
    pad_start = pad_end - padded
    blk_exp = jnp.minimum(jnp.sum(pad_end[None, :] <= (jnp.arange(n_blk, dtype=jnp.int32) * M)[:, None], axis=1),
                          N_EXPERTS - 1).astype(jnp.int32)
    n_used = (pad_end[-1:] // M).astype(jnp.int32)
    tile_off = (jnp.cumsum(cnt_t, axis=1) - cnt_t).reshape(-1)
    expert_off = (carry + pad_start[None, :]).reshape(-1)
    run_n = cnt_t.reshape(-1)
    x_rows = _dispatch(h2, posw, n_pad, tile_off, expert_off, run_n, pad_start + counts, padded - counts, n_used, tm)
    y_rows = _experts(blk_exp, n_used, x_rows, n_pad, w_e_gate, b_e_gate.reshape(N_EXPERTS, 1, D_FF),
                      w_e_up, b_e_up.reshape(N_EXPERTS, 1, D_FF), w_e_down,
                      b_e_down.reshape(N_EXPERTS, 1, D_MODEL))
    return _final(tile_off, expert_off, run_n, jnp.sum(cnt_t, axis=1), x1, y_rows, posw, p2, row(g_ple_gate),
                  w_ple_gate.astype(BF16), w_ple_proj.astype(BF16), row(g_ple_proj), row(g_final), tm)


def kernel(x, p, rel_bias, g_mix, w_in, w_a2, b_a2, g_gla_out, w_up_gla, w_up_moba, w_o, g_ffn, w_router, b_router,
           w_e_gate, b_e_gate, w_e_up, b_e_up, w_e_down, b_e_down, g_ple_gate, w_ple_gate, w_ple_proj, g_ple_proj,
           g_final):
    B, S, D = x.shape
    assert D == D_MODEL and S % MOBA_BLOCK == 0 and S % GLA_CHUNK == 0 and p.shape[0] == 1
    bias = _bias_tiles(rel_bias, S // MOBA_BLOCK)
    out = _layer(x.reshape(B * S, D), p[0].reshape(B * S, PLE_DIM), bias, B, S,
                 g_mix[0], w_in[0], w_a2[0], b_a2[0], g_gla_out[0], w_up_gla[0], w_up_moba[0], w_o[0], g_ffn[0],
                 w_router[0], b_router[0], w_e_gate[0], b_e_gate[0], w_e_up[0], b_e_up[0], w_e_down[0],
                 b_e_down[0], g_ple_gate[0], w_ple_gate[0], w_ple_proj[0], g_ple_proj[0], g_final)
    return out.reshape(B, S, D)
```

```python
import functools
import math

import jax
import jax.numpy as jnp
from jax import lax
from jax.experimental import pallas as pl
from jax.experimental.pallas import tpu as pltpu

F32 = jnp.float32
BF16 = jnp.bfloat16

D_MODEL = 1024
PLE_DIM = 256
GLA_HEADS = 4
GLA_DK = 128
GLA_DV = 256
GLA_RANK = 16
GLA_TAU = 16.0
GLA_QK = GLA_HEADS * GLA_DK
GLA_V = GLA_HEADS * GLA_DV
MOBA_HEADS = 8
MOBA_HD = 128
MOBA_BLOCK = 256
MOBA_TOPK = 3
MOBA_W = MOBA_HEADS * MOBA_HD
REL_BUCKETS = 32
REL_MAX_DIST = 4096
N_EXPERTS = 32
TOP_K = 4
D_FF = 1024
SWIGLU_LIMIT = 7.0
SWIGLU_ALPHA = 1.702
EPS = 1e-6

LANES = 128
NEG = -1e30
LOG2E = math.log2(math.e)
VMEM_LIMIT = 56 * 1024 * 1024

OFF_QA, OFF_KA, OFF_VA, OFF_RA = 0, 512, 1024, 2048
OFF_QB, OFF_KB, OFF_VB, OFF_GA, OFF_GB = 3072, 4096, 5120, 6144, 7168
D_PROJ = 8192

GLA_CHUNK = 128
EXPERT_BLOCK = 512
TOKEN_TILE = 256
RUN_ALIGN = 8
TILE_ROWS = TOP_K * TOKEN_TILE + N_EXPERTS * RUN_ALIGN

NT = (((1,), (1,)), ((), ()))
TN = (((0,), (0,)), ((), ()))


def _params(sem, vmem=None):
    return pltpu.CompilerParams(dimension_semantics=sem, vmem_limit_bytes=vmem)


def _rms(x, g):
    return x * lax.rsqrt(jnp.mean(x * x, axis=-1, keepdims=True) + EPS) * g


def _sigmoid(x):
    return 1.0 / (1.0 + jnp.exp(-x))


def _bias_kernel(tab_ref, bkt_ref, o_ref):
    strip = 8

    def body(s, carry):
        r0 = pl.multiple_of(s * strip, strip)
        b = bkt_ref[0, pl.ds(r0, strip), :]
        accs = [jnp.zeros(b.shape, F32) for _ in range(MOBA_HEADS)]
        for bb in range(REL_BUCKETS):
            hit = b == bb
            for h in range(MOBA_HEADS):
                accs[h] = jnp.where(hit, tab_ref[h, bb], accs[h])
        for h in range(MOBA_HEADS):
            o_ref[h, 0, pl.ds(r0, strip), :] = jnp.where(b < 0, NEG, accs[h])
        return carry

    lax.fori_loop(0, MOBA_BLOCK // strip, body, 0)


def _bias_tiles(rel_bias, nblk):
    i = jnp.arange(MOBA_BLOCK, dtype=jnp.int32)
    dist = (jnp.arange(nblk, dtype=jnp.int32)[:, None, None] * MOBA_BLOCK + i[None, :, None] - i[None, None, :])
    n = jnp.maximum(dist, 0)
    max_exact = REL_BUCKETS // 2
    nf = jnp.maximum(n, 1).astype(F32)
    large = max_exact + (jnp.log(nf / max_exact) / math.log(REL_MAX_DIST / max_exact)
                         * (REL_BUCKETS - max_exact)).astype(jnp.int32)
    large = jnp.minimum(large, REL_BUCKETS - 1)
    bkt = jnp.where(dist < 0, -1, jnp.where(n < max_exact, n, large)).astype(jnp.int32)
    tab = rel_bias.astype(F32).T * LOG2E
    return pl.pallas_call(
        _bias_kernel,
        grid=(nblk,),
        in_specs=[pl.BlockSpec(memory_space=pltpu.SMEM),
                  pl.BlockSpec((1, MOBA_BLOCK, MOBA_BLOCK), lambda m: (m, 0, 0))],
        out_specs=pl.BlockSpec((MOBA_HEADS, 1, MOBA_BLOCK, MOBA_BLOCK), lambda m: (0, m, 0, 0)),
        out_shape=jax.ShapeDtypeStruct((MOBA_HEADS, nblk, MOBA_BLOCK, MOBA_BLOCK), F32),
        compiler_params=_params(("arbitrary",)),
        name="bias_tiles",
    )(tab, bkt)


def _inproj_kernel(x_ref, g_ref, w_ref, cs_ref, wal_ref, wa2_ref, ba2_ref, o_ref, glog_ref, h_scr):
    @pl.when(pl.program_id(1) == 0)
    def _():
        h = _rms(x_ref[...], g_ref[...]).astype(BF16)
        h_scr[...] = h
        a_low = jnp.dot(h, wal_ref[...], preferred_element_type=F32)
        a = jnp.dot(a_low.astype(BF16), wa2_ref[...], preferred_element_type=F32) + ba2_ref[...]
        log_sig = jnp.minimum(a, 0.0) - jnp.log1p(jnp.exp(-jnp.abs(a)))
        glog_ref[...] = log_sig * (1.0 / GLA_TAU)

    acc = jnp.dot(h_scr[...], w_ref[...], preferred_element_type=F32)
    o_ref[...] = (acc * cs_ref[...]).astype(BF16)


def _inproj(x2, g_mix, w_main, colscale, w_alow, w_a2p, b_a2, tm=1024, tn=1024):
    T = x2.shape[0]
    return pl.pallas_call(
        _inproj_kernel,
        grid=(T // tm, D_PROJ // tn),
        in_specs=[pl.BlockSpec((tm, D_MODEL), lambda i, j: (i, 0)),
                  pl.BlockSpec((1, D_MODEL), lambda i, j: (0, 0)),
                  pl.BlockSpec((D_MODEL, tn), lambda i, j: (0, j)),
                  pl.BlockSpec((1, tn), lambda i, j: (0, j)),
                  pl.BlockSpec((D_MODEL, LANES), lambda i, j: (0, 0)),
                  pl.BlockSpec((LANES, GLA_QK), lambda i, j: (0, 0)),
                  pl.BlockSpec((1, GLA_QK), lambda i, j: (0, 0))],
        out_specs=[pl.BlockSpec((tm, tn), lambda i, j: (i, j)),
                   pl.BlockSpec((tm, GLA_QK), lambda i, j: (i, 0))],
        out_shape=[jax.ShapeDtypeStruct((T, D_PROJ), BF16),
                   jax.ShapeDtypeStruct((T, GLA_QK), F32)],
        scratch_shapes=[pltpu.VMEM((tm, D_MODEL), BF16)],
        compiler_params=_params(("arbitrary", "arbitrary"), VMEM_LIMIT),
        name="inproj",
    )(x2, g_mix, w_main, colscale, w_alow, w_a2p, b_a2)


def _gla_kernel(q_ref, k_ref, v_ref, r_ref, g_ref, gout_ref, o_ref, st_ref):
    C = GLA_CHUNK

    @pl.when(pl.program_id(1) == 0)
    def _():
        st_ref[...] = jnp.zeros(st_ref.shape, F32)

    row = lax.broadcasted_iota(jnp.int32, (C, C), 0)
    col = lax.broadcasted_iota(jnp.int32, (C, C), 1)
    causal = col <= row
    ltri = causal.astype(BF16)
    g = g_ref[...]
    g_hi = g.astype(BF16)
    g_lo = (g - g_hi.astype(F32)).astype(BF16)
    G = jnp.dot(ltri, g_hi, preferred_element_type=F32) + jnp.dot(ltri, g_lo, preferred_element_type=F32)
    mid = C // 2
    for h in range(GLA_HEADS):
        ks = slice(h * GLA_DK, (h + 1) * GLA_DK)
        vs = slice(h * GLA_DV, (h + 1) * GLA_DV)
        Gh = G[:, ks]
        qh = q_ref[:, ks].astype(F32)
        kh = k_ref[:, ks].astype(F32)
        vh = v_ref[:, vs]
        g_mid = Gh[mid:mid + 1, :]
        g_last = Gh[C - 1:C, :]
        q_in = (qh * jnp.exp(Gh)).astype(BF16)
        q_a = (qh * jnp.exp(Gh - g_mid)).astype(BF16)
        k_a = (kh * jnp.exp(g_mid - Gh)).astype(BF16)
        k_d = (kh * jnp.exp(g_last - Gh)).astype(BF16)
        A = lax.dot_general(q_a, k_a, NT, preferred_element_type=F32)
        A = jnp.where(causal, A, 0.0).astype(BF16)
        intra = jnp.dot(A, vh, preferred_element_type=F32)
        st = st_ref[h]
        inter = lax.dot_general(q_in, st.astype(BF16), NT, preferred_element_type=F32)
        o = inter + intra
        st_ref[h] = jnp.exp(g_last) * st + lax.dot_general(vh, k_d, TN, preferred_element_type=F32)
        r = r_ref[:, vs].astype(F32)
        o_ref[:, vs] = (_rms(o, gout_ref[...]) * (r * _sigmoid(r))).astype(BF16)


def _gla(proj, glog, g_gla_out, B, S):
    C = GLA_CHUNK
    nc = S // C
    return pl.pallas_call(
        _gla_kernel,
        grid=(B, nc),
        in_specs=[pl.BlockSpec((C, GLA_QK), lambda b, c: (b * nc + c, OFF_QA // GLA_QK)),
                  pl.BlockSpec((C, GLA_QK), lambda b, c: (b * nc + c, OFF_KA // GLA_QK)),
                  pl.BlockSpec((C, GLA_V), lambda b, c: (b * nc + c, OFF_VA // GLA_V)),
                  pl.BlockSpec((C, GLA_V), lambda b, c: (b * nc + c, OFF_RA // GLA_V)),
                  pl.BlockSpec((C, GLA_QK), lambda b, c: (b * nc + c, 0)),
                  pl.BlockSpec((1, GLA_DV), lambda b, c: (0, 0))],
        out_specs=pl.BlockSpec((C, GLA_V), lambda b, c: (b * nc + c, 0)),
        out_shape=jax.ShapeDtypeStruct((B * S, GLA_V), BF16),
        scratch_shapes=[pltpu.VMEM((GLA_HEADS, GLA_DV, GLA_DK), F32)],
        compiler_params=_params(("arbitrary", "arbitrary")),
        name="gla",
    )(proj, proj, proj, proj, glog, g_gla_out)


def _moba_kernel(q_ref, k_ref, v_ref, bias_ref, o_ref, ka_scr, va_scr, lg_scr, mx_scr, qa_scr, *, nblk):
    BLK, HD = MOBA_BLOCK, MOBA_HD
    S = nblk * BLK

    @pl.when((pl.program_id(0) == 0) & (pl.program_id(1) == 0))
    def _():
        blk = lax.broadcasted_iota(jnp.int32, (S, HD), 0) // BLK
        lane = lax.broadcasted_iota(jnp.int32, (S, HD), 1)
        ka_scr[:, HD:] = (lane == blk).astype(BF16)
        va_scr[:, HD:] = (lane == 0).astype(BF16)

    ka_scr[:, :HD] = k_ref[...]
    va_scr[:, :HD] = v_ref[...]
    n_plain = min(MOBA_TOPK + 1, nblk)

    def tile_id(cc, j):
        return cc * (cc + 1) // 2 + j

    def stage1(cc, q_in, keys):
        mx = None
        for j in range(cc + 1):
            lg = (lax.dot_general(q_in, keys[j * BLK:(j + 1) * BLK, :], NT, preferred_element_type=F32)
                  + bias_ref[0, cc - j])
            lg_scr[tile_id(cc, j)] = lg
            t = jnp.maximum(lg[:, :LANES], lg[:, LANES:])
            mx = t if mx is None else jnp.maximum(mx, t)
            if j == cc:
                mx_scr[cc] = mx
            yield

    def stage2(cc):
        m = mx_scr[cc].max(axis=-1, keepdims=True)
        acc = jnp.zeros((BLK, 2 * HD), F32)
        for j in range(cc + 1):
            p = jnp.exp2(lg_scr[tile_id(cc, j)] - m).astype(BF16)
            acc = acc + jnp.dot(p, va_scr[j * BLK:(j + 1) * BLK, :], preferred_element_type=F32)
            if j == cc:
                o_ref[cc * BLK:(cc + 1) * BLK, :] = (acc[:, :HD] / acc[:, HD:HD + 1]).astype(BF16)
            yield

    def drain(gen):
        for _ in gen:
            pass

    def interleave(main, side, n_main, n_side):
        side_steps = (s for g in side for s in g)
        done = 0
        for i, _ in enumerate(s for g in main for s in g):
            assert n_main > 0
            want = (i + 1) * n_side // n_main
            while done < want and next(side_steps, "end") != "end":
                done += 1
        drain(side_steps)

    plain = [stage1(cc, q_ref[cc * BLK:(cc + 1) * BLK, :], k_ref) for cc in range(n_plain)]
    if nblk > n_plain:
        next(plain[0])
        ksum = [k_ref[j * BLK:(j + 1) * BLK, :].astype(F32).reshape(BLK // 8, 8, HD).sum(axis=0).sum(
            axis=0, keepdims=True) for j in range(nblk)]
        kmean = jnp.concatenate(ksum, axis=0) * (1.0 / BLK)
        km_hi = kmean.astype(BF16)
        km_lo = (kmean - km_hi.astype(F32)).astype(BF16)
        km2 = jnp.concatenate([km_hi, km_lo], axis=0)
        pens = []
        for cc in range(n_plain, nblk):
            q = q_ref[cc * BLK:(cc + 1) * BLK, :]
            s2 = lax.dot_general(km2, q, NT, preferred_element_type=F32)
            pens.append((cc, q, s2[:nblk] + s2[nblk:]))
        for g in plain[:2]:
            drain(g)
        for cc, q, s in pens:
            ji = lax.broadcasted_iota(jnp.int32, s.shape, 0)
            cnt = jnp.zeros(s.shape, F32)
            for jp in range(cc):
                sj = s[jp:jp + 1, :]
                beats = (sj > s) | ((sj == s) & (jp < ji))
                cnt = cnt + beats.astype(F32)
            pen = jnp.where((ji < cc) & (cnt >= MOBA_TOPK), NEG, 0.0)
            pen_t = jnp.concatenate([pen, jnp.zeros((HD - nblk, BLK), F32)], axis=0).T
            qa_scr[cc - n_plain] = jnp.concatenate([q, pen_t.astype(BF16)], axis=1)
    for g in plain:
        drain(g)

    @pl.when(pl.program_id(0) >= 0)
    def _():
        late = [stage1(cc, qa_scr[cc - n_plain], ka_scr) for cc in range(n_plain, nblk)]
        early = [stage2(cc) for cc in range(n_plain)]
        interleave(late, early, tile_id(nblk, 0) - tile_id(n_plain, 0), tile_id(n_plain, 0))

    @pl.when(pl.program_id(0) >= 0)
    def _():
        for cc in range(n_plain, nblk):
            drain(stage2(cc))


def _moba(proj, bias, B, S):
    BLK = MOBA_BLOCK
    nblk = S // BLK
    H = MOBA_HEADS
    assert nblk <= MOBA_HD
    return pl.pallas_call(
        functools.partial(_moba_kernel, nblk=nblk),
        grid=(H, B),
        in_specs=[pl.BlockSpec((S, MOBA_HD), lambda h, b: (b, OFF_QB // MOBA_HD + h)),
                  pl.BlockSpec((S, MOBA_HD), lambda h, b: (b, OFF_KB // MOBA_HD + h)),
                  pl.BlockSpec((S, MOBA_HD), lambda h, b: (b, OFF_VB // MOBA_HD + h)),
                  pl.BlockSpec((1, nblk, BLK, BLK), lambda h, b: (h, 0, 0, 0))],
        out_specs=pl.BlockSpec((S, MOBA_HD), lambda h, b: (b, h)),
        out_shape=jax.ShapeDtypeStruct((B * S, MOBA_W), BF16),
        scratch_shapes=[pltpu.VMEM((S, 2 * MOBA_HD), BF16), pltpu.VMEM((S, 2 * MOBA_HD), BF16),
                        pltpu.VMEM((nblk * (nblk + 1) // 2, BLK, BLK), F32),
                        pltpu.VMEM((nblk, BLK, LANES), F32),
                        pltpu.VMEM((max(nblk - MOBA_TOPK - 1, 1), BLK, 2 * MOBA_HD), BF16)],
        compiler_params=_params(("arbitrary", "arbitrary"), VMEM_LIMIT),
        name="moba",
    )(proj, proj, proj, bias)


def _pack(lo, hi):
    lo_b = lax.bitcast_convert_type(lo.astype(BF16).astype(F32), jnp.uint32)
    hi_b = lax.bitcast_convert_type(hi.astype(BF16).astype(F32), jnp.uint32)
    return (lo_b >> 16) | (hi_b & jnp.uint32(0xFFFF0000))


def _unpack(w):
    lo = lax.bitcast_convert_type(w << 16, F32)
    hi = lax.bitcast_convert_type(w & jnp.uint32(0xFFFF0000), F32)
    return lo.astype(BF16), hi.astype(BF16)


def _merge_kernel(oa_ref, ob_ref, ga_ref, gb_ref, x_ref, wua_ref, wub_ref, wo_ref, gffn_ref, wr_ref, br_ref,
                  x1_ref, h2_ref, lg_ref):
    tm = x_ref.shape[0]
    u_a = jnp.dot(oa_ref[...], wua_ref[...], preferred_element_type=F32)
    u_b = jnp.dot(ob_ref[...], wub_ref[...], preferred_element_type=F32)
    y = _sigmoid(ga_ref[...].astype(F32)) * u_a + _sigmoid(gb_ref[...].astype(F32)) * u_b
    x1 = x_ref[...] + jnp.dot(y.astype(BF16), wo_ref[...], preferred_element_type=F32)
    x1_ref[...] = x1
    h2 = _rms(x1, gffn_ref[...])
    h_hi = h2.astype(BF16)
    h2_ref[...] = h_hi
    h_lo = (h2 - h_hi.astype(F32)).astype(BF16)
    r = jnp.dot(jnp.concatenate([h_hi, h_lo], axis=0), wr_ref[...], preferred_element_type=F32)
    lg_ref[...] = r[:tm, :LANES] + r[:tm, LANES:] + r[tm:, :LANES] + br_ref[...]


def _merge(o_a, o_b, proj, x2, w_ua, w_ub, w_o, g_ffn, w_r2, b_r, tm):
    T = x2.shape[0]
    full = lambda shape: pl.BlockSpec(shape, lambda i: (0, 0))
    rowblk = lambda w: pl.BlockSpec((tm, w), lambda i: (i, 0))
    return pl.pallas_call(
        _merge_kernel,
        grid=(T // tm,),
        in_specs=[rowblk(GLA_V), rowblk(MOBA_W),
                  pl.BlockSpec((tm, D_MODEL), lambda i: (i, OFF_GA // D_MODEL)),
                  pl.BlockSpec((tm, D_MODEL), lambda i: (i, OFF_GB // D_MODEL)),
                  rowblk(D_MODEL),
                  full((GLA_V, D_MODEL)), full((MOBA_W, D_MODEL)), full((D_MODEL, D_MODEL)),
                  full((1, D_MODEL)), full((D_MODEL, 2 * LANES)), full((1, LANES))],
        out_specs=[rowblk(D_MODEL), rowblk(D_MODEL), rowblk(LANES)],
        out_shape=[jax.ShapeDtypeStruct((T, D_MODEL), F32),
                   jax.ShapeDtypeStruct((T, D_MODEL), BF16),
                   jax.ShapeDtypeStruct((T, LANES), F32)],
        compiler_params=_params(("arbitrary",), VMEM_LIMIT),
        name="merge",
    )(o_a, o_b, proj, proj, x2, w_ua, w_ub, w_o, g_ffn, w_r2, b_r)


def _router_kernel(lg_ref, posw_ref, cnt_ref, carry_ref, cnt_scr, *, tm):
    rows = lg_ref.shape[0]

    @pl.when(pl.program_id(0) == 0)
    def _():
        cnt_scr[...] = jnp.zeros(cnt_scr.shape, F32)

    lane = lax.broadcasted_iota(jnp.int32, (rows, LANES), 1)
    lane_f = lane.astype(F32)
    work = jnp.where(lane < N_EXPERTS, lg_ref[...], NEG)
    vals, hots = [], []
    for _ in range(TOP_K):
        mx = work.max(axis=-1, keepdims=True)
        idx = jnp.min(jnp.where(work == mx, lane_f, float(LANES)), axis=-1, keepdims=True)
        hot = lane_f == idx
        vals.append(mx)
        hots.append(hot)
        work = jnp.where(hot, 2.0 * NEG, work)
    exps = [jnp.exp(v - vals[0]) for v in vals]
    den = exps[0] + exps[1] + exps[2] + exps[3]
    sel = jnp.zeros((rows, LANES), F32)
    for hot in hots:
        sel = sel + hot.astype(F32)
    row = lax.broadcasted_iota(jnp.int32, (tm, tm), 0)
    col = lax.broadcasted_iota(jnp.int32, (tm, tm), 1)
    below = (col < row).astype(BF16)
    er = lax.broadcasted_iota(jnp.int32, (LANES, LANES), 0)
    ec = lax.broadcasted_iota(jnp.int32, (LANES, LANES), 1)
    before = (er < ec).astype(F32)
    pos_parts = []
    for t in range(rows // tm):
        sel_t = sel[t * tm:(t + 1) * tm]
        local_rank = jnp.dot(below, sel_t.astype(BF16), preferred_element_type=F32)
        cnt_t = sel_t.sum(axis=0, keepdims=True)
        cnt_t = jnp.floor((cnt_t + (RUN_ALIGN - 1.0)) * (1.0 / RUN_ALIGN)) * RUN_ALIGN
        tile_off = jnp.dot(jnp.broadcast_to(cnt_t, (8, LANES)), before, preferred_element_type=F32,
                           precision=lax.Precision.HIGHEST)[0:1]
        pos_parts.append(local_rank + tile_off)
        carry_ref[t] = cnt_scr[...]
        cnt_ref[t] = cnt_t
        cnt_scr[...] = cnt_scr[...] + cnt_t
    pos_all = jnp.concatenate(pos_parts, axis=0)
    posw = jnp.zeros((rows, LANES), F32)
    for k in range(TOP_K):
        pk = jnp.sum(jnp.where(hots[k], pos_all, 0.0), axis=-1, keepdims=True)
        posw = jnp.where(lane == k, pk, posw)
        posw = jnp.where(lane == TOP_K + k, exps[k] / den, posw)
    posw_ref[...] = posw


def _router(logits, tm, tiles_per_step=4):
    T = logits.shape[0]
    nt = T // tm
    rows = tm * tiles_per_step
    tilerow = pl.BlockSpec((tiles_per_step, 1, LANES), lambda i: (i, 0, 0))
    return pl.pallas_call(
        functools.partial(_router_kernel, tm=tm),
        grid=(T // rows,),
        in_specs=[pl.BlockSpec((rows, LANES), lambda i: (i, 0))],
        out_specs=[pl.BlockSpec((rows, LANES), lambda i: (i, 0)), tilerow, tilerow],
        out_shape=[jax.ShapeDtypeStruct((T, LANES), F32),
                   jax.ShapeDtypeStruct((nt, 1, LANES), F32),
                   jax.ShapeDtypeStruct((nt, 1, LANES), F32)],
        scratch_shapes=[pltpu.VMEM((1, LANES), F32)],
        compiler_params=_params(("arbitrary",)),
        name="router",
    )(logits)


def _run_pieces(n, max_rows, fn):
    for b in reversed(range(RUN_ALIGN.bit_length() - 1, max_rows.bit_length())):
        size = 1 << b
        done = n & ~((2 << b) - 1)

        @pl.when((n & size) != 0)
        def _():
            fn(done, size)


def _aligned(i):
    return pl.multiple_of(i, RUN_ALIGN)


def _dispatch_kernel(toff_ref, eoff_ref, n_ref, zoff_ref, zn_ref, tail_ref, h2_ref, posw_ref, x_ref,
                     buf, zbuf, sems, *, nt):
    tm = h2_ref.shape[0]
    half = D_MODEL // 2
    j = pl.program_id(0)
    zrows = zbuf.shape[0]

    def tile_runs(t, act):
        slot = t % 2

        def body(e, c):
            r = t * N_EXPERTS + e
            t0, d0 = toff_ref[r], eoff_ref[r]
            _run_pieces(n_ref[r], tm, lambda done, size: act(pltpu.make_async_copy(
                buf.at[slot, pl.ds(_aligned(t0 + done), size)], x_ref.at[pl.ds(_aligned(d0 + done), size)],
                sems.at[slot])))
            return c

        lax.fori_loop(0, N_EXPERTS, body, 0)

    def zero_fill(act):
        def body(e, c):
            d0 = zoff_ref[e]
            _run_pieces(zn_ref[e], zrows, lambda done, size: act(pltpu.make_async_copy(
                zbuf.at[pl.ds(0, size)], x_ref.at[pl.ds(_aligned(d0 + done), size)], sems.at[2])))
            return c

        lax.fori_loop(0, N_EXPERTS, body, 0)

        def tail(i, c):
            act(pltpu.make_async_copy(zbuf, x_ref.at[pl.ds(pl.multiple_of(i * zrows, zrows), zrows)], sems.at[2]))
            return c

        lax.fori_loop(tail_ref[0], x_ref.shape[0] // zrows, tail, 0)

    start = lambda cp: cp.start()
    wait = lambda cp: cp.wait()

    @pl.when(j == 0)
    def _():
        zbuf[...] = jnp.zeros(zbuf.shape, zbuf.dtype)
        zero_fill(start)

    @pl.when(j >= 2)
    def _():
        tile_runs(j - 2, wait)

    posw = posw_ref[...]
    pos_t = posw.T
    sub = lax.broadcasted_iota(jnp.int32, (TILE_ROWS, tm), 0).astype(F32)
    perm = jnp.zeros((TILE_ROWS, tm), F32)
    for k in range(TOP_K):
        perm = perm + (sub == pos_t[k:k + 1, :]).astype(F32)
    xs = jnp.dot(perm.astype(BF16), h2_ref[...], preferred_element_type=F32)
    buf[j % 2] = _pack(xs[:, :half], xs[:, half:])
    tile_runs(j, start)

    @pl.when(j == nt - 1)
    def _():
        if nt >= 2:
            tile_runs(j - 1, wait)
        tile_runs(j, wait)
        zero_fill(wait)


def _dispatch(h2, posw, n_rows, tile_off, expert_off, run_n, zoff, zn, tail, tm):
    T = h2.shape[0]
    nt = T // tm
    assert n_rows % EXPERT_BLOCK == 0
    return pl.pallas_call(
        functools.partial(_dispatch_kernel, nt=nt),
        grid_spec=pltpu.PrefetchScalarGridSpec(
            num_scalar_prefetch=6,
            grid=(nt,),
            in_specs=[pl.BlockSpec((tm, D_MODEL), lambda i, *_: (i, 0)),
                      pl.BlockSpec((tm, LANES), lambda i, *_: (i, 0))],
            out_specs=pl.BlockSpec(memory_space=pl.ANY),
            scratch_shapes=[pltpu.VMEM((2, TILE_ROWS, D_MODEL // 2), jnp.uint32),
                            pltpu.VMEM((EXPERT_BLOCK, D_MODEL // 2), jnp.uint32),
                            pltpu.SemaphoreType.DMA((3,))]),
        out_shape=jax.ShapeDtypeStruct((n_rows, D_MODEL // 2), jnp.uint32),
        compiler_params=_params(("arbitrary",), VMEM_LIMIT),
        name="dispatch",
    )(tile_off, expert_off, run_n, zoff, zn, tail, h2, posw)


def _expert_kernel(be_ref, nu_ref, x_ref, wg_ref, bg_ref, wu_ref, bu_ref, wd_ref, bd_ref, y_ref,
                   wg_s, wu_s, wd_s):
    i = pl.program_id(0)
    half = D_MODEL // 2
    used = i < nu_ref[0]
    e = be_ref[i]
    prev = be_ref[jnp.maximum(i - 1, 0)]

    @pl.when(used & ((i == 0) | (e != prev)))
    def _():
        wg_s[...] = wg_ref[0].astype(BF16)
        wu_s[...] = wu_ref[0].astype(BF16)
        wd_s[...] = wd_ref[0].astype(BF16)

    @pl.when(used)
    def _():
        x_lo, x_hi = _unpack(x_ref[...])

        def proj_in(w_s, b_ref):
            return (jnp.dot(x_lo, w_s[:half, :], preferred_element_type=F32)
                    + jnp.dot(x_hi, w_s[half:, :], preferred_element_type=F32) + b_ref[0])

        gate = jnp.minimum(proj_in(wg_s, bg_ref), SWIGLU_LIMIT)
        up = jnp.clip(proj_in(wu_s, bu_ref), -SWIGLU_LIMIT, SWIGLU_LIMIT)
        glu = gate * _sigmoid(gate * SWIGLU_ALPHA)
        act = ((up + 1.0) * glu).astype(BF16)
        y = jnp.dot(act, wd_s[...], preferred_element_type=F32) + bd_ref[0]
        y_ref[...] = _pack(y[:, :half], y[:, half:])

    @pl.when(jnp.logical_not(used))
    def _():
        y_ref[...] = jnp.zeros(y_ref.shape, y_ref.dtype)


def _experts(blk_exp, n_used, x_rows, n_pad, w_g, b_g, w_u, b_u, w_d, b_d):
    M = EXPERT_BLOCK
    wspec = lambda a, b: pl.BlockSpec((1, a, b), lambda i, be, nu: (be[i], 0, 0))
    return pl.pallas_call(
        _expert_kernel,
        grid_spec=pltpu.PrefetchScalarGridSpec(
            num_scalar_prefetch=2,
            grid=(n_pad // M,),
            in_specs=[pl.BlockSpec((M, D_MODEL // 2), lambda i, be, nu: (i, 0)),
                      wspec(D_MODEL, D_FF), wspec(1, D_FF),
                      wspec(D_MODEL, D_FF), wspec(1, D_FF),
                      wspec(D_FF, D_MODEL), wspec(1, D_MODEL)],
            out_specs=pl.BlockSpec((M, D_MODEL // 2), lambda i, be, nu: (i, 0)),
            scratch_shapes=[pltpu.VMEM((D_MODEL, D_FF), BF16),
                            pltpu.VMEM((D_MODEL, D_FF), BF16),
                            pltpu.VMEM((D_FF, D_MODEL), BF16)]),
        out_shape=jax.ShapeDtypeStruct((n_pad, D_MODEL // 2), jnp.uint32),
        compiler_params=_params(("arbitrary",), VMEM_LIMIT),
        name="experts",
    )(blk_exp, n_used, x_rows, w_g, b_g, w_u, b_u, w_d, b_d)


def _final_kernel(toff_ref, eoff_ref, n_ref, trows_ref, x1_ref, posw_ref, p_ref, gpg_ref, wpg_ref, wpp_ref, gpp_ref,
                  gfin_ref, y_ref, o_ref, buf, sems, *, nt):
    tm = x1_ref.shape[0]
    j = pl.program_id(0)

    def tile_runs(t, act):
        slot = t % 2

        def body(e, c):
            r = t * N_EXPERTS + e
            t0, s0 = toff_ref[r], eoff_ref[r]
            _run_pieces(n_ref[r], tm, lambda done, size: act(pltpu.make_async_copy(
                y_ref.at[pl.ds(_aligned(s0 + done), size)], buf.at[slot, pl.ds(_aligned(t0 + done), size)],
                sems.at[slot])))
            return c

        lax.fori_loop(0, N_EXPERTS, body, 0)

    @pl.when(j == 0)
    def _():
        buf[...] = jnp.zeros(buf.shape, buf.dtype)
        tile_runs(0, lambda cp: cp.start())

    @pl.when(j + 1 < nt)
    def _():
        tile_runs(j + 1, lambda cp: cp.start())

    tile_runs(j, lambda cp: cp.wait())
    posw = posw_ref[...]
    lane = lax.broadcasted_iota(jnp.int32, (tm, TILE_ROWS), 1).astype(F32)
    comb = jnp.zeros((tm, TILE_ROWS), F32)
    for k in range(TOP_K):
        comb = comb + jnp.where(lane == posw[:, k:k + 1], posw[:, TOP_K + k:TOP_K + k + 1], 0.0)
    comb = comb.astype(BF16)
    sure = TOP_K * tm
    tail = buf[j % 2, sure:, :]
    live = lax.broadcasted_iota(jnp.int32, tail.shape, 0) < trows_ref[j] - sure
    y_lo, y_hi = _unpack(jnp.concatenate([buf[j % 2, :sure, :], jnp.where(live, tail, jnp.uint32(0))], axis=0))
    moe = jnp.concatenate([jnp.dot(comb, y_lo, preferred_element_type=F32),
                           jnp.dot(comb, y_hi, preferred_element_type=F32)], axis=-1)
    x = x1_ref[...] + moe
    pg = _sigmoid(jnp.dot(_rms(x, gpg_ref[...]).astype(BF16), wpg_ref[...], preferred_element_type=F32))
    pp = jnp.dot(p_ref[...].astype(BF16), wpp_ref[...], preferred_element_type=F32)
    x = x + pg * _rms(pp, gpp_ref[...])
    o_ref[...] = _rms(x, gfin_ref[...])


def _final(tile_off, expert_off, run_n, tile_rows, x1, y_rows, posw, p2, g_pg, w_pg, w_pp, g_pp, g_fin, tm):
    T = x1.shape[0]
    nt = T // tm
    full = lambda shape: pl.BlockSpec(shape, lambda i, *_: (0, 0))
    rowblk = lambda w: pl.BlockSpec((tm, w), lambda i, *_: (i, 0))
    return pl.pallas_call(
        functools.partial(_final_kernel, nt=nt),
        grid_spec=pltpu.PrefetchScalarGridSpec(
            num_scalar_prefetch=4,
            grid=(nt,),
            in_specs=[rowblk(D_MODEL), rowblk(LANES), rowblk(PLE_DIM),
                      full((1, D_MODEL)), full((D_MODEL, D_MODEL)), full((PLE_DIM, D_MODEL)),
                      full((1, D_MODEL)), full((1, D_MODEL)),
                      pl.BlockSpec(memory_space=pl.ANY)],
            out_specs=rowblk(D_MODEL),
            scratch_shapes=[pltpu.VMEM((2, TILE_ROWS, D_MODEL // 2), jnp.uint32),
                            pltpu.SemaphoreType.DMA((2,))]),
        out_shape=jax.ShapeDtypeStruct((T, D_MODEL), F32),
        compiler_params=_params(("arbitrary",), VMEM_LIMIT),
        name="final",
    )(tile_off, expert_off, run_n, tile_rows, x1, posw, p2, g_pg, w_pg, w_pp, g_pp, g_fin, y_rows)


def _split_w_in(w_in):
    sizes = (GLA_QK, GLA_QK, GLA_V, GLA_V, GLA_RANK, MOBA_W, MOBA_W, MOBA_W, D_MODEL, D_MODEL)
    offs = [0]
    for s in sizes:
        offs.append(offs[-1] + s)
    parts = [w_in[:, offs[i]:offs[i + 1]] for i in range(len(sizes))]
    main = jnp.concatenate(parts[:4] + parts[5:], axis=1).astype(BF16)
    alow = jnp.pad(parts[4], ((0, 0), (0, LANES - GLA_RANK))).astype(BF16)
    return main, alow


def _layer(x2, p2, bias, B, S, g_mix, w_in, w_a2, b_a2, g_gla_out, w_up_gla, w_up_moba, w_o, g_ffn, w_router,
           b_router, w_e_gate, b_e_gate, w_e_up, b_e_up, w_e_down, b_e_down, g_ple_gate, w_ple_gate, w_ple_proj,
           g_ple_proj, g_final):
    T = B * S
    row = lambda v: v.reshape(1, -1).astype(F32)
    w_main, w_alow = _split_w_in(w_in)
    colscale = jnp.ones((D_PROJ,), F32)
    colscale = colscale.at[OFF_QA:OFF_QA + GLA_QK].set(GLA_DK ** -0.5)
    colscale = colscale.at[OFF_QB:OFF_QB + MOBA_W].set(MOBA_HD ** -0.5 * LOG2E)
    w_a2p = jnp.pad(w_a2, ((0, LANES - GLA_RANK), (0, 0))).astype(BF16)
    proj, glog = _inproj(x2, row(g_mix), w_main, colscale.reshape(1, -1), w_alow, w_a2p, row(b_a2))

    o_a = _gla(proj, glog, row(g_gla_out), B, S)
    o_b = _moba(proj, bias, B, S)

    w_r = jnp.pad(w_router.astype(F32), ((0, 0), (0, LANES - N_EXPERTS)))
    w_r_hi = w_r.astype(BF16)
    w_r2 = jnp.concatenate([w_r_hi, (w_r - w_r_hi.astype(F32)).astype(BF16)], axis=1)
    b_r = jnp.pad(b_router.astype(F32), (0, LANES - N_EXPERTS)).reshape(1, -1)
    tm = TOKEN_TILE
    nt = T // tm
    x1, h2, logits = _merge(o_a, o_b, proj, x2, w_up_gla.astype(BF16), w_up_moba.astype(BF16),
                            w_o.astype(BF16), row(g_ffn), w_r2, b_r, tm)
    posw, cnt_t, carry = _router(logits, tm)

    M = EXPERT_BLOCK
    A = nt * TILE_ROWS
    n_pad = (-(-A // M)) * M + N_EXPERTS * M
    n_blk = n_pad // M
    cnt_t = cnt_t[:, 0, :N_EXPERTS].astype(jnp.int32)
    carry = carry[:, 0, :N_EXPERTS].astype(jnp.int32)
    counts = carry[-1] + cnt_t[-1]
    padded = (counts + M - 1) // M * M
    pad_end = jnp.cumsum(padded)
    pad_start = pad_end - padded
    blk_exp = jnp.minimum(jnp.sum(pad_end[None, :] <= (jnp.arange(n_blk, dtype=jnp.int32) * M)[:, None], axis=1),
                          N_EXPERTS - 1).astype(jnp.int32)
    n_used = (pad_end[-1:] // M).astype(jnp.int32)
    tile_off = (jnp.cumsum(cnt_t, axis=1) - cnt_t).reshape(-1)
    expert_off = (carry + pad_start[None, :]).reshape(-1)
    run_n = cnt_t.reshape(-1)
    x_rows = _dispatch(h2, posw, n_pad, tile_off, expert_off, run_n, pad_start + counts, padded - counts, n_used, tm)
    y_rows = _experts(blk_exp, n_used, x_rows, n_pad, w_e_gate, b_e_gate.reshape(N_EXPERTS, 1, D_FF),
                      w_e_up, b_e_up.reshape(N_EXPERTS, 1, D_FF), w_e_down,
                      b_e_down.reshape(N_EXPERTS, 1, D_MODEL))
    return _final(tile_off, expert_off, run_n, jnp.sum(cnt_t, axis=1), x1, y_rows, posw, p2, row(g_ple_gate),
                  w_ple_gate.astype(BF16), w_ple_proj.astype(BF16), row(g_ple_proj), row(g_final), tm)


def kernel(x, p, rel_bias, g_mix, w_in, w_a2, b_a2, g_gla_out, w_up_gla, w_up_moba, w_o, g_ffn, w_router, b_router,
           w_e_gate, b_e_gate, w_e_up, b_e_up, w_e_down, b_e_down, g_ple_gate, w_ple_gate, w_ple_proj, g_ple_proj,
           g_final):
    B, S, D = x.shape
    assert D == D_MODEL and S % MOBA_BLOCK == 0 and S % GLA_CHUNK == 0 and p.shape[0] == 1
    bias = _bias_tiles(rel_bias, S // MOBA_BLOCK)
    out = _layer(x.reshape(B * S, D), p[0].reshape(B * S, PLE_DIM), bias, B, S,
                 g_mix[0], w_in[0], w_a2[0], b_a2[0], g_gla_out[0], w_up_gla[0], w_up_moba[0], w_o[0], g_ffn[0],
                 w_router[0], b_router[0], w_e_gate[0], b_e_gate[0], w_e_up[0], b_e_up[0], w_e_down[0],
                 b_e_down[0], g_ple_gate[0], w_ple_gate[0], w_ple_proj[0], g_ple_proj[0], g_final)
    return out.reshape(B, S, D)
```

```python
import functools
import math

import jax
import jax.numpy as jnp
from jax import lax
from jax.experimental import pallas as pl
from jax.experimental.pallas import tpu as pltpu

F32 = jnp.float32
BF16 = jnp.bfloat16

D_MODEL = 1024
PLE_DIM = 256
GLA_HEADS = 4
GLA_DK = 128
GLA_DV = 256
GLA_RANK = 16
GLA_TAU = 16.0
GLA_QK = GLA_HEADS * GLA_DK
GLA_V = GLA_HEADS * GLA_DV
MOBA_HEADS = 8
MOBA_HD = 128
MOBA_BLOCK = 256
MOBA_TOPK = 3
MOBA_W = MOBA_HEADS * MOBA_HD
REL_BUCKETS = 32
REL_MAX_DIST = 4096
N_EXPERTS = 32
TOP_K = 4
D_FF = 1024
SWIGLU_LIMIT = 7.0
SWIGLU_ALPHA = 1.702
EPS = 1e-6

LANES = 128
NEG = -1e30
LOG2E = math.log2(math.e)
VMEM_LIMIT = 56 * 1024 * 1024

OFF_QA, OFF_KA, OFF_VA, OFF_RA = 0, 512, 1024, 2048
OFF_QB, OFF_KB, OFF_VB, OFF_GA, OFF_GB = 3072, 4096, 5120, 6144, 7168
D_PROJ = 8192

GLA_CHUNK = 128
EXPERT_BLOCK = 512
EXPERT_SUB = 128
TOKEN_TILE = 256
RUN_ALIGN = 8
TILE_ROWS = TOP_K * TOKEN_TILE + N_EXPERTS * RUN_ALIGN

NT = (((1,), (1,)), ((), ()))
TN = (((0,), (0,)), ((), ()))


def _params(sem, vmem=None):
    return pltpu.CompilerParams(dimension_semantics=sem, vmem_limit_bytes=vmem)


def _rms(x, g):
    return x * lax.rsqrt(jnp.mean(x * x, axis=-1, keepdims=True) + EPS) * g


def _sigmoid(x):
    return 1.0 / (1.0 + jnp.exp(-x))


def _bias_kernel(tab_ref, bkt_ref, o_ref):
    strip = 8

    def body(s, carry):
        r0 = pl.multiple_of(s * strip, strip)
        b = bkt_ref[0, pl.ds(r0, strip), :]
        accs = [jnp.zeros(b.shape, F32) for _ in range(MOBA_HEADS)]
        for bb in range(REL_BUCKETS):
            hit = b == bb
            for h in range(MOBA_HEADS):
                accs[h] = jnp.where(hit, tab_ref[h, bb], accs[h])
        for h in range(MOBA_HEADS):
            o_ref[h, 0, pl.ds(r0, strip), :] = jnp.where(b < 0, NEG, accs[h])
        return carry

    lax.fori_loop(0, MOBA_BLOCK // strip, body, 0)


def _bias_tiles(rel_bias, nblk):
    i = jnp.arange(MOBA_BLOCK, dtype=jnp.int32)
    dist = (jnp.arange(nblk, dtype=jnp.int32)[:, None, None] * MOBA_BLOCK + i[None, :, None] - i[None, None, :])
    n = jnp.maximum(dist, 0)
    max_exact = REL_BUCKETS // 2
    nf = jnp.maximum(n, 1).astype(F32)
    large = max_exact + (jnp.log(nf / max_exact) / math.log(REL_MAX_DIST / max_exact)
                         * (REL_BUCKETS - max_exact)).astype(jnp.int32)
    large = jnp.minimum(large, REL_BUCKETS - 1)
    bkt = jnp.where(dist < 0, -1, jnp.where(n < max_exact, n, large)).astype(jnp.int32)
    tab = rel_bias.astype(F32).T * LOG2E
    return pl.pallas_call(
        _bias_kernel,
        grid=(nblk,),
        in_specs=[pl.BlockSpec(memory_space=pltpu.SMEM),
                  pl.BlockSpec((1, MOBA_BLOCK, MOBA_BLOCK), lambda m: (m, 0, 0))],
        out_specs=pl.BlockSpec((MOBA_HEADS, 1, MOBA_BLOCK, MOBA_BLOCK), lambda m: (0, m, 0, 0)),
        out_shape=jax.ShapeDtypeStruct((MOBA_HEADS, nblk, MOBA_BLOCK, MOBA_BLOCK), F32),
        compiler_params=_params(("arbitrary",)),
        name="bias_tiles",
    )(tab, bkt)


def _inproj_kernel(x0_ref, xn_ref, g_ref, w_ref, cs_ref, wal_ref, wa2_ref, ba2_ref, o_ref, glog_ref, h_scr, al_scr,
                   *, n_j):
    i, j = pl.program_id(0), pl.program_id(1)
    tm = xn_ref.shape[0]
    rs = tm // n_j

    def gate_logits(a_low):
        a = jnp.dot(a_low, wa2_ref[...], preferred_element_type=F32) + ba2_ref[...]
        log_sig = jnp.minimum(a, 0.0) - jnp.log1p(jnp.exp(-jnp.abs(a)))
        return log_sig * (1.0 / GLA_TAU)

    def slice_rows(behind):
        return pl.multiple_of(((j + n_j - behind) % n_j) * rs, rs)

    @pl.when((i == 0) & (j == 0))
    def _():
        h0 = _rms(x0_ref[...], g_ref[...]).astype(BF16)
        h_scr[0] = h0
        a_low0 = jnp.dot(h0, wal_ref[...], preferred_element_type=F32).astype(BF16)
        glog_ref[...] = gate_logits(a_low0)
        al_scr[...] = a_low0[(n_j - 2) * rs:(n_j - 1) * rs, :]

    glog_ref[pl.ds(slice_rows(2), rs), :] = gate_logits(al_scr[...])
    slot = jnp.where(j == 0, i, i + 1) % 2
    h_lag = h_scr[slot, pl.ds(slice_rows(1), rs), :]
    al_scr[...] = jnp.dot(h_lag, wal_ref[...], preferred_element_type=F32).astype(BF16)

    acc = jnp.dot(h_scr[i % 2], w_ref[...], preferred_element_type=F32)
    o_ref[...] = (acc * cs_ref[...]).astype(BF16)

    h_scr[(i + 1) % 2, pl.ds(slice_rows(0), rs), :] = _rms(xn_ref[pl.ds(slice_rows(0), rs), :],
                                                             g_ref[...]).astype(BF16)


def _inproj(x2, g_mix, w_main, colscale, w_alow, w_a2p, b_a2, tm=1024, tn=1024):
    T = x2.shape[0]
    n_i, n_j = T // tm, D_PROJ // tn
    assert n_i >= 2 and tm % (8 * n_j) == 0
    return pl.pallas_call(
        functools.partial(_inproj_kernel, n_j=n_j),
        grid=(n_i, n_j),
        in_specs=[pl.BlockSpec((tm, D_MODEL), lambda i, j: (0, 0)),
                  pl.BlockSpec((tm, D_MODEL), lambda i, j: (jnp.minimum(i + 1, n_i - 1), 0)),
                  pl.BlockSpec((1, D_MODEL), lambda i, j: (0, 0)),
                  pl.BlockSpec((D_MODEL, tn), lambda i, j: (0, j)),
                  pl.BlockSpec((1, tn), lambda i, j: (0, j)),
                  pl.BlockSpec((D_MODEL, LANES), lambda i, j: (0, 0)),
                  pl.BlockSpec((LANES, GLA_QK), lambda i, j: (0, 0)),
                  pl.BlockSpec((1, GLA_QK), lambda i, j: (0, 0))],
        out_specs=[pl.BlockSpec((tm, tn), lambda i, j: (i, j)),
                   pl.BlockSpec((tm, GLA_QK), lambda i, j: (jnp.minimum(i + jnp.minimum(j // 2, 1), n_i - 1), 0))],
        out_shape=[jax.ShapeDtypeStruct((T, D_PROJ), BF16),
                   jax.ShapeDtypeStruct((T, GLA_QK), F32)],
        scratch_shapes=[pltpu.VMEM((2, tm, D_MODEL), BF16), pltpu.VMEM((tm // n_j, LANES), BF16)],
        compiler_params=_params(("arbitrary", "arbitrary"), VMEM_LIMIT),
        name="inproj",
    )(x2, x2, g_mix, w_main, colscale, w_alow, w_a2p, b_a2)


def _gla_kernel(q_ref, k_ref, v_ref, r_ref, g_ref, gout_ref, o_ref, st_ref):
    C = GLA_CHUNK

    @pl.when(pl.program_id(1) == 0)
    def _():
        st_ref[...] = jnp.zeros(st_ref.shape, F32)

    row = lax.broadcasted_iota(jnp.int32, (C, C), 0)
    col = lax.broadcasted_iota(jnp.int32, (C, C), 1)
    causal = col <= row
    ltri = causal.astype(BF16)
    g = g_ref[...]
    g_hi = g.astype(BF16)
    g_lo = (g - g_hi.astype(F32)).astype(BF16)
    G = jnp.dot(ltri, g_hi, preferred_element_type=F32) + jnp.dot(ltri, g_lo, preferred_element_type=F32)
    mid = C // 2
    for h in range(GLA_HEADS):
        ks = slice(h * GLA_DK, (h + 1) * GLA_DK)
        vs = slice(h * GLA_DV, (h + 1) * GLA_DV)
        Gh = G[:, ks]
        qh = q_ref[:, ks].astype(F32)
        kh = k_ref[:, ks].astype(F32)
        vh = v_ref[:, vs]
        g_mid = Gh[mid:mid + 1, :]
        g_last = Gh[C - 1:C, :]
        q_in = (qh * jnp.exp(Gh)).astype(BF16)
        q_a = (qh * jnp.exp(Gh - g_mid)).astype(BF16)
        k_a = (kh * jnp.exp(g_mid - Gh)).astype(BF16)
        k_d = (kh * jnp.exp(g_last - Gh)).astype(BF16)
        A = lax.dot_general(q_a, k_a, NT, preferred_element_type=F32)
        A = jnp.where(causal, A, 0.0).astype(BF16)
        intra = jnp.dot(A, vh, preferred_element_type=F32)
        st = st_ref[h]
        inter = lax.dot_general(q_in, st.astype(BF16), NT, preferred_element_type=F32)
        o = inter + intra
        st_ref[h] = jnp.exp(g_last) * st + lax.dot_general(vh, k_d, TN, preferred_element_type=F32)
        r = r_ref[:, vs].astype(F32)
        o_ref[:, vs] = (_rms(o, gout_ref[...]) * (r * _sigmoid(r))).astype(BF16)


def _gla(proj, glog, g_gla_out, B, S):
    C = GLA_CHUNK
    nc = S // C
    return pl.pallas_call(
        _gla_kernel,
        grid=(B, nc),
        in_specs=[pl.BlockSpec((C, GLA_QK), lambda b, c: (b * nc + c, OFF_QA // GLA_QK)),
                  pl.BlockSpec((C, GLA_QK), lambda b, c: (b * nc + c, OFF_KA // GLA_QK)),
                  pl.BlockSpec((C, GLA_V), lambda b, c: (b * nc + c, OFF_VA // GLA_V)),
                  pl.BlockSpec((C, GLA_V), lambda b, c: (b * nc + c, OFF_RA // GLA_V)),
                  pl.BlockSpec((C, GLA_QK), lambda b, c: (b * nc + c, 0)),
                  pl.BlockSpec((1, GLA_DV), lambda b, c: (0, 0))],
        out_specs=pl.BlockSpec((C, GLA_V), lambda b, c: (b * nc + c, 0)),
        out_shape=jax.ShapeDtypeStruct((B * S, GLA_V), BF16),
        scratch_shapes=[pltpu.VMEM((GLA_HEADS, GLA_DV, GLA_DK), F32)],
        compiler_params=_params(("arbitrary", "arbitrary")),
        name="gla",
    )(proj, proj, proj, proj, glog, g_gla_out)


def _moba_kernel(q_ref, k_ref, v_ref, bias_ref, o_ref, ka_scr, va_scr, lg_scr, mx_scr, qa_scr, *, nblk):
    BLK, HD = MOBA_BLOCK, MOBA_HD
    S = nblk * BLK

    @pl.when((pl.program_id(0) == 0) & (pl.program_id(1) == 0))
    def _():
        blk = lax.broadcasted_iota(jnp.int32, (S, HD), 0) // BLK
        lane = lax.broadcasted_iota(jnp.int32, (S, HD), 1)
        ka_scr[:, HD:] = (lane == blk).astype(BF16)
        va_scr[:, HD:] = (lane == 0).astype(BF16)

    ka_scr[:, :HD] = k_ref[...]
    va_scr[:, :HD] = v_ref[...]
    n_plain = min(MOBA_TOPK + 1, nblk)

    def tile_id(cc, j):
        return cc * (cc + 1) // 2 + j

    def stage1(cc, q_in, keys):
        mx = None
        for j in range(cc + 1):
            lg = (lax.dot_general(q_in, keys[j * BLK:(j + 1) * BLK, :], NT, preferred_element_type=F32)
                  + bias_ref[0, cc - j])
            lg_scr[tile_id(cc, j)] = lg
            t = jnp.maximum(lg[:, :LANES], lg[:, LANES:])
            mx = t if mx is None else jnp.maximum(mx, t)
            if j == cc:
                mx_scr[cc] = mx
            yield

    def stage2(cc):
        m = mx_scr[cc].max(axis=-1, keepdims=True)
        acc = jnp.zeros((BLK, 2 * HD), F32)
        for j in range(cc + 1):
            p = jnp.exp2(lg_scr[tile_id(cc, j)] - m).astype(BF16)
            acc = acc + jnp.dot(p, va_scr[j * BLK:(j + 1) * BLK, :], preferred_element_type=F32)
            if j == cc:
                o_ref[cc * BLK:(cc + 1) * BLK, :] = (acc[:, :HD] / acc[:, HD:HD + 1]).astype(BF16)
            yield

    def drain(gen):
        for _ in gen:
            pass

    def interleave(main, side, n_main, n_side):
        side_steps = (s for g in side for s in g)
        done = 0
        for i, _ in enumerate(s for g in main for s in g):
            assert n_main > 0
            want = (i + 1) * n_side // n_main
            while done < want and next(side_steps, "end") != "end":
                done += 1
        drain(side_steps)

    plain = [stage1(cc, q_ref[cc * BLK:(cc + 1) * BLK, :], k_ref) for cc in range(n_plain)]
    if nblk > n_plain:
        next(plain[0])
        ksum = [k_ref[j * BLK:(j + 1) * BLK, :].astype(F32).reshape(BLK // 8, 8, HD).sum(axis=0).sum(
            axis=0, keepdims=True) for j in range(nblk)]
        kmean = jnp.concatenate(ksum, axis=0) * (1.0 / BLK)
        km_hi = kmean.astype(BF16)
        km_lo = (kmean - km_hi.astype(F32)).astype(BF16)
        km2 = jnp.concatenate([km_hi, km_lo], axis=0)
        pens = []
        for cc in range(n_plain, nblk):
            q = q_ref[cc * BLK:(cc + 1) * BLK, :]
            s2 = lax.dot_general(km2, q, NT, preferred_element_type=F32)
            pens.append((cc, q, s2[:nblk] + s2[nblk:]))
        for g in plain[:2]:
            drain(g)
        for cc, q, s in pens:
            ji = lax.broadcasted_iota(jnp.int32, s.shape, 0)
            cnt = jnp.zeros(s.shape, F32)
            for jp in range(cc):
                sj = s[jp:jp + 1, :]
                beats = (sj > s) | ((sj == s) & (jp < ji))
                cnt = cnt + beats.astype(F32)
            pen = jnp.where((ji < cc) & (cnt >= MOBA_TOPK), NEG, 0.0)
            pen_t = jnp.concatenate([pen, jnp.zeros((HD - nblk, BLK), F32)], axis=0).T
            qa_scr[cc - n_plain] = jnp.concatenate([q, pen_t.astype(BF16)], axis=1)
    for g in plain:
        drain(g)

    @pl.when(pl.program_id(0) >= 0)
    def _():
        late = [stage1(cc, qa_scr[cc - n_plain], ka_scr) for cc in range(n_plain, nblk)]
        early = [stage2(cc) for cc in range(n_plain)]
        interleave(late, early, tile_id(nblk, 0) - tile_id(n_plain, 0), tile_id(n_plain, 0))

    @pl.when(pl.program_id(0) >= 0)
    def _():
        for cc in range(n_plain, nblk):
            drain(stage2(cc))


def _moba(proj, bias, B, S):
    BLK = MOBA_BLOCK
    nblk = S // BLK
    H = MOBA_HEADS
    assert nblk <= MOBA_HD
    return pl.pallas_call(
        functools.partial(_moba_kernel, nblk=nblk),
        grid=(H, B),
        in_specs=[pl.BlockSpec((S, MOBA_HD), lambda h, b: (b, OFF_QB // MOBA_HD + h)),
                  pl.BlockSpec((S, MOBA_HD), lambda h, b: (b, OFF_KB // MOBA_HD + h)),
                  pl.BlockSpec((S, MOBA_HD), lambda h, b: (b, OFF_VB // MOBA_HD + h)),
                  pl.BlockSpec((1, nblk, BLK, BLK), lambda h, b: (h, 0, 0, 0))],
        out_specs=pl.BlockSpec((S, MOBA_HD), lambda h, b: (b, h)),
        out_shape=jax.ShapeDtypeStruct((B * S, MOBA_W), BF16),
        scratch_shapes=[pltpu.VMEM((S, 2 * MOBA_HD), BF16), pltpu.VMEM((S, 2 * MOBA_HD), BF16),
                        pltpu.VMEM((nblk * (nblk + 1) // 2, BLK, BLK), F32),
                        pltpu.VMEM((nblk, BLK, LANES), F32),
                        pltpu.VMEM((max(nblk - MOBA_TOPK - 1, 1), BLK, 2 * MOBA_HD), BF16)],
        compiler_params=_params(("arbitrary", "arbitrary"), VMEM_LIMIT),
        name="moba",
    )(proj, proj, proj, bias)


def _pack(lo, hi):
    lo_b = lax.bitcast_convert_type(lo.astype(BF16).astype(F32), jnp.uint32)
    hi_b = lax.bitcast_convert_type(hi.astype(BF16).astype(F32), jnp.uint32)
    return (lo_b >> 16) | (hi_b & jnp.uint32(0xFFFF0000))


def _unpack(w):
    lo = lax.bitcast_convert_type(w << 16, F32)
    hi = lax.bitcast_convert_type(w & jnp.uint32(0xFFFF0000), F32)
    return lo.astype(BF16), hi.astype(BF16)


def _merge_kernel(oa_ref, ob_ref, ga_ref, gb_ref, x_ref, wua_ref, wub_ref, wo_ref, gffn_ref, wr_ref, br_ref,
                  x1_ref, h2_ref, lg_ref):
    tm = x_ref.shape[0]
    u_a = jnp.dot(oa_ref[...], wua_ref[...], preferred_element_type=F32)
    u_b = jnp.dot(ob_ref[...], wub_ref[...], preferred_element_type=F32)
    y = _sigmoid(ga_ref[...].astype(F32)) * u_a + _sigmoid(gb_ref[...].astype(F32)) * u_b
    x1 = x_ref[...] + jnp.dot(y.astype(BF16), wo_ref[...], preferred_element_type=F32)
    x1_ref[...] = x1
    h2 = _rms(x1, gffn_ref[...])
    h_hi = h2.astype(BF16)
    h2_ref[...] = h_hi
    h_lo = (h2 - h_hi.astype(F32)).astype(BF16)
    r = jnp.dot(jnp.concatenate([h_hi, h_lo], axis=0), wr_ref[...], preferred_element_type=F32)
    lg_ref[...] = r[:tm, :LANES] + r[:tm, LANES:] + r[tm:, :LANES] + br_ref[...]


def _merge(o_a, o_b, proj, x2, w_ua, w_ub, w_o, g_ffn, w_r2, b_r, tm):
    T = x2.shape[0]
    full = lambda shape: pl.BlockSpec(shape, lambda i: (0, 0))
    rowblk = lambda w: pl.BlockSpec((tm, w), lambda i: (i, 0))
    return pl.pallas_call(
        _merge_kernel,
        grid=(T // tm,),
        in_specs=[rowblk(GLA_V), rowblk(MOBA_W),
                  pl.BlockSpec((tm, D_MODEL), lambda i: (i, OFF_GA // D_MODEL)),
                  pl.BlockSpec((tm, D_MODEL), lambda i: (i, OFF_GB // D_MODEL)),
                  rowblk(D_MODEL),
                  full((GLA_V, D_MODEL)), full((MOBA_W, D_MODEL)), full((D_MODEL, D_MODEL)),
                  full((1, D_MODEL)), full((D_MODEL, 2 * LANES)), full((1, LANES))],
        out_specs=[rowblk(D_MODEL), rowblk(D_MODEL), rowblk(LANES)],
        out_shape=[jax.ShapeDtypeStruct((T, D_MODEL), F32),
                   jax.ShapeDtypeStruct((T, D_MODEL), BF16),
                   jax.ShapeDtypeStruct((T, LANES), F32)],
        compiler_params=_params(("arbitrary",), VMEM_LIMIT),
        name="merge",
    )(o_a, o_b, proj, proj, x2, w_ua, w_ub, w_o, g_ffn, w_r2, b_r)


def _router_kernel(lg_ref, posw_ref, cnt_ref, carry_ref, cnt_scr, *, tm):
    rows = lg_ref.shape[0]

    @pl.when(pl.program_id(0) == 0)
    def _():
        cnt_scr[...] = jnp.zeros(cnt_scr.shape, F32)

    lane = lax.broadcasted_iota(jnp.int32, (rows, LANES), 1)
    lane_f = lane.astype(F32)
    work = jnp.where(lane < N_EXPERTS, lg_ref[...], NEG)
    vals, hots = [], []
    for _ in range(TOP_K):
        mx = work.max(axis=-1, keepdims=True)
        idx = jnp.min(jnp.where(work == mx, lane_f, float(LANES)), axis=-1, keepdims=True)
        hot = lane_f == idx
        vals.append(mx)
        hots.append(hot)
        work = jnp.where(hot, 2.0 * NEG, work)
    exps = [jnp.exp(v - vals[0]) for v in vals]
    den = exps[0] + exps[1] + exps[2] + exps[3]
    sel = jnp.zeros((rows, LANES), F32)
    for hot in hots:
        sel = sel + hot.astype(F32)
    row = lax.broadcasted_iota(jnp.int32, (tm, tm), 0)
    col = lax.broadcasted_iota(jnp.int32, (tm, tm), 1)
    below = (col < row).astype(BF16)
    er = lax.broadcasted_iota(jnp.int32, (LANES, LANES), 0)
    ec = lax.broadcasted_iota(jnp.int32, (LANES, LANES), 1)
    before = (er < ec).astype(F32)
    pos_parts = []
    for t in range(rows // tm):
        sel_t = sel[t * tm:(t + 1) * tm]
        local_rank = jnp.dot(below, sel_t.astype(BF16), preferred_element_type=F32)
        cnt_t = sel_t.sum(axis=0, keepdims=True)
        cnt_t = jnp.floor((cnt_t + (RUN_ALIGN - 1.0)) * (1.0 / RUN_ALIGN)) * RUN_ALIGN
        tile_off = jnp.dot(jnp.broadcast_to(cnt_t, (8, LANES)), before, preferred_element_type=F32,
                           precision=lax.Precision.HIGHEST)[0:1]
        pos_parts.append(local_rank + tile_off)
        carry_ref[t] = cnt_scr[...]
        cnt_ref[t] = cnt_t
        cnt_scr[...] = cnt_scr[...] + cnt_t
    pos_all = jnp.concatenate(pos_parts, axis=0)
    posw = jnp.zeros((rows, LANES), F32)
    for k in range(TOP_K):
        pk = jnp.sum(jnp.where(hots[k], pos_all, 0.0), axis=-1, keepdims=True)
        posw = jnp.where(lane == k, pk, posw)
        posw = jnp.where(lane == TOP_K + k, exps[k] / den, posw)
    posw_ref[...] = posw


def _router(logits, tm, tiles_per_step=4):
    T = logits.shape[0]
    nt = T // tm
    rows = tm * tiles_per_step
    tilerow = pl.BlockSpec((tiles_per_step, 1, LANES), lambda i: (i, 0, 0))
    return pl.pallas_call(
        functools.partial(_router_kernel, tm=tm),
        grid=(T // rows,),
        in_specs=[pl.BlockSpec((rows, LANES), lambda i: (i, 0))],
        out_specs=[pl.BlockSpec((rows, LANES), lambda i: (i, 0)), tilerow, tilerow],
        out_shape=[jax.ShapeDtypeStruct((T, LANES), F32),
                   jax.ShapeDtypeStruct((nt, 1, LANES), F32),
                   jax.ShapeDtypeStruct((nt, 1, LANES), F32)],
        scratch_shapes=[pltpu.VMEM((1, LANES), F32)],
        compiler_params=_params(("arbitrary",)),
        name="router",
    )(logits)


def _run_pieces(n, max_rows, fn):
    for b in reversed(range(RUN_ALIGN.bit_length() - 1, max_rows.bit_length())):
        size = 1 << b
        done = n & ~((2 << b) - 1)

        @pl.when((n & size) != 0)
        def _():
            fn(done, size)


def _aligned(i):
    return pl.multiple_of(i, RUN_ALIGN)


def _dispatch_kernel(toff_ref, eoff_ref, n_ref, zoff_ref, zn_ref, tail_ref, h2_ref, posw_ref, x_ref,
                     buf, zbuf, sems, *, nt):
    tm = h2_ref.shape[0]
    half = D_MODEL // 2
    j = pl.program_id(0)
    zrows = zbuf.shape[0]

    def tile_runs(t, act):
        slot = t % 2

        def body(e, c):
            r = t * N_EXPERTS + e
            t0, d0 = toff_ref[r], eoff_ref[r]
            _run_pieces(n_ref[r], tm, lambda done, size: act(pltpu.make_async_copy(
                buf.at[slot, pl.ds(_aligned(t0 + done), size)], x_ref.at[pl.ds(_aligned(d0 + done), size)],
                sems.at[slot])))
            return c

        lax.fori_loop(0, N_EXPERTS, body, 0)

    def zero_fill(act):
        def body(e, c):
            d0 = zoff_ref[e]
            _run_pieces(zn_ref[e], zrows, lambda done, size: act(pltpu.make_async_copy(
                zbuf.at[pl.ds(0, size)], x_ref.at[pl.ds(_aligned(d0 + done), size)], sems.at[2])))
            return c

        lax.fori_loop(0, N_EXPERTS, body, 0)

        def tail(i, c):
            act(pltpu.make_async_copy(zbuf, x_ref.at[pl.ds(pl.multiple_of(i * zrows, zrows), zrows)], sems.at[2]))
            return c

        lax.fori_loop(tail_ref[0], x_ref.shape[0] // zrows, tail, 0)

    start = lambda cp: cp.start()
    wait = lambda cp: cp.wait()

    @pl.when(j == 0)
    def _():
        zbuf[...] = jnp.zeros(zbuf.shape, zbuf.dtype)
        zero_fill(start)

    @pl.when(j >= 2)
    def _():
        tile_runs(j - 2, wait)

    posw = posw_ref[...]
    pos_t = posw.T
    sub = lax.broadcasted_iota(jnp.int32, (TILE_ROWS, tm), 0).astype(F32)
    perm = jnp.zeros((TILE_ROWS, tm), F32)
    for k in range(TOP_K):
        perm = perm + (sub == pos_t[k:k + 1, :]).astype(F32)
    xs = jnp.dot(perm.astype(BF16), h2_ref[...], preferred_element_type=F32)
    buf[j % 2] = _pack(xs[:, :half], xs[:, half:])
    tile_runs(j, start)

    @pl.when(j == nt - 1)
    def _():
        if nt >= 2:
            tile_runs(j - 1, wait)
        tile_runs(j, wait)
        zero_fill(wait)


def _dispatch(h2, posw, n_rows, tile_off, expert_off, run_n, zoff, zn, tail, tm):
    T = h2.shape[0]
    nt = T // tm
    assert n_rows % EXPERT_BLOCK == 0
    return pl.pallas_call(
        functools.partial(_dispatch_kernel, nt=nt),
        grid_spec=pltpu.PrefetchScalarGridSpec(
            num_scalar_prefetch=6,
            grid=(nt,),
            in_specs=[pl.BlockSpec((tm, D_MODEL), lambda i, *_: (i, 0)),
                      pl.BlockSpec((tm, LANES), lambda i, *_: (i, 0))],
            out_specs=pl.BlockSpec(memory_space=pl.ANY),
            scratch_shapes=[pltpu.VMEM((2, TILE_ROWS, D_MODEL // 2), jnp.uint32),
                            pltpu.VMEM((EXPERT_BLOCK, D_MODEL // 2), jnp.uint32),
                            pltpu.SemaphoreType.DMA((3,))]),
        out_shape=jax.ShapeDtypeStruct((n_rows, D_MODEL // 2), jnp.uint32),
        compiler_params=_params(("arbitrary",), VMEM_LIMIT),
        name="dispatch",
    )(tile_off, expert_off, run_n, zoff, zn, tail, h2, posw)


def _expert_kernel(be_ref, rows_ref, x_ref, wg_ref, bg_ref, wu_ref, bu_ref, wd_ref, bd_ref, y_ref,
                   wg_s, wu_s, wd_s):
    i = pl.program_id(0)
    half = D_MODEL // 2
    M = x_ref.shape[0]
    rows = rows_ref[i]
    e = be_ref[i]
    prev = be_ref[jnp.maximum(i - 1, 0)]

    @pl.when((rows > 0) & ((i == 0) | (e != prev)))
    def _():
        wg_s[...] = wg_ref[0].astype(BF16)
        wu_s[...] = wu_ref[0].astype(BF16)
        wd_s[...] = wd_ref[0].astype(BF16)

    def compute(r):
        x_lo, x_hi = _unpack(x_ref[:r, :])

        def proj_in(w_s, b_ref):
            return (jnp.dot(x_lo, w_s[:half, :], preferred_element_type=F32)
                    + jnp.dot(x_hi, w_s[half:, :], preferred_element_type=F32) + b_ref[0])

        gate = jnp.minimum(proj_in(wg_s, bg_ref), SWIGLU_LIMIT)
        up = jnp.clip(proj_in(wu_s, bu_ref), -SWIGLU_LIMIT, SWIGLU_LIMIT)
        glu = gate * _sigmoid(gate * SWIGLU_ALPHA)
        act = ((up + 1.0) * glu).astype(BF16)
        y = jnp.dot(act, wd_s[...], preferred_element_type=F32) + bd_ref[0]
        y_ref[:r, :] = _pack(y[:, :half], y[:, half:])
        if r < M:
            y_ref[r:, :] = jnp.zeros((M - r, half), y_ref.dtype)

    for r in range(EXPERT_SUB, M + 1, EXPERT_SUB):
        pl.when(rows == r)(functools.partial(compute, r))

    @pl.when(rows == 0)
    def _():
        y_ref[...] = jnp.zeros(y_ref.shape, y_ref.dtype)


def _experts(blk_exp, blk_rows, x_rows, n_pad, w_g, b_g, w_u, b_u, w_d, b_d):
    M = EXPERT_BLOCK
    wspec = lambda a, b: pl.BlockSpec((1, a, b), lambda i, be, nu: (be[i], 0, 0))
    return pl.pallas_call(
        _expert_kernel,
        grid_spec=pltpu.PrefetchScalarGridSpec(
            num_scalar_prefetch=2,
            grid=(n_pad // M,),
            in_specs=[pl.BlockSpec((M, D_MODEL // 2), lambda i, be, nu: (i, 0)),
                      wspec(D_MODEL, D_FF), wspec(1, D_FF),
                      wspec(D_MODEL, D_FF), wspec(1, D_FF),
                      wspec(D_FF, D_MODEL), wspec(1, D_MODEL)],
            out_specs=pl.BlockSpec((M, D_MODEL // 2), lambda i, be, nu: (i, 0)),
            scratch_shapes=[pltpu.VMEM((D_MODEL, D_FF), BF16),
                            pltpu.VMEM((D_MODEL, D_FF), BF16),
                            pltpu.VMEM((D_FF, D_MODEL), BF16)]),
        out_shape=jax.ShapeDtypeStruct((n_pad, D_MODEL // 2), jnp.uint32),
        compiler_params=_params(("arbitrary",), VMEM_LIMIT),
        name="experts",
    )(blk_exp, blk_rows, x_rows, w_g, b_g, w_u, b_u, w_d, b_d)


def _final_kernel(toff_ref, eoff_ref, n_ref, trows_ref, x1_ref, posw_ref, p_ref, gpg_ref, wpg_ref, wpp_ref, gpp_ref,
                  gfin_ref, y_ref, o_ref, buf, sems, *, nt):
    tm = x1_ref.shape[0]
    j = pl.program_id(0)

    def tile_runs(t, act):
        slot = t % 2

        def body(e, c):
            r = t * N_EXPERTS + e
            t0, s0 = toff_ref[r], eoff_ref[r]
            _run_pieces(n_ref[r], tm, lambda done, size: act(pltpu.make_async_copy(
                y_ref.at[pl.ds(_aligned(s0 + done), size)], buf.at[slot, pl.ds(_aligned(t0 + done), size)],
                sems.at[slot])))
            return c

        lax.fori_loop(0, N_EXPERTS, body, 0)

    @pl.when(j == 0)
    def _():
        buf[...] = jnp.zeros(buf.shape, buf.dtype)
        tile_runs(0, lambda cp: cp.start())

    @pl.when(j + 1 < nt)
    def _():
        tile_runs(j + 1, lambda cp: cp.start())

    tile_runs(j, lambda cp: cp.wait())
    posw = posw_ref[...]
    lane = lax.broadcasted_iota(jnp.int32, (tm, TILE_ROWS), 1).astype(F32)
    comb = jnp.zeros((tm, TILE_ROWS), F32)
    for k in range(TOP_K):
        comb = comb + jnp.where(lane == posw[:, k:k + 1], posw[:, TOP_K + k:TOP_K + k + 1], 0.0)
    comb = comb.astype(BF16)
    sure = TOP_K * tm
    tail = buf[j % 2, sure:, :]
    live = lax.broadcasted_iota(jnp.int32, tail.shape, 0) < trows_ref[j] - sure
    y_lo, y_hi = _unpack(jnp.concatenate([buf[j % 2, :sure, :], jnp.where(live, tail, jnp.uint32(0))], axis=0))
    moe = jnp.concatenate([jnp.dot(comb, y_lo, preferred_element_type=F32),
                           jnp.dot(comb, y_hi, preferred_element_type=F32)], axis=-1)
    x = x1_ref[...] + moe
    pg = _sigmoid(jnp.dot(_rms(x, gpg_ref[...]).astype(BF16), wpg_ref[...], preferred_element_type=F32))
    pp = jnp.dot(p_ref[...].astype(BF16), wpp_ref[...], preferred_element_type=F32)
    x = x + pg * _rms(pp, gpp_ref[...])
    o_ref[...] = _rms(x, gfin_ref[...])


def _final(tile_off, expert_off, run_n, tile_rows, x1, y_rows, posw, p2, g_pg, w_pg, w_pp, g_pp, g_fin, tm):
    T = x1.shape[0]
    nt = T // tm
    full = lambda shape: pl.BlockSpec(shape, lambda i, *_: (0, 0))
    rowblk = lambda w: pl.BlockSpec((tm, w), lambda i, *_: (i, 0))
    return pl.pallas_call(
        functools.partial(_final_kernel, nt=nt),
        grid_spec=pltpu.PrefetchScalarGridSpec(
            num_scalar_prefetch=4,
            grid=(nt,),
            in_specs=[rowblk(D_MODEL), rowblk(LANES), rowblk(PLE_DIM),
                      full((1, D_MODEL)), full((D_MODEL, D_MODEL)), full((PLE_DIM, D_MODEL)),
                      full((1, D_MODEL)), full((1, D_MODEL)),
                      pl.BlockSpec(memory_space=pl.ANY)],
            out_specs=rowblk(D_MODEL),
            scratch_shapes=[pltpu.VMEM((2, TILE_ROWS, D_MODEL // 2), jnp.uint32),
                            pltpu.SemaphoreType.DMA((2,))]),
        out_shape=jax.ShapeDtypeStruct((T, D_MODEL), F32),
        compiler_params=_params(("arbitrary",), VMEM_LIMIT),
        name="final",
    )(tile_off, expert_off, run_n, tile_rows, x1, posw, p2, g_pg, w_pg, w_pp, g_pp, g_fin, y_rows)


def _split_w_in(w_in):
    sizes = (GLA_QK, GLA_QK, GLA_V, GLA_V, GLA_RANK, MOBA_W, MOBA_W, MOBA_W, D_MODEL, D_MODEL)
    offs = [0]
    for s in sizes:
        offs.append(offs[-1] + s)
    parts = [w_in[:, offs[i]:offs[i + 1]] for i in range(len(sizes))]
    main = jnp.concatenate(parts[:4] + parts[5:], axis=1).astype(BF16)
    alow = jnp.pad(parts[4], ((0, 0), (0, LANES - GLA_RANK))).astype(BF16)
    return main, alow


def _layer(x2, p2, bias, B, S, g_mix, w_in, w_a2, b_a2, g_gla_out, w_up_gla, w_up_moba, w_o, g_ffn, w_router,
           b_router, w_e_gate, b_e_gate, w_e_up, b_e_up, w_e_down, b_e_down, g_ple_gate, w_ple_gate, w_ple_proj,
           g_ple_proj, g_final):
    T = B * S
    row = lambda v: v.reshape(1, -1).astype(F32)
    w_main, w_alow = _split_w_in(w_in)
    colscale = jnp.ones((D_PROJ,), F32)
    colscale = colscale.at[OFF_QA:OFF_QA + GLA_QK].set(GLA_DK ** -0.5)
    colscale = colscale.at[OFF_QB:OFF_QB + MOBA_W].set(MOBA_HD ** -0.5 * LOG2E)
    w_a2p = jnp.pad(w_a2, ((0, LANES - GLA_RANK), (0, 0))).astype(BF16)
    proj, glog = _inproj(x2, row(g_mix), w_main, colscale.reshape(1, -1), w_alow, w_a2p, row(b_a2))

    o_a = _gla(proj, glog, row(g_gla_out), B, S)
    o_b = _moba(proj, bias, B, S)

    w_r = jnp.pad(w_router.astype(F32), ((0, 0), (0, LANES - N_EXPERTS)))
    w_r_hi = w_r.astype(BF16)
    w_r2 = jnp.concatenate([w_r_hi, (w_r - w_r_hi.astype(F32)).astype(BF16)], axis=1)
    b_r = jnp.pad(b_router.astype(F32), (0, LANES - N_EXPERTS)).reshape(1, -1)
    tm = TOKEN_TILE
    nt = T // tm
    x1, h2, logits = _merge(o_a, o_b, proj, x2, w_up_gla.astype(BF16), w_up_moba.astype(BF16),
                            w_o.astype(BF16), row(g_ffn), w_r2, b_r, tm)
    posw, cnt_t, carry = _router(logits, tm)

    M = EXPERT_BLOCK
    A = nt * TILE_ROWS
    n_pad = (-(-A // M)) * M + N_EXPERTS * M
    n_blk = n_pad // M
    cnt_t = cnt_t[:, 0, :N_EXPERTS].astype(jnp.int32)
    carry = carry[:, 0, :N_EXPERTS].astype(jnp.int32)
    counts = carry[-1] + cnt_t[-1]
    padded = (counts + M - 1) // M * M
    pad_end = jnp.cumsum(padded)
    pad_start = pad_end - padded
    blk_exp = jnp.minimum(jnp.sum(pad_end[None, :] <= (jnp.arange(n_blk, dtype=jnp.int32) * M)[:, None], axis=1),
                          N_EXPERTS - 1).astype(jnp.int32)
    n_used = (pad_end[-1:] // M).astype(jnp.int32)
    blk_start = jnp.arange(n_blk, dtype=jnp.int32) * M
    blk_end = jnp.sum(jnp.where(blk_exp[:, None] == jnp.arange(N_EXPERTS, dtype=jnp.int32)[None, :],
                                (pad_start + counts)[None, :], 0), axis=1)
    blk_rows = jnp.clip(blk_end - blk_start, 0, M)
    blk_rows = jnp.where(blk_start < pad_end[-1], (blk_rows + EXPERT_SUB - 1) // EXPERT_SUB * EXPERT_SUB, 0)
    tile_off = (jnp.cumsum(cnt_t, axis=1) - cnt_t).reshape(-1)
    expert_off = (carry + pad_start[None, :]).reshape(-1)
    run_n = cnt_t.reshape(-1)
    x_rows = _dispatch(h2, posw, n_pad, tile_off, expert_off, run_n, pad_start + counts, padded - counts, n_used, tm)
    y_rows = _experts(blk_exp, blk_rows.astype(jnp.int32), x_rows, n_pad, w_e_gate, b_e_gate.reshape(N_EXPERTS, 1, D_FF),
                      w_e_up, b_e_up.reshape(N_EXPERTS, 1, D_FF), w_e_down,
                      b_e_down.reshape(N_EXPERTS, 1, D_MODEL))
    return _final(tile_off, expert_off, run_n, jnp.sum(cnt_t, axis=1), x1, y_rows, posw, p2, row(g_ple_gate),
                  w_ple_gate.astype(BF16), w_ple_proj.astype(BF16), row(g_ple_proj), row(g_final), tm)


def kernel(x, p, rel_bias, g_mix, w_in, w_a2, b_a2, g_gla_out, w_up_gla, w_up_moba, w_o, g_ffn, w_router, b_router,
           w_e_gate, b_e_gate, w_e_up, b_e_up, w_e_down, b_e_down, g_ple_gate, w_ple_gate, w_ple_proj, g_ple_proj,
           g_final):
    B, S, D = x.shape
    assert D == D_MODEL and S % MOBA_BLOCK == 0 and S % GLA_CHUNK == 0 and p.shape[0] == 1
    bias = _bias_tiles(rel_bias, S // MOBA_BLOCK)
    out = _layer(x.reshape(B * S, D), p[0].reshape(B * S, PLE_DIM), bias, B, S,
                 g_mix[0], w_in[0], w_a2[0], b_a2[0], g_gla_out[0], w_up_gla[0], w_up_moba[0], w_o[0], g_ffn[0],
                 w_router[0], b_router[0], w_e_gate[0], b_e_gate[0], w_e_up[0], b_e_up[0], w_e_down[0],
                 b_e_down[0], g_ple_gate[0], w_ple_gate[0], w_ple_proj[0], g_ple_proj[0], g_final)
    return out.reshape(B, S, D)
```

```python
import functools
import math

import jax
import jax.numpy as jnp
from jax import lax
from jax.experimental import pallas as pl
from jax.experimental.pallas import tpu as pltpu

F32 = jnp.float32
BF16 = jnp.bfloat16

D_MODEL = 1024
PLE_DIM = 256
GLA_HEADS = 4
GLA_DK = 128
GLA_DV = 256
GLA_RANK = 16
GLA_TAU = 16.0
GLA_QK = GLA_HEADS * GLA_DK
GLA_V = GLA_HEADS * GLA_DV
MOBA_HEADS = 8
MOBA_HD = 128
MOBA_BLOCK = 256
MOBA_TOPK = 3
MOBA_W = MOBA_HEADS * MOBA_HD
REL_BUCKETS = 32
REL_MAX_DIST = 4096
N_EXPERTS = 32
TOP_K = 4
D_FF = 1024
SWIGLU_LIMIT = 7.0
SWIGLU_ALPHA = 1.702
EPS = 1e-6

LANES = 128
NEG = -1e30
LOG2E = math.log2(math.e)
VMEM_LIMIT = 56 * 1024 * 1024

OFF_QA, OFF_KA, OFF_VA, OFF_RA = 0, 512, 1024, 2048
OFF_QB, OFF_KB, OFF_VB, OFF_GA, OFF_GB = 3072, 4096, 5120, 6144, 7168
D_PROJ = 8192

GLA_CHUNK = 128
EXPERT_BLOCK = 512
EXPERT_SUB = 128
TOKEN_TILE = 256
RUN_ALIGN = 8
TILE_ROWS = TOP_K * TOKEN_TILE + N_EXPERTS * RUN_ALIGN

NT = (((1,), (1,)), ((), ()))
TN = (((0,), (0,)), ((), ()))


def _params(sem, vmem=None):
    return pltpu.CompilerParams(dimension_semantics=sem, vmem_limit_bytes=vmem)


def _rms(x, g):
    return x * lax.rsqrt(jnp.mean(x * x, axis=-1, keepdims=True) + EPS) * g


def _sigmoid(x):
    return 1.0 / (1.0 + jnp.exp(-x))


def _bias_kernel(tab_ref, bkt_ref, o_ref):
    strip = 8

    def body(s, carry):
        r0 = pl.multiple_of(s * strip, strip)
        b = bkt_ref[0, pl.ds(r0, strip), :]
        accs = [jnp.zeros(b.shape, F32) for _ in range(MOBA_HEADS)]
        for bb in range(REL_BUCKETS):
            hit = b == bb
            for h in range(MOBA_HEADS):
                accs[h] = jnp.where(hit, tab_ref[h, bb], accs[h])
        for h in range(MOBA_HEADS):
            o_ref[h, 0, pl.ds(r0, strip), :] = jnp.where(b < 0, NEG, accs[h])
        return carry

    lax.fori_loop(0, MOBA_BLOCK // strip, body, 0)


def _bias_tiles(rel_bias, nblk):
    i = jnp.arange(MOBA_BLOCK, dtype=jnp.int32)
    dist = (jnp.arange(nblk, dtype=jnp.int32)[:, None, None] * MOBA_BLOCK + i[None, :, None] - i[None, None, :])
    n = jnp.maximum(dist, 0)
    max_exact = REL_BUCKETS // 2
    nf = jnp.maximum(n, 1).astype(F32)
    large = max_exact + (jnp.log(nf / max_exact) / math.log(REL_MAX_DIST / max_exact)
                         * (REL_BUCKETS - max_exact)).astype(jnp.int32)
    large = jnp.minimum(large, REL_BUCKETS - 1)
    bkt = jnp.where(dist < 0, -1, jnp.where(n < max_exact, n, large)).astype(jnp.int32)
    tab = rel_bias.astype(F32).T * LOG2E
    return pl.pallas_call(
        _bias_kernel,
        grid=(nblk,),
        in_specs=[pl.BlockSpec(memory_space=pltpu.SMEM),
                  pl.BlockSpec((1, MOBA_BLOCK, MOBA_BLOCK), lambda m: (m, 0, 0))],
        out_specs=pl.BlockSpec((MOBA_HEADS, 1, MOBA_BLOCK, MOBA_BLOCK), lambda m: (0, m, 0, 0)),
        out_shape=jax.ShapeDtypeStruct((MOBA_HEADS, nblk, MOBA_BLOCK, MOBA_BLOCK), F32),
        compiler_params=_params(("arbitrary",)),
        name="bias_tiles",
    )(tab, bkt)


def _inproj_kernel(x0_ref, xn_ref, g_ref, w_ref, cs_ref, wal_ref, wa2_ref, ba2_ref, o_ref, glog_ref, h_scr, al_scr,
                   *, n_j):
    i, j = pl.program_id(0), pl.program_id(1)
    tm = xn_ref.shape[0]
    rs = tm // n_j

    def gate_logits(a_low):
        a = jnp.dot(a_low, wa2_ref[...], preferred_element_type=F32) + ba2_ref[...]
        log_sig = jnp.minimum(a, 0.0) - jnp.log1p(jnp.exp(-jnp.abs(a)))
        return log_sig * (1.0 / GLA_TAU)

    def slice_rows(behind):
        return pl.multiple_of(((j + n_j - behind) % n_j) * rs, rs)

    @pl.when((i == 0) & (j == 0))
    def _():
        h0 = _rms(x0_ref[...], g_ref[...]).astype(BF16)
        h_scr[0] = h0
        a_low0 = jnp.dot(h0, wal_ref[...], preferred_element_type=F32).astype(BF16)
        glog_ref[...] = gate_logits(a_low0)
        al_scr[...] = a_low0[(n_j - 2) * rs:(n_j - 1) * rs, :]

    glog_ref[pl.ds(slice_rows(2), rs), :] = gate_logits(al_scr[...])
    slot = jnp.where(j == 0, i, i + 1) % 2
    h_lag = h_scr[slot, pl.ds(slice_rows(1), rs), :]
    al_scr[...] = jnp.dot(h_lag, wal_ref[...], preferred_element_type=F32).astype(BF16)

    acc = jnp.dot(h_scr[i % 2], w_ref[...], preferred_element_type=F32)
    o_ref[...] = (acc * cs_ref[...]).astype(BF16)

    h_scr[(i + 1) % 2, pl.ds(slice_rows(0), rs), :] = _rms(xn_ref[pl.ds(slice_rows(0), rs), :],
                                                             g_ref[...]).astype(BF16)


def _inproj(x2, g_mix, w_main, colscale, w_alow, w_a2p, b_a2, tm=1024, tn=1024):
    T = x2.shape[0]
    n_i, n_j = T // tm, D_PROJ // tn
    assert n_i >= 2 and tm % (8 * n_j) == 0
    return pl.pallas_call(
        functools.partial(_inproj_kernel, n_j=n_j),
        grid=(n_i, n_j),
        in_specs=[pl.BlockSpec((tm, D_MODEL), lambda i, j: (0, 0)),
                  pl.BlockSpec((tm, D_MODEL), lambda i, j: (jnp.minimum(i + 1, n_i - 1), 0)),
                  pl.BlockSpec((1, D_MODEL), lambda i, j: (0, 0)),
                  pl.BlockSpec((D_MODEL, tn), lambda i, j: (0, j)),
                  pl.BlockSpec((1, tn), lambda i, j: (0, j)),
                  pl.BlockSpec((D_MODEL, LANES), lambda i, j: (0, 0)),
                  pl.BlockSpec((LANES, GLA_QK), lambda i, j: (0, 0)),
                  pl.BlockSpec((1, GLA_QK), lambda i, j: (0, 0))],
        out_specs=[pl.BlockSpec((tm, tn), lambda i, j: (i, j)),
                   pl.BlockSpec((tm, GLA_QK), lambda i, j: (jnp.minimum(i + jnp.minimum(j // 2, 1), n_i - 1), 0))],
        out_shape=[jax.ShapeDtypeStruct((T, D_PROJ), BF16),
                   jax.ShapeDtypeStruct((T, GLA_QK), F32)],
        scratch_shapes=[pltpu.VMEM((2, tm, D_MODEL), BF16), pltpu.VMEM((tm // n_j, LANES), BF16)],
        compiler_params=_params(("arbitrary", "arbitrary"), VMEM_LIMIT),
        name="inproj",
    )(x2, x2, g_mix, w_main, colscale, w_alow, w_a2p, b_a2)


def _gla_kernel(q_ref, k_ref, v_ref, r_ref, g_ref, gout_ref, o_ref, st_ref):
    C = GLA_CHUNK

    @pl.when(pl.program_id(1) == 0)
    def _():
        st_ref[...] = jnp.zeros(st_ref.shape, F32)

    row = lax.broadcasted_iota(jnp.int32, (C, C), 0)
    col = lax.broadcasted_iota(jnp.int32, (C, C), 1)
    causal = col <= row
    ltri = causal.astype(BF16)
    g = g_ref[...]
    g_hi = g.astype(BF16)
    g_lo = (g - g_hi.astype(F32)).astype(BF16)
    G = jnp.dot(ltri, g_hi, preferred_element_type=F32) + jnp.dot(ltri, g_lo, preferred_element_type=F32)
    mid = C // 2
    for h in range(GLA_HEADS):
        ks = slice(h * GLA_DK, (h + 1) * GLA_DK)
        vs = slice(h * GLA_DV, (h + 1) * GLA_DV)
        Gh = G[:, ks]
        qh = q_ref[:, ks].astype(F32)
        kh = k_ref[:, ks].astype(F32)
        vh = v_ref[:, vs]
        g_mid = Gh[mid:mid + 1, :]
        g_last = Gh[C - 1:C, :]
        q_in = (qh * jnp.exp(Gh)).astype(BF16)
        q_a = (qh * jnp.exp(Gh - g_mid)).astype(BF16)
        k_a = (kh * jnp.exp(g_mid - Gh)).astype(BF16)
        k_d = (kh * jnp.exp(g_last - Gh)).astype(BF16)
        A = lax.dot_general(q_a, k_a, NT, preferred_element_type=F32)
        A = jnp.where(causal, A, 0.0).astype(BF16)
        intra = jnp.dot(A, vh, preferred_element_type=F32)
        st = st_ref[h]
        inter = lax.dot_general(q_in, st.astype(BF16), NT, preferred_element_type=F32)
        o = inter + intra
        st_ref[h] = jnp.exp(g_last) * st + lax.dot_general(vh, k_d, TN, preferred_element_type=F32)
        r = r_ref[:, vs].astype(F32)
        o_ref[:, vs] = (_rms(o, gout_ref[...]) * (r * _sigmoid(r))).astype(BF16)


def _gla(proj, glog, g_gla_out, B, S):
    C = GLA_CHUNK
    nc = S // C
    return pl.pallas_call(
        _gla_kernel,
        grid=(B, nc),
        in_specs=[pl.BlockSpec((C, GLA_QK), lambda b, c: (b * nc + c, OFF_QA // GLA_QK)),
                  pl.BlockSpec((C, GLA_QK), lambda b, c: (b * nc + c, OFF_KA // GLA_QK)),
                  pl.BlockSpec((C, GLA_V), lambda b, c: (b * nc + c, OFF_VA // GLA_V)),
                  pl.BlockSpec((C, GLA_V), lambda b, c: (b * nc + c, OFF_RA // GLA_V)),
                  pl.BlockSpec((C, GLA_QK), lambda b, c: (b * nc + c, 0)),
                  pl.BlockSpec((1, GLA_DV), lambda b, c: (0, 0))],
        out_specs=pl.BlockSpec((C, GLA_V), lambda b, c: (b * nc + c, 0)),
        out_shape=jax.ShapeDtypeStruct((B * S, GLA_V), BF16),
        scratch_shapes=[pltpu.VMEM((GLA_HEADS, GLA_DV, GLA_DK), F32)],
        compiler_params=_params(("arbitrary", "arbitrary")),
        name="gla",
    )(proj, proj, proj, proj, glog, g_gla_out)


def _moba_kernel(q_ref, k_ref, v_ref, bias_ref, o_ref, ka_scr, va_scr, lg_scr, mx_scr, qa_scr, *, nblk):
    BLK, HD = MOBA_BLOCK, MOBA_HD
    S = nblk * BLK

    @pl.when((pl.program_id(0) == 0) & (pl.program_id(1) == 0))
    def _():
        blk = lax.broadcasted_iota(jnp.int32, (S, HD), 0) // BLK
        lane = lax.broadcasted_iota(jnp.int32, (S, HD), 1)
        ka_scr[:, HD:] = (lane == blk).astype(BF16)
        va_scr[:, HD:] = (lane == 0).astype(BF16)

    ka_scr[:, :HD] = k_ref[...]
    va_scr[:, :HD] = v_ref[...]
    n_plain = min(MOBA_TOPK + 1, nblk)

    def tile_id(cc, j):
        return cc * (cc + 1) // 2 + j

    def stage1(cc, q_in, keys):
        mx = None
        for j in range(cc + 1):
            lg = (lax.dot_general(q_in, keys[j * BLK:(j + 1) * BLK, :], NT, preferred_element_type=F32)
                  + bias_ref[0, cc - j])
            lg_scr[tile_id(cc, j)] = lg
            t = jnp.maximum(lg[:, :LANES], lg[:, LANES:])
            mx = t if mx is None else jnp.maximum(mx, t)
            if j == cc:
                mx_scr[cc] = mx
            yield

    def stage2(cc):
        m = mx_scr[cc].max(axis=-1, keepdims=True)
        acc = jnp.zeros((BLK, 2 * HD), F32)
        for j in range(cc + 1):
            p = jnp.exp2(lg_scr[tile_id(cc, j)] - m).astype(BF16)
            acc = acc + jnp.dot(p, va_scr[j * BLK:(j + 1) * BLK, :], preferred_element_type=F32)
            if j == cc:
                o_ref[cc * BLK:(cc + 1) * BLK, :] = (acc[:, :HD] / acc[:, HD:HD + 1]).astype(BF16)
            yield

    def drain(gen):
        for _ in gen:
            pass

    def interleave(main, side, n_main, n_side):
        side_steps = (s for g in side for s in g)
        done = 0
        for i, _ in enumerate(s for g in main for s in g):
            assert n_main > 0
            want = (i + 1) * n_side // n_main
            while done < want and next(side_steps, "end") != "end":
                done += 1
        drain(side_steps)

    plain = [stage1(cc, q_ref[cc * BLK:(cc + 1) * BLK, :], k_ref) for cc in range(n_plain)]
    if nblk > n_plain:
        next(plain[0])
        ksum = [k_ref[j * BLK:(j + 1) * BLK, :].astype(F32).reshape(BLK // 8, 8, HD).sum(axis=0).sum(
            axis=0, keepdims=True) for j in range(nblk)]
        kmean = jnp.concatenate(ksum, axis=0) * (1.0 / BLK)
        km_hi = kmean.astype(BF16)
        km_lo = (kmean - km_hi.astype(F32)).astype(BF16)
        km2 = jnp.concatenate([km_hi, km_lo], axis=0)
        pens = []
        for cc in range(n_plain, nblk):
            q = q_ref[cc * BLK:(cc + 1) * BLK, :]
            s2 = lax.dot_general(km2, q, NT, preferred_element_type=F32)
            pens.append((cc, q, s2[:nblk] + s2[nblk:]))
        for g in plain[:2]:
            drain(g)
        for cc, q, s in pens:
            ji = lax.broadcasted_iota(jnp.int32, s.shape, 0)
            cnt = jnp.zeros(s.shape, F32)
            for jp in range(cc):
                sj = s[jp:jp + 1, :]
                beats = (sj > s) | ((sj == s) & (jp < ji))
                cnt = cnt + beats.astype(F32)
            pen = jnp.where((ji < cc) & (cnt >= MOBA_TOPK), NEG, 0.0)
            pen_t = jnp.concatenate([pen, jnp.zeros((HD - nblk, BLK), F32)], axis=0).T
            qa_scr[cc - n_plain] = jnp.concatenate([q, pen_t.astype(BF16)], axis=1)
    for g in plain:
        drain(g)

    @pl.when(pl.program_id(0) >= 0)
    def _():
        late = [stage1(cc, qa_scr[cc - n_plain], ka_scr) for cc in range(n_plain, nblk)]
        early = [stage2(cc) for cc in range(n_plain)]
        interleave(late, early, tile_id(nblk, 0) - tile_id(n_plain, 0), tile_id(n_plain, 0))

    @pl.when(pl.program_id(0) >= 0)
    def _():
        for cc in range(n_plain, nblk):
            drain(stage2(cc))


def _moba(proj, bias, B, S):
    BLK = MOBA_BLOCK
    nblk = S // BLK
    H = MOBA_HEADS
    assert nblk <= MOBA_HD
    return pl.pallas_call(
        functools.partial(_moba_kernel, nblk=nblk),
        grid=(H, B),
        in_specs=[pl.BlockSpec((S, MOBA_HD), lambda h, b: (b, OFF_QB // MOBA_HD + h)),
                  pl.BlockSpec((S, MOBA_HD), lambda h, b: (b, OFF_KB // MOBA_HD + h)),
                  pl.BlockSpec((S, MOBA_HD), lambda h, b: (b, OFF_VB // MOBA_HD + h)),
                  pl.BlockSpec((1, nblk, BLK, BLK), lambda h, b: (h, 0, 0, 0))],
        out_specs=pl.BlockSpec((S, MOBA_HD), lambda h, b: (b, h)),
        out_shape=jax.ShapeDtypeStruct((B * S, MOBA_W), BF16),
        scratch_shapes=[pltpu.VMEM((S, 2 * MOBA_HD), BF16), pltpu.VMEM((S, 2 * MOBA_HD), BF16),
                        pltpu.VMEM((nblk * (nblk + 1) // 2, BLK, BLK), F32),
                        pltpu.VMEM((nblk, BLK, LANES), F32),
                        pltpu.VMEM((max(nblk - MOBA_TOPK - 1, 1), BLK, 2 * MOBA_HD), BF16)],
        compiler_params=_params(("arbitrary", "arbitrary"), VMEM_LIMIT),
        name="moba",
    )(proj, proj, proj, bias)


def _pack(lo, hi):
    lo_b = lax.bitcast_convert_type(lo.astype(BF16).astype(F32), jnp.uint32)
    hi_b = lax.bitcast_convert_type(hi.astype(BF16).astype(F32), jnp.uint32)
    return (lo_b >> 16) | (hi_b & jnp.uint32(0xFFFF0000))


def _unpack(w):
    lo = lax.bitcast_convert_type(w << 16, F32)
    hi = lax.bitcast_convert_type(w & jnp.uint32(0xFFFF0000), F32)
    return lo.astype(BF16), hi.astype(BF16)


def _merge_kernel(oa_ref, ob_ref, ga_ref, gb_ref, x_ref, wua_ref, wub_ref, wo_ref, gffn_ref, wr_ref, br_ref,
                  x1_ref, h2_ref, lg_ref):
    tm = x_ref.shape[0]
    u_a = jnp.dot(oa_ref[...], wua_ref[...], preferred_element_type=F32)
    u_b = jnp.dot(ob_ref[...], wub_ref[...], preferred_element_type=F32)
    y = _sigmoid(ga_ref[...].astype(F32)) * u_a + _sigmoid(gb_ref[...].astype(F32)) * u_b
    x1 = x_ref[...] + jnp.dot(y.astype(BF16), wo_ref[...], preferred_element_type=F32)
    x1_ref[...] = x1
    h2 = _rms(x1, gffn_ref[...])
    h_hi = h2.astype(BF16)
    h2_ref[...] = h_hi
    h_lo = (h2 - h_hi.astype(F32)).astype(BF16)
    r = jnp.dot(jnp.concatenate([h_hi, h_lo], axis=0), wr_ref[...], preferred_element_type=F32)
    lg_ref[...] = r[:tm, :LANES] + r[:tm, LANES:] + r[tm:, :LANES] + br_ref[...]


def _merge(o_a, o_b, proj, x2, w_ua, w_ub, w_o, g_ffn, w_r2, b_r, tm):
    T = x2.shape[0]
    full = lambda shape: pl.BlockSpec(shape, lambda i: (0, 0))
    rowblk = lambda w: pl.BlockSpec((tm, w), lambda i: (i, 0))
    return pl.pallas_call(
        _merge_kernel,
        grid=(T // tm,),
        in_specs=[rowblk(GLA_V), rowblk(MOBA_W),
                  pl.BlockSpec((tm, D_MODEL), lambda i: (i, OFF_GA // D_MODEL)),
                  pl.BlockSpec((tm, D_MODEL), lambda i: (i, OFF_GB // D_MODEL)),
                  rowblk(D_MODEL),
                  full((GLA_V, D_MODEL)), full((MOBA_W, D_MODEL)), full((D_MODEL, D_MODEL)),
                  full((1, D_MODEL)), full((D_MODEL, 2 * LANES)), full((1, LANES))],
        out_specs=[rowblk(D_MODEL), rowblk(D_MODEL), rowblk(LANES)],
        out_shape=[jax.ShapeDtypeStruct((T, D_MODEL), F32),
                   jax.ShapeDtypeStruct((T, D_MODEL), BF16),
                   jax.ShapeDtypeStruct((T, LANES), F32)],
        compiler_params=_params(("arbitrary",), VMEM_LIMIT),
        name="merge",
    )(o_a, o_b, proj, proj, x2, w_ua, w_ub, w_o, g_ffn, w_r2, b_r)


def _router_kernel(lg_ref, posw_ref, cnt_ref, carry_ref, cnt_scr, *, tm):
    rows = lg_ref.shape[0]

    @pl.when(pl.program_id(0) == 0)
    def _():
        cnt_scr[...] = jnp.zeros(cnt_scr.shape, F32)

    lane = lax.broadcasted_iota(jnp.int32, (rows, LANES), 1)
    lane_f = lane.astype(F32)
    work = jnp.where(lane < N_EXPERTS, lg_ref[...], NEG)
    vals, hots = [], []
    for _ in range(TOP_K):
        mx = work.max(axis=-1, keepdims=True)
        idx = jnp.min(jnp.where(work == mx, lane_f, float(LANES)), axis=-1, keepdims=True)
        hot = lane_f == idx
        vals.append(mx)
        hots.append(hot)
        work = jnp.where(hot, 2.0 * NEG, work)
    exps = [jnp.exp(v - vals[0]) for v in vals]
    den = exps[0] + exps[1] + exps[2] + exps[3]
    sel = jnp.zeros((rows, LANES), F32)
    for hot in hots:
        sel = sel + hot.astype(F32)
    row = lax.broadcasted_iota(jnp.int32, (tm, tm), 0)
    col = lax.broadcasted_iota(jnp.int32, (tm, tm), 1)
    below = (col < row).astype(BF16)
    er = lax.broadcasted_iota(jnp.int32, (LANES, LANES), 0)
    ec = lax.broadcasted_iota(jnp.int32, (LANES, LANES), 1)
    before = (er < ec).astype(F32)
    pos_parts = []
    for t in range(rows // tm):
        sel_t = sel[t * tm:(t + 1) * tm]
        local_rank = jnp.dot(below, sel_t.astype(BF16), preferred_element_type=F32)
        cnt_t = sel_t.sum(axis=0, keepdims=True)
        cnt_t = jnp.floor((cnt_t + (RUN_ALIGN - 1.0)) * (1.0 / RUN_ALIGN)) * RUN_ALIGN
        tile_off = jnp.dot(jnp.broadcast_to(cnt_t, (8, LANES)), before, preferred_element_type=F32,
                           precision=lax.Precision.HIGHEST)[0:1]
        pos_parts.append(local_rank + tile_off)
        carry_ref[t] = cnt_scr[...]
        cnt_ref[t] = cnt_t
        cnt_scr[...] = cnt_scr[...] + cnt_t
    pos_all = jnp.concatenate(pos_parts, axis=0)
    posw = jnp.zeros((rows, LANES), F32)
    for k in range(TOP_K):
        pk = jnp.sum(jnp.where(hots[k], pos_all, 0.0), axis=-1, keepdims=True)
        posw = jnp.where(lane == k, pk, posw)
        posw = jnp.where(lane == TOP_K + k, exps[k] / den, posw)
    posw_ref[...] = posw


def _router(logits, tm, tiles_per_step=4):
    T = logits.shape[0]
    nt = T // tm
    rows = tm * tiles_per_step
    tilerow = pl.BlockSpec((tiles_per_step, 1, LANES), lambda i: (i, 0, 0))
    return pl.pallas_call(
        functools.partial(_router_kernel, tm=tm),
        grid=(T // rows,),
        in_specs=[pl.BlockSpec((rows, LANES), lambda i: (i, 0))],
        out_specs=[pl.BlockSpec((rows, LANES), lambda i: (i, 0)), tilerow, tilerow],
        out_shape=[jax.ShapeDtypeStruct((T, LANES), F32),
                   jax.ShapeDtypeStruct((nt, 1, LANES), F32),
                   jax.ShapeDtypeStruct((nt, 1, LANES), F32)],
        scratch_shapes=[pltpu.VMEM((1, LANES), F32)],
        compiler_params=_params(("arbitrary",)),
        name="router",
    )(logits)


def _run_pieces(n, max_rows, fn):
    for b in reversed(range(RUN_ALIGN.bit_length() - 1, max_rows.bit_length())):
        size = 1 << b
        done = n & ~((2 << b) - 1)

        @pl.when((n & size) != 0)
        def _():
            fn(done, size)


def _aligned(i):
    return pl.multiple_of(i, RUN_ALIGN)


def _dispatch_kernel(toff_ref, eoff_ref, n_ref, zoff_ref, zn_ref, tail_ref, h2_ref, posw_ref, x_ref,
                     buf, zbuf, sems, *, nt):
    tm = h2_ref.shape[0]
    half = D_MODEL // 2
    j = pl.program_id(0)
    zrows = zbuf.shape[0]

    def tile_runs(t, act):
        slot = t % 2

        def body(e, c):
            r = t * N_EXPERTS + e
            t0, d0 = toff_ref[r], eoff_ref[r]
            _run_pieces(n_ref[r], tm, lambda done, size: act(pltpu.make_async_copy(
                buf.at[slot, pl.ds(_aligned(t0 + done), size)], x_ref.at[pl.ds(_aligned(d0 + done), size)],
                sems.at[slot])))
            return c

        lax.fori_loop(0, N_EXPERTS, body, 0)

    def zero_fill(act):
        def body(e, c):
            d0 = zoff_ref[e]
            _run_pieces(zn_ref[e], zrows, lambda done, size: act(pltpu.make_async_copy(
                zbuf.at[pl.ds(0, size)], x_ref.at[pl.ds(_aligned(d0 + done), size)], sems.at[2])))
            return c

        lax.fori_loop(0, N_EXPERTS, body, 0)

        def tail(i, c):
            act(pltpu.make_async_copy(zbuf, x_ref.at[pl.ds(pl.multiple_of(i * zrows, zrows), zrows)], sems.at[2]))
            return c

        lax.fori_loop(tail_ref[0], x_ref.shape[0] // zrows, tail, 0)

    start = lambda cp: cp.start()
    wait = lambda cp: cp.wait()

    @pl.when(j == 0)
    def _():
        zbuf[...] = jnp.zeros(zbuf.shape, zbuf.dtype)
        zero_fill(start)

    @pl.when(j >= 2)
    def _():
        tile_runs(j - 2, wait)

    posw = posw_ref[...]
    pos_t = posw.T
    sub = lax.broadcasted_iota(jnp.int32, (TILE_ROWS, tm), 0).astype(F32)
    perm = jnp.zeros((TILE_ROWS, tm), F32)
    for k in range(TOP_K):
        perm = perm + (sub == pos_t[k:k + 1, :]).astype(F32)
    xs = jnp.dot(perm.astype(BF16), h2_ref[...], preferred_element_type=F32)
    buf[j % 2] = _pack(xs[:, :half], xs[:, half:])
    tile_runs(j, start)

    @pl.when(j == nt - 1)
    def _():
        if nt >= 2:
            tile_runs(j - 1, wait)
        tile_runs(j, wait)
        zero_fill(wait)


def _dispatch(h2, posw, n_rows, tile_off, expert_off, run_n, zoff, zn, tail, tm):
    T = h2.shape[0]
    nt = T // tm
    assert n_rows % EXPERT_BLOCK == 0
    return pl.pallas_call(
        functools.partial(_dispatch_kernel, nt=nt),
        grid_spec=pltpu.PrefetchScalarGridSpec(
            num_scalar_prefetch=6,
            grid=(nt,),
            in_specs=[pl.BlockSpec((tm, D_MODEL), lambda i, *_: (i, 0)),
                      pl.BlockSpec((tm, LANES), lambda i, *_: (i, 0))],
            out_specs=pl.BlockSpec(memory_space=pl.ANY),
            scratch_shapes=[pltpu.VMEM((2, TILE_ROWS, D_MODEL // 2), jnp.uint32),
                            pltpu.VMEM((EXPERT_BLOCK, D_MODEL // 2), jnp.uint32),
                            pltpu.SemaphoreType.DMA((3,))]),
        out_shape=jax.ShapeDtypeStruct((n_rows, D_MODEL // 2), jnp.uint32),
        compiler_params=_params(("arbitrary",), VMEM_LIMIT),
        name="dispatch",
    )(tile_off, expert_off, run_n, zoff, zn, tail, h2, posw)


def _expert_kernel(be_ref, rows_ref, slot_ref, next_ref, x_ref, wg_ref, bg_ref, wu_ref, bu_ref, wd_ref, bd_ref, y_ref,
                   w_in, wg_s, wu_s, wd_s, sems):
    i = pl.program_id(0)
    half = D_MODEL // 2
    M = x_ref.shape[0]
    rows = rows_ref[i]
    e = be_ref[i]
    prev = be_ref[jnp.maximum(i - 1, 0)]

    def weight_copies(expert, slot):
        return [pltpu.make_async_copy(w_hbm.at[expert], w_in.at[slot, k], sems.at[slot, k])
                for k, w_hbm in enumerate((wg_ref, wu_ref, wd_ref))]

    @pl.when((rows > 0) & ((i == 0) | (e != prev)))
    def _():
        slot = slot_ref[i]

        @pl.when(i == 0)
        def _():
            for cp in weight_copies(e, slot):
                cp.start()

        nxt = next_ref[i]

        @pl.when(nxt >= 0)
        def _():
            for cp in weight_copies(nxt, 1 - slot):
                cp.start()

        for cp in weight_copies(e, slot):
            cp.wait()
        for k, dst in enumerate((wg_s, wu_s, wd_s)):
            dst[...] = w_in[slot, k].astype(BF16)

    def compute(r):
        x_lo, x_hi = _unpack(x_ref[:r, :])

        def proj_in(w_s, b_ref):
            return (jnp.dot(x_lo, w_s[:half, :], preferred_element_type=F32)
                    + jnp.dot(x_hi, w_s[half:, :], preferred_element_type=F32) + b_ref[0])

        gate = jnp.minimum(proj_in(wg_s, bg_ref), SWIGLU_LIMIT)
        up = jnp.clip(proj_in(wu_s, bu_ref), -SWIGLU_LIMIT, SWIGLU_LIMIT)
        glu = gate * _sigmoid(gate * SWIGLU_ALPHA)
        act = ((up + 1.0) * glu).astype(BF16)
        y = jnp.dot(act, wd_s[...], preferred_element_type=F32) + bd_ref[0]
        y_ref[:r, :] = _pack(y[:, :half], y[:, half:])
        if r < M:
            y_ref[r:, :] = jnp.zeros((M - r, half), y_ref.dtype)

    for r in range(EXPERT_SUB, M + 1, EXPERT_SUB):
        pl.when(rows == r)(functools.partial(compute, r))

    @pl.when(rows == 0)
    def _():
        y_ref[...] = jnp.zeros(y_ref.shape, y_ref.dtype)


def _experts(blk_exp, blk_rows, blk_slot, blk_next, x_rows, n_pad, w_g, b_g, w_u, b_u, w_d, b_d):
    M = EXPERT_BLOCK
    assert D_FF == D_MODEL
    bspec = lambda n: pl.BlockSpec((1, 1, n), lambda i, be, *_: (be[i], 0, 0))
    wspec = pl.BlockSpec(memory_space=pl.ANY)
    return pl.pallas_call(
        _expert_kernel,
        grid_spec=pltpu.PrefetchScalarGridSpec(
            num_scalar_prefetch=4,
            grid=(n_pad // M,),
            in_specs=[pl.BlockSpec((M, D_MODEL // 2), lambda i, *_: (i, 0)),
                      wspec, bspec(D_FF), wspec, bspec(D_FF), wspec, bspec(D_MODEL)],
            out_specs=pl.BlockSpec((M, D_MODEL // 2), lambda i, *_: (i, 0)),
            scratch_shapes=[pltpu.VMEM((2, 3, D_MODEL, D_FF), F32),
                            pltpu.VMEM((D_MODEL, D_FF), BF16),
                            pltpu.VMEM((D_MODEL, D_FF), BF16),
                            pltpu.VMEM((D_FF, D_MODEL), BF16),
                            pltpu.SemaphoreType.DMA((2, 3))]),
        out_shape=jax.ShapeDtypeStruct((n_pad, D_MODEL // 2), jnp.uint32),
        compiler_params=_params(("arbitrary",), VMEM_LIMIT),
        name="experts",
    )(blk_exp, blk_rows, blk_slot, blk_next, x_rows, w_g, b_g, w_u, b_u, w_d, b_d)


def _final_kernel(toff_ref, eoff_ref, n_ref, trows_ref, x1_ref, posw_ref, p_ref, gpg_ref, wpg_ref, wpp_ref, gpp_ref,
                  gfin_ref, y_ref, o_ref, buf, sems, *, nt):
    tm = x1_ref.shape[0]
    j = pl.program_id(0)

    def tile_runs(t, act):
        slot = t % 2

        def body(e, c):
            r = t * N_EXPERTS + e
            t0, s0 = toff_ref[r], eoff_ref[r]
            _run_pieces(n_ref[r], tm, lambda done, size: act(pltpu.make_async_copy(
                y_ref.at[pl.ds(_aligned(s0 + done), size)], buf.at[slot, pl.ds(_aligned(t0 + done), size)],
                sems.at[slot])))
            return c

        lax.fori_loop(0, N_EXPERTS, body, 0)

    @pl.when(j == 0)
    def _():
        buf[...] = jnp.zeros(buf.shape, buf.dtype)
        tile_runs(0, lambda cp: cp.start())

    @pl.when(j + 1 < nt)
    def _():
        tile_runs(j + 1, lambda cp: cp.start())

    tile_runs(j, lambda cp: cp.wait())
    posw = posw_ref[...]
    lane = lax.broadcasted_iota(jnp.int32, (tm, TILE_ROWS), 1).astype(F32)
    comb = jnp.zeros((tm, TILE_ROWS), F32)
    for k in range(TOP_K):
        comb = comb + jnp.where(lane == posw[:, k:k + 1], posw[:, TOP_K + k:TOP_K + k + 1], 0.0)
    comb = comb.astype(BF16)
    sure = TOP_K * tm
    tail = buf[j % 2, sure:, :]
    live = lax.broadcasted_iota(jnp.int32, tail.shape, 0) < trows_ref[j] - sure
    y_lo, y_hi = _unpack(jnp.concatenate([buf[j % 2, :sure, :], jnp.where(live, tail, jnp.uint32(0))], axis=0))
    moe = jnp.concatenate([jnp.dot(comb, y_lo, preferred_element_type=F32),
                           jnp.dot(comb, y_hi, preferred_element_type=F32)], axis=-1)
    x = x1_ref[...] + moe
    pg = _sigmoid(jnp.dot(_rms(x, gpg_ref[...]).astype(BF16), wpg_ref[...], preferred_element_type=F32))
    pp = jnp.dot(p_ref[...].astype(BF16), wpp_ref[...], preferred_element_type=F32)
    x = x + pg * _rms(pp, gpp_ref[...])
    o_ref[...] = _rms(x, gfin_ref[...])


def _final(tile_off, expert_off, run_n, tile_rows, x1, y_rows, posw, p2, g_pg, w_pg, w_pp, g_pp, g_fin, tm):
    T = x1.shape[0]
    nt = T // tm
    full = lambda shape: pl.BlockSpec(shape, lambda i, *_: (0, 0))
    rowblk = lambda w: pl.BlockSpec((tm, w), lambda i, *_: (i, 0))
    return pl.pallas_call(
        functools.partial(_final_kernel, nt=nt),
        grid_spec=pltpu.PrefetchScalarGridSpec(
            num_scalar_prefetch=4,
            grid=(nt,),
            in_specs=[rowblk(D_MODEL), rowblk(LANES), rowblk(PLE_DIM),
                      full((1, D_MODEL)), full((D_MODEL, D_MODEL)), full((PLE_DIM, D_MODEL)),
                      full((1, D_MODEL)), full((1, D_MODEL)),
                      pl.BlockSpec(memory_space=pl.ANY)],
            out_specs=rowblk(D_MODEL),
            scratch_shapes=[pltpu.VMEM((2, TILE_ROWS, D_MODEL // 2), jnp.uint32),
                            pltpu.SemaphoreType.DMA((2,))]),
        out_shape=jax.ShapeDtypeStruct((T, D_MODEL), F32),
        compiler_params=_params(("arbitrary",), VMEM_LIMIT),
        name="final",
    )(tile_off, expert_off, run_n, tile_rows, x1, posw, p2, g_pg, w_pg, w_pp, g_pp, g_fin, y_rows)


def _split_w_in(w_in):
    sizes = (GLA_QK, GLA_QK, GLA_V, GLA_V, GLA_RANK, MOBA_W, MOBA_W, MOBA_W, D_MODEL, D_MODEL)
    offs = [0]
    for s in sizes:
        offs.append(offs[-1] + s)
    parts = [w_in[:, offs[i]:offs[i + 1]] for i in range(len(sizes))]
    main = jnp.concatenate(parts[:4] + parts[5:], axis=1).astype(BF16)
    alow = jnp.pad(parts[4], ((0, 0), (0, LANES - GLA_RANK))).astype(BF16)
    return main, alow


def _layer(x2, p2, bias, B, S, g_mix, w_in, w_a2, b_a2, g_gla_out, w_up_gla, w_up_moba, w_o, g_ffn, w_router,
           b_router, w_e_gate, b_e_gate, w_e_up, b_e_up, w_e_down, b_e_down, g_ple_gate, w_ple_gate, w_ple_proj,
           g_ple_proj, g_final):
    T = B * S
    row = lambda v: v.reshape(1, -1).astype(F32)
    w_main, w_alow = _split_w_in(w_in)
    colscale = jnp.ones((D_PROJ,), F32)
    colscale = colscale.at[OFF_QA:OFF_QA + GLA_QK].set(GLA_DK ** -0.5)
    colscale = colscale.at[OFF_QB:OFF_QB + MOBA_W].set(MOBA_HD ** -0.5 * LOG2E)
    w_a2p = jnp.pad(w_a2, ((0, LANES - GLA_RANK), (0, 0))).astype(BF16)
    proj, glog = _inproj(x2, row(g_mix), w_main, colscale.reshape(1, -1), w_alow, w_a2p, row(b_a2))

    o_a = _gla(proj, glog, row(g_gla_out), B, S)
    o_b = _moba(proj, bias, B, S)

    w_r = jnp.pad(w_router.astype(F32), ((0, 0), (0, LANES - N_EXPERTS)))
    w_r_hi = w_r.astype(BF16)
    w_r2 = jnp.concatenate([w_r_hi, (w_r - w_r_hi.astype(F32)).astype(BF16)], axis=1)
    b_r = jnp.pad(b_router.astype(F32), (0, LANES - N_EXPERTS)).reshape(1, -1)
    tm = TOKEN_TILE
    nt = T // tm
    x1, h2, logits = _merge(o_a, o_b, proj, x2, w_up_gla.astype(BF16), w_up_moba.astype(BF16),
                            w_o.astype(BF16), row(g_ffn), w_r2, b_r, tm)
    posw, cnt_t, carry = _router(logits, tm)

    M = EXPERT_BLOCK
    A = nt * TILE_ROWS
    n_pad = (-(-A // M)) * M + N_EXPERTS * M
    n_blk = n_pad // M
    cnt_t = cnt_t[:, 0, :N_EXPERTS].astype(jnp.int32)
    carry = carry[:, 0, :N_EXPERTS].astype(jnp.int32)
    counts = carry[-1] + cnt_t[-1]
    padded = (counts + M - 1) // M * M
    pad_end = jnp.cumsum(padded)
    pad_start = pad_end - padded
    blk_exp = jnp.minimum(jnp.sum(pad_end[None, :] <= (jnp.arange(n_blk, dtype=jnp.int32) * M)[:, None], axis=1),
                          N_EXPERTS - 1).astype(jnp.int32)
    n_used = (pad_end[-1:] // M).astype(jnp.int32)
    blk_start = jnp.arange(n_blk, dtype=jnp.int32) * M
    eids = jnp.arange(N_EXPERTS, dtype=jnp.int32)

    def per_block(per_expert):
        return jnp.sum(jnp.where(blk_exp[:, None] == eids[None, :], per_expert[None, :], 0), axis=1).astype(jnp.int32)

    blk_rows = jnp.clip(per_block(pad_start + counts) - blk_start, 0, M)
    blk_rows = jnp.where(blk_start < pad_end[-1], (blk_rows + EXPERT_SUB - 1) // EXPERT_SUB * EXPERT_SUB, 0)
    has_rows = counts > 0
    blk_slot = per_block((jnp.cumsum(has_rows) - 1) % 2)
    later = jnp.where((eids[None, :] > eids[:, None]) & has_rows[None, :], eids[None, :], N_EXPERTS)
    next_exp = jnp.min(later, axis=1)
    blk_next = per_block(jnp.where(next_exp < N_EXPERTS, next_exp, -1))
    tile_off = (jnp.cumsum(cnt_t, axis=1) - cnt_t).reshape(-1)
    expert_off = (carry + pad_start[None, :]).reshape(-1)
    run_n = cnt_t.reshape(-1)
    x_rows = _dispatch(h2, posw, n_pad, tile_off, expert_off, run_n, pad_start + counts, padded - counts, n_used, tm)
    y_rows = _experts(blk_exp, blk_rows.astype(jnp.int32), blk_slot, blk_next, x_rows, n_pad,
                      w_e_gate, b_e_gate.reshape(N_EXPERTS, 1, D_FF),
                      w_e_up, b_e_up.reshape(N_EXPERTS, 1, D_FF), w_e_down,
                      b_e_down.reshape(N_EXPERTS, 1, D_MODEL))
    return _final(tile_off, expert_off, run_n, jnp.sum(cnt_t, axis=1), x1, y_rows, posw, p2, row(g_ple_gate),
                  w_ple_gate.astype(BF16), w_ple_proj.astype(BF16), row(g_ple_proj), row(g_final), tm)


def kernel(x, p, rel_bias, g_mix, w_in, w_a2, b_a2, g_gla_out, w_up_gla, w_up_moba, w_o, g_ffn, w_router, b_router,
           w_e_gate, b_e_gate, w_e_up, b_e_up, w_e_down, b_e_down, g_ple_gate, w_ple_gate, w_ple_proj, g_ple_proj,
           g_final):
    B, S, D = x.shape
    assert D == D_MODEL and S % MOBA_BLOCK == 0 and S % GLA_CHUNK == 0 and p.shape[0] == 1
    bias = _bias_tiles(rel_bias, S // MOBA_BLOCK)
    out = _layer(x.reshape(B * S, D), p[0].reshape(B * S, PLE_DIM), bias, B, S,
                 g_mix[0], w_in[0], w_a2[0], b_a2[0], g_gla_out[0], w_up_gla[0], w_up_moba[0], w_o[0], g_ffn[0],
                 w_router[0], b_router[0], w_e_gate[0], b_e_gate[0], w_e_up[0], b_e_up[0], w_e_down[0],
                 b_e_down[0], g_ple_gate[0], w_ple_gate[0], w_ple_proj[0], g_ple_proj[0], g_final)
    return out.reshape(B, S, D)
```

```python
import functools
import math

import jax
import jax.numpy as jnp
from jax import lax
from jax.experimental import pallas as pl
from jax.experimental.pallas import tpu as pltpu

F32 = jnp.float32
BF16 = jnp.bfloat16

D_MODEL = 1024
PLE_DIM = 256
GLA_HEADS = 4
GLA_DK = 128
GLA_DV = 256
GLA_RANK = 16
GLA_TAU = 16.0
GLA_QK = GLA_HEADS * GLA_DK
GLA_V = GLA_HEADS * GLA_DV
MOBA_HEADS = 8
MOBA_HD = 128
MOBA_BLOCK = 256
MOBA_TOPK = 3
MOBA_W = MOBA_HEADS * MOBA_HD
REL_BUCKETS = 32
REL_MAX_DIST = 4096
N_EXPERTS = 32
TOP_K = 4
D_FF = 1024
SWIGLU_LIMIT = 7.0
SWIGLU_ALPHA = 1.702
EPS = 1e-6

LANES = 128
NEG = -1e30
LOG2E = math.log2(math.e)
VMEM_LIMIT = 56 * 1024 * 1024

OFF_QA, OFF_KA, OFF_VA, OFF_RA = 0, 512, 1024, 2048
OFF_QB, OFF_KB, OFF_VB, OFF_GA, OFF_GB = 3072, 4096, 5120, 6144, 7168
D_PROJ = 8192

GLA_CHUNK = 128
EXPERT_BLOCK = 512
EXPERT_SUB = 128
TOKEN_TILE = 256
RUN_ALIGN = 8
TILE_ROWS = TOP_K * TOKEN_TILE + N_EXPERTS * RUN_ALIGN

NT = (((1,), (1,)), ((), ()))
TN = (((0,), (0,)), ((), ()))


def _params(sem, vmem=None):
    return pltpu.CompilerParams(dimension_semantics=sem, vmem_limit_bytes=vmem)


def _rms(x, g):
    return x * lax.rsqrt(jnp.mean(x * x, axis=-1, keepdims=True) + EPS) * g


def _sigmoid(x):
    return 1.0 / (1.0 + jnp.exp(-x))


def _bias_kernel(tab_ref, bkt_ref, o_ref):
    strip = 8

    def body(s, carry):
        r0 = pl.multiple_of(s * strip, strip)
        b = bkt_ref[0, pl.ds(r0, strip), :]
        accs = [jnp.zeros(b.shape, F32) for _ in range(MOBA_HEADS)]
        for bb in range(REL_BUCKETS):
            hit = b == bb
            for h in range(MOBA_HEADS):
                accs[h] = jnp.where(hit, tab_ref[h, bb], accs[h])
        for h in range(MOBA_HEADS):
            o_ref[h, 0, pl.ds(r0, strip), :] = jnp.where(b < 0, NEG, accs[h])
        return carry

    lax.fori_loop(0, MOBA_BLOCK // strip, body, 0)


def _bias_tiles(rel_bias, nblk):
    i = jnp.arange(MOBA_BLOCK, dtype=jnp.int32)
    dist = (jnp.arange(nblk, dtype=jnp.int32)[:, None, None] * MOBA_BLOCK + i[None, :, None] - i[None, None, :])
    n = jnp.maximum(dist, 0)
    max_exact = REL_BUCKETS // 2
    nf = jnp.maximum(n, 1).astype(F32)
    large = max_exact + (jnp.log(nf / max_exact) / math.log(REL_MAX_DIST / max_exact)
                         * (REL_BUCKETS - max_exact)).astype(jnp.int32)
    large = jnp.minimum(large, REL_BUCKETS - 1)
    bkt = jnp.where(dist < 0, -1, jnp.where(n < max_exact, n, large)).astype(jnp.int32)
    tab = rel_bias.astype(F32).T * LOG2E
    return pl.pallas_call(
        _bias_kernel,
        grid=(nblk,),
        in_specs=[pl.BlockSpec(memory_space=pltpu.SMEM),
                  pl.BlockSpec((1, MOBA_BLOCK, MOBA_BLOCK), lambda m: (m, 0, 0))],
        out_specs=pl.BlockSpec((MOBA_HEADS, 1, MOBA_BLOCK, MOBA_BLOCK), lambda m: (0, m, 0, 0)),
        out_shape=jax.ShapeDtypeStruct((MOBA_HEADS, nblk, MOBA_BLOCK, MOBA_BLOCK), F32),
        compiler_params=_params(("arbitrary",)),
        name="bias_tiles",
    )(tab, bkt)


def _inproj_kernel(x0_ref, xn_ref, g_ref, w_ref, cs_ref, wal_ref, wa2_ref, ba2_ref, o_ref, glog_ref, h_scr, al_scr,
                   *, n_j):
    i, j = pl.program_id(0), pl.program_id(1)
    tm = xn_ref.shape[0]
    rs = tm // n_j

    def gate_logits(a_low):
        a = jnp.dot(a_low, wa2_ref[...], preferred_element_type=F32) + ba2_ref[...]
        log_sig = jnp.minimum(a, 0.0) - jnp.log1p(jnp.exp(-jnp.abs(a)))
        return log_sig * (1.0 / GLA_TAU)

    def slice_rows(behind):
        return pl.multiple_of(((j + n_j - behind) % n_j) * rs, rs)

    @pl.when((i == 0) & (j == 0))
    def _():
        h0 = _rms(x0_ref[...], g_ref[...]).astype(BF16)
        h_scr[0] = h0
        a_low0 = jnp.dot(h0, wal_ref[...], preferred_element_type=F32).astype(BF16)
        glog_ref[...] = gate_logits(a_low0)
        al_scr[...] = a_low0[(n_j - 2) * rs:(n_j - 1) * rs, :]

    glog_ref[pl.ds(slice_rows(2), rs), :] = gate_logits(al_scr[...])
    slot = jnp.where(j == 0, i, i + 1) % 2
    h_lag = h_scr[slot, pl.ds(slice_rows(1), rs), :]
    al_scr[...] = jnp.dot(h_lag, wal_ref[...], preferred_element_type=F32).astype(BF16)

    acc = jnp.dot(h_scr[i % 2], w_ref[...], preferred_element_type=F32)
    o_ref[...] = (acc * cs_ref[...]).astype(BF16)

    h_scr[(i + 1) % 2, pl.ds(slice_rows(0), rs), :] = _rms(xn_ref[pl.ds(slice_rows(0), rs), :],
                                                             g_ref[...]).astype(BF16)


def _inproj(x2, g_mix, w_main, colscale, w_alow, w_a2p, b_a2, tm=1024, tn=1024):
    T = x2.shape[0]
    n_i, n_j = T // tm, D_PROJ // tn
    assert n_i >= 2 and tm % (8 * n_j) == 0
    return pl.pallas_call(
        functools.partial(_inproj_kernel, n_j=n_j),
        grid=(n_i, n_j),
        in_specs=[pl.BlockSpec((tm, D_MODEL), lambda i, j: (0, 0)),
                  pl.BlockSpec((tm, D_MODEL), lambda i, j: (jnp.minimum(i + 1, n_i - 1), 0)),
                  pl.BlockSpec((1, D_MODEL), lambda i, j: (0, 0)),
                  pl.BlockSpec((D_MODEL, tn), lambda i, j: (0, j)),
                  pl.BlockSpec((1, tn), lambda i, j: (0, j)),
                  pl.BlockSpec((D_MODEL, LANES), lambda i, j: (0, 0)),
                  pl.BlockSpec((LANES, GLA_QK), lambda i, j: (0, 0)),
                  pl.BlockSpec((1, GLA_QK), lambda i, j: (0, 0))],
        out_specs=[pl.BlockSpec((tm, tn), lambda i, j: (i, j)),
                   pl.BlockSpec((tm, GLA_QK), lambda i, j: (jnp.minimum(i + jnp.minimum(j // 2, 1), n_i - 1), 0))],
        out_shape=[jax.ShapeDtypeStruct((T, D_PROJ), BF16),
                   jax.ShapeDtypeStruct((T, GLA_QK), F32)],
        scratch_shapes=[pltpu.VMEM((2, tm, D_MODEL), BF16), pltpu.VMEM((tm // n_j, LANES), BF16)],
        compiler_params=_params(("arbitrary", "arbitrary"), VMEM_LIMIT),
        name="inproj",
    )(x2, x2, g_mix, w_main, colscale, w_alow, w_a2p, b_a2)


def _gla_kernel(q_ref, k_ref, v_ref, r_ref, g_ref, gout_ref, o_ref, st_ref):
    C = GLA_CHUNK

    @pl.when(pl.program_id(1) == 0)
    def _():
        st_ref[...] = jnp.zeros(st_ref.shape, F32)

    row = lax.broadcasted_iota(jnp.int32, (C, C), 0)
    col = lax.broadcasted_iota(jnp.int32, (C, C), 1)
    causal = col <= row
    ltri = causal.astype(BF16)
    g = g_ref[...]
    g_hi = g.astype(BF16)
    g_lo = (g - g_hi.astype(F32)).astype(BF16)
    G = jnp.dot(ltri, g_hi, preferred_element_type=F32) + jnp.dot(ltri, g_lo, preferred_element_type=F32)
    mid = C // 2
    for h in range(GLA_HEADS):
        ks = slice(h * GLA_DK, (h + 1) * GLA_DK)
        vs = slice(h * GLA_DV, (h + 1) * GLA_DV)
        Gh = G[:, ks]
        qh = q_ref[:, ks].astype(F32)
        kh = k_ref[:, ks].astype(F32)
        vh = v_ref[:, vs]
        g_mid = Gh[mid:mid + 1, :]
        g_last = Gh[C - 1:C, :]
        q_in = (qh * jnp.exp(Gh)).astype(BF16)
        q_a = (qh * jnp.exp(Gh - g_mid)).astype(BF16)
        k_a = (kh * jnp.exp(g_mid - Gh)).astype(BF16)
        k_d = (kh * jnp.exp(g_last - Gh)).astype(BF16)
        A = lax.dot_general(q_a, k_a, NT, preferred_element_type=F32)
        A = jnp.where(causal, A, 0.0).astype(BF16)
        intra = jnp.dot(A, vh, preferred_element_type=F32)
        st = st_ref[h]
        inter = lax.dot_general(q_in, st.astype(BF16), NT, preferred_element_type=F32)
        o = inter + intra
        st_ref[h] = jnp.exp(g_last) * st + lax.dot_general(vh, k_d, TN, preferred_element_type=F32)
        r = r_ref[:, vs].astype(F32)
        o_ref[:, vs] = (_rms(o, gout_ref[...]) * (r * _sigmoid(r))).astype(BF16)


def _gla(proj, glog, g_gla_out, B, S):
    C = GLA_CHUNK
    nc = S // C
    return pl.pallas_call(
        _gla_kernel,
        grid=(B, nc),
        in_specs=[pl.BlockSpec((C, GLA_QK), lambda b, c: (b * nc + c, OFF_QA // GLA_QK)),
                  pl.BlockSpec((C, GLA_QK), lambda b, c: (b * nc + c, OFF_KA // GLA_QK)),
                  pl.BlockSpec((C, GLA_V), lambda b, c: (b * nc + c, OFF_VA // GLA_V)),
                  pl.BlockSpec((C, GLA_V), lambda b, c: (b * nc + c, OFF_RA // GLA_V)),
                  pl.BlockSpec((C, GLA_QK), lambda b, c: (b * nc + c, 0)),
                  pl.BlockSpec((1, GLA_DV), lambda b, c: (0, 0))],
        out_specs=pl.BlockSpec((C, GLA_V), lambda b, c: (b * nc + c, 0)),
        out_shape=jax.ShapeDtypeStruct((B * S, GLA_V), BF16),
        scratch_shapes=[pltpu.VMEM((GLA_HEADS, GLA_DV, GLA_DK), F32)],
        compiler_params=_params(("arbitrary", "arbitrary")),
        name="gla",
    )(proj, proj, proj, proj, glog, g_gla_out)


def _moba_kernel(q_ref, k_ref, v_ref, bias_ref, o_ref, ka_scr, va_scr, lg_scr, mx_scr, qa_scr, *, nblk):
    BLK, HD = MOBA_BLOCK, MOBA_HD
    S = nblk * BLK

    @pl.when((pl.program_id(0) == 0) & (pl.program_id(1) == 0))
    def _():
        blk = lax.broadcasted_iota(jnp.int32, (S, HD), 0) // BLK
        lane = lax.broadcasted_iota(jnp.int32, (S, HD), 1)
        ka_scr[:, HD:] = (lane == blk).astype(BF16)
        va_scr[:, HD:] = (lane == 0).astype(BF16)

    ka_scr[:, :HD] = k_ref[...]
    va_scr[:, :HD] = v_ref[...]
    n_plain = min(MOBA_TOPK + 1, nblk)

    def tile_id(cc, j):
        return cc * (cc + 1) // 2 + j

    def stage1(cc, q_in, keys):
        mx = None
        for j in range(cc + 1):
            lg = (lax.dot_general(q_in, keys[j * BLK:(j + 1) * BLK, :], NT, preferred_element_type=F32)
                  + bias_ref[0, cc - j])
            lg_scr[tile_id(cc, j)] = lg
            t = jnp.maximum(lg[:, :LANES], lg[:, LANES:])
            mx = t if mx is None else jnp.maximum(mx, t)
            if j == cc:
                mx_scr[cc] = mx
            yield

    def stage2(cc):
        m = mx_scr[cc].max(axis=-1, keepdims=True)
        acc = jnp.zeros((BLK, 2 * HD), F32)
        for j in range(cc + 1):
            p = jnp.exp2(lg_scr[tile_id(cc, j)] - m).astype(BF16)
            acc = acc + jnp.dot(p, va_scr[j * BLK:(j + 1) * BLK, :], preferred_element_type=F32)
            if j == cc:
                o_ref[cc * BLK:(cc + 1) * BLK, :] = (acc[:, :HD] / acc[:, HD:HD + 1]).astype(BF16)
            yield

    def drain(gen):
        for _ in gen:
            pass

    def interleave(main, side, n_main, n_side):
        side_steps = (s for g in side for s in g)
        done = 0
        for i, _ in enumerate(s for g in main for s in g):
            assert n_main > 0
            want = (i + 1) * n_side // n_main
            while done < want and next(side_steps, "end") != "end":
                done += 1
        drain(side_steps)

    plain = [stage1(cc, q_ref[cc * BLK:(cc + 1) * BLK, :], k_ref) for cc in range(n_plain)]
    if nblk > n_plain:
        next(plain[0])
        ksum = [k_ref[j * BLK:(j + 1) * BLK, :].astype(F32).reshape(BLK // 8, 8, HD).sum(axis=0).sum(
            axis=0, keepdims=True) for j in range(nblk)]
        kmean = jnp.concatenate(ksum, axis=0) * (1.0 / BLK)
        km_hi = kmean.astype(BF16)
        km_lo = (kmean - km_hi.astype(F32)).astype(BF16)
        km2 = jnp.concatenate([km_hi, km_lo], axis=0)
        pens = []
        for cc in range(n_plain, nblk):
            q = q_ref[cc * BLK:(cc + 1) * BLK, :]
            s2 = lax.dot_general(km2, q, NT, preferred_element_type=F32)
            pens.append((cc, q, s2[:nblk] + s2[nblk:]))
        for g in plain[:2]:
            drain(g)
        for cc, q, s in pens:
            ji = lax.broadcasted_iota(jnp.int32, s.shape, 0)
            cnt = jnp.zeros(s.shape, F32)
            for jp in range(cc):
                sj = s[jp:jp + 1, :]
                beats = (sj > s) | ((sj == s) & (jp < ji))
                cnt = cnt + beats.astype(F32)
            pen = jnp.where((ji < cc) & (cnt >= MOBA_TOPK), NEG, 0.0)
            pen_t = jnp.concatenate([pen, jnp.zeros((HD - nblk, BLK), F32)], axis=0).T
            qa_scr[cc - n_plain] = jnp.concatenate([q, pen_t.astype(BF16)], axis=1)
    for g in plain:
        drain(g)

    @pl.when(pl.program_id(0) >= 0)
    def _():
        late = [stage1(cc, qa_scr[cc - n_plain], ka_scr) for cc in range(n_plain, nblk)]
        early = [stage2(cc) for cc in range(n_plain)]
        interleave(late, early, tile_id(nblk, 0) - tile_id(n_plain, 0), tile_id(n_plain, 0))

    @pl.when(pl.program_id(0) >= 0)
    def _():
        for cc in range(n_plain, nblk):
            drain(stage2(cc))


def _moba(proj, bias, B, S):
    BLK = MOBA_BLOCK
    nblk = S // BLK
    H = MOBA_HEADS
    assert nblk <= MOBA_HD
    return pl.pallas_call(
        functools.partial(_moba_kernel, nblk=nblk),
        grid=(H, B),
        in_specs=[pl.BlockSpec((S, MOBA_HD), lambda h, b: (b, OFF_QB // MOBA_HD + h)),
                  pl.BlockSpec((S, MOBA_HD), lambda h, b: (b, OFF_KB // MOBA_HD + h)),
                  pl.BlockSpec((S, MOBA_HD), lambda h, b: (b, OFF_VB // MOBA_HD + h)),
                  pl.BlockSpec((1, nblk, BLK, BLK), lambda h, b: (h, 0, 0, 0))],
        out_specs=pl.BlockSpec((S, MOBA_HD), lambda h, b: (b, h)),
        out_shape=jax.ShapeDtypeStruct((B * S, MOBA_W), BF16),
        scratch_shapes=[pltpu.VMEM((S, 2 * MOBA_HD), BF16), pltpu.VMEM((S, 2 * MOBA_HD), BF16),
                        pltpu.VMEM((nblk * (nblk + 1) // 2, BLK, BLK), F32),
                        pltpu.VMEM((nblk, BLK, LANES), F32),
                        pltpu.VMEM((max(nblk - MOBA_TOPK - 1, 1), BLK, 2 * MOBA_HD), BF16)],
        compiler_params=_params(("arbitrary", "arbitrary"), VMEM_LIMIT),
        name="moba",
    )(proj, proj, proj, bias)


def _pack(lo, hi):
    lo_b = lax.bitcast_convert_type(lo.astype(BF16).astype(F32), jnp.uint32)
    hi_b = lax.bitcast_convert_type(hi.astype(BF16).astype(F32), jnp.uint32)
    return (lo_b >> 16) | (hi_b & jnp.uint32(0xFFFF0000))


def _pack_exact(lo, hi):
    lo_b = lax.bitcast_convert_type(lo, jnp.uint32)
    hi_b = lax.bitcast_convert_type(hi, jnp.uint32)
    return (lo_b >> 16) | (hi_b & jnp.uint32(0xFFFF0000))


def _unpack(w):
    lo = lax.bitcast_convert_type(w << 16, F32)
    hi = lax.bitcast_convert_type(w & jnp.uint32(0xFFFF0000), F32)
    return lo.astype(BF16), hi.astype(BF16)


def _merge_kernel(oa_ref, ob_ref, ga_ref, gb_ref, x_ref, wua_ref, wub_ref, wo_ref, gffn_ref, wr_ref, br_ref,
                  x1_ref, h2_ref, lg_ref):
    tm = x_ref.shape[0]
    u_a = jnp.dot(oa_ref[...], wua_ref[...], preferred_element_type=F32)
    u_b = jnp.dot(ob_ref[...], wub_ref[...], preferred_element_type=F32)
    y = _sigmoid(ga_ref[...].astype(F32)) * u_a + _sigmoid(gb_ref[...].astype(F32)) * u_b
    x1 = x_ref[...] + jnp.dot(y.astype(BF16), wo_ref[...], preferred_element_type=F32)
    x1_ref[...] = x1
    h2 = _rms(x1, gffn_ref[...])
    h_hi = h2.astype(BF16)
    h2_ref[...] = h_hi
    h_lo = (h2 - h_hi.astype(F32)).astype(BF16)
    r = jnp.dot(jnp.concatenate([h_hi, h_lo], axis=0), wr_ref[...], preferred_element_type=F32)
    lg_ref[...] = r[:tm, :LANES] + r[:tm, LANES:] + r[tm:, :LANES] + br_ref[...]


def _merge(o_a, o_b, proj, x2, w_ua, w_ub, w_o, g_ffn, w_r2, b_r, tm):
    T = x2.shape[0]
    full = lambda shape: pl.BlockSpec(shape, lambda i: (0, 0))
    rowblk = lambda w: pl.BlockSpec((tm, w), lambda i: (i, 0))
    return pl.pallas_call(
        _merge_kernel,
        grid=(T // tm,),
        in_specs=[rowblk(GLA_V), rowblk(MOBA_W),
                  pl.BlockSpec((tm, D_MODEL), lambda i: (i, OFF_GA // D_MODEL)),
                  pl.BlockSpec((tm, D_MODEL), lambda i: (i, OFF_GB // D_MODEL)),
                  rowblk(D_MODEL),
                  full((GLA_V, D_MODEL)), full((MOBA_W, D_MODEL)), full((D_MODEL, D_MODEL)),
                  full((1, D_MODEL)), full((D_MODEL, 2 * LANES)), full((1, LANES))],
        out_specs=[rowblk(D_MODEL), rowblk(D_MODEL), rowblk(LANES)],
        out_shape=[jax.ShapeDtypeStruct((T, D_MODEL), F32),
                   jax.ShapeDtypeStruct((T, D_MODEL), BF16),
                   jax.ShapeDtypeStruct((T, LANES), F32)],
        compiler_params=_params(("arbitrary",), VMEM_LIMIT),
        name="merge",
    )(o_a, o_b, proj, proj, x2, w_ua, w_ub, w_o, g_ffn, w_r2, b_r)


def _router_kernel(lg_ref, posw_ref, cnt_ref, carry_ref, cnt_scr, *, tm):
    rows = lg_ref.shape[0]

    @pl.when(pl.program_id(0) == 0)
    def _():
        cnt_scr[...] = jnp.zeros(cnt_scr.shape, F32)

    lane = lax.broadcasted_iota(jnp.int32, (rows, LANES), 1)
    lane_f = lane.astype(F32)
    work = jnp.where(lane < N_EXPERTS, lg_ref[...], NEG)
    vals, hots = [], []
    for _ in range(TOP_K):
        mx = work.max(axis=-1, keepdims=True)
        idx = jnp.min(jnp.where(work == mx, lane_f, float(LANES)), axis=-1, keepdims=True)
        hot = lane_f == idx
        vals.append(mx)
        hots.append(hot)
        work = jnp.where(hot, 2.0 * NEG, work)
    exps = [jnp.exp(v - vals[0]) for v in vals]
    den = exps[0] + exps[1] + exps[2] + exps[3]
    sel = jnp.zeros((rows, LANES), F32)
    for hot in hots:
        sel = sel + hot.astype(F32)
    row = lax.broadcasted_iota(jnp.int32, (tm, tm), 0)
    col = lax.broadcasted_iota(jnp.int32, (tm, tm), 1)
    below = (col < row).astype(BF16)
    er = lax.broadcasted_iota(jnp.int32, (LANES, LANES), 0)
    ec = lax.broadcasted_iota(jnp.int32, (LANES, LANES), 1)
    before = (er < ec).astype(F32)
    pos_parts = []
    for t in range(rows // tm):
        sel_t = sel[t * tm:(t + 1) * tm]
        local_rank = jnp.dot(below, sel_t.astype(BF16), preferred_element_type=F32)
        cnt_t = sel_t.sum(axis=0, keepdims=True)
        cnt_t = jnp.floor((cnt_t + (RUN_ALIGN - 1.0)) * (1.0 / RUN_ALIGN)) * RUN_ALIGN
        tile_off = jnp.dot(jnp.broadcast_to(cnt_t, (8, LANES)), before, preferred_element_type=F32,
                           precision=lax.Precision.HIGHEST)[0:1]
        pos_parts.append(local_rank + tile_off)
        carry_ref[t] = cnt_scr[...]
        cnt_ref[t] = cnt_t
        cnt_scr[...] = cnt_scr[...] + cnt_t
    pos_all = jnp.concatenate(pos_parts, axis=0)
    posw = jnp.zeros((rows, LANES), F32)
    for k in range(TOP_K):
        pk = jnp.sum(jnp.where(hots[k], pos_all, 0.0), axis=-1, keepdims=True)
        posw = jnp.where(lane == k, pk, posw)
        posw = jnp.where(lane == TOP_K + k, exps[k] / den, posw)
    posw_ref[...] = posw


def _router(logits, tm, tiles_per_step=4):
    T = logits.shape[0]
    nt = T // tm
    rows = tm * tiles_per_step
    tilerow = pl.BlockSpec((tiles_per_step, 1, LANES), lambda i: (i, 0, 0))
    return pl.pallas_call(
        functools.partial(_router_kernel, tm=tm),
        grid=(T // rows,),
        in_specs=[pl.BlockSpec((rows, LANES), lambda i: (i, 0))],
        out_specs=[pl.BlockSpec((rows, LANES), lambda i: (i, 0)), tilerow, tilerow],
        out_shape=[jax.ShapeDtypeStruct((T, LANES), F32),
                   jax.ShapeDtypeStruct((nt, 1, LANES), F32),
                   jax.ShapeDtypeStruct((nt, 1, LANES), F32)],
        scratch_shapes=[pltpu.VMEM((1, LANES), F32)],
        compiler_params=_params(("arbitrary",)),
        name="router",
    )(logits)


def _run_pieces(n, max_rows, fn):
    for b in reversed(range(RUN_ALIGN.bit_length() - 1, max_rows.bit_length())):
        size = 1 << b
        done = n & ~((2 << b) - 1)

        @pl.when((n & size) != 0)
        def _():
            fn(done, size)


def _aligned(i):
    return pl.multiple_of(i, RUN_ALIGN)


def _dispatch_kernel(toff_ref, eoff_ref, n_ref, trows_ref, zoff_ref, zn_ref, tail_ref, h2_ref, posw_ref, x_ref,
                     buf, zbuf, sems, *, nt):
    tm = h2_ref.shape[0]
    half = D_MODEL // 2
    j = pl.program_id(0)
    zrows = zbuf.shape[0]

    def tile_runs(t, act):
        slot = t % 2

        def body(e, c):
            r = t * N_EXPERTS + e
            t0, d0 = toff_ref[r], eoff_ref[r]
            _run_pieces(n_ref[r], tm, lambda done, size: act(pltpu.make_async_copy(
                buf.at[slot, pl.ds(_aligned(t0 + done), size)], x_ref.at[pl.ds(_aligned(d0 + done), size)],
                sems.at[slot])))
            return c

        lax.fori_loop(0, N_EXPERTS, body, 0)

    def zero_fill(act):
        def body(e, c):
            d0 = zoff_ref[e]
            _run_pieces(zn_ref[e], zrows, lambda done, size: act(pltpu.make_async_copy(
                zbuf.at[pl.ds(0, size)], x_ref.at[pl.ds(_aligned(d0 + done), size)], sems.at[2])))
            return c

        lax.fori_loop(0, N_EXPERTS, body, 0)

        def tail(i, c):
            act(pltpu.make_async_copy(zbuf, x_ref.at[pl.ds(pl.multiple_of(i * zrows, zrows), zrows)], sems.at[2]))
            return c

        lax.fori_loop(tail_ref[0], x_ref.shape[0] // zrows, tail, 0)

    start = lambda cp: cp.start()
    wait = lambda cp: cp.wait()

    def wait_tile(t):
        slot = t % 2
        _run_pieces(trows_ref[t], TILE_ROWS, lambda done, size: pltpu.make_async_copy(
            buf.at[slot, pl.ds(0, size)], x_ref.at[pl.ds(0, size)], sems.at[slot]).wait())

    @pl.when(j == 0)
    def _():
        zbuf[...] = jnp.zeros(zbuf.shape, zbuf.dtype)
        zero_fill(start)

    @pl.when(j >= 2)
    def _():
        wait_tile(j - 2)

    band = 256
    assert TILE_ROWS % band == 0
    pos_t = posw_ref[...].T[:TOP_K]
    pos_a = jnp.floor(pos_t * (1.0 / band))
    pos_b = pos_t - band * pos_a
    sub = lax.broadcasted_iota(jnp.int32, (band, tm), 0).astype(F32).astype(BF16)
    bands = []
    for a in range(TILE_ROWS // band):
        want = jnp.where(pos_a == a, pos_b, -1.0).astype(BF16)
        hit = jnp.zeros((band, tm), BF16)
        for k in range(TOP_K):
            row = jnp.broadcast_to(want[k:k + 1, :], (band, tm))
            hit = hit + jnp.where(sub == row, jnp.ones((band, tm), BF16), jnp.zeros((band, tm), BF16))
        bands.append(hit)
    perm = jnp.concatenate(bands, axis=0)
    xs = jnp.dot(perm, h2_ref[...], preferred_element_type=F32)
    buf[j % 2] = _pack_exact(xs[:, :half], xs[:, half:])
    tile_runs(j, start)

    @pl.when(j == nt - 1)
    def _():
        if nt >= 2:
            wait_tile(j - 1)
        wait_tile(j)
        zero_fill(wait)


def _dispatch(h2, posw, n_rows, tile_off, expert_off, run_n, tile_rows, zoff, zn, tail, tm):
    T = h2.shape[0]
    nt = T // tm
    assert n_rows % EXPERT_BLOCK == 0
    return pl.pallas_call(
        functools.partial(_dispatch_kernel, nt=nt),
        grid_spec=pltpu.PrefetchScalarGridSpec(
            num_scalar_prefetch=7,
            grid=(nt,),
            in_specs=[pl.BlockSpec((tm, D_MODEL), lambda i, *_: (i, 0)),
                      pl.BlockSpec((tm, LANES), lambda i, *_: (i, 0))],
            out_specs=pl.BlockSpec(memory_space=pl.ANY),
            scratch_shapes=[pltpu.VMEM((2, TILE_ROWS, D_MODEL // 2), jnp.uint32),
                            pltpu.VMEM((EXPERT_BLOCK, D_MODEL // 2), jnp.uint32),
                            pltpu.SemaphoreType.DMA((3,))]),
        out_shape=jax.ShapeDtypeStruct((n_rows, D_MODEL // 2), jnp.uint32),
        compiler_params=_params(("arbitrary",), VMEM_LIMIT),
        name="dispatch",
    )(tile_off, expert_off, run_n, tile_rows, zoff, zn, tail, h2, posw)


def _expert_kernel(be_ref, rows_ref, slot_ref, next_ref, x_ref, wg_ref, bg_ref, wu_ref, bu_ref, wd_ref, bd_ref, y_ref,
                   w_in, wg_s, wu_s, wd_s, sems):
    i = pl.program_id(0)
    half = D_MODEL // 2
    M = x_ref.shape[0]
    rows = rows_ref[i]
    e = be_ref[i]
    prev = be_ref[jnp.maximum(i - 1, 0)]

    def weight_copies(expert, slot):
        return [pltpu.make_async_copy(w_hbm.at[expert], w_in.at[slot, k], sems.at[slot, k])
                for k, w_hbm in enumerate((wg_ref, wu_ref, wd_ref))]

    @pl.when((rows > 0) & ((i == 0) | (e != prev)))
    def _():
        slot = slot_ref[i]

        @pl.when(i == 0)
        def _():
            for cp in weight_copies(e, slot):
                cp.start()

        nxt = next_ref[i]

        @pl.when(nxt >= 0)
        def _():
            for cp in weight_copies(nxt, 1 - slot):
                cp.start()

        for cp in weight_copies(e, slot):
            cp.wait()
        for k, dst in enumerate((wg_s, wu_s, wd_s)):
            dst[...] = w_in[slot, k].astype(BF16)

    def compute(r):
        x_lo, x_hi = _unpack(x_ref[:r, :])

        def proj_in(w_s, b_ref):
            return (jnp.dot(x_lo, w_s[:half, :], preferred_element_type=F32)
                    + jnp.dot(x_hi, w_s[half:, :], preferred_element_type=F32) + b_ref[0])

        gate = jnp.minimum(proj_in(wg_s, bg_ref), SWIGLU_LIMIT)
        up = jnp.clip(proj_in(wu_s, bu_ref), -SWIGLU_LIMIT, SWIGLU_LIMIT)
        glu = gate * _sigmoid(gate * SWIGLU_ALPHA)
        act = ((up + 1.0) * glu).astype(BF16)
        y = jnp.dot(act, wd_s[...], preferred_element_type=F32) + bd_ref[0]
        y_ref[:r, :] = _pack(y[:, :half], y[:, half:])
        if r < M:
            y_ref[r:, :] = jnp.zeros((M - r, half), y_ref.dtype)

    for r in range(EXPERT_SUB, M + 1, EXPERT_SUB):
        pl.when(rows == r)(functools.partial(compute, r))

    @pl.when(rows == 0)
    def _():
        y_ref[...] = jnp.zeros(y_ref.shape, y_ref.dtype)


def _experts(blk_exp, blk_rows, blk_slot, blk_next, x_rows, n_pad, w_g, b_g, w_u, b_u, w_d, b_d):
    M = EXPERT_BLOCK
    assert D_FF == D_MODEL
    bspec = lambda n: pl.BlockSpec((1, 1, n), lambda i, be, *_: (be[i], 0, 0))
    wspec = pl.BlockSpec(memory_space=pl.ANY)
    return pl.pallas_call(
        _expert_kernel,
        grid_spec=pltpu.PrefetchScalarGridSpec(
            num_scalar_prefetch=4,
            grid=(n_pad // M,),
            in_specs=[pl.BlockSpec((M, D_MODEL // 2), lambda i, *_: (i, 0)),
                      wspec, bspec(D_FF), wspec, bspec(D_FF), wspec, bspec(D_MODEL)],
            out_specs=pl.BlockSpec((M, D_MODEL // 2), lambda i, *_: (i, 0)),
            scratch_shapes=[pltpu.VMEM((2, 3, D_MODEL, D_FF), F32),
                            pltpu.VMEM((D_MODEL, D_FF), BF16),
                            pltpu.VMEM((D_MODEL, D_FF), BF16),
                            pltpu.VMEM((D_FF, D_MODEL), BF16),
                            pltpu.SemaphoreType.DMA((2, 3))]),
        out_shape=jax.ShapeDtypeStruct((n_pad, D_MODEL // 2), jnp.uint32),
        compiler_params=_params(("arbitrary",), VMEM_LIMIT),
        name="experts",
    )(blk_exp, blk_rows, blk_slot, blk_next, x_rows, w_g, b_g, w_u, b_u, w_d, b_d)


def _final_kernel(toff_ref, eoff_ref, n_ref, trows_ref, x1_ref, posw_ref, p_ref, gpg_ref, wpg_ref, wpp_ref, gpp_ref,
                  gfin_ref, y_ref, o_ref, buf, sems, *, nt):
    tm = x1_ref.shape[0]
    j = pl.program_id(0)

    def tile_runs(t, act):
        slot = t % 2

        def body(e, c):
            r = t * N_EXPERTS + e
            t0, s0 = toff_ref[r], eoff_ref[r]
            _run_pieces(n_ref[r], tm, lambda done, size: act(pltpu.make_async_copy(
                y_ref.at[pl.ds(_aligned(s0 + done), size)], buf.at[slot, pl.ds(_aligned(t0 + done), size)],
                sems.at[slot])))
            return c

        lax.fori_loop(0, N_EXPERTS, body, 0)

    @pl.when(j == 0)
    def _():
        buf[...] = jnp.zeros(buf.shape, buf.dtype)
        tile_runs(0, lambda cp: cp.start())

    @pl.when(j + 1 < nt)
    def _():
        tile_runs(j + 1, lambda cp: cp.start())

    _run_pieces(trows_ref[j], TILE_ROWS, lambda done, size: pltpu.make_async_copy(
        y_ref.at[pl.ds(0, size)], buf.at[j % 2, pl.ds(0, size)], sems.at[j % 2]).wait())
    posw = posw_ref[...]
    lane = lax.broadcasted_iota(jnp.int32, (tm, TILE_ROWS), 1).astype(F32)
    comb = jnp.zeros((tm, TILE_ROWS), F32)
    for k in range(TOP_K):
        comb = comb + jnp.where(lane == posw[:, k:k + 1], posw[:, TOP_K + k:TOP_K + k + 1], 0.0)
    comb = comb.astype(BF16)
    sure = TOP_K * tm
    tail = buf[j % 2, sure:, :]
    live = lax.broadcasted_iota(jnp.int32, tail.shape, 0) < trows_ref[j] - sure
    y_lo, y_hi = _unpack(jnp.concatenate([buf[j % 2, :sure, :], jnp.where(live, tail, jnp.uint32(0))], axis=0))
    moe = jnp.concatenate([jnp.dot(comb, y_lo, preferred_element_type=F32),
                           jnp.dot(comb, y_hi, preferred_element_type=F32)], axis=-1)
    x = x1_ref[...] + moe
    pg = _sigmoid(jnp.dot(_rms(x, gpg_ref[...]).astype(BF16), wpg_ref[...], preferred_element_type=F32))
    pp = jnp.dot(p_ref[...].astype(BF16), wpp_ref[...], preferred_element_type=F32)
    x = x + pg * _rms(pp, gpp_ref[...])
    o_ref[...] = _rms(x, gfin_ref[...])


def _final(tile_off, expert_off, run_n, tile_rows, x1, y_rows, posw, p2, g_pg, w_pg, w_pp, g_pp, g_fin, tm):
    T = x1.shape[0]
    nt = T // tm
    full = lambda shape: pl.BlockSpec(shape, lambda i, *_: (0, 0))
    rowblk = lambda w: pl.BlockSpec((tm, w), lambda i, *_: (i, 0))
    return pl.pallas_call(
        functools.partial(_final_kernel, nt=nt),
        grid_spec=pltpu.PrefetchScalarGridSpec(
            num_scalar_prefetch=4,
            grid=(nt,),
            in_specs=[rowblk(D_MODEL), rowblk(LANES), rowblk(PLE_DIM),
                      full((1, D_MODEL)), full((D_MODEL, D_MODEL)), full((PLE_DIM, D_MODEL)),
                      full((1, D_MODEL)), full((1, D_MODEL)),
                      pl.BlockSpec(memory_space=pl.ANY)],
            out_specs=rowblk(D_MODEL),
            scratch_shapes=[pltpu.VMEM((2, TILE_ROWS, D_MODEL // 2), jnp.uint32),
                            pltpu.SemaphoreType.DMA((2,))]),
        out_shape=jax.ShapeDtypeStruct((T, D_MODEL), F32),
        compiler_params=_params(("arbitrary",), VMEM_LIMIT),
        name="final",
    )(tile_off, expert_off, run_n, tile_rows, x1, posw, p2, g_pg, w_pg, w_pp, g_pp, g_fin, y_rows)


def _split_w_in(w_in):
    sizes = (GLA_QK, GLA_QK, GLA_V, GLA_V, GLA_RANK, MOBA_W, MOBA_W, MOBA_W, D_MODEL, D_MODEL)
    offs = [0]
    for s in sizes:
        offs.append(offs[-1] + s)
    parts = [w_in[:, offs[i]:offs[i + 1]] for i in range(len(sizes))]
    main = jnp.concatenate(parts[:4] + parts[5:], axis=1).astype(BF16)
    alow = jnp.pad(parts[4], ((0, 0), (0, LANES - GLA_RANK))).astype(BF16)
    return main, alow


def _layer(x2, p2, bias, B, S, g_mix, w_in, w_a2, b_a2, g_gla_out, w_up_gla, w_up_moba, w_o, g_ffn, w_router,
           b_router, w_e_gate, b_e_gate, w_e_up, b_e_up, w_e_down, b_e_down, g_ple_gate, w_ple_gate, w_ple_proj,
           g_ple_proj, g_final):
    T = B * S
    row = lambda v: v.reshape(1, -1).astype(F32)
    w_main, w_alow = _split_w_in(w_in)
    colscale = jnp.ones((D_PROJ,), F32)
    colscale = colscale.at[OFF_QA:OFF_QA + GLA_QK].set(GLA_DK ** -0.5)
    colscale = colscale.at[OFF_QB:OFF_QB + MOBA_W].set(MOBA_HD ** -0.5 * LOG2E)
    w_a2p = jnp.pad(w_a2, ((0, LANES - GLA_RANK), (0, 0))).astype(BF16)
    proj, glog = _inproj(x2, row(g_mix), w_main, colscale.reshape(1, -1), w_alow, w_a2p, row(b_a2))

    o_a = _gla(proj, glog, row(g_gla_out), B, S)
    o_b = _moba(proj, bias, B, S)

    w_r = jnp.pad(w_router.astype(F32), ((0, 0), (0, LANES - N_EXPERTS)))
    w_r_hi = w_r.astype(BF16)
    w_r2 = jnp.concatenate([w_r_hi, (w_r - w_r_hi.astype(F32)).astype(BF16)], axis=1)
    b_r = jnp.pad(b_router.astype(F32), (0, LANES - N_EXPERTS)).reshape(1, -1)
    tm = TOKEN_TILE
    nt = T // tm
    x1, h2, logits = _merge(o_a, o_b, proj, x2, w_up_gla.astype(BF16), w_up_moba.astype(BF16),
                            w_o.astype(BF16), row(g_ffn), w_r2, b_r, tm)
    posw, cnt_t, carry = _router(logits, tm)

    M = EXPERT_BLOCK
    A = nt * TILE_ROWS
    n_pad = (-(-A // M)) * M + N_EXPERTS * M
    n_blk = n_pad // M
    cnt_t = cnt_t[:, 0, :N_EXPERTS].astype(jnp.int32)
    carry = carry[:, 0, :N_EXPERTS].astype(jnp.int32)
    counts = carry[-1] + cnt_t[-1]
    padded = (counts + M - 1) // M * M
    pad_end = jnp.cumsum(padded)
    pad_start = pad_end - padded
    blk_exp = jnp.minimum(jnp.sum(pad_end[None, :] <= (jnp.arange(n_blk, dtype=jnp.int32) * M)[:, None], axis=1),
                          N_EXPERTS - 1).astype(jnp.int32)
    n_used = (pad_end[-1:] // M).astype(jnp.int32)
    blk_start = jnp.arange(n_blk, dtype=jnp.int32) * M
    eids = jnp.arange(N_EXPERTS, dtype=jnp.int32)

    def per_block(per_expert):
        return jnp.sum(jnp.where(blk_exp[:, None] == eids[None, :], per_expert[None, :], 0), axis=1).astype(jnp.int32)

    blk_rows = jnp.clip(per_block(pad_start + counts) - blk_start, 0, M)
    blk_rows = jnp.where(blk_start < pad_end[-1], (blk_rows + EXPERT_SUB - 1) // EXPERT_SUB * EXPERT_SUB, 0)
    has_rows = counts > 0
    blk_slot = per_block((jnp.cumsum(has_rows) - 1) % 2)
    later = jnp.where((eids[None, :] > eids[:, None]) & has_rows[None, :], eids[None, :], N_EXPERTS)
    next_exp = jnp.min(later, axis=1)
    blk_next = per_block(jnp.where(next_exp < N_EXPERTS, next_exp, -1))
    tile_off = (jnp.cumsum(cnt_t, axis=1) - cnt_t).reshape(-1)
    expert_off = (carry + pad_start[None, :]).reshape(-1)
    run_n = cnt_t.reshape(-1)
    tile_rows = jnp.sum(cnt_t, axis=1)
    x_rows = _dispatch(h2, posw, n_pad, tile_off, expert_off, run_n, tile_rows, pad_start + counts, padded - counts,
                       n_used, tm)
    y_rows = _experts(blk_exp, blk_rows.astype(jnp.int32), blk_slot, blk_next, x_rows, n_pad,
                      w_e_gate, b_e_gate.reshape(N_EXPERTS, 1, D_FF),
                      w_e_up, b_e_up.reshape(N_EXPERTS, 1, D_FF), w_e_down,
                      b_e_down.reshape(N_EXPERTS, 1, D_MODEL))
    return _final(tile_off, expert_off, run_n, tile_rows, x1, y_rows, posw, p2, row(g_ple_gate),
                  w_ple_gate.astype(BF16), w_ple_proj.astype(BF16), row(g_ple_proj), row(g_final), tm)


def kernel(x, p, rel_bias, g_mix, w_in, w_a2, b_a2, g_gla_out, w_up_gla, w_up_moba, w_o, g_ffn, w_router, b_router,
           w_e_gate, b_e_gate, w_e_up, b_e_up, w_e_down, b_e_down, g_ple_gate, w_ple_gate, w_ple_proj, g_ple_proj,
           g_final):
    B, S, D = x.shape
    assert D == D_MODEL and S % MOBA_BLOCK == 0 and S % GLA_CHUNK == 0 and p.shape[0] == 1
    bias = _bias_tiles(rel_bias, S // MOBA_BLOCK)
    out = _layer(x.reshape(B * S, D), p[0].reshape(B * S, PLE_DIM), bias, B, S,
                 g_mix[0], w_in[0], w_a2[0], b_a2[0], g_gla_out[0], w_up_gla[0], w_up_moba[0], w_o[0], g_ffn[0],
                 w_router[0], b_router[0], w_e_gate[0], b_e_gate[0], w_e_up[0], b_e_up[0], w_e_down[0],
                 b_e_down[0], g_ple_gate[0], w_ple_gate[0], w_ple_proj[0], g_ple_proj[0], g_final)
    return out.reshape(B, S, D)
```

```python
import functools
import math

import jax
import jax.numpy as jnp
from jax import lax
from jax.experimental import pallas as pl
from jax.experimental.pallas import tpu as pltpu

F32 = jnp.float32
BF16 = jnp.bfloat16

D_MODEL = 1024
PLE_DIM = 256
GLA_HEADS = 4
GLA_DK = 128
GLA_DV = 256
GLA_RANK = 16
GLA_TAU = 16.0
GLA_QK = GLA_HEADS * GLA_DK
GLA_V = GLA_HEADS * GLA_DV
MOBA_HEADS = 8
MOBA_HD = 128
MOBA_BLOCK = 256
MOBA_TOPK = 3
MOBA_W = MOBA_HEADS * MOBA_HD
REL_BUCKETS = 32
REL_MAX_DIST = 4096
N_EXPERTS = 32
TOP_K = 4
D_FF = 1024
SWIGLU_LIMIT = 7.0
SWIGLU_ALPHA = 1.702
EPS = 1e-6

LANES = 128
NEG = -1e30
LOG2E = math.log2(math.e)
VMEM_LIMIT = 56 * 1024 * 1024

OFF_QA, OFF_KA, OFF_VA, OFF_RA = 0, 512, 1024, 2048
OFF_QB, OFF_KB, OFF_VB, OFF_GA, OFF_GB = 3072, 4096, 5120, 6144, 7168
D_PROJ = 8192

GLA_CHUNK = 128
EXPERT_BLOCK = 512
EXPERT_SUB = 128
TOKEN_TILE = 256
RUN_ALIGN = 8
TILE_ROWS = TOP_K * TOKEN_TILE + N_EXPERTS * RUN_ALIGN

NT = (((1,), (1,)), ((), ()))
TN = (((0,), (0,)), ((), ()))


def _params(sem, vmem=None):
    return pltpu.CompilerParams(dimension_semantics=sem, vmem_limit_bytes=vmem)


def _rms(x, g):
    return x * lax.rsqrt(jnp.mean(x * x, axis=-1, keepdims=True) + EPS) * g


def _sigmoid(x):
    return 1.0 / (1.0 + jnp.exp(-x))


def _bucket_of(n):
    max_exact = REL_BUCKETS // 2
    if n < max_exact:
        return n
    return min(max_exact + int(math.log(n / max_exact) / math.log(REL_MAX_DIST / max_exact)
                               * (REL_BUCKETS - max_exact)), REL_BUCKETS - 1)


def _bias_kernel(tab_ref, bkt_ref, o_ref, *, nblk):
    strip = 8
    m = pl.program_id(0)

    def fill(lo, hi):
        def body(s, carry):
            r0 = pl.multiple_of(s * strip, strip)
            b = bkt_ref[0, pl.ds(r0, strip), :]
            accs = [jnp.zeros(b.shape, F32) for _ in range(MOBA_HEADS)]
            for bb in range(lo, hi + 1):
                hit = b == bb
                for h in range(MOBA_HEADS):
                    accs[h] = jnp.where(hit, tab_ref[h, bb], accs[h])
            for h in range(MOBA_HEADS):
                o_ref[h, 0, pl.ds(r0, strip), :] = jnp.where(b < 0, NEG, accs[h])
            return carry

        lax.fori_loop(0, MOBA_BLOCK // strip, body, 0)

    for mm in range(nblk):
        lo = max(_bucket_of(max(mm * MOBA_BLOCK - (MOBA_BLOCK - 1), 0)) - 1, 0)
        hi = min(_bucket_of(mm * MOBA_BLOCK + MOBA_BLOCK - 1) + 1, REL_BUCKETS - 1)
        pl.when(m == mm)(functools.partial(fill, lo, hi))


def _bias_tiles(rel_bias, nblk):
    i = jnp.arange(MOBA_BLOCK, dtype=jnp.int32)
    dist = (jnp.arange(nblk, dtype=jnp.int32)[:, None, None] * MOBA_BLOCK + i[None, :, None] - i[None, None, :])
    n = jnp.maximum(dist, 0)
    max_exact = REL_BUCKETS // 2
    nf = jnp.maximum(n, 1).astype(F32)
    large = max_exact + (jnp.log(nf / max_exact) / math.log(REL_MAX_DIST / max_exact)
                         * (REL_BUCKETS - max_exact)).astype(jnp.int32)
    large = jnp.minimum(large, REL_BUCKETS - 1)
    bkt = jnp.where(dist < 0, -1, jnp.where(n < max_exact, n, large)).astype(jnp.int32)
    tab = rel_bias.astype(F32).T * LOG2E
    return pl.pallas_call(
        functools.partial(_bias_kernel, nblk=nblk),
        grid=(nblk,),
        in_specs=[pl.BlockSpec(memory_space=pltpu.SMEM),
                  pl.BlockSpec((1, MOBA_BLOCK, MOBA_BLOCK), lambda m: (m, 0, 0))],
        out_specs=pl.BlockSpec((MOBA_HEADS, 1, MOBA_BLOCK, MOBA_BLOCK), lambda m: (0, m, 0, 0)),
        out_shape=jax.ShapeDtypeStruct((MOBA_HEADS, nblk, MOBA_BLOCK, MOBA_BLOCK), F32),
        compiler_params=_params(("arbitrary",)),
        name="bias_tiles",
    )(tab, bkt)


def _inproj_kernel(x0_ref, xn_ref, g_ref, w_ref, cs_ref, wal_ref, wa2_ref, ba2_ref, o_ref, glog_ref, h_scr, al_scr,
                   *, n_j):
    i, j = pl.program_id(0), pl.program_id(1)
    tm = xn_ref.shape[0]
    rs = tm // n_j

    def gate_logits(a_low):
        a = jnp.dot(a_low, wa2_ref[...], preferred_element_type=F32) + ba2_ref[...]
        log_sig = jnp.minimum(a, 0.0) - jnp.log1p(jnp.exp(-jnp.abs(a)))
        return log_sig * (1.0 / GLA_TAU)

    def slice_rows(behind):
        return pl.multiple_of(((j + n_j - behind) % n_j) * rs, rs)

    @pl.when((i == 0) & (j == 0))
    def _():
        h0 = _rms(x0_ref[...], g_ref[...]).astype(BF16)
        h_scr[0] = h0
        a_low0 = jnp.dot(h0, wal_ref[...], preferred_element_type=F32).astype(BF16)
        glog_ref[...] = gate_logits(a_low0)
        al_scr[...] = a_low0[(n_j - 2) * rs:(n_j - 1) * rs, :]

    glog_ref[pl.ds(slice_rows(2), rs), :] = gate_logits(al_scr[...])
    slot = jnp.where(j == 0, i, i + 1) % 2
    h_lag = h_scr[slot, pl.ds(slice_rows(1), rs), :]
    al_scr[...] = jnp.dot(h_lag, wal_ref[...], preferred_element_type=F32).astype(BF16)

    acc = jnp.dot(h_scr[i % 2], w_ref[...], preferred_element_type=F32)
    o_ref[...] = (acc * cs_ref[...]).astype(BF16)

    h_scr[(i + 1) % 2, pl.ds(slice_rows(0), rs), :] = _rms(xn_ref[pl.ds(slice_rows(0), rs), :],
                                                             g_ref[...]).astype(BF16)


def _inproj(x2, g_mix, w_main, colscale, w_alow, w_a2p, b_a2, tm=1024, tn=1024):
    T = x2.shape[0]
    n_i, n_j = T // tm, D_PROJ // tn
    assert n_i >= 2 and tm % (8 * n_j) == 0
    return pl.pallas_call(
        functools.partial(_inproj_kernel, n_j=n_j),
        grid=(n_i, n_j),
        in_specs=[pl.BlockSpec((tm, D_MODEL), lambda i, j: (0, 0)),
                  pl.BlockSpec((tm, D_MODEL), lambda i, j: (jnp.minimum(i + 1, n_i - 1), 0)),
                  pl.BlockSpec((1, D_MODEL), lambda i, j: (0, 0)),
                  pl.BlockSpec((D_MODEL, tn), lambda i, j: (0, j)),
                  pl.BlockSpec((1, tn), lambda i, j: (0, j)),
                  pl.BlockSpec((D_MODEL, LANES), lambda i, j: (0, 0)),
                  pl.BlockSpec((LANES, GLA_QK), lambda i, j: (0, 0)),
                  pl.BlockSpec((1, GLA_QK), lambda i, j: (0, 0))],
        out_specs=[pl.BlockSpec((tm, tn), lambda i, j: (i, j)),
                   pl.BlockSpec((tm, GLA_QK), lambda i, j: (jnp.minimum(i + jnp.minimum(j // 2, 1), n_i - 1), 0))],
        out_shape=[jax.ShapeDtypeStruct((T, D_PROJ), BF16),
                   jax.ShapeDtypeStruct((T, GLA_QK), F32)],
        scratch_shapes=[pltpu.VMEM((2, tm, D_MODEL), BF16), pltpu.VMEM((tm // n_j, LANES), BF16)],
        compiler_params=_params(("arbitrary", "arbitrary"), VMEM_LIMIT),
        name="inproj",
    )(x2, x2, g_mix, w_main, colscale, w_alow, w_a2p, b_a2)


def _gla_kernel(q_ref, k_ref, v_ref, r_ref, g_ref, gout_ref, o_ref, st_ref):
    C = GLA_CHUNK

    @pl.when(pl.program_id(1) == 0)
    def _():
        st_ref[...] = jnp.zeros(st_ref.shape, F32)

    nb = q_ref.shape[0]
    row = lax.broadcasted_iota(jnp.int32, (C, C), 0)
    col = lax.broadcasted_iota(jnp.int32, (C, C), 1)
    causal = col <= row
    ltri = causal.astype(BF16)
    mid = C // 2
    pairs = [(b, h) for b in range(nb) for h in range(GLA_HEADS)]
    ks = lambda h: slice(h * GLA_DK, (h + 1) * GLA_DK)
    vs = lambda h: slice(h * GLA_DV, (h + 1) * GLA_DV)
    gate = {}
    for b, h in pairs:
        r = r_ref[b, :, vs(h)].astype(F32)
        gate[b, h] = r * _sigmoid(r)
    G = []
    for b in range(nb):
        g = g_ref[b]
        g_hi = g.astype(BF16)
        g_lo = (g - g_hi.astype(F32)).astype(BF16)
        G.append(jnp.dot(ltri, g_hi, preferred_element_type=F32) + jnp.dot(ltri, g_lo, preferred_element_type=F32))
    Gh = {(b, h): G[b][:, ks(h)] for b, h in pairs}
    qh = {(b, h): q_ref[b, :, ks(h)].astype(F32) for b, h in pairs}
    kh = {(b, h): k_ref[b, :, ks(h)].astype(F32) for b, h in pairs}
    g_mid = {p: Gh[p][mid:mid + 1, :] for p in pairs}
    g_last = {p: Gh[p][C - 1:C, :] for p in pairs}
    A = {p: lax.dot_general((qh[p] * jnp.exp(Gh[p] - g_mid[p])).astype(BF16),
                            (kh[p] * jnp.exp(g_mid[p] - Gh[p])).astype(BF16), NT, preferred_element_type=F32)
         for p in pairs}
    st = {p: st_ref[p[0], p[1]] for p in pairs}
    inter = {p: lax.dot_general((qh[p] * jnp.exp(Gh[p])).astype(BF16), st[p].astype(BF16), NT,
                                preferred_element_type=F32) for p in pairs}
    for b, h in pairs:
        p = (b, h)
        k_d = (kh[p] * jnp.exp(g_last[p] - Gh[p])).astype(BF16)
        st_ref[b, h] = jnp.exp(g_last[p]) * st[p] + lax.dot_general(v_ref[b, :, vs(h)], k_d, TN,
                                                                     preferred_element_type=F32)
    intra = {(b, h): jnp.dot(jnp.where(causal, A[b, h], 0.0).astype(BF16), v_ref[b, :, vs(h)],
                             preferred_element_type=F32) for b, h in pairs}
    for b, h in pairs:
        o = inter[b, h] + intra[b, h]
        o_ref[b, :, vs(h)] = (_rms(o, gout_ref[...]) * gate[b, h]).astype(BF16)


def _gla(proj, glog, g_gla_out, B, S, nb=4):
    C = GLA_CHUNK
    assert B % nb == 0
    proj3 = proj.reshape(B, S, D_PROJ)
    glog3 = glog.reshape(B, S, GLA_QK)
    spec = lambda w, off: pl.BlockSpec((nb, C, w), lambda b, c: (b, c, off // w))
    out = pl.pallas_call(
        _gla_kernel,
        grid=(B // nb, S // C),
        in_specs=[spec(GLA_QK, OFF_QA), spec(GLA_QK, OFF_KA), spec(GLA_V, OFF_VA), spec(GLA_V, OFF_RA),
                  spec(GLA_QK, 0), pl.BlockSpec((1, GLA_DV), lambda b, c: (0, 0))],
        out_specs=spec(GLA_V, 0),
        out_shape=jax.ShapeDtypeStruct((B, S, GLA_V), BF16),
        scratch_shapes=[pltpu.VMEM((nb, GLA_HEADS, GLA_DV, GLA_DK), F32)],
        compiler_params=_params(("arbitrary", "arbitrary")),
        name="gla",
    )(proj3, proj3, proj3, proj3, glog3, g_gla_out)
    return out.reshape(B * S, GLA_V)


def _moba_kernel(q_ref, k_ref, v_ref, bias_ref, o_ref, ka_scr, va_scr, lg_scr, mx_scr, qa_scr, *, nblk):
    BLK, HD = MOBA_BLOCK, MOBA_HD
    S = nblk * BLK

    @pl.when((pl.program_id(0) == 0) & (pl.program_id(1) == 0))
    def _():
        blk = lax.broadcasted_iota(jnp.int32, (S, HD), 0) // BLK
        lane = lax.broadcasted_iota(jnp.int32, (S, HD), 1)
        ka_scr[:, HD:] = (lane == blk).astype(BF16)
        va_scr[:, HD:] = (lane == 0).astype(BF16)

    ka_scr[:, :HD] = k_ref[...]
    va_scr[:, :HD] = v_ref[...]
    n_plain = min(MOBA_TOPK + 1, nblk)

    def tile_id(cc, j):
        return cc * (cc + 1) // 2 + j

    def stage1(cc, q_in, keys):
        mx = None
        for j in range(cc + 1):
            lg = (lax.dot_general(q_in, keys[j * BLK:(j + 1) * BLK, :], NT, preferred_element_type=F32)
                  + bias_ref[0, cc - j])
            lg_scr[tile_id(cc, j)] = lg
            t = jnp.maximum(lg[:, :LANES], lg[:, LANES:])
            mx = t if mx is None else jnp.maximum(mx, t)
            if j == cc:
                mx_scr[cc] = mx
            yield

    def stage2(cc):
        m = mx_scr[cc].max(axis=-1, keepdims=True)
        acc = jnp.zeros((BLK, 2 * HD), F32)
        for j in range(cc + 1):
            p = jnp.exp2(lg_scr[tile_id(cc, j)] - m).astype(BF16)
            acc = acc + jnp.dot(p, va_scr[j * BLK:(j + 1) * BLK, :], preferred_element_type=F32)
            if j == cc:
                o_ref[cc * BLK:(cc + 1) * BLK, :] = (acc[:, :HD] / acc[:, HD:HD + 1]).astype(BF16)
            yield

    def drain(gen):
        for _ in gen:
            pass

    def interleave(main, side, n_main, n_side):
        side_steps = (s for g in side for s in g)
        done = 0
        for i, _ in enumerate(s for g in main for s in g):
            assert n_main > 0
            want = (i + 1) * n_side // n_main
            while done < want and next(side_steps, "end") != "end":
                done += 1
        drain(side_steps)

    plain = [stage1(cc, q_ref[cc * BLK:(cc + 1) * BLK, :], k_ref) for cc in range(n_plain)]
    if nblk > n_plain:
        next(plain[0])
        ksum = [k_ref[j * BLK:(j + 1) * BLK, :].astype(F32).reshape(BLK // 8, 8, HD).sum(axis=0).sum(
            axis=0, keepdims=True) for j in range(nblk)]
        kmean = jnp.concatenate(ksum, axis=0) * (1.0 / BLK)
        km_hi = kmean.astype(BF16)
        km_lo = (kmean - km_hi.astype(F32)).astype(BF16)
        km2 = jnp.concatenate([km_hi, km_lo], axis=0)
        pens = []
        for cc in range(n_plain, nblk):
            q = q_ref[cc * BLK:(cc + 1) * BLK, :]
            s2 = lax.dot_general(km2, q, NT, preferred_element_type=F32)
            pens.append((cc, q, s2[:nblk] + s2[nblk:]))
        for g in plain[:2]:
            drain(g)
        for cc, q, s in pens:
            ji = lax.broadcasted_iota(jnp.int32, s.shape, 0)
            cnt = jnp.zeros(s.shape, F32)
            for jp in range(cc):
                sj = s[jp:jp + 1, :]
                beats = (sj > s) | ((sj == s) & (jp < ji))
                cnt = cnt + beats.astype(F32)
            pen = jnp.where((ji < cc) & (cnt >= MOBA_TOPK), NEG, 0.0)
            pen_t = jnp.concatenate([pen, jnp.zeros((HD - nblk, BLK), F32)], axis=0).T
            qa_scr[cc - n_plain] = jnp.concatenate([q, pen_t.astype(BF16)], axis=1)
    for g in plain:
        drain(g)

    @pl.when(pl.program_id(0) >= 0)
    def _():
        late = [stage1(cc, qa_scr[cc - n_plain], ka_scr) for cc in range(n_plain, nblk)]
        early = [stage2(cc) for cc in range(n_plain)]
        interleave(late, early, tile_id(nblk, 0) - tile_id(n_plain, 0), tile_id(n_plain, 0))

    @pl.when(pl.program_id(0) >= 0)
    def _():
        for cc in range(n_plain, nblk):
            drain(stage2(cc))


def _moba(proj, bias, B, S):
    BLK = MOBA_BLOCK
    nblk = S // BLK
    H = MOBA_HEADS
    assert nblk <= MOBA_HD
    return pl.pallas_call(
        functools.partial(_moba_kernel, nblk=nblk),
        grid=(H, B),
        in_specs=[pl.BlockSpec((S, MOBA_HD), lambda h, b: (b, OFF_QB // MOBA_HD + h)),
                  pl.BlockSpec((S, MOBA_HD), lambda h, b: (b, OFF_KB // MOBA_HD + h)),
                  pl.BlockSpec((S, MOBA_HD), lambda h, b: (b, OFF_VB // MOBA_HD + h)),
                  pl.BlockSpec((1, nblk, BLK, BLK), lambda h, b: (h, 0, 0, 0))],
        out_specs=pl.BlockSpec((S, MOBA_HD), lambda h, b: (b, h)),
        out_shape=jax.ShapeDtypeStruct((B * S, MOBA_W), BF16),
        scratch_shapes=[pltpu.VMEM((S, 2 * MOBA_HD), BF16), pltpu.VMEM((S, 2 * MOBA_HD), BF16),
                        pltpu.VMEM((nblk * (nblk + 1) // 2, BLK, BLK), F32),
                        pltpu.VMEM((nblk, BLK, LANES), F32),
                        pltpu.VMEM((max(nblk - MOBA_TOPK - 1, 1), BLK, 2 * MOBA_HD), BF16)],
        compiler_params=_params(("arbitrary", "arbitrary"), VMEM_LIMIT),
        name="moba",
    )(proj, proj, proj, bias)


def _pack(lo, hi):
    lo_b = lax.bitcast_convert_type(lo.astype(BF16).astype(F32), jnp.uint32)
    hi_b = lax.bitcast_convert_type(hi.astype(BF16).astype(F32), jnp.uint32)
    return (lo_b >> 16) | (hi_b & jnp.uint32(0xFFFF0000))


def _pack_exact(lo, hi):
    lo_b = lax.bitcast_convert_type(lo, jnp.uint32)
    hi_b = lax.bitcast_convert_type(hi, jnp.uint32)
    return (lo_b >> 16) | (hi_b & jnp.uint32(0xFFFF0000))


def _unpack(w):
    lo = lax.bitcast_convert_type(w << 16, F32)
    hi = lax.bitcast_convert_type(w & jnp.uint32(0xFFFF0000), F32)
    return lo.astype(BF16), hi.astype(BF16)


def _merge_kernel(oa_ref, ob_ref, ga_ref, gb_ref, x_ref, wua_ref, wub_ref, wo_ref, gffn_ref, wr_ref, br_ref,
                  x1_ref, h2_ref, lg_ref):
    tm = x_ref.shape[0]
    u_a = jnp.dot(oa_ref[...], wua_ref[...], preferred_element_type=F32)
    u_b = jnp.dot(ob_ref[...], wub_ref[...], preferred_element_type=F32)
    y = _sigmoid(ga_ref[...].astype(F32)) * u_a + _sigmoid(gb_ref[...].astype(F32)) * u_b
    x1 = x_ref[...] + jnp.dot(y.astype(BF16), wo_ref[...], preferred_element_type=F32)
    x1_ref[...] = x1
    h2 = _rms(x1, gffn_ref[...])
    h_hi = h2.astype(BF16)
    h2_ref[...] = h_hi
    h_lo = (h2 - h_hi.astype(F32)).astype(BF16)
    r = jnp.dot(jnp.concatenate([h_hi, h_lo], axis=0), wr_ref[...], preferred_element_type=F32)
    lg_ref[...] = r[:tm, :LANES] + r[:tm, LANES:] + r[tm:, :LANES] + br_ref[...]


def _merge(o_a, o_b, proj, x2, w_ua, w_ub, w_o, g_ffn, w_r2, b_r, tm):
    T = x2.shape[0]
    full = lambda shape: pl.BlockSpec(shape, lambda i: (0, 0))
    rowblk = lambda w: pl.BlockSpec((tm, w), lambda i: (i, 0))
    return pl.pallas_call(
        _merge_kernel,
        grid=(T // tm,),
        in_specs=[rowblk(GLA_V), rowblk(MOBA_W),
                  pl.BlockSpec((tm, D_MODEL), lambda i: (i, OFF_GA // D_MODEL)),
                  pl.BlockSpec((tm, D_MODEL), lambda i: (i, OFF_GB // D_MODEL)),
                  rowblk(D_MODEL),
                  full((GLA_V, D_MODEL)), full((MOBA_W, D_MODEL)), full((D_MODEL, D_MODEL)),
                  full((1, D_MODEL)), full((D_MODEL, 2 * LANES)), full((1, LANES))],
        out_specs=[rowblk(D_MODEL), rowblk(D_MODEL), rowblk(LANES)],
        out_shape=[jax.ShapeDtypeStruct((T, D_MODEL), F32),
                   jax.ShapeDtypeStruct((T, D_MODEL), BF16),
                   jax.ShapeDtypeStruct((T, LANES), F32)],
        compiler_params=_params(("arbitrary",), VMEM_LIMIT),
        name="merge",
    )(o_a, o_b, proj, proj, x2, w_ua, w_ub, w_o, g_ffn, w_r2, b_r)


def _router_kernel(lg_ref, posw_ref, cnt_ref, carry_ref, cnt_scr, *, tm):
    rows = lg_ref.shape[0]

    @pl.when(pl.program_id(0) == 0)
    def _():
        cnt_scr[...] = jnp.zeros(cnt_scr.shape, F32)

    lane = lax.broadcasted_iota(jnp.int32, (rows, LANES), 1)
    lane_f = lane.astype(F32)
    work = jnp.where(lane < N_EXPERTS, lg_ref[...], NEG)
    vals, hots = [], []
    for _ in range(TOP_K):
        mx = work.max(axis=-1, keepdims=True)
        idx = jnp.min(jnp.where(work == mx, lane_f, float(LANES)), axis=-1, keepdims=True)
        hot = lane_f == idx
        vals.append(mx)
        hots.append(hot)
        work = jnp.where(hot, 2.0 * NEG, work)
    exps = [jnp.exp(v - vals[0]) for v in vals]
    den = exps[0] + exps[1] + exps[2] + exps[3]
    sel = jnp.zeros((rows, LANES), F32)
    for hot in hots:
        sel = sel + hot.astype(F32)
    row = lax.broadcasted_iota(jnp.int32, (tm, tm), 0)
    col = lax.broadcasted_iota(jnp.int32, (tm, tm), 1)
    below = (col < row).astype(BF16)
    er = lax.broadcasted_iota(jnp.int32, (LANES, LANES), 0)
    ec = lax.broadcasted_iota(jnp.int32, (LANES, LANES), 1)
    before = (er < ec).astype(F32)
    pos_parts = []
    for t in range(rows // tm):
        sel_t = sel[t * tm:(t + 1) * tm]
        local_rank = jnp.dot(below, sel_t.astype(BF16), preferred_element_type=F32)
        cnt_t = sel_t.sum(axis=0, keepdims=True)
        cnt_t = jnp.floor((cnt_t + (RUN_ALIGN - 1.0)) * (1.0 / RUN_ALIGN)) * RUN_ALIGN
        tile_off = jnp.dot(jnp.broadcast_to(cnt_t, (8, LANES)), before, preferred_element_type=F32,
                           precision=lax.Precision.HIGHEST)[0:1]
        pos_parts.append(local_rank + tile_off)
        carry_ref[t] = cnt_scr[...]
        cnt_ref[t] = cnt_t
        cnt_scr[...] = cnt_scr[...] + cnt_t
    pos_all = jnp.concatenate(pos_parts, axis=0)
    posw = jnp.zeros((rows, LANES), F32)
    for k in range(TOP_K):
        pk = jnp.sum(jnp.where(hots[k], pos_all, 0.0), axis=-1, keepdims=True)
        posw = jnp.where(lane == k, pk, posw)
        posw = jnp.where(lane == TOP_K + k, exps[k] / den, posw)
    posw_ref[...] = posw


def _router(logits, tm, tiles_per_step=4):
    T = logits.shape[0]
    nt = T // tm
    rows = tm * tiles_per_step
    tilerow = pl.BlockSpec((tiles_per_step, 1, LANES), lambda i: (i, 0, 0))
    return pl.pallas_call(
        functools.partial(_router_kernel, tm=tm),
        grid=(T // rows,),
        in_specs=[pl.BlockSpec((rows, LANES), lambda i: (i, 0))],
        out_specs=[pl.BlockSpec((rows, LANES), lambda i: (i, 0)), tilerow, tilerow],
        out_shape=[jax.ShapeDtypeStruct((T, LANES), F32),
                   jax.ShapeDtypeStruct((nt, 1, LANES), F32),
                   jax.ShapeDtypeStruct((nt, 1, LANES), F32)],
        scratch_shapes=[pltpu.VMEM((1, LANES), F32)],
        compiler_params=_params(("arbitrary",)),
        name="router",
    )(logits)


def _run_pieces(n, max_rows, fn):
    for b in reversed(range(RUN_ALIGN.bit_length() - 1, max_rows.bit_length())):
        size = 1 << b
        done = n & ~((2 << b) - 1)

        @pl.when((n & size) != 0)
        def _():
            fn(done, size)


def _aligned(i):
    return pl.multiple_of(i, RUN_ALIGN)


def _dispatch_kernel(toff_ref, eoff_ref, n_ref, trows_ref, zoff_ref, zn_ref, tail_ref, h2_ref, posw_ref, x_ref,
                     buf, zbuf, sems, *, nt):
    tm = h2_ref.shape[0]
    half = D_MODEL // 2
    j = pl.program_id(0)
    zrows = zbuf.shape[0]

    def tile_runs(t, act):
        slot = t % 2

        def body(e, c):
            r = t * N_EXPERTS + e
            t0, d0 = toff_ref[r], eoff_ref[r]
            _run_pieces(n_ref[r], tm, lambda done, size: act(pltpu.make_async_copy(
                buf.at[slot, pl.ds(_aligned(t0 + done), size)], x_ref.at[pl.ds(_aligned(d0 + done), size)],
                sems.at[slot])))
            return c

        lax.fori_loop(0, N_EXPERTS, body, 0)

    def zero_fill(act):
        def body(e, c):
            d0 = zoff_ref[e]
            _run_pieces(zn_ref[e], zrows, lambda done, size: act(pltpu.make_async_copy(
                zbuf.at[pl.ds(0, size)], x_ref.at[pl.ds(_aligned(d0 + done), size)], sems.at[2])))
            return c

        lax.fori_loop(0, N_EXPERTS, body, 0)

        def tail(i, c):
            act(pltpu.make_async_copy(zbuf, x_ref.at[pl.ds(pl.multiple_of(i * zrows, zrows), zrows)], sems.at[2]))
            return c

        lax.fori_loop(tail_ref[0], x_ref.shape[0] // zrows, tail, 0)

    start = lambda cp: cp.start()
    wait = lambda cp: cp.wait()

    def wait_tile(t):
        slot = t % 2
        _run_pieces(trows_ref[t], TILE_ROWS, lambda done, size: pltpu.make_async_copy(
            buf.at[slot, pl.ds(0, size)], x_ref.at[pl.ds(0, size)], sems.at[slot]).wait())

    @pl.when(j == 0)
    def _():
        zbuf[...] = jnp.zeros(zbuf.shape, zbuf.dtype)
        zero_fill(start)

    @pl.when(j >= 2)
    def _():
        wait_tile(j - 2)

    band = 256
    assert TILE_ROWS % band == 0
    pos_t = posw_ref[...].T[:TOP_K]
    pos_a = jnp.floor(pos_t * (1.0 / band))
    pos_b = pos_t - band * pos_a
    sub = lax.broadcasted_iota(jnp.int32, (band, tm), 0).astype(F32).astype(BF16)
    bands = []
    for a in range(TILE_ROWS // band):
        want = jnp.where(pos_a == a, pos_b, -1.0).astype(BF16)
        hit = jnp.zeros((band, tm), BF16)
        for k in range(TOP_K):
            row = jnp.broadcast_to(want[k:k + 1, :], (band, tm))
            hit = hit + jnp.where(sub == row, jnp.ones((band, tm), BF16), jnp.zeros((band, tm), BF16))
        bands.append(hit)
    perm = jnp.concatenate(bands, axis=0)
    xs = jnp.dot(perm, h2_ref[...], preferred_element_type=F32)
    buf[j % 2] = _pack_exact(xs[:, :half], xs[:, half:])
    tile_runs(j, start)

    @pl.when(j == nt - 1)
    def _():
        if nt >= 2:
            wait_tile(j - 1)
        wait_tile(j)
        zero_fill(wait)


def _dispatch(h2, posw, n_rows, tile_off, expert_off, run_n, tile_rows, zoff, zn, tail, tm):
    T = h2.shape[0]
    nt = T // tm
    assert n_rows % EXPERT_BLOCK == 0
    return pl.pallas_call(
        functools.partial(_dispatch_kernel, nt=nt),
        grid_spec=pltpu.PrefetchScalarGridSpec(
            num_scalar_prefetch=7,
            grid=(nt,),
            in_specs=[pl.BlockSpec((tm, D_MODEL), lambda i, *_: (i, 0)),
                      pl.BlockSpec((tm, LANES), lambda i, *_: (i, 0))],
            out_specs=pl.BlockSpec(memory_space=pl.ANY),
            scratch_shapes=[pltpu.VMEM((2, TILE_ROWS, D_MODEL // 2), jnp.uint32),
                            pltpu.VMEM((EXPERT_BLOCK, D_MODEL // 2), jnp.uint32),
                            pltpu.SemaphoreType.DMA((3,))]),
        out_shape=jax.ShapeDtypeStruct((n_rows, D_MODEL // 2), jnp.uint32),
        compiler_params=_params(("arbitrary",), VMEM_LIMIT),
        name="dispatch",
    )(tile_off, expert_off, run_n, tile_rows, zoff, zn, tail, h2, posw)


def _expert_kernel(be_ref, rows_ref, slot_ref, next_ref, x_ref, wg_ref, bg_ref, wu_ref, bu_ref, wd_ref, bd_ref, y_ref,
                   w_in, wg_s, wu_s, wd_s, sems):
    i = pl.program_id(0)
    half = D_MODEL // 2
    M = x_ref.shape[0]
    rows = rows_ref[i]
    e = be_ref[i]
    prev = be_ref[jnp.maximum(i - 1, 0)]

    def weight_copies(expert, slot):
        return [pltpu.make_async_copy(w_hbm.at[expert], w_in.at[slot, k], sems.at[slot, k])
                for k, w_hbm in enumerate((wg_ref, wu_ref, wd_ref))]

    @pl.when((rows > 0) & ((i == 0) | (e != prev)))
    def _():
        slot = slot_ref[i]

        @pl.when(i == 0)
        def _():
            for cp in weight_copies(e, slot):
                cp.start()

        nxt = next_ref[i]

        @pl.when(nxt >= 0)
        def _():
            for cp in weight_copies(nxt, 1 - slot):
                cp.start()

        for cp in weight_copies(e, slot):
            cp.wait()
        for k, dst in enumerate((wg_s, wu_s, wd_s)):
            dst[...] = w_in[slot, k].astype(BF16)

    def compute(r):
        x_lo, x_hi = _unpack(x_ref[:r, :])

        def proj_in(w_s, b_ref):
            return (jnp.dot(x_lo, w_s[:half, :], preferred_element_type=F32)
                    + jnp.dot(x_hi, w_s[half:, :], preferred_element_type=F32) + b_ref[0])

        gate = jnp.minimum(proj_in(wg_s, bg_ref), SWIGLU_LIMIT)
        up = jnp.clip(proj_in(wu_s, bu_ref), -SWIGLU_LIMIT, SWIGLU_LIMIT)
        glu = gate * _sigmoid(gate * SWIGLU_ALPHA)
        act = ((up + 1.0) * glu).astype(BF16)
        y = jnp.dot(act, wd_s[...], preferred_element_type=F32) + bd_ref[0]
        y_ref[:r, :] = _pack(y[:, :half], y[:, half:])
        if r < M:
            y_ref[r:, :] = jnp.zeros((M - r, half), y_ref.dtype)

    for r in range(EXPERT_SUB, M + 1, EXPERT_SUB):
        pl.when(rows == r)(functools.partial(compute, r))

    @pl.when(rows == 0)
    def _():
        y_ref[...] = jnp.zeros(y_ref.shape, y_ref.dtype)


def _experts(blk_exp, blk_rows, blk_slot, blk_next, x_rows, n_pad, w_g, b_g, w_u, b_u, w_d, b_d):
    M = EXPERT_BLOCK
    assert D_FF == D_MODEL
    bspec = lambda n: pl.BlockSpec((1, 1, n), lambda i, be, *_: (be[i], 0, 0))
    wspec = pl.BlockSpec(memory_space=pl.ANY)
    return pl.pallas_call(
        _expert_kernel,
        grid_spec=pltpu.PrefetchScalarGridSpec(
            num_scalar_prefetch=4,
            grid=(n_pad // M,),
            in_specs=[pl.BlockSpec((M, D_MODEL // 2), lambda i, *_: (i, 0)),
                      wspec, bspec(D_FF), wspec, bspec(D_FF), wspec, bspec(D_MODEL)],
            out_specs=pl.BlockSpec((M, D_MODEL // 2), lambda i, *_: (i, 0)),
            scratch_shapes=[pltpu.VMEM((2, 3, D_MODEL, D_FF), F32),
                            pltpu.VMEM((D_MODEL, D_FF), BF16),
                            pltpu.VMEM((D_MODEL, D_FF), BF16),
                            pltpu.VMEM((D_FF, D_MODEL), BF16),
                            pltpu.SemaphoreType.DMA((2, 3))]),
        out_shape=jax.ShapeDtypeStruct((n_pad, D_MODEL // 2), jnp.uint32),
        compiler_params=_params(("arbitrary",), VMEM_LIMIT),
        name="experts",
    )(blk_exp, blk_rows, blk_slot, blk_next, x_rows, w_g, b_g, w_u, b_u, w_d, b_d)


def _final_kernel(toff_ref, eoff_ref, n_ref, trows_ref, x1_ref, posw_ref, p_ref, gpg_ref, wpg_ref, wpp_ref, gpp_ref,
                  gfin_ref, y_ref, o_ref, buf, sems, *, nt):
    tm = x1_ref.shape[0]
    j = pl.program_id(0)

    def tile_runs(t, act):
        slot = t % 2

        def body(e, c):
            r = t * N_EXPERTS + e
            t0, s0 = toff_ref[r], eoff_ref[r]
            _run_pieces(n_ref[r], tm, lambda done, size: act(pltpu.make_async_copy(
                y_ref.at[pl.ds(_aligned(s0 + done), size)], buf.at[slot, pl.ds(_aligned(t0 + done), size)],
                sems.at[slot])))
            return c

        lax.fori_loop(0, N_EXPERTS, body, 0)

    @pl.when(j == 0)
    def _():
        buf[...] = jnp.zeros(buf.shape, buf.dtype)
        tile_runs(0, lambda cp: cp.start())

    @pl.when(j + 1 < nt)
    def _():
        tile_runs(j + 1, lambda cp: cp.start())

    _run_pieces(trows_ref[j], TILE_ROWS, lambda done, size: pltpu.make_async_copy(
        y_ref.at[pl.ds(0, size)], buf.at[j % 2, pl.ds(0, size)], sems.at[j % 2]).wait())
    posw = posw_ref[...]
    lane = lax.broadcasted_iota(jnp.int32, (tm, TILE_ROWS), 1).astype(F32)
    comb = jnp.zeros((tm, TILE_ROWS), F32)
    for k in range(TOP_K):
        comb = comb + jnp.where(lane == posw[:, k:k + 1], posw[:, TOP_K + k:TOP_K + k + 1], 0.0)
    comb = comb.astype(BF16)
    sure = TOP_K * tm
    tail = buf[j % 2, sure:, :]
    live = lax.broadcasted_iota(jnp.int32, tail.shape, 0) < trows_ref[j] - sure
    y_lo, y_hi = _unpack(jnp.concatenate([buf[j % 2, :sure, :], jnp.where(live, tail, jnp.uint32(0))], axis=0))
    moe = jnp.concatenate([jnp.dot(comb, y_lo, preferred_element_type=F32),
                           jnp.dot(comb, y_hi, preferred_element_type=F32)], axis=-1)
    x = x1_ref[...] + moe
    pg = _sigmoid(jnp.dot(_rms(x, gpg_ref[...]).astype(BF16), wpg_ref[...], preferred_element_type=F32))
    pp = jnp.dot(p_ref[...].astype(BF16), wpp_ref[...], preferred_element_type=F32)
    x = x + pg * _rms(pp, gpp_ref[...])
    o_ref[...] = _rms(x, gfin_ref[...])


def _final(tile_off, expert_off, run_n, tile_rows, x1, y_rows, posw, p2, g_pg, w_pg, w_pp, g_pp, g_fin, tm):
    T = x1.shape[0]
    nt = T // tm
    full = lambda shape: pl.BlockSpec(shape, lambda i, *_: (0, 0))
    rowblk = lambda w: pl.BlockSpec((tm, w), lambda i, *_: (i, 0))
    return pl.pallas_call(
        functools.partial(_final_kernel, nt=nt),
        grid_spec=pltpu.PrefetchScalarGridSpec(
            num_scalar_prefetch=4,
            grid=(nt,),
            in_specs=[rowblk(D_MODEL), rowblk(LANES), rowblk(PLE_DIM),
                      full((1, D_MODEL)), full((D_MODEL, D_MODEL)), full((PLE_DIM, D_MODEL)),
                      full((1, D_MODEL)), full((1, D_MODEL)),
                      pl.BlockSpec(memory_space=pl.ANY)],
            out_specs=rowblk(D_MODEL),
            scratch_shapes=[pltpu.VMEM((2, TILE_ROWS, D_MODEL // 2), jnp.uint32),
                            pltpu.SemaphoreType.DMA((2,))]),
        out_shape=jax.ShapeDtypeStruct((T, D_MODEL), F32),
        compiler_params=_params(("arbitrary",), VMEM_LIMIT),
        name="final",
    )(tile_off, expert_off, run_n, tile_rows, x1, posw, p2, g_pg, w_pg, w_pp, g_pp, g_fin, y_rows)


def _split_w_in(w_in):
    sizes = (GLA_QK, GLA_QK, GLA_V, GLA_V, GLA_RANK, MOBA_W, MOBA_W, MOBA_W, D_MODEL, D_MODEL)
    offs = [0]
    for s in sizes:
        offs.append(offs[-1] + s)
    parts = [w_in[:, offs[i]:offs[i + 1]] for i in range(len(sizes))]
    main = jnp.concatenate(parts[:4] + parts[5:], axis=1).astype(BF16)
    alow = jnp.pad(parts[4], ((0, 0), (0, LANES - GLA_RANK))).astype(BF16)
    return main, alow


def _layer(x2, p2, bias, B, S, g_mix, w_in, w_a2, b_a2, g_gla_out, w_up_gla, w_up_moba, w_o, g_ffn, w_router,
           b_router, w_e_gate, b_e_gate, w_e_up, b_e_up, w_e_down, b_e_down, g_ple_gate, w_ple_gate, w_ple_proj,
           g_ple_proj, g_final):
    T = B * S
    row = lambda v: v.reshape(1, -1).astype(F32)
    w_main, w_alow = _split_w_in(w_in)
    colscale = jnp.ones((D_PROJ,), F32)
    colscale = colscale.at[OFF_QA:OFF_QA + GLA_QK].set(GLA_DK ** -0.5)
    colscale = colscale.at[OFF_QB:OFF_QB + MOBA_W].set(MOBA_HD ** -0.5 * LOG2E)
    w_a2p = jnp.pad(w_a2, ((0, LANES - GLA_RANK), (0, 0))).astype(BF16)
    proj, glog = _inproj(x2, row(g_mix), w_main, colscale.reshape(1, -1), w_alow, w_a2p, row(b_a2))

    o_a = _gla(proj, glog, row(g_gla_out), B, S)
    o_b = _moba(proj, bias, B, S)

    w_r = jnp.pad(w_router.astype(F32), ((0, 0), (0, LANES - N_EXPERTS)))
    w_r_hi = w_r.astype(BF16)
    w_r2 = jnp.concatenate([w_r_hi, (w_r - w_r_hi.astype(F32)).astype(BF16)], axis=1)
    b_r = jnp.pad(b_router.astype(F32), (0, LANES - N_EXPERTS)).reshape(1, -1)
    tm = TOKEN_TILE
    nt = T // tm
    x1, h2, logits = _merge(o_a, o_b, proj, x2, w_up_gla.astype(BF16), w_up_moba.astype(BF16),
                            w_o.astype(BF16), row(g_ffn), w_r2, b_r, tm)
    posw, cnt_t, carry = _router(logits, tm)

    M = EXPERT_BLOCK
    A = nt * TILE_ROWS
    n_pad = (-(-A // M)) * M + N_EXPERTS * M
    n_blk = n_pad // M
    cnt_t = cnt_t[:, 0, :N_EXPERTS].astype(jnp.int32)
    carry = carry[:, 0, :N_EXPERTS].astype(jnp.int32)
    counts = carry[-1] + cnt_t[-1]
    padded = (counts + M - 1) // M * M
    pad_end = jnp.cumsum(padded)
    pad_start = pad_end - padded
    blk_exp = jnp.minimum(jnp.sum(pad_end[None, :] <= (jnp.arange(n_blk, dtype=jnp.int32) * M)[:, None], axis=1),
                          N_EXPERTS - 1).astype(jnp.int32)
    n_used = (pad_end[-1:] // M).astype(jnp.int32)
    blk_start = jnp.arange(n_blk, dtype=jnp.int32) * M
    eids = jnp.arange(N_EXPERTS, dtype=jnp.int32)

    def per_block(per_expert):
        return jnp.sum(jnp.where(blk_exp[:, None] == eids[None, :], per_expert[None, :], 0), axis=1).astype(jnp.int32)

    blk_rows = jnp.clip(per_block(pad_start + counts) - blk_start, 0, M)
    blk_rows = jnp.where(blk_start < pad_end[-1], (blk_rows + EXPERT_SUB - 1) // EXPERT_SUB * EXPERT_SUB, 0)
    has_rows = counts > 0
    blk_slot = per_block((jnp.cumsum(has_rows) - 1) % 2)
    later = jnp.where((eids[None, :] > eids[:, None]) & has_rows[None, :], eids[None, :], N_EXPERTS)
    next_exp = jnp.min(later, axis=1)
    blk_next = per_block(jnp.where(next_exp < N_EXPERTS, next_exp, -1))
    tile_off = (jnp.cumsum(cnt_t, axis=1) - cnt_t).reshape(-1)
    expert_off = (carry + pad_start[None, :]).reshape(-1)
    run_n = cnt_t.reshape(-1)
    tile_rows = jnp.sum(cnt_t, axis=1)
    x_rows = _dispatch(h2, posw, n_pad, tile_off, expert_off, run_n, tile_rows, pad_start + counts, padded - counts,
                       n_used, tm)
    y_rows = _experts(blk_exp, blk_rows.astype(jnp.int32), blk_slot, blk_next, x_rows, n_pad,
                      w_e_gate, b_e_gate.reshape(N_EXPERTS, 1, D_FF),
                      w_e_up, b_e_up.reshape(N_EXPERTS, 1, D_FF), w_e_down,
                      b_e_down.reshape(N_EXPERTS, 1, D_MODEL))
    return _final(tile_off, expert_off, run_n, tile_rows, x1, y_rows, posw, p2, row(g_ple_gate),
                  w_ple_gate.astype(BF16), w_ple_proj.astype(BF16), row(g_ple_proj), row(g_final), tm)


def kernel(x, p, rel_bias, g_mix, w_in, w_a2, b_a2, g_gla_out, w_up_gla, w_up_moba, w_o, g_ffn, w_router, b_router,
           w_e_gate, b_e_gate, w_e_up, b_e_up, w_e_down, b_e_down, g_ple_gate, w_ple_gate, w_ple_proj, g_ple_proj,
           g_final):
    B, S, D = x.shape
    assert D == D_MODEL and S % MOBA_BLOCK == 0 and S % GLA_CHUNK == 0 and p.shape[0] == 1
    bias = _bias_tiles(rel_bias, S // MOBA_BLOCK)
    out = _layer(x.reshape(B * S, D), p[0].reshape(B * S, PLE_DIM), bias, B, S,
                 g_mix[0], w_in[0], w_a2[0], b_a2[0], g_gla_out[0], w_up_gla[0], w_up_moba[0], w_o[0], g_ffn[0],
                 w_router[0], b_router[0], w_e_gate[0], b_e_gate[0], w_e_up[0], b_e_up[0], w_e_down[0],
                 b_e_down[0], g_ple_gate[0], w_ple_gate[0], w_ple_proj[0], g_ple_proj[0], g_final)
    return out.reshape(B, S, D)
```

```python
import functools
import math

import jax
import jax.numpy as jnp
from jax import lax
from jax.experimental import pallas as pl
from jax.experimental.pallas import tpu as pltpu

F32 = jnp.float32
BF16 = jnp.bfloat16

D_MODEL = 1024
PLE_DIM = 256
GLA_HEADS = 4
GLA_DK = 128
GLA_DV = 256
GLA_RANK = 16
GLA_TAU = 16.0
GLA_QK = GLA_HEADS * GLA_DK
GLA_V = GLA_HEADS * GLA_DV
MOBA_HEADS = 8
MOBA_HD = 128
MOBA_BLOCK = 256
MOBA_TOPK = 3
MOBA_W = MOBA_HEADS * MOBA_HD
REL_BUCKETS = 32
REL_MAX_DIST = 4096
N_EXPERTS = 32
TOP_K = 4
D_FF = 1024
SWIGLU_LIMIT = 7.0
SWIGLU_ALPHA = 1.702
EPS = 1e-6

LANES = 128
NEG = -1e30
LOG2E = math.log2(math.e)
VMEM_LIMIT = 56 * 1024 * 1024

OFF_QA, OFF_KA, OFF_VA, OFF_RA = 0, 512, 1024, 2048
OFF_QB, OFF_KB, OFF_VB, OFF_GA, OFF_GB = 3072, 4096, 5120, 6144, 7168
D_PROJ = 8192

GLA_CHUNK = 128
EXPERT_BLOCK = 512
EXPERT_SUB = 128
TOKEN_TILE = 256
RUN_ALIGN = 8
TILE_ROWS = TOP_K * TOKEN_TILE + N_EXPERTS * RUN_ALIGN
RUN_UNROLL = 8

NT = (((1,), (1,)), ((), ()))
TN = (((0,), (0,)), ((), ()))


def _params(sem, vmem=None):
    return pltpu.CompilerParams(dimension_semantics=sem, vmem_limit_bytes=vmem)


def _rms(x, g):
    return x * lax.rsqrt(jnp.mean(x * x, axis=-1, keepdims=True) + EPS) * g


def _sigmoid(x):
    return 1.0 / (1.0 + jnp.exp(-x))


def _bucket_of(n):
    max_exact = REL_BUCKETS // 2
    if n < max_exact:
        return n
    return min(max_exact + int(math.log(n / max_exact) / math.log(REL_MAX_DIST / max_exact)
                               * (REL_BUCKETS - max_exact)), REL_BUCKETS - 1)


def _bias_kernel(tab_ref, bkt_ref, o_ref, *, nblk):
    strip = 8
    m = pl.program_id(0)

    def fill(lo, hi):
        def body(s, carry):
            r0 = pl.multiple_of(s * strip, strip)
            b = bkt_ref[0, pl.ds(r0, strip), :]
            accs = [jnp.zeros(b.shape, F32) for _ in range(MOBA_HEADS)]
            for bb in range(lo, hi + 1):
                hit = b == bb
                for h in range(MOBA_HEADS):
                    accs[h] = jnp.where(hit, tab_ref[h, bb], accs[h])
            for h in range(MOBA_HEADS):
                o_ref[h, 0, pl.ds(r0, strip), :] = jnp.where(b < 0, NEG, accs[h])
            return carry

        lax.fori_loop(0, MOBA_BLOCK // strip, body, 0)

    for mm in range(nblk):
        lo = max(_bucket_of(max(mm * MOBA_BLOCK - (MOBA_BLOCK - 1), 0)) - 1, 0)
        hi = min(_bucket_of(mm * MOBA_BLOCK + MOBA_BLOCK - 1) + 1, REL_BUCKETS - 1)
        pl.when(m == mm)(functools.partial(fill, lo, hi))


def _bias_tiles(rel_bias, nblk):
    i = jnp.arange(MOBA_BLOCK, dtype=jnp.int32)
    dist = (jnp.arange(nblk, dtype=jnp.int32)[:, None, None] * MOBA_BLOCK + i[None, :, None] - i[None, None, :])
    n = jnp.maximum(dist, 0)
    max_exact = REL_BUCKETS // 2
    nf = jnp.maximum(n, 1).astype(F32)
    large = max_exact + (jnp.log(nf / max_exact) / math.log(REL_MAX_DIST / max_exact)
                         * (REL_BUCKETS - max_exact)).astype(jnp.int32)
    large = jnp.minimum(large, REL_BUCKETS - 1)
    bkt = jnp.where(dist < 0, -1, jnp.where(n < max_exact, n, large)).astype(jnp.int32)
    tab = rel_bias.astype(F32).T * LOG2E
    return pl.pallas_call(
        functools.partial(_bias_kernel, nblk=nblk),
        grid=(nblk,),
        in_specs=[pl.BlockSpec(memory_space=pltpu.SMEM),
                  pl.BlockSpec((1, MOBA_BLOCK, MOBA_BLOCK), lambda m: (m, 0, 0))],
        out_specs=pl.BlockSpec((MOBA_HEADS, 1, MOBA_BLOCK, MOBA_BLOCK), lambda m: (0, m, 0, 0)),
        out_shape=jax.ShapeDtypeStruct((MOBA_HEADS, nblk, MOBA_BLOCK, MOBA_BLOCK), F32),
        compiler_params=_params(("arbitrary",)),
        name="bias_tiles",
    )(tab, bkt)


def _inproj_kernel(x0_ref, xn_ref, g_ref, w_ref, cs_ref, wal_ref, wa2_ref, ba2_ref, o_ref, glog_ref, h_scr, al_scr,
                   *, n_j):
    i, j = pl.program_id(0), pl.program_id(1)
    tm = xn_ref.shape[0]
    rs = tm // n_j

    def gate_logits(a_low):
        a = jnp.dot(a_low, wa2_ref[...], preferred_element_type=F32) + ba2_ref[...]
        log_sig = jnp.minimum(a, 0.0) - jnp.log1p(jnp.exp(-jnp.abs(a)))
        return log_sig * (1.0 / GLA_TAU)

    def slice_rows(behind):
        return pl.multiple_of(((j + n_j - behind) % n_j) * rs, rs)

    @pl.when((i == 0) & (j == 0))
    def _():
        h0 = _rms(x0_ref[...], g_ref[...]).astype(BF16)
        h_scr[0] = h0
        a_low0 = jnp.dot(h0, wal_ref[...], preferred_element_type=F32).astype(BF16)
        glog_ref[...] = gate_logits(a_low0)
        al_scr[...] = a_low0[(n_j - 2) * rs:(n_j - 1) * rs, :]

    glog_ref[pl.ds(slice_rows(2), rs), :] = gate_logits(al_scr[...])
    slot = jnp.where(j == 0, i, i + 1) % 2
    h_lag = h_scr[slot, pl.ds(slice_rows(1), rs), :]
    al_scr[...] = jnp.dot(h_lag, wal_ref[...], preferred_element_type=F32).astype(BF16)

    acc = jnp.dot(h_scr[i % 2], w_ref[...], preferred_element_type=F32)
    o_ref[...] = (acc * cs_ref[...]).astype(BF16)

    h_scr[(i + 1) % 2, pl.ds(slice_rows(0), rs), :] = _rms(xn_ref[pl.ds(slice_rows(0), rs), :],
                                                             g_ref[...]).astype(BF16)


def _inproj(x2, g_mix, w_main, colscale, w_alow, w_a2p, b_a2, tm=1024, tn=1024):
    T = x2.shape[0]
    n_i, n_j = T // tm, D_PROJ // tn
    assert n_i >= 2 and tm % (8 * n_j) == 0
    return pl.pallas_call(
        functools.partial(_inproj_kernel, n_j=n_j),
        grid=(n_i, n_j),
        in_specs=[pl.BlockSpec((tm, D_MODEL), lambda i, j: (0, 0)),
                  pl.BlockSpec((tm, D_MODEL), lambda i, j: (jnp.minimum(i + 1, n_i - 1), 0)),
                  pl.BlockSpec((1, D_MODEL), lambda i, j: (0, 0)),
                  pl.BlockSpec((D_MODEL, tn), lambda i, j: (0, j)),
                  pl.BlockSpec((1, tn), lambda i, j: (0, j)),
                  pl.BlockSpec((D_MODEL, LANES), lambda i, j: (0, 0)),
                  pl.BlockSpec((LANES, GLA_QK), lambda i, j: (0, 0)),
                  pl.BlockSpec((1, GLA_QK), lambda i, j: (0, 0))],
        out_specs=[pl.BlockSpec((tm, tn), lambda i, j: (i, j)),
                   pl.BlockSpec((tm, GLA_QK), lambda i, j: (jnp.minimum(i + jnp.minimum(j // 2, 1), n_i - 1), 0))],
        out_shape=[jax.ShapeDtypeStruct((T, D_PROJ), BF16),
                   jax.ShapeDtypeStruct((T, GLA_QK), F32)],
        scratch_shapes=[pltpu.VMEM((2, tm, D_MODEL), BF16), pltpu.VMEM((tm // n_j, LANES), BF16)],
        compiler_params=_params(("arbitrary", "arbitrary"), VMEM_LIMIT),
        name="inproj",
    )(x2, x2, g_mix, w_main, colscale, w_alow, w_a2p, b_a2)


def _gla_kernel(q_ref, k_ref, v_ref, r_ref, g_ref, gout_ref, o_ref, st_ref):
    C = GLA_CHUNK

    @pl.when(pl.program_id(1) == 0)
    def _():
        st_ref[...] = jnp.zeros(st_ref.shape, F32)

    nb = q_ref.shape[0]
    row = lax.broadcasted_iota(jnp.int32, (C, C), 0)
    col = lax.broadcasted_iota(jnp.int32, (C, C), 1)
    causal = col <= row
    ltri = causal.astype(BF16)
    mid = C // 2
    pairs = [(b, h) for b in range(nb) for h in range(GLA_HEADS)]
    ks = lambda h: slice(h * GLA_DK, (h + 1) * GLA_DK)
    vs = lambda h: slice(h * GLA_DV, (h + 1) * GLA_DV)
    gate = {}
    for b, h in pairs:
        r = r_ref[b, :, vs(h)].astype(F32)
        gate[b, h] = r * _sigmoid(r)
    G = []
    for b in range(nb):
        g = g_ref[b]
        g_hi = g.astype(BF16)
        g_lo = (g - g_hi.astype(F32)).astype(BF16)
        G.append(jnp.dot(ltri, g_hi, preferred_element_type=F32) + jnp.dot(ltri, g_lo, preferred_element_type=F32))
    Gh = {(b, h): G[b][:, ks(h)] for b, h in pairs}
    qh = {(b, h): q_ref[b, :, ks(h)].astype(F32) for b, h in pairs}
    kh = {(b, h): k_ref[b, :, ks(h)].astype(F32) for b, h in pairs}
    g_mid = {p: Gh[p][mid:mid + 1, :] for p in pairs}
    g_last = {p: Gh[p][C - 1:C, :] for p in pairs}
    A = {p: lax.dot_general((qh[p] * jnp.exp(Gh[p] - g_mid[p])).astype(BF16),
                            (kh[p] * jnp.exp(g_mid[p] - Gh[p])).astype(BF16), NT, preferred_element_type=F32)
         for p in pairs}
    st = {p: st_ref[p[0], p[1]] for p in pairs}
    inter = {p: lax.dot_general((qh[p] * jnp.exp(Gh[p])).astype(BF16), st[p].astype(BF16), NT,
                                preferred_element_type=F32) for p in pairs}
    for b, h in pairs:
        p = (b, h)
        k_d = (kh[p] * jnp.exp(g_last[p] - Gh[p])).astype(BF16)
        st_ref[b, h] = jnp.exp(g_last[p]) * st[p] + lax.dot_general(v_ref[b, :, vs(h)], k_d, TN,
                                                                     preferred_element_type=F32)
    intra = {(b, h): jnp.dot(jnp.where(causal, A[b, h], 0.0).astype(BF16), v_ref[b, :, vs(h)],
                             preferred_element_type=F32) for b, h in pairs}
    for b, h in pairs:
        o = inter[b, h] + intra[b, h]
        o_ref[b, :, vs(h)] = (_rms(o, gout_ref[...]) * gate[b, h]).astype(BF16)


def _gla(proj, glog, g_gla_out, B, S, nb=4):
    C = GLA_CHUNK
    assert B % nb == 0
    proj3 = proj.reshape(B, S, D_PROJ)
    glog3 = glog.reshape(B, S, GLA_QK)
    spec = lambda w, off: pl.BlockSpec((nb, C, w), lambda b, c: (b, c, off // w))
    out = pl.pallas_call(
        _gla_kernel,
        grid=(B // nb, S // C),
        in_specs=[spec(GLA_QK, OFF_QA), spec(GLA_QK, OFF_KA), spec(GLA_V, OFF_VA), spec(GLA_V, OFF_RA),
                  spec(GLA_QK, 0), pl.BlockSpec((1, GLA_DV), lambda b, c: (0, 0))],
        out_specs=spec(GLA_V, 0),
        out_shape=jax.ShapeDtypeStruct((B, S, GLA_V), BF16),
        scratch_shapes=[pltpu.VMEM((nb, GLA_HEADS, GLA_DV, GLA_DK), F32)],
        compiler_params=_params(("arbitrary", "arbitrary")),
        name="gla",
    )(proj3, proj3, proj3, proj3, glog3, g_gla_out)
    return out.reshape(B * S, GLA_V)


def _moba_kernel(q_ref, k_ref, v_ref, bias_ref, o_ref, ka_scr, va_scr, lg_scr, mx_scr, qa_scr, *, nblk):
    BLK, HD = MOBA_BLOCK, MOBA_HD
    S = nblk * BLK

    @pl.when((pl.program_id(0) == 0) & (pl.program_id(1) == 0))
    def _():
        blk = lax.broadcasted_iota(jnp.int32, (S, HD), 0) // BLK
        lane = lax.broadcasted_iota(jnp.int32, (S, HD), 1)
        ka_scr[:, HD:] = (lane == blk).astype(BF16)
        va_scr[:, HD:] = (lane == 0).astype(BF16)

    ka_scr[:, :HD] = k_ref[...]
    va_scr[:, :HD] = v_ref[...]
    n_plain = min(MOBA_TOPK + 1, nblk)

    def tile_id(cc, j):
        return cc * (cc + 1) // 2 + j

    def stage1(cc, q_in, keys):
        mx = None
        for j in range(cc + 1):
            lg = (lax.dot_general(q_in, keys[j * BLK:(j + 1) * BLK, :], NT, preferred_element_type=F32)
                  + bias_ref[0, cc - j])
            lg_scr[tile_id(cc, j)] = lg
            t = jnp.maximum(lg[:, :LANES], lg[:, LANES:])
            mx = t if mx is None else jnp.maximum(mx, t)
            if j == cc:
                mx_scr[cc] = mx
            yield

    def stage2(cc):
        m = mx_scr[cc].max(axis=-1, keepdims=True)
        acc = jnp.zeros((BLK, 2 * HD), F32)
        for j in range(cc + 1):
            p = jnp.exp2(lg_scr[tile_id(cc, j)] - m).astype(BF16)
            acc = acc + jnp.dot(p, va_scr[j * BLK:(j + 1) * BLK, :], preferred_element_type=F32)
            if j == cc:
                o_ref[cc * BLK:(cc + 1) * BLK, :] = (acc[:, :HD] / acc[:, HD:HD + 1]).astype(BF16)
            yield

    def drain(gen):
        for _ in gen:
            pass

    def interleave(main, side, n_main, n_side):
        side_steps = (s for g in side for s in g)
        done = 0
        for i, _ in enumerate(s for g in main for s in g):
            assert n_main > 0
            want = (i + 1) * n_side // n_main
            while done < want and next(side_steps, "end") != "end":
                done += 1
        drain(side_steps)

    plain = [stage1(cc, q_ref[cc * BLK:(cc + 1) * BLK, :], k_ref) for cc in range(n_plain)]
    if nblk > n_plain:
        next(plain[0])
        ksum = [k_ref[j * BLK:(j + 1) * BLK, :].astype(F32).reshape(BLK // 8, 8, HD).sum(axis=0).sum(
            axis=0, keepdims=True) for j in range(nblk)]
        kmean = jnp.concatenate(ksum, axis=0) * (1.0 / BLK)
        km_hi = kmean.astype(BF16)
        km_lo = (kmean - km_hi.astype(F32)).astype(BF16)
        km2 = jnp.concatenate([km_hi, km_lo], axis=0)
        pens = []
        for cc in range(n_plain, nblk):
            q = q_ref[cc * BLK:(cc + 1) * BLK, :]
            s2 = lax.dot_general(km2, q, NT, preferred_element_type=F32)
            pens.append((cc, q, s2[:nblk] + s2[nblk:]))
        for g in plain[:2]:
            drain(g)
        for cc, q, s in pens:
            ji = lax.broadcasted_iota(jnp.int32, s.shape, 0)
            cnt = jnp.zeros(s.shape, F32)
            for jp in range(cc):
                sj = s[jp:jp + 1, :]
                beats = (sj > s) | ((sj == s) & (jp < ji))
                cnt = cnt + beats.astype(F32)
            pen = jnp.where((ji < cc) & (cnt >= MOBA_TOPK), NEG, 0.0)
            pen_t = jnp.concatenate([pen, jnp.zeros((HD - nblk, BLK), F32)], axis=0).T
            qa_scr[cc - n_plain] = jnp.concatenate([q, pen_t.astype(BF16)], axis=1)
    for g in plain:
        drain(g)

    @pl.when(pl.program_id(0) >= 0)
    def _():
        late = [stage1(cc, qa_scr[cc - n_plain], ka_scr) for cc in range(n_plain, nblk)]
        early = [stage2(cc) for cc in range(n_plain)]
        interleave(late, early, tile_id(nblk, 0) - tile_id(n_plain, 0), tile_id(n_plain, 0))

    @pl.when(pl.program_id(0) >= 0)
    def _():
        for cc in range(n_plain, nblk):
            drain(stage2(cc))


def _moba(proj, bias, B, S):
    BLK = MOBA_BLOCK
    nblk = S // BLK
    H = MOBA_HEADS
    assert nblk <= MOBA_HD
    return pl.pallas_call(
        functools.partial(_moba_kernel, nblk=nblk),
        grid=(H, B),
        in_specs=[pl.BlockSpec((S, MOBA_HD), lambda h, b: (b, OFF_QB // MOBA_HD + h)),
                  pl.BlockSpec((S, MOBA_HD), lambda h, b: (b, OFF_KB // MOBA_HD + h)),
                  pl.BlockSpec((S, MOBA_HD), lambda h, b: (b, OFF_VB // MOBA_HD + h)),
                  pl.BlockSpec((1, nblk, BLK, BLK), lambda h, b: (h, 0, 0, 0))],
        out_specs=pl.BlockSpec((S, MOBA_HD), lambda h, b: (b, h)),
        out_shape=jax.ShapeDtypeStruct((B * S, MOBA_W), BF16),
        scratch_shapes=[pltpu.VMEM((S, 2 * MOBA_HD), BF16), pltpu.VMEM((S, 2 * MOBA_HD), BF16),
                        pltpu.VMEM((nblk * (nblk + 1) // 2, BLK, BLK), F32),
                        pltpu.VMEM((nblk, BLK, LANES), F32),
                        pltpu.VMEM((max(nblk - MOBA_TOPK - 1, 1), BLK, 2 * MOBA_HD), BF16)],
        compiler_params=_params(("arbitrary", "arbitrary"), VMEM_LIMIT),
        name="moba",
    )(proj, proj, proj, bias)


def _pack(lo, hi):
    lo_b = lax.bitcast_convert_type(lo.astype(BF16).astype(F32), jnp.uint32)
    hi_b = lax.bitcast_convert_type(hi.astype(BF16).astype(F32), jnp.uint32)
    return (lo_b >> 16) | (hi_b & jnp.uint32(0xFFFF0000))


def _pack_exact(lo, hi):
    lo_b = lax.bitcast_convert_type(lo, jnp.uint32)
    hi_b = lax.bitcast_convert_type(hi, jnp.uint32)
    return (lo_b >> 16) | (hi_b & jnp.uint32(0xFFFF0000))


def _unpack(w):
    lo = lax.bitcast_convert_type(w << 16, F32)
    hi = lax.bitcast_convert_type(w & jnp.uint32(0xFFFF0000), F32)
    return lo.astype(BF16), hi.astype(BF16)


def _merge_kernel(oa_ref, ob_ref, ga_ref, gb_ref, x_ref, wua_ref, wub_ref, wo_ref, gffn_ref, wr_ref, br_ref,
                  x1_ref, h2_ref, lg_ref):
    tm = x_ref.shape[0]
    u_a = jnp.dot(oa_ref[...], wua_ref[...], preferred_element_type=F32)
    u_b = jnp.dot(ob_ref[...], wub_ref[...], preferred_element_type=F32)
    y = _sigmoid(ga_ref[...].astype(F32)) * u_a + _sigmoid(gb_ref[...].astype(F32)) * u_b
    x1 = x_ref[...] + jnp.dot(y.astype(BF16), wo_ref[...], preferred_element_type=F32)
    x1_ref[...] = x1
    h2 = _rms(x1, gffn_ref[...])
    h_hi = h2.astype(BF16)
    h2_ref[...] = h_hi
    h_lo = (h2 - h_hi.astype(F32)).astype(BF16)
    r = jnp.dot(jnp.concatenate([h_hi, h_lo], axis=0), wr_ref[...], preferred_element_type=F32)
    lg_ref[...] = r[:tm, :LANES] + r[:tm, LANES:] + r[tm:, :LANES] + br_ref[...]


def _merge(o_a, o_b, proj, x2, w_ua, w_ub, w_o, g_ffn, w_r2, b_r, tm):
    T = x2.shape[0]
    full = lambda shape: pl.BlockSpec(shape, lambda i: (0, 0))
    rowblk = lambda w: pl.BlockSpec((tm, w), lambda i: (i, 0))
    return pl.pallas_call(
        _merge_kernel,
        grid=(T // tm,),
        in_specs=[rowblk(GLA_V), rowblk(MOBA_W),
                  pl.BlockSpec((tm, D_MODEL), lambda i: (i, OFF_GA // D_MODEL)),
                  pl.BlockSpec((tm, D_MODEL), lambda i: (i, OFF_GB // D_MODEL)),
                  rowblk(D_MODEL),
                  full((GLA_V, D_MODEL)), full((MOBA_W, D_MODEL)), full((D_MODEL, D_MODEL)),
                  full((1, D_MODEL)), full((D_MODEL, 2 * LANES)), full((1, LANES))],
        out_specs=[rowblk(D_MODEL), rowblk(D_MODEL), rowblk(LANES)],
        out_shape=[jax.ShapeDtypeStruct((T, D_MODEL), F32),
                   jax.ShapeDtypeStruct((T, D_MODEL), BF16),
                   jax.ShapeDtypeStruct((T, LANES), F32)],
        compiler_params=_params(("arbitrary",), VMEM_LIMIT),
        name="merge",
    )(o_a, o_b, proj, proj, x2, w_ua, w_ub, w_o, g_ffn, w_r2, b_r)


def _router_kernel(lg_ref, posw_ref, cnt_ref, carry_ref, cnt_scr, *, tm):
    rows = lg_ref.shape[0]

    @pl.when(pl.program_id(0) == 0)
    def _():
        cnt_scr[...] = jnp.zeros(cnt_scr.shape, F32)

    lane = lax.broadcasted_iota(jnp.int32, (rows, LANES), 1)
    lane_f = lane.astype(F32)
    work = jnp.where(lane < N_EXPERTS, lg_ref[...], NEG)
    vals, hots = [], []
    for _ in range(TOP_K):
        mx = work.max(axis=-1, keepdims=True)
        idx = jnp.min(jnp.where(work == mx, lane_f, float(LANES)), axis=-1, keepdims=True)
        hot = lane_f == idx
        vals.append(mx)
        hots.append(hot)
        work = jnp.where(hot, 2.0 * NEG, work)
    exps = [jnp.exp(v - vals[0]) for v in vals]
    den = exps[0] + exps[1] + exps[2] + exps[3]
    sel = jnp.zeros((rows, LANES), F32)
    for hot in hots:
        sel = sel + hot.astype(F32)
    row = lax.broadcasted_iota(jnp.int32, (tm, tm), 0)
    col = lax.broadcasted_iota(jnp.int32, (tm, tm), 1)
    below = (col < row).astype(BF16)
    er = lax.broadcasted_iota(jnp.int32, (LANES, LANES), 0)
    ec = lax.broadcasted_iota(jnp.int32, (LANES, LANES), 1)
    before = (er < ec).astype(F32)
    pos_parts = []
    for t in range(rows // tm):
        sel_t = sel[t * tm:(t + 1) * tm]
        local_rank = jnp.dot(below, sel_t.astype(BF16), preferred_element_type=F32)
        cnt_t = sel_t.sum(axis=0, keepdims=True)
        cnt_t = jnp.floor((cnt_t + (RUN_ALIGN - 1.0)) * (1.0 / RUN_ALIGN)) * RUN_ALIGN
        tile_off = jnp.dot(jnp.broadcast_to(cnt_t, (8, LANES)), before, preferred_element_type=F32,
                           precision=lax.Precision.HIGHEST)[0:1]
        pos_parts.append(local_rank + tile_off)
        carry_ref[t] = cnt_scr[...]
        cnt_ref[t] = cnt_t
        cnt_scr[...] = cnt_scr[...] + cnt_t
    pos_all = jnp.concatenate(pos_parts, axis=0)
    posw = jnp.zeros((rows, LANES), F32)
    for k in range(TOP_K):
        pk = jnp.sum(jnp.where(hots[k], pos_all, 0.0), axis=-1, keepdims=True)
        posw = jnp.where(lane == k, pk, posw)
        posw = jnp.where(lane == TOP_K + k, exps[k] / den, posw)
    posw_ref[...] = posw


def _router(logits, tm, tiles_per_step=4):
    T = logits.shape[0]
    nt = T // tm
    rows = tm * tiles_per_step
    tilerow = pl.BlockSpec((tiles_per_step, 1, LANES), lambda i: (i, 0, 0))
    return pl.pallas_call(
        functools.partial(_router_kernel, tm=tm),
        grid=(T // rows,),
        in_specs=[pl.BlockSpec((rows, LANES), lambda i: (i, 0))],
        out_specs=[pl.BlockSpec((rows, LANES), lambda i: (i, 0)), tilerow, tilerow],
        out_shape=[jax.ShapeDtypeStruct((T, LANES), F32),
                   jax.ShapeDtypeStruct((nt, 1, LANES), F32),
                   jax.ShapeDtypeStruct((nt, 1, LANES), F32)],
        scratch_shapes=[pltpu.VMEM((1, LANES), F32)],
        compiler_params=_params(("arbitrary",)),
        name="router",
    )(logits)


def _run_pieces(n, max_rows, fn):
    for b in reversed(range(RUN_ALIGN.bit_length() - 1, max_rows.bit_length())):
        size = 1 << b
        done = n & ~((2 << b) - 1)

        @pl.when((n & size) != 0)
        def _():
            fn(done, size)


def _aligned(i):
    return pl.multiple_of(i, RUN_ALIGN)


def _onehot_bands(pos, val):
    band = 256
    assert TILE_ROWS % band == 0
    n_k, n_tok = pos.shape
    pos_a = jnp.floor(pos * (1.0 / band))
    pos_b = pos - band * pos_a
    sub = lax.broadcasted_iota(jnp.int32, (band, n_tok), 0).astype(F32).astype(BF16)
    zero = jnp.zeros((band, n_tok), BF16)
    bands = []
    for a in range(TILE_ROWS // band):
        want = jnp.where(pos_a == a, pos_b, -1.0).astype(BF16)
        hit = zero
        for k in range(n_k):
            row = jnp.broadcast_to(want[k:k + 1, :], (band, n_tok))
            fill = jnp.ones((band, n_tok), BF16) if val is None else jnp.broadcast_to(val[k:k + 1, :], (band, n_tok))
            hit = hit + jnp.where(sub == row, fill, zero)
        bands.append(hit)
    return jnp.concatenate(bands, axis=0)


def _dispatch_kernel(toff_ref, eoff_ref, n_ref, trows_ref, zoff_ref, zn_ref, tail_ref, h2_ref, posw_ref, x_ref,
                     buf, zbuf, sems, *, nt):
    tm = h2_ref.shape[0]
    half = D_MODEL // 2
    j = pl.program_id(0)
    zrows = zbuf.shape[0]

    def tile_runs(t, act):
        slot = t % 2

        def body(e, c):
            r = t * N_EXPERTS + e
            t0, d0 = toff_ref[r], eoff_ref[r]
            _run_pieces(n_ref[r], tm, lambda done, size: act(pltpu.make_async_copy(
                buf.at[slot, pl.ds(_aligned(t0 + done), size)], x_ref.at[pl.ds(_aligned(d0 + done), size)],
                sems.at[slot])))
            return c

        lax.fori_loop(0, N_EXPERTS, body, 0, unroll=RUN_UNROLL)

    def zero_fill(act):
        def body(e, c):
            d0 = zoff_ref[e]
            _run_pieces(zn_ref[e], zrows, lambda done, size: act(pltpu.make_async_copy(
                zbuf.at[pl.ds(0, size)], x_ref.at[pl.ds(_aligned(d0 + done), size)], sems.at[2])))
            return c

        lax.fori_loop(0, N_EXPERTS, body, 0)

        def tail(i, c):
            act(pltpu.make_async_copy(zbuf, x_ref.at[pl.ds(pl.multiple_of(i * zrows, zrows), zrows)], sems.at[2]))
            return c

        lax.fori_loop(tail_ref[0], x_ref.shape[0] // zrows, tail, 0)

    start = lambda cp: cp.start()
    wait = lambda cp: cp.wait()

    def wait_tile(t):
        slot = t % 2
        _run_pieces(trows_ref[t], TILE_ROWS, lambda done, size: pltpu.make_async_copy(
            buf.at[slot, pl.ds(0, size)], x_ref.at[pl.ds(0, size)], sems.at[slot]).wait())

    @pl.when(j == 0)
    def _():
        zbuf[...] = jnp.zeros(zbuf.shape, zbuf.dtype)
        zero_fill(start)

    @pl.when(j >= 2)
    def _():
        wait_tile(j - 2)

    perm = _onehot_bands(posw_ref[...].T[:TOP_K], None)
    xs = jnp.dot(perm, h2_ref[...], preferred_element_type=F32)
    buf[j % 2] = _pack_exact(xs[:, :half], xs[:, half:])
    tile_runs(j, start)

    @pl.when(j == nt - 1)
    def _():
        if nt >= 2:
            wait_tile(j - 1)
        wait_tile(j)
        zero_fill(wait)


def _dispatch(h2, posw, n_rows, tile_off, expert_off, run_n, tile_rows, zoff, zn, tail, tm):
    T = h2.shape[0]
    nt = T // tm
    assert n_rows % EXPERT_BLOCK == 0
    return pl.pallas_call(
        functools.partial(_dispatch_kernel, nt=nt),
        grid_spec=pltpu.PrefetchScalarGridSpec(
            num_scalar_prefetch=7,
            grid=(nt,),
            in_specs=[pl.BlockSpec((tm, D_MODEL), lambda i, *_: (i, 0)),
                      pl.BlockSpec((tm, LANES), lambda i, *_: (i, 0))],
            out_specs=pl.BlockSpec(memory_space=pl.ANY),
            scratch_shapes=[pltpu.VMEM((2, TILE_ROWS, D_MODEL // 2), jnp.uint32),
                            pltpu.VMEM((EXPERT_BLOCK, D_MODEL // 2), jnp.uint32),
                            pltpu.SemaphoreType.DMA((3,))]),
        out_shape=jax.ShapeDtypeStruct((n_rows, D_MODEL // 2), jnp.uint32),
        compiler_params=_params(("arbitrary",), VMEM_LIMIT),
        name="dispatch",
    )(tile_off, expert_off, run_n, tile_rows, zoff, zn, tail, h2, posw)


def _expert_kernel(be_ref, rows_ref, slot_ref, next_ref, x_ref, wg_ref, bg_ref, wu_ref, bu_ref, wd_ref, bd_ref, y_ref,
                   w_in, wg_s, wu_s, wd_s, sems):
    i = pl.program_id(0)
    half = D_MODEL // 2
    M = x_ref.shape[0]
    rows = rows_ref[i]
    e = be_ref[i]
    prev = be_ref[jnp.maximum(i - 1, 0)]

    def weight_copies(expert, slot):
        return [pltpu.make_async_copy(w_hbm.at[expert], w_in.at[slot, k], sems.at[slot, k])
                for k, w_hbm in enumerate((wg_ref, wu_ref, wd_ref))]

    @pl.when((rows > 0) & ((i == 0) | (e != prev)))
    def _():
        slot = slot_ref[i]

        @pl.when(i == 0)
        def _():
            for cp in weight_copies(e, slot):
                cp.start()

        nxt = next_ref[i]

        @pl.when(nxt >= 0)
        def _():
            for cp in weight_copies(nxt, 1 - slot):
                cp.start()

        for cp in weight_copies(e, slot):
            cp.wait()
        for k, dst in enumerate((wg_s, wu_s, wd_s)):
            dst[...] = w_in[slot, k].astype(BF16)

    def compute(r):
        x_lo, x_hi = _unpack(x_ref[:r, :])

        def proj_in(w_s, b_ref):
            return (jnp.dot(x_lo, w_s[:half, :], preferred_element_type=F32)
                    + jnp.dot(x_hi, w_s[half:, :], preferred_element_type=F32) + b_ref[0])

        gate = jnp.minimum(proj_in(wg_s, bg_ref), SWIGLU_LIMIT)
        up = jnp.clip(proj_in(wu_s, bu_ref), -SWIGLU_LIMIT, SWIGLU_LIMIT)
        glu = gate * _sigmoid(gate * SWIGLU_ALPHA)
        act = ((up + 1.0) * glu).astype(BF16)
        y = jnp.dot(act, wd_s[...], preferred_element_type=F32) + bd_ref[0]
        y_ref[:r, :] = _pack(y[:, :half], y[:, half:])
        if r < M:
            y_ref[r:, :] = jnp.zeros((M - r, half), y_ref.dtype)

    for r in range(EXPERT_SUB, M + 1, EXPERT_SUB):
        pl.when(rows == r)(functools.partial(compute, r))

    @pl.when(rows == 0)
    def _():
        y_ref[...] = jnp.zeros(y_ref.shape, y_ref.dtype)


def _experts(blk_exp, blk_rows, blk_slot, blk_next, x_rows, n_pad, w_g, b_g, w_u, b_u, w_d, b_d):
    M = EXPERT_BLOCK
    assert D_FF == D_MODEL
    bspec = lambda n: pl.BlockSpec((1, 1, n), lambda i, be, *_: (be[i], 0, 0))
    wspec = pl.BlockSpec(memory_space=pl.ANY)
    return pl.pallas_call(
        _expert_kernel,
        grid_spec=pltpu.PrefetchScalarGridSpec(
            num_scalar_prefetch=4,
            grid=(n_pad // M,),
            in_specs=[pl.BlockSpec((M, D_MODEL // 2), lambda i, *_: (i, 0)),
                      wspec, bspec(D_FF), wspec, bspec(D_FF), wspec, bspec(D_MODEL)],
            out_specs=pl.BlockSpec((M, D_MODEL // 2), lambda i, *_: (i, 0)),
            scratch_shapes=[pltpu.VMEM((2, 3, D_MODEL, D_FF), F32),
                            pltpu.VMEM((D_MODEL, D_FF), BF16),
                            pltpu.VMEM((D_MODEL, D_FF), BF16),
                            pltpu.VMEM((D_FF, D_MODEL), BF16),
                            pltpu.SemaphoreType.DMA((2, 3))]),
        out_shape=jax.ShapeDtypeStruct((n_pad, D_MODEL // 2), jnp.uint32),
        compiler_params=_params(("arbitrary",), VMEM_LIMIT),
        name="experts",
    )(blk_exp, blk_rows, blk_slot, blk_next, x_rows, w_g, b_g, w_u, b_u, w_d, b_d)


def _final_kernel(toff_ref, eoff_ref, n_ref, trows_ref, x1_ref, posw_ref, p_ref, gpg_ref, wpg_ref, wpp_ref, gpp_ref,
                  gfin_ref, y_ref, o_ref, buf, sems, *, nt):
    tm = x1_ref.shape[0]
    j = pl.program_id(0)

    def tile_runs(t, act):
        slot = t % 2

        def body(e, c):
            r = t * N_EXPERTS + e
            t0, s0 = toff_ref[r], eoff_ref[r]
            _run_pieces(n_ref[r], tm, lambda done, size: act(pltpu.make_async_copy(
                y_ref.at[pl.ds(_aligned(s0 + done), size)], buf.at[slot, pl.ds(_aligned(t0 + done), size)],
                sems.at[slot])))
            return c

        lax.fori_loop(0, N_EXPERTS, body, 0, unroll=RUN_UNROLL)

    @pl.when(j == 0)
    def _():
        buf[...] = jnp.zeros(buf.shape, buf.dtype)
        tile_runs(0, lambda cp: cp.start())

    @pl.when(j + 1 < nt)
    def _():
        tile_runs(j + 1, lambda cp: cp.start())

    _run_pieces(trows_ref[j], TILE_ROWS, lambda done, size: pltpu.make_async_copy(
        y_ref.at[pl.ds(0, size)], buf.at[j % 2, pl.ds(0, size)], sems.at[j % 2]).wait())
    posw_t = posw_ref[...].T
    comb_t = _onehot_bands(posw_t[:TOP_K], posw_t[TOP_K:2 * TOP_K].astype(BF16))
    sure = TOP_K * tm
    tail = buf[j % 2, sure:, :]
    live = lax.broadcasted_iota(jnp.int32, tail.shape, 0) < trows_ref[j] - sure
    y_lo, y_hi = _unpack(jnp.concatenate([buf[j % 2, :sure, :], jnp.where(live, tail, jnp.uint32(0))], axis=0))
    moe = jnp.concatenate([lax.dot_general(comb_t, y_lo, TN, preferred_element_type=F32),
                           lax.dot_general(comb_t, y_hi, TN, preferred_element_type=F32)], axis=-1)
    x = x1_ref[...] + moe
    pg = _sigmoid(jnp.dot(_rms(x, gpg_ref[...]).astype(BF16), wpg_ref[...], preferred_element_type=F32))
    pp = jnp.dot(p_ref[...].astype(BF16), wpp_ref[...], preferred_element_type=F32)
    x = x + pg * _rms(pp, gpp_ref[...])
    o_ref[...] = _rms(x, gfin_ref[...])


def _final(tile_off, expert_off, run_n, tile_rows, x1, y_rows, posw, p2, g_pg, w_pg, w_pp, g_pp, g_fin, tm):
    T = x1.shape[0]
    nt = T // tm
    full = lambda shape: pl.BlockSpec(shape, lambda i, *_: (0, 0))
    rowblk = lambda w: pl.BlockSpec((tm, w), lambda i, *_: (i, 0))
    return pl.pallas_call(
        functools.partial(_final_kernel, nt=nt),
        grid_spec=pltpu.PrefetchScalarGridSpec(
            num_scalar_prefetch=4,
            grid=(nt,),
            in_specs=[rowblk(D_MODEL), rowblk(LANES), rowblk(PLE_DIM),
                      full((1, D_MODEL)), full((D_MODEL, D_MODEL)), full((PLE_DIM, D_MODEL)),
                      full((1, D_MODEL)), full((1, D_MODEL)),
                      pl.BlockSpec(memory_space=pl.ANY)],
            out_specs=rowblk(D_MODEL),
            scratch_shapes=[pltpu.VMEM((2, TILE_ROWS, D_MODEL // 2), jnp.uint32),
                            pltpu.SemaphoreType.DMA((2,))]),
        out_shape=jax.ShapeDtypeStruct((T, D_MODEL), F32),
        compiler_params=_params(("arbitrary",), VMEM_LIMIT),
        name="final",
    )(tile_off, expert_off, run_n, tile_rows, x1, posw, p2, g_pg, w_pg, w_pp, g_pp, g_fin, y_rows)


def _split_w_in(w_in):
    sizes = (GLA_QK, GLA_QK, GLA_V, GLA_V, GLA_RANK, MOBA_W, MOBA_W, MOBA_W, D_MODEL, D_MODEL)
    offs = [0]
    for s in sizes:
        offs.append(offs[-1] + s)
    parts = [w_in[:, offs[i]:offs[i + 1]] for i in range(len(sizes))]
    main = jnp.concatenate(parts[:4] + parts[5:], axis=1).astype(BF16)
    alow = jnp.pad(parts[4], ((0, 0), (0, LANES - GLA_RANK))).astype(BF16)
    return main, alow


def _layer(x2, p2, bias, B, S, g_mix, w_in, w_a2, b_a2, g_gla_out, w_up_gla, w_up_moba, w_o, g_ffn, w_router,
           b_router, w_e_gate, b_e_gate, w_e_up, b_e_up, w_e_down, b_e_down, g_ple_gate, w_ple_gate, w_ple_proj,
           g_ple_proj, g_final):
    T = B * S
    row = lambda v: v.reshape(1, -1).astype(F32)
    w_main, w_alow = _split_w_in(w_in)
    colscale = jnp.ones((D_PROJ,), F32)
    colscale = colscale.at[OFF_QA:OFF_QA + GLA_QK].set(GLA_DK ** -0.5)
    colscale = colscale.at[OFF_QB:OFF_QB + MOBA_W].set(MOBA_HD ** -0.5 * LOG2E)
    w_a2p = jnp.pad(w_a2, ((0, LANES - GLA_RANK), (0, 0))).astype(BF16)
    proj, glog = _inproj(x2, row(g_mix), w_main, colscale.reshape(1, -1), w_alow, w_a2p, row(b_a2))

    o_a = _gla(proj, glog, row(g_gla_out), B, S)
    o_b = _moba(proj, bias, B, S)

    w_r = jnp.pad(w_router.astype(F32), ((0, 0), (0, LANES - N_EXPERTS)))
    w_r_hi = w_r.astype(BF16)
    w_r2 = jnp.concatenate([w_r_hi, (w_r - w_r_hi.astype(F32)).astype(BF16)], axis=1)
    b_r = jnp.pad(b_router.astype(F32), (0, LANES - N_EXPERTS)).reshape(1, -1)
    tm = TOKEN_TILE
    nt = T // tm
    x1, h2, logits = _merge(o_a, o_b, proj, x2, w_up_gla.astype(BF16), w_up_moba.astype(BF16),
                            w_o.astype(BF16), row(g_ffn), w_r2, b_r, tm)
    posw, cnt_t, carry = _router(logits, tm)

    M = EXPERT_BLOCK
    A = nt * TILE_ROWS
    n_pad = (-(-A // M)) * M + N_EXPERTS * M
    n_blk = n_pad // M
    cnt_t = cnt_t[:, 0, :N_EXPERTS].astype(jnp.int32)
    carry = carry[:, 0, :N_EXPERTS].astype(jnp.int32)
    counts = carry[-1] + cnt_t[-1]
    padded = (counts + M - 1) // M * M
    pad_end = jnp.cumsum(padded)
    pad_start = pad_end - padded
    blk_exp = jnp.minimum(jnp.sum(pad_end[None, :] <= (jnp.arange(n_blk, dtype=jnp.int32) * M)[:, None], axis=1),
                          N_EXPERTS - 1).astype(jnp.int32)
    n_used = (pad_end[-1:] // M).astype(jnp.int32)
    blk_start = jnp.arange(n_blk, dtype=jnp.int32) * M
    eids = jnp.arange(N_EXPERTS, dtype=jnp.int32)

    def per_block(per_expert):
        return jnp.sum(jnp.where(blk_exp[:, None] == eids[None, :], per_expert[None, :], 0), axis=1).astype(jnp.int32)

    blk_rows = jnp.clip(per_block(pad_start + counts) - blk_start, 0, M)
    blk_rows = jnp.where(blk_start < pad_end[-1], (blk_rows + EXPERT_SUB - 1) // EXPERT_SUB * EXPERT_SUB, 0)
    has_rows = counts > 0
    blk_slot = per_block((jnp.cumsum(has_rows) - 1) % 2)
    later = jnp.where((eids[None, :] > eids[:, None]) & has_rows[None, :], eids[None, :], N_EXPERTS)
    next_exp = jnp.min(later, axis=1)
    blk_next = per_block(jnp.where(next_exp < N_EXPERTS, next_exp, -1))
    tile_off = (jnp.cumsum(cnt_t, axis=1) - cnt_t).reshape(-1)
    expert_off = (carry + pad_start[None, :]).reshape(-1)
    run_n = cnt_t.reshape(-1)
    tile_rows = jnp.sum(cnt_t, axis=1)
    x_rows = _dispatch(h2, posw, n_pad, tile_off, expert_off, run_n, tile_rows, pad_start + counts, padded - counts,
                       n_used, tm)
    y_rows = _experts(blk_exp, blk_rows.astype(jnp.int32), blk_slot, blk_next, x_rows, n_pad,
                      w_e_gate, b_e_gate.reshape(N_EXPERTS, 1, D_FF),
                      w_e_up, b_e_up.reshape(N_EXPERTS, 1, D_FF), w_e_down,
                      b_e_down.reshape(N_EXPERTS, 1, D_MODEL))
    return _final(tile_off, expert_off, run_n, tile_rows, x1, y_rows, posw, p2, row(g_ple_gate),
                  w_ple_gate.astype(BF16), w_ple_proj.astype(BF16), row(g_ple_proj), row(g_final), tm)


def kernel(x, p, rel_bias, g_mix, w_in, w_a2, b_a2, g_gla_out, w_up_gla, w_up_moba, w_o, g_ffn, w_router, b_router,
           w_e_gate, b_e_gate, w_e_up, b_e_up, w_e_down, b_e_down, g_ple_gate, w_ple_gate, w_ple_proj, g_ple_proj,
           g_final):
    B, S, D = x.shape
    assert D == D_MODEL and S % MOBA_BLOCK == 0 and S % GLA_CHUNK == 0 and p.shape[0] == 1
    bias = _bias_tiles(rel_bias, S // MOBA_BLOCK)
    out = _layer(x.reshape(B * S, D), p[0].reshape(B * S, PLE_DIM), bias, B, S,
                 g_mix[0], w_in[0], w_a2[0], b_a2[0], g_gla_out[0], w_up_gla[0], w_up_moba[0], w_o[0], g_ffn[0],
                 w_router[0], b_router[0], w_e_gate[0], b_e_gate[0], w_e_up[0], b_e_up[0], w_e_down[0],
                 b_e_down[0], g_ple_gate[0], w_ple_gate[0], w_ple_proj[0], g_ple_proj[0], g_final)
    return out.reshape(B, S, D)
```

```python
import functools
import math

import jax
import jax.numpy as jnp
from jax import lax
from jax.experimental import pallas as pl
from jax.experimental.pallas import tpu as pltpu

F32 = jnp.float32
BF16 = jnp.bfloat16

D_MODEL = 1024
PLE_DIM = 256
GLA_HEADS = 4
GLA_DK = 128
GLA_DV = 256
GLA_RANK = 16
GLA_TAU = 16.0
GLA_QK = GLA_HEADS * GLA_DK
GLA_V = GLA_HEADS * GLA_DV
MOBA_HEADS = 8
MOBA_HD = 128
MOBA_BLOCK = 256
MOBA_TOPK = 3
MOBA_W = MOBA_HEADS * MOBA_HD
REL_BUCKETS = 32
REL_MAX_DIST = 4096
N_EXPERTS = 32
TOP_K = 4
D_FF = 1024
SWIGLU_LIMIT = 7.0
SWIGLU_ALPHA = 1.702
EPS = 1e-6

LANES = 128
NEG = -1e30
LOG2E = math.log2(math.e)
VMEM_LIMIT = 56 * 1024 * 1024

OFF_QA, OFF_KA, OFF_VA, OFF_RA = 0, 512, 1024, 2048
OFF_QB, OFF_KB, OFF_VB, OFF_GA, OFF_GB = 3072, 4096, 5120, 6144, 7168
D_PROJ = 8192

GLA_CHUNK = 128
EXPERT_BLOCK = 512
EXPERT_SUB = 128
TOKEN_TILE = 256
RUN_ALIGN = 8
TILE_ROWS = TOP_K * TOKEN_TILE + N_EXPERTS * RUN_ALIGN
RUN_UNROLL = 8
MERGE_TILE = 512

NT = (((1,), (1,)), ((), ()))
TN = (((0,), (0,)), ((), ()))


def _params(sem, vmem=None):
    return pltpu.CompilerParams(dimension_semantics=sem, vmem_limit_bytes=vmem)


def _rms(x, g):
    return x * lax.rsqrt(jnp.mean(x * x, axis=-1, keepdims=True) + EPS) * g


def _sigmoid(x):
    return 1.0 / (1.0 + jnp.exp(-x))


def _bucket_of(n):
    max_exact = REL_BUCKETS // 2
    if n < max_exact:
        return n
    return min(max_exact + int(math.log(n / max_exact) / math.log(REL_MAX_DIST / max_exact)
                               * (REL_BUCKETS - max_exact)), REL_BUCKETS - 1)


def _bias_kernel(tab_ref, bkt_ref, o_ref, *, nblk):
    strip = 8
    m = pl.program_id(0)

    def fill(lo, hi):
        def body(s, carry):
            r0 = pl.multiple_of(s * strip, strip)
            b = bkt_ref[0, pl.ds(r0, strip), :]
            accs = [jnp.zeros(b.shape, F32) for _ in range(MOBA_HEADS)]
            for bb in range(lo, hi + 1):
                hit = b == bb
                for h in range(MOBA_HEADS):
                    accs[h] = jnp.where(hit, tab_ref[h, bb], accs[h])
            for h in range(MOBA_HEADS):
                o_ref[h, 0, pl.ds(r0, strip), :] = jnp.where(b < 0, NEG, accs[h])
            return carry

        lax.fori_loop(0, MOBA_BLOCK // strip, body, 0)

    for mm in range(nblk):
        lo = max(_bucket_of(max(mm * MOBA_BLOCK - (MOBA_BLOCK - 1), 0)) - 1, 0)
        hi = min(_bucket_of(mm * MOBA_BLOCK + MOBA_BLOCK - 1) + 1, REL_BUCKETS - 1)
        pl.when(m == mm)(functools.partial(fill, lo, hi))


def _bias_tiles(rel_bias, nblk):
    i = jnp.arange(MOBA_BLOCK, dtype=jnp.int32)
    dist = (jnp.arange(nblk, dtype=jnp.int32)[:, None, None] * MOBA_BLOCK + i[None, :, None] - i[None, None, :])
    n = jnp.maximum(dist, 0)
    max_exact = REL_BUCKETS // 2
    nf = jnp.maximum(n, 1).astype(F32)
    large = max_exact + (jnp.log(nf / max_exact) / math.log(REL_MAX_DIST / max_exact)
                         * (REL_BUCKETS - max_exact)).astype(jnp.int32)
    large = jnp.minimum(large, REL_BUCKETS - 1)
    bkt = jnp.where(dist < 0, -1, jnp.where(n < max_exact, n, large)).astype(jnp.int32)
    tab = rel_bias.astype(F32).T * LOG2E
    return pl.pallas_call(
        functools.partial(_bias_kernel, nblk=nblk),
        grid=(nblk,),
        in_specs=[pl.BlockSpec(memory_space=pltpu.SMEM),
                  pl.BlockSpec((1, MOBA_BLOCK, MOBA_BLOCK), lambda m: (m, 0, 0))],
        out_specs=pl.BlockSpec((MOBA_HEADS, 1, MOBA_BLOCK, MOBA_BLOCK), lambda m: (0, m, 0, 0)),
        out_shape=jax.ShapeDtypeStruct((MOBA_HEADS, nblk, MOBA_BLOCK, MOBA_BLOCK), F32),
        compiler_params=_params(("arbitrary",)),
        name="bias_tiles",
    )(tab, bkt)


def _inproj_kernel(x0_ref, xn_ref, g_ref, w_ref, cs_ref, wal_ref, wa2_ref, ba2_ref, o_ref, glog_ref, h_scr, al_scr,
                   *, n_j):
    i, j = pl.program_id(0), pl.program_id(1)
    tm = xn_ref.shape[0]
    rs = tm // n_j

    def gate_logits(a_low):
        a = jnp.dot(a_low, wa2_ref[...], preferred_element_type=F32) + ba2_ref[...]
        log_sig = jnp.minimum(a, 0.0) - jnp.log1p(jnp.exp(-jnp.abs(a)))
        return log_sig * (1.0 / GLA_TAU)

    def slice_rows(behind):
        return pl.multiple_of(((j + n_j - behind) % n_j) * rs, rs)

    @pl.when((i == 0) & (j == 0))
    def _():
        h0 = _rms(x0_ref[...], g_ref[...]).astype(BF16)
        h_scr[0] = h0
        a_low0 = jnp.dot(h0, wal_ref[...], preferred_element_type=F32).astype(BF16)
        glog_ref[...] = gate_logits(a_low0)
        al_scr[...] = a_low0[(n_j - 2) * rs:(n_j - 1) * rs, :]

    glog_ref[pl.ds(slice_rows(2), rs), :] = gate_logits(al_scr[...])
    slot = jnp.where(j == 0, i, i + 1) % 2
    h_lag = h_scr[slot, pl.ds(slice_rows(1), rs), :]
    al_scr[...] = jnp.dot(h_lag, wal_ref[...], preferred_element_type=F32).astype(BF16)

    acc = jnp.dot(h_scr[i % 2], w_ref[...], preferred_element_type=F32)
    o_ref[...] = (acc * cs_ref[...]).astype(BF16)

    h_scr[(i + 1) % 2, pl.ds(slice_rows(0), rs), :] = _rms(xn_ref[pl.ds(slice_rows(0), rs), :],
                                                             g_ref[...]).astype(BF16)


def _inproj(x2, g_mix, w_main, colscale, w_alow, w_a2p, b_a2, tm=1024, tn=2048):
    T = x2.shape[0]
    n_i, n_j = T // tm, D_PROJ // tn
    assert n_i >= 2 and tm % (8 * n_j) == 0
    return pl.pallas_call(
        functools.partial(_inproj_kernel, n_j=n_j),
        grid=(n_i, n_j),
        in_specs=[pl.BlockSpec((tm, D_MODEL), lambda i, j: (0, 0)),
                  pl.BlockSpec((tm, D_MODEL), lambda i, j: (jnp.minimum(i + 1, n_i - 1), 0)),
                  pl.BlockSpec((1, D_MODEL), lambda i, j: (0, 0)),
                  pl.BlockSpec((D_MODEL, tn), lambda i, j: (0, j)),
                  pl.BlockSpec((1, tn), lambda i, j: (0, j)),
                  pl.BlockSpec((D_MODEL, LANES), lambda i, j: (0, 0)),
                  pl.BlockSpec((LANES, GLA_QK), lambda i, j: (0, 0)),
                  pl.BlockSpec((1, GLA_QK), lambda i, j: (0, 0))],
        out_specs=[pl.BlockSpec((tm, tn), lambda i, j: (i, j)),
                   pl.BlockSpec((tm, GLA_QK), lambda i, j: (jnp.minimum(i + jnp.minimum(j // 2, 1), n_i - 1), 0))],
        out_shape=[jax.ShapeDtypeStruct((T, D_PROJ), BF16),
                   jax.ShapeDtypeStruct((T, GLA_QK), F32)],
        scratch_shapes=[pltpu.VMEM((2, tm, D_MODEL), BF16), pltpu.VMEM((tm // n_j, LANES), BF16)],
        compiler_params=_params(("arbitrary", "arbitrary"), VMEM_LIMIT),
        name="inproj",
    )(x2, x2, g_mix, w_main, colscale, w_alow, w_a2p, b_a2)


def _gla_kernel(q_ref, k_ref, v_ref, r_ref, g_ref, gout_ref, o_ref, st_ref):
    C = GLA_CHUNK

    @pl.when(pl.program_id(1) == 0)
    def _():
        st_ref[...] = jnp.zeros(st_ref.shape, F32)

    nb = q_ref.shape[0]
    row = lax.broadcasted_iota(jnp.int32, (C, C), 0)
    col = lax.broadcasted_iota(jnp.int32, (C, C), 1)
    causal = col <= row
    ltri = causal.astype(BF16)
    mid = C // 2
    pairs = [(b, h) for b in range(nb) for h in range(GLA_HEADS)]
    ks = lambda h: slice(h * GLA_DK, (h + 1) * GLA_DK)
    vs = lambda h: slice(h * GLA_DV, (h + 1) * GLA_DV)
    gate = {}
    for b, h in pairs:
        r = r_ref[b, :, vs(h)].astype(F32)
        gate[b, h] = r * _sigmoid(r)
    G = []
    for b in range(nb):
        g = g_ref[b]
        g_hi = g.astype(BF16)
        g_lo = (g - g_hi.astype(F32)).astype(BF16)
        G.append(jnp.dot(ltri, g_hi, preferred_element_type=F32) + jnp.dot(ltri, g_lo, preferred_element_type=F32))
    Gh = {(b, h): G[b][:, ks(h)] for b, h in pairs}
    qh = {(b, h): q_ref[b, :, ks(h)].astype(F32) for b, h in pairs}
    kh = {(b, h): k_ref[b, :, ks(h)].astype(F32) for b, h in pairs}
    g_mid = {p: Gh[p][mid:mid + 1, :] for p in pairs}
    g_last = {p: Gh[p][C - 1:C, :] for p in pairs}
    A = {p: lax.dot_general((qh[p] * jnp.exp(Gh[p] - g_mid[p])).astype(BF16),
                            (kh[p] * jnp.exp(g_mid[p] - Gh[p])).astype(BF16), NT, preferred_element_type=F32)
         for p in pairs}
    st = {p: st_ref[p[0], p[1]] for p in pairs}
    inter = {p: lax.dot_general((qh[p] * jnp.exp(Gh[p])).astype(BF16), st[p].astype(BF16), NT,
                                preferred_element_type=F32) for p in pairs}
    for b, h in pairs:
        p = (b, h)
        k_d = (kh[p] * jnp.exp(g_last[p] - Gh[p])).astype(BF16)
        st_ref[b, h] = jnp.exp(g_last[p]) * st[p] + lax.dot_general(v_ref[b, :, vs(h)], k_d, TN,
                                                                     preferred_element_type=F32)
    intra = {(b, h): jnp.dot(jnp.where(causal, A[b, h], 0.0).astype(BF16), v_ref[b, :, vs(h)],
                             preferred_element_type=F32) for b, h in pairs}
    for b, h in pairs:
        o = inter[b, h] + intra[b, h]
        o_ref[b, :, vs(h)] = (_rms(o, gout_ref[...]) * gate[b, h]).astype(BF16)


def _gla(proj, glog, g_gla_out, B, S, nb=4):
    C = GLA_CHUNK
    assert B % nb == 0
    proj3 = proj.reshape(B, S, D_PROJ)
    glog3 = glog.reshape(B, S, GLA_QK)
    spec = lambda w, off: pl.BlockSpec((nb, C, w), lambda b, c: (b, c, off // w))
    out = pl.pallas_call(
        _gla_kernel,
        grid=(B // nb, S // C),
        in_specs=[spec(GLA_QK, OFF_QA), spec(GLA_QK, OFF_KA), spec(GLA_V, OFF_VA), spec(GLA_V, OFF_RA),
                  spec(GLA_QK, 0), pl.BlockSpec((1, GLA_DV), lambda b, c: (0, 0))],
        out_specs=spec(GLA_V, 0),
        out_shape=jax.ShapeDtypeStruct((B, S, GLA_V), BF16),
        scratch_shapes=[pltpu.VMEM((nb, GLA_HEADS, GLA_DV, GLA_DK), F32)],
        compiler_params=_params(("arbitrary", "arbitrary")),
        name="gla",
    )(proj3, proj3, proj3, proj3, glog3, g_gla_out)
    return out.reshape(B * S, GLA_V)


def _moba_kernel(q_ref, k_ref, v_ref, bias_ref, o_ref, ka_scr, va_scr, lg_scr, mx_scr, qa_scr, *, nblk):
    BLK, HD = MOBA_BLOCK, MOBA_HD
    S = nblk * BLK

    @pl.when((pl.program_id(0) == 0) & (pl.program_id(1) == 0))
    def _():
        blk = lax.broadcasted_iota(jnp.int32, (S, HD), 0) // BLK
        lane = lax.broadcasted_iota(jnp.int32, (S, HD), 1)
        ka_scr[:, HD:] = (lane == blk).astype(BF16)
        va_scr[:, HD:] = (lane == 0).astype(BF16)

    n_plain = min(MOBA_TOPK + 1, nblk)

    def tile_id(cc, j):
        return cc * (cc + 1) // 2 + j

    def stage1(cc, q_in, keys):
        mx = None
        for j in range(cc + 1):
            lg = (lax.dot_general(q_in, keys[j * BLK:(j + 1) * BLK, :], NT, preferred_element_type=F32)
                  + bias_ref[0, cc - j])
            lg_scr[tile_id(cc, j)] = lg
            t = jnp.maximum(lg[:, :LANES], lg[:, LANES:])
            mx = t if mx is None else jnp.maximum(mx, t)
            if j == cc:
                mx_scr[cc] = mx
            yield

    def stage2(cc):
        m = mx_scr[cc].max(axis=-1, keepdims=True)
        acc = jnp.zeros((BLK, 2 * HD), F32)
        for j in range(cc + 1):
            p = jnp.exp2(lg_scr[tile_id(cc, j)] - m).astype(BF16)
            acc = acc + jnp.dot(p, va_scr[j * BLK:(j + 1) * BLK, :], preferred_element_type=F32)
            if j == cc:
                o_ref[cc * BLK:(cc + 1) * BLK, :] = (acc[:, :HD] / acc[:, HD:HD + 1]).astype(BF16)
            yield

    def drain(gen):
        for _ in gen:
            pass

    def interleave(main, side, n_main, n_side):
        side_steps = (s for g in side for s in g)
        done = 0
        for i, _ in enumerate(s for g in main for s in g):
            assert n_main > 0
            want = (i + 1) * n_side // n_main
            while done < want and next(side_steps, "end") != "end":
                done += 1
        drain(side_steps)

    plain = [stage1(cc, q_ref[cc * BLK:(cc + 1) * BLK, :], k_ref) for cc in range(n_plain)]
    if nblk > n_plain:
        next(plain[0])
        ksum = [k_ref[j * BLK:(j + 1) * BLK, :].astype(F32).reshape(BLK // 8, 8, HD).sum(axis=0).sum(
            axis=0, keepdims=True) for j in range(nblk)]
        kmean = jnp.concatenate(ksum, axis=0) * (1.0 / BLK)
        km_hi = kmean.astype(BF16)
        km_lo = (kmean - km_hi.astype(F32)).astype(BF16)
        km2 = jnp.concatenate([km_hi, km_lo], axis=0)
        pens = []
        for cc in range(n_plain, nblk):
            q = q_ref[cc * BLK:(cc + 1) * BLK, :]
            s2 = lax.dot_general(km2, q, NT, preferred_element_type=F32)
            pens.append((cc, q, s2[:nblk] + s2[nblk:]))
        for g in plain[:2]:
            drain(g)
        for cc, q, s in pens:
            ji = lax.broadcasted_iota(jnp.int32, s.shape, 0)
            cnt = jnp.zeros(s.shape, F32)
            for jp in range(cc):
                sj = s[jp:jp + 1, :]
                beats = (sj > s) | ((sj == s) & (jp < ji))
                cnt = cnt + beats.astype(F32)
            pen = jnp.where((ji < cc) & (cnt >= MOBA_TOPK), NEG, 0.0)
            pen_t = jnp.concatenate([pen, jnp.zeros((HD - nblk, BLK), F32)], axis=0).T
            qa_scr[cc - n_plain] = jnp.concatenate([q, pen_t.astype(BF16)], axis=1)
    ka_scr[:, :HD] = k_ref[...]
    va_scr[:, :HD] = v_ref[...]
    for g in plain:
        drain(g)

    @pl.when(pl.program_id(0) >= 0)
    def _():
        late = [stage1(cc, qa_scr[cc - n_plain], ka_scr) for cc in range(n_plain, nblk)]
        early = [stage2(cc) for cc in range(n_plain)]
        interleave(late, early, tile_id(nblk, 0) - tile_id(n_plain, 0), tile_id(n_plain, 0))

    @pl.when(pl.program_id(0) >= 0)
    def _():
        for cc in range(n_plain, nblk):
            drain(stage2(cc))


def _moba(proj, bias, B, S):
    BLK = MOBA_BLOCK
    nblk = S // BLK
    H = MOBA_HEADS
    assert nblk <= MOBA_HD
    return pl.pallas_call(
        functools.partial(_moba_kernel, nblk=nblk),
        grid=(H, B),
        in_specs=[pl.BlockSpec((S, MOBA_HD), lambda h, b: (b, OFF_QB // MOBA_HD + h)),
                  pl.BlockSpec((S, MOBA_HD), lambda h, b: (b, OFF_KB // MOBA_HD + h)),
                  pl.BlockSpec((S, MOBA_HD), lambda h, b: (b, OFF_VB // MOBA_HD + h)),
                  pl.BlockSpec((1, nblk, BLK, BLK), lambda h, b: (h, 0, 0, 0))],
        out_specs=pl.BlockSpec((S, MOBA_HD), lambda h, b: (b, h)),
        out_shape=jax.ShapeDtypeStruct((B * S, MOBA_W), BF16),
        scratch_shapes=[pltpu.VMEM((S, 2 * MOBA_HD), BF16), pltpu.VMEM((S, 2 * MOBA_HD), BF16),
                        pltpu.VMEM((nblk * (nblk + 1) // 2, BLK, BLK), F32),
                        pltpu.VMEM((nblk, BLK, LANES), F32),
                        pltpu.VMEM((max(nblk - MOBA_TOPK - 1, 1), BLK, 2 * MOBA_HD), BF16)],
        compiler_params=_params(("arbitrary", "arbitrary"), VMEM_LIMIT),
        name="moba",
    )(proj, proj, proj, bias)


def _pack(lo, hi):
    lo_b = lax.bitcast_convert_type(lo.astype(BF16).astype(F32), jnp.uint32)
    hi_b = lax.bitcast_convert_type(hi.astype(BF16).astype(F32), jnp.uint32)
    return (lo_b >> 16) | (hi_b & jnp.uint32(0xFFFF0000))


def _pack_exact(lo, hi):
    lo_b = lax.bitcast_convert_type(lo, jnp.uint32)
    hi_b = lax.bitcast_convert_type(hi, jnp.uint32)
    return (lo_b >> 16) | (hi_b & jnp.uint32(0xFFFF0000))


def _unpack(w):
    lo = lax.bitcast_convert_type(w << 16, F32)
    hi = lax.bitcast_convert_type(w & jnp.uint32(0xFFFF0000), F32)
    return lo.astype(BF16), hi.astype(BF16)


def _merge_kernel(oa_ref, ob_ref, ga_ref, gb_ref, x_ref, wua_ref, wub_ref, wo_ref, gffn_ref, wr_ref, br_ref,
                  x1_ref, h2_ref, lg_ref):
    tm = x_ref.shape[0]
    u_a = jnp.dot(oa_ref[...], wua_ref[...], preferred_element_type=F32)
    u_b = jnp.dot(ob_ref[...], wub_ref[...], preferred_element_type=F32)
    y = _sigmoid(ga_ref[...].astype(F32)) * u_a + _sigmoid(gb_ref[...].astype(F32)) * u_b
    x1 = x_ref[...] + jnp.dot(y.astype(BF16), wo_ref[...], preferred_element_type=F32)
    x1_ref[...] = x1
    h2 = _rms(x1, gffn_ref[...])
    h_hi = h2.astype(BF16)
    h2_ref[...] = h_hi
    h_lo = (h2 - h_hi.astype(F32)).astype(BF16)
    r = jnp.dot(jnp.concatenate([h_hi, h_lo], axis=0), wr_ref[...], preferred_element_type=F32)
    lg_ref[...] = r[:tm, :LANES] + r[:tm, LANES:] + r[tm:, :LANES] + br_ref[...]


def _merge(o_a, o_b, proj, x2, w_ua, w_ub, w_o, g_ffn, w_r2, b_r, tm):
    T = x2.shape[0]
    full = lambda shape: pl.BlockSpec(shape, lambda i: (0, 0))
    rowblk = lambda w: pl.BlockSpec((tm, w), lambda i: (i, 0))
    return pl.pallas_call(
        _merge_kernel,
        grid=(T // tm,),
        in_specs=[rowblk(GLA_V), rowblk(MOBA_W),
                  pl.BlockSpec((tm, D_MODEL), lambda i: (i, OFF_GA // D_MODEL)),
                  pl.BlockSpec((tm, D_MODEL), lambda i: (i, OFF_GB // D_MODEL)),
                  rowblk(D_MODEL),
                  full((GLA_V, D_MODEL)), full((MOBA_W, D_MODEL)), full((D_MODEL, D_MODEL)),
                  full((1, D_MODEL)), full((D_MODEL, 2 * LANES)), full((1, LANES))],
        out_specs=[rowblk(D_MODEL), rowblk(D_MODEL), rowblk(LANES)],
        out_shape=[jax.ShapeDtypeStruct((T, D_MODEL), F32),
                   jax.ShapeDtypeStruct((T, D_MODEL), BF16),
                   jax.ShapeDtypeStruct((T, LANES), F32)],
        compiler_params=_params(("arbitrary",), VMEM_LIMIT),
        name="merge",
    )(o_a, o_b, proj, proj, x2, w_ua, w_ub, w_o, g_ffn, w_r2, b_r)


def _router_kernel(lg_ref, posw_ref, cnt_ref, carry_ref, cnt_scr, *, tm):
    rows = lg_ref.shape[0]

    @pl.when(pl.program_id(0) == 0)
    def _():
        cnt_scr[...] = jnp.zeros(cnt_scr.shape, F32)

    lane = lax.broadcasted_iota(jnp.int32, (rows, LANES), 1)
    lane_f = lane.astype(F32)
    work = jnp.where(lane < N_EXPERTS, lg_ref[...], NEG)
    vals, hots = [], []
    for _ in range(TOP_K):
        mx = work.max(axis=-1, keepdims=True)
        idx = jnp.min(jnp.where(work == mx, lane_f, float(LANES)), axis=-1, keepdims=True)
        hot = lane_f == idx
        vals.append(mx)
        hots.append(hot)
        work = jnp.where(hot, 2.0 * NEG, work)
    exps = [jnp.exp(v - vals[0]) for v in vals]
    den = exps[0] + exps[1] + exps[2] + exps[3]
    sel = jnp.zeros((rows, LANES), F32)
    for hot in hots:
        sel = sel + hot.astype(F32)
    row = lax.broadcasted_iota(jnp.int32, (tm, tm), 0)
    col = lax.broadcasted_iota(jnp.int32, (tm, tm), 1)
    below = (col < row).astype(BF16)
    er = lax.broadcasted_iota(jnp.int32, (LANES, LANES), 0)
    ec = lax.broadcasted_iota(jnp.int32, (LANES, LANES), 1)
    before = (er < ec).astype(F32)
    pos_parts = []
    for t in range(rows // tm):
        sel_t = sel[t * tm:(t + 1) * tm]
        local_rank = jnp.dot(below, sel_t.astype(BF16), preferred_element_type=F32)
        cnt_t = sel_t.sum(axis=0, keepdims=True)
        cnt_t = jnp.floor((cnt_t + (RUN_ALIGN - 1.0)) * (1.0 / RUN_ALIGN)) * RUN_ALIGN
        tile_off = jnp.dot(jnp.broadcast_to(cnt_t, (8, LANES)), before, preferred_element_type=F32,
                           precision=lax.Precision.HIGHEST)[0:1]
        pos_parts.append(local_rank + tile_off)
        carry_ref[t] = cnt_scr[...]
        cnt_ref[t] = cnt_t
        cnt_scr[...] = cnt_scr[...] + cnt_t
    pos_all = jnp.concatenate(pos_parts, axis=0)
    posw = jnp.zeros((rows, LANES), F32)
    for k in range(TOP_K):
        pk = jnp.sum(jnp.where(hots[k], pos_all, 0.0), axis=-1, keepdims=True)
        posw = jnp.where(lane == k, pk, posw)
        posw = jnp.where(lane == TOP_K + k, exps[k] / den, posw)
    posw_ref[...] = posw


def _router(logits, tm, tiles_per_step=4):
    T = logits.shape[0]
    nt = T // tm
    rows = tm * tiles_per_step
    tilerow = pl.BlockSpec((tiles_per_step, 1, LANES), lambda i: (i, 0, 0))
    return pl.pallas_call(
        functools.partial(_router_kernel, tm=tm),
        grid=(T // rows,),
        in_specs=[pl.BlockSpec((rows, LANES), lambda i: (i, 0))],
        out_specs=[pl.BlockSpec((rows, LANES), lambda i: (i, 0)), tilerow, tilerow],
        out_shape=[jax.ShapeDtypeStruct((T, LANES), F32),
                   jax.ShapeDtypeStruct((nt, 1, LANES), F32),
                   jax.ShapeDtypeStruct((nt, 1, LANES), F32)],
        scratch_shapes=[pltpu.VMEM((1, LANES), F32)],
        compiler_params=_params(("arbitrary",)),
        name="router",
    )(logits)


def _run_pieces(n, max_rows, fn):
    for b in reversed(range(RUN_ALIGN.bit_length() - 1, max_rows.bit_length())):
        size = 1 << b
        done = n & ~((2 << b) - 1)

        @pl.when((n & size) != 0)
        def _():
            fn(done, size)


def _aligned(i):
    return pl.multiple_of(i, RUN_ALIGN)


def _onehot_bands(pos, val):
    band = 256
    assert TILE_ROWS % band == 0
    n_k, n_tok = pos.shape
    pos_a = jnp.floor(pos * (1.0 / band))
    pos_b = pos - band * pos_a
    sub = lax.broadcasted_iota(jnp.int32, (band, n_tok), 0).astype(F32).astype(BF16)
    zero = jnp.zeros((band, n_tok), BF16)
    bands = []
    for a in range(TILE_ROWS // band):
        want = jnp.where(pos_a == a, pos_b, -1.0).astype(BF16)
        hit = zero
        for k in range(n_k):
            row = jnp.broadcast_to(want[k:k + 1, :], (band, n_tok))
            fill = jnp.ones((band, n_tok), BF16) if val is None else jnp.broadcast_to(val[k:k + 1, :], (band, n_tok))
            hit = hit + jnp.where(sub == row, fill, zero)
        bands.append(hit)
    return jnp.concatenate(bands, axis=0)


def _dispatch_kernel(toff_ref, eoff_ref, n_ref, trows_ref, zoff_ref, zn_ref, tail_ref, h2_ref, posw_ref, x_ref,
                     buf, zbuf, sems, *, nt):
    tm = h2_ref.shape[0]
    half = D_MODEL // 2
    j = pl.program_id(0)
    zrows = zbuf.shape[0]

    def tile_runs(t, act):
        slot = t % 2

        def body(e, c):
            r = t * N_EXPERTS + e
            t0, d0 = toff_ref[r], eoff_ref[r]
            _run_pieces(n_ref[r], tm, lambda done, size: act(pltpu.make_async_copy(
                buf.at[slot, pl.ds(_aligned(t0 + done), size)], x_ref.at[pl.ds(_aligned(d0 + done), size)],
                sems.at[slot])))
            return c

        lax.fori_loop(0, N_EXPERTS, body, 0, unroll=RUN_UNROLL)

    def zero_fill(act):
        def body(e, c):
            d0 = zoff_ref[e]
            _run_pieces(zn_ref[e], zrows, lambda done, size: act(pltpu.make_async_copy(
                zbuf.at[pl.ds(0, size)], x_ref.at[pl.ds(_aligned(d0 + done), size)], sems.at[2])))
            return c

        lax.fori_loop(0, N_EXPERTS, body, 0)

        def tail(i, c):
            act(pltpu.make_async_copy(zbuf, x_ref.at[pl.ds(pl.multiple_of(i * zrows, zrows), zrows)], sems.at[2]))
            return c

        lax.fori_loop(tail_ref[0], x_ref.shape[0] // zrows, tail, 0)

    start = lambda cp: cp.start()
    wait = lambda cp: cp.wait()

    def wait_tile(t):
        slot = t % 2
        _run_pieces(trows_ref[t], TILE_ROWS, lambda done, size: pltpu.make_async_copy(
            buf.at[slot, pl.ds(0, size)], x_ref.at[pl.ds(0, size)], sems.at[slot]).wait())

    @pl.when(j == 0)
    def _():
        zbuf[...] = jnp.zeros(zbuf.shape, zbuf.dtype)
        zero_fill(start)

    @pl.when(j >= 2)
    def _():
        wait_tile(j - 2)

    perm = _onehot_bands(posw_ref[...].T[:TOP_K], None)
    xs = jnp.dot(perm, h2_ref[...], preferred_element_type=F32)
    buf[j % 2] = _pack_exact(xs[:, :half], xs[:, half:])
    tile_runs(j, start)

    @pl.when(j == nt - 1)
    def _():
        if nt >= 2:
            wait_tile(j - 1)
        wait_tile(j)
        zero_fill(wait)


def _dispatch(h2, posw, n_rows, tile_off, expert_off, run_n, tile_rows, zoff, zn, tail, tm):
    T = h2.shape[0]
    nt = T // tm
    assert n_rows % EXPERT_BLOCK == 0
    return pl.pallas_call(
        functools.partial(_dispatch_kernel, nt=nt),
        grid_spec=pltpu.PrefetchScalarGridSpec(
            num_scalar_prefetch=7,
            grid=(nt,),
            in_specs=[pl.BlockSpec((tm, D_MODEL), lambda i, *_: (i, 0)),
                      pl.BlockSpec((tm, LANES), lambda i, *_: (i, 0))],
            out_specs=pl.BlockSpec(memory_space=pl.ANY),
            scratch_shapes=[pltpu.VMEM((2, TILE_ROWS, D_MODEL // 2), jnp.uint32),
                            pltpu.VMEM((EXPERT_BLOCK, D_MODEL // 2), jnp.uint32),
                            pltpu.SemaphoreType.DMA((3,))]),
        out_shape=jax.ShapeDtypeStruct((n_rows, D_MODEL // 2), jnp.uint32),
        compiler_params=_params(("arbitrary",), VMEM_LIMIT),
        name="dispatch",
    )(tile_off, expert_off, run_n, tile_rows, zoff, zn, tail, h2, posw)


def _expert_kernel(be_ref, rows_ref, slot_ref, next_ref, x_ref, wg_ref, bg_ref, wu_ref, bu_ref, wd_ref, bd_ref, y_ref,
                   w_in, wg_s, wu_s, wd_s, sems):
    i = pl.program_id(0)
    half = D_MODEL // 2
    M = x_ref.shape[0]
    rows = rows_ref[i]
    e = be_ref[i]
    prev = be_ref[jnp.maximum(i - 1, 0)]

    def weight_copies(expert, slot):
        return [pltpu.make_async_copy(w_hbm.at[expert], w_in.at[slot, k], sems.at[slot, k])
                for k, w_hbm in enumerate((wg_ref, wu_ref, wd_ref))]

    @pl.when((rows > 0) & ((i == 0) | (e != prev)))
    def _():
        slot = slot_ref[i]

        @pl.when(i == 0)
        def _():
            for cp in weight_copies(e, slot):
                cp.start()

        nxt = next_ref[i]

        @pl.when(nxt >= 0)
        def _():
            for cp in weight_copies(nxt, 1 - slot):
                cp.start()

        for cp in weight_copies(e, slot):
            cp.wait()
        for k, dst in enumerate((wg_s, wu_s, wd_s)):
            dst[...] = w_in[slot, k].astype(BF16)

    def compute(r):
        x_lo, x_hi = _unpack(x_ref[:r, :])

        def proj_in(w_s, b_ref):
            return (jnp.dot(x_lo, w_s[:half, :], preferred_element_type=F32)
                    + jnp.dot(x_hi, w_s[half:, :], preferred_element_type=F32) + b_ref[0])

        gate = jnp.minimum(proj_in(wg_s, bg_ref), SWIGLU_LIMIT)
        up = jnp.clip(proj_in(wu_s, bu_ref), -SWIGLU_LIMIT, SWIGLU_LIMIT)
        glu = gate * _sigmoid(gate * SWIGLU_ALPHA)
        act = ((up + 1.0) * glu).astype(BF16)
        y = jnp.dot(act, wd_s[...], preferred_element_type=F32) + bd_ref[0]
        y_ref[:r, :] = _pack(y[:, :half], y[:, half:])
        if r < M:
            y_ref[r:, :] = jnp.zeros((M - r, half), y_ref.dtype)

    for r in range(EXPERT_SUB, M + 1, EXPERT_SUB):
        pl.when(rows == r)(functools.partial(compute, r))

    @pl.when(rows == 0)
    def _():
        y_ref[...] = jnp.zeros(y_ref.shape, y_ref.dtype)


def _experts(blk_exp, blk_rows, blk_slot, blk_next, x_rows, n_pad, w_g, b_g, w_u, b_u, w_d, b_d):
    M = EXPERT_BLOCK
    assert D_FF == D_MODEL
    bspec = lambda n: pl.BlockSpec((1, 1, n), lambda i, be, *_: (be[i], 0, 0))
    wspec = pl.BlockSpec(memory_space=pl.ANY)
    return pl.pallas_call(
        _expert_kernel,
        grid_spec=pltpu.PrefetchScalarGridSpec(
            num_scalar_prefetch=4,
            grid=(n_pad // M,),
            in_specs=[pl.BlockSpec((M, D_MODEL // 2), lambda i, *_: (i, 0)),
                      wspec, bspec(D_FF), wspec, bspec(D_FF), wspec, bspec(D_MODEL)],
            out_specs=pl.BlockSpec((M, D_MODEL // 2), lambda i, *_: (i, 0)),
            scratch_shapes=[pltpu.VMEM((2, 3, D_MODEL, D_FF), F32),
                            pltpu.VMEM((D_MODEL, D_FF), BF16),
                            pltpu.VMEM((D_MODEL, D_FF), BF16),
                            pltpu.VMEM((D_FF, D_MODEL), BF16),
                            pltpu.SemaphoreType.DMA((2, 3))]),
        out_shape=jax.ShapeDtypeStruct((n_pad, D_MODEL // 2), jnp.uint32),
        compiler_params=_params(("arbitrary",), VMEM_LIMIT),
        name="experts",
    )(blk_exp, blk_rows, blk_slot, blk_next, x_rows, w_g, b_g, w_u, b_u, w_d, b_d)


def _final_kernel(toff_ref, eoff_ref, n_ref, trows_ref, x1_ref, posw_ref, p_ref, gpg_ref, wpg_ref, wpp_ref, gpp_ref,
                  gfin_ref, y_ref, o_ref, buf, sems, *, nt):
    tm = x1_ref.shape[0]
    j = pl.program_id(0)

    def tile_runs(t, act):
        slot = t % 2

        def body(e, c):
            r = t * N_EXPERTS + e
            t0, s0 = toff_ref[r], eoff_ref[r]
            _run_pieces(n_ref[r], tm, lambda done, size: act(pltpu.make_async_copy(
                y_ref.at[pl.ds(_aligned(s0 + done), size)], buf.at[slot, pl.ds(_aligned(t0 + done), size)],
                sems.at[slot])))
            return c

        lax.fori_loop(0, N_EXPERTS, body, 0, unroll=RUN_UNROLL)

    @pl.when(j == 0)
    def _():
        buf[...] = jnp.zeros(buf.shape, buf.dtype)
        tile_runs(0, lambda cp: cp.start())

    @pl.when(j + 1 < nt)
    def _():
        tile_runs(j + 1, lambda cp: cp.start())

    _run_pieces(trows_ref[j], TILE_ROWS, lambda done, size: pltpu.make_async_copy(
        y_ref.at[pl.ds(0, size)], buf.at[j % 2, pl.ds(0, size)], sems.at[j % 2]).wait())
    posw_t = posw_ref[...].T
    comb_t = _onehot_bands(posw_t[:TOP_K], posw_t[TOP_K:2 * TOP_K].astype(BF16))
    sure = TOP_K * tm
    tail = buf[j % 2, sure:, :]
    live = lax.broadcasted_iota(jnp.int32, tail.shape, 0) < trows_ref[j] - sure
    y_lo, y_hi = _unpack(jnp.concatenate([buf[j % 2, :sure, :], jnp.where(live, tail, jnp.uint32(0))], axis=0))
    moe = jnp.concatenate([lax.dot_general(comb_t, y_lo, TN, preferred_element_type=F32),
                           lax.dot_general(comb_t, y_hi, TN, preferred_element_type=F32)], axis=-1)
    x = x1_ref[...] + moe
    pg = _sigmoid(jnp.dot(_rms(x, gpg_ref[...]).astype(BF16), wpg_ref[...], preferred_element_type=F32))
    pp = jnp.dot(p_ref[...].astype(BF16), wpp_ref[...], preferred_element_type=F32)
    x = x + pg * _rms(pp, gpp_ref[...])
    o_ref[...] = _rms(x, gfin_ref[...])


def _final(tile_off, expert_off, run_n, tile_rows, x1, y_rows, posw, p2, g_pg, w_pg, w_pp, g_pp, g_fin, tm):
    T = x1.shape[0]
    nt = T // tm
    full = lambda shape: pl.BlockSpec(shape, lambda i, *_: (0, 0))
    rowblk = lambda w: pl.BlockSpec((tm, w), lambda i, *_: (i, 0))
    return pl.pallas_call(
        functools.partial(_final_kernel, nt=nt),
        grid_spec=pltpu.PrefetchScalarGridSpec(
            num_scalar_prefetch=4,
            grid=(nt,),
            in_specs=[rowblk(D_MODEL), rowblk(LANES), rowblk(PLE_DIM),
                      full((1, D_MODEL)), full((D_MODEL, D_MODEL)), full((PLE_DIM, D_MODEL)),
                      full((1, D_MODEL)), full((1, D_MODEL)),
                      pl.BlockSpec(memory_space=pl.ANY)],
            out_specs=rowblk(D_MODEL),
            scratch_shapes=[pltpu.VMEM((2, TILE_ROWS, D_MODEL // 2), jnp.uint32),
                            pltpu.SemaphoreType.DMA((2,))]),
        out_shape=jax.ShapeDtypeStruct((T, D_MODEL), F32),
        compiler_params=_params(("arbitrary",), VMEM_LIMIT),
        name="final",
    )(tile_off, expert_off, run_n, tile_rows, x1, posw, p2, g_pg, w_pg, w_pp, g_pp, g_fin, y_rows)


def _split_w_in(w_in):
    sizes = (GLA_QK, GLA_QK, GLA_V, GLA_V, GLA_RANK, MOBA_W, MOBA_W, MOBA_W, D_MODEL, D_MODEL)
    offs = [0]
    for s in sizes:
        offs.append(offs[-1] + s)
    parts = [w_in[:, offs[i]:offs[i + 1]] for i in range(len(sizes))]
    main = jnp.concatenate(parts[:4] + parts[5:], axis=1).astype(BF16)
    alow = jnp.pad(parts[4], ((0, 0), (0, LANES - GLA_RANK))).astype(BF16)
    return main, alow


def _layer(x2, p2, bias, B, S, g_mix, w_in, w_a2, b_a2, g_gla_out, w_up_gla, w_up_moba, w_o, g_ffn, w_router,
           b_router, w_e_gate, b_e_gate, w_e_up, b_e_up, w_e_down, b_e_down, g_ple_gate, w_ple_gate, w_ple_proj,
           g_ple_proj, g_final):
    T = B * S
    row = lambda v: v.reshape(1, -1).astype(F32)
    w_main, w_alow = _split_w_in(w_in)
    colscale = jnp.ones((D_PROJ,), F32)
    colscale = colscale.at[OFF_QA:OFF_QA + GLA_QK].set(GLA_DK ** -0.5)
    colscale = colscale.at[OFF_QB:OFF_QB + MOBA_W].set(MOBA_HD ** -0.5 * LOG2E)
    w_a2p = jnp.pad(w_a2, ((0, LANES - GLA_RANK), (0, 0))).astype(BF16)
    proj, glog = _inproj(x2, row(g_mix), w_main, colscale.reshape(1, -1), w_alow, w_a2p, row(b_a2))

    o_a = _gla(proj, glog, row(g_gla_out), B, S)
    o_b = _moba(proj, bias, B, S)

    w_r = jnp.pad(w_router.astype(F32), ((0, 0), (0, LANES - N_EXPERTS)))
    w_r_hi = w_r.astype(BF16)
    w_r2 = jnp.concatenate([w_r_hi, (w_r - w_r_hi.astype(F32)).astype(BF16)], axis=1)
    b_r = jnp.pad(b_router.astype(F32), (0, LANES - N_EXPERTS)).reshape(1, -1)
    tm = TOKEN_TILE
    nt = T // tm
    x1, h2, logits = _merge(o_a, o_b, proj, x2, w_up_gla.astype(BF16), w_up_moba.astype(BF16),
                            w_o.astype(BF16), row(g_ffn), w_r2, b_r, MERGE_TILE)
    posw, cnt_t, carry = _router(logits, tm)

    M = EXPERT_BLOCK
    A = nt * TILE_ROWS
    n_pad = (-(-A // M)) * M + N_EXPERTS * M
    n_blk = n_pad // M
    cnt_t = cnt_t[:, 0, :N_EXPERTS].astype(jnp.int32)
    carry = carry[:, 0, :N_EXPERTS].astype(jnp.int32)
    counts = carry[-1] + cnt_t[-1]
    padded = (counts + M - 1) // M * M
    pad_end = jnp.cumsum(padded)
    pad_start = pad_end - padded
    blk_exp = jnp.minimum(jnp.sum(pad_end[None, :] <= (jnp.arange(n_blk, dtype=jnp.int32) * M)[:, None], axis=1),
                          N_EXPERTS - 1).astype(jnp.int32)
    n_used = (pad_end[-1:] // M).astype(jnp.int32)
    blk_start = jnp.arange(n_blk, dtype=jnp.int32) * M
    eids = jnp.arange(N_EXPERTS, dtype=jnp.int32)

    def per_block(per_expert):
        return jnp.sum(jnp.where(blk_exp[:, None] == eids[None, :], per_expert[None, :], 0), axis=1).astype(jnp.int32)

    blk_rows = jnp.clip(per_block(pad_start + counts) - blk_start, 0, M)
    blk_rows = jnp.where(blk_start < pad_end[-1], (blk_rows + EXPERT_SUB - 1) // EXPERT_SUB * EXPERT_SUB, 0)
    has_rows = counts > 0
    blk_slot = per_block((jnp.cumsum(has_rows) - 1) % 2)
    later = jnp.where((eids[None, :] > eids[:, None]) & has_rows[None, :], eids[None, :], N_EXPERTS)
    next_exp = jnp.min(later, axis=1)
    blk_next = per_block(jnp.where(next_exp < N_EXPERTS, next_exp, -1))
    tile_off = (jnp.cumsum(cnt_t, axis=1) - cnt_t).reshape(-1)
    expert_off = (carry + pad_start[None, :]).reshape(-1)
    run_n = cnt_t.reshape(-1)
    tile_rows = jnp.sum(cnt_t, axis=1)
    x_rows = _dispatch(h2, posw, n_pad, tile_off, expert_off, run_n, tile_rows, pad_start + counts, padded - counts,
                       n_used, tm)
    y_rows = _experts(blk_exp, blk_rows.astype(jnp.int32), blk_slot, blk_next, x_rows, n_pad,
                      w_e_gate, b_e_gate.reshape(N_EXPERTS, 1, D_FF),
                      w_e_up, b_e_up.reshape(N_EXPERTS, 1, D_FF), w_e_down,
                      b_e_down.reshape(N_EXPERTS, 1, D_MODEL))
    return _final(tile_off, expert_off, run_n, tile_rows, x1, y_rows, posw, p2, row(g_ple_gate),
                  w_ple_gate.astype(BF16), w_ple_proj.astype(BF16), row(g_ple_proj), row(g_final), tm)


def kernel(x, p, rel_bias, g_mix, w_in, w_a2, b_a2, g_gla_out, w_up_gla, w_up_moba, w_o, g_ffn, w_router, b_router,
           w_e_gate, b_e_gate, w_e_up, b_e_up, w_e_down, b_e_down, g_ple_gate, w_ple_gate, w_ple_proj, g_ple_proj,
           g_final):
    B, S, D = x.shape
    assert D == D_MODEL and S % MOBA_BLOCK == 0 and S % GLA_CHUNK == 0 and p.shape[0] == 1
    bias = _bias_tiles(rel_bias, S // MOBA_BLOCK)
    out = _layer(x.reshape(B * S, D), p[0].reshape(B * S, PLE_DIM), bias, B, S,
                 g_mix[0], w_in[0], w_a2[0], b_a2[0], g_gla_out[0], w_up_gla[0], w_up_moba[0], w_o[0], g_ffn[0],
                 w_router[0], b_router[0], w_e_gate[0], b_e_gate[0], w_e_up[0], b_e_up[0], w_e_down[0],
                 b_e_down[0], g_ple_gate[0], w_ple_gate[0], w_ple_proj[0], g_ple_proj[0], g_final)
    return out.reshape(B, S, D)
```

```python
import functools
import math

import jax
import jax.numpy as jnp
from jax import lax
from jax.experimental import pallas as pl
from jax.experimental.pallas import tpu as pltpu

F32 = jnp.float32
BF16 = jnp.bfloat16

D_MODEL = 1024
PLE_DIM = 256
GLA_HEADS = 4
GLA_DK = 128
GLA_DV = 256
GLA_RANK = 16
GLA_TAU = 16.0
GLA_QK = GLA_HEADS * GLA_DK
GLA_V = GLA_HEADS * GLA_DV
MOBA_HEADS = 8
MOBA_HD = 128
MOBA_BLOCK = 256
MOBA_TOPK = 3
MOBA_W = MOBA_HEADS * MOBA_HD
REL_BUCKETS = 32
REL_MAX_DIST = 4096
N_EXPERTS = 32
TOP_K = 4
D_FF = 1024
SWIGLU_LIMIT = 7.0
SWIGLU_ALPHA = 1.702
EPS = 1e-6

LANES = 128
NEG = -1e30
LOG2E = math.log2(math.e)
VMEM_LIMIT = 56 * 1024 * 1024

OFF_QA, OFF_KA, OFF_VA, OFF_RA = 0, 512, 1024, 2048
OFF_QB, OFF_KB, OFF_VB, OFF_GA, OFF_GB = 3072, 4096, 5120, 6144, 7168
D_PROJ = 8192

GLA_CHUNK = 128
EXPERT_BLOCK = 512
EXPERT_SUB = 128
TOKEN_TILE = 256
RUN_ALIGN = 8
TILE_ROWS = TOP_K * TOKEN_TILE + N_EXPERTS * RUN_ALIGN
RUN_UNROLL = 8
MERGE_TILE = 512

NT = (((1,), (1,)), ((), ()))
TN = (((0,), (0,)), ((), ()))


def _params(sem, vmem=None):
    return pltpu.CompilerParams(dimension_semantics=sem, vmem_limit_bytes=vmem)


def _rms(x, g):
    return x * lax.rsqrt(jnp.mean(x * x, axis=-1, keepdims=True) + EPS) * g


def _sigmoid(x):
    return 1.0 / (1.0 + jnp.exp(-x))


def _bucket_of(n):
    max_exact = REL_BUCKETS // 2
    if n < max_exact:
        return n
    return min(max_exact + int(math.log(n / max_exact) / math.log(REL_MAX_DIST / max_exact)
                               * (REL_BUCKETS - max_exact)), REL_BUCKETS - 1)


def _bias_kernel(tab_ref, bkt_ref, o_ref, *, nblk):
    strip = 8
    m = pl.program_id(0)

    def fill(lo, hi):
        def body(s, carry):
            r0 = pl.multiple_of(s * strip, strip)
            b = bkt_ref[0, pl.ds(r0, strip), :]
            accs = [jnp.zeros(b.shape, F32) for _ in range(MOBA_HEADS)]
            for bb in range(lo, hi + 1):
                hit = b == bb
                for h in range(MOBA_HEADS):
                    accs[h] = jnp.where(hit, tab_ref[h, bb], accs[h])
            for h in range(MOBA_HEADS):
                o_ref[h, 0, pl.ds(r0, strip), :] = jnp.where(b < 0, NEG, accs[h])
            return carry

        lax.fori_loop(0, MOBA_BLOCK // strip, body, 0)

    for mm in range(nblk):
        lo = max(_bucket_of(max(mm * MOBA_BLOCK - (MOBA_BLOCK - 1), 0)) - 1, 0)
        hi = min(_bucket_of(mm * MOBA_BLOCK + MOBA_BLOCK - 1) + 1, REL_BUCKETS - 1)
        pl.when(m == mm)(functools.partial(fill, lo, hi))


def _bias_tiles(rel_bias, nblk):
    i = jnp.arange(MOBA_BLOCK, dtype=jnp.int32)
    dist = (jnp.arange(nblk, dtype=jnp.int32)[:, None, None] * MOBA_BLOCK + i[None, :, None] - i[None, None, :])
    n = jnp.maximum(dist, 0)
    max_exact = REL_BUCKETS // 2
    nf = jnp.maximum(n, 1).astype(F32)
    large = max_exact + (jnp.log(nf / max_exact) / math.log(REL_MAX_DIST / max_exact)
                         * (REL_BUCKETS - max_exact)).astype(jnp.int32)
    large = jnp.minimum(large, REL_BUCKETS - 1)
    bkt = jnp.where(dist < 0, -1, jnp.where(n < max_exact, n, large)).astype(jnp.int32)
    tab = rel_bias.astype(F32).T * LOG2E
    return pl.pallas_call(
        functools.partial(_bias_kernel, nblk=nblk),
        grid=(nblk,),
        in_specs=[pl.BlockSpec(memory_space=pltpu.SMEM),
                  pl.BlockSpec((1, MOBA_BLOCK, MOBA_BLOCK), lambda m: (m, 0, 0))],
        out_specs=pl.BlockSpec((MOBA_HEADS, 1, MOBA_BLOCK, MOBA_BLOCK), lambda m: (0, m, 0, 0)),
        out_shape=jax.ShapeDtypeStruct((MOBA_HEADS, nblk, MOBA_BLOCK, MOBA_BLOCK), F32),
        compiler_params=_params(("arbitrary",)),
        name="bias_tiles",
    )(tab, bkt)


def _inproj_kernel(x0_ref, xn_ref, g_ref, w_ref, cs_ref, wal_ref, wa2_ref, ba2_ref, o_ref, glog_ref, h_scr, al_scr,
                   *, n_j):
    i, j = pl.program_id(0), pl.program_id(1)
    tm = xn_ref.shape[0]
    rs = tm // n_j

    def gate_logits(a_low):
        a = jnp.dot(a_low, wa2_ref[...], preferred_element_type=F32) + ba2_ref[...]
        log_sig = jnp.minimum(a, 0.0) - jnp.log1p(jnp.exp(-jnp.abs(a)))
        return log_sig * (1.0 / GLA_TAU)

    def slice_rows(behind):
        return pl.multiple_of(((j + n_j - behind) % n_j) * rs, rs)

    @pl.when((i == 0) & (j == 0))
    def _():
        h0 = _rms(x0_ref[...], g_ref[...]).astype(BF16)
        h_scr[0] = h0
        a_low0 = jnp.dot(h0, wal_ref[...], preferred_element_type=F32).astype(BF16)
        glog_ref[...] = gate_logits(a_low0)
        al_scr[...] = a_low0[(n_j - 2) * rs:(n_j - 1) * rs, :]

    glog_ref[pl.ds(slice_rows(2), rs), :] = gate_logits(al_scr[...])
    slot = jnp.where(j == 0, i, i + 1) % 2
    h_lag = h_scr[slot, pl.ds(slice_rows(1), rs), :]
    al_scr[...] = jnp.dot(h_lag, wal_ref[...], preferred_element_type=F32).astype(BF16)

    acc = jnp.dot(h_scr[i % 2], w_ref[...], preferred_element_type=F32)
    o_ref[...] = (acc * cs_ref[...]).astype(BF16)

    h_scr[(i + 1) % 2, pl.ds(slice_rows(0), rs), :] = _rms(xn_ref[pl.ds(slice_rows(0), rs), :],
                                                             g_ref[...]).astype(BF16)


def _inproj(x2, g_mix, w_main, colscale, w_alow, w_a2p, b_a2, tm=1024, tn=2048):
    T = x2.shape[0]
    n_i, n_j = T // tm, D_PROJ // tn
    assert n_i >= 2 and tm % (8 * n_j) == 0
    return pl.pallas_call(
        functools.partial(_inproj_kernel, n_j=n_j),
        grid=(n_i, n_j),
        in_specs=[pl.BlockSpec((tm, D_MODEL), lambda i, j: (0, 0)),
                  pl.BlockSpec((tm, D_MODEL), lambda i, j: (jnp.minimum(i + 1, n_i - 1), 0)),
                  pl.BlockSpec((1, D_MODEL), lambda i, j: (0, 0)),
                  pl.BlockSpec((D_MODEL, tn), lambda i, j: (0, j)),
                  pl.BlockSpec((1, tn), lambda i, j: (0, j)),
                  pl.BlockSpec((D_MODEL, LANES), lambda i, j: (0, 0)),
                  pl.BlockSpec((LANES, GLA_QK), lambda i, j: (0, 0)),
                  pl.BlockSpec((1, GLA_QK), lambda i, j: (0, 0))],
        out_specs=[pl.BlockSpec((tm, tn), lambda i, j: (i, j)),
                   pl.BlockSpec((tm, GLA_QK), lambda i, j: (jnp.minimum(i + jnp.minimum(j // 2, 1), n_i - 1), 0))],
        out_shape=[jax.ShapeDtypeStruct((T, D_PROJ), BF16),
                   jax.ShapeDtypeStruct((T, GLA_QK), F32)],
        scratch_shapes=[pltpu.VMEM((2, tm, D_MODEL), BF16), pltpu.VMEM((tm // n_j, LANES), BF16)],
        compiler_params=_params(("arbitrary", "arbitrary"), VMEM_LIMIT),
        name="inproj",
    )(x2, x2, g_mix, w_main, colscale, w_alow, w_a2p, b_a2)


def _gla_kernel(q_ref, k_ref, v_ref, r_ref, g_ref, gout_ref, o_ref, st_ref):
    C = GLA_CHUNK

    @pl.when(pl.program_id(1) == 0)
    def _():
        st_ref[...] = jnp.zeros(st_ref.shape, F32)

    nb = q_ref.shape[0]
    row = lax.broadcasted_iota(jnp.int32, (C, C), 0)
    col = lax.broadcasted_iota(jnp.int32, (C, C), 1)
    causal = col <= row
    ltri = causal.astype(BF16)
    mid = C // 2
    pairs = [(b, h) for b in range(nb) for h in range(GLA_HEADS)]
    ks = lambda h: slice(h * GLA_DK, (h + 1) * GLA_DK)
    vs = lambda h: slice(h * GLA_DV, (h + 1) * GLA_DV)
    gate = {}
    for b, h in pairs:
        r = r_ref[b, :, vs(h)].astype(F32)
        gate[b, h] = r * _sigmoid(r)
    G = []
    for b in range(nb):
        g = g_ref[b]
        g_hi = g.astype(BF16)
        g_lo = (g - g_hi.astype(F32)).astype(BF16)
        G.append(jnp.dot(ltri, g_hi, preferred_element_type=F32) + jnp.dot(ltri, g_lo, preferred_element_type=F32))
    Gh = {(b, h): G[b][:, ks(h)] for b, h in pairs}
    qh = {(b, h): q_ref[b, :, ks(h)].astype(F32) for b, h in pairs}
    kh = {(b, h): k_ref[b, :, ks(h)].astype(F32) for b, h in pairs}
    g_mid = {p: Gh[p][mid:mid + 1, :] for p in pairs}
    g_last = {p: Gh[p][C - 1:C, :] for p in pairs}
    A = {p: lax.dot_general((qh[p] * jnp.exp(Gh[p] - g_mid[p])).astype(BF16),
                            (kh[p] * jnp.exp(g_mid[p] - Gh[p])).astype(BF16), NT, preferred_element_type=F32)
         for p in pairs}
    st = {p: st_ref[p[0], p[1]] for p in pairs}
    inter = {p: lax.dot_general((qh[p] * jnp.exp(Gh[p])).astype(BF16), st[p].astype(BF16), NT,
                                preferred_element_type=F32) for p in pairs}
    for b, h in pairs:
        p = (b, h)
        k_d = (kh[p] * jnp.exp(g_last[p] - Gh[p])).astype(BF16)
        st_ref[b, h] = jnp.exp(g_last[p]) * st[p] + lax.dot_general(v_ref[b, :, vs(h)], k_d, TN,
                                                                     preferred_element_type=F32)
    intra = {(b, h): jnp.dot(jnp.where(causal, A[b, h], 0.0).astype(BF16), v_ref[b, :, vs(h)],
                             preferred_element_type=F32) for b, h in pairs}
    for b, h in pairs:
        o = inter[b, h] + intra[b, h]
        o_ref[b, :, vs(h)] = (_rms(o, gout_ref[...]) * gate[b, h]).astype(BF16)


def _gla(proj, glog, g_gla_out, B, S, nb=4):
    C = GLA_CHUNK
    assert B % nb == 0
    proj3 = proj.reshape(B, S, D_PROJ)
    glog3 = glog.reshape(B, S, GLA_QK)
    spec = lambda w, off: pl.BlockSpec((nb, C, w), lambda b, c: (b, c, off // w))
    out = pl.pallas_call(
        _gla_kernel,
        grid=(B // nb, S // C),
        in_specs=[spec(GLA_QK, OFF_QA), spec(GLA_QK, OFF_KA), spec(GLA_V, OFF_VA), spec(GLA_V, OFF_RA),
                  spec(GLA_QK, 0), pl.BlockSpec((1, GLA_DV), lambda b, c: (0, 0))],
        out_specs=spec(GLA_V, 0),
        out_shape=jax.ShapeDtypeStruct((B, S, GLA_V), BF16),
        scratch_shapes=[pltpu.VMEM((nb, GLA_HEADS, GLA_DV, GLA_DK), F32)],
        compiler_params=_params(("arbitrary", "arbitrary")),
        name="gla",
    )(proj3, proj3, proj3, proj3, glog3, g_gla_out)
    return out.reshape(B * S, GLA_V)


def _moba_kernel(q_ref, k_ref, v_ref, bias_ref, o_ref, ka_scr, va_scr, lg_scr, mx_scr, qa_scr, *, nblk):
    BLK, HD = MOBA_BLOCK, MOBA_HD
    S = nblk * BLK

    @pl.when((pl.program_id(0) == 0) & (pl.program_id(1) == 0))
    def _():
        blk = lax.broadcasted_iota(jnp.int32, (S, HD), 0) // BLK
        lane = lax.broadcasted_iota(jnp.int32, (S, HD), 1)
        ka_scr[:, HD:] = (lane == blk).astype(BF16)
        va_scr[:, HD:] = (lane == 0).astype(BF16)

    n_plain = min(MOBA_TOPK + 1, nblk)

    def tile_id(cc, j):
        return cc * (cc + 1) // 2 + j

    def stage1(cc, q_in, keys):
        mx = None
        for j in range(cc + 1):
            lg = (lax.dot_general(q_in, keys[j * BLK:(j + 1) * BLK, :], NT, preferred_element_type=F32)
                  + bias_ref[0, cc - j])
            lg_scr[tile_id(cc, j)] = lg
            t = jnp.maximum(lg[:, :LANES], lg[:, LANES:])
            mx = t if mx is None else jnp.maximum(mx, t)
            if j == cc:
                mx_scr[cc] = mx
            yield

    def stage2(cc):
        m = mx_scr[cc].max(axis=-1, keepdims=True)
        acc = jnp.zeros((BLK, 2 * HD), F32)
        for j in range(cc + 1):
            p = jnp.exp2(lg_scr[tile_id(cc, j)] - m).astype(BF16)
            acc = acc + jnp.dot(p, va_scr[j * BLK:(j + 1) * BLK, :], preferred_element_type=F32)
            if j == cc:
                o_ref[cc * BLK:(cc + 1) * BLK, :] = (acc[:, :HD] / acc[:, HD:HD + 1]).astype(BF16)
            yield

    def drain(gen):
        for _ in gen:
            pass

    def interleave(main, side, n_main, n_side):
        side_steps = (s for g in side for s in g)
        done = 0
        for i, _ in enumerate(s for g in main for s in g):
            assert n_main > 0
            want = (i + 1) * n_side // n_main
            while done < want and next(side_steps, "end") != "end":
                done += 1
        drain(side_steps)

    plain = [stage1(cc, q_ref[cc * BLK:(cc + 1) * BLK, :], k_ref) for cc in range(n_plain)]
    if nblk > n_plain:
        next(plain[0])
        ksum = [k_ref[j * BLK:(j + 1) * BLK, :].astype(F32).reshape(BLK // 8, 8, HD).sum(axis=0).sum(
            axis=0, keepdims=True) for j in range(nblk)]
        kmean = jnp.concatenate(ksum, axis=0) * (1.0 / BLK)
        km_hi = kmean.astype(BF16)
        km_lo = (kmean - km_hi.astype(F32)).astype(BF16)
        km2 = jnp.concatenate([km_hi, km_lo], axis=0)
        pens = []
        for cc in range(n_plain, nblk):
            q = q_ref[cc * BLK:(cc + 1) * BLK, :]
            s2 = lax.dot_general(km2, q, NT, preferred_element_type=F32)
            pens.append((cc, q, s2[:nblk] + s2[nblk:]))
        for g in plain[:2]:
            drain(g)
        for cc, q, s in pens:
            ji = lax.broadcasted_iota(jnp.int32, s.shape, 0)
            cnt = jnp.zeros(s.shape, F32)
            for jp in range(cc):
                sj = s[jp:jp + 1, :]
                beats = (sj > s) | ((sj == s) & (jp < ji))
                cnt = cnt + beats.astype(F32)
            pen = jnp.where((ji < cc) & (cnt >= MOBA_TOPK), NEG, 0.0)
            pen_t = jnp.concatenate([pen, jnp.zeros((HD - nblk, BLK), F32)], axis=0).T
            qa_scr[cc - n_plain] = jnp.concatenate([q, pen_t.astype(BF16)], axis=1)
    ka_scr[:, :HD] = k_ref[...]
    va_scr[:, :HD] = v_ref[...]
    for g in plain:
        drain(g)

    @pl.when(pl.program_id(0) >= 0)
    def _():
        late = [stage1(cc, qa_scr[cc - n_plain], ka_scr) for cc in range(n_plain, nblk)]
        early = [stage2(cc) for cc in range(n_plain)]
        interleave(late, early, tile_id(nblk, 0) - tile_id(n_plain, 0), tile_id(n_plain, 0))

    @pl.when(pl.program_id(0) >= 0)
    def _():
        for cc in range(n_plain, nblk):
            drain(stage2(cc))


def _moba(proj, bias, B, S):
    BLK = MOBA_BLOCK
    nblk = S // BLK
    H = MOBA_HEADS
    assert nblk <= MOBA_HD
    return pl.pallas_call(
        functools.partial(_moba_kernel, nblk=nblk),
        grid=(H, B),
        in_specs=[pl.BlockSpec((S, MOBA_HD), lambda h, b: (b, OFF_QB // MOBA_HD + h)),
                  pl.BlockSpec((S, MOBA_HD), lambda h, b: (b, OFF_KB // MOBA_HD + h)),
                  pl.BlockSpec((S, MOBA_HD), lambda h, b: (b, OFF_VB // MOBA_HD + h)),
                  pl.BlockSpec((1, nblk, BLK, BLK), lambda h, b: (h, 0, 0, 0))],
        out_specs=pl.BlockSpec((S, MOBA_HD), lambda h, b: (b, h)),
        out_shape=jax.ShapeDtypeStruct((B * S, MOBA_W), BF16),
        scratch_shapes=[pltpu.VMEM((S, 2 * MOBA_HD), BF16), pltpu.VMEM((S, 2 * MOBA_HD), BF16),
                        pltpu.VMEM((nblk * (nblk + 1) // 2, BLK, BLK), F32),
                        pltpu.VMEM((nblk, BLK, LANES), F32),
                        pltpu.VMEM((max(nblk - MOBA_TOPK - 1, 1), BLK, 2 * MOBA_HD), BF16)],
        compiler_params=_params(("arbitrary", "arbitrary"), VMEM_LIMIT),
        name="moba",
    )(proj, proj, proj, bias)


def _pack(lo, hi):
    lo_b = lax.bitcast_convert_type(lo.astype(BF16).astype(F32), jnp.uint32)
    hi_b = lax.bitcast_convert_type(hi.astype(BF16).astype(F32), jnp.uint32)
    return (lo_b >> 16) | (hi_b & jnp.uint32(0xFFFF0000))


def _pack_exact(lo, hi):
    lo_b = lax.bitcast_convert_type(lo, jnp.uint32)
    hi_b = lax.bitcast_convert_type(hi, jnp.uint32)
    return (lo_b >> 16) | (hi_b & jnp.uint32(0xFFFF0000))


def _unpack(w):
    lo = lax.bitcast_convert_type(w << 16, F32)
    hi = lax.bitcast_convert_type(w & jnp.uint32(0xFFFF0000), F32)
    return lo.astype(BF16), hi.astype(BF16)


def _merge_kernel(oa_ref, ob_ref, ga_ref, gb_ref, x_ref, wua_ref, wub_ref, wo_ref, gffn_ref, wr_ref, br_ref,
                  x1_ref, h2_ref, lg_ref):
    tm = x_ref.shape[0]
    u_a = jnp.dot(oa_ref[...], wua_ref[...], preferred_element_type=F32)
    u_b = jnp.dot(ob_ref[...], wub_ref[...], preferred_element_type=F32)
    y = _sigmoid(ga_ref[...].astype(F32)) * u_a + _sigmoid(gb_ref[...].astype(F32)) * u_b
    x1 = x_ref[...] + jnp.dot(y.astype(BF16), wo_ref[...], preferred_element_type=F32)
    x1_ref[...] = x1
    h2 = _rms(x1, gffn_ref[...])
    h_hi = h2.astype(BF16)
    h2_ref[...] = h_hi
    h_lo = (h2 - h_hi.astype(F32)).astype(BF16)
    r = jnp.dot(jnp.concatenate([h_hi, h_lo], axis=0), wr_ref[...], preferred_element_type=F32)
    lg_ref[...] = r[:tm, :LANES] + r[:tm, LANES:] + r[tm:, :LANES] + br_ref[...]


def _merge(o_a, o_b, proj, x2, w_ua, w_ub, w_o, g_ffn, w_r2, b_r, tm):
    T = x2.shape[0]
    full = lambda shape: pl.BlockSpec(shape, lambda i: (0, 0))
    rowblk = lambda w: pl.BlockSpec((tm, w), lambda i: (i, 0))
    return pl.pallas_call(
        _merge_kernel,
        grid=(T // tm,),
        in_specs=[rowblk(GLA_V), rowblk(MOBA_W),
                  pl.BlockSpec((tm, D_MODEL), lambda i: (i, OFF_GA // D_MODEL)),
                  pl.BlockSpec((tm, D_MODEL), lambda i: (i, OFF_GB // D_MODEL)),
                  rowblk(D_MODEL),
                  full((GLA_V, D_MODEL)), full((MOBA_W, D_MODEL)), full((D_MODEL, D_MODEL)),
                  full((1, D_MODEL)), full((D_MODEL, 2 * LANES)), full((1, LANES))],
        out_specs=[rowblk(D_MODEL), rowblk(D_MODEL), rowblk(LANES)],
        out_shape=[jax.ShapeDtypeStruct((T, D_MODEL), F32),
                   jax.ShapeDtypeStruct((T, D_MODEL), BF16),
                   jax.ShapeDtypeStruct((T, LANES), F32)],
        compiler_params=_params(("arbitrary",), VMEM_LIMIT),
        name="merge",
    )(o_a, o_b, proj, proj, x2, w_ua, w_ub, w_o, g_ffn, w_r2, b_r)


def _router_kernel(lg_ref, posw_ref, cnt_ref, carry_ref, cnt_scr, *, tm):
    rows = lg_ref.shape[0]

    @pl.when(pl.program_id(0) == 0)
    def _():
        cnt_scr[...] = jnp.zeros(cnt_scr.shape, F32)

    lane = lax.broadcasted_iota(jnp.int32, (rows, LANES), 1)
    lane_f = lane.astype(F32)
    work = jnp.where(lane < N_EXPERTS, lg_ref[...], NEG)
    vals, hots = [], []
    for _ in range(TOP_K):
        mx = work.max(axis=-1, keepdims=True)
        idx = jnp.min(jnp.where(work == mx, lane_f, float(LANES)), axis=-1, keepdims=True)
        hot = lane_f == idx
        vals.append(mx)
        hots.append(hot)
        work = jnp.where(hot, 2.0 * NEG, work)
    exps = [jnp.exp(v - vals[0]) for v in vals]
    den = exps[0] + exps[1] + exps[2] + exps[3]
    sel = jnp.zeros((rows, LANES), F32)
    for hot in hots:
        sel = sel + hot.astype(F32)
    row = lax.broadcasted_iota(jnp.int32, (tm, tm), 0)
    col = lax.broadcasted_iota(jnp.int32, (tm, tm), 1)
    below = (col < row).astype(BF16)
    er = lax.broadcasted_iota(jnp.int32, (LANES, LANES), 0)
    ec = lax.broadcasted_iota(jnp.int32, (LANES, LANES), 1)
    before = (er < ec).astype(F32)
    pos_parts = []
    for t in range(rows // tm):
        sel_t = sel[t * tm:(t + 1) * tm]
        local_rank = jnp.dot(below, sel_t.astype(BF16), preferred_element_type=F32)
        cnt_t = sel_t.sum(axis=0, keepdims=True)
        cnt_t = jnp.floor((cnt_t + (RUN_ALIGN - 1.0)) * (1.0 / RUN_ALIGN)) * RUN_ALIGN
        tile_off = jnp.dot(jnp.broadcast_to(cnt_t, (8, LANES)), before, preferred_element_type=F32,
                           precision=lax.Precision.HIGHEST)[0:1]
        pos_parts.append(local_rank + tile_off)
        carry_ref[t] = cnt_scr[...]
        cnt_ref[t] = cnt_t
        cnt_scr[...] = cnt_scr[...] + cnt_t
    pos_all = jnp.concatenate(pos_parts, axis=0)
    posw = jnp.zeros((rows, LANES), F32)
    for k in range(TOP_K):
        pk = jnp.sum(jnp.where(hots[k], pos_all, 0.0), axis=-1, keepdims=True)
        posw = jnp.where(lane == k, pk, posw)
        posw = jnp.where(lane == TOP_K + k, exps[k] / den, posw)
    posw_ref[...] = posw


def _router(logits, tm, tiles_per_step=4):
    T = logits.shape[0]
    nt = T // tm
    rows = tm * tiles_per_step
    tilerow = pl.BlockSpec((tiles_per_step, 1, LANES), lambda i: (i, 0, 0))
    return pl.pallas_call(
        functools.partial(_router_kernel, tm=tm),
        grid=(T // rows,),
        in_specs=[pl.BlockSpec((rows, LANES), lambda i: (i, 0))],
        out_specs=[pl.BlockSpec((rows, LANES), lambda i: (i, 0)), tilerow, tilerow],
        out_shape=[jax.ShapeDtypeStruct((T, LANES), F32),
                   jax.ShapeDtypeStruct((nt, 1, LANES), F32),
                   jax.ShapeDtypeStruct((nt, 1, LANES), F32)],
        scratch_shapes=[pltpu.VMEM((1, LANES), F32)],
        compiler_params=_params(("arbitrary",)),
        name="router",
    )(logits)


def _run_pieces(n, max_rows, fn):
    for b in reversed(range(RUN_ALIGN.bit_length() - 1, max_rows.bit_length())):
        size = 1 << b
        done = n & ~((2 << b) - 1)

        @pl.when((n & size) != 0)
        def _():
            fn(done, size)


def _run_copy(n, max_rows, fn, single=8):
    q = n // RUN_ALIGN

    def tree(lo, hi):
        if lo == hi:
            fn(0, lo * RUN_ALIGN)
            return
        mid = (lo + hi) // 2
        pl.when(q <= mid)(functools.partial(tree, lo, mid))
        pl.when(q > mid)(functools.partial(tree, mid + 1, hi))

    pl.when((q >= 1) & (q <= single))(functools.partial(tree, 1, single))
    pl.when(q > single)(functools.partial(_run_pieces, n, max_rows, fn))


def _aligned(i):
    return pl.multiple_of(i, RUN_ALIGN)


def _onehot_bands(pos, val):
    band = 256
    assert TILE_ROWS % band == 0
    n_k, n_tok = pos.shape
    pos_a = jnp.floor(pos * (1.0 / band))
    pos_b = pos - band * pos_a
    sub = lax.broadcasted_iota(jnp.int32, (band, n_tok), 0).astype(F32).astype(BF16)
    zero = jnp.zeros((band, n_tok), BF16)
    bands = []
    for a in range(TILE_ROWS // band):
        want = jnp.where(pos_a == a, pos_b, -1.0).astype(BF16)
        hit = zero
        for k in range(n_k):
            row = jnp.broadcast_to(want[k:k + 1, :], (band, n_tok))
            fill = jnp.ones((band, n_tok), BF16) if val is None else jnp.broadcast_to(val[k:k + 1, :], (band, n_tok))
            hit = hit + jnp.where(sub == row, fill, zero)
        bands.append(hit)
    return jnp.concatenate(bands, axis=0)


def _dispatch_kernel(toff_ref, eoff_ref, n_ref, trows_ref, zoff_ref, zn_ref, tail_ref, h2_ref, posw_ref, x_ref,
                     buf, zbuf, sems, *, nt):
    tm = h2_ref.shape[0]
    half = D_MODEL // 2
    j = pl.program_id(0)
    zrows = zbuf.shape[0]

    def tile_runs(t, act):
        slot = t % 2

        def body(e, c):
            r = t * N_EXPERTS + e
            t0, d0 = toff_ref[r], eoff_ref[r]
            _run_copy(n_ref[r], tm, lambda done, size: act(pltpu.make_async_copy(
                buf.at[slot, pl.ds(_aligned(t0 + done), size)], x_ref.at[pl.ds(_aligned(d0 + done), size)],
                sems.at[slot])))
            return c

        lax.fori_loop(0, N_EXPERTS, body, 0, unroll=RUN_UNROLL)

    def zero_fill(act):
        def body(e, c):
            d0 = zoff_ref[e]
            _run_pieces(zn_ref[e], zrows, lambda done, size: act(pltpu.make_async_copy(
                zbuf.at[pl.ds(0, size)], x_ref.at[pl.ds(_aligned(d0 + done), size)], sems.at[2])))
            return c

        lax.fori_loop(0, N_EXPERTS, body, 0)

        def tail(i, c):
            act(pltpu.make_async_copy(zbuf, x_ref.at[pl.ds(pl.multiple_of(i * zrows, zrows), zrows)], sems.at[2]))
            return c

        lax.fori_loop(tail_ref[0], x_ref.shape[0] // zrows, tail, 0)

    start = lambda cp: cp.start()
    wait = lambda cp: cp.wait()

    def wait_tile(t):
        slot = t % 2
        _run_pieces(trows_ref[t], TILE_ROWS, lambda done, size: pltpu.make_async_copy(
            buf.at[slot, pl.ds(0, size)], x_ref.at[pl.ds(0, size)], sems.at[slot]).wait())

    @pl.when(j == 0)
    def _():
        zbuf[...] = jnp.zeros(zbuf.shape, zbuf.dtype)
        zero_fill(start)

    @pl.when(j >= 2)
    def _():
        wait_tile(j - 2)

    perm = _onehot_bands(posw_ref[...].T[:TOP_K], None)
    xs = jnp.dot(perm, h2_ref[...], preferred_element_type=F32)
    buf[j % 2] = _pack_exact(xs[:, :half], xs[:, half:])
    tile_runs(j, start)

    @pl.when(j == nt - 1)
    def _():
        if nt >= 2:
            wait_tile(j - 1)
        wait_tile(j)
        zero_fill(wait)


def _dispatch(h2, posw, n_rows, tile_off, expert_off, run_n, tile_rows, zoff, zn, tail, tm):
    T = h2.shape[0]
    nt = T // tm
    assert n_rows % EXPERT_BLOCK == 0
    return pl.pallas_call(
        functools.partial(_dispatch_kernel, nt=nt),
        grid_spec=pltpu.PrefetchScalarGridSpec(
            num_scalar_prefetch=7,
            grid=(nt,),
            in_specs=[pl.BlockSpec((tm, D_MODEL), lambda i, *_: (i, 0)),
                      pl.BlockSpec((tm, LANES), lambda i, *_: (i, 0))],
            out_specs=pl.BlockSpec(memory_space=pl.ANY),
            scratch_shapes=[pltpu.VMEM((2, TILE_ROWS, D_MODEL // 2), jnp.uint32),
                            pltpu.VMEM((EXPERT_BLOCK, D_MODEL // 2), jnp.uint32),
                            pltpu.SemaphoreType.DMA((3,))]),
        out_shape=jax.ShapeDtypeStruct((n_rows, D_MODEL // 2), jnp.uint32),
        compiler_params=_params(("arbitrary",), VMEM_LIMIT),
        name="dispatch",
    )(tile_off, expert_off, run_n, tile_rows, zoff, zn, tail, h2, posw)


def _expert_kernel(be_ref, rows_ref, slot_ref, next_ref, x_ref, wg_ref, bg_ref, wu_ref, bu_ref, wd_ref, bd_ref, y_ref,
                   w_in, wg_s, wu_s, wd_s, sems):
    i = pl.program_id(0)
    half = D_MODEL // 2
    M = x_ref.shape[0]
    rows = rows_ref[i]
    e = be_ref[i]
    prev = be_ref[jnp.maximum(i - 1, 0)]

    def weight_copies(expert, slot):
        return [pltpu.make_async_copy(w_hbm.at[expert], w_in.at[slot, k], sems.at[slot, k])
                for k, w_hbm in enumerate((wg_ref, wu_ref, wd_ref))]

    @pl.when((rows > 0) & ((i == 0) | (e != prev)))
    def _():
        slot = slot_ref[i]

        @pl.when(i == 0)
        def _():
            for cp in weight_copies(e, slot):
                cp.start()

        nxt = next_ref[i]

        @pl.when(nxt >= 0)
        def _():
            for cp in weight_copies(nxt, 1 - slot):
                cp.start()

        for cp in weight_copies(e, slot):
            cp.wait()
        for k, dst in enumerate((wg_s, wu_s, wd_s)):
            dst[...] = w_in[slot, k].astype(BF16)

    def compute(r):
        x_lo, x_hi = _unpack(x_ref[:r, :])

        def proj_in(w_s, b_ref):
            return (jnp.dot(x_lo, w_s[:half, :], preferred_element_type=F32)
                    + jnp.dot(x_hi, w_s[half:, :], preferred_element_type=F32) + b_ref[0])

        gate = jnp.minimum(proj_in(wg_s, bg_ref), SWIGLU_LIMIT)
        up = jnp.clip(proj_in(wu_s, bu_ref), -SWIGLU_LIMIT, SWIGLU_LIMIT)
        glu = gate * _sigmoid(gate * SWIGLU_ALPHA)
        act = ((up + 1.0) * glu).astype(BF16)
        y = jnp.dot(act, wd_s[...], preferred_element_type=F32) + bd_ref[0]
        y_ref[:r, :] = _pack(y[:, :half], y[:, half:])
        if r < M:
            y_ref[r:, :] = jnp.zeros((M - r, half), y_ref.dtype)

    for r in range(EXPERT_SUB, M + 1, EXPERT_SUB):
        pl.when(rows == r)(functools.partial(compute, r))

    @pl.when(rows == 0)
    def _():
        y_ref[...] = jnp.zeros(y_ref.shape, y_ref.dtype)


def _experts(blk_exp, blk_rows, blk_slot, blk_next, x_rows, n_pad, w_g, b_g, w_u, b_u, w_d, b_d):
    M = EXPERT_BLOCK
    assert D_FF == D_MODEL
    bspec = lambda n: pl.BlockSpec((1, 1, n), lambda i, be, *_: (be[i], 0, 0))
    wspec = pl.BlockSpec(memory_space=pl.ANY)
    return pl.pallas_call(
        _expert_kernel,
        grid_spec=pltpu.PrefetchScalarGridSpec(
            num_scalar_prefetch=4,
            grid=(n_pad // M,),
            in_specs=[pl.BlockSpec((M, D_MODEL // 2), lambda i, *_: (i, 0)),
                      wspec, bspec(D_FF), wspec, bspec(D_FF), wspec, bspec(D_MODEL)],
            out_specs=pl.BlockSpec((M, D_MODEL // 2), lambda i, *_: (i, 0)),
            scratch_shapes=[pltpu.VMEM((2, 3, D_MODEL, D_FF), F32),
                            pltpu.VMEM((D_MODEL, D_FF), BF16),
                            pltpu.VMEM((D_MODEL, D_FF), BF16),
                            pltpu.VMEM((D_FF, D_MODEL), BF16),
                            pltpu.SemaphoreType.DMA((2, 3))]),
        out_shape=jax.ShapeDtypeStruct((n_pad, D_MODEL // 2), jnp.uint32),
        compiler_params=_params(("arbitrary",), VMEM_LIMIT),
        name="experts",
    )(blk_exp, blk_rows, blk_slot, blk_next, x_rows, w_g, b_g, w_u, b_u, w_d, b_d)


def _final_kernel(toff_ref, eoff_ref, n_ref, trows_ref, x1_ref, posw_ref, p_ref, gpg_ref, wpg_ref, wpp_ref, gpp_ref,
                  gfin_ref, y_ref, o_ref, buf, sems, *, nt):
    tm = x1_ref.shape[0]
    j = pl.program_id(0)

    def tile_runs(t, act):
        slot = t % 2

        def body(e, c):
            r = t * N_EXPERTS + e
            t0, s0 = toff_ref[r], eoff_ref[r]
            _run_copy(n_ref[r], tm, lambda done, size: act(pltpu.make_async_copy(
                y_ref.at[pl.ds(_aligned(s0 + done), size)], buf.at[slot, pl.ds(_aligned(t0 + done), size)],
                sems.at[slot])))
            return c

        lax.fori_loop(0, N_EXPERTS, body, 0, unroll=RUN_UNROLL)

    @pl.when(j == 0)
    def _():
        buf[...] = jnp.zeros(buf.shape, buf.dtype)
        tile_runs(0, lambda cp: cp.start())

    @pl.when(j + 1 < nt)
    def _():
        tile_runs(j + 1, lambda cp: cp.start())

    _run_pieces(trows_ref[j], TILE_ROWS, lambda done, size: pltpu.make_async_copy(
        y_ref.at[pl.ds(0, size)], buf.at[j % 2, pl.ds(0, size)], sems.at[j % 2]).wait())
    posw_t = posw_ref[...].T
    comb_t = _onehot_bands(posw_t[:TOP_K], posw_t[TOP_K:2 * TOP_K].astype(BF16))
    sure = TOP_K * tm
    tail = buf[j % 2, sure:, :]
    live = lax.broadcasted_iota(jnp.int32, tail.shape, 0) < trows_ref[j] - sure
    y_lo, y_hi = _unpack(jnp.concatenate([buf[j % 2, :sure, :], jnp.where(live, tail, jnp.uint32(0))], axis=0))
    moe = jnp.concatenate([lax.dot_general(comb_t, y_lo, TN, preferred_element_type=F32),
                           lax.dot_general(comb_t, y_hi, TN, preferred_element_type=F32)], axis=-1)
    x = x1_ref[...] + moe
    pg = _sigmoid(jnp.dot(_rms(x, gpg_ref[...]).astype(BF16), wpg_ref[...], preferred_element_type=F32))
    pp = jnp.dot(p_ref[...].astype(BF16), wpp_ref[...], preferred_element_type=F32)
    x = x + pg * _rms(pp, gpp_ref[...])
    o_ref[...] = _rms(x, gfin_ref[...])


def _final(tile_off, expert_off, run_n, tile_rows, x1, y_rows, posw, p2, g_pg, w_pg, w_pp, g_pp, g_fin, tm):
    T = x1.shape[0]
    nt = T // tm
    full = lambda shape: pl.BlockSpec(shape, lambda i, *_: (0, 0))
    rowblk = lambda w: pl.BlockSpec((tm, w), lambda i, *_: (i, 0))
    return pl.pallas_call(
        functools.partial(_final_kernel, nt=nt),
        grid_spec=pltpu.PrefetchScalarGridSpec(
            num_scalar_prefetch=4,
            grid=(nt,),
            in_specs=[rowblk(D_MODEL), rowblk(LANES), rowblk(PLE_DIM),
                      full((1, D_MODEL)), full((D_MODEL, D_MODEL)), full((PLE_DIM, D_MODEL)),
                      full((1, D_MODEL)), full((1, D_MODEL)),
                      pl.BlockSpec(memory_space=pl.ANY)],
            out_specs=rowblk(D_MODEL),
            scratch_shapes=[pltpu.VMEM((2, TILE_ROWS, D_MODEL // 2), jnp.uint32),
                            pltpu.SemaphoreType.DMA((2,))]),
        out_shape=jax.ShapeDtypeStruct((T, D_MODEL), F32),
        compiler_params=_params(("arbitrary",), VMEM_LIMIT),
        name="final",
    )(tile_off, expert_off, run_n, tile_rows, x1, posw, p2, g_pg, w_pg, w_pp, g_pp, g_fin, y_rows)


def _split_w_in(w_in):
    sizes = (GLA_QK, GLA_QK, GLA_V, GLA_V, GLA_RANK, MOBA_W, MOBA_W, MOBA_W, D_MODEL, D_MODEL)
    offs = [0]
    for s in sizes:
        offs.append(offs[-1] + s)
    parts = [w_in[:, offs[i]:offs[i + 1]] for i in range(len(sizes))]
    main = jnp.concatenate(parts[:4] + parts[5:], axis=1).astype(BF16)
    alow = jnp.pad(parts[4], ((0, 0), (0, LANES - GLA_RANK))).astype(BF16)
    return main, alow


def _layer(x2, p2, bias, B, S, g_mix, w_in, w_a2, b_a2, g_gla_out, w_up_gla, w_up_moba, w_o, g_ffn, w_router,
           b_router, w_e_gate, b_e_gate, w_e_up, b_e_up, w_e_down, b_e_down, g_ple_gate, w_ple_gate, w_ple_proj,
           g_ple_proj, g_final):
    T = B * S
    row = lambda v: v.reshape(1, -1).astype(F32)
    w_main, w_alow = _split_w_in(w_in)
    colscale = jnp.ones((D_PROJ,), F32)
    colscale = colscale.at[OFF_QA:OFF_QA + GLA_QK].set(GLA_DK ** -0.5)
    colscale = colscale.at[OFF_QB:OFF_QB + MOBA_W].set(MOBA_HD ** -0.5 * LOG2E)
    w_a2p = jnp.pad(w_a2, ((0, LANES - GLA_RANK), (0, 0))).astype(BF16)
    proj, glog = _inproj(x2, row(g_mix), w_main, colscale.reshape(1, -1), w_alow, w_a2p, row(b_a2))

    o_a = _gla(proj, glog, row(g_gla_out), B, S)
    o_b = _moba(proj, bias, B, S)

    w_r = jnp.pad(w_router.astype(F32), ((0, 0), (0, LANES - N_EXPERTS)))
    w_r_hi = w_r.astype(BF16)
    w_r2 = jnp.concatenate([w_r_hi, (w_r - w_r_hi.astype(F32)).astype(BF16)], axis=1)
    b_r = jnp.pad(b_router.astype(F32), (0, LANES - N_EXPERTS)).reshape(1, -1)
    tm = TOKEN_TILE
    nt = T // tm
    x1, h2, logits = _merge(o_a, o_b, proj, x2, w_up_gla.astype(BF16), w_up_moba.astype(BF16),
                            w_o.astype(BF16), row(g_ffn), w_r2, b_r, MERGE_TILE)
    posw, cnt_t, carry = _router(logits, tm)

    M = EXPERT_BLOCK
    A = nt * TILE_ROWS
    n_pad = (-(-A // M)) * M + N_EXPERTS * M
    n_blk = n_pad // M
    cnt_t = cnt_t[:, 0, :N_EXPERTS].astype(jnp.int32)
    carry = carry[:, 0, :N_EXPERTS].astype(jnp.int32)
    counts = carry[-1] + cnt_t[-1]
    padded = (counts + M - 1) // M * M
    pad_end = jnp.cumsum(padded)
    pad_start = pad_end - padded
    blk_exp = jnp.minimum(jnp.sum(pad_end[None, :] <= (jnp.arange(n_blk, dtype=jnp.int32) * M)[:, None], axis=1),
                          N_EXPERTS - 1).astype(jnp.int32)
    n_used = (pad_end[-1:] // M).astype(jnp.int32)
    blk_start = jnp.arange(n_blk, dtype=jnp.int32) * M
    eids = jnp.arange(N_EXPERTS, dtype=jnp.int32)

    def per_block(per_expert):
        return jnp.sum(jnp.where(blk_exp[:, None] == eids[None, :], per_expert[None, :], 0), axis=1).astype(jnp.int32)

    blk_rows = jnp.clip(per_block(pad_start + counts) - blk_start, 0, M)
    blk_rows = jnp.where(blk_start < pad_end[-1], (blk_rows + EXPERT_SUB - 1) // EXPERT_SUB * EXPERT_SUB, 0)
    has_rows = counts > 0
    blk_slot = per_block((jnp.cumsum(has_rows) - 1) % 2)
    later = jnp.where((eids[None, :] > eids[:, None]) & has_rows[None, :], eids[None, :], N_EXPERTS)
    next_exp = jnp.min(later, axis=1)
    blk_next = per_block(jnp.where(next_exp < N_EXPERTS, next_exp, -1))
    tile_off = (jnp.cumsum(cnt_t, axis=1) - cnt_t).reshape(-1)
    expert_off = (carry + pad_start[None, :]).reshape(-1)
    run_n = cnt_t.reshape(-1)
    tile_rows = jnp.sum(cnt_t, axis=1)
    x_rows = _dispatch(h2, posw, n_pad, tile_off, expert_off, run_n, tile_rows, pad_start + counts, padded - counts,
                       n_used, tm)
    y_rows = _experts(blk_exp, blk_rows.astype(jnp.int32), blk_slot, blk_next, x_rows, n_pad,
                      w_e_gate, b_e_gate.reshape(N_EXPERTS, 1, D_FF),
                      w_e_up, b_e_up.reshape(N_EXPERTS, 1, D_FF), w_e_down,
                      b_e_down.reshape(N_EXPERTS, 1, D_MODEL))
    return _final(tile_off, expert_off, run_n, tile_rows, x1, y_rows, posw, p2, row(g_ple_gate),
                  w_ple_gate.astype(BF16), w_ple_proj.astype(BF16), row(g_ple_proj), row(g_final), tm)


def kernel(x, p, rel_bias, g_mix, w_in, w_a2, b_a2, g_gla_out, w_up_gla, w_up_moba, w_o, g_ffn, w_router, b_router,
           w_e_gate, b_e_gate, w_e_up, b_e_up, w_e_down, b_e_down, g_ple_gate, w_ple_gate, w_ple_proj, g_ple_proj,
           g_final):
    B, S, D = x.shape
    assert D == D_MODEL and S % MOBA_BLOCK == 0 and S % GLA_CHUNK == 0 and p.shape[0] == 1
    bias = _bias_tiles(rel_bias, S // MOBA_BLOCK)
    out = _layer(x.reshape(B * S, D), p[0].reshape(B * S, PLE_DIM), bias, B, S,
                 g_mix[0], w_in[0], w_a2[0], b_a2[0], g_gla_out[0], w_up_gla[0], w_up_moba[0], w_o[0], g_ffn[0],
                 w_router[0], b_router[0], w_e_gate[0], b_e_gate[0], w_e_up[0], b_e_up[0], w_e_down[0],
                 b_e_down[0], g_ple_gate[0], w_ple_gate[0], w_ple_proj[0], g_ple_proj[0], g_final)
    return out.reshape(B, S, D)
```

```python
import functools
import math

import jax
import jax.numpy as jnp
from jax import lax
from jax.experimental import pallas as pl
from jax.experimental.pallas import tpu as pltpu

F32 = jnp.float32
BF16 = jnp.bfloat16

D_MODEL = 1024
PLE_DIM = 256
GLA_HEADS = 4
GLA_DK = 128
GLA_DV = 256
GLA_RANK = 16
GLA_TAU = 16.0
GLA_QK = GLA_HEADS * GLA_DK
GLA_V = GLA_HEADS * GLA_DV
MOBA_HEADS = 8
MOBA_HD = 128
MOBA_BLOCK = 256
MOBA_TOPK = 3
MOBA_W = MOBA_HEADS * MOBA_HD
REL_BUCKETS = 32
REL_MAX_DIST = 4096
N_EXPERTS = 32
TOP_K = 4
D_FF = 1024
SWIGLU_LIMIT = 7.0
SWIGLU_ALPHA = 1.702
EPS = 1e-6

LANES = 128
NEG = -1e30
LOG2E = math.log2(math.e)
VMEM_LIMIT = 56 * 1024 * 1024

OFF_QA, OFF_KA, OFF_VA, OFF_RA = 0, 512, 1024, 2048
OFF_QB, OFF_KB, OFF_VB, OFF_GA, OFF_GB = 3072, 4096, 5120, 6144, 7168
D_PROJ = 8192

GLA_CHUNK = 128
EXPERT_BLOCK = 512
EXPERT_SUB = 128
TOKEN_TILE = 256
RUN_ALIGN = 8
TILE_ROWS = TOP_K * TOKEN_TILE + N_EXPERTS * RUN_ALIGN
RUN_UNROLL = 8
MERGE_TILE = 512

NT = (((1,), (1,)), ((), ()))
TN = (((0,), (0,)), ((), ()))


def _params(sem, vmem=None):
    return pltpu.CompilerParams(dimension_semantics=sem, vmem_limit_bytes=vmem)


def _rms(x, g):
    return x * lax.rsqrt(jnp.mean(x * x, axis=-1, keepdims=True) + EPS) * g


def _sigmoid(x):
    return 1.0 / (1.0 + jnp.exp(-x))


def _bucket_of(n):
    max_exact = REL_BUCKETS // 2
    if n < max_exact:
        return n
    return min(max_exact + int(math.log(n / max_exact) / math.log(REL_MAX_DIST / max_exact)
                               * (REL_BUCKETS - max_exact)), REL_BUCKETS - 1)


def _bias_kernel(tab_ref, bkt_ref, o_ref, *, nblk):
    strip = 8
    m = pl.program_id(0)

    def fill(lo, hi):
        def body(s, carry):
            r0 = pl.multiple_of(s * strip, strip)
            b = bkt_ref[0, pl.ds(r0, strip), :]
            accs = [jnp.zeros(b.shape, F32) for _ in range(MOBA_HEADS)]
            for bb in range(lo, hi + 1):
                hit = b == bb
                for h in range(MOBA_HEADS):
                    accs[h] = jnp.where(hit, tab_ref[h, bb], accs[h])
            for h in range(MOBA_HEADS):
                o_ref[h, 0, pl.ds(r0, strip), :] = jnp.where(b < 0, NEG, accs[h])
            return carry

        lax.fori_loop(0, MOBA_BLOCK // strip, body, 0)

    for mm in range(nblk):
        lo = max(_bucket_of(max(mm * MOBA_BLOCK - (MOBA_BLOCK - 1), 0)) - 1, 0)
        hi = min(_bucket_of(mm * MOBA_BLOCK + MOBA_BLOCK - 1) + 1, REL_BUCKETS - 1)
        pl.when(m == mm)(functools.partial(fill, lo, hi))


def _bias_tiles(rel_bias, nblk):
    i = jnp.arange(MOBA_BLOCK, dtype=jnp.int32)
    dist = (jnp.arange(nblk, dtype=jnp.int32)[:, None, None] * MOBA_BLOCK + i[None, :, None] - i[None, None, :])
    n = jnp.maximum(dist, 0)
    max_exact = REL_BUCKETS // 2
    nf = jnp.maximum(n, 1).astype(F32)
    large = max_exact + (jnp.log(nf / max_exact) / math.log(REL_MAX_DIST / max_exact)
                         * (REL_BUCKETS - max_exact)).astype(jnp.int32)
    large = jnp.minimum(large, REL_BUCKETS - 1)
    bkt = jnp.where(dist < 0, -1, jnp.where(n < max_exact, n, large)).astype(jnp.int32)
    tab = rel_bias.astype(F32).T * LOG2E
    return pl.pallas_call(
        functools.partial(_bias_kernel, nblk=nblk),
        grid=(nblk,),
        in_specs=[pl.BlockSpec(memory_space=pltpu.SMEM),
                  pl.BlockSpec((1, MOBA_BLOCK, MOBA_BLOCK), lambda m: (m, 0, 0))],
        out_specs=pl.BlockSpec((MOBA_HEADS, 1, MOBA_BLOCK, MOBA_BLOCK), lambda m: (0, m, 0, 0)),
        out_shape=jax.ShapeDtypeStruct((MOBA_HEADS, nblk, MOBA_BLOCK, MOBA_BLOCK), F32),
        compiler_params=_params(("arbitrary",)),
        name="bias_tiles",
    )(tab, bkt)


def _inproj_kernel(x0_ref, xn_ref, g_ref, w_ref, cs_ref, wal_ref, wa2_ref, ba2_ref, o_ref, glog_ref, h_scr, al_scr,
                   *, n_j):
    i, j = pl.program_id(0), pl.program_id(1)
    tm = xn_ref.shape[0]
    rs = tm // n_j

    def gate_logits(a_low):
        a = jnp.dot(a_low, wa2_ref[...], preferred_element_type=F32) + ba2_ref[...]
        log_sig = jnp.minimum(a, 0.0) - jnp.log1p(jnp.exp(-jnp.abs(a)))
        return log_sig * (1.0 / GLA_TAU)

    def slice_rows(behind):
        return pl.multiple_of(((j + n_j - behind) % n_j) * rs, rs)

    @pl.when((i == 0) & (j == 0))
    def _():
        h0 = _rms(x0_ref[...], g_ref[...]).astype(BF16)
        h_scr[0] = h0
        a_low0 = jnp.dot(h0, wal_ref[...], preferred_element_type=F32).astype(BF16)
        glog_ref[...] = gate_logits(a_low0)
        al_scr[...] = a_low0[(n_j - 2) * rs:(n_j - 1) * rs, :]

    glog_ref[pl.ds(slice_rows(2), rs), :] = gate_logits(al_scr[...])
    slot = jnp.where(j == 0, i, i + 1) % 2
    h_lag = h_scr[slot, pl.ds(slice_rows(1), rs), :]
    al_scr[...] = jnp.dot(h_lag, wal_ref[...], preferred_element_type=F32).astype(BF16)

    acc = jnp.dot(h_scr[i % 2], w_ref[...], preferred_element_type=F32)
    o_ref[...] = (acc * cs_ref[...]).astype(BF16)

    h_scr[(i + 1) % 2, pl.ds(slice_rows(0), rs), :] = _rms(xn_ref[pl.ds(slice_rows(0), rs), :],
                                                             g_ref[...]).astype(BF16)


def _inproj(x2, g_mix, w_main, colscale, w_alow, w_a2p, b_a2, tm=1024, tn=2048):
    T = x2.shape[0]
    n_i, n_j = T // tm, D_PROJ // tn
    assert n_i >= 2 and tm % (8 * n_j) == 0
    return pl.pallas_call(
        functools.partial(_inproj_kernel, n_j=n_j),
        grid=(n_i, n_j),
        in_specs=[pl.BlockSpec((tm, D_MODEL), lambda i, j: (0, 0)),
                  pl.BlockSpec((tm, D_MODEL), lambda i, j: (jnp.minimum(i + 1, n_i - 1), 0)),
                  pl.BlockSpec((1, D_MODEL), lambda i, j: (0, 0)),
                  pl.BlockSpec((D_MODEL, tn), lambda i, j: (0, j)),
                  pl.BlockSpec((1, tn), lambda i, j: (0, j)),
                  pl.BlockSpec((D_MODEL, LANES), lambda i, j: (0, 0)),
                  pl.BlockSpec((LANES, GLA_QK), lambda i, j: (0, 0)),
                  pl.BlockSpec((1, GLA_QK), lambda i, j: (0, 0))],
        out_specs=[pl.BlockSpec((tm, tn), lambda i, j: (i, j)),
                   pl.BlockSpec((tm, GLA_QK), lambda i, j: (jnp.minimum(i + jnp.minimum(j // 2, 1), n_i - 1), 0))],
        out_shape=[jax.ShapeDtypeStruct((T, D_PROJ), BF16),
                   jax.ShapeDtypeStruct((T, GLA_QK), F32)],
        scratch_shapes=[pltpu.VMEM((2, tm, D_MODEL), BF16), pltpu.VMEM((tm // n_j, LANES), BF16)],
        compiler_params=_params(("arbitrary", "arbitrary"), VMEM_LIMIT),
        name="inproj",
    )(x2, x2, g_mix, w_main, colscale, w_alow, w_a2p, b_a2)


def _gla_kernel(q_ref, k_ref, v_ref, r_ref, g_ref, gout_ref, o_ref, st_ref):
    C = GLA_CHUNK

    @pl.when(pl.program_id(1) == 0)
    def _():
        st_ref[...] = jnp.zeros(st_ref.shape, F32)

    nb = q_ref.shape[0]
    row = lax.broadcasted_iota(jnp.int32, (C, C), 0)
    col = lax.broadcasted_iota(jnp.int32, (C, C), 1)
    causal = col <= row
    ltri = causal.astype(BF16)
    mid = C // 2
    pairs = [(b, h) for b in range(nb) for h in range(GLA_HEADS)]
    ks = lambda h: slice(h * GLA_DK, (h + 1) * GLA_DK)
    vs = lambda h: slice(h * GLA_DV, (h + 1) * GLA_DV)
    gate = {}
    for b, h in pairs:
        r = r_ref[b, :, vs(h)].astype(F32)
        gate[b, h] = r * _sigmoid(r)
    G = []
    for b in range(nb):
        g = g_ref[b]
        g_hi = g.astype(BF16)
        g_lo = (g - g_hi.astype(F32)).astype(BF16)
        G.append(jnp.dot(ltri, g_hi, preferred_element_type=F32) + jnp.dot(ltri, g_lo, preferred_element_type=F32))
    Gh = {(b, h): G[b][:, ks(h)] for b, h in pairs}
    qh = {(b, h): q_ref[b, :, ks(h)].astype(F32) for b, h in pairs}
    kh = {(b, h): k_ref[b, :, ks(h)].astype(F32) for b, h in pairs}
    g_mid = {p: Gh[p][mid:mid + 1, :] for p in pairs}
    g_last = {p: Gh[p][C - 1:C, :] for p in pairs}
    A = {p: lax.dot_general((qh[p] * jnp.exp(Gh[p] - g_mid[p])).astype(BF16),
                            (kh[p] * jnp.exp(g_mid[p] - Gh[p])).astype(BF16), NT, preferred_element_type=F32)
         for p in pairs}
    st = {p: st_ref[p[0], p[1]] for p in pairs}
    inter = {p: lax.dot_general((qh[p] * jnp.exp(Gh[p])).astype(BF16), st[p].astype(BF16), NT,
                                preferred_element_type=F32) for p in pairs}
    for b, h in pairs:
        p = (b, h)
        k_d = (kh[p] * jnp.exp(g_last[p] - Gh[p])).astype(BF16)
        st_ref[b, h] = jnp.exp(g_last[p]) * st[p] + lax.dot_general(v_ref[b, :, vs(h)], k_d, TN,
                                                                     preferred_element_type=F32)
    intra = {(b, h): jnp.dot(jnp.where(causal, A[b, h], 0.0).astype(BF16), v_ref[b, :, vs(h)],
                             preferred_element_type=F32) for b, h in pairs}
    for b, h in pairs:
        o = inter[b, h] + intra[b, h]
        o_ref[b, :, vs(h)] = (_rms(o, gout_ref[...]) * gate[b, h]).astype(BF16)


def _gla(proj, glog, g_gla_out, B, S, nb=4):
    C = GLA_CHUNK
    assert B % nb == 0
    proj3 = proj.reshape(B, S, D_PROJ)
    glog3 = glog.reshape(B, S, GLA_QK)
    spec = lambda w, off: pl.BlockSpec((nb, C, w), lambda b, c: (b, c, off // w))
    out = pl.pallas_call(
        _gla_kernel,
        grid=(B // nb, S // C),
        in_specs=[spec(GLA_QK, OFF_QA), spec(GLA_QK, OFF_KA), spec(GLA_V, OFF_VA), spec(GLA_V, OFF_RA),
                  spec(GLA_QK, 0), pl.BlockSpec((1, GLA_DV), lambda b, c: (0, 0))],
        out_specs=spec(GLA_V, 0),
        out_shape=jax.ShapeDtypeStruct((B, S, GLA_V), BF16),
        scratch_shapes=[pltpu.VMEM((nb, GLA_HEADS, GLA_DV, GLA_DK), F32)],
        compiler_params=_params(("arbitrary", "arbitrary")),
        name="gla",
    )(proj3, proj3, proj3, proj3, glog3, g_gla_out)
    return out.reshape(B * S, GLA_V)


def _moba_kernel(q_ref, k_ref, v_ref, bias_ref, o_ref, ka_scr, va_scr, lg_scr, mx_scr, qa_scr, *, nblk):
    BLK, HD = MOBA_BLOCK, MOBA_HD
    S = nblk * BLK

    @pl.when((pl.program_id(0) == 0) & (pl.program_id(1) == 0))
    def _():
        blk = lax.broadcasted_iota(jnp.int32, (S, HD), 0) // BLK
        lane = lax.broadcasted_iota(jnp.int32, (S, HD), 1)
        ka_scr[:, HD:] = (lane == blk).astype(BF16)
        va_scr[:, HD:] = (lane == 0).astype(BF16)

    n_plain = min(MOBA_TOPK + 1, nblk)

    def tile_id(cc, j):
        return cc * (cc + 1) // 2 + j

    def stage1(cc, q_in, keys):
        mx = None
        for j in range(cc + 1):
            lg = (lax.dot_general(q_in, keys[j * BLK:(j + 1) * BLK, :], NT, preferred_element_type=F32)
                  + bias_ref[0, cc - j])
            lg_scr[tile_id(cc, j)] = lg
            t = jnp.maximum(lg[:, :LANES], lg[:, LANES:])
            mx = t if mx is None else jnp.maximum(mx, t)
            if j == cc:
                mx_scr[cc] = mx
            yield

    def stage2(cc):
        m = mx_scr[cc].max(axis=-1, keepdims=True)
        acc = jnp.zeros((BLK, 2 * HD), F32)
        for j in range(cc + 1):
            p = jnp.exp2(lg_scr[tile_id(cc, j)] - m).astype(BF16)
            acc = acc + jnp.dot(p, va_scr[j * BLK:(j + 1) * BLK, :], preferred_element_type=F32)
            if j == cc:
                o_ref[cc * BLK:(cc + 1) * BLK, :] = (acc[:, :HD] / acc[:, HD:HD + 1]).astype(BF16)
            yield

    def drain(gen):
        for _ in gen:
            pass

    def interleave(main, side, n_main, n_side):
        side_steps = (s for g in side for s in g)
        done = 0
        for i, _ in enumerate(s for g in main for s in g):
            assert n_main > 0
            want = (i + 1) * n_side // n_main
            while done < want and next(side_steps, "end") != "end":
                done += 1
        drain(side_steps)

    plain = [stage1(cc, q_ref[cc * BLK:(cc + 1) * BLK, :], k_ref) for cc in range(n_plain)]
    if nblk > n_plain:
        next(plain[0])
        ksum = [k_ref[j * BLK:(j + 1) * BLK, :].astype(F32).reshape(BLK // 8, 8, HD).sum(axis=0).sum(
            axis=0, keepdims=True) for j in range(nblk)]
        kmean = jnp.concatenate(ksum, axis=0) * (1.0 / BLK)
        km_hi = kmean.astype(BF16)
        km_lo = (kmean - km_hi.astype(F32)).astype(BF16)
        km2 = jnp.concatenate([km_hi, km_lo], axis=0)
        pens = []
        for cc in range(n_plain, nblk):
            q = q_ref[cc * BLK:(cc + 1) * BLK, :]
            s2 = lax.dot_general(km2, q, NT, preferred_element_type=F32)
            pens.append((cc, q, s2[:nblk] + s2[nblk:]))
        for g in plain[:2]:
            drain(g)
        for cc, q, s in pens:
            ji = lax.broadcasted_iota(jnp.int32, s.shape, 0)
            cnt = jnp.zeros(s.shape, F32)
            for jp in range(cc):
                sj = s[jp:jp + 1, :]
                beats = (sj > s) | ((sj == s) & (jp < ji))
                cnt = cnt + beats.astype(F32)
            pen = jnp.where((ji < cc) & (cnt >= MOBA_TOPK), NEG, 0.0)
            pen_t = jnp.concatenate([pen, jnp.zeros((HD - nblk, BLK), F32)], axis=0).T
            qa_scr[cc - n_plain] = jnp.concatenate([q, pen_t.astype(BF16)], axis=1)
    ka_scr[:, :HD] = k_ref[...]
    va_scr[:, :HD] = v_ref[...]
    for g in plain:
        drain(g)

    @pl.when(pl.program_id(0) >= 0)
    def _():
        late = [stage1(cc, qa_scr[cc - n_plain], ka_scr) for cc in range(n_plain, nblk)]
        early = [stage2(cc) for cc in range(n_plain)]
        interleave(late, early, tile_id(nblk, 0) - tile_id(n_plain, 0), tile_id(n_plain, 0))

    @pl.when(pl.program_id(0) >= 0)
    def _():
        for cc in range(n_plain, nblk):
            drain(stage2(cc))


def _moba(proj, bias, B, S):
    BLK = MOBA_BLOCK
    nblk = S // BLK
    H = MOBA_HEADS
    assert nblk <= MOBA_HD
    return pl.pallas_call(
        functools.partial(_moba_kernel, nblk=nblk),
        grid=(H, B),
        in_specs=[pl.BlockSpec((S, MOBA_HD), lambda h, b: (b, OFF_QB // MOBA_HD + h)),
                  pl.BlockSpec((S, MOBA_HD), lambda h, b: (b, OFF_KB // MOBA_HD + h)),
                  pl.BlockSpec((S, MOBA_HD), lambda h, b: (b, OFF_VB // MOBA_HD + h)),
                  pl.BlockSpec((1, nblk, BLK, BLK), lambda h, b: (h, 0, 0, 0))],
        out_specs=pl.BlockSpec((S, MOBA_HD), lambda h, b: (b, h)),
        out_shape=jax.ShapeDtypeStruct((B * S, MOBA_W), BF16),
        scratch_shapes=[pltpu.VMEM((S, 2 * MOBA_HD), BF16), pltpu.VMEM((S, 2 * MOBA_HD), BF16),
                        pltpu.VMEM((nblk * (nblk + 1) // 2, BLK, BLK), F32),
                        pltpu.VMEM((nblk, BLK, LANES), F32),
                        pltpu.VMEM((max(nblk - MOBA_TOPK - 1, 1), BLK, 2 * MOBA_HD), BF16)],
        compiler_params=_params(("arbitrary", "arbitrary"), VMEM_LIMIT),
        name="moba",
    )(proj, proj, proj, bias)


def _pack(lo, hi):
    lo_b = lax.bitcast_convert_type(lo.astype(BF16).astype(F32), jnp.uint32)
    hi_b = lax.bitcast_convert_type(hi.astype(BF16).astype(F32), jnp.uint32)
    return (lo_b >> 16) | (hi_b & jnp.uint32(0xFFFF0000))


def _pack_exact(lo, hi):
    lo_b = lax.bitcast_convert_type(lo, jnp.uint32)
    hi_b = lax.bitcast_convert_type(hi, jnp.uint32)
    return (lo_b >> 16) | (hi_b & jnp.uint32(0xFFFF0000))


def _unpack(w):
    lo = lax.bitcast_convert_type(w << 16, F32)
    hi = lax.bitcast_convert_type(w & jnp.uint32(0xFFFF0000), F32)
    return lo.astype(BF16), hi.astype(BF16)


def _merge_kernel(oa_ref, ob_ref, ga_ref, gb_ref, x_ref, wua_ref, wub_ref, wo_ref, gffn_ref, wr_ref, br_ref,
                  x1_ref, h2_ref, lg_ref):
    tm = x_ref.shape[0]
    u_a = jnp.dot(oa_ref[...], wua_ref[...], preferred_element_type=F32)
    u_b = jnp.dot(ob_ref[...], wub_ref[...], preferred_element_type=F32)
    y = _sigmoid(ga_ref[...].astype(F32)) * u_a + _sigmoid(gb_ref[...].astype(F32)) * u_b
    x1 = x_ref[...] + jnp.dot(y.astype(BF16), wo_ref[...], preferred_element_type=F32)
    x1_ref[...] = x1
    h2 = _rms(x1, gffn_ref[...])
    h_hi = h2.astype(BF16)
    h2_ref[...] = h_hi
    h_lo = (h2 - h_hi.astype(F32)).astype(BF16)
    r = jnp.dot(jnp.concatenate([h_hi, h_lo], axis=0), wr_ref[...], preferred_element_type=F32)
    lg_ref[...] = r[:tm, :LANES] + r[:tm, LANES:] + r[tm:, :LANES] + br_ref[...]


def _merge(o_a, o_b, proj, x2, w_ua, w_ub, w_o, g_ffn, w_r2, b_r, tm):
    T = x2.shape[0]
    full = lambda shape: pl.BlockSpec(shape, lambda i: (0, 0))
    rowblk = lambda w: pl.BlockSpec((tm, w), lambda i: (i, 0))
    return pl.pallas_call(
        _merge_kernel,
        grid=(T // tm,),
        in_specs=[rowblk(GLA_V), rowblk(MOBA_W),
                  pl.BlockSpec((tm, D_MODEL), lambda i: (i, OFF_GA // D_MODEL)),
                  pl.BlockSpec((tm, D_MODEL), lambda i: (i, OFF_GB // D_MODEL)),
                  rowblk(D_MODEL),
                  full((GLA_V, D_MODEL)), full((MOBA_W, D_MODEL)), full((D_MODEL, D_MODEL)),
                  full((1, D_MODEL)), full((D_MODEL, 2 * LANES)), full((1, LANES))],
        out_specs=[rowblk(D_MODEL), rowblk(D_MODEL), rowblk(LANES)],
        out_shape=[jax.ShapeDtypeStruct((T, D_MODEL), F32),
                   jax.ShapeDtypeStruct((T, D_MODEL), BF16),
                   jax.ShapeDtypeStruct((T, LANES), F32)],
        compiler_params=_params(("arbitrary",), VMEM_LIMIT),
        name="merge",
    )(o_a, o_b, proj, proj, x2, w_ua, w_ub, w_o, g_ffn, w_r2, b_r)


def _router_kernel(lg_ref, posw_ref, cnt_ref, carry_ref, cnt_scr, *, tm):
    rows = lg_ref.shape[0]

    @pl.when(pl.program_id(0) == 0)
    def _():
        cnt_scr[...] = jnp.zeros(cnt_scr.shape, F32)

    lane = lax.broadcasted_iota(jnp.int32, (rows, LANES), 1)
    lane_f = lane.astype(F32)
    work = jnp.where(lane < N_EXPERTS, lg_ref[...], NEG)
    vals, hots = [], []
    for _ in range(TOP_K):
        mx = work.max(axis=-1, keepdims=True)
        idx = jnp.min(jnp.where(work == mx, lane_f, float(LANES)), axis=-1, keepdims=True)
        hot = lane_f == idx
        vals.append(mx)
        hots.append(hot)
        work = jnp.where(hot, 2.0 * NEG, work)
    exps = [jnp.exp(v - vals[0]) for v in vals]
    den = exps[0] + exps[1] + exps[2] + exps[3]
    sel = jnp.zeros((rows, LANES), F32)
    for hot in hots:
        sel = sel + hot.astype(F32)
    row = lax.broadcasted_iota(jnp.int32, (tm, tm), 0)
    col = lax.broadcasted_iota(jnp.int32, (tm, tm), 1)
    below = (col < row).astype(BF16)
    er = lax.broadcasted_iota(jnp.int32, (LANES, LANES), 0)
    ec = lax.broadcasted_iota(jnp.int32, (LANES, LANES), 1)
    before = (er < ec).astype(F32)
    pos_parts = []
    for t in range(rows // tm):
        sel_t = sel[t * tm:(t + 1) * tm]
        local_rank = jnp.dot(below, sel_t.astype(BF16), preferred_element_type=F32)
        cnt_t = sel_t.sum(axis=0, keepdims=True)
        cnt_t = jnp.floor((cnt_t + (RUN_ALIGN - 1.0)) * (1.0 / RUN_ALIGN)) * RUN_ALIGN
        tile_off = jnp.dot(jnp.broadcast_to(cnt_t, (8, LANES)), before, preferred_element_type=F32,
                           precision=lax.Precision.HIGHEST)[0:1]
        pos_parts.append(local_rank + tile_off)
        carry_ref[t] = cnt_scr[...]
        cnt_ref[t] = cnt_t
        cnt_scr[...] = cnt_scr[...] + cnt_t
    pos_all = jnp.concatenate(pos_parts, axis=0)
    posw = jnp.zeros((rows, LANES), F32)
    for k in range(TOP_K):
        pk = jnp.sum(jnp.where(hots[k], pos_all, 0.0), axis=-1, keepdims=True)
        posw = jnp.where(lane == k, pk, posw)
        posw = jnp.where(lane == TOP_K + k, exps[k] / den, posw)
    posw_ref[...] = posw


def _router(logits, tm, tiles_per_step=4):
    T = logits.shape[0]
    nt = T // tm
    rows = tm * tiles_per_step
    tilerow = pl.BlockSpec((tiles_per_step, 1, LANES), lambda i: (i, 0, 0))
    return pl.pallas_call(
        functools.partial(_router_kernel, tm=tm),
        grid=(T // rows,),
        in_specs=[pl.BlockSpec((rows, LANES), lambda i: (i, 0))],
        out_specs=[pl.BlockSpec((rows, LANES), lambda i: (i, 0)), tilerow, tilerow],
        out_shape=[jax.ShapeDtypeStruct((T, LANES), F32),
                   jax.ShapeDtypeStruct((nt, 1, LANES), F32),
                   jax.ShapeDtypeStruct((nt, 1, LANES), F32)],
        scratch_shapes=[pltpu.VMEM((1, LANES), F32)],
        compiler_params=_params(("arbitrary",)),
        name="router",
    )(logits)


def _run_pieces(n, max_rows, fn):
    for b in reversed(range(RUN_ALIGN.bit_length() - 1, max_rows.bit_length())):
        size = 1 << b
        done = n & ~((2 << b) - 1)

        @pl.when((n & size) != 0)
        def _():
            fn(done, size)


def _aligned(i):
    return pl.multiple_of(i, RUN_ALIGN)


def _onehot_bands(pos, val):
    band = 256
    assert TILE_ROWS % band == 0
    n_k, n_tok = pos.shape
    pos_a = jnp.floor(pos * (1.0 / band))
    pos_b = pos - band * pos_a
    sub = lax.broadcasted_iota(jnp.int32, (band, n_tok), 0).astype(F32).astype(BF16)
    zero = jnp.zeros((band, n_tok), BF16)
    bands = []
    for a in range(TILE_ROWS // band):
        want = jnp.where(pos_a == a, pos_b, -1.0).astype(BF16)
        hit = zero
        for k in range(n_k):
            row = jnp.broadcast_to(want[k:k + 1, :], (band, n_tok))
            fill = jnp.ones((band, n_tok), BF16) if val is None else jnp.broadcast_to(val[k:k + 1, :], (band, n_tok))
            hit = hit + jnp.where(sub == row, fill, zero)
        bands.append(hit)
    return jnp.concatenate(bands, axis=0)


def _dispatch_kernel(toff_ref, eoff_ref, n_ref, trows_ref, zoff_ref, zn_ref, tail_ref, h2_ref, posw_ref, x_ref,
                     buf, zbuf, sems, *, nt):
    tm = h2_ref.shape[0]
    half = D_MODEL // 2
    j = pl.program_id(0)
    zrows = zbuf.shape[0]

    def tile_runs(t, act):
        slot = t % 2

        def body(e, c):
            r = t * N_EXPERTS + e
            t0, d0 = toff_ref[r], eoff_ref[r]
            _run_pieces(n_ref[r], tm, lambda done, size: act(pltpu.make_async_copy(
                buf.at[slot, pl.ds(_aligned(t0 + done), size)], x_ref.at[pl.ds(_aligned(d0 + done), size)],
                sems.at[slot])))
            return c

        lax.fori_loop(0, N_EXPERTS, body, 0, unroll=RUN_UNROLL)

    def zero_fill(act):
        def body(e, c):
            d0 = zoff_ref[e]
            _run_pieces(zn_ref[e], zrows, lambda done, size: act(pltpu.make_async_copy(
                zbuf.at[pl.ds(0, size)], x_ref.at[pl.ds(_aligned(d0 + done), size)], sems.at[2])))
            return c

        lax.fori_loop(0, N_EXPERTS, body, 0)

        def tail(i, c):
            act(pltpu.make_async_copy(zbuf, x_ref.at[pl.ds(pl.multiple_of(i * zrows, zrows), zrows)], sems.at[2]))
            return c

        lax.fori_loop(tail_ref[0], x_ref.shape[0] // zrows, tail, 0)

    start = lambda cp: cp.start()
    wait = lambda cp: cp.wait()

    def wait_tile(t):
        slot = t % 2
        _run_pieces(trows_ref[t], TILE_ROWS, lambda done, size: pltpu.make_async_copy(
            buf.at[slot, pl.ds(0, size)], x_ref.at[pl.ds(0, size)], sems.at[slot]).wait())

    @pl.when(j == 0)
    def _():
        zbuf[...] = jnp.zeros(zbuf.shape, zbuf.dtype)
        zero_fill(start)

    @pl.when(j >= 2)
    def _():
        wait_tile(j - 2)

    perm = _onehot_bands(posw_ref[...].T[:TOP_K], None)
    xs = jnp.dot(perm, h2_ref[...], preferred_element_type=F32)
    buf[j % 2] = _pack_exact(xs[:, :half], xs[:, half:])
    tile_runs(j, start)

    @pl.when(j == nt - 1)
    def _():
        if nt >= 2:
            wait_tile(j - 1)
        wait_tile(j)
        zero_fill(wait)


def _dispatch(h2, posw, n_rows, tile_off, expert_off, run_n, tile_rows, zoff, zn, tail, tm):
    T = h2.shape[0]
    nt = T // tm
    assert n_rows % EXPERT_BLOCK == 0
    return pl.pallas_call(
        functools.partial(_dispatch_kernel, nt=nt),
        grid_spec=pltpu.PrefetchScalarGridSpec(
            num_scalar_prefetch=7,
            grid=(nt,),
            in_specs=[pl.BlockSpec((tm, D_MODEL), lambda i, *_: (i, 0)),
                      pl.BlockSpec((tm, LANES), lambda i, *_: (i, 0))],
            out_specs=pl.BlockSpec(memory_space=pl.ANY),
            scratch_shapes=[pltpu.VMEM((2, TILE_ROWS, D_MODEL // 2), jnp.uint32),
                            pltpu.VMEM((EXPERT_BLOCK, D_MODEL // 2), jnp.uint32),
                            pltpu.SemaphoreType.DMA((3,))]),
        out_shape=jax.ShapeDtypeStruct((n_rows, D_MODEL // 2), jnp.uint32),
        compiler_params=_params(("arbitrary",), VMEM_LIMIT),
        name="dispatch",
    )(tile_off, expert_off, run_n, tile_rows, zoff, zn, tail, h2, posw)


def _expert_kernel(be_ref, rows_ref, slot_ref, next_ref, x_ref, wg_ref, bg_ref, wu_ref, bu_ref, wd_ref, bd_ref, y_ref,
                   w_in, wg_s, wu_s, wd_s, sems):
    i = pl.program_id(0)
    half = D_MODEL // 2
    M = x_ref.shape[0]
    rows = rows_ref[i]
    e = be_ref[i]
    prev = be_ref[jnp.maximum(i - 1, 0)]

    def weight_copies(expert, slot):
        return [pltpu.make_async_copy(w_hbm.at[expert], w_in.at[slot, k], sems.at[slot, k])
                for k, w_hbm in enumerate((wg_ref, wu_ref, wd_ref))]

    @pl.when((rows > 0) & ((i == 0) | (e != prev)))
    def _():
        slot = slot_ref[i]

        @pl.when(i == 0)
        def _():
            for cp in weight_copies(e, slot):
                cp.start()

        nxt = next_ref[i]

        @pl.when(nxt >= 0)
        def _():
            for cp in weight_copies(nxt, 1 - slot):
                cp.start()

        for cp in weight_copies(e, slot):
            cp.wait()
        for k, dst in enumerate((wg_s, wu_s, wd_s)):
            dst[...] = w_in[slot, k].astype(BF16)

    def compute(r):
        x_lo, x_hi = _unpack(x_ref[:r, :])

        def proj_in(w_s, b_ref):
            return (jnp.dot(x_lo, w_s[:half, :], preferred_element_type=F32)
                    + jnp.dot(x_hi, w_s[half:, :], preferred_element_type=F32) + b_ref[0])

        gate = jnp.minimum(proj_in(wg_s, bg_ref), SWIGLU_LIMIT)
        up = jnp.clip(proj_in(wu_s, bu_ref), -SWIGLU_LIMIT, SWIGLU_LIMIT)
        glu = gate * _sigmoid(gate * SWIGLU_ALPHA)
        act = ((up + 1.0) * glu).astype(BF16)
        y = jnp.dot(act, wd_s[...], preferred_element_type=F32) + bd_ref[0]
        y_ref[:r, :] = _pack(y[:, :half], y[:, half:])
        if r < M:
            y_ref[r:, :] = jnp.zeros((M - r, half), y_ref.dtype)

    for r in range(EXPERT_SUB, M + 1, EXPERT_SUB):
        pl.when(rows == r)(functools.partial(compute, r))

    @pl.when(rows == 0)
    def _():
        y_ref[...] = jnp.zeros(y_ref.shape, y_ref.dtype)


def _experts(blk_exp, blk_rows, blk_slot, blk_next, x_rows, n_pad, w_g, b_g, w_u, b_u, w_d, b_d):
    M = EXPERT_BLOCK
    assert D_FF == D_MODEL
    bspec = lambda n: pl.BlockSpec((1, 1, n), lambda i, be, *_: (be[i], 0, 0))
    wspec = pl.BlockSpec(memory_space=pl.ANY)
    return pl.pallas_call(
        _expert_kernel,
        grid_spec=pltpu.PrefetchScalarGridSpec(
            num_scalar_prefetch=4,
            grid=(n_pad // M,),
            in_specs=[pl.BlockSpec((M, D_MODEL // 2), lambda i, *_: (i, 0)),
                      wspec, bspec(D_FF), wspec, bspec(D_FF), wspec, bspec(D_MODEL)],
            out_specs=pl.BlockSpec((M, D_MODEL // 2), lambda i, *_: (i, 0)),
            scratch_shapes=[pltpu.VMEM((2, 3, D_MODEL, D_FF), F32),
                            pltpu.VMEM((D_MODEL, D_FF), BF16),
                            pltpu.VMEM((D_MODEL, D_FF), BF16),
                            pltpu.VMEM((D_FF, D_MODEL), BF16),
                            pltpu.SemaphoreType.DMA((2, 3))]),
        out_shape=jax.ShapeDtypeStruct((n_pad, D_MODEL // 2), jnp.uint32),
        compiler_params=_params(("arbitrary",), VMEM_LIMIT),
        name="experts",
    )(blk_exp, blk_rows, blk_slot, blk_next, x_rows, w_g, b_g, w_u, b_u, w_d, b_d)


def _final_kernel(toff_ref, eoff_ref, n_ref, trows_ref, x1_ref, posw_ref, p_ref, gpg_ref, wpg_ref, wpp_ref, gpp_ref,
                  gfin_ref, y_ref, o_ref, buf, xmid, sems, *, nt):
    tm = x1_ref.shape[0]
    step = pl.program_id(0)
    j = jnp.minimum(step, nt - 1)

    def tile_runs(t, act):
        slot = t % 2

        def body(e, c):
            r = t * N_EXPERTS + e
            t0, s0 = toff_ref[r], eoff_ref[r]
            _run_pieces(n_ref[r], tm, lambda done, size: act(pltpu.make_async_copy(
                y_ref.at[pl.ds(_aligned(s0 + done), size)], buf.at[slot, pl.ds(_aligned(t0 + done), size)],
                sems.at[slot])))
            return c

        lax.fori_loop(0, N_EXPERTS, body, 0, unroll=RUN_UNROLL)

    @pl.when(step == 0)
    def _():
        buf[...] = jnp.zeros(buf.shape, buf.dtype)
        xmid[...] = jnp.zeros(xmid.shape, xmid.dtype)
        tile_runs(0, lambda cp: cp.start())

    @pl.when(step + 1 < nt)
    def _():
        tile_runs(step + 1, lambda cp: cp.start())

    @pl.when(step < nt)
    def _():
        _run_pieces(trows_ref[j], TILE_ROWS, lambda done, size: pltpu.make_async_copy(
            y_ref.at[pl.ds(0, size)], buf.at[j % 2, pl.ds(0, size)], sems.at[j % 2]).wait())

    x = xmid[...]
    pp = jnp.dot(p_ref[...].astype(BF16), wpp_ref[...], preferred_element_type=F32)
    pg = _sigmoid(jnp.dot(_rms(x, gpg_ref[...]).astype(BF16), wpg_ref[...], preferred_element_type=F32))
    x = x + pg * _rms(pp, gpp_ref[...])
    o_ref[...] = _rms(x, gfin_ref[...])

    posw_t = posw_ref[...].T
    comb_t = _onehot_bands(posw_t[:TOP_K], posw_t[TOP_K:2 * TOP_K].astype(BF16))
    sure = TOP_K * tm
    tail = buf[j % 2, sure:, :]
    live = lax.broadcasted_iota(jnp.int32, tail.shape, 0) < trows_ref[j] - sure
    y_lo, y_hi = _unpack(jnp.concatenate([buf[j % 2, :sure, :], jnp.where(live, tail, jnp.uint32(0))], axis=0))
    moe = jnp.concatenate([lax.dot_general(comb_t, y_lo, TN, preferred_element_type=F32),
                           lax.dot_general(comb_t, y_hi, TN, preferred_element_type=F32)], axis=-1)
    xmid[...] = x1_ref[...] + moe


def _final(tile_off, expert_off, run_n, tile_rows, x1, y_rows, posw, p2, g_pg, w_pg, w_pp, g_pp, g_fin, tm):
    T = x1.shape[0]
    nt = T // tm
    full = lambda shape: pl.BlockSpec(shape, lambda i, *_: (0, 0))
    this = lambda w: pl.BlockSpec((tm, w), lambda i, *_: (jnp.minimum(i, nt - 1), 0))
    prev = lambda w: pl.BlockSpec((tm, w), lambda i, *_: (jnp.maximum(i - 1, 0), 0))
    return pl.pallas_call(
        functools.partial(_final_kernel, nt=nt),
        grid_spec=pltpu.PrefetchScalarGridSpec(
            num_scalar_prefetch=4,
            grid=(nt + 1,),
            in_specs=[this(D_MODEL), this(LANES), prev(PLE_DIM),
                      full((1, D_MODEL)), full((D_MODEL, D_MODEL)), full((PLE_DIM, D_MODEL)),
                      full((1, D_MODEL)), full((1, D_MODEL)),
                      pl.BlockSpec(memory_space=pl.ANY)],
            out_specs=prev(D_MODEL),
            scratch_shapes=[pltpu.VMEM((2, TILE_ROWS, D_MODEL // 2), jnp.uint32),
                            pltpu.VMEM((tm, D_MODEL), F32),
                            pltpu.SemaphoreType.DMA((2,))]),
        out_shape=jax.ShapeDtypeStruct((T, D_MODEL), F32),
        compiler_params=_params(("arbitrary",), VMEM_LIMIT),
        name="final",
    )(tile_off, expert_off, run_n, tile_rows, x1, posw, p2, g_pg, w_pg, w_pp, g_pp, g_fin, y_rows)


def _split_w_in(w_in):
    sizes = (GLA_QK, GLA_QK, GLA_V, GLA_V, GLA_RANK, MOBA_W, MOBA_W, MOBA_W, D_MODEL, D_MODEL)
    offs = [0]
    for s in sizes:
        offs.append(offs[-1] + s)
    parts = [w_in[:, offs[i]:offs[i + 1]] for i in range(len(sizes))]
    main = jnp.concatenate(parts[:4] + parts[5:], axis=1).astype(BF16)
    alow = jnp.pad(parts[4], ((0, 0), (0, LANES - GLA_RANK))).astype(BF16)
    return main, alow


def _layer(x2, p2, bias, B, S, g_mix, w_in, w_a2, b_a2, g_gla_out, w_up_gla, w_up_moba, w_o, g_ffn, w_router,
           b_router, w_e_gate, b_e_gate, w_e_up, b_e_up, w_e_down, b_e_down, g_ple_gate, w_ple_gate, w_ple_proj,
           g_ple_proj, g_final):
    T = B * S
    row = lambda v: v.reshape(1, -1).astype(F32)
    w_main, w_alow = _split_w_in(w_in)
    colscale = jnp.ones((D_PROJ,), F32)
    colscale = colscale.at[OFF_QA:OFF_QA + GLA_QK].set(GLA_DK ** -0.5)
    colscale = colscale.at[OFF_QB:OFF_QB + MOBA_W].set(MOBA_HD ** -0.5 * LOG2E)
    w_a2p = jnp.pad(w_a2, ((0, LANES - GLA_RANK), (0, 0))).astype(BF16)
    proj, glog = _inproj(x2, row(g_mix), w_main, colscale.reshape(1, -1), w_alow, w_a2p, row(b_a2))

    o_a = _gla(proj, glog, row(g_gla_out), B, S)
    o_b = _moba(proj, bias, B, S)

    w_r = jnp.pad(w_router.astype(F32), ((0, 0), (0, LANES - N_EXPERTS)))
    w_r_hi = w_r.astype(BF16)
    w_r2 = jnp.concatenate([w_r_hi, (w_r - w_r_hi.astype(F32)).astype(BF16)], axis=1)
    b_r = jnp.pad(b_router.astype(F32), (0, LANES - N_EXPERTS)).reshape(1, -1)
    tm = TOKEN_TILE
    nt = T // tm
    x1, h2, logits = _merge(o_a, o_b, proj, x2, w_up_gla.astype(BF16), w_up_moba.astype(BF16),
                            w_o.astype(BF16), row(g_ffn), w_r2, b_r, MERGE_TILE)
    posw, cnt_t, carry = _router(logits, tm)

    M = EXPERT_BLOCK
    A = nt * TILE_ROWS
    n_pad = (-(-A // M)) * M + N_EXPERTS * M
    n_blk = n_pad // M
    cnt_t = cnt_t[:, 0, :N_EXPERTS].astype(jnp.int32)
    carry = carry[:, 0, :N_EXPERTS].astype(jnp.int32)
    counts = carry[-1] + cnt_t[-1]
    padded = (counts + M - 1) // M * M
    pad_end = jnp.cumsum(padded)
    pad_start = pad_end - padded
    blk_exp = jnp.minimum(jnp.sum(pad_end[None, :] <= (jnp.arange(n_blk, dtype=jnp.int32) * M)[:, None], axis=1),
                          N_EXPERTS - 1).astype(jnp.int32)
    n_used = (pad_end[-1:] // M).astype(jnp.int32)
    blk_start = jnp.arange(n_blk, dtype=jnp.int32) * M
    eids = jnp.arange(N_EXPERTS, dtype=jnp.int32)

    def per_block(per_expert):
        return jnp.sum(jnp.where(blk_exp[:, None] == eids[None, :], per_expert[None, :], 0), axis=1).astype(jnp.int32)

    blk_rows = jnp.clip(per_block(pad_start + counts) - blk_start, 0, M)
    blk_rows = jnp.where(blk_start < pad_end[-1], (blk_rows + EXPERT_SUB - 1) // EXPERT_SUB * EXPERT_SUB, 0)
    has_rows = counts > 0
    blk_slot = per_block((jnp.cumsum(has_rows) - 1) % 2)
    later = jnp.where((eids[None, :] > eids[:, None]) & has_rows[None, :], eids[None, :], N_EXPERTS)
    next_exp = jnp.min(later, axis=1)
    blk_next = per_block(jnp.where(next_exp < N_EXPERTS, next_exp, -1))
    tile_off = (jnp.cumsum(cnt_t, axis=1) - cnt_t).reshape(-1)
    expert_off = (carry + pad_start[None, :]).reshape(-1)
    run_n = cnt_t.reshape(-1)
    tile_rows = jnp.sum(cnt_t, axis=1)
    x_rows = _dispatch(h2, posw, n_pad, tile_off, expert_off, run_n, tile_rows, pad_start + counts, padded - counts,
                       n_used, tm)
    y_rows = _experts(blk_exp, blk_rows.astype(jnp.int32), blk_slot, blk_next, x_rows, n_pad,
                      w_e_gate, b_e_gate.reshape(N_EXPERTS, 1, D_FF),
                      w_e_up, b_e_up.reshape(N_EXPERTS, 1, D_FF), w_e_down,
                      b_e_down.reshape(N_EXPERTS, 1, D_MODEL))
    return _final(tile_off, expert_off, run_n, tile_rows, x1, y_rows, posw, p2, row(g_ple_gate),
                  w_ple_gate.astype(BF16), w_ple_proj.astype(BF16), row(g_ple_proj), row(g_final), tm)


def kernel(x, p, rel_bias, g_mix, w_in, w_a2, b_a2, g_gla_out, w_up_gla, w_up_moba, w_o, g_ffn, w_router, b_router,
           w_e_gate, b_e_gate, w_e_up, b_e_up, w_e_down, b_e_down, g_ple_gate, w_ple_gate, w_ple_proj, g_ple_proj,
           g_final):
    B, S, D = x.shape
    assert D == D_MODEL and S % MOBA_BLOCK == 0 and S % GLA_CHUNK == 0 and p.shape[0] == 1
    bias = _bias_tiles(rel_bias, S // MOBA_BLOCK)
    out = _layer(x.reshape(B * S, D), p[0].reshape(B * S, PLE_DIM), bias, B, S,
                 g_mix[0], w_in[0], w_a2[0], b_a2[0], g_gla_out[0], w_up_gla[0], w_up_moba[0], w_o[0], g_ffn[0],
                 w_router[0], b_router[0], w_e_gate[0], b_e_gate[0], w_e_up[0], b_e_up[0], w_e_down[0],
                 b_e_down[0], g_ple_gate[0], w_ple_gate[0], w_ple_proj[0], g_ple_proj[0], g_final)
    return out.reshape(B, S, D)
```

```python
import functools
import math

import jax
import jax.numpy as jnp
from jax import lax
from jax.experimental import pallas as pl
from jax.experimental.pallas import tpu as pltpu

F32 = jnp.float32
BF16 = jnp.bfloat16

D_MODEL = 1024
PLE_DIM = 256
GLA_HEADS = 4
GLA_DK = 128
GLA_DV = 256
GLA_RANK = 16
GLA_TAU = 16.0
GLA_QK = GLA_HEADS * GLA_DK
GLA_V = GLA_HEADS * GLA_DV
MOBA_HEADS = 8
MOBA_HD = 128
MOBA_BLOCK = 256
MOBA_TOPK = 3
MOBA_W = MOBA_HEADS * MOBA_HD
REL_BUCKETS = 32
REL_MAX_DIST = 4096
N_EXPERTS = 32
TOP_K = 4
D_FF = 1024
SWIGLU_LIMIT = 7.0
SWIGLU_ALPHA = 1.702
EPS = 1e-6

LANES = 128
NEG = -1e30
LOG2E = math.log2(math.e)
VMEM_LIMIT = 56 * 1024 * 1024

OFF_QA, OFF_KA, OFF_VA, OFF_RA = 0, 512, 1024, 2048
OFF_QB, OFF_KB, OFF_VB, OFF_GA, OFF_GB = 3072, 4096, 5120, 6144, 7168
D_PROJ = 8192

GLA_CHUNK = 128
EXPERT_BLOCK = 512
EXPERT_SUB = 128
TOKEN_TILE = 256
RUN_ALIGN = 8
TILE_ROWS = TOP_K * TOKEN_TILE + N_EXPERTS * RUN_ALIGN
RUN_UNROLL = 8
MERGE_TILE = 512

NT = (((1,), (1,)), ((), ()))
TN = (((0,), (0,)), ((), ()))


def _params(sem, vmem=None):
    return pltpu.CompilerParams(dimension_semantics=sem, vmem_limit_bytes=vmem)


def _rms(x, g):
    return x * lax.rsqrt(jnp.mean(x * x, axis=-1, keepdims=True) + EPS) * g


def _sigmoid(x):
    return 1.0 / (1.0 + jnp.exp(-x))


def _bucket_of(n):
    max_exact = REL_BUCKETS // 2
    if n < max_exact:
        return n
    return min(max_exact + int(math.log(n / max_exact) / math.log(REL_MAX_DIST / max_exact)
                               * (REL_BUCKETS - max_exact)), REL_BUCKETS - 1)


def _bias_kernel(tab_ref, bkt_ref, o_ref, *, nblk):
    strip = 8
    m = pl.program_id(0)

    def fill(lo, hi):
        def body(s, carry):
            r0 = pl.multiple_of(s * strip, strip)
            b = bkt_ref[0, pl.ds(r0, strip), :]
            accs = [jnp.zeros(b.shape, F32) for _ in range(MOBA_HEADS)]
            for bb in range(lo, hi + 1):
                hit = b == bb
                for h in range(MOBA_HEADS):
                    accs[h] = jnp.where(hit, tab_ref[h, bb], accs[h])
            for h in range(MOBA_HEADS):
                o_ref[h, 0, pl.ds(r0, strip), :] = jnp.where(b < 0, NEG, accs[h])
            return carry

        lax.fori_loop(0, MOBA_BLOCK // strip, body, 0)

    for mm in range(nblk):
        lo = max(_bucket_of(max(mm * MOBA_BLOCK - (MOBA_BLOCK - 1), 0)) - 1, 0)
        hi = min(_bucket_of(mm * MOBA_BLOCK + MOBA_BLOCK - 1) + 1, REL_BUCKETS - 1)
        pl.when(m == mm)(functools.partial(fill, lo, hi))


def _bias_tiles(rel_bias, nblk):
    i = jnp.arange(MOBA_BLOCK, dtype=jnp.int32)
    dist = (jnp.arange(nblk, dtype=jnp.int32)[:, None, None] * MOBA_BLOCK + i[None, :, None] - i[None, None, :])
    n = jnp.maximum(dist, 0)
    max_exact = REL_BUCKETS // 2
    nf = jnp.maximum(n, 1).astype(F32)
    large = max_exact + (jnp.log(nf / max_exact) / math.log(REL_MAX_DIST / max_exact)
                         * (REL_BUCKETS - max_exact)).astype(jnp.int32)
    large = jnp.minimum(large, REL_BUCKETS - 1)
    bkt = jnp.where(dist < 0, -1, jnp.where(n < max_exact, n, large)).astype(jnp.int32)
    tab = rel_bias.astype(F32).T * LOG2E
    return pl.pallas_call(
        functools.partial(_bias_kernel, nblk=nblk),
        grid=(nblk,),
        in_specs=[pl.BlockSpec(memory_space=pltpu.SMEM),
                  pl.BlockSpec((1, MOBA_BLOCK, MOBA_BLOCK), lambda m: (m, 0, 0))],
        out_specs=pl.BlockSpec((MOBA_HEADS, 1, MOBA_BLOCK, MOBA_BLOCK), lambda m: (0, m, 0, 0)),
        out_shape=jax.ShapeDtypeStruct((MOBA_HEADS, nblk, MOBA_BLOCK, MOBA_BLOCK), F32),
        compiler_params=_params(("arbitrary",)),
        name="bias_tiles",
    )(tab, bkt)


def _inproj_kernel(x0_ref, xn_ref, g_ref, w_ref, cs_ref, wal_ref, wa2_ref, ba2_ref, o_ref, glog_ref, h_scr, al_scr,
                   *, n_j):
    i, j = pl.program_id(0), pl.program_id(1)
    tm = xn_ref.shape[0]
    rs = tm // n_j

    def gate_logits(a_low):
        a = jnp.dot(a_low, wa2_ref[...], preferred_element_type=F32) + ba2_ref[...]
        log_sig = jnp.minimum(a, 0.0) - jnp.log1p(jnp.exp(-jnp.abs(a)))
        return log_sig * (1.0 / GLA_TAU)

    def slice_rows(behind):
        return pl.multiple_of(((j + n_j - behind) % n_j) * rs, rs)

    @pl.when((i == 0) & (j == 0))
    def _():
        h0 = _rms(x0_ref[...], g_ref[...]).astype(BF16)
        h_scr[0] = h0
        a_low0 = jnp.dot(h0, wal_ref[...], preferred_element_type=F32).astype(BF16)
        glog_ref[...] = gate_logits(a_low0)
        al_scr[...] = a_low0[(n_j - 2) * rs:(n_j - 1) * rs, :]

    glog_ref[pl.ds(slice_rows(2), rs), :] = gate_logits(al_scr[...])
    slot = jnp.where(j == 0, i, i + 1) % 2
    h_lag = h_scr[slot, pl.ds(slice_rows(1), rs), :]
    al_scr[...] = jnp.dot(h_lag, wal_ref[...], preferred_element_type=F32).astype(BF16)

    acc = jnp.dot(h_scr[i % 2], w_ref[...], preferred_element_type=F32)
    o_ref[...] = (acc * cs_ref[...]).astype(BF16)

    h_scr[(i + 1) % 2, pl.ds(slice_rows(0), rs), :] = _rms(xn_ref[pl.ds(slice_rows(0), rs), :],
                                                             g_ref[...]).astype(BF16)


def _inproj(x2, g_mix, w_main, colscale, w_alow, w_a2p, b_a2, tm=1024, tn=2048):
    T = x2.shape[0]
    n_i, n_j = T // tm, D_PROJ // tn
    assert n_i >= 2 and tm % (8 * n_j) == 0
    return pl.pallas_call(
        functools.partial(_inproj_kernel, n_j=n_j),
        grid=(n_i, n_j),
        in_specs=[pl.BlockSpec((tm, D_MODEL), lambda i, j: (0, 0)),
                  pl.BlockSpec((tm, D_MODEL), lambda i, j: (jnp.minimum(i + 1, n_i - 1), 0)),
                  pl.BlockSpec((1, D_MODEL), lambda i, j: (0, 0)),
                  pl.BlockSpec((D_MODEL, tn), lambda i, j: (0, j)),
                  pl.BlockSpec((1, tn), lambda i, j: (0, j)),
                  pl.BlockSpec((D_MODEL, LANES), lambda i, j: (0, 0)),
                  pl.BlockSpec((LANES, GLA_QK), lambda i, j: (0, 0)),
                  pl.BlockSpec((1, GLA_QK), lambda i, j: (0, 0))],
        out_specs=[pl.BlockSpec((tm, tn), lambda i, j: (i, j)),
                   pl.BlockSpec((tm, GLA_QK), lambda i, j: (jnp.minimum(i + jnp.minimum(j // 2, 1), n_i - 1), 0))],
        out_shape=[jax.ShapeDtypeStruct((T, D_PROJ), BF16),
                   jax.ShapeDtypeStruct((T, GLA_QK), F32)],
        scratch_shapes=[pltpu.VMEM((2, tm, D_MODEL), BF16), pltpu.VMEM((tm // n_j, LANES), BF16)],
        compiler_params=_params(("arbitrary", "arbitrary"), VMEM_LIMIT),
        name="inproj",
    )(x2, x2, g_mix, w_main, colscale, w_alow, w_a2p, b_a2)


def _gla_kernel(q_ref, k_ref, v_ref, r_ref, g_ref, gout_ref, o_ref, st_ref):
    C = GLA_CHUNK

    @pl.when(pl.program_id(1) == 0)
    def _():
        st_ref[...] = jnp.zeros(st_ref.shape, F32)

    nb = q_ref.shape[0]
    row = lax.broadcasted_iota(jnp.int32, (C, C), 0)
    col = lax.broadcasted_iota(jnp.int32, (C, C), 1)
    causal = col <= row
    ltri = causal.astype(BF16)
    mid = C // 2
    pairs = [(b, h) for b in range(nb) for h in range(GLA_HEADS)]
    ks = lambda h: slice(h * GLA_DK, (h + 1) * GLA_DK)
    vs = lambda h: slice(h * GLA_DV, (h + 1) * GLA_DV)
    gate = {}
    for b, h in pairs:
        r = r_ref[b, :, vs(h)].astype(F32)
        gate[b, h] = r * _sigmoid(r)
    G = []
    for b in range(nb):
        g = g_ref[b]
        g_hi = g.astype(BF16)
        g_lo = (g - g_hi.astype(F32)).astype(BF16)
        G.append(jnp.dot(ltri, g_hi, preferred_element_type=F32) + jnp.dot(ltri, g_lo, preferred_element_type=F32))
    Gh = {(b, h): G[b][:, ks(h)] for b, h in pairs}
    qh = {(b, h): q_ref[b, :, ks(h)].astype(F32) for b, h in pairs}
    kh = {(b, h): k_ref[b, :, ks(h)].astype(F32) for b, h in pairs}
    g_mid = {p: Gh[p][mid:mid + 1, :] for p in pairs}
    g_last = {p: Gh[p][C - 1:C, :] for p in pairs}
    A = {p: lax.dot_general((qh[p] * jnp.exp(Gh[p] - g_mid[p])).astype(BF16),
                            (kh[p] * jnp.exp(g_mid[p] - Gh[p])).astype(BF16), NT, preferred_element_type=F32)
         for p in pairs}
    st = {p: st_ref[p[0], p[1]] for p in pairs}
    inter = {p: lax.dot_general((qh[p] * jnp.exp(Gh[p])).astype(BF16), st[p].astype(BF16), NT,
                                preferred_element_type=F32) for p in pairs}
    for b, h in pairs:
        p = (b, h)
        k_d = (kh[p] * jnp.exp(g_last[p] - Gh[p])).astype(BF16)
        st_ref[b, h] = jnp.exp(g_last[p]) * st[p] + lax.dot_general(v_ref[b, :, vs(h)], k_d, TN,
                                                                     preferred_element_type=F32)
    intra = {(b, h): jnp.dot(jnp.where(causal, A[b, h], 0.0).astype(BF16), v_ref[b, :, vs(h)],
                             preferred_element_type=F32) for b, h in pairs}
    for b, h in pairs:
        o = inter[b, h] + intra[b, h]
        o_ref[b, :, vs(h)] = (_rms(o, gout_ref[...]) * gate[b, h]).astype(BF16)


def _gla(proj, glog, g_gla_out, B, S, nb=4):
    C = GLA_CHUNK
    assert B % nb == 0
    proj3 = proj.reshape(B, S, D_PROJ)
    glog3 = glog.reshape(B, S, GLA_QK)
    spec = lambda w, off: pl.BlockSpec((nb, C, w), lambda b, c: (b, c, off // w))
    out = pl.pallas_call(
        _gla_kernel,
        grid=(B // nb, S // C),
        in_specs=[spec(GLA_QK, OFF_QA), spec(GLA_QK, OFF_KA), spec(GLA_V, OFF_VA), spec(GLA_V, OFF_RA),
                  spec(GLA_QK, 0), pl.BlockSpec((1, GLA_DV), lambda b, c: (0, 0))],
        out_specs=spec(GLA_V, 0),
        out_shape=jax.ShapeDtypeStruct((B, S, GLA_V), BF16),
        scratch_shapes=[pltpu.VMEM((nb, GLA_HEADS, GLA_DV, GLA_DK), F32)],
        compiler_params=_params(("arbitrary", "arbitrary")),
        name="gla",
    )(proj3, proj3, proj3, proj3, glog3, g_gla_out)
    return out.reshape(B * S, GLA_V)


def _moba_kernel(q_ref, k_ref, v_ref, bias_ref, o_ref, ka_scr, va_scr, lg_scr, mx_scr, qa_scr, *, nblk):
    BLK, HD = MOBA_BLOCK, MOBA_HD
    S = nblk * BLK

    @pl.when((pl.program_id(0) == 0) & (pl.program_id(1) == 0))
    def _():
        blk = lax.broadcasted_iota(jnp.int32, (S, HD), 0) // BLK
        lane = lax.broadcasted_iota(jnp.int32, (S, HD), 1)
        ka_scr[:, HD:] = (lane == blk).astype(BF16)
        va_scr[:, HD:] = (lane == 0).astype(BF16)

    n_plain = min(MOBA_TOPK + 1, nblk)

    def tile_id(cc, j):
        return cc * (cc + 1) // 2 + j

    def stage1(cc, q_in, keys):
        mx = None
        for j in range(cc + 1):
            lg = (lax.dot_general(q_in, keys[j * BLK:(j + 1) * BLK, :], NT, preferred_element_type=F32)
                  + bias_ref[0, cc - j])
            lg_scr[tile_id(cc, j)] = lg
            t = jnp.maximum(lg[:, :LANES], lg[:, LANES:])
            mx = t if mx is None else jnp.maximum(mx, t)
            if j == cc:
                mx_scr[cc] = mx
            yield

    def stage2(cc):
        m = mx_scr[cc].max(axis=-1, keepdims=True)
        acc = jnp.zeros((BLK, 2 * HD), F32)
        for j in range(cc + 1):
            p = jnp.exp2(lg_scr[tile_id(cc, j)] - m).astype(BF16)
            acc = acc + jnp.dot(p, va_scr[j * BLK:(j + 1) * BLK, :], preferred_element_type=F32)
            if j == cc:
                o_ref[cc * BLK:(cc + 1) * BLK, :] = (acc[:, :HD] / acc[:, HD:HD + 1]).astype(BF16)
            yield

    def drain(gen):
        for _ in gen:
            pass

    def interleave(main, side, n_main, n_side):
        side_steps = (s for g in side for s in g)
        done = 0
        for i, _ in enumerate(s for g in main for s in g):
            assert n_main > 0
            want = (i + 1) * n_side // n_main
            while done < want and next(side_steps, "end") != "end":
                done += 1
        drain(side_steps)

    plain = [stage1(cc, q_ref[cc * BLK:(cc + 1) * BLK, :], k_ref) for cc in range(n_plain)]
    if nblk > n_plain:
        next(plain[0])
        ksum = [k_ref[j * BLK:(j + 1) * BLK, :].astype(F32).reshape(BLK // 8, 8, HD).sum(axis=0).sum(
            axis=0, keepdims=True) for j in range(nblk)]
        kmean = jnp.concatenate(ksum, axis=0) * (1.0 / BLK)
        km_hi = kmean.astype(BF16)
        km_lo = (kmean - km_hi.astype(F32)).astype(BF16)
        km2 = jnp.concatenate([km_hi, km_lo], axis=0)
        pens = []
        for cc in range(n_plain, nblk):
            q = q_ref[cc * BLK:(cc + 1) * BLK, :]
            s2 = lax.dot_general(km2, q, NT, preferred_element_type=F32)
            pens.append((cc, q, s2[:nblk] + s2[nblk:]))
        for g in plain[:2]:
            drain(g)
        for cc, q, s in pens:
            ji = lax.broadcasted_iota(jnp.int32, s.shape, 0)
            cnt = jnp.zeros(s.shape, F32)
            for jp in range(cc):
                sj = s[jp:jp + 1, :]
                beats = (sj > s) | ((sj == s) & (jp < ji))
                cnt = cnt + beats.astype(F32)
            pen = jnp.where((ji < cc) & (cnt >= MOBA_TOPK), NEG, 0.0)
            pen_t = jnp.concatenate([pen, jnp.zeros((HD - nblk, BLK), F32)], axis=0).T
            qa_scr[cc - n_plain] = jnp.concatenate([q, pen_t.astype(BF16)], axis=1)
    ka_scr[:, :HD] = k_ref[...]
    va_scr[:, :HD] = v_ref[...]
    for g in plain:
        drain(g)

    @pl.when(pl.program_id(0) >= 0)
    def _():
        late = [stage1(cc, qa_scr[cc - n_plain], ka_scr) for cc in range(n_plain, nblk)]
        early = [stage2(cc) for cc in range(n_plain)]
        interleave(late, early, tile_id(nblk, 0) - tile_id(n_plain, 0), tile_id(n_plain, 0))

    @pl.when(pl.program_id(0) >= 0)
    def _():
        for cc in range(n_plain, nblk):
            drain(stage2(cc))


def _moba(proj, bias, B, S):
    BLK = MOBA_BLOCK
    nblk = S // BLK
    H = MOBA_HEADS
    assert nblk <= MOBA_HD
    return pl.pallas_call(
        functools.partial(_moba_kernel, nblk=nblk),
        grid=(H, B),
        in_specs=[pl.BlockSpec((S, MOBA_HD), lambda h, b: (b, OFF_QB // MOBA_HD + h)),
                  pl.BlockSpec((S, MOBA_HD), lambda h, b: (b, OFF_KB // MOBA_HD + h)),
                  pl.BlockSpec((S, MOBA_HD), lambda h, b: (b, OFF_VB // MOBA_HD + h)),
                  pl.BlockSpec((1, nblk, BLK, BLK), lambda h, b: (h, 0, 0, 0))],
        out_specs=pl.BlockSpec((S, MOBA_HD), lambda h, b: (b, h)),
        out_shape=jax.ShapeDtypeStruct((B * S, MOBA_W), BF16),
        scratch_shapes=[pltpu.VMEM((S, 2 * MOBA_HD), BF16), pltpu.VMEM((S, 2 * MOBA_HD), BF16),
                        pltpu.VMEM((nblk * (nblk + 1) // 2, BLK, BLK), F32),
                        pltpu.VMEM((nblk, BLK, LANES), F32),
                        pltpu.VMEM((max(nblk - MOBA_TOPK - 1, 1), BLK, 2 * MOBA_HD), BF16)],
        compiler_params=_params(("arbitrary", "arbitrary"), VMEM_LIMIT),
        name="moba",
    )(proj, proj, proj, bias)


def _pack_exact(lo, hi):
    lo_b = lax.bitcast_convert_type(lo, jnp.uint32)
    hi_b = lax.bitcast_convert_type(hi, jnp.uint32)
    return (lo_b >> 16) | (hi_b & jnp.uint32(0xFFFF0000))


def _unpack(w):
    lo = lax.bitcast_convert_type(w << 16, F32)
    hi = lax.bitcast_convert_type(w & jnp.uint32(0xFFFF0000), F32)
    return lo.astype(BF16), hi.astype(BF16)


def _merge_kernel(oa_ref, ob_ref, ga_ref, gb_ref, x_ref, wua_ref, wub_ref, wo_ref, gffn_ref, wr_ref, br_ref,
                  x1_ref, h2_ref, lg_ref):
    tm = x_ref.shape[0]
    u_a = jnp.dot(oa_ref[...], wua_ref[...], preferred_element_type=F32)
    u_b = jnp.dot(ob_ref[...], wub_ref[...], preferred_element_type=F32)
    y = _sigmoid(ga_ref[...].astype(F32)) * u_a + _sigmoid(gb_ref[...].astype(F32)) * u_b
    x1 = x_ref[...] + jnp.dot(y.astype(BF16), wo_ref[...], preferred_element_type=F32)
    x1_ref[...] = x1
    h2 = _rms(x1, gffn_ref[...])
    h_hi = h2.astype(BF16)
    h2_ref[...] = h_hi
    h_lo = (h2 - h_hi.astype(F32)).astype(BF16)
    r = jnp.dot(jnp.concatenate([h_hi, h_lo], axis=0), wr_ref[...], preferred_element_type=F32)
    lg_ref[...] = r[:tm, :LANES] + r[:tm, LANES:] + r[tm:, :LANES] + br_ref[...]


def _merge(o_a, o_b, proj, x2, w_ua, w_ub, w_o, g_ffn, w_r2, b_r, tm):
    T = x2.shape[0]
    full = lambda shape: pl.BlockSpec(shape, lambda i: (0, 0))
    rowblk = lambda w: pl.BlockSpec((tm, w), lambda i: (i, 0))
    return pl.pallas_call(
        _merge_kernel,
        grid=(T // tm,),
        in_specs=[rowblk(GLA_V), rowblk(MOBA_W),
                  pl.BlockSpec((tm, D_MODEL), lambda i: (i, OFF_GA // D_MODEL)),
                  pl.BlockSpec((tm, D_MODEL), lambda i: (i, OFF_GB // D_MODEL)),
                  rowblk(D_MODEL),
                  full((GLA_V, D_MODEL)), full((MOBA_W, D_MODEL)), full((D_MODEL, D_MODEL)),
                  full((1, D_MODEL)), full((D_MODEL, 2 * LANES)), full((1, LANES))],
        out_specs=[rowblk(D_MODEL), rowblk(D_MODEL), rowblk(LANES)],
        out_shape=[jax.ShapeDtypeStruct((T, D_MODEL), F32),
                   jax.ShapeDtypeStruct((T, D_MODEL), BF16),
                   jax.ShapeDtypeStruct((T, LANES), F32)],
        compiler_params=_params(("arbitrary",), VMEM_LIMIT),
        name="merge",
    )(o_a, o_b, proj, proj, x2, w_ua, w_ub, w_o, g_ffn, w_r2, b_r)


def _router_kernel(lg_ref, posw_ref, cnt_ref, carry_ref, cnt_scr, *, tm):
    rows = lg_ref.shape[0]

    @pl.when(pl.program_id(0) == 0)
    def _():
        cnt_scr[...] = jnp.zeros(cnt_scr.shape, F32)

    lane = lax.broadcasted_iota(jnp.int32, (rows, LANES), 1)
    lane_f = lane.astype(F32)
    work = jnp.where(lane < N_EXPERTS, lg_ref[...], NEG)
    vals, hots = [], []
    for _ in range(TOP_K):
        mx = work.max(axis=-1, keepdims=True)
        idx = jnp.min(jnp.where(work == mx, lane_f, float(LANES)), axis=-1, keepdims=True)
        hot = lane_f == idx
        vals.append(mx)
        hots.append(hot)
        work = jnp.where(hot, 2.0 * NEG, work)
    exps = [jnp.exp(v - vals[0]) for v in vals]
    den = exps[0] + exps[1] + exps[2] + exps[3]
    sel = jnp.zeros((rows, LANES), F32)
    for hot in hots:
        sel = sel + hot.astype(F32)
    row = lax.broadcasted_iota(jnp.int32, (tm, tm), 0)
    col = lax.broadcasted_iota(jnp.int32, (tm, tm), 1)
    below = (col < row).astype(BF16)
    er = lax.broadcasted_iota(jnp.int32, (LANES, LANES), 0)
    ec = lax.broadcasted_iota(jnp.int32, (LANES, LANES), 1)
    before = (er < ec).astype(F32)
    pos_parts = []
    for t in range(rows // tm):
        sel_t = sel[t * tm:(t + 1) * tm]
        local_rank = jnp.dot(below, sel_t.astype(BF16), preferred_element_type=F32)
        cnt_t = sel_t.sum(axis=0, keepdims=True)
        cnt_t = jnp.floor((cnt_t + (RUN_ALIGN - 1.0)) * (1.0 / RUN_ALIGN)) * RUN_ALIGN
        tile_off = jnp.dot(jnp.broadcast_to(cnt_t, (8, LANES)), before, preferred_element_type=F32,
                           precision=lax.Precision.HIGHEST)[0:1]
        pos_parts.append(local_rank + tile_off)
        carry_ref[t] = cnt_scr[...]
        cnt_ref[t] = cnt_t
        cnt_scr[...] = cnt_scr[...] + cnt_t
    pos_all = jnp.concatenate(pos_parts, axis=0)
    posw = jnp.zeros((rows, LANES), F32)
    for k in range(TOP_K):
        pk = jnp.sum(jnp.where(hots[k], pos_all, 0.0), axis=-1, keepdims=True)
        posw = jnp.where(lane == k, pk, posw)
        posw = jnp.where(lane == TOP_K + k, exps[k] / den, posw)
    posw_ref[...] = posw


def _router(logits, tm, tiles_per_step=4):
    T = logits.shape[0]
    nt = T // tm
    rows = tm * tiles_per_step
    tilerow = pl.BlockSpec((tiles_per_step, 1, LANES), lambda i: (i, 0, 0))
    return pl.pallas_call(
        functools.partial(_router_kernel, tm=tm),
        grid=(T // rows,),
        in_specs=[pl.BlockSpec((rows, LANES), lambda i: (i, 0))],
        out_specs=[pl.BlockSpec((rows, LANES), lambda i: (i, 0)), tilerow, tilerow],
        out_shape=[jax.ShapeDtypeStruct((T, LANES), F32),
                   jax.ShapeDtypeStruct((nt, 1, LANES), F32),
                   jax.ShapeDtypeStruct((nt, 1, LANES), F32)],
        scratch_shapes=[pltpu.VMEM((1, LANES), F32)],
        compiler_params=_params(("arbitrary",)),
        name="router",
    )(logits)


def _run_pieces(n, max_rows, fn):
    for b in reversed(range(RUN_ALIGN.bit_length() - 1, max_rows.bit_length())):
        size = 1 << b
        done = n & ~((2 << b) - 1)

        @pl.when((n & size) != 0)
        def _():
            fn(done, size)


def _aligned(i):
    return pl.multiple_of(i, RUN_ALIGN)


def _onehot_bands(pos, val):
    band = 256
    assert TILE_ROWS % band == 0
    n_k, n_tok = pos.shape
    pos_a = jnp.floor(pos * (1.0 / band))
    pos_b = pos - band * pos_a
    sub = lax.broadcasted_iota(jnp.int32, (band, n_tok), 0).astype(F32).astype(BF16)
    zero = jnp.zeros((band, n_tok), BF16)
    bands = []
    for a in range(TILE_ROWS // band):
        want = jnp.where(pos_a == a, pos_b, -1.0).astype(BF16)
        hit = zero
        for k in range(n_k):
            row = jnp.broadcast_to(want[k:k + 1, :], (band, n_tok))
            fill = jnp.ones((band, n_tok), BF16) if val is None else jnp.broadcast_to(val[k:k + 1, :], (band, n_tok))
            hit = hit + jnp.where(sub == row, fill, zero)
        bands.append(hit)
    return jnp.concatenate(bands, axis=0)


def _dispatch_kernel(toff_ref, eoff_ref, n_ref, trows_ref, zoff_ref, zn_ref, tail_ref, h2_ref, posw_ref, x_ref,
                     buf, zbuf, sems, *, nt):
    tm = h2_ref.shape[0]
    half = D_MODEL // 2
    j = pl.program_id(0)
    zrows = zbuf.shape[0]

    def tile_runs(t, act):
        slot = t % 2

        def body(e, c):
            r = t * N_EXPERTS + e
            t0, d0 = toff_ref[r], eoff_ref[r]
            _run_pieces(n_ref[r], tm, lambda done, size: act(pltpu.make_async_copy(
                buf.at[slot, pl.ds(_aligned(t0 + done), size)], x_ref.at[pl.ds(_aligned(d0 + done), size)],
                sems.at[slot])))
            return c

        lax.fori_loop(0, N_EXPERTS, body, 0, unroll=RUN_UNROLL)

    def zero_fill(act):
        def body(e, c):
            d0 = zoff_ref[e]
            _run_pieces(zn_ref[e], zrows, lambda done, size: act(pltpu.make_async_copy(
                zbuf.at[pl.ds(0, size)], x_ref.at[pl.ds(_aligned(d0 + done), size)], sems.at[2])))
            return c

        lax.fori_loop(0, N_EXPERTS, body, 0)

        def tail(i, c):
            act(pltpu.make_async_copy(zbuf, x_ref.at[pl.ds(pl.multiple_of(i * zrows, zrows), zrows)], sems.at[2]))
            return c

        lax.fori_loop(tail_ref[0], x_ref.shape[0] // zrows, tail, 0)

    start = lambda cp: cp.start()
    wait = lambda cp: cp.wait()

    def wait_tile(t):
        slot = t % 2
        _run_pieces(trows_ref[t], TILE_ROWS, lambda done, size: pltpu.make_async_copy(
            buf.at[slot, pl.ds(0, size)], x_ref.at[pl.ds(0, size)], sems.at[slot]).wait())

    @pl.when(j == 0)
    def _():
        zbuf[...] = jnp.zeros(zbuf.shape, zbuf.dtype)
        zero_fill(start)

    @pl.when(j >= 2)
    def _():
        wait_tile(j - 2)

    perm = _onehot_bands(posw_ref[...].T[:TOP_K], None)
    xs = jnp.dot(perm, h2_ref[...], preferred_element_type=F32)
    buf[j % 2] = _pack_exact(xs[:, :half], xs[:, half:])
    tile_runs(j, start)

    @pl.when(j == nt - 1)
    def _():
        if nt >= 2:
            wait_tile(j - 1)
        wait_tile(j)
        zero_fill(wait)


def _dispatch(h2, posw, n_rows, tile_off, expert_off, run_n, tile_rows, zoff, zn, tail, tm):
    T = h2.shape[0]
    nt = T // tm
    assert n_rows % EXPERT_BLOCK == 0
    return pl.pallas_call(
        functools.partial(_dispatch_kernel, nt=nt),
        grid_spec=pltpu.PrefetchScalarGridSpec(
            num_scalar_prefetch=7,
            grid=(nt,),
            in_specs=[pl.BlockSpec((tm, D_MODEL), lambda i, *_: (i, 0)),
                      pl.BlockSpec((tm, LANES), lambda i, *_: (i, 0))],
            out_specs=pl.BlockSpec(memory_space=pl.ANY),
            scratch_shapes=[pltpu.VMEM((2, TILE_ROWS, D_MODEL // 2), jnp.uint32),
                            pltpu.VMEM((EXPERT_BLOCK, D_MODEL // 2), jnp.uint32),
                            pltpu.SemaphoreType.DMA((3,))]),
        out_shape=jax.ShapeDtypeStruct((n_rows, D_MODEL // 2), jnp.uint32),
        compiler_params=_params(("arbitrary",), VMEM_LIMIT),
        name="dispatch",
    )(tile_off, expert_off, run_n, tile_rows, zoff, zn, tail, h2, posw)


def _expert_kernel(be_ref, rows_ref, slot_ref, next_ref, x_ref, wg_ref, bg_ref, wu_ref, bu_ref, wd_ref, bd_ref, y_ref,
                   w_in, wg_s, wu_s, wd_s, sems):
    i = pl.program_id(0)
    half = D_MODEL // 2
    M = x_ref.shape[0]
    rows = rows_ref[i]
    e = be_ref[i]
    prev = be_ref[jnp.maximum(i - 1, 0)]

    def weight_copies(expert, slot):
        return [pltpu.make_async_copy(w_hbm.at[expert], w_in.at[slot, k], sems.at[slot, k])
                for k, w_hbm in enumerate((wg_ref, wu_ref, wd_ref))]

    @pl.when((rows > 0) & ((i == 0) | (e != prev)))
    def _():
        slot = slot_ref[i]

        @pl.when(i == 0)
        def _():
            for cp in weight_copies(e, slot):
                cp.start()

        nxt = next_ref[i]

        @pl.when(nxt >= 0)
        def _():
            for cp in weight_copies(nxt, 1 - slot):
                cp.start()

        for cp in weight_copies(e, slot):
            cp.wait()
        for k, dst in enumerate((wg_s, wu_s, wd_s)):
            dst[...] = w_in[slot, k].astype(BF16)

    def compute(r):
        x_lo, x_hi = _unpack(x_ref[:r, :])

        def proj_in(w_s, b_ref):
            return (jnp.dot(x_lo, w_s[:half, :], preferred_element_type=F32)
                    + jnp.dot(x_hi, w_s[half:, :], preferred_element_type=F32) + b_ref[0])

        gate = jnp.minimum(proj_in(wg_s, bg_ref), SWIGLU_LIMIT)
        up = jnp.clip(proj_in(wu_s, bu_ref), -SWIGLU_LIMIT, SWIGLU_LIMIT)
        glu = gate * _sigmoid(gate * SWIGLU_ALPHA)
        act = ((up + 1.0) * glu).astype(BF16)
        y = jnp.dot(act, wd_s[...], preferred_element_type=F32) + bd_ref[0]
        y_ref[:r, :] = y
        if r < M:
            y_ref[r:, :] = jnp.zeros((M - r, D_MODEL), y_ref.dtype)

    for r in range(EXPERT_SUB, M + 1, EXPERT_SUB):
        pl.when(rows == r)(functools.partial(compute, r))

    @pl.when(rows == 0)
    def _():
        y_ref[...] = jnp.zeros(y_ref.shape, y_ref.dtype)


def _experts(blk_exp, blk_rows, blk_slot, blk_next, x_rows, n_pad, w_g, b_g, w_u, b_u, w_d, b_d):
    M = EXPERT_BLOCK
    assert D_FF == D_MODEL
    bspec = lambda n: pl.BlockSpec((1, 1, n), lambda i, be, *_: (be[i], 0, 0))
    wspec = pl.BlockSpec(memory_space=pl.ANY)
    return pl.pallas_call(
        _expert_kernel,
        grid_spec=pltpu.PrefetchScalarGridSpec(
            num_scalar_prefetch=4,
            grid=(n_pad // M,),
            in_specs=[pl.BlockSpec((M, D_MODEL // 2), lambda i, *_: (i, 0)),
                      wspec, bspec(D_FF), wspec, bspec(D_FF), wspec, bspec(D_MODEL)],
            out_specs=pl.BlockSpec((M, D_MODEL), lambda i, *_: (i, 0)),
            scratch_shapes=[pltpu.VMEM((2, 3, D_MODEL, D_FF), F32),
                            pltpu.VMEM((D_MODEL, D_FF), BF16),
                            pltpu.VMEM((D_MODEL, D_FF), BF16),
                            pltpu.VMEM((D_FF, D_MODEL), BF16),
                            pltpu.SemaphoreType.DMA((2, 3))]),
        out_shape=jax.ShapeDtypeStruct((n_pad, D_MODEL), F32),
        compiler_params=_params(("arbitrary",), VMEM_LIMIT),
        name="experts",
    )(blk_exp, blk_rows, blk_slot, blk_next, x_rows, w_g, b_g, w_u, b_u, w_d, b_d)


def _final_kernel(toff_ref, eoff_ref, n_ref, trows_ref, x1_ref, posw_ref, p_ref, gpg_ref, wpg_ref, wpp_ref, gpp_ref,
                  gfin_ref, y_ref, o_ref, buf, xmid, sems, *, nt):
    tm = x1_ref.shape[0]
    step = pl.program_id(0)
    j = jnp.minimum(step, nt - 1)

    def tile_runs(t, act):
        slot = t % 2

        def body(e, c):
            r = t * N_EXPERTS + e
            t0, s0 = toff_ref[r], eoff_ref[r]
            _run_pieces(n_ref[r], tm, lambda done, size: act(pltpu.make_async_copy(
                y_ref.at[pl.ds(_aligned(s0 + done), size)], buf.at[slot, pl.ds(_aligned(t0 + done), size)],
                sems.at[slot])))
            return c

        lax.fori_loop(0, N_EXPERTS, body, 0, unroll=RUN_UNROLL)

    @pl.when(step == 0)
    def _():
        buf[...] = jnp.zeros(buf.shape, buf.dtype)
        xmid[...] = jnp.zeros(xmid.shape, xmid.dtype)
        tile_runs(0, lambda cp: cp.start())

    @pl.when(step + 1 < nt)
    def _():
        tile_runs(step + 1, lambda cp: cp.start())

    @pl.when(step < nt)
    def _():
        _run_pieces(trows_ref[j], TILE_ROWS, lambda done, size: pltpu.make_async_copy(
            y_ref.at[pl.ds(0, size)], buf.at[j % 2, pl.ds(0, size)], sems.at[j % 2]).wait())

    x = xmid[...]
    pp = jnp.dot(p_ref[...].astype(BF16), wpp_ref[...], preferred_element_type=F32)
    pg = _sigmoid(jnp.dot(_rms(x, gpg_ref[...]).astype(BF16), wpg_ref[...], preferred_element_type=F32))
    x = x + pg * _rms(pp, gpp_ref[...])
    o_ref[...] = _rms(x, gfin_ref[...])

    posw_t = posw_ref[...].T
    comb_t = _onehot_bands(posw_t[:TOP_K], posw_t[TOP_K:2 * TOP_K].astype(BF16))
    sure = TOP_K * tm
    tail = buf[j % 2, sure:, :]
    live = lax.broadcasted_iota(jnp.int32, tail.shape, 0) < trows_ref[j] - sure
    y = jnp.concatenate([buf[j % 2, :sure, :], jnp.where(live, tail, 0.0)], axis=0).astype(BF16)
    xmid[...] = x1_ref[...] + lax.dot_general(comb_t, y, TN, preferred_element_type=F32)


def _final(tile_off, expert_off, run_n, tile_rows, x1, y_rows, posw, p2, g_pg, w_pg, w_pp, g_pp, g_fin, tm):
    T = x1.shape[0]
    nt = T // tm
    full = lambda shape: pl.BlockSpec(shape, lambda i, *_: (0, 0))
    this = lambda w: pl.BlockSpec((tm, w), lambda i, *_: (jnp.minimum(i, nt - 1), 0))
    prev = lambda w: pl.BlockSpec((tm, w), lambda i, *_: (jnp.maximum(i - 1, 0), 0))
    return pl.pallas_call(
        functools.partial(_final_kernel, nt=nt),
        grid_spec=pltpu.PrefetchScalarGridSpec(
            num_scalar_prefetch=4,
            grid=(nt + 1,),
            in_specs=[this(D_MODEL), this(LANES), prev(PLE_DIM),
                      full((1, D_MODEL)), full((D_MODEL, D_MODEL)), full((PLE_DIM, D_MODEL)),
                      full((1, D_MODEL)), full((1, D_MODEL)),
                      pl.BlockSpec(memory_space=pl.ANY)],
            out_specs=prev(D_MODEL),
            scratch_shapes=[pltpu.VMEM((2, TILE_ROWS, D_MODEL), F32),
                            pltpu.VMEM((tm, D_MODEL), F32),
                            pltpu.SemaphoreType.DMA((2,))]),
        out_shape=jax.ShapeDtypeStruct((T, D_MODEL), F32),
        compiler_params=_params(("arbitrary",), VMEM_LIMIT),
        name="final",
    )(tile_off, expert_off, run_n, tile_rows, x1, posw, p2, g_pg, w_pg, w_pp, g_pp, g_fin, y_rows)


def _split_w_in(w_in):
    sizes = (GLA_QK, GLA_QK, GLA_V, GLA_V, GLA_RANK, MOBA_W, MOBA_W, MOBA_W, D_MODEL, D_MODEL)
    offs = [0]
    for s in sizes:
        offs.append(offs[-1] + s)
    main = jnp.concatenate([w_in[:, :offs[4]], w_in[:, offs[5]:]], axis=1).astype(BF16)
    alow = jnp.pad(w_in[:, offs[4]:offs[5]], ((0, 0), (0, LANES - GLA_RANK))).astype(BF16)
    return main, alow


def _layer(x2, p2, bias, B, S, g_mix, w_in, w_a2, b_a2, g_gla_out, w_up_gla, w_up_moba, w_o, g_ffn, w_router,
           b_router, w_e_gate, b_e_gate, w_e_up, b_e_up, w_e_down, b_e_down, g_ple_gate, w_ple_gate, w_ple_proj,
           g_ple_proj, g_final):
    T = B * S
    row = lambda v: v.reshape(1, -1).astype(F32)
    w_main, w_alow = _split_w_in(w_in)
    colscale = jnp.ones((D_PROJ,), F32)
    colscale = colscale.at[OFF_QA:OFF_QA + GLA_QK].set(GLA_DK ** -0.5)
    colscale = colscale.at[OFF_QB:OFF_QB + MOBA_W].set(MOBA_HD ** -0.5 * LOG2E)
    w_a2p = jnp.pad(w_a2, ((0, LANES - GLA_RANK), (0, 0))).astype(BF16)
    proj, glog = _inproj(x2, row(g_mix), w_main, colscale.reshape(1, -1), w_alow, w_a2p, row(b_a2))

    o_a = _gla(proj, glog, row(g_gla_out), B, S)
    o_b = _moba(proj, bias, B, S)

    w_r = jnp.pad(w_router.astype(F32), ((0, 0), (0, LANES - N_EXPERTS)))
    w_r_hi = w_r.astype(BF16)
    w_r2 = jnp.concatenate([w_r_hi, (w_r - w_r_hi.astype(F32)).astype(BF16)], axis=1)
    b_r = jnp.pad(b_router.astype(F32), (0, LANES - N_EXPERTS)).reshape(1, -1)
    tm = TOKEN_TILE
    nt = T // tm
    x1, h2, logits = _merge(o_a, o_b, proj, x2, w_up_gla.astype(BF16), w_up_moba.astype(BF16),
                            w_o.astype(BF16), row(g_ffn), w_r2, b_r, MERGE_TILE)
    posw, cnt_t, carry = _router(logits, tm)

    M = EXPERT_BLOCK
    A = nt * TILE_ROWS
    n_pad = (-(-A // M)) * M + N_EXPERTS * M
    n_blk = n_pad // M
    cnt_t = cnt_t[:, 0, :N_EXPERTS].astype(jnp.int32)
    carry = carry[:, 0, :N_EXPERTS].astype(jnp.int32)
    counts = carry[-1] + cnt_t[-1]
    padded = (counts + M - 1) // M * M
    pad_end = jnp.cumsum(padded)
    pad_start = pad_end - padded
    blk_exp = jnp.minimum(jnp.sum(pad_end[None, :] <= (jnp.arange(n_blk, dtype=jnp.int32) * M)[:, None], axis=1),
                          N_EXPERTS - 1).astype(jnp.int32)
    n_used = (pad_end[-1:] // M).astype(jnp.int32)
    blk_start = jnp.arange(n_blk, dtype=jnp.int32) * M
    eids = jnp.arange(N_EXPERTS, dtype=jnp.int32)

    def per_block(per_expert):
        return jnp.sum(jnp.where(blk_exp[:, None] == eids[None, :], per_expert[None, :], 0), axis=1).astype(jnp.int32)

    blk_rows = jnp.clip(per_block(pad_start + counts) - blk_start, 0, M)
    blk_rows = jnp.where(blk_start < pad_end[-1], (blk_rows + EXPERT_SUB - 1) // EXPERT_SUB * EXPERT_SUB, 0)
    has_rows = counts > 0
    blk_slot = per_block((jnp.cumsum(has_rows) - 1) % 2)
    later = jnp.where((eids[None, :] > eids[:, None]) & has_rows[None, :], eids[None, :], N_EXPERTS)
    next_exp = jnp.min(later, axis=1)
    blk_next = per_block(jnp.where(next_exp < N_EXPERTS, next_exp, -1))
    tile_off = (jnp.cumsum(cnt_t, axis=1) - cnt_t).reshape(-1)
    expert_off = (carry + pad_start[None, :]).reshape(-1)
    run_n = cnt_t.reshape(-1)
    tile_rows = jnp.sum(cnt_t, axis=1)
    x_rows = _dispatch(h2, posw, n_pad, tile_off, expert_off, run_n, tile_rows, pad_start + counts, padded - counts,
                       n_used, tm)
    y_rows = _experts(blk_exp, blk_rows.astype(jnp.int32), blk_slot, blk_next, x_rows, n_pad,
                      w_e_gate, b_e_gate.reshape(N_EXPERTS, 1, D_FF),
                      w_e_up, b_e_up.reshape(N_EXPERTS, 1, D_FF), w_e_down,
                      b_e_down.reshape(N_EXPERTS, 1, D_MODEL))
    return _final(tile_off, expert_off, run_n, tile_rows, x1, y_rows, posw, p2, row(g_ple_gate),
                  w_ple_gate.astype(BF16), w_ple_proj.astype(BF16), row(g_ple_proj), row(g_final), tm)


def kernel(x, p, rel_bias, g_mix, w_in, w_a2, b_a2, g_gla_out, w_up_gla, w_up_moba, w_o, g_ffn, w_router, b_router,
           w_e_gate, b_e_gate, w_e_up, b_e_up, w_e_down, b_e_down, g_ple_gate, w_ple_gate, w_ple_proj, g_ple_proj,
           g_final):
    B, S, D = x.shape
    assert D == D_MODEL and S % MOBA_BLOCK == 0 and S % GLA_CHUNK == 0 and p.shape[0] == 1
    bias = _bias_tiles(rel_bias, S // MOBA_BLOCK)
    out = _layer(x.reshape(B * S, D), p[0].reshape(B * S, PLE_DIM), bias, B, S,
                 g_mix[0], w_in[0], w_a2[0], b_a2[0], g_gla_out[0], w_up_gla[0], w_up_moba[0], w_o[0], g_ffn[0],
                 w_router[0], b_router[0], w_e_gate[0], b_e_gate[0], w_e_up[0], b_e_up[0], w_e_down[0],
                 b_e_down[0], g_ple_gate[0], w_ple_gate[0], w_ple_proj[0], g_ple_proj[0], g_final)
    return out.reshape(B, S, D)
```

```python
import functools
import math

import jax
import jax.numpy as jnp
from jax import lax
from jax.experimental import pallas as pl
from jax.experimental.pallas import tpu as pltpu

F32 = jnp.float32
BF16 = jnp.bfloat16

D_MODEL = 1024
PLE_DIM = 256
GLA_HEADS = 4
GLA_DK = 128
GLA_DV = 256
GLA_RANK = 16
GLA_TAU = 16.0
GLA_QK = GLA_HEADS * GLA_DK
GLA_V = GLA_HEADS * GLA_DV
MOBA_HEADS = 8
MOBA_HD = 128
MOBA_BLOCK = 256
MOBA_TOPK = 3
MOBA_W = MOBA_HEADS * MOBA_HD
REL_BUCKETS = 32
REL_MAX_DIST = 4096
N_EXPERTS = 32
TOP_K = 4
D_FF = 1024
SWIGLU_LIMIT = 7.0
SWIGLU_ALPHA = 1.702
EPS = 1e-6

LANES = 128
NEG = -1e30
LOG2E = math.log2(math.e)
VMEM_LIMIT = 56 * 1024 * 1024

OFF_QA, OFF_KA, OFF_VA, OFF_RA = 0, 512, 1024, 2048
OFF_QB, OFF_KB, OFF_VB, OFF_GA, OFF_GB = 3072, 4096, 5120, 6144, 7168
D_PROJ = 8192

GLA_CHUNK = 128
EXPERT_BLOCK = 512
EXPERT_SUB = 128
TOKEN_TILE = 256
RUN_ALIGN = 8
TILE_ROWS = TOP_K * TOKEN_TILE + N_EXPERTS * RUN_ALIGN
RUN_UNROLL = 8
MERGE_TILE = 512

NT = (((1,), (1,)), ((), ()))
TN = (((0,), (0,)), ((), ()))


def _params(sem, vmem=None):
    return pltpu.CompilerParams(dimension_semantics=sem, vmem_limit_bytes=vmem)


def _rms(x, g):
    return x * lax.rsqrt(jnp.mean(x * x, axis=-1, keepdims=True) + EPS) * g


def _sigmoid(x):
    return 1.0 / (1.0 + jnp.exp(-x))


def _bucket_of(n):
    max_exact = REL_BUCKETS // 2
    if n < max_exact:
        return n
    return min(max_exact + int(math.log(n / max_exact) / math.log(REL_MAX_DIST / max_exact)
                               * (REL_BUCKETS - max_exact)), REL_BUCKETS - 1)


def _bias_kernel(tab_ref, bkt_ref, o_ref, *, nblk):
    strip = 8
    m = pl.program_id(0)

    def fill(lo, hi):
        def body(s, carry):
            r0 = pl.multiple_of(s * strip, strip)
            b = bkt_ref[0, pl.ds(r0, strip), :]
            accs = [jnp.zeros(b.shape, F32) for _ in range(MOBA_HEADS)]
            for bb in range(lo, hi + 1):
                hit = b == bb
                for h in range(MOBA_HEADS):
                    accs[h] = jnp.where(hit, tab_ref[h, bb], accs[h])
            for h in range(MOBA_HEADS):
                o_ref[h, 0, pl.ds(r0, strip), :] = jnp.where(b < 0, NEG, accs[h])
            return carry

        lax.fori_loop(0, MOBA_BLOCK // strip, body, 0)

    for mm in range(nblk):
        lo = max(_bucket_of(max(mm * MOBA_BLOCK - (MOBA_BLOCK - 1), 0)) - 1, 0)
        hi = min(_bucket_of(mm * MOBA_BLOCK + MOBA_BLOCK - 1) + 1, REL_BUCKETS - 1)
        pl.when(m == mm)(functools.partial(fill, lo, hi))


def _bias_tiles(rel_bias, nblk):
    i = jnp.arange(MOBA_BLOCK, dtype=jnp.int32)
    dist = (jnp.arange(nblk, dtype=jnp.int32)[:, None, None] * MOBA_BLOCK + i[None, :, None] - i[None, None, :])
    n = jnp.maximum(dist, 0)
    max_exact = REL_BUCKETS // 2
    nf = jnp.maximum(n, 1).astype(F32)
    large = max_exact + (jnp.log(nf / max_exact) / math.log(REL_MAX_DIST / max_exact)
                         * (REL_BUCKETS - max_exact)).astype(jnp.int32)
    large = jnp.minimum(large, REL_BUCKETS - 1)
    bkt = jnp.where(dist < 0, -1, jnp.where(n < max_exact, n, large)).astype(jnp.int32)
    tab = rel_bias.astype(F32).T * LOG2E
    return pl.pallas_call(
        functools.partial(_bias_kernel, nblk=nblk),
        grid=(nblk,),
        in_specs=[pl.BlockSpec(memory_space=pltpu.SMEM),
                  pl.BlockSpec((1, MOBA_BLOCK, MOBA_BLOCK), lambda m: (m, 0, 0))],
        out_specs=pl.BlockSpec((MOBA_HEADS, 1, MOBA_BLOCK, MOBA_BLOCK), lambda m: (0, m, 0, 0)),
        out_shape=jax.ShapeDtypeStruct((MOBA_HEADS, nblk, MOBA_BLOCK, MOBA_BLOCK), F32),
        compiler_params=_params(("arbitrary",)),
        name="bias_tiles",
    )(tab, bkt)


def _inproj_kernel(x0_ref, xn_ref, g_ref, w_ref, cs_ref, wal_ref, wa2_ref, ba2_ref, o_ref, glog_ref, h_scr, al_scr,
                   *, n_j):
    i, j = pl.program_id(0), pl.program_id(1)
    tm = xn_ref.shape[0]
    rs = tm // n_j

    def gate_logits(a_low):
        a = jnp.dot(a_low, wa2_ref[...], preferred_element_type=F32) + ba2_ref[...]
        log_sig = jnp.minimum(a, 0.0) - jnp.log1p(jnp.exp(-jnp.abs(a)))
        return log_sig * (1.0 / GLA_TAU)

    def slice_rows(behind):
        return pl.multiple_of(((j + n_j - behind) % n_j) * rs, rs)

    @pl.when((i == 0) & (j == 0))
    def _():
        h0 = _rms(x0_ref[...], g_ref[...]).astype(BF16)
        h_scr[0] = h0
        a_low0 = jnp.dot(h0, wal_ref[...], preferred_element_type=F32).astype(BF16)
        glog_ref[...] = gate_logits(a_low0)
        al_scr[...] = a_low0[(n_j - 2) * rs:(n_j - 1) * rs, :]

    glog_ref[pl.ds(slice_rows(2), rs), :] = gate_logits(al_scr[...])
    slot = jnp.where(j == 0, i, i + 1) % 2
    h_lag = h_scr[slot, pl.ds(slice_rows(1), rs), :]
    al_scr[...] = jnp.dot(h_lag, wal_ref[...], preferred_element_type=F32).astype(BF16)

    acc = jnp.dot(h_scr[i % 2], w_ref[...], preferred_element_type=F32)
    o_ref[...] = (acc * cs_ref[...]).astype(BF16)

    h_scr[(i + 1) % 2, pl.ds(slice_rows(0), rs), :] = _rms(xn_ref[pl.ds(slice_rows(0), rs), :],
                                                             g_ref[...]).astype(BF16)


def _inproj(x2, g_mix, w_main, colscale, w_alow, w_a2p, b_a2, tm=1024, tn=2048):
    T = x2.shape[0]
    n_i, n_j = T // tm, D_PROJ // tn
    assert n_i >= 2 and tm % (8 * n_j) == 0
    return pl.pallas_call(
        functools.partial(_inproj_kernel, n_j=n_j),
        grid=(n_i, n_j),
        in_specs=[pl.BlockSpec((tm, D_MODEL), lambda i, j: (0, 0)),
                  pl.BlockSpec((tm, D_MODEL), lambda i, j: (jnp.minimum(i + 1, n_i - 1), 0)),
                  pl.BlockSpec((1, D_MODEL), lambda i, j: (0, 0)),
                  pl.BlockSpec((D_MODEL, tn), lambda i, j: (0, j)),
                  pl.BlockSpec((1, tn), lambda i, j: (0, j)),
                  pl.BlockSpec((D_MODEL, LANES), lambda i, j: (0, 0)),
                  pl.BlockSpec((LANES, GLA_QK), lambda i, j: (0, 0)),
                  pl.BlockSpec((1, GLA_QK), lambda i, j: (0, 0))],
        out_specs=[pl.BlockSpec((tm, tn), lambda i, j: (i, j)),
                   pl.BlockSpec((tm, GLA_QK), lambda i, j: (jnp.minimum(i + jnp.minimum(j // 2, 1), n_i - 1), 0))],
        out_shape=[jax.ShapeDtypeStruct((T, D_PROJ), BF16),
                   jax.ShapeDtypeStruct((T, GLA_QK), F32)],
        scratch_shapes=[pltpu.VMEM((2, tm, D_MODEL), BF16), pltpu.VMEM((tm // n_j, LANES), BF16)],
        compiler_params=_params(("arbitrary", "arbitrary"), VMEM_LIMIT),
        name="inproj",
    )(x2, x2, g_mix, w_main, colscale, w_alow, w_a2p, b_a2)


def _gla_kernel(q_ref, k_ref, v_ref, r_ref, g_ref, gout_ref, o_ref, st_ref):
    C = GLA_CHUNK

    @pl.when(pl.program_id(1) == 0)
    def _():
        st_ref[...] = jnp.zeros(st_ref.shape, F32)

    nb = q_ref.shape[0]
    row = lax.broadcasted_iota(jnp.int32, (C, C), 0)
    col = lax.broadcasted_iota(jnp.int32, (C, C), 1)
    causal = col <= row
    ltri = causal.astype(BF16)
    mid = C // 2
    pairs = [(b, h) for b in range(nb) for h in range(GLA_HEADS)]
    ks = lambda h: slice(h * GLA_DK, (h + 1) * GLA_DK)
    vs = lambda h: slice(h * GLA_DV, (h + 1) * GLA_DV)
    gate = {}
    for b, h in pairs:
        r = r_ref[b, :, vs(h)].astype(F32)
        gate[b, h] = r * _sigmoid(r)
    G = []
    for b in range(nb):
        g = g_ref[b]
        g_hi = g.astype(BF16)
        g_lo = (g - g_hi.astype(F32)).astype(BF16)
        G.append(jnp.dot(ltri, g_hi, preferred_element_type=F32) + jnp.dot(ltri, g_lo, preferred_element_type=F32))
    Gh = {(b, h): G[b][:, ks(h)] for b, h in pairs}
    qh = {(b, h): q_ref[b, :, ks(h)].astype(F32) for b, h in pairs}
    kh = {(b, h): k_ref[b, :, ks(h)].astype(F32) for b, h in pairs}
    g_mid = {p: Gh[p][mid:mid + 1, :] for p in pairs}
    g_last = {p: Gh[p][C - 1:C, :] for p in pairs}
    A = {p: lax.dot_general((qh[p] * jnp.exp(Gh[p] - g_mid[p])).astype(BF16),
                            (kh[p] * jnp.exp(g_mid[p] - Gh[p])).astype(BF16), NT, preferred_element_type=F32)
         for p in pairs}
    st = {p: st_ref[p[0], p[1]] for p in pairs}
    inter = {p: lax.dot_general((qh[p] * jnp.exp(Gh[p])).astype(BF16), st[p].astype(BF16), NT,
                                preferred_element_type=F32) for p in pairs}
    for b, h in pairs:
        p = (b, h)
        k_d = (kh[p] * jnp.exp(g_last[p] - Gh[p])).astype(BF16)
        st_ref[b, h] = jnp.exp(g_last[p]) * st[p] + lax.dot_general(v_ref[b, :, vs(h)], k_d, TN,
                                                                     preferred_element_type=F32)
    intra = {(b, h): jnp.dot(jnp.where(causal, A[b, h], 0.0).astype(BF16), v_ref[b, :, vs(h)],
                             preferred_element_type=F32) for b, h in pairs}
    for b, h in pairs:
        o = inter[b, h] + intra[b, h]
        o_ref[b, :, vs(h)] = (_rms(o, gout_ref[...]) * gate[b, h]).astype(BF16)


def _gla(proj, glog, g_gla_out, B, S, nb=4):
    C = GLA_CHUNK
    assert B % nb == 0
    proj3 = proj.reshape(B, S, D_PROJ)
    glog3 = glog.reshape(B, S, GLA_QK)
    spec = lambda w, off: pl.BlockSpec((nb, C, w), lambda b, c: (b, c, off // w))
    out = pl.pallas_call(
        _gla_kernel,
        grid=(B // nb, S // C),
        in_specs=[spec(GLA_QK, OFF_QA), spec(GLA_QK, OFF_KA), spec(GLA_V, OFF_VA), spec(GLA_V, OFF_RA),
                  spec(GLA_QK, 0), pl.BlockSpec((1, GLA_DV), lambda b, c: (0, 0))],
        out_specs=spec(GLA_V, 0),
        out_shape=jax.ShapeDtypeStruct((B, S, GLA_V), BF16),
        scratch_shapes=[pltpu.VMEM((nb, GLA_HEADS, GLA_DV, GLA_DK), F32)],
        compiler_params=_params(("arbitrary", "arbitrary")),
        name="gla",
    )(proj3, proj3, proj3, proj3, glog3, g_gla_out)
    return out.reshape(B * S, GLA_V)


def _moba_kernel(q_ref, k_ref, v_ref, bias_ref, o_ref, ka_scr, va_scr, lg_scr, mx_scr, qa_scr, *, nblk):
    BLK, HD = MOBA_BLOCK, MOBA_HD
    S = nblk * BLK
    units = range(q_ref.shape[0])

    @pl.when((pl.program_id(0) == 0) & (pl.program_id(1) == 0))
    def _():
        blk = lax.broadcasted_iota(jnp.int32, (S, HD), 0) // BLK
        lane = lax.broadcasted_iota(jnp.int32, (S, HD), 1)
        for u in units:
            ka_scr[u, :, HD:] = (lane == blk).astype(BF16)
            va_scr[u, :, HD:] = (lane == 0).astype(BF16)

    n_plain = min(MOBA_TOPK + 1, nblk)
    n_late = nblk - n_plain
    n_tiles = nblk * (nblk + 1) // 2

    def tile_id(cc, j):
        return cc * (cc + 1) // 2 + j

    def rows(cc):
        return slice(cc * BLK, (cc + 1) * BLK)

    def stage1(u, cc, q_in, keys):
        mx = None
        for j in range(cc + 1):
            lg = lax.dot_general(q_in, keys(j), NT, preferred_element_type=F32) + bias_ref[0, cc - j]
            lg_scr[u * n_tiles + tile_id(cc, j)] = lg
            t = jnp.maximum(lg[:, :LANES], lg[:, LANES:])
            mx = t if mx is None else jnp.maximum(mx, t)
            if j == cc:
                mx_scr[u * nblk + cc] = mx
            yield

    def stage2(u, cc):
        m = mx_scr[u * nblk + cc].max(axis=-1, keepdims=True)
        acc = jnp.zeros((BLK, 2 * HD), F32)
        for j in range(cc + 1):
            p = jnp.exp2(lg_scr[u * n_tiles + tile_id(cc, j)] - m).astype(BF16)
            acc = acc + jnp.dot(p, va_scr[u, rows(j), :], preferred_element_type=F32)
            if j == cc:
                o_ref[u, rows(cc), :] = (acc[:, :HD] / acc[:, HD:HD + 1]).astype(BF16)
            yield

    def drain(gen):
        for _ in gen:
            pass

    def interleave(main, side, n_main, n_side):
        side_steps = (s for g in side for s in g)
        done = 0
        for i, _ in enumerate(s for g in main for s in g):
            assert n_main > 0
            want = (i + 1) * n_side // n_main
            while done < want and next(side_steps, "end") != "end":
                done += 1
        drain(side_steps)

    plain = {u: [stage1(u, cc, q_ref[u, rows(cc), :], lambda j, u=u: k_ref[u, rows(j), :]) for cc in range(n_plain)]
             for u in units}
    if n_late:
        pens = []
        for u in units:
            next(plain[u][0])
            ksum = [k_ref[u, rows(j), :].astype(F32).reshape(BLK // 8, 8, HD).sum(axis=0).sum(axis=0, keepdims=True)
                    for j in range(nblk)]
            kmean = jnp.concatenate(ksum, axis=0) * (1.0 / BLK)
            km_hi = kmean.astype(BF16)
            km_lo = (kmean - km_hi.astype(F32)).astype(BF16)
            km2 = jnp.concatenate([km_hi, km_lo], axis=0)
            for cc in range(n_plain, nblk):
                q = q_ref[u, rows(cc), :]
                s2 = lax.dot_general(km2, q, NT, preferred_element_type=F32)
                pens.append((u, cc, q, s2[:nblk] + s2[nblk:]))
        for u in units:
            for g in plain[u][:2]:
                drain(g)
        for u, cc, q, s in pens:
            ji = lax.broadcasted_iota(jnp.int32, s.shape, 0)
            cnt = jnp.zeros(s.shape, F32)
            for jp in range(cc):
                sj = s[jp:jp + 1, :]
                beats = (sj > s) | ((sj == s) & (jp < ji))
                cnt = cnt + beats.astype(F32)
            pen = jnp.where((ji < cc) & (cnt >= MOBA_TOPK), NEG, 0.0)
            pen_t = jnp.concatenate([pen, jnp.zeros((HD - nblk, BLK), F32)], axis=0).T
            qa_scr[u * n_late + cc - n_plain] = jnp.concatenate([q, pen_t.astype(BF16)], axis=1)
    for u in units:
        ka_scr[u, :, :HD] = k_ref[u]
        va_scr[u, :, :HD] = v_ref[u]
    for u in units:
        for g in plain[u]:
            drain(g)

    @pl.when(pl.program_id(0) >= 0)
    def _():
        late = [stage1(u, cc, qa_scr[u * n_late + cc - n_plain], lambda j, u=u: ka_scr[u, rows(j), :])
                for u in units for cc in range(n_plain, nblk)]
        early = [stage2(u, cc) for u in units for cc in range(n_plain)]
        if late:
            interleave(late, early, len(units) * (n_tiles - tile_id(n_plain, 0)), len(units) * tile_id(n_plain, 0))
        else:
            for g in early:
                drain(g)

    @pl.when(pl.program_id(0) >= 0)
    def _():
        for u in units:
            for cc in range(n_plain, nblk):
                drain(stage2(u, cc))


def _moba(proj, bias, B, S, nb=2):
    BLK = MOBA_BLOCK
    nblk = S // BLK
    H = MOBA_HEADS
    assert nblk <= MOBA_HD and B % nb == 0
    proj3 = proj.reshape(B, S, D_PROJ)
    spec = lambda off: pl.BlockSpec((nb, S, MOBA_HD), lambda h, b: (b, 0, off // MOBA_HD + h))
    out = pl.pallas_call(
        functools.partial(_moba_kernel, nblk=nblk),
        grid=(H, B // nb),
        in_specs=[spec(OFF_QB), spec(OFF_KB), spec(OFF_VB),
                  pl.BlockSpec((1, nblk, BLK, BLK), lambda h, b: (h, 0, 0, 0))],
        out_specs=spec(0),
        out_shape=jax.ShapeDtypeStruct((B, S, MOBA_W), BF16),
        scratch_shapes=[pltpu.VMEM((nb, S, 2 * MOBA_HD), BF16), pltpu.VMEM((nb, S, 2 * MOBA_HD), BF16),
                        pltpu.VMEM((nb * nblk * (nblk + 1) // 2, BLK, BLK), F32),
                        pltpu.VMEM((nb * nblk, BLK, LANES), F32),
                        pltpu.VMEM((nb * max(nblk - MOBA_TOPK - 1, 1), BLK, 2 * MOBA_HD), BF16)],
        compiler_params=_params(("arbitrary", "arbitrary"), VMEM_LIMIT),
        name="moba",
    )(proj3, proj3, proj3, bias)
    return out.reshape(B * S, MOBA_W)


def _pack_exact(lo, hi):
    lo_b = lax.bitcast_convert_type(lo, jnp.uint32)
    hi_b = lax.bitcast_convert_type(hi, jnp.uint32)
    return (lo_b >> 16) | (hi_b & jnp.uint32(0xFFFF0000))


def _unpack(w):
    lo = lax.bitcast_convert_type(w << 16, F32)
    hi = lax.bitcast_convert_type(w & jnp.uint32(0xFFFF0000), F32)
    return lo.astype(BF16), hi.astype(BF16)


def _merge_kernel(oa_ref, ob_ref, ga_ref, gb_ref, x_ref, wua_ref, wub_ref, wo_ref, gffn_ref, wr_ref, br_ref,
                  x1_ref, h2_ref, lg_ref):
    tm = x_ref.shape[0]
    u_a = jnp.dot(oa_ref[...], wua_ref[...], preferred_element_type=F32)
    u_b = jnp.dot(ob_ref[...], wub_ref[...], preferred_element_type=F32)
    y = _sigmoid(ga_ref[...].astype(F32)) * u_a + _sigmoid(gb_ref[...].astype(F32)) * u_b
    x1 = x_ref[...] + jnp.dot(y.astype(BF16), wo_ref[...], preferred_element_type=F32)
    x1_ref[...] = x1
    h2 = _rms(x1, gffn_ref[...])
    h_hi = h2.astype(BF16)
    h2_ref[...] = h_hi
    h_lo = (h2 - h_hi.astype(F32)).astype(BF16)
    r = jnp.dot(jnp.concatenate([h_hi, h_lo], axis=0), wr_ref[...], preferred_element_type=F32)
    lg_ref[...] = r[:tm, :LANES] + r[:tm, LANES:] + r[tm:, :LANES] + br_ref[...]


def _merge(o_a, o_b, proj, x2, w_ua, w_ub, w_o, g_ffn, w_r2, b_r, tm):
    T = x2.shape[0]
    full = lambda shape: pl.BlockSpec(shape, lambda i: (0, 0))
    rowblk = lambda w: pl.BlockSpec((tm, w), lambda i: (i, 0))
    return pl.pallas_call(
        _merge_kernel,
        grid=(T // tm,),
        in_specs=[rowblk(GLA_V), rowblk(MOBA_W),
                  pl.BlockSpec((tm, D_MODEL), lambda i: (i, OFF_GA // D_MODEL)),
                  pl.BlockSpec((tm, D_MODEL), lambda i: (i, OFF_GB // D_MODEL)),
                  rowblk(D_MODEL),
                  full((GLA_V, D_MODEL)), full((MOBA_W, D_MODEL)), full((D_MODEL, D_MODEL)),
                  full((1, D_MODEL)), full((D_MODEL, 2 * LANES)), full((1, LANES))],
        out_specs=[rowblk(D_MODEL), rowblk(D_MODEL), rowblk(LANES)],
        out_shape=[jax.ShapeDtypeStruct((T, D_MODEL), F32),
                   jax.ShapeDtypeStruct((T, D_MODEL), BF16),
                   jax.ShapeDtypeStruct((T, LANES), F32)],
        compiler_params=_params(("arbitrary",), VMEM_LIMIT),
        name="merge",
    )(o_a, o_b, proj, proj, x2, w_ua, w_ub, w_o, g_ffn, w_r2, b_r)


def _router_kernel(lg_ref, posw_ref, cnt_ref, carry_ref, cnt_scr, *, tm):
    rows = lg_ref.shape[0]

    @pl.when(pl.program_id(0) == 0)
    def _():
        cnt_scr[...] = jnp.zeros(cnt_scr.shape, F32)

    lane = lax.broadcasted_iota(jnp.int32, (rows, LANES), 1)
    lane_f = lane.astype(F32)
    work = jnp.where(lane < N_EXPERTS, lg_ref[...], NEG)
    vals, hots = [], []
    for _ in range(TOP_K):
        mx = work.max(axis=-1, keepdims=True)
        idx = jnp.min(jnp.where(work == mx, lane_f, float(LANES)), axis=-1, keepdims=True)
        hot = lane_f == idx
        vals.append(mx)
        hots.append(hot)
        work = jnp.where(hot, 2.0 * NEG, work)
    exps = [jnp.exp(v - vals[0]) for v in vals]
    den = exps[0] + exps[1] + exps[2] + exps[3]
    sel = jnp.zeros((rows, LANES), F32)
    for hot in hots:
        sel = sel + hot.astype(F32)
    row = lax.broadcasted_iota(jnp.int32, (tm, tm), 0)
    col = lax.broadcasted_iota(jnp.int32, (tm, tm), 1)
    below = (col < row).astype(BF16)
    er = lax.broadcasted_iota(jnp.int32, (LANES, LANES), 0)
    ec = lax.broadcasted_iota(jnp.int32, (LANES, LANES), 1)
    before = (er < ec).astype(F32)
    pos_parts = []
    for t in range(rows // tm):
        sel_t = sel[t * tm:(t + 1) * tm]
        local_rank = jnp.dot(below, sel_t.astype(BF16), preferred_element_type=F32)
        cnt_t = sel_t.sum(axis=0, keepdims=True)
        cnt_t = jnp.floor((cnt_t + (RUN_ALIGN - 1.0)) * (1.0 / RUN_ALIGN)) * RUN_ALIGN
        tile_off = jnp.dot(jnp.broadcast_to(cnt_t, (8, LANES)), before, preferred_element_type=F32,
                           precision=lax.Precision.HIGHEST)[0:1]
        pos_parts.append(local_rank + tile_off)
        carry_ref[t] = cnt_scr[...]
        cnt_ref[t] = cnt_t
        cnt_scr[...] = cnt_scr[...] + cnt_t
    pos_all = jnp.concatenate(pos_parts, axis=0)
    posw = jnp.zeros((rows, LANES), F32)
    for k in range(TOP_K):
        pk = jnp.sum(jnp.where(hots[k], pos_all, 0.0), axis=-1, keepdims=True)
        posw = jnp.where(lane == k, pk, posw)
        posw = jnp.where(lane == TOP_K + k, exps[k] / den, posw)
    posw_ref[...] = posw


def _router(logits, tm, tiles_per_step=4):
    T = logits.shape[0]
    nt = T // tm
    rows = tm * tiles_per_step
    tilerow = pl.BlockSpec((tiles_per_step, 1, LANES), lambda i: (i, 0, 0))
    return pl.pallas_call(
        functools.partial(_router_kernel, tm=tm),
        grid=(T // rows,),
        in_specs=[pl.BlockSpec((rows, LANES), lambda i: (i, 0))],
        out_specs=[pl.BlockSpec((rows, LANES), lambda i: (i, 0)), tilerow, tilerow],
        out_shape=[jax.ShapeDtypeStruct((T, LANES), F32),
                   jax.ShapeDtypeStruct((nt, 1, LANES), F32),
                   jax.ShapeDtypeStruct((nt, 1, LANES), F32)],
        scratch_shapes=[pltpu.VMEM((1, LANES), F32)],
        compiler_params=_params(("arbitrary",)),
        name="router",
    )(logits)


def _run_pieces(n, max_rows, fn):
    for b in reversed(range(RUN_ALIGN.bit_length() - 1, max_rows.bit_length())):
        size = 1 << b
        done = n & ~((2 << b) - 1)

        @pl.when((n & size) != 0)
        def _():
            fn(done, size)


def _aligned(i):
    return pl.multiple_of(i, RUN_ALIGN)


def _onehot_bands(pos, val):
    band = 256
    assert TILE_ROWS % band == 0
    n_k, n_tok = pos.shape
    pos_a = jnp.floor(pos * (1.0 / band))
    pos_b = pos - band * pos_a
    sub = lax.broadcasted_iota(jnp.int32, (band, n_tok), 0).astype(F32).astype(BF16)
    zero = jnp.zeros((band, n_tok), BF16)
    bands = []
    for a in range(TILE_ROWS // band):
        want = jnp.where(pos_a == a, pos_b, -1.0).astype(BF16)
        hit = zero
        for k in range(n_k):
            row = jnp.broadcast_to(want[k:k + 1, :], (band, n_tok))
            fill = jnp.ones((band, n_tok), BF16) if val is None else jnp.broadcast_to(val[k:k + 1, :], (band, n_tok))
            hit = hit + jnp.where(sub == row, fill, zero)
        bands.append(hit)
    return jnp.concatenate(bands, axis=0)


def _dispatch_kernel(toff_ref, eoff_ref, n_ref, trows_ref, zoff_ref, zn_ref, tail_ref, h2_ref, posw_ref, x_ref,
                     buf, zbuf, sems, *, nt):
    tm = h2_ref.shape[0]
    half = D_MODEL // 2
    j = pl.program_id(0)
    zrows = zbuf.shape[0]

    def tile_runs(t, act):
        slot = t % 2

        def body(e, c):
            r = t * N_EXPERTS + e
            t0, d0 = toff_ref[r], eoff_ref[r]
            _run_pieces(n_ref[r], tm, lambda done, size: act(pltpu.make_async_copy(
                buf.at[slot, pl.ds(_aligned(t0 + done), size)], x_ref.at[pl.ds(_aligned(d0 + done), size)],
                sems.at[slot])))
            return c

        lax.fori_loop(0, N_EXPERTS, body, 0, unroll=RUN_UNROLL)

    def zero_fill(act):
        def body(e, c):
            d0 = zoff_ref[e]
            _run_pieces(zn_ref[e], zrows, lambda done, size: act(pltpu.make_async_copy(
                zbuf.at[pl.ds(0, size)], x_ref.at[pl.ds(_aligned(d0 + done), size)], sems.at[2])))
            return c

        lax.fori_loop(0, N_EXPERTS, body, 0)

        def tail(i, c):
            act(pltpu.make_async_copy(zbuf, x_ref.at[pl.ds(pl.multiple_of(i * zrows, zrows), zrows)], sems.at[2]))
            return c

        lax.fori_loop(tail_ref[0], x_ref.shape[0] // zrows, tail, 0)

    start = lambda cp: cp.start()
    wait = lambda cp: cp.wait()

    def wait_tile(t):
        slot = t % 2
        _run_pieces(trows_ref[t], TILE_ROWS, lambda done, size: pltpu.make_async_copy(
            buf.at[slot, pl.ds(0, size)], x_ref.at[pl.ds(0, size)], sems.at[slot]).wait())

    @pl.when(j == 0)
    def _():
        zbuf[...] = jnp.zeros(zbuf.shape, zbuf.dtype)
        zero_fill(start)

    @pl.when(j >= 2)
    def _():
        wait_tile(j - 2)

    perm = _onehot_bands(posw_ref[...].T[:TOP_K], None)
    xs = jnp.dot(perm, h2_ref[...], preferred_element_type=F32)
    buf[j % 2] = _pack_exact(xs[:, :half], xs[:, half:])
    tile_runs(j, start)

    @pl.when(j == nt - 1)
    def _():
        if nt >= 2:
            wait_tile(j - 1)
        wait_tile(j)
        zero_fill(wait)


def _dispatch(h2, posw, n_rows, tile_off, expert_off, run_n, tile_rows, zoff, zn, tail, tm):
    T = h2.shape[0]
    nt = T // tm
    assert n_rows % EXPERT_BLOCK == 0
    return pl.pallas_call(
        functools.partial(_dispatch_kernel, nt=nt),
        grid_spec=pltpu.PrefetchScalarGridSpec(
            num_scalar_prefetch=7,
            grid=(nt,),
            in_specs=[pl.BlockSpec((tm, D_MODEL), lambda i, *_: (i, 0)),
                      pl.BlockSpec((tm, LANES), lambda i, *_: (i, 0))],
            out_specs=pl.BlockSpec(memory_space=pl.ANY),
            scratch_shapes=[pltpu.VMEM((2, TILE_ROWS, D_MODEL // 2), jnp.uint32),
                            pltpu.VMEM((EXPERT_BLOCK, D_MODEL // 2), jnp.uint32),
                            pltpu.SemaphoreType.DMA((3,))]),
        out_shape=jax.ShapeDtypeStruct((n_rows, D_MODEL // 2), jnp.uint32),
        compiler_params=_params(("arbitrary",), VMEM_LIMIT),
        name="dispatch",
    )(tile_off, expert_off, run_n, tile_rows, zoff, zn, tail, h2, posw)


def _expert_kernel(be_ref, rows_ref, slot_ref, next_ref, x_ref, wg_ref, bg_ref, wu_ref, bu_ref, wd_ref, bd_ref, y_ref,
                   w_in, wg_s, wu_s, wd_s, sems):
    i = pl.program_id(0)
    half = D_MODEL // 2
    M = x_ref.shape[0]
    rows = rows_ref[i]
    e = be_ref[i]
    prev = be_ref[jnp.maximum(i - 1, 0)]

    def weight_copies(expert, slot):
        return [pltpu.make_async_copy(w_hbm.at[expert], w_in.at[slot, k], sems.at[slot, k])
                for k, w_hbm in enumerate((wg_ref, wu_ref, wd_ref))]

    @pl.when((rows > 0) & ((i == 0) | (e != prev)))
    def _():
        slot = slot_ref[i]

        @pl.when(i == 0)
        def _():
            for cp in weight_copies(e, slot):
                cp.start()

        nxt = next_ref[i]

        @pl.when(nxt >= 0)
        def _():
            for cp in weight_copies(nxt, 1 - slot):
                cp.start()

        for cp in weight_copies(e, slot):
            cp.wait()
        for k, dst in enumerate((wg_s, wu_s, wd_s)):
            dst[...] = w_in[slot, k].astype(BF16)

    def compute(r):
        x_lo, x_hi = _unpack(x_ref[:r, :])

        def proj_in(w_s, b_ref):
            return (jnp.dot(x_lo, w_s[:half, :], preferred_element_type=F32)
                    + jnp.dot(x_hi, w_s[half:, :], preferred_element_type=F32) + b_ref[0])

        gate = jnp.minimum(proj_in(wg_s, bg_ref), SWIGLU_LIMIT)
        up = jnp.clip(proj_in(wu_s, bu_ref), -SWIGLU_LIMIT, SWIGLU_LIMIT)
        glu = gate * _sigmoid(gate * SWIGLU_ALPHA)
        act = ((up + 1.0) * glu).astype(BF16)
        y = jnp.dot(act, wd_s[...], preferred_element_type=F32) + bd_ref[0]
        y_ref[:r, :] = y
        if r < M:
            y_ref[r:, :] = jnp.zeros((M - r, D_MODEL), y_ref.dtype)

    for r in range(EXPERT_SUB, M + 1, EXPERT_SUB):
        pl.when(rows == r)(functools.partial(compute, r))

    @pl.when(rows == 0)
    def _():
        y_ref[...] = jnp.zeros(y_ref.shape, y_ref.dtype)


def _experts(blk_exp, blk_rows, blk_slot, blk_next, x_rows, n_pad, w_g, b_g, w_u, b_u, w_d, b_d):
    M = EXPERT_BLOCK
    assert D_FF == D_MODEL
    bspec = lambda n: pl.BlockSpec((1, 1, n), lambda i, be, *_: (be[i], 0, 0))
    wspec = pl.BlockSpec(memory_space=pl.ANY)
    return pl.pallas_call(
        _expert_kernel,
        grid_spec=pltpu.PrefetchScalarGridSpec(
            num_scalar_prefetch=4,
            grid=(n_pad // M,),
            in_specs=[pl.BlockSpec((M, D_MODEL // 2), lambda i, *_: (i, 0)),
                      wspec, bspec(D_FF), wspec, bspec(D_FF), wspec, bspec(D_MODEL)],
            out_specs=pl.BlockSpec((M, D_MODEL), lambda i, *_: (i, 0)),
            scratch_shapes=[pltpu.VMEM((2, 3, D_MODEL, D_FF), F32),
                            pltpu.VMEM((D_MODEL, D_FF), BF16),
                            pltpu.VMEM((D_MODEL, D_FF), BF16),
                            pltpu.VMEM((D_FF, D_MODEL), BF16),
                            pltpu.SemaphoreType.DMA((2, 3))]),
        out_shape=jax.ShapeDtypeStruct((n_pad, D_MODEL), F32),
        compiler_params=_params(("arbitrary",), VMEM_LIMIT),
        name="experts",
    )(blk_exp, blk_rows, blk_slot, blk_next, x_rows, w_g, b_g, w_u, b_u, w_d, b_d)


def _final_kernel(toff_ref, eoff_ref, n_ref, trows_ref, x1_ref, posw_ref, p_ref, gpg_ref, wpg_ref, wpp_ref, gpp_ref,
                  gfin_ref, y_ref, o_ref, buf, xmid, sems, *, nt):
    tm = x1_ref.shape[0]
    step = pl.program_id(0)
    j = jnp.minimum(step, nt - 1)

    def tile_runs(t, act):
        slot = t % 2

        def body(e, c):
            r = t * N_EXPERTS + e
            t0, s0 = toff_ref[r], eoff_ref[r]
            _run_pieces(n_ref[r], tm, lambda done, size: act(pltpu.make_async_copy(
                y_ref.at[pl.ds(_aligned(s0 + done), size)], buf.at[slot, pl.ds(_aligned(t0 + done), size)],
                sems.at[slot])))
            return c

        lax.fori_loop(0, N_EXPERTS, body, 0, unroll=RUN_UNROLL)

    @pl.when(step == 0)
    def _():
        buf[...] = jnp.zeros(buf.shape, buf.dtype)
        xmid[...] = jnp.zeros(xmid.shape, xmid.dtype)
        tile_runs(0, lambda cp: cp.start())

    @pl.when(step + 1 < nt)
    def _():
        tile_runs(step + 1, lambda cp: cp.start())

    @pl.when(step < nt)
    def _():
        _run_pieces(trows_ref[j], TILE_ROWS, lambda done, size: pltpu.make_async_copy(
            y_ref.at[pl.ds(0, size)], buf.at[j % 2, pl.ds(0, size)], sems.at[j % 2]).wait())

    x = xmid[...]
    pp = jnp.dot(p_ref[...].astype(BF16), wpp_ref[...], preferred_element_type=F32)
    pg = _sigmoid(jnp.dot(_rms(x, gpg_ref[...]).astype(BF16), wpg_ref[...], preferred_element_type=F32))
    x = x + pg * _rms(pp, gpp_ref[...])
    o_ref[...] = _rms(x, gfin_ref[...])

    posw_t = posw_ref[...].T
    comb_t = _onehot_bands(posw_t[:TOP_K], posw_t[TOP_K:2 * TOP_K].astype(BF16))
    sure = TOP_K * tm
    tail = buf[j % 2, sure:, :]
    live = lax.broadcasted_iota(jnp.int32, tail.shape, 0) < trows_ref[j] - sure
    y = jnp.concatenate([buf[j % 2, :sure, :], jnp.where(live, tail, 0.0)], axis=0).astype(BF16)
    xmid[...] = x1_ref[...] + lax.dot_general(comb_t, y, TN, preferred_element_type=F32)


def _final(tile_off, expert_off, run_n, tile_rows, x1, y_rows, posw, p2, g_pg, w_pg, w_pp, g_pp, g_fin, tm):
    T = x1.shape[0]
    nt = T // tm
    full = lambda shape: pl.BlockSpec(shape, lambda i, *_: (0, 0))
    this = lambda w: pl.BlockSpec((tm, w), lambda i, *_: (jnp.minimum(i, nt - 1), 0))
    prev = lambda w: pl.BlockSpec((tm, w), lambda i, *_: (jnp.maximum(i - 1, 0), 0))
    return pl.pallas_call(
        functools.partial(_final_kernel, nt=nt),
        grid_spec=pltpu.PrefetchScalarGridSpec(
            num_scalar_prefetch=4,
            grid=(nt + 1,),
            in_specs=[this(D_MODEL), this(LANES), prev(PLE_DIM),
                      full((1, D_MODEL)), full((D_MODEL, D_MODEL)), full((PLE_DIM, D_MODEL)),
                      full((1, D_MODEL)), full((1, D_MODEL)),
                      pl.BlockSpec(memory_space=pl.ANY)],
            out_specs=prev(D_MODEL),
            scratch_shapes=[pltpu.VMEM((2, TILE_ROWS, D_MODEL), F32),
                            pltpu.VMEM((tm, D_MODEL), F32),
                            pltpu.SemaphoreType.DMA((2,))]),
        out_shape=jax.ShapeDtypeStruct((T, D_MODEL), F32),
        compiler_params=_params(("arbitrary",), VMEM_LIMIT),
        name="final",
    )(tile_off, expert_off, run_n, tile_rows, x1, posw, p2, g_pg, w_pg, w_pp, g_pp, g_fin, y_rows)


def _split_w_in(w_in):
    sizes = (GLA_QK, GLA_QK, GLA_V, GLA_V, GLA_RANK, MOBA_W, MOBA_W, MOBA_W, D_MODEL, D_MODEL)
    offs = [0]
    for s in sizes:
        offs.append(offs[-1] + s)
    main = jnp.concatenate([w_in[:, :offs[4]], w_in[:, offs[5]:]], axis=1).astype(BF16)
    alow = jnp.pad(w_in[:, offs[4]:offs[5]], ((0, 0), (0, LANES - GLA_RANK))).astype(BF16)
    return main, alow


def _layer(x2, p2, bias, B, S, g_mix, w_in, w_a2, b_a2, g_gla_out, w_up_gla, w_up_moba, w_o, g_ffn, w_router,
           b_router, w_e_gate, b_e_gate, w_e_up, b_e_up, w_e_down, b_e_down, g_ple_gate, w_ple_gate, w_ple_proj,
           g_ple_proj, g_final):
    T = B * S
    row = lambda v: v.reshape(1, -1).astype(F32)
    w_main, w_alow = _split_w_in(w_in)
    colscale = jnp.ones((D_PROJ,), F32)
    colscale = colscale.at[OFF_QA:OFF_QA + GLA_QK].set(GLA_DK ** -0.5)
    colscale = colscale.at[OFF_QB:OFF_QB + MOBA_W].set(MOBA_HD ** -0.5 * LOG2E)
    w_a2p = jnp.pad(w_a2, ((0, LANES - GLA_RANK), (0, 0))).astype(BF16)
    proj, glog = _inproj(x2, row(g_mix), w_main, colscale.reshape(1, -1), w_alow, w_a2p, row(b_a2))

    o_a = _gla(proj, glog, row(g_gla_out), B, S)
    o_b = _moba(proj, bias, B, S)

    w_r = jnp.pad(w_router.astype(F32), ((0, 0), (0, LANES - N_EXPERTS)))
    w_r_hi = w_r.astype(BF16)
    w_r2 = jnp.concatenate([w_r_hi, (w_r - w_r_hi.astype(F32)).astype(BF16)], axis=1)
    b_r = jnp.pad(b_router.astype(F32), (0, LANES - N_EXPERTS)).reshape(1, -1)
    tm = TOKEN_TILE
    nt = T // tm
    x1, h2, logits = _merge(o_a, o_b, proj, x2, w_up_gla.astype(BF16), w_up_moba.astype(BF16),
                            w_o.astype(BF16), row(g_ffn), w_r2, b_r, MERGE_TILE)
    posw, cnt_t, carry = _router(logits, tm)

    M = EXPERT_BLOCK
    A = nt * TILE_ROWS
    n_pad = (-(-A // M)) * M + N_EXPERTS * M
    n_blk = n_pad // M
    cnt_t = cnt_t[:, 0, :N_EXPERTS].astype(jnp.int32)
    carry = carry[:, 0, :N_EXPERTS].astype(jnp.int32)
    counts = carry[-1] + cnt_t[-1]
    padded = (counts + M - 1) // M * M
    pad_end = jnp.cumsum(padded)
    pad_start = pad_end - padded
    blk_exp = jnp.minimum(jnp.sum(pad_end[None, :] <= (jnp.arange(n_blk, dtype=jnp.int32) * M)[:, None], axis=1),
                          N_EXPERTS - 1).astype(jnp.int32)
    n_used = (pad_end[-1:] // M).astype(jnp.int32)
    blk_start = jnp.arange(n_blk, dtype=jnp.int32) * M
    eids = jnp.arange(N_EXPERTS, dtype=jnp.int32)

    def per_block(per_expert):
        return jnp.sum(jnp.where(blk_exp[:, None] == eids[None, :], per_expert[None, :], 0), axis=1).astype(jnp.int32)

    blk_rows = jnp.clip(per_block(pad_start + counts) - blk_start, 0, M)
    blk_rows = jnp.where(blk_start < pad_end[-1], (blk_rows + EXPERT_SUB - 1) // EXPERT_SUB * EXPERT_SUB, 0)
    has_rows = counts > 0
    blk_slot = per_block((jnp.cumsum(has_rows) - 1) % 2)
    later = jnp.where((eids[None, :] > eids[:, None]) & has_rows[None, :], eids[None, :], N_EXPERTS)
    next_exp = jnp.min(later, axis=1)
    blk_next = per_block(jnp.where(next_exp < N_EXPERTS, next_exp, -1))
    tile_off = (jnp.cumsum(cnt_t, axis=1) - cnt_t).reshape(-1)
    expert_off = (carry + pad_start[None, :]).reshape(-1)
    run_n = cnt_t.reshape(-1)
    tile_rows = jnp.sum(cnt_t, axis=1)
    x_rows = _dispatch(h2, posw, n_pad, tile_off, expert_off, run_n, tile_rows, pad_start + counts, padded - counts,
                       n_used, tm)
    y_rows = _experts(blk_exp, blk_rows.astype(jnp.int32), blk_slot, blk_next, x_rows, n_pad,
                      w_e_gate, b_e_gate.reshape(N_EXPERTS, 1, D_FF),
                      w_e_up, b_e_up.reshape(N_EXPERTS, 1, D_FF), w_e_down,
                      b_e_down.reshape(N_EXPERTS, 1, D_MODEL))
    return _final(tile_off, expert_off, run_n, tile_rows, x1, y_rows, posw, p2, row(g_ple_gate),
                  w_ple_gate.astype(BF16), w_ple_proj.astype(BF16), row(g_ple_proj), row(g_final), tm)


def kernel(x, p, rel_bias, g_mix, w_in, w_a2, b_a2, g_gla_out, w_up_gla, w_up_moba, w_o, g_ffn, w_router, b_router,
           w_e_gate, b_e_gate, w_e_up, b_e_up, w_e_down, b_e_down, g_ple_gate, w_ple_gate, w_ple_proj, g_ple_proj,
           g_final):
    B, S, D = x.shape
    assert D == D_MODEL and S % MOBA_BLOCK == 0 and S % GLA_CHUNK == 0 and p.shape[0] == 1
    bias = _bias_tiles(rel_bias, S // MOBA_BLOCK)
    out = _layer(x.reshape(B * S, D), p[0].reshape(B * S, PLE_DIM), bias, B, S,
                 g_mix[0], w_in[0], w_a2[0], b_a2[0], g_gla_out[0], w_up_gla[0], w_up_moba[0], w_o[0], g_ffn[0],
                 w_router[0], b_router[0], w_e_gate[0], b_e_gate[0], w_e_up[0], b_e_up[0], w_e_down[0],
                 b_e_down[0], g_ple_gate[0], w_ple_gate[0], w_ple_proj[0], g_ple_proj[0], g_final)
    return out.reshape(B, S, D)
```

```python
import functools
import math

import jax
import jax.numpy as jnp
from jax import lax
from jax.experimental import pallas as pl
from jax.experimental.pallas import tpu as pltpu

F32 = jnp.float32
BF16 = jnp.bfloat16

D_MODEL = 1024
PLE_DIM = 256
GLA_HEADS = 4
GLA_DK = 128
GLA_DV = 256
GLA_RANK = 16
GLA_TAU = 16.0
GLA_QK = GLA_HEADS * GLA_DK
GLA_V = GLA_HEADS * GLA_DV
MOBA_HEADS = 8
MOBA_HD = 128
MOBA_BLOCK = 256
MOBA_TOPK = 3
MOBA_W = MOBA_HEADS * MOBA_HD
REL_BUCKETS = 32
REL_MAX_DIST = 4096
N_EXPERTS = 32
TOP_K = 4
D_FF = 1024
SWIGLU_LIMIT = 7.0
SWIGLU_ALPHA = 1.702
EPS = 1e-6

LANES = 128
NEG = -1e30
LOG2E = math.log2(math.e)
VMEM_LIMIT = 56 * 1024 * 1024

OFF_QA, OFF_KA, OFF_VA, OFF_RA = 0, 512, 1024, 2048
OFF_QB, OFF_KB, OFF_VB, OFF_GA, OFF_GB = 3072, 4096, 5120, 6144, 7168
D_PROJ = 8192

GLA_CHUNK = 128
EXPERT_BLOCK = 512
EXPERT_SUB = 128
TOKEN_TILE = 256
RUN_ALIGN = 8
TILE_ROWS = TOP_K * TOKEN_TILE + N_EXPERTS * RUN_ALIGN
RUN_UNROLL = 8
MERGE_TILE = 512

NT = (((1,), (1,)), ((), ()))
TN = (((0,), (0,)), ((), ()))


def _params(sem, vmem=None):
    return pltpu.CompilerParams(dimension_semantics=sem, vmem_limit_bytes=vmem)


def _rms(x, g):
    return x * lax.rsqrt(jnp.mean(x * x, axis=-1, keepdims=True) + EPS) * g


def _sigmoid(x):
    return 1.0 / (1.0 + jnp.exp(-x))


def _bucket_of(n):
    max_exact = REL_BUCKETS // 2
    if n < max_exact:
        return n
    return min(max_exact + int(math.log(n / max_exact) / math.log(REL_MAX_DIST / max_exact)
                               * (REL_BUCKETS - max_exact)), REL_BUCKETS - 1)


def _bias_kernel(tab_ref, bkt_ref, o_ref, *, nblk):
    strip = 8
    m = pl.program_id(0)

    def fill(lo, hi):
        def body(s, carry):
            r0 = pl.multiple_of(s * strip, strip)
            b = bkt_ref[0, pl.ds(r0, strip), :]
            accs = [jnp.zeros(b.shape, F32) for _ in range(MOBA_HEADS)]
            for bb in range(lo, hi + 1):
                hit = b == bb
                for h in range(MOBA_HEADS):
                    accs[h] = jnp.where(hit, tab_ref[h, bb], accs[h])
            for h in range(MOBA_HEADS):
                o_ref[h, 0, pl.ds(r0, strip), :] = jnp.where(b < 0, NEG, accs[h])
            return carry

        lax.fori_loop(0, MOBA_BLOCK // strip, body, 0)

    for mm in range(nblk):
        lo = max(_bucket_of(max(mm * MOBA_BLOCK - (MOBA_BLOCK - 1), 0)) - 1, 0)
        hi = min(_bucket_of(mm * MOBA_BLOCK + MOBA_BLOCK - 1) + 1, REL_BUCKETS - 1)
        pl.when(m == mm)(functools.partial(fill, lo, hi))


def _bias_tiles(rel_bias, nblk):
    i = jnp.arange(MOBA_BLOCK, dtype=jnp.int32)
    dist = (jnp.arange(nblk, dtype=jnp.int32)[:, None, None] * MOBA_BLOCK + i[None, :, None] - i[None, None, :])
    n = jnp.maximum(dist, 0)
    max_exact = REL_BUCKETS // 2
    nf = jnp.maximum(n, 1).astype(F32)
    large = max_exact + (jnp.log(nf / max_exact) / math.log(REL_MAX_DIST / max_exact)
                         * (REL_BUCKETS - max_exact)).astype(jnp.int32)
    large = jnp.minimum(large, REL_BUCKETS - 1)
    bkt = jnp.where(dist < 0, -1, jnp.where(n < max_exact, n, large)).astype(jnp.int32)
    tab = rel_bias.astype(F32).T * LOG2E
    return pl.pallas_call(
        functools.partial(_bias_kernel, nblk=nblk),
        grid=(nblk,),
        in_specs=[pl.BlockSpec(memory_space=pltpu.SMEM),
                  pl.BlockSpec((1, MOBA_BLOCK, MOBA_BLOCK), lambda m: (m, 0, 0))],
        out_specs=pl.BlockSpec((MOBA_HEADS, 1, MOBA_BLOCK, MOBA_BLOCK), lambda m: (0, m, 0, 0)),
        out_shape=jax.ShapeDtypeStruct((MOBA_HEADS, nblk, MOBA_BLOCK, MOBA_BLOCK), F32),
        compiler_params=_params(("arbitrary",)),
        name="bias_tiles",
    )(tab, bkt)


def _inproj_kernel(x0_ref, xn_ref, g_ref, w_ref, cs_ref, wal_ref, wa2_ref, ba2_ref, o_ref, glog_ref, h_scr, al_scr,
                   *, n_j):
    i, j = pl.program_id(0), pl.program_id(1)
    tm = xn_ref.shape[0]
    rs = tm // n_j

    def gate_logits(a_low):
        a = jnp.dot(a_low, wa2_ref[...], preferred_element_type=F32) + ba2_ref[...]
        log_sig = jnp.minimum(a, 0.0) - jnp.log1p(jnp.exp(-jnp.abs(a)))
        return log_sig * (1.0 / GLA_TAU)

    def slice_rows(behind):
        return pl.multiple_of(((j + n_j - behind) % n_j) * rs, rs)

    @pl.when((i == 0) & (j == 0))
    def _():
        h0 = _rms(x0_ref[...], g_ref[...]).astype(BF16)
        h_scr[0] = h0
        a_low0 = jnp.dot(h0, wal_ref[...], preferred_element_type=F32).astype(BF16)
        glog_ref[...] = gate_logits(a_low0)
        al_scr[...] = a_low0[(n_j - 2) * rs:(n_j - 1) * rs, :]

    glog_ref[pl.ds(slice_rows(2), rs), :] = gate_logits(al_scr[...])
    slot = jnp.where(j == 0, i, i + 1) % 2
    h_lag = h_scr[slot, pl.ds(slice_rows(1), rs), :]
    al_scr[...] = jnp.dot(h_lag, wal_ref[...], preferred_element_type=F32).astype(BF16)

    acc = jnp.dot(h_scr[i % 2], w_ref[...], preferred_element_type=F32)
    o_ref[...] = (acc * cs_ref[...]).astype(BF16)

    h_scr[(i + 1) % 2, pl.ds(slice_rows(0), rs), :] = _rms(xn_ref[pl.ds(slice_rows(0), rs), :],
                                                             g_ref[...]).astype(BF16)


def _inproj(x2, g_mix, w_main, colscale, w_alow, w_a2p, b_a2, tm=1024, tn=2048):
    T = x2.shape[0]
    n_i, n_j = T // tm, D_PROJ // tn
    assert n_i >= 2 and tm % (8 * n_j) == 0
    return pl.pallas_call(
        functools.partial(_inproj_kernel, n_j=n_j),
        grid=(n_i, n_j),
        in_specs=[pl.BlockSpec((tm, D_MODEL), lambda i, j: (0, 0)),
                  pl.BlockSpec((tm, D_MODEL), lambda i, j: (jnp.minimum(i + 1, n_i - 1), 0)),
                  pl.BlockSpec((1, D_MODEL), lambda i, j: (0, 0)),
                  pl.BlockSpec((D_MODEL, tn), lambda i, j: (0, j)),
                  pl.BlockSpec((1, tn), lambda i, j: (0, j)),
                  pl.BlockSpec((D_MODEL, LANES), lambda i, j: (0, 0)),
                  pl.BlockSpec((LANES, GLA_QK), lambda i, j: (0, 0)),
                  pl.BlockSpec((1, GLA_QK), lambda i, j: (0, 0))],
        out_specs=[pl.BlockSpec((tm, tn), lambda i, j: (i, j)),
                   pl.BlockSpec((tm, GLA_QK), lambda i, j: (jnp.minimum(i + jnp.minimum(j // 2, 1), n_i - 1), 0))],
        out_shape=[jax.ShapeDtypeStruct((T, D_PROJ), BF16),
                   jax.ShapeDtypeStruct((T, GLA_QK), F32)],
        scratch_shapes=[pltpu.VMEM((2, tm, D_MODEL), BF16), pltpu.VMEM((tm // n_j, LANES), BF16)],
        compiler_params=_params(("arbitrary", "arbitrary"), VMEM_LIMIT),
        name="inproj",
    )(x2, x2, g_mix, w_main, colscale, w_alow, w_a2p, b_a2)


def _gla_kernel(q_ref, k_ref, v_ref, r_ref, g_ref, gout_ref, o_ref, st_ref):
    C = GLA_CHUNK

    @pl.when(pl.program_id(1) == 0)
    def _():
        st_ref[...] = jnp.zeros(st_ref.shape, F32)

    nb = q_ref.shape[0]
    row = lax.broadcasted_iota(jnp.int32, (C, C), 0)
    col = lax.broadcasted_iota(jnp.int32, (C, C), 1)
    causal = col <= row
    ltri = causal.astype(BF16)
    mid = C // 2
    pairs = [(b, h) for b in range(nb) for h in range(GLA_HEADS)]
    ks = lambda h: slice(h * GLA_DK, (h + 1) * GLA_DK)
    vs = lambda h: slice(h * GLA_DV, (h + 1) * GLA_DV)
    gate = {}
    for b, h in pairs:
        r = r_ref[b, :, vs(h)].astype(F32)
        gate[b, h] = r * _sigmoid(r)
    G = []
    for b in range(nb):
        g = g_ref[b]
        g_hi = g.astype(BF16)
        g_lo = (g - g_hi.astype(F32)).astype(BF16)
        G.append(jnp.dot(ltri, g_hi, preferred_element_type=F32) + jnp.dot(ltri, g_lo, preferred_element_type=F32))
    Gh = {(b, h): G[b][:, ks(h)] for b, h in pairs}
    qh = {(b, h): q_ref[b, :, ks(h)].astype(F32) for b, h in pairs}
    kh = {(b, h): k_ref[b, :, ks(h)].astype(F32) for b, h in pairs}
    g_mid = {p: Gh[p][mid:mid + 1, :] for p in pairs}
    g_last = {p: Gh[p][C - 1:C, :] for p in pairs}
    A = {p: lax.dot_general((qh[p] * jnp.exp(Gh[p] - g_mid[p])).astype(BF16),
                            (kh[p] * jnp.exp(g_mid[p] - Gh[p])).astype(BF16), NT, preferred_element_type=F32)
         for p in pairs}
    st = {p: st_ref[p[0], p[1]] for p in pairs}
    inter = {p: lax.dot_general((qh[p] * jnp.exp(Gh[p])).astype(BF16), st[p].astype(BF16), NT,
                                preferred_element_type=F32) for p in pairs}
    for b, h in pairs:
        p = (b, h)
        k_d = (kh[p] * jnp.exp(g_last[p] - Gh[p])).astype(BF16)
        st_ref[b, h] = jnp.exp(g_last[p]) * st[p] + lax.dot_general(v_ref[b, :, vs(h)], k_d, TN,
                                                                     preferred_element_type=F32)
    intra = {(b, h): jnp.dot(jnp.where(causal, A[b, h], 0.0).astype(BF16), v_ref[b, :, vs(h)],
                             preferred_element_type=F32) for b, h in pairs}
    for b, h in pairs:
        o = inter[b, h] + intra[b, h]
        o_ref[b, :, vs(h)] = (_rms(o, gout_ref[...]) * gate[b, h]).astype(BF16)


def _gla(proj, glog, g_gla_out, B, S, nb=4):
    C = GLA_CHUNK
    assert B % nb == 0
    proj3 = proj.reshape(B, S, D_PROJ)
    glog3 = glog.reshape(B, S, GLA_QK)
    spec = lambda w, off: pl.BlockSpec((nb, C, w), lambda b, c: (b, c, off // w))
    out = pl.pallas_call(
        _gla_kernel,
        grid=(B // nb, S // C),
        in_specs=[spec(GLA_QK, OFF_QA), spec(GLA_QK, OFF_KA), spec(GLA_V, OFF_VA), spec(GLA_V, OFF_RA),
                  spec(GLA_QK, 0), pl.BlockSpec((1, GLA_DV), lambda b, c: (0, 0))],
        out_specs=spec(GLA_V, 0),
        out_shape=jax.ShapeDtypeStruct((B, S, GLA_V), BF16),
        scratch_shapes=[pltpu.VMEM((nb, GLA_HEADS, GLA_DV, GLA_DK), F32)],
        compiler_params=_params(("arbitrary", "arbitrary")),
        name="gla",
    )(proj3, proj3, proj3, proj3, glog3, g_gla_out)
    return out.reshape(B * S, GLA_V)


def _moba_kernel(q_ref, k_ref, v_ref, bias_ref, o_ref, ka_scr, va_scr, lg_scr, mx_scr, qa_scr, *, nblk):
    BLK, HD = MOBA_BLOCK, MOBA_HD
    S = nblk * BLK
    units = range(q_ref.shape[0])

    @pl.when((pl.program_id(0) == 0) & (pl.program_id(1) == 0))
    def _():
        blk = lax.broadcasted_iota(jnp.int32, (S, HD), 0) // BLK
        lane = lax.broadcasted_iota(jnp.int32, (S, HD), 1)
        for u in units:
            ka_scr[u, :, HD:] = (lane == blk).astype(BF16)
            va_scr[u, :, HD:] = (lane == 0).astype(BF16)

    n_plain = min(MOBA_TOPK + 1, nblk)
    n_late = nblk - n_plain
    n_tiles = nblk * (nblk + 1) // 2

    def tile_id(cc, j):
        return cc * (cc + 1) // 2 + j

    def rows(cc):
        return slice(cc * BLK, (cc + 1) * BLK)

    def stage1(u, cc, q_in, keys):
        mx = None
        for j in range(cc + 1):
            lg = lax.dot_general(q_in, keys(j), NT, preferred_element_type=F32) + bias_ref[0, cc - j]
            lg_scr[u * n_tiles + tile_id(cc, j)] = lg
            t = jnp.maximum(lg[:, :LANES], lg[:, LANES:])
            mx = t if mx is None else jnp.maximum(mx, t)
            if j == cc:
                mx_scr[u * nblk + cc] = mx
            yield

    def stage2(u, cc):
        m = mx_scr[u * nblk + cc].max(axis=-1, keepdims=True)
        acc = jnp.zeros((BLK, 2 * HD), F32)
        for j in range(cc + 1):
            p = jnp.exp2(lg_scr[u * n_tiles + tile_id(cc, j)] - m).astype(BF16)
            acc = acc + jnp.dot(p, va_scr[u, rows(j), :], preferred_element_type=F32)
            if j == cc:
                o_ref[u, rows(cc), :] = (acc[:, :HD] / acc[:, HD:HD + 1]).astype(BF16)
            yield

    def drain(gen):
        for _ in gen:
            pass

    def interleave(main, side, n_main, n_side):
        side_steps = (s for g in side for s in g)
        done = 0
        for i, _ in enumerate(s for g in main for s in g):
            assert n_main > 0
            want = (i + 1) * n_side // n_main
            while done < want and next(side_steps, "end") != "end":
                done += 1
        drain(side_steps)

    plain = {u: [stage1(u, cc, q_ref[u, rows(cc), :], lambda j, u=u: k_ref[u, rows(j), :]) for cc in range(n_plain)]
             for u in units}
    if n_late:
        pens = []
        for u in units:
            next(plain[u][0])
            ksum = [k_ref[u, rows(j), :].astype(F32).reshape(BLK // 8, 8, HD).sum(axis=0).sum(axis=0, keepdims=True)
                    for j in range(nblk)]
            kmean = jnp.concatenate(ksum, axis=0) * (1.0 / BLK)
            km_hi = kmean.astype(BF16)
            km_lo = (kmean - km_hi.astype(F32)).astype(BF16)
            km2 = jnp.concatenate([km_hi, km_lo], axis=0)
            for cc in range(n_plain, nblk):
                q = q_ref[u, rows(cc), :]
                s2 = lax.dot_general(km2, q, NT, preferred_element_type=F32)
                pens.append((u, cc, q, s2[:nblk] + s2[nblk:]))
        for u in units:
            for g in plain[u][:2]:
                drain(g)
        for u, cc, q, s in pens:
            ji = lax.broadcasted_iota(jnp.int32, s.shape, 0)
            cnt = jnp.zeros(s.shape, F32)
            for jp in range(cc):
                sj = s[jp:jp + 1, :]
                beats = (sj > s) | ((sj == s) & (jp < ji))
                cnt = cnt + beats.astype(F32)
            pen = jnp.where((ji < cc) & (cnt >= MOBA_TOPK), NEG, 0.0)
            pen_t = jnp.concatenate([pen, jnp.zeros((HD - nblk, BLK), F32)], axis=0).T
            qa_scr[u * n_late + cc - n_plain] = jnp.concatenate([q, pen_t.astype(BF16)], axis=1)
    for u in units:
        ka_scr[u, :, :HD] = k_ref[u]
        va_scr[u, :, :HD] = v_ref[u]
    for u in units:
        for g in plain[u]:
            drain(g)

    @pl.when(pl.program_id(0) >= 0)
    def _():
        late = [stage1(u, cc, qa_scr[u * n_late + cc - n_plain], lambda j, u=u: ka_scr[u, rows(j), :])
                for u in units for cc in range(n_plain, nblk)]
        early = [stage2(u, cc) for u in units for cc in range(n_plain)]
        if late:
            interleave(late, early, len(units) * (n_tiles - tile_id(n_plain, 0)), len(units) * tile_id(n_plain, 0))
        else:
            for g in early:
                drain(g)

    @pl.when(pl.program_id(0) >= 0)
    def _():
        for u in units:
            for cc in range(n_plain, nblk):
                drain(stage2(u, cc))


def _moba(proj, bias, B, S, nb=2):
    BLK = MOBA_BLOCK
    nblk = S // BLK
    H = MOBA_HEADS
    assert nblk <= MOBA_HD and B % nb == 0
    proj3 = proj.reshape(B, S, D_PROJ)
    spec = lambda off: pl.BlockSpec((nb, S, MOBA_HD), lambda h, b: (b, 0, off // MOBA_HD + h))
    out = pl.pallas_call(
        functools.partial(_moba_kernel, nblk=nblk),
        grid=(H, B // nb),
        in_specs=[spec(OFF_QB), spec(OFF_KB), spec(OFF_VB),
                  pl.BlockSpec((1, nblk, BLK, BLK), lambda h, b: (h, 0, 0, 0))],
        out_specs=spec(0),
        out_shape=jax.ShapeDtypeStruct((B, S, MOBA_W), BF16),
        scratch_shapes=[pltpu.VMEM((nb, S, 2 * MOBA_HD), BF16), pltpu.VMEM((nb, S, 2 * MOBA_HD), BF16),
                        pltpu.VMEM((nb * nblk * (nblk + 1) // 2, BLK, BLK), F32),
                        pltpu.VMEM((nb * nblk, BLK, LANES), F32),
                        pltpu.VMEM((nb * max(nblk - MOBA_TOPK - 1, 1), BLK, 2 * MOBA_HD), BF16)],
        compiler_params=_params(("arbitrary", "arbitrary"), VMEM_LIMIT),
        name="moba",
    )(proj3, proj3, proj3, bias)
    return out.reshape(B * S, MOBA_W)


def _pack_exact(lo, hi):
    lo_b = lax.bitcast_convert_type(lo, jnp.uint32)
    hi_b = lax.bitcast_convert_type(hi, jnp.uint32)
    return (lo_b >> 16) | (hi_b & jnp.uint32(0xFFFF0000))


def _unpack(w):
    lo = lax.bitcast_convert_type(w << 16, F32)
    hi = lax.bitcast_convert_type(w & jnp.uint32(0xFFFF0000), F32)
    return lo.astype(BF16), hi.astype(BF16)


def _merge_kernel(oa_ref, ob_ref, ga_ref, gb_ref, x_ref, wua_ref, wub_ref, wo_ref, gffn_ref, wr_ref, br_ref,
                  x1_ref, h2_ref, lg_ref):
    tm = x_ref.shape[0]
    u_a = jnp.dot(oa_ref[...], wua_ref[...], preferred_element_type=F32)
    u_b = jnp.dot(ob_ref[...], wub_ref[...], preferred_element_type=F32)
    y = _sigmoid(ga_ref[...].astype(F32)) * u_a + _sigmoid(gb_ref[...].astype(F32)) * u_b
    x1 = x_ref[...] + jnp.dot(y.astype(BF16), wo_ref[...], preferred_element_type=F32)
    x1_ref[...] = x1
    h2 = _rms(x1, gffn_ref[...])
    h_hi = h2.astype(BF16)
    h2_ref[...] = h_hi
    h_lo = (h2 - h_hi.astype(F32)).astype(BF16)
    r = jnp.dot(jnp.concatenate([h_hi, h_lo], axis=0), wr_ref[...], preferred_element_type=F32)
    lg_ref[...] = r[:tm, :LANES] + r[:tm, LANES:] + r[tm:, :LANES] + br_ref[...]


def _merge(o_a, o_b, proj, x2, w_ua, w_ub, w_o, g_ffn, w_r2, b_r, tm):
    T = x2.shape[0]
    full = lambda shape: pl.BlockSpec(shape, lambda i: (0, 0))
    rowblk = lambda w: pl.BlockSpec((tm, w), lambda i: (i, 0))
    return pl.pallas_call(
        _merge_kernel,
        grid=(T // tm,),
        in_specs=[rowblk(GLA_V), rowblk(MOBA_W),
                  pl.BlockSpec((tm, D_MODEL), lambda i: (i, OFF_GA // D_MODEL)),
                  pl.BlockSpec((tm, D_MODEL), lambda i: (i, OFF_GB // D_MODEL)),
                  rowblk(D_MODEL),
                  full((GLA_V, D_MODEL)), full((MOBA_W, D_MODEL)), full((D_MODEL, D_MODEL)),
                  full((1, D_MODEL)), full((D_MODEL, 2 * LANES)), full((1, LANES))],
        out_specs=[rowblk(D_MODEL), rowblk(D_MODEL), rowblk(LANES)],
        out_shape=[jax.ShapeDtypeStruct((T, D_MODEL), F32),
                   jax.ShapeDtypeStruct((T, D_MODEL), BF16),
                   jax.ShapeDtypeStruct((T, LANES), F32)],
        compiler_params=_params(("arbitrary",), VMEM_LIMIT),
        name="merge",
    )(o_a, o_b, proj, proj, x2, w_ua, w_ub, w_o, g_ffn, w_r2, b_r)


def _router_kernel(lg_ref, posw_ref, cnt_ref, carry_ref, cnt_scr, *, tm):
    rows = lg_ref.shape[0]

    @pl.when(pl.program_id(0) == 0)
    def _():
        cnt_scr[...] = jnp.zeros(cnt_scr.shape, F32)

    lane = lax.broadcasted_iota(jnp.int32, (rows, LANES), 1)
    lane_f = lane.astype(F32)
    work = jnp.where(lane < N_EXPERTS, lg_ref[...], NEG)
    vals, hots = [], []
    for _ in range(TOP_K):
        mx = work.max(axis=-1, keepdims=True)
        idx = jnp.min(jnp.where(work == mx, lane_f, float(LANES)), axis=-1, keepdims=True)
        hot = lane_f == idx
        vals.append(mx)
        hots.append(hot)
        work = jnp.where(hot, 2.0 * NEG, work)
    exps = [jnp.exp(v - vals[0]) for v in vals]
    den = exps[0] + exps[1] + exps[2] + exps[3]
    sel = jnp.zeros((rows, LANES), F32)
    for hot in hots:
        sel = sel + hot.astype(F32)
    row = lax.broadcasted_iota(jnp.int32, (tm, tm), 0)
    col = lax.broadcasted_iota(jnp.int32, (tm, tm), 1)
    below = (col < row).astype(BF16)
    er = lax.broadcasted_iota(jnp.int32, (LANES, LANES), 0)
    ec = lax.broadcasted_iota(jnp.int32, (LANES, LANES), 1)
    before = (er < ec).astype(F32)
    pos_parts = []
    for t in range(rows // tm):
        sel_t = sel[t * tm:(t + 1) * tm]
        local_rank = jnp.dot(below, sel_t.astype(BF16), preferred_element_type=F32)
        cnt_t = sel_t.sum(axis=0, keepdims=True)
        cnt_t = jnp.floor((cnt_t + (RUN_ALIGN - 1.0)) * (1.0 / RUN_ALIGN)) * RUN_ALIGN
        tile_off = jnp.dot(jnp.broadcast_to(cnt_t, (8, LANES)), before, preferred_element_type=F32,
                           precision=lax.Precision.HIGHEST)[0:1]
        pos_parts.append(local_rank + tile_off)
        carry_ref[t] = cnt_scr[...]
        cnt_ref[t] = cnt_t
        cnt_scr[...] = cnt_scr[...] + cnt_t
    pos_all = jnp.concatenate(pos_parts, axis=0)
    posw = jnp.zeros((rows, LANES), F32)
    for k in range(TOP_K):
        pk = jnp.sum(jnp.where(hots[k], pos_all, 0.0), axis=-1, keepdims=True)
        posw = jnp.where(lane == k, pk, posw)
        posw = jnp.where(lane == TOP_K + k, exps[k] / den, posw)
    posw_ref[...] = posw


def _router(logits, tm, tiles_per_step=4):
    T = logits.shape[0]
    nt = T // tm
    rows = tm * tiles_per_step
    tilerow = pl.BlockSpec((tiles_per_step, 1, LANES), lambda i: (i, 0, 0))
    return pl.pallas_call(
        functools.partial(_router_kernel, tm=tm),
        grid=(T // rows,),
        in_specs=[pl.BlockSpec((rows, LANES), lambda i: (i, 0))],
        out_specs=[pl.BlockSpec((rows, LANES), lambda i: (i, 0)), tilerow, tilerow],
        out_shape=[jax.ShapeDtypeStruct((T, LANES), F32),
                   jax.ShapeDtypeStruct((nt, 1, LANES), F32),
                   jax.ShapeDtypeStruct((nt, 1, LANES), F32)],
        scratch_shapes=[pltpu.VMEM((1, LANES), F32)],
        compiler_params=_params(("arbitrary",)),
        name="router",
    )(logits)


def _run_pieces(n, max_n, fn):
    for b in reversed(range(max_n.bit_length())):
        size = 1 << b
        done = n & ~((2 << b) - 1)

        @pl.when((n & size) != 0)
        def _():
            fn(done, size)


def _onehot_bands(pos, val):
    band = 256
    assert TILE_ROWS % band == 0
    n_k, n_tok = pos.shape
    pos_a = jnp.floor(pos * (1.0 / band))
    pos_b = pos - band * pos_a
    sub = lax.broadcasted_iota(jnp.int32, (band, n_tok), 0).astype(F32).astype(BF16)
    zero = jnp.zeros((band, n_tok), BF16)
    bands = []
    for a in range(TILE_ROWS // band):
        want = jnp.where(pos_a == a, pos_b, -1.0).astype(BF16)
        hit = zero
        for k in range(n_k):
            row = jnp.broadcast_to(want[k:k + 1, :], (band, n_tok))
            fill = jnp.ones((band, n_tok), BF16) if val is None else jnp.broadcast_to(val[k:k + 1, :], (band, n_tok))
            hit = hit + jnp.where(sub == row, fill, zero)
        bands.append(hit)
    return jnp.concatenate(bands, axis=0)


def _dispatch_kernel(toff_ref, eoff_ref, n_ref, trows_ref, zoff_ref, zn_ref, tail_ref, h2_ref, posw_ref, x_ref,
                     buf, zbuf, sems, *, nt):
    tm = h2_ref.shape[0]
    half = D_MODEL // 2
    j = pl.program_id(0)
    zgroups = zbuf.shape[0]
    tile_groups = buf.shape[1]

    def tile_runs(t, act):
        slot = t % 2

        def body(e, c):
            r = t * N_EXPERTS + e
            t0, d0 = toff_ref[r], eoff_ref[r]
            _run_pieces(n_ref[r], tm // RUN_ALIGN, lambda done, size: act(pltpu.make_async_copy(
                buf.at[slot, pl.ds(t0 + done, size)], x_ref.at[pl.ds(d0 + done, size)], sems.at[slot])))
            return c

        lax.fori_loop(0, N_EXPERTS, body, 0, unroll=RUN_UNROLL)

    def zero_fill(act):
        def body(e, c):
            d0 = zoff_ref[e]
            _run_pieces(zn_ref[e], zgroups, lambda done, size: act(pltpu.make_async_copy(
                zbuf.at[pl.ds(0, size)], x_ref.at[pl.ds(d0 + done, size)], sems.at[2])))
            return c

        lax.fori_loop(0, N_EXPERTS, body, 0)

        def tail(i, c):
            act(pltpu.make_async_copy(zbuf, x_ref.at[pl.ds(i * zgroups, zgroups)], sems.at[2]))
            return c

        lax.fori_loop(tail_ref[0], x_ref.shape[0] // zgroups, tail, 0)

    start = lambda cp: cp.start()
    wait = lambda cp: cp.wait()

    def wait_tile(t):
        slot = t % 2
        _run_pieces(trows_ref[t], tile_groups, lambda done, size: pltpu.make_async_copy(
            buf.at[slot, pl.ds(0, size)], x_ref.at[pl.ds(0, size)], sems.at[slot]).wait())

    @pl.when(j == 0)
    def _():
        zbuf[...] = jnp.zeros(zbuf.shape, zbuf.dtype)
        zero_fill(start)

    @pl.when(j >= 2)
    def _():
        wait_tile(j - 2)

    perm = _onehot_bands(posw_ref[...].T[:TOP_K], None)
    xs = jnp.dot(perm, h2_ref[...], preferred_element_type=F32)
    buf[j % 2] = _pack_exact(xs[:, :half], xs[:, half:]).reshape(tile_groups, RUN_ALIGN, half)
    tile_runs(j, start)

    @pl.when(j == nt - 1)
    def _():
        if nt >= 2:
            wait_tile(j - 1)
        wait_tile(j)
        zero_fill(wait)


def _dispatch(h2, posw, n_rows, tile_off, expert_off, run_n, tile_rows, zoff, zn, tail, tm):
    T = h2.shape[0]
    nt = T // tm
    G = RUN_ALIGN
    assert n_rows % EXPERT_BLOCK == 0 and EXPERT_BLOCK % G == 0 and TILE_ROWS % G == 0
    x_rows = pl.pallas_call(
        functools.partial(_dispatch_kernel, nt=nt),
        grid_spec=pltpu.PrefetchScalarGridSpec(
            num_scalar_prefetch=7,
            grid=(nt,),
            in_specs=[pl.BlockSpec((tm, D_MODEL), lambda i, *_: (i, 0)),
                      pl.BlockSpec((tm, LANES), lambda i, *_: (i, 0))],
            out_specs=pl.BlockSpec(memory_space=pl.ANY),
            scratch_shapes=[pltpu.VMEM((2, TILE_ROWS // G, G, D_MODEL // 2), jnp.uint32),
                            pltpu.VMEM((EXPERT_BLOCK // G, G, D_MODEL // 2), jnp.uint32),
                            pltpu.SemaphoreType.DMA((3,))]),
        out_shape=jax.ShapeDtypeStruct((n_rows // G, G, D_MODEL // 2), jnp.uint32),
        compiler_params=_params(("arbitrary",), VMEM_LIMIT),
        name="dispatch",
    )(tile_off, expert_off, run_n, tile_rows, zoff, zn, tail, h2, posw)
    return x_rows.reshape(n_rows, D_MODEL // 2)


def _expert_kernel(be_ref, rows_ref, slot_ref, next_ref, x_ref, wg_ref, bg_ref, wu_ref, bu_ref, wd_ref, bd_ref, y_ref,
                   w_in, wg_s, wu_s, wd_s, sems):
    i = pl.program_id(0)
    half = D_MODEL // 2
    M = x_ref.shape[0]
    rows = rows_ref[i]
    e = be_ref[i]
    prev = be_ref[jnp.maximum(i - 1, 0)]

    def weight_copies(expert, slot):
        return [pltpu.make_async_copy(w_hbm.at[expert], w_in.at[slot, k], sems.at[slot, k])
                for k, w_hbm in enumerate((wg_ref, wu_ref, wd_ref))]

    @pl.when((rows > 0) & ((i == 0) | (e != prev)))
    def _():
        slot = slot_ref[i]

        @pl.when(i == 0)
        def _():
            for cp in weight_copies(e, slot):
                cp.start()

        nxt = next_ref[i]

        @pl.when(nxt >= 0)
        def _():
            for cp in weight_copies(nxt, 1 - slot):
                cp.start()

        for cp in weight_copies(e, slot):
            cp.wait()
        for k, dst in enumerate((wg_s, wu_s, wd_s)):
            dst[...] = w_in[slot, k].astype(BF16)

    def compute(r):
        x_lo, x_hi = _unpack(x_ref[:r, :])

        def proj_in(w_s, b_ref):
            return (jnp.dot(x_lo, w_s[:half, :], preferred_element_type=F32)
                    + jnp.dot(x_hi, w_s[half:, :], preferred_element_type=F32) + b_ref[0])

        gate = jnp.minimum(proj_in(wg_s, bg_ref), SWIGLU_LIMIT)
        up = jnp.clip(proj_in(wu_s, bu_ref), -SWIGLU_LIMIT, SWIGLU_LIMIT)
        glu = gate * _sigmoid(gate * SWIGLU_ALPHA)
        act = ((up + 1.0) * glu).astype(BF16)
        y = jnp.dot(act, wd_s[...], preferred_element_type=F32) + bd_ref[0]
        y_ref[:r, :] = y
        if r < M:
            y_ref[r:, :] = jnp.zeros((M - r, D_MODEL), y_ref.dtype)

    for r in range(EXPERT_SUB, M + 1, EXPERT_SUB):
        pl.when(rows == r)(functools.partial(compute, r))

    @pl.when(rows == 0)
    def _():
        y_ref[...] = jnp.zeros(y_ref.shape, y_ref.dtype)


def _experts(blk_exp, blk_rows, blk_slot, blk_next, x_rows, n_pad, w_g, b_g, w_u, b_u, w_d, b_d):
    M = EXPERT_BLOCK
    assert D_FF == D_MODEL
    bspec = lambda n: pl.BlockSpec((1, 1, n), lambda i, be, *_: (be[i], 0, 0))
    wspec = pl.BlockSpec(memory_space=pl.ANY)
    return pl.pallas_call(
        _expert_kernel,
        grid_spec=pltpu.PrefetchScalarGridSpec(
            num_scalar_prefetch=4,
            grid=(n_pad // M,),
            in_specs=[pl.BlockSpec((M, D_MODEL // 2), lambda i, *_: (i, 0)),
                      wspec, bspec(D_FF), wspec, bspec(D_FF), wspec, bspec(D_MODEL)],
            out_specs=pl.BlockSpec((M, D_MODEL), lambda i, *_: (i, 0)),
            scratch_shapes=[pltpu.VMEM((2, 3, D_MODEL, D_FF), F32),
                            pltpu.VMEM((D_MODEL, D_FF), BF16),
                            pltpu.VMEM((D_MODEL, D_FF), BF16),
                            pltpu.VMEM((D_FF, D_MODEL), BF16),
                            pltpu.SemaphoreType.DMA((2, 3))]),
        out_shape=jax.ShapeDtypeStruct((n_pad, D_MODEL), F32),
        compiler_params=_params(("arbitrary",), VMEM_LIMIT),
        name="experts",
    )(blk_exp, blk_rows, blk_slot, blk_next, x_rows, w_g, b_g, w_u, b_u, w_d, b_d)


def _final_kernel(toff_ref, eoff_ref, n_ref, trows_ref, x1_ref, posw_ref, p_ref, gpg_ref, wpg_ref, wpp_ref, gpp_ref,
                  gfin_ref, y_ref, o_ref, buf, xmid, sems, *, nt):
    tm = x1_ref.shape[0]
    step = pl.program_id(0)
    j = jnp.minimum(step, nt - 1)

    def tile_runs(t, act):
        slot = t % 2

        def body(e, c):
            r = t * N_EXPERTS + e
            t0, s0 = toff_ref[r], eoff_ref[r]
            _run_pieces(n_ref[r], tm // RUN_ALIGN, lambda done, size: act(pltpu.make_async_copy(
                y_ref.at[pl.ds(s0 + done, size)], buf.at[slot, pl.ds(t0 + done, size)], sems.at[slot])))
            return c

        lax.fori_loop(0, N_EXPERTS, body, 0, unroll=RUN_UNROLL)

    @pl.when(step == 0)
    def _():
        buf[...] = jnp.zeros(buf.shape, buf.dtype)
        xmid[...] = jnp.zeros(xmid.shape, xmid.dtype)
        tile_runs(0, lambda cp: cp.start())

    @pl.when(step + 1 < nt)
    def _():
        tile_runs(step + 1, lambda cp: cp.start())

    @pl.when(step < nt)
    def _():
        _run_pieces(trows_ref[j], buf.shape[1], lambda done, size: pltpu.make_async_copy(
            y_ref.at[pl.ds(0, size)], buf.at[j % 2, pl.ds(0, size)], sems.at[j % 2]).wait())

    x = xmid[...]
    pp = jnp.dot(p_ref[...].astype(BF16), wpp_ref[...], preferred_element_type=F32)
    pg = _sigmoid(jnp.dot(_rms(x, gpg_ref[...]).astype(BF16), wpg_ref[...], preferred_element_type=F32))
    x = x + pg * _rms(pp, gpp_ref[...])
    o_ref[...] = _rms(x, gfin_ref[...])

    posw_t = posw_ref[...].T
    comb_t = _onehot_bands(posw_t[:TOP_K], posw_t[TOP_K:2 * TOP_K].astype(BF16))
    sure = TOP_K * tm
    sure_g = sure // RUN_ALIGN
    head = buf[j % 2, :sure_g].reshape(sure, D_MODEL)
    tail = buf[j % 2, sure_g:].reshape(TILE_ROWS - sure, D_MODEL)
    live = lax.broadcasted_iota(jnp.int32, tail.shape, 0) < trows_ref[j] * RUN_ALIGN - sure
    y = jnp.concatenate([head, jnp.where(live, tail, 0.0)], axis=0).astype(BF16)
    xmid[...] = x1_ref[...] + lax.dot_general(comb_t, y, TN, preferred_element_type=F32)


def _final(tile_off, expert_off, run_n, tile_rows, x1, y_rows, posw, p2, g_pg, w_pg, w_pp, g_pp, g_fin, tm):
    T = x1.shape[0]
    nt = T // tm
    full = lambda shape: pl.BlockSpec(shape, lambda i, *_: (0, 0))
    this = lambda w: pl.BlockSpec((tm, w), lambda i, *_: (jnp.minimum(i, nt - 1), 0))
    prev = lambda w: pl.BlockSpec((tm, w), lambda i, *_: (jnp.maximum(i - 1, 0), 0))
    return pl.pallas_call(
        functools.partial(_final_kernel, nt=nt),
        grid_spec=pltpu.PrefetchScalarGridSpec(
            num_scalar_prefetch=4,
            grid=(nt + 1,),
            in_specs=[this(D_MODEL), this(LANES), prev(PLE_DIM),
                      full((1, D_MODEL)), full((D_MODEL, D_MODEL)), full((PLE_DIM, D_MODEL)),
                      full((1, D_MODEL)), full((1, D_MODEL)),
                      pl.BlockSpec(memory_space=pl.ANY)],
            out_specs=prev(D_MODEL),
            scratch_shapes=[pltpu.VMEM((2, TILE_ROWS // RUN_ALIGN, RUN_ALIGN, D_MODEL), F32),
                            pltpu.VMEM((tm, D_MODEL), F32),
                            pltpu.SemaphoreType.DMA((2,))]),
        out_shape=jax.ShapeDtypeStruct((T, D_MODEL), F32),
        compiler_params=_params(("arbitrary",), VMEM_LIMIT),
        name="final",
    )(tile_off, expert_off, run_n, tile_rows, x1, posw, p2, g_pg, w_pg, w_pp, g_pp, g_fin,
      y_rows.reshape(-1, RUN_ALIGN, D_MODEL))


def _split_w_in(w_in):
    sizes = (GLA_QK, GLA_QK, GLA_V, GLA_V, GLA_RANK, MOBA_W, MOBA_W, MOBA_W, D_MODEL, D_MODEL)
    offs = [0]
    for s in sizes:
        offs.append(offs[-1] + s)
    main = jnp.concatenate([w_in[:, :offs[4]], w_in[:, offs[5]:]], axis=1).astype(BF16)
    alow = jnp.pad(w_in[:, offs[4]:offs[5]], ((0, 0), (0, LANES - GLA_RANK))).astype(BF16)
    return main, alow


def _layer(x2, p2, bias, B, S, g_mix, w_in, w_a2, b_a2, g_gla_out, w_up_gla, w_up_moba, w_o, g_ffn, w_router,
           b_router, w_e_gate, b_e_gate, w_e_up, b_e_up, w_e_down, b_e_down, g_ple_gate, w_ple_gate, w_ple_proj,
           g_ple_proj, g_final):
    T = B * S
    row = lambda v: v.reshape(1, -1).astype(F32)
    w_main, w_alow = _split_w_in(w_in)
    colscale = jnp.ones((D_PROJ,), F32)
    colscale = colscale.at[OFF_QA:OFF_QA + GLA_QK].set(GLA_DK ** -0.5)
    colscale = colscale.at[OFF_QB:OFF_QB + MOBA_W].set(MOBA_HD ** -0.5 * LOG2E)
    w_a2p = jnp.pad(w_a2, ((0, LANES - GLA_RANK), (0, 0))).astype(BF16)
    proj, glog = _inproj(x2, row(g_mix), w_main, colscale.reshape(1, -1), w_alow, w_a2p, row(b_a2))

    o_a = _gla(proj, glog, row(g_gla_out), B, S)
    o_b = _moba(proj, bias, B, S)

    w_r = jnp.pad(w_router.astype(F32), ((0, 0), (0, LANES - N_EXPERTS)))
    w_r_hi = w_r.astype(BF16)
    w_r2 = jnp.concatenate([w_r_hi, (w_r - w_r_hi.astype(F32)).astype(BF16)], axis=1)
    b_r = jnp.pad(b_router.astype(F32), (0, LANES - N_EXPERTS)).reshape(1, -1)
    tm = TOKEN_TILE
    nt = T // tm
    x1, h2, logits = _merge(o_a, o_b, proj, x2, w_up_gla.astype(BF16), w_up_moba.astype(BF16),
                            w_o.astype(BF16), row(g_ffn), w_r2, b_r, MERGE_TILE)
    posw, cnt_t, carry = _router(logits, tm)

    M = EXPERT_BLOCK
    A = nt * TILE_ROWS
    n_pad = (-(-A // M)) * M + N_EXPERTS * M
    n_blk = n_pad // M
    cnt_t = cnt_t[:, 0, :N_EXPERTS].astype(jnp.int32)
    carry = carry[:, 0, :N_EXPERTS].astype(jnp.int32)
    counts = carry[-1] + cnt_t[-1]
    padded = (counts + M - 1) // M * M
    pad_end = jnp.cumsum(padded)
    pad_start = pad_end - padded
    blk_exp = jnp.minimum(jnp.sum(pad_end[None, :] <= (jnp.arange(n_blk, dtype=jnp.int32) * M)[:, None], axis=1),
                          N_EXPERTS - 1).astype(jnp.int32)
    n_used = (pad_end[-1:] // M).astype(jnp.int32)
    blk_start = jnp.arange(n_blk, dtype=jnp.int32) * M
    eids = jnp.arange(N_EXPERTS, dtype=jnp.int32)

    def per_block(per_expert):
        return jnp.sum(jnp.where(blk_exp[:, None] == eids[None, :], per_expert[None, :], 0), axis=1).astype(jnp.int32)

    blk_rows = jnp.clip(per_block(pad_start + counts) - blk_start, 0, M)
    blk_rows = jnp.where(blk_start < pad_end[-1], (blk_rows + EXPERT_SUB - 1) // EXPERT_SUB * EXPERT_SUB, 0)
    has_rows = counts > 0
    blk_slot = per_block((jnp.cumsum(has_rows) - 1) % 2)
    later = jnp.where((eids[None, :] > eids[:, None]) & has_rows[None, :], eids[None, :], N_EXPERTS)
    next_exp = jnp.min(later, axis=1)
    blk_next = per_block(jnp.where(next_exp < N_EXPERTS, next_exp, -1))
    G = RUN_ALIGN
    tile_off = (jnp.cumsum(cnt_t, axis=1) - cnt_t).reshape(-1) // G
    expert_off = (carry + pad_start[None, :]).reshape(-1) // G
    run_n = cnt_t.reshape(-1) // G
    tile_rows = jnp.sum(cnt_t, axis=1) // G
    x_rows = _dispatch(h2, posw, n_pad, tile_off, expert_off, run_n, tile_rows, (pad_start + counts) // G,
                       (padded - counts) // G, n_used, tm)
    y_rows = _experts(blk_exp, blk_rows.astype(jnp.int32), blk_slot, blk_next, x_rows, n_pad,
                      w_e_gate, b_e_gate.reshape(N_EXPERTS, 1, D_FF),
                      w_e_up, b_e_up.reshape(N_EXPERTS, 1, D_FF), w_e_down,
                      b_e_down.reshape(N_EXPERTS, 1, D_MODEL))
    return _final(tile_off, expert_off, run_n, tile_rows, x1, y_rows, posw, p2, row(g_ple_gate),
                  w_ple_gate.astype(BF16), w_ple_proj.astype(BF16), row(g_ple_proj), row(g_final), tm)


def kernel(x, p, rel_bias, g_mix, w_in, w_a2, b_a2, g_gla_out, w_up_gla, w_up_moba, w_o, g_ffn, w_router, b_router,
           w_e_gate, b_e_gate, w_e_up, b_e_up, w_e_down, b_e_down, g_ple_gate, w_ple_gate, w_ple_proj, g_ple_proj,
           g_final):
    B, S, D = x.shape
    assert D == D_MODEL and S % MOBA_BLOCK == 0 and S % GLA_CHUNK == 0 and p.shape[0] == 1
    bias = _bias_tiles(rel_bias, S // MOBA_BLOCK)
    out = _layer(x.reshape(B * S, D), p[0].reshape(B * S, PLE_DIM), bias, B, S,
                 g_mix[0], w_in[0], w_a2[0], b_a2[0], g_gla_out[0], w_up_gla[0], w_up_moba[0], w_o[0], g_ffn[0],
                 w_router[0], b_router[0], w_e_gate[0], b_e_gate[0], w_e_up[0], b_e_up[0], w_e_down[0],
                 b_e_down[0], g_ple_gate[0], w_ple_gate[0], w_ple_proj[0], g_ple_proj[0], g_final)
    return out.reshape(B, S, D)
```

```python
import functools
import math

import jax
import jax.numpy as jnp
from jax import lax
from jax.experimental import pallas as pl
from jax.experimental.pallas import tpu as pltpu

F32 = jnp.float32
BF16 = jnp.bfloat16

D_MODEL = 1024
PLE_DIM = 256
GLA_HEADS = 4
GLA_DK = 128
GLA_DV = 256
GLA_RANK = 16
GLA_TAU = 16.0
GLA_QK = GLA_HEADS * GLA_DK
GLA_V = GLA_HEADS * GLA_DV
MOBA_HEADS = 8
MOBA_HD = 128
MOBA_BLOCK = 256
MOBA_TOPK = 3
MOBA_W = MOBA_HEADS * MOBA_HD
REL_BUCKETS = 32
REL_MAX_DIST = 4096
N_EXPERTS = 32
TOP_K = 4
D_FF = 1024
SWIGLU_LIMIT = 7.0
SWIGLU_ALPHA = 1.702
EPS = 1e-6

LANES = 128
NEG = -1e30
LOG2E = math.log2(math.e)
VMEM_LIMIT = 56 * 1024 * 1024

OFF_QA, OFF_KA, OFF_VA, OFF_RA = 0, 512, 1024, 2048
OFF_QB, OFF_KB, OFF_VB, OFF_GA, OFF_GB = 3072, 4096, 5120, 6144, 7168
D_PROJ = 8192

GLA_CHUNK = 128
EXPERT_BLOCK = 1024
EXPERT_PART = 512
EXPERT_SUB = 128
TOKEN_TILE = 256
RUN_ALIGN = 8
TILE_ROWS = TOP_K * TOKEN_TILE + N_EXPERTS * RUN_ALIGN
RUN_UNROLL = 8
MERGE_TILE = 512

NT = (((1,), (1,)), ((), ()))
TN = (((0,), (0,)), ((), ()))


def _params(sem, vmem=None):
    return pltpu.CompilerParams(dimension_semantics=sem, vmem_limit_bytes=vmem)


def _rms(x, g):
    return x * lax.rsqrt(jnp.mean(x * x, axis=-1, keepdims=True) + EPS) * g


def _sigmoid(x):
    return 1.0 / (1.0 + jnp.exp(-x))


def _bucket_of(n):
    max_exact = REL_BUCKETS // 2
    if n < max_exact:
        return n
    return min(max_exact + int(math.log(n / max_exact) / math.log(REL_MAX_DIST / max_exact)
                               * (REL_BUCKETS - max_exact)), REL_BUCKETS - 1)


def _bias_kernel(tab_ref, bkt_ref, o_ref, *, nblk):
    strip = 8
    m = pl.program_id(0)

    def fill(lo, hi):
        def body(s, carry):
            r0 = pl.multiple_of(s * strip, strip)
            b = bkt_ref[0, pl.ds(r0, strip), :]
            accs = [jnp.zeros(b.shape, F32) for _ in range(MOBA_HEADS)]
            for bb in range(lo, hi + 1):
                hit = b == bb
                for h in range(MOBA_HEADS):
                    accs[h] = jnp.where(hit, tab_ref[h, bb], accs[h])
            for h in range(MOBA_HEADS):
                o_ref[h, 0, pl.ds(r0, strip), :] = jnp.where(b < 0, NEG, accs[h])
            return carry

        lax.fori_loop(0, MOBA_BLOCK // strip, body, 0)

    for mm in range(nblk):
        lo = max(_bucket_of(max(mm * MOBA_BLOCK - (MOBA_BLOCK - 1), 0)) - 1, 0)
        hi = min(_bucket_of(mm * MOBA_BLOCK + MOBA_BLOCK - 1) + 1, REL_BUCKETS - 1)
        pl.when(m == mm)(functools.partial(fill, lo, hi))


def _bias_tiles(rel_bias, nblk):
    i = jnp.arange(MOBA_BLOCK, dtype=jnp.int32)
    dist = (jnp.arange(nblk, dtype=jnp.int32)[:, None, None] * MOBA_BLOCK + i[None, :, None] - i[None, None, :])
    n = jnp.maximum(dist, 0)
    max_exact = REL_BUCKETS // 2
    nf = jnp.maximum(n, 1).astype(F32)
    large = max_exact + (jnp.log(nf / max_exact) / math.log(REL_MAX_DIST / max_exact)
                         * (REL_BUCKETS - max_exact)).astype(jnp.int32)
    large = jnp.minimum(large, REL_BUCKETS - 1)
    bkt = jnp.where(dist < 0, -1, jnp.where(n < max_exact, n, large)).astype(jnp.int32)
    tab = rel_bias.astype(F32).T * LOG2E
    return pl.pallas_call(
        functools.partial(_bias_kernel, nblk=nblk),
        grid=(nblk,),
        in_specs=[pl.BlockSpec(memory_space=pltpu.SMEM),
                  pl.BlockSpec((1, MOBA_BLOCK, MOBA_BLOCK), lambda m: (m, 0, 0))],
        out_specs=pl.BlockSpec((MOBA_HEADS, 1, MOBA_BLOCK, MOBA_BLOCK), lambda m: (0, m, 0, 0)),
        out_shape=jax.ShapeDtypeStruct((MOBA_HEADS, nblk, MOBA_BLOCK, MOBA_BLOCK), F32),
        compiler_params=_params(("arbitrary",)),
        name="bias_tiles",
    )(tab, bkt)


def _inproj_kernel(x0_ref, xn_ref, g_ref, w_ref, cs_ref, wal_ref, wa2_ref, ba2_ref, o_ref, glog_ref, h_scr, al_scr,
                   *, n_j):
    i, j = pl.program_id(0), pl.program_id(1)
    tm = xn_ref.shape[0]
    rs = tm // n_j

    def gate_logits(a_low):
        a = jnp.dot(a_low, wa2_ref[...], preferred_element_type=F32) + ba2_ref[...]
        log_sig = jnp.minimum(a, 0.0) - jnp.log1p(jnp.exp(-jnp.abs(a)))
        return log_sig * (1.0 / GLA_TAU)

    def slice_rows(behind):
        return pl.multiple_of(((j + n_j - behind) % n_j) * rs, rs)

    @pl.when((i == 0) & (j == 0))
    def _():
        h0 = _rms(x0_ref[...], g_ref[...]).astype(BF16)
        h_scr[0] = h0
        a_low0 = jnp.dot(h0, wal_ref[...], preferred_element_type=F32).astype(BF16)
        glog_ref[...] = gate_logits(a_low0)
        al_scr[...] = a_low0[(n_j - 2) * rs:(n_j - 1) * rs, :]

    glog_ref[pl.ds(slice_rows(2), rs), :] = gate_logits(al_scr[...])
    slot = jnp.where(j == 0, i, i + 1) % 2
    h_lag = h_scr[slot, pl.ds(slice_rows(1), rs), :]
    al_scr[...] = jnp.dot(h_lag, wal_ref[...], preferred_element_type=F32).astype(BF16)

    acc = jnp.dot(h_scr[i % 2], w_ref[...], preferred_element_type=F32)
    o_ref[...] = (acc * cs_ref[...]).astype(BF16)

    h_scr[(i + 1) % 2, pl.ds(slice_rows(0), rs), :] = _rms(xn_ref[pl.ds(slice_rows(0), rs), :],
                                                             g_ref[...]).astype(BF16)


def _inproj(x2, g_mix, w_main, colscale, w_alow, w_a2p, b_a2, tm=1024, tn=2048):
    T = x2.shape[0]
    n_i, n_j = T // tm, D_PROJ // tn
    assert n_i >= 2 and tm % (8 * n_j) == 0
    return pl.pallas_call(
        functools.partial(_inproj_kernel, n_j=n_j),
        grid=(n_i, n_j),
        in_specs=[pl.BlockSpec((tm, D_MODEL), lambda i, j: (0, 0)),
                  pl.BlockSpec((tm, D_MODEL), lambda i, j: (jnp.minimum(i + 1, n_i - 1), 0)),
                  pl.BlockSpec((1, D_MODEL), lambda i, j: (0, 0)),
                  pl.BlockSpec((D_MODEL, tn), lambda i, j: (0, j)),
                  pl.BlockSpec((1, tn), lambda i, j: (0, j)),
                  pl.BlockSpec((D_MODEL, LANES), lambda i, j: (0, 0)),
                  pl.BlockSpec((LANES, GLA_QK), lambda i, j: (0, 0)),
                  pl.BlockSpec((1, GLA_QK), lambda i, j: (0, 0))],
        out_specs=[pl.BlockSpec((tm, tn), lambda i, j: (i, j)),
                   pl.BlockSpec((tm, GLA_QK), lambda i, j: (jnp.minimum(i + jnp.minimum(j // 2, 1), n_i - 1), 0))],
        out_shape=[jax.ShapeDtypeStruct((T, D_PROJ), BF16),
                   jax.ShapeDtypeStruct((T, GLA_QK), F32)],
        scratch_shapes=[pltpu.VMEM((2, tm, D_MODEL), BF16), pltpu.VMEM((tm // n_j, LANES), BF16)],
        compiler_params=_params(("arbitrary", "arbitrary"), VMEM_LIMIT),
        name="inproj",
    )(x2, x2, g_mix, w_main, colscale, w_alow, w_a2p, b_a2)


def _gla_kernel(q_ref, k_ref, v_ref, r_ref, g_ref, gout_ref, o_ref, st_ref):
    C = GLA_CHUNK

    @pl.when(pl.program_id(1) == 0)
    def _():
        st_ref[...] = jnp.zeros(st_ref.shape, F32)

    nb = q_ref.shape[0]
    row = lax.broadcasted_iota(jnp.int32, (C, C), 0)
    col = lax.broadcasted_iota(jnp.int32, (C, C), 1)
    causal = col <= row
    ltri = causal.astype(BF16)
    mid = C // 2
    pairs = [(b, h) for b in range(nb) for h in range(GLA_HEADS)]
    ks = lambda h: slice(h * GLA_DK, (h + 1) * GLA_DK)
    vs = lambda h: slice(h * GLA_DV, (h + 1) * GLA_DV)
    gate = {}
    for b, h in pairs:
        r = r_ref[b, :, vs(h)].astype(F32)
        gate[b, h] = r * _sigmoid(r)
    G = []
    for b in range(nb):
        g = g_ref[b]
        g_hi = g.astype(BF16)
        g_lo = (g - g_hi.astype(F32)).astype(BF16)
        G.append(jnp.dot(ltri, g_hi, preferred_element_type=F32) + jnp.dot(ltri, g_lo, preferred_element_type=F32))
    Gh = {(b, h): G[b][:, ks(h)] for b, h in pairs}
    qh = {(b, h): q_ref[b, :, ks(h)].astype(F32) for b, h in pairs}
    kh = {(b, h): k_ref[b, :, ks(h)].astype(F32) for b, h in pairs}
    g_mid = {p: Gh[p][mid:mid + 1, :] for p in pairs}
    g_last = {p: Gh[p][C - 1:C, :] for p in pairs}
    A = {p: lax.dot_general((qh[p] * jnp.exp(Gh[p] - g_mid[p])).astype(BF16),
                            (kh[p] * jnp.exp(g_mid[p] - Gh[p])).astype(BF16), NT, preferred_element_type=F32)
         for p in pairs}
    st = {p: st_ref[p[0], p[1]] for p in pairs}
    inter = {p: lax.dot_general((qh[p] * jnp.exp(Gh[p])).astype(BF16), st[p].astype(BF16), NT,
                                preferred_element_type=F32) for p in pairs}
    for b, h in pairs:
        p = (b, h)
        k_d = (kh[p] * jnp.exp(g_last[p] - Gh[p])).astype(BF16)
        st_ref[b, h] = jnp.exp(g_last[p]) * st[p] + lax.dot_general(v_ref[b, :, vs(h)], k_d, TN,
                                                                     preferred_element_type=F32)
    intra = {(b, h): jnp.dot(jnp.where(causal, A[b, h], 0.0).astype(BF16), v_ref[b, :, vs(h)],
                             preferred_element_type=F32) for b, h in pairs}
    for b, h in pairs:
        o = inter[b, h] + intra[b, h]
        o_ref[b, :, vs(h)] = (_rms(o, gout_ref[...]) * gate[b, h]).astype(BF16)


def _gla(proj, glog, g_gla_out, B, S, nb=4):
    C = GLA_CHUNK
    assert B % nb == 0
    proj3 = proj.reshape(B, S, D_PROJ)
    glog3 = glog.reshape(B, S, GLA_QK)
    spec = lambda w, off: pl.BlockSpec((nb, C, w), lambda b, c: (b, c, off // w))
    out = pl.pallas_call(
        _gla_kernel,
        grid=(B // nb, S // C),
        in_specs=[spec(GLA_QK, OFF_QA), spec(GLA_QK, OFF_KA), spec(GLA_V, OFF_VA), spec(GLA_V, OFF_RA),
                  spec(GLA_QK, 0), pl.BlockSpec((1, GLA_DV), lambda b, c: (0, 0))],
        out_specs=spec(GLA_V, 0),
        out_shape=jax.ShapeDtypeStruct((B, S, GLA_V), BF16),
        scratch_shapes=[pltpu.VMEM((nb, GLA_HEADS, GLA_DV, GLA_DK), F32)],
        compiler_params=_params(("arbitrary", "arbitrary")),
        name="gla",
    )(proj3, proj3, proj3, proj3, glog3, g_gla_out)
    return out.reshape(B * S, GLA_V)


def _moba_kernel(q_ref, k_ref, v_ref, bias_ref, o_ref, ka_scr, va_scr, lg_scr, mx_scr, qa_scr, *, nblk):
    BLK, HD = MOBA_BLOCK, MOBA_HD
    S = nblk * BLK
    units = range(q_ref.shape[0])

    @pl.when((pl.program_id(0) == 0) & (pl.program_id(1) == 0))
    def _():
        blk = lax.broadcasted_iota(jnp.int32, (S, HD), 0) // BLK
        lane = lax.broadcasted_iota(jnp.int32, (S, HD), 1)
        for u in units:
            ka_scr[u, :, HD:] = (lane == blk).astype(BF16)
            va_scr[u, :, HD:] = (lane == 0).astype(BF16)

    n_plain = min(MOBA_TOPK + 1, nblk)
    n_late = nblk - n_plain
    n_tiles = nblk * (nblk + 1) // 2

    def tile_id(cc, j):
        return cc * (cc + 1) // 2 + j

    def rows(cc):
        return slice(cc * BLK, (cc + 1) * BLK)

    def stage1(u, cc, q_in, keys):
        mx = None
        for j in range(cc + 1):
            lg = lax.dot_general(q_in, keys(j), NT, preferred_element_type=F32) + bias_ref[0, cc - j]
            lg_scr[u * n_tiles + tile_id(cc, j)] = lg
            t = jnp.maximum(lg[:, :LANES], lg[:, LANES:])
            mx = t if mx is None else jnp.maximum(mx, t)
            if j == cc:
                mx_scr[u * nblk + cc] = mx
            yield

    def stage2(u, cc):
        m = mx_scr[u * nblk + cc].max(axis=-1, keepdims=True)
        acc = jnp.zeros((BLK, 2 * HD), F32)
        for j in range(cc + 1):
            p = jnp.exp2(lg_scr[u * n_tiles + tile_id(cc, j)] - m).astype(BF16)
            acc = acc + jnp.dot(p, va_scr[u, rows(j), :], preferred_element_type=F32)
            if j == cc:
                o_ref[u, rows(cc), :] = (acc[:, :HD] / acc[:, HD:HD + 1]).astype(BF16)
            yield

    def drain(gen):
        for _ in gen:
            pass

    def interleave(main, side, n_main, n_side):
        side_steps = (s for g in side for s in g)
        done = 0
        for i, _ in enumerate(s for g in main for s in g):
            assert n_main > 0
            want = (i + 1) * n_side // n_main
            while done < want and next(side_steps, "end") != "end":
                done += 1
        drain(side_steps)

    plain = {u: [stage1(u, cc, q_ref[u, rows(cc), :], lambda j, u=u: k_ref[u, rows(j), :]) for cc in range(n_plain)]
             for u in units}
    if n_late:
        pens = []
        for u in units:
            next(plain[u][0])
            ksum = [k_ref[u, rows(j), :].astype(F32).reshape(BLK // 8, 8, HD).sum(axis=0).sum(axis=0, keepdims=True)
                    for j in range(nblk)]
            kmean = jnp.concatenate(ksum, axis=0) * (1.0 / BLK)
            km_hi = kmean.astype(BF16)
            km_lo = (kmean - km_hi.astype(F32)).astype(BF16)
            km2 = jnp.concatenate([km_hi, km_lo], axis=0)
            for cc in range(n_plain, nblk):
                q = q_ref[u, rows(cc), :]
                s2 = lax.dot_general(km2, q, NT, preferred_element_type=F32)
                pens.append((u, cc, q, s2[:nblk] + s2[nblk:]))
        for u in units:
            for g in plain[u][:2]:
                drain(g)
        for u, cc, q, s in pens:
            ji = lax.broadcasted_iota(jnp.int32, s.shape, 0)
            cnt = jnp.zeros(s.shape, F32)
            for jp in range(cc):
                sj = s[jp:jp + 1, :]
                beats = (sj > s) | ((sj == s) & (jp < ji))
                cnt = cnt + beats.astype(F32)
            pen = jnp.where((ji < cc) & (cnt >= MOBA_TOPK), NEG, 0.0)
            pen_t = jnp.concatenate([pen, jnp.zeros((HD - nblk, BLK), F32)], axis=0).T
            qa_scr[u * n_late + cc - n_plain] = jnp.concatenate([q, pen_t.astype(BF16)], axis=1)
    for u in units:
        ka_scr[u, :, :HD] = k_ref[u]
        va_scr[u, :, :HD] = v_ref[u]
    for u in units:
        for g in plain[u]:
            drain(g)

    @pl.when(pl.program_id(0) >= 0)
    def _():
        late = [stage1(u, cc, qa_scr[u * n_late + cc - n_plain], lambda j, u=u: ka_scr[u, rows(j), :])
                for u in units for cc in range(n_plain, nblk)]
        early = [stage2(u, cc) for u in units for cc in range(n_plain)]
        if late:
            interleave(late, early, len(units) * (n_tiles - tile_id(n_plain, 0)), len(units) * tile_id(n_plain, 0))
        else:
            for g in early:
                drain(g)

    @pl.when(pl.program_id(0) >= 0)
    def _():
        for u in units:
            for cc in range(n_plain, nblk):
                drain(stage2(u, cc))


def _moba(proj, bias, B, S, nb=2):
    BLK = MOBA_BLOCK
    nblk = S // BLK
    H = MOBA_HEADS
    assert nblk <= MOBA_HD and B % nb == 0
    proj3 = proj.reshape(B, S, D_PROJ)
    spec = lambda off: pl.BlockSpec((nb, S, MOBA_HD), lambda h, b: (b, 0, off // MOBA_HD + h))
    out = pl.pallas_call(
        functools.partial(_moba_kernel, nblk=nblk),
        grid=(H, B // nb),
        in_specs=[spec(OFF_QB), spec(OFF_KB), spec(OFF_VB),
                  pl.BlockSpec((1, nblk, BLK, BLK), lambda h, b: (h, 0, 0, 0))],
        out_specs=spec(0),
        out_shape=jax.ShapeDtypeStruct((B, S, MOBA_W), BF16),
        scratch_shapes=[pltpu.VMEM((nb, S, 2 * MOBA_HD), BF16), pltpu.VMEM((nb, S, 2 * MOBA_HD), BF16),
                        pltpu.VMEM((nb * nblk * (nblk + 1) // 2, BLK, BLK), F32),
                        pltpu.VMEM((nb * nblk, BLK, LANES), F32),
                        pltpu.VMEM((nb * max(nblk - MOBA_TOPK - 1, 1), BLK, 2 * MOBA_HD), BF16)],
        compiler_params=_params(("arbitrary", "arbitrary"), VMEM_LIMIT),
        name="moba",
    )(proj3, proj3, proj3, bias)
    return out.reshape(B * S, MOBA_W)


def _pack_exact(lo, hi):
    lo_b = lax.bitcast_convert_type(lo, jnp.uint32)
    hi_b = lax.bitcast_convert_type(hi, jnp.uint32)
    return (lo_b >> 16) | (hi_b & jnp.uint32(0xFFFF0000))


def _unpack(w):
    lo = lax.bitcast_convert_type(w << 16, F32)
    hi = lax.bitcast_convert_type(w & jnp.uint32(0xFFFF0000), F32)
    return lo.astype(BF16), hi.astype(BF16)


def _merge_kernel(oa_ref, ob_ref, ga_ref, gb_ref, x_ref, wua_ref, wub_ref, wo_ref, gffn_ref, wr_ref, br_ref,
                  x1_ref, h2_ref, lg_ref):
    tm = x_ref.shape[0]
    u_a = jnp.dot(oa_ref[...], wua_ref[...], preferred_element_type=F32)
    u_b = jnp.dot(ob_ref[...], wub_ref[...], preferred_element_type=F32)
    y = _sigmoid(ga_ref[...].astype(F32)) * u_a + _sigmoid(gb_ref[...].astype(F32)) * u_b
    x1 = x_ref[...] + jnp.dot(y.astype(BF16), wo_ref[...], preferred_element_type=F32)
    x1_ref[...] = x1
    h2 = _rms(x1, gffn_ref[...])
    h_hi = h2.astype(BF16)
    h2_ref[...] = h_hi
    h_lo = (h2 - h_hi.astype(F32)).astype(BF16)
    r = jnp.dot(jnp.concatenate([h_hi, h_lo], axis=0), wr_ref[...], preferred_element_type=F32)
    lg_ref[...] = r[:tm, :LANES] + r[:tm, LANES:] + r[tm:, :LANES] + br_ref[...]


def _merge(o_a, o_b, proj, x2, w_ua, w_ub, w_o, g_ffn, w_r2, b_r, tm):
    T = x2.shape[0]
    full = lambda shape: pl.BlockSpec(shape, lambda i: (0, 0))
    rowblk = lambda w: pl.BlockSpec((tm, w), lambda i: (i, 0))
    return pl.pallas_call(
        _merge_kernel,
        grid=(T // tm,),
        in_specs=[rowblk(GLA_V), rowblk(MOBA_W),
                  pl.BlockSpec((tm, D_MODEL), lambda i: (i, OFF_GA // D_MODEL)),
                  pl.BlockSpec((tm, D_MODEL), lambda i: (i, OFF_GB // D_MODEL)),
                  rowblk(D_MODEL),
                  full((GLA_V, D_MODEL)), full((MOBA_W, D_MODEL)), full((D_MODEL, D_MODEL)),
                  full((1, D_MODEL)), full((D_MODEL, 2 * LANES)), full((1, LANES))],
        out_specs=[rowblk(D_MODEL), rowblk(D_MODEL), rowblk(LANES)],
        out_shape=[jax.ShapeDtypeStruct((T, D_MODEL), F32),
                   jax.ShapeDtypeStruct((T, D_MODEL), BF16),
                   jax.ShapeDtypeStruct((T, LANES), F32)],
        compiler_params=_params(("arbitrary",), VMEM_LIMIT),
        name="merge",
    )(o_a, o_b, proj, proj, x2, w_ua, w_ub, w_o, g_ffn, w_r2, b_r)


def _router_kernel(lg_ref, posw_ref, cnt_ref, carry_ref, cnt_scr, *, tm):
    rows = lg_ref.shape[0]

    @pl.when(pl.program_id(0) == 0)
    def _():
        cnt_scr[...] = jnp.zeros(cnt_scr.shape, F32)

    lane = lax.broadcasted_iota(jnp.int32, (rows, LANES), 1)
    lane_f = lane.astype(F32)
    work = jnp.where(lane < N_EXPERTS, lg_ref[...], NEG)
    vals, hots = [], []
    for _ in range(TOP_K):
        mx = work.max(axis=-1, keepdims=True)
        idx = jnp.min(jnp.where(work == mx, lane_f, float(LANES)), axis=-1, keepdims=True)
        hot = lane_f == idx
        vals.append(mx)
        hots.append(hot)
        work = jnp.where(hot, 2.0 * NEG, work)
    exps = [jnp.exp(v - vals[0]) for v in vals]
    den = exps[0] + exps[1] + exps[2] + exps[3]
    sel = jnp.zeros((rows, LANES), F32)
    for hot in hots:
        sel = sel + hot.astype(F32)
    row = lax.broadcasted_iota(jnp.int32, (tm, tm), 0)
    col = lax.broadcasted_iota(jnp.int32, (tm, tm), 1)
    below = (col < row).astype(BF16)
    er = lax.broadcasted_iota(jnp.int32, (LANES, LANES), 0)
    ec = lax.broadcasted_iota(jnp.int32, (LANES, LANES), 1)
    before = (er < ec).astype(F32)
    pos_parts = []
    for t in range(rows // tm):
        sel_t = sel[t * tm:(t + 1) * tm]
        local_rank = jnp.dot(below, sel_t.astype(BF16), preferred_element_type=F32)
        cnt_t = sel_t.sum(axis=0, keepdims=True)
        cnt_t = jnp.floor((cnt_t + (RUN_ALIGN - 1.0)) * (1.0 / RUN_ALIGN)) * RUN_ALIGN
        tile_off = jnp.dot(jnp.broadcast_to(cnt_t, (8, LANES)), before, preferred_element_type=F32,
                           precision=lax.Precision.HIGHEST)[0:1]
        pos_parts.append(local_rank + tile_off)
        carry_ref[t] = cnt_scr[...]
        cnt_ref[t] = cnt_t
        cnt_scr[...] = cnt_scr[...] + cnt_t
    pos_all = jnp.concatenate(pos_parts, axis=0)
    posw = jnp.zeros((rows, LANES), F32)
    for k in range(TOP_K):
        pk = jnp.sum(jnp.where(hots[k], pos_all, 0.0), axis=-1, keepdims=True)
        posw = jnp.where(lane == k, pk, posw)
        posw = jnp.where(lane == TOP_K + k, exps[k] / den, posw)
    posw_ref[...] = posw


def _router(logits, tm, tiles_per_step=4):
    T = logits.shape[0]
    nt = T // tm
    rows = tm * tiles_per_step
    tilerow = pl.BlockSpec((tiles_per_step, 1, LANES), lambda i: (i, 0, 0))
    return pl.pallas_call(
        functools.partial(_router_kernel, tm=tm),
        grid=(T // rows,),
        in_specs=[pl.BlockSpec((rows, LANES), lambda i: (i, 0))],
        out_specs=[pl.BlockSpec((rows, LANES), lambda i: (i, 0)), tilerow, tilerow],
        out_shape=[jax.ShapeDtypeStruct((T, LANES), F32),
                   jax.ShapeDtypeStruct((nt, 1, LANES), F32),
                   jax.ShapeDtypeStruct((nt, 1, LANES), F32)],
        scratch_shapes=[pltpu.VMEM((1, LANES), F32)],
        compiler_params=_params(("arbitrary",)),
        name="router",
    )(logits)


def _run_pieces(n, max_n, fn):
    for b in reversed(range(max_n.bit_length())):
        size = 1 << b
        done = n & ~((2 << b) - 1)

        @pl.when((n & size) != 0)
        def _():
            fn(done, size)


def _onehot_bands(pos, val):
    band = 256
    assert TILE_ROWS % band == 0
    n_k, n_tok = pos.shape
    pos_a = jnp.floor(pos * (1.0 / band))
    pos_b = pos - band * pos_a
    sub = lax.broadcasted_iota(jnp.int32, (band, n_tok), 0).astype(F32).astype(BF16)
    zero = jnp.zeros((band, n_tok), BF16)
    bands = []
    for a in range(TILE_ROWS // band):
        want = jnp.where(pos_a == a, pos_b, -1.0).astype(BF16)
        hit = zero
        for k in range(n_k):
            row = jnp.broadcast_to(want[k:k + 1, :], (band, n_tok))
            fill = jnp.ones((band, n_tok), BF16) if val is None else jnp.broadcast_to(val[k:k + 1, :], (band, n_tok))
            hit = hit + jnp.where(sub == row, fill, zero)
        bands.append(hit)
    return jnp.concatenate(bands, axis=0)


def _dispatch_kernel(toff_ref, eoff_ref, n_ref, trows_ref, zoff_ref, zn_ref, tail_ref, h2_ref, posw_ref, x_ref,
                     buf, zbuf, sems, *, nt):
    tm = h2_ref.shape[0]
    half = D_MODEL // 2
    j = pl.program_id(0)
    zgroups = zbuf.shape[0]
    tile_groups = buf.shape[1]

    def tile_runs(t, act):
        slot = t % 2

        def body(e, c):
            r = t * N_EXPERTS + e
            t0, d0 = toff_ref[r], eoff_ref[r]
            _run_pieces(n_ref[r], tm // RUN_ALIGN, lambda done, size: act(pltpu.make_async_copy(
                buf.at[slot, pl.ds(t0 + done, size)], x_ref.at[pl.ds(d0 + done, size)], sems.at[slot])))
            return c

        lax.fori_loop(0, N_EXPERTS, body, 0, unroll=RUN_UNROLL)

    def zero_fill(act):
        def body(e, c):
            d0 = zoff_ref[e]
            _run_pieces(zn_ref[e], zgroups, lambda done, size: act(pltpu.make_async_copy(
                zbuf.at[pl.ds(0, size)], x_ref.at[pl.ds(d0 + done, size)], sems.at[2])))
            return c

        lax.fori_loop(0, N_EXPERTS, body, 0)

        def tail(i, c):
            act(pltpu.make_async_copy(zbuf, x_ref.at[pl.ds(i * zgroups, zgroups)], sems.at[2]))
            return c

        lax.fori_loop(tail_ref[0], x_ref.shape[0] // zgroups, tail, 0)

    start = lambda cp: cp.start()
    wait = lambda cp: cp.wait()

    def wait_tile(t):
        slot = t % 2
        _run_pieces(trows_ref[t], tile_groups, lambda done, size: pltpu.make_async_copy(
            buf.at[slot, pl.ds(0, size)], x_ref.at[pl.ds(0, size)], sems.at[slot]).wait())

    @pl.when(j == 0)
    def _():
        zbuf[...] = jnp.zeros(zbuf.shape, zbuf.dtype)
        zero_fill(start)

    @pl.when(j >= 2)
    def _():
        wait_tile(j - 2)

    perm = _onehot_bands(posw_ref[...].T[:TOP_K], None)
    xs = jnp.dot(perm, h2_ref[...], preferred_element_type=F32)
    buf[j % 2] = _pack_exact(xs[:, :half], xs[:, half:]).reshape(tile_groups, RUN_ALIGN, half)
    tile_runs(j, start)

    @pl.when(j == nt - 1)
    def _():
        if nt >= 2:
            wait_tile(j - 1)
        wait_tile(j)
        zero_fill(wait)


def _dispatch(h2, posw, n_rows, tile_off, expert_off, run_n, tile_rows, zoff, zn, tail, tm):
    T = h2.shape[0]
    nt = T // tm
    G = RUN_ALIGN
    assert n_rows % EXPERT_BLOCK == 0 and EXPERT_BLOCK % G == 0 and TILE_ROWS % G == 0
    x_rows = pl.pallas_call(
        functools.partial(_dispatch_kernel, nt=nt),
        grid_spec=pltpu.PrefetchScalarGridSpec(
            num_scalar_prefetch=7,
            grid=(nt,),
            in_specs=[pl.BlockSpec((tm, D_MODEL), lambda i, *_: (i, 0)),
                      pl.BlockSpec((tm, LANES), lambda i, *_: (i, 0))],
            out_specs=pl.BlockSpec(memory_space=pl.ANY),
            scratch_shapes=[pltpu.VMEM((2, TILE_ROWS // G, G, D_MODEL // 2), jnp.uint32),
                            pltpu.VMEM((EXPERT_BLOCK // G, G, D_MODEL // 2), jnp.uint32),
                            pltpu.SemaphoreType.DMA((3,))]),
        out_shape=jax.ShapeDtypeStruct((n_rows // G, G, D_MODEL // 2), jnp.uint32),
        compiler_params=_params(("arbitrary",), VMEM_LIMIT),
        name="dispatch",
    )(tile_off, expert_off, run_n, tile_rows, zoff, zn, tail, h2, posw)
    return x_rows.reshape(n_rows, D_MODEL // 2)


def _expert_kernel(be_ref, rows_ref, slot_ref, next_ref, x_ref, wg_ref, bg_ref, wu_ref, bu_ref, wd_ref, bd_ref, y_ref,
                   w_in, wg_s, wu_s, wd_s, sems):
    i = pl.program_id(0)
    half = D_MODEL // 2
    M = x_ref.shape[0]
    rows = rows_ref[i]
    e = be_ref[i]
    prev = be_ref[jnp.maximum(i - 1, 0)]

    def weight_copies(expert, slot):
        return [pltpu.make_async_copy(w_hbm.at[expert], w_in.at[slot, k], sems.at[slot, k])
                for k, w_hbm in enumerate((wg_ref, wu_ref, wd_ref))]

    @pl.when((rows > 0) & ((i == 0) | (e != prev)))
    def _():
        slot = slot_ref[i]

        @pl.when(i == 0)
        def _():
            for cp in weight_copies(e, slot):
                cp.start()

        nxt = next_ref[i]

        @pl.when(nxt >= 0)
        def _():
            for cp in weight_copies(nxt, 1 - slot):
                cp.start()

        for cp in weight_copies(e, slot):
            cp.wait()
        for k, dst in enumerate((wg_s, wu_s, wd_s)):
            dst[...] = w_in[slot, k].astype(BF16)

    def compute(base, r):
        x_lo, x_hi = _unpack(x_ref[base:base + r, :])

        def proj_in(w_s, b_ref):
            return (jnp.dot(x_lo, w_s[:half, :], preferred_element_type=F32)
                    + jnp.dot(x_hi, w_s[half:, :], preferred_element_type=F32) + b_ref[0])

        gate = jnp.minimum(proj_in(wg_s, bg_ref), SWIGLU_LIMIT)
        up = jnp.clip(proj_in(wu_s, bu_ref), -SWIGLU_LIMIT, SWIGLU_LIMIT)
        glu = gate * _sigmoid(gate * SWIGLU_ALPHA)
        act = ((up + 1.0) * glu).astype(BF16)
        y = jnp.dot(act, wd_s[...], preferred_element_type=F32) + bd_ref[0]
        y_ref[base:base + r, :] = y
        if r < EXPERT_PART:
            y_ref[base + r:base + EXPERT_PART, :] = jnp.zeros((EXPERT_PART - r, D_MODEL), y_ref.dtype)

    for base in range(0, M, EXPERT_PART):
        part_rows = jnp.clip(rows - base, 0, EXPERT_PART)
        for r in range(EXPERT_SUB, EXPERT_PART + 1, EXPERT_SUB):
            pl.when(part_rows == r)(functools.partial(compute, base, r))

        @pl.when(part_rows == 0)
        def _(base=base):
            y_ref[base:base + EXPERT_PART, :] = jnp.zeros((EXPERT_PART, D_MODEL), y_ref.dtype)


def _experts(blk_exp, blk_rows, blk_slot, blk_next, x_rows, n_pad, w_g, b_g, w_u, b_u, w_d, b_d):
    M = EXPERT_BLOCK
    assert D_FF == D_MODEL
    bspec = lambda n: pl.BlockSpec((1, 1, n), lambda i, be, *_: (be[i], 0, 0))
    wspec = pl.BlockSpec(memory_space=pl.ANY)
    return pl.pallas_call(
        _expert_kernel,
        grid_spec=pltpu.PrefetchScalarGridSpec(
            num_scalar_prefetch=4,
            grid=(n_pad // M,),
            in_specs=[pl.BlockSpec((M, D_MODEL // 2), lambda i, *_: (i, 0)),
                      wspec, bspec(D_FF), wspec, bspec(D_FF), wspec, bspec(D_MODEL)],
            out_specs=pl.BlockSpec((M, D_MODEL), lambda i, *_: (i, 0)),
            scratch_shapes=[pltpu.VMEM((2, 3, D_MODEL, D_FF), F32),
                            pltpu.VMEM((D_MODEL, D_FF), BF16),
                            pltpu.VMEM((D_MODEL, D_FF), BF16),
                            pltpu.VMEM((D_FF, D_MODEL), BF16),
                            pltpu.SemaphoreType.DMA((2, 3))]),
        out_shape=jax.ShapeDtypeStruct((n_pad, D_MODEL), F32),
        compiler_params=_params(("arbitrary",), VMEM_LIMIT),
        name="experts",
    )(blk_exp, blk_rows, blk_slot, blk_next, x_rows, w_g, b_g, w_u, b_u, w_d, b_d)


def _final_kernel(toff_ref, eoff_ref, n_ref, trows_ref, x1_ref, posw_ref, p_ref, gpg_ref, wpg_ref, wpp_ref, gpp_ref,
                  gfin_ref, y_ref, o_ref, buf, xmid, sems, *, nt):
    tm = x1_ref.shape[0]
    step = pl.program_id(0)
    j = jnp.minimum(step, nt - 1)

    def tile_runs(t, act):
        slot = t % 2

        def body(e, c):
            r = t * N_EXPERTS + e
            t0, s0 = toff_ref[r], eoff_ref[r]
            _run_pieces(n_ref[r], tm // RUN_ALIGN, lambda done, size: act(pltpu.make_async_copy(
                y_ref.at[pl.ds(s0 + done, size)], buf.at[slot, pl.ds(t0 + done, size)], sems.at[slot])))
            return c

        lax.fori_loop(0, N_EXPERTS, body, 0, unroll=RUN_UNROLL)

    @pl.when(step == 0)
    def _():
        buf[...] = jnp.zeros(buf.shape, buf.dtype)
        xmid[...] = jnp.zeros(xmid.shape, xmid.dtype)
        tile_runs(0, lambda cp: cp.start())

    @pl.when(step + 1 < nt)
    def _():
        tile_runs(step + 1, lambda cp: cp.start())

    @pl.when(step < nt)
    def _():
        _run_pieces(trows_ref[j], buf.shape[1], lambda done, size: pltpu.make_async_copy(
            y_ref.at[pl.ds(0, size)], buf.at[j % 2, pl.ds(0, size)], sems.at[j % 2]).wait())

    x = xmid[...]
    pp = jnp.dot(p_ref[...].astype(BF16), wpp_ref[...], preferred_element_type=F32)
    pg = _sigmoid(jnp.dot(_rms(x, gpg_ref[...]).astype(BF16), wpg_ref[...], preferred_element_type=F32))
    x = x + pg * _rms(pp, gpp_ref[...])
    o_ref[...] = _rms(x, gfin_ref[...])

    posw_t = posw_ref[...].T
    comb_t = _onehot_bands(posw_t[:TOP_K], posw_t[TOP_K:2 * TOP_K].astype(BF16))
    sure = TOP_K * tm
    sure_g = sure // RUN_ALIGN
    head = buf[j % 2, :sure_g].reshape(sure, D_MODEL)
    tail = buf[j % 2, sure_g:].reshape(TILE_ROWS - sure, D_MODEL)
    live = lax.broadcasted_iota(jnp.int32, tail.shape, 0) < trows_ref[j] * RUN_ALIGN - sure
    y = jnp.concatenate([head, jnp.where(live, tail, 0.0)], axis=0).astype(BF16)
    xmid[...] = x1_ref[...] + lax.dot_general(comb_t, y, TN, preferred_element_type=F32)


def _final(tile_off, expert_off, run_n, tile_rows, x1, y_rows, posw, p2, g_pg, w_pg, w_pp, g_pp, g_fin, tm):
    T = x1.shape[0]
    nt = T // tm
    full = lambda shape: pl.BlockSpec(shape, lambda i, *_: (0, 0))
    this = lambda w: pl.BlockSpec((tm, w), lambda i, *_: (jnp.minimum(i, nt - 1), 0))
    prev = lambda w: pl.BlockSpec((tm, w), lambda i, *_: (jnp.maximum(i - 1, 0), 0))
    return pl.pallas_call(
        functools.partial(_final_kernel, nt=nt),
        grid_spec=pltpu.PrefetchScalarGridSpec(
            num_scalar_prefetch=4,
            grid=(nt + 1,),
            in_specs=[this(D_MODEL), this(LANES), prev(PLE_DIM),
                      full((1, D_MODEL)), full((D_MODEL, D_MODEL)), full((PLE_DIM, D_MODEL)),
                      full((1, D_MODEL)), full((1, D_MODEL)),
                      pl.BlockSpec(memory_space=pl.ANY)],
            out_specs=prev(D_MODEL),
            scratch_shapes=[pltpu.VMEM((2, TILE_ROWS // RUN_ALIGN, RUN_ALIGN, D_MODEL), F32),
                            pltpu.VMEM((tm, D_MODEL), F32),
                            pltpu.SemaphoreType.DMA((2,))]),
        out_shape=jax.ShapeDtypeStruct((T, D_MODEL), F32),
        compiler_params=_params(("arbitrary",), VMEM_LIMIT),
        name="final",
    )(tile_off, expert_off, run_n, tile_rows, x1, posw, p2, g_pg, w_pg, w_pp, g_pp, g_fin,
      y_rows.reshape(-1, RUN_ALIGN, D_MODEL))


def _split_w_in(w_in):
    sizes = (GLA_QK, GLA_QK, GLA_V, GLA_V, GLA_RANK, MOBA_W, MOBA_W, MOBA_W, D_MODEL, D_MODEL)
    offs = [0]
    for s in sizes:
        offs.append(offs[-1] + s)
    main = jnp.concatenate([w_in[:, :offs[4]], w_in[:, offs[5]:]], axis=1).astype(BF16)
    alow = jnp.pad(w_in[:, offs[4]:offs[5]], ((0, 0), (0, LANES - GLA_RANK))).astype(BF16)
    return main, alow


def _layer(x2, p2, bias, B, S, g_mix, w_in, w_a2, b_a2, g_gla_out, w_up_gla, w_up_moba, w_o, g_ffn, w_router,
           b_router, w_e_gate, b_e_gate, w_e_up, b_e_up, w_e_down, b_e_down, g_ple_gate, w_ple_gate, w_ple_proj,
           g_ple_proj, g_final):
    T = B * S
    row = lambda v: v.reshape(1, -1).astype(F32)
    w_main, w_alow = _split_w_in(w_in)
    colscale = jnp.ones((D_PROJ,), F32)
    colscale = colscale.at[OFF_QA:OFF_QA + GLA_QK].set(GLA_DK ** -0.5)
    colscale = colscale.at[OFF_QB:OFF_QB + MOBA_W].set(MOBA_HD ** -0.5 * LOG2E)
    w_a2p = jnp.pad(w_a2, ((0, LANES - GLA_RANK), (0, 0))).astype(BF16)
    proj, glog = _inproj(x2, row(g_mix), w_main, colscale.reshape(1, -1), w_alow, w_a2p, row(b_a2))

    o_a = _gla(proj, glog, row(g_gla_out), B, S)
    o_b = _moba(proj, bias, B, S)

    w_r = jnp.pad(w_router.astype(F32), ((0, 0), (0, LANES - N_EXPERTS)))
    w_r_hi = w_r.astype(BF16)
    w_r2 = jnp.concatenate([w_r_hi, (w_r - w_r_hi.astype(F32)).astype(BF16)], axis=1)
    b_r = jnp.pad(b_router.astype(F32), (0, LANES - N_EXPERTS)).reshape(1, -1)
    tm = TOKEN_TILE
    nt = T // tm
    x1, h2, logits = _merge(o_a, o_b, proj, x2, w_up_gla.astype(BF16), w_up_moba.astype(BF16),
                            w_o.astype(BF16), row(g_ffn), w_r2, b_r, MERGE_TILE)
    posw, cnt_t, carry = _router(logits, tm)

    M = EXPERT_BLOCK
    A = nt * TILE_ROWS
    n_pad = (-(-A // M)) * M + N_EXPERTS * M
    n_blk = n_pad // M
    cnt_t = cnt_t[:, 0, :N_EXPERTS].astype(jnp.int32)
    carry = carry[:, 0, :N_EXPERTS].astype(jnp.int32)
    counts = carry[-1] + cnt_t[-1]
    padded = (counts + M - 1) // M * M
    pad_end = jnp.cumsum(padded)
    pad_start = pad_end - padded
    blk_exp = jnp.minimum(jnp.sum(pad_end[None, :] <= (jnp.arange(n_blk, dtype=jnp.int32) * M)[:, None], axis=1),
                          N_EXPERTS - 1).astype(jnp.int32)
    n_used = (pad_end[-1:] // M).astype(jnp.int32)
    blk_start = jnp.arange(n_blk, dtype=jnp.int32) * M
    eids = jnp.arange(N_EXPERTS, dtype=jnp.int32)

    def per_block(per_expert):
        return jnp.sum(jnp.where(blk_exp[:, None] == eids[None, :], per_expert[None, :], 0), axis=1).astype(jnp.int32)

    blk_rows = jnp.clip(per_block(pad_start + counts) - blk_start, 0, M)
    blk_rows = jnp.where(blk_start < pad_end[-1], (blk_rows + EXPERT_SUB - 1) // EXPERT_SUB * EXPERT_SUB, 0)
    has_rows = counts > 0
    blk_slot = per_block((jnp.cumsum(has_rows) - 1) % 2)
    later = jnp.where((eids[None, :] > eids[:, None]) & has_rows[None, :], eids[None, :], N_EXPERTS)
    next_exp = jnp.min(later, axis=1)
    blk_next = per_block(jnp.where(next_exp < N_EXPERTS, next_exp, -1))
    G = RUN_ALIGN
    tile_off = (jnp.cumsum(cnt_t, axis=1) - cnt_t).reshape(-1) // G
    expert_off = (carry + pad_start[None, :]).reshape(-1) // G
    run_n = cnt_t.reshape(-1) // G
    tile_rows = jnp.sum(cnt_t, axis=1) // G
    x_rows = _dispatch(h2, posw, n_pad, tile_off, expert_off, run_n, tile_rows, (pad_start + counts) // G,
                       (padded - counts) // G, n_used, tm)
    y_rows = _experts(blk_exp, blk_rows.astype(jnp.int32), blk_slot, blk_next, x_rows, n_pad,
                      w_e_gate, b_e_gate.reshape(N_EXPERTS, 1, D_FF),
                      w_e_up, b_e_up.reshape(N_EXPERTS, 1, D_FF), w_e_down,
                      b_e_down.reshape(N_EXPERTS, 1, D_MODEL))
    return _final(tile_off, expert_off, run_n, tile_rows, x1, y_rows, posw, p2, row(g_ple_gate),
                  w_ple_gate.astype(BF16), w_ple_proj.astype(BF16), row(g_ple_proj), row(g_final), tm)


def kernel(x, p, rel_bias, g_mix, w_in, w_a2, b_a2, g_gla_out, w_up_gla, w_up_moba, w_o, g_ffn, w_router, b_router,
           w_e_gate, b_e_gate, w_e_up, b_e_up, w_e_down, b_e_down, g_ple_gate, w_ple_gate, w_ple_proj, g_ple_proj,
           g_final):
    B, S, D = x.shape
    assert D == D_MODEL and S % MOBA_BLOCK == 0 and S % GLA_CHUNK == 0 and p.shape[0] == 1
    bias = _bias_tiles(rel_bias, S // MOBA_BLOCK)
    out = _layer(x.reshape(B * S, D), p[0].reshape(B * S, PLE_DIM), bias, B, S,
                 g_mix[0], w_in[0], w_a2[0], b_a2[0], g_gla_out[0], w_up_gla[0], w_up_moba[0], w_o[0], g_ffn[0],
                 w_router[0], b_router[0], w_e_gate[0], b_e_gate[0], w_e_up[0], b_e_up[0], w_e_down[0],
                 b_e_down[0], g_ple_gate[0], w_ple_gate[0], w_ple_proj[0], g_ple_proj[0], g_final)
    return out.reshape(B, S, D)
```

```python
import functools
import math

import jax
import jax.numpy as jnp
from jax import lax
from jax.experimental import pallas as pl
from jax.experimental.pallas import tpu as pltpu

F32 = jnp.float32
BF16 = jnp.bfloat16

D_MODEL = 1024
PLE_DIM = 256
GLA_HEADS = 4
GLA_DK = 128
GLA_DV = 256
GLA_RANK = 16
GLA_TAU = 16.0
GLA_QK = GLA_HEADS * GLA_DK
GLA_V = GLA_HEADS * GLA_DV
MOBA_HEADS = 8
MOBA_HD = 128
MOBA_BLOCK = 256
MOBA_TOPK = 3
MOBA_W = MOBA_HEADS * MOBA_HD
REL_BUCKETS = 32
REL_MAX_DIST = 4096
N_EXPERTS = 32
TOP_K = 4
D_FF = 1024
SWIGLU_LIMIT = 7.0
SWIGLU_ALPHA = 1.702
EPS = 1e-6

LANES = 128
NEG = -1e30
LOG2E = math.log2(math.e)
VMEM_LIMIT = 56 * 1024 * 1024

OFF_QA, OFF_KA, OFF_VA, OFF_RA = 0, 512, 1024, 2048
OFF_QB, OFF_KB, OFF_VB, OFF_GA, OFF_GB = 3072, 4096, 5120, 6144, 7168
D_PROJ = 8192

GLA_CHUNK = 128
EXPERT_BLOCK = 512
EXPERT_SUB = 128
TOKEN_TILE = 256
RUN_ALIGN = 8
TILE_ROWS = TOP_K * TOKEN_TILE + N_EXPERTS * RUN_ALIGN
RUN_UNROLL = 8
MERGE_TILE = 512

NT = (((1,), (1,)), ((), ()))
TN = (((0,), (0,)), ((), ()))


def _params(sem, vmem=None):
    return pltpu.CompilerParams(dimension_semantics=sem, vmem_limit_bytes=vmem)


def _rms(x, g):
    return x * lax.rsqrt(jnp.mean(x * x, axis=-1, keepdims=True) + EPS) * g


def _sigmoid(x):
    return 1.0 / (1.0 + jnp.exp(-x))


def _bucket_of(n):
    max_exact = REL_BUCKETS // 2
    if n < max_exact:
        return n
    return min(max_exact + int(math.log(n / max_exact) / math.log(REL_MAX_DIST / max_exact)
                               * (REL_BUCKETS - max_exact)), REL_BUCKETS - 1)


def _bias_kernel(tab_ref, bkt_ref, o_ref, *, nblk):
    strip = 8
    m = pl.program_id(0)

    def fill(lo, hi):
        def body(s, carry):
            r0 = pl.multiple_of(s * strip, strip)
            b = bkt_ref[0, pl.ds(r0, strip), :]
            accs = [jnp.zeros(b.shape, F32) for _ in range(MOBA_HEADS)]
            for bb in range(lo, hi + 1):
                hit = b == bb
                for h in range(MOBA_HEADS):
                    accs[h] = jnp.where(hit, tab_ref[h, bb], accs[h])
            for h in range(MOBA_HEADS):
                o_ref[h, 0, pl.ds(r0, strip), :] = jnp.where(b < 0, NEG, accs[h])
            return carry

        lax.fori_loop(0, MOBA_BLOCK // strip, body, 0)

    for mm in range(nblk):
        lo = max(_bucket_of(max(mm * MOBA_BLOCK - (MOBA_BLOCK - 1), 0)) - 1, 0)
        hi = min(_bucket_of(mm * MOBA_BLOCK + MOBA_BLOCK - 1) + 1, REL_BUCKETS - 1)
        pl.when(m == mm)(functools.partial(fill, lo, hi))


def _bias_tiles(rel_bias, nblk):
    i = jnp.arange(MOBA_BLOCK, dtype=jnp.int32)
    dist = (jnp.arange(nblk, dtype=jnp.int32)[:, None, None] * MOBA_BLOCK + i[None, :, None] - i[None, None, :])
    n = jnp.maximum(dist, 0)
    max_exact = REL_BUCKETS // 2
    nf = jnp.maximum(n, 1).astype(F32)
    large = max_exact + (jnp.log(nf / max_exact) / math.log(REL_MAX_DIST / max_exact)
                         * (REL_BUCKETS - max_exact)).astype(jnp.int32)
    large = jnp.minimum(large, REL_BUCKETS - 1)
    bkt = jnp.where(dist < 0, -1, jnp.where(n < max_exact, n, large)).astype(jnp.int32)
    tab = rel_bias.astype(F32).T * LOG2E
    return pl.pallas_call(
        functools.partial(_bias_kernel, nblk=nblk),
        grid=(nblk,),
        in_specs=[pl.BlockSpec(memory_space=pltpu.SMEM),
                  pl.BlockSpec((1, MOBA_BLOCK, MOBA_BLOCK), lambda m: (m, 0, 0))],
        out_specs=pl.BlockSpec((MOBA_HEADS, 1, MOBA_BLOCK, MOBA_BLOCK), lambda m: (0, m, 0, 0)),
        out_shape=jax.ShapeDtypeStruct((MOBA_HEADS, nblk, MOBA_BLOCK, MOBA_BLOCK), F32),
        compiler_params=_params(("arbitrary",)),
        name="bias_tiles",
    )(tab, bkt)


def _inproj_kernel(x0_ref, xn_ref, g_ref, w_ref, cs_ref, wal_ref, wa2_ref, ba2_ref, o_ref, glog_ref, h_scr, al_scr,
                   *, n_j):
    i, j = pl.program_id(0), pl.program_id(1)
    tm = xn_ref.shape[0]
    rs = tm // n_j

    def gate_logits(a_low):
        a = jnp.dot(a_low, wa2_ref[...], preferred_element_type=F32) + ba2_ref[...]
        log_sig = jnp.minimum(a, 0.0) - jnp.log1p(jnp.exp(-jnp.abs(a)))
        return log_sig * (1.0 / GLA_TAU)

    def slice_rows(behind):
        return pl.multiple_of(((j + n_j - behind) % n_j) * rs, rs)

    @pl.when((i == 0) & (j == 0))
    def _():
        h0 = _rms(x0_ref[...], g_ref[...]).astype(BF16)
        h_scr[0] = h0
        a_low0 = jnp.dot(h0, wal_ref[...], preferred_element_type=F32).astype(BF16)
        glog_ref[...] = gate_logits(a_low0)
        al_scr[...] = a_low0[(n_j - 2) * rs:(n_j - 1) * rs, :]

    glog_ref[pl.ds(slice_rows(2), rs), :] = gate_logits(al_scr[...])
    slot = jnp.where(j == 0, i, i + 1) % 2
    h_lag = h_scr[slot, pl.ds(slice_rows(1), rs), :]
    al_scr[...] = jnp.dot(h_lag, wal_ref[...], preferred_element_type=F32).astype(BF16)

    acc = jnp.dot(h_scr[i % 2], w_ref[...], preferred_element_type=F32)
    o_ref[...] = (acc * cs_ref[...]).astype(BF16)

    h_scr[(i + 1) % 2, pl.ds(slice_rows(0), rs), :] = _rms(xn_ref[pl.ds(slice_rows(0), rs), :],
                                                             g_ref[...]).astype(BF16)


def _inproj(x2, g_mix, w_main, colscale, w_alow, w_a2p, b_a2, tm=1024, tn=2048):
    T = x2.shape[0]
    n_i, n_j = T // tm, D_PROJ // tn
    assert n_i >= 2 and tm % (8 * n_j) == 0
    return pl.pallas_call(
        functools.partial(_inproj_kernel, n_j=n_j),
        grid=(n_i, n_j),
        in_specs=[pl.BlockSpec((tm, D_MODEL), lambda i, j: (0, 0)),
                  pl.BlockSpec((tm, D_MODEL), lambda i, j: (jnp.minimum(i + 1, n_i - 1), 0)),
                  pl.BlockSpec((1, D_MODEL), lambda i, j: (0, 0)),
                  pl.BlockSpec((D_MODEL, tn), lambda i, j: (0, j)),
                  pl.BlockSpec((1, tn), lambda i, j: (0, j)),
                  pl.BlockSpec((D_MODEL, LANES), lambda i, j: (0, 0)),
                  pl.BlockSpec((LANES, GLA_QK), lambda i, j: (0, 0)),
                  pl.BlockSpec((1, GLA_QK), lambda i, j: (0, 0))],
        out_specs=[pl.BlockSpec((tm, tn), lambda i, j: (i, j)),
                   pl.BlockSpec((tm, GLA_QK), lambda i, j: (jnp.minimum(i + jnp.minimum(j // 2, 1), n_i - 1), 0))],
        out_shape=[jax.ShapeDtypeStruct((T, D_PROJ), BF16),
                   jax.ShapeDtypeStruct((T, GLA_QK), F32)],
        scratch_shapes=[pltpu.VMEM((2, tm, D_MODEL), BF16), pltpu.VMEM((tm // n_j, LANES), BF16)],
        compiler_params=_params(("arbitrary", "arbitrary"), VMEM_LIMIT),
        name="inproj",
    )(x2, x2, g_mix, w_main, colscale, w_alow, w_a2p, b_a2)


def _gla_kernel(q_ref, k_ref, v_ref, r_ref, g_ref, gout_ref, o_ref, st_ref):
    C = GLA_CHUNK

    @pl.when(pl.program_id(1) == 0)
    def _():
        st_ref[...] = jnp.zeros(st_ref.shape, F32)

    nb = q_ref.shape[0]
    row = lax.broadcasted_iota(jnp.int32, (C, C), 0)
    col = lax.broadcasted_iota(jnp.int32, (C, C), 1)
    causal = col <= row
    ltri = causal.astype(BF16)
    mid = C // 2
    pairs = [(b, h) for b in range(nb) for h in range(GLA_HEADS)]
    ks = lambda h: slice(h * GLA_DK, (h + 1) * GLA_DK)
    vs = lambda h: slice(h * GLA_DV, (h + 1) * GLA_DV)
    gate = {}
    for b, h in pairs:
        r = r_ref[b, :, vs(h)].astype(F32)
        gate[b, h] = r * _sigmoid(r)
    G = []
    for b in range(nb):
        g = g_ref[b]
        g_hi = g.astype(BF16)
        g_lo = (g - g_hi.astype(F32)).astype(BF16)
        G.append(jnp.dot(ltri, g_hi, preferred_element_type=F32) + jnp.dot(ltri, g_lo, preferred_element_type=F32))
    Gh = {(b, h): G[b][:, ks(h)] for b, h in pairs}
    qh = {(b, h): q_ref[b, :, ks(h)].astype(F32) for b, h in pairs}
    kh = {(b, h): k_ref[b, :, ks(h)].astype(F32) for b, h in pairs}
    g_mid = {p: Gh[p][mid:mid + 1, :] for p in pairs}
    g_last = {p: Gh[p][C - 1:C, :] for p in pairs}
    A = {p: lax.dot_general((qh[p] * jnp.exp(Gh[p] - g_mid[p])).astype(BF16),
                            (kh[p] * jnp.exp(g_mid[p] - Gh[p])).astype(BF16), NT, preferred_element_type=F32)
         for p in pairs}
    st = {p: st_ref[p[0], p[1]] for p in pairs}
    inter = {p: lax.dot_general((qh[p] * jnp.exp(Gh[p])).astype(BF16), st[p].astype(BF16), NT,
                                preferred_element_type=F32) for p in pairs}
    for b, h in pairs:
        p = (b, h)
        k_d = (kh[p] * jnp.exp(g_last[p] - Gh[p])).astype(BF16)
        st_ref[b, h] = jnp.exp(g_last[p]) * st[p] + lax.dot_general(v_ref[b, :, vs(h)], k_d, TN,
                                                                     preferred_element_type=F32)
    intra = {(b, h): jnp.dot(jnp.where(causal, A[b, h], 0.0).astype(BF16), v_ref[b, :, vs(h)],
                             preferred_element_type=F32) for b, h in pairs}
    for b, h in pairs:
        o = inter[b, h] + intra[b, h]
        o_ref[b, :, vs(h)] = (_rms(o, gout_ref[...]) * gate[b, h]).astype(BF16)


def _gla(proj, glog, g_gla_out, B, S, nb=4):
    C = GLA_CHUNK
    assert B % nb == 0
    proj3 = proj.reshape(B, S, D_PROJ)
    glog3 = glog.reshape(B, S, GLA_QK)
    spec = lambda w, off: pl.BlockSpec((nb, C, w), lambda b, c: (b, c, off // w))
    out = pl.pallas_call(
        _gla_kernel,
        grid=(B // nb, S // C),
        in_specs=[spec(GLA_QK, OFF_QA), spec(GLA_QK, OFF_KA), spec(GLA_V, OFF_VA), spec(GLA_V, OFF_RA),
                  spec(GLA_QK, 0), pl.BlockSpec((1, GLA_DV), lambda b, c: (0, 0))],
        out_specs=spec(GLA_V, 0),
        out_shape=jax.ShapeDtypeStruct((B, S, GLA_V), BF16),
        scratch_shapes=[pltpu.VMEM((nb, GLA_HEADS, GLA_DV, GLA_DK), F32)],
        compiler_params=_params(("arbitrary", "arbitrary")),
        name="gla",
    )(proj3, proj3, proj3, proj3, glog3, g_gla_out)
    return out.reshape(B * S, GLA_V)


def _moba_kernel(q_ref, k_ref, v_ref, bias_ref, o_ref, ka_scr, va_scr, lg_scr, mx_scr, qa_scr, *, nblk):
    BLK, HD = MOBA_BLOCK, MOBA_HD
    S = nblk * BLK
    units = range(q_ref.shape[0])

    @pl.when((pl.program_id(0) == 0) & (pl.program_id(1) == 0))
    def _():
        blk = lax.broadcasted_iota(jnp.int32, (S, HD), 0) // BLK
        lane = lax.broadcasted_iota(jnp.int32, (S, HD), 1)
        for u in units:
            ka_scr[u, :, HD:] = (lane == blk).astype(BF16)
            va_scr[u, :, HD:] = (lane == 0).astype(BF16)

    n_plain = min(MOBA_TOPK + 1, nblk)
    n_late = nblk - n_plain
    n_tiles = nblk * (nblk + 1) // 2

    def tile_id(cc, j):
        return cc * (cc + 1) // 2 + j

    def rows(cc):
        return slice(cc * BLK, (cc + 1) * BLK)

    def stage1(u, cc, q_in, keys):
        mx = None
        for j in range(cc + 1):
            lg = lax.dot_general(q_in, keys(j), NT, preferred_element_type=F32) + bias_ref[0, cc - j]
            lg_scr[u * n_tiles + tile_id(cc, j)] = lg
            t = jnp.maximum(lg[:, :LANES], lg[:, LANES:])
            mx = t if mx is None else jnp.maximum(mx, t)
            if j == cc:
                mx_scr[u * nblk + cc] = mx
            yield

    def stage2(u, cc):
        m = mx_scr[u * nblk + cc].max(axis=-1, keepdims=True)
        acc = jnp.zeros((BLK, 2 * HD), F32)
        for j in range(cc + 1):
            p = jnp.exp2(lg_scr[u * n_tiles + tile_id(cc, j)] - m).astype(BF16)
            acc = acc + jnp.dot(p, va_scr[u, rows(j), :], preferred_element_type=F32)
            if j == cc:
                o_ref[u, rows(cc), :] = (acc[:, :HD] / acc[:, HD:HD + 1]).astype(BF16)
            yield

    def drain(gen):
        for _ in gen:
            pass

    def interleave(main, side, n_main, n_side):
        side_steps = (s for g in side for s in g)
        done = 0
        for i, _ in enumerate(s for g in main for s in g):
            assert n_main > 0
            want = (i + 1) * n_side // n_main
            while done < want and next(side_steps, "end") != "end":
                done += 1
        drain(side_steps)

    plain = {u: [stage1(u, cc, q_ref[u, rows(cc), :], lambda j, u=u: k_ref[u, rows(j), :]) for cc in range(n_plain)]
             for u in units}
    if n_late:
        pens = []
        for u in units:
            next(plain[u][0])
            ksum = [k_ref[u, rows(j), :].astype(F32).reshape(BLK // 8, 8, HD).sum(axis=0).sum(axis=0, keepdims=True)
                    for j in range(nblk)]
            kmean = jnp.concatenate(ksum, axis=0) * (1.0 / BLK)
            km_hi = kmean.astype(BF16)
            km_lo = (kmean - km_hi.astype(F32)).astype(BF16)
            km2 = jnp.concatenate([km_hi, km_lo], axis=0)
            for cc in range(n_plain, nblk):
                q = q_ref[u, rows(cc), :]
                s2 = lax.dot_general(km2, q, NT, preferred_element_type=F32)
                pens.append((u, cc, q, s2[:nblk] + s2[nblk:]))
        for u in units:
            for g in plain[u][:2]:
                drain(g)
        for u, cc, q, s in pens:
            ji = lax.broadcasted_iota(jnp.int32, s.shape, 0)
            cnt = jnp.zeros(s.shape, F32)
            for jp in range(cc):
                sj = s[jp:jp + 1, :]
                beats = (sj > s) | ((sj == s) & (jp < ji))
                cnt = cnt + beats.astype(F32)
            pen = jnp.where((ji < cc) & (cnt >= MOBA_TOPK), NEG, 0.0)
            pen_t = jnp.concatenate([pen, jnp.zeros((HD - nblk, BLK), F32)], axis=0).T
            qa_scr[u * n_late + cc - n_plain] = jnp.concatenate([q, pen_t.astype(BF16)], axis=1)
    for u in units:
        ka_scr[u, :, :HD] = k_ref[u]
        va_scr[u, :, :HD] = v_ref[u]
    for u in units:
        for g in plain[u]:
            drain(g)

    @pl.when(pl.program_id(0) >= 0)
    def _():
        late = [stage1(u, cc, qa_scr[u * n_late + cc - n_plain], lambda j, u=u: ka_scr[u, rows(j), :])
                for u in units for cc in range(n_plain, nblk)]
        early = [stage2(u, cc) for u in units for cc in range(n_plain)]
        if late:
            interleave(late, early, len(units) * (n_tiles - tile_id(n_plain, 0)), len(units) * tile_id(n_plain, 0))
        else:
            for g in early:
                drain(g)

    @pl.when(pl.program_id(0) >= 0)
    def _():
        for u in units:
            for cc in range(n_plain, nblk):
                drain(stage2(u, cc))


def _moba(proj, bias, B, S, nb=2):
    BLK = MOBA_BLOCK
    nblk = S // BLK
    H = MOBA_HEADS
    assert nblk <= MOBA_HD and B % nb == 0
    proj3 = proj.reshape(B, S, D_PROJ)
    spec = lambda off: pl.BlockSpec((nb, S, MOBA_HD), lambda h, b: (b, 0, off // MOBA_HD + h))
    out = pl.pallas_call(
        functools.partial(_moba_kernel, nblk=nblk),
        grid=(H, B // nb),
        in_specs=[spec(OFF_QB), spec(OFF_KB), spec(OFF_VB),
                  pl.BlockSpec((1, nblk, BLK, BLK), lambda h, b: (h, 0, 0, 0))],
        out_specs=spec(0),
        out_shape=jax.ShapeDtypeStruct((B, S, MOBA_W), BF16),
        scratch_shapes=[pltpu.VMEM((nb, S, 2 * MOBA_HD), BF16), pltpu.VMEM((nb, S, 2 * MOBA_HD), BF16),
                        pltpu.VMEM((nb * nblk * (nblk + 1) // 2, BLK, BLK), F32),
                        pltpu.VMEM((nb * nblk, BLK, LANES), F32),
                        pltpu.VMEM((nb * max(nblk - MOBA_TOPK - 1, 1), BLK, 2 * MOBA_HD), BF16)],
        compiler_params=_params(("arbitrary", "arbitrary"), VMEM_LIMIT),
        name="moba",
    )(proj3, proj3, proj3, bias)
    return out.reshape(B * S, MOBA_W)


def _pack_exact(lo, hi):
    lo_b = lax.bitcast_convert_type(lo, jnp.uint32)
    hi_b = lax.bitcast_convert_type(hi, jnp.uint32)
    return (lo_b >> 16) | (hi_b & jnp.uint32(0xFFFF0000))


def _unpack(w):
    lo = lax.bitcast_convert_type(w << 16, F32)
    hi = lax.bitcast_convert_type(w & jnp.uint32(0xFFFF0000), F32)
    return lo.astype(BF16), hi.astype(BF16)


def _merge_kernel(oa_ref, ob_ref, ga_ref, gb_ref, x_ref, wua_ref, wub_ref, wo_ref, gffn_ref, wr_ref, br_ref,
                  x1_ref, h2_ref, lg_ref):
    tm = x_ref.shape[0]
    u_a = jnp.dot(oa_ref[...], wua_ref[...], preferred_element_type=F32)
    u_b = jnp.dot(ob_ref[...], wub_ref[...], preferred_element_type=F32)
    y = _sigmoid(ga_ref[...].astype(F32)) * u_a + _sigmoid(gb_ref[...].astype(F32)) * u_b
    x1 = x_ref[...] + jnp.dot(y.astype(BF16), wo_ref[...], preferred_element_type=F32)
    x1_ref[...] = x1
    h2 = _rms(x1, gffn_ref[...])
    h_hi = h2.astype(BF16)
    h2_ref[...] = h_hi
    h_lo = (h2 - h_hi.astype(F32)).astype(BF16)
    r = jnp.dot(jnp.concatenate([h_hi, h_lo], axis=0), wr_ref[...], preferred_element_type=F32)
    lg_ref[...] = r[:tm, :LANES] + r[:tm, LANES:] + r[tm:, :LANES] + br_ref[...]


def _merge(o_a, o_b, proj, x2, w_ua, w_ub, w_o, g_ffn, w_r2, b_r, tm):
    T = x2.shape[0]
    full = lambda shape: pl.BlockSpec(shape, lambda i: (0, 0))
    rowblk = lambda w: pl.BlockSpec((tm, w), lambda i: (i, 0))
    return pl.pallas_call(
        _merge_kernel,
        grid=(T // tm,),
        in_specs=[rowblk(GLA_V), rowblk(MOBA_W),
                  pl.BlockSpec((tm, D_MODEL), lambda i: (i, OFF_GA // D_MODEL)),
                  pl.BlockSpec((tm, D_MODEL), lambda i: (i, OFF_GB // D_MODEL)),
                  rowblk(D_MODEL),
                  full((GLA_V, D_MODEL)), full((MOBA_W, D_MODEL)), full((D_MODEL, D_MODEL)),
                  full((1, D_MODEL)), full((D_MODEL, 2 * LANES)), full((1, LANES))],
        out_specs=[rowblk(D_MODEL), rowblk(D_MODEL), rowblk(LANES)],
        out_shape=[jax.ShapeDtypeStruct((T, D_MODEL), F32),
                   jax.ShapeDtypeStruct((T, D_MODEL), BF16),
                   jax.ShapeDtypeStruct((T, LANES), F32)],
        compiler_params=_params(("arbitrary",), VMEM_LIMIT),
        name="merge",
    )(o_a, o_b, proj, proj, x2, w_ua, w_ub, w_o, g_ffn, w_r2, b_r)


def _router_kernel(lg_ref, posw_ref, cnt_ref, carry_ref, cnt_scr, *, tm):
    rows = lg_ref.shape[0]

    @pl.when(pl.program_id(0) == 0)
    def _():
        cnt_scr[...] = jnp.zeros(cnt_scr.shape, F32)

    lane = lax.broadcasted_iota(jnp.int32, (rows, LANES), 1)
    lane_f = lane.astype(F32)
    work = jnp.where(lane < N_EXPERTS, lg_ref[...], NEG)
    vals, hots = [], []
    for _ in range(TOP_K):
        mx = work.max(axis=-1, keepdims=True)
        idx = jnp.min(jnp.where(work == mx, lane_f, float(LANES)), axis=-1, keepdims=True)
        hot = lane_f == idx
        vals.append(mx)
        hots.append(hot)
        work = jnp.where(hot, 2.0 * NEG, work)
    exps = [jnp.exp(v - vals[0]) for v in vals]
    den = exps[0] + exps[1] + exps[2] + exps[3]
    sel = jnp.zeros((rows, LANES), F32)
    for hot in hots:
        sel = sel + hot.astype(F32)
    row = lax.broadcasted_iota(jnp.int32, (tm, tm), 0)
    col = lax.broadcasted_iota(jnp.int32, (tm, tm), 1)
    below = (col < row).astype(BF16)
    er = lax.broadcasted_iota(jnp.int32, (LANES, LANES), 0)
    ec = lax.broadcasted_iota(jnp.int32, (LANES, LANES), 1)
    before = (er < ec).astype(F32)
    pos_parts = []
    for t in range(rows // tm):
        sel_t = sel[t * tm:(t + 1) * tm]
        local_rank = jnp.dot(below, sel_t.astype(BF16), preferred_element_type=F32)
        cnt_t = sel_t.sum(axis=0, keepdims=True)
        cnt_t = jnp.floor((cnt_t + (RUN_ALIGN - 1.0)) * (1.0 / RUN_ALIGN)) * RUN_ALIGN
        tile_off = jnp.dot(jnp.broadcast_to(cnt_t, (8, LANES)), before, preferred_element_type=F32,
                           precision=lax.Precision.HIGHEST)[0:1]
        pos_parts.append(local_rank + tile_off)
        carry_ref[t] = cnt_scr[...]
        cnt_ref[t] = cnt_t
        cnt_scr[...] = cnt_scr[...] + cnt_t
    pos_all = jnp.concatenate(pos_parts, axis=0)
    posw = jnp.zeros((rows, LANES), F32)
    for k in range(TOP_K):
        pk = jnp.sum(jnp.where(hots[k], pos_all, 0.0), axis=-1, keepdims=True)
        posw = jnp.where(lane == k, pk, posw)
        posw = jnp.where(lane == TOP_K + k, exps[k] / den, posw)
    posw_ref[...] = posw


def _router(logits, tm, tiles_per_step=4):
    T = logits.shape[0]
    nt = T // tm
    rows = tm * tiles_per_step
    tilerow = pl.BlockSpec((tiles_per_step, 1, LANES), lambda i: (i, 0, 0))
    return pl.pallas_call(
        functools.partial(_router_kernel, tm=tm),
        grid=(T // rows,),
        in_specs=[pl.BlockSpec((rows, LANES), lambda i: (i, 0))],
        out_specs=[pl.BlockSpec((rows, LANES), lambda i: (i, 0)), tilerow, tilerow],
        out_shape=[jax.ShapeDtypeStruct((T, LANES), F32),
                   jax.ShapeDtypeStruct((nt, 1, LANES), F32),
                   jax.ShapeDtypeStruct((nt, 1, LANES), F32)],
        scratch_shapes=[pltpu.VMEM((1, LANES), F32)],
        compiler_params=_params(("arbitrary",)),
        name="router",
    )(logits)


def _run_pieces(n, max_n, fn):
    for b in reversed(range(max_n.bit_length())):
        size = 1 << b
        done = n & ~((2 << b) - 1)

        @pl.when((n & size) != 0)
        def _():
            fn(done, size)


def _onehot_bands(pos, val):
    band = 256
    assert TILE_ROWS % band == 0
    n_k, n_tok = pos.shape
    pos_a = jnp.floor(pos * (1.0 / band))
    pos_b = pos - band * pos_a
    sub = lax.broadcasted_iota(jnp.int32, (band, n_tok), 0).astype(F32).astype(BF16)
    zero = jnp.zeros((band, n_tok), BF16)
    bands = []
    for a in range(TILE_ROWS // band):
        want = jnp.where(pos_a == a, pos_b, -1.0).astype(BF16)
        hit = zero
        for k in range(n_k):
            row = jnp.broadcast_to(want[k:k + 1, :], (band, n_tok))
            fill = jnp.ones((band, n_tok), BF16) if val is None else jnp.broadcast_to(val[k:k + 1, :], (band, n_tok))
            hit = hit + jnp.where(sub == row, fill, zero)
        bands.append(hit)
    return jnp.concatenate(bands, axis=0)


def _dispatch_kernel(toff_ref, eoff_ref, n_ref, trows_ref, zoff_ref, zn_ref, tail_ref, h2_ref, posw_ref, x_ref,
                     buf, zbuf, sems, *, nt):
    tm = h2_ref.shape[0]
    half = D_MODEL // 2
    j = pl.program_id(0)
    zgroups = zbuf.shape[0]
    tile_groups = buf.shape[1]

    def tile_runs(t, act):
        slot = t % 2

        def body(e, c):
            r = t * N_EXPERTS + e
            t0, d0 = toff_ref[r], eoff_ref[r]
            _run_pieces(n_ref[r], tm // RUN_ALIGN, lambda done, size: act(pltpu.make_async_copy(
                buf.at[slot, pl.ds(t0 + done, size)], x_ref.at[pl.ds(d0 + done, size)], sems.at[slot])))
            return c

        lax.fori_loop(0, N_EXPERTS, body, 0, unroll=RUN_UNROLL)

    def zero_fill(act):
        def body(e, c):
            d0 = zoff_ref[e]
            _run_pieces(zn_ref[e], zgroups, lambda done, size: act(pltpu.make_async_copy(
                zbuf.at[pl.ds(0, size)], x_ref.at[pl.ds(d0 + done, size)], sems.at[2])))
            return c

        lax.fori_loop(0, N_EXPERTS, body, 0)

        def tail(i, c):
            act(pltpu.make_async_copy(zbuf, x_ref.at[pl.ds(i * zgroups, zgroups)], sems.at[2]))
            return c

        lax.fori_loop(tail_ref[0], x_ref.shape[0] // zgroups, tail, 0)

    start = lambda cp: cp.start()
    wait = lambda cp: cp.wait()

    def wait_tile(t):
        slot = t % 2
        _run_pieces(trows_ref[t], tile_groups, lambda done, size: pltpu.make_async_copy(
            buf.at[slot, pl.ds(0, size)], x_ref.at[pl.ds(0, size)], sems.at[slot]).wait())

    @pl.when(j == 0)
    def _():
        zbuf[...] = jnp.zeros(zbuf.shape, zbuf.dtype)
        zero_fill(start)

    @pl.when(j >= 2)
    def _():
        wait_tile(j - 2)

    perm = _onehot_bands(posw_ref[...].T[:TOP_K], None)
    xs = jnp.dot(perm, h2_ref[...], preferred_element_type=F32)
    buf[j % 2] = _pack_exact(xs[:, :half], xs[:, half:]).reshape(tile_groups, RUN_ALIGN, half)
    tile_runs(j, start)

    @pl.when(j == nt - 1)
    def _():
        if nt >= 2:
            wait_tile(j - 1)
        wait_tile(j)
        zero_fill(wait)


def _dispatch(h2, posw, n_rows, tile_off, expert_off, run_n, tile_rows, zoff, zn, tail, tm):
    T = h2.shape[0]
    nt = T // tm
    G = RUN_ALIGN
    assert n_rows % EXPERT_BLOCK == 0 and EXPERT_BLOCK % G == 0 and TILE_ROWS % G == 0
    x_rows = pl.pallas_call(
        functools.partial(_dispatch_kernel, nt=nt),
        grid_spec=pltpu.PrefetchScalarGridSpec(
            num_scalar_prefetch=7,
            grid=(nt,),
            in_specs=[pl.BlockSpec((tm, D_MODEL), lambda i, *_: (i, 0)),
                      pl.BlockSpec((tm, LANES), lambda i, *_: (i, 0))],
            out_specs=pl.BlockSpec(memory_space=pl.ANY),
            scratch_shapes=[pltpu.VMEM((2, TILE_ROWS // G, G, D_MODEL // 2), jnp.uint32),
                            pltpu.VMEM((EXPERT_BLOCK // G, G, D_MODEL // 2), jnp.uint32),
                            pltpu.SemaphoreType.DMA((3,))]),
        out_shape=jax.ShapeDtypeStruct((n_rows // G, G, D_MODEL // 2), jnp.uint32),
        compiler_params=_params(("arbitrary",), VMEM_LIMIT),
        name="dispatch",
    )(tile_off, expert_off, run_n, tile_rows, zoff, zn, tail, h2, posw)
    return x_rows.reshape(n_rows, D_MODEL // 2)


def _expert_kernel(be_ref, rows_ref, slot_ref, next_ref, x_ref, wg_ref, bg_ref, wu_ref, bu_ref, wd_ref, bd_ref, y_ref,
                   w_in, wg_s, wu_s, wd_s, sems):
    i = pl.program_id(0)
    half = D_MODEL // 2
    M = x_ref.shape[0]
    rows = rows_ref[i]
    e = be_ref[i]
    prev = be_ref[jnp.maximum(i - 1, 0)]

    def weight_copies(expert, slot):
        return [pltpu.make_async_copy(w_hbm.at[expert], w_in.at[slot, k], sems.at[slot, k])
                for k, w_hbm in enumerate((wg_ref, wu_ref, wd_ref))]

    @pl.when((rows > 0) & ((i == 0) | (e != prev)))
    def _():
        slot = slot_ref[i]

        @pl.when(i == 0)
        def _():
            for cp in weight_copies(e, slot):
                cp.start()

        nxt = next_ref[i]

        @pl.when(nxt >= 0)
        def _():
            for cp in weight_copies(nxt, 1 - slot):
                cp.start()

        for cp in weight_copies(e, slot):
            cp.wait()
        for k, dst in enumerate((wg_s, wu_s, wd_s)):
            dst[...] = w_in[slot, k].astype(BF16)

    def compute(r):
        x_lo, x_hi = _unpack(x_ref[:r, :])

        def proj_in(w_s, b_ref):
            return (jnp.dot(x_lo, w_s[:half, :], preferred_element_type=F32)
                    + jnp.dot(x_hi, w_s[half:, :], preferred_element_type=F32) + b_ref[0])

        gate = jnp.minimum(proj_in(wg_s, bg_ref), SWIGLU_LIMIT)
        up = jnp.clip(proj_in(wu_s, bu_ref), -SWIGLU_LIMIT, SWIGLU_LIMIT)
        glu = gate * _sigmoid(gate * SWIGLU_ALPHA)
        act = ((up + 1.0) * glu).astype(BF16)
        y = jnp.dot(act, wd_s[...], preferred_element_type=F32) + bd_ref[0]
        y_ref[:r, :] = y
        if r < M:
            y_ref[r:, :] = jnp.zeros((M - r, D_MODEL), y_ref.dtype)

    for r in range(EXPERT_SUB, M + 1, EXPERT_SUB):
        pl.when(rows == r)(functools.partial(compute, r))

    @pl.when(rows == 0)
    def _():
        y_ref[...] = jnp.zeros(y_ref.shape, y_ref.dtype)


def _experts(blk_exp, blk_rows, blk_slot, blk_next, x_rows, n_pad, w_g, b_g, w_u, b_u, w_d, b_d):
    M = EXPERT_BLOCK
    assert D_FF == D_MODEL
    bspec = lambda n: pl.BlockSpec((1, 1, n), lambda i, be, *_: (be[i], 0, 0))
    wspec = pl.BlockSpec(memory_space=pl.ANY)
    return pl.pallas_call(
        _expert_kernel,
        grid_spec=pltpu.PrefetchScalarGridSpec(
            num_scalar_prefetch=4,
            grid=(n_pad // M,),
            in_specs=[pl.BlockSpec((M, D_MODEL // 2), lambda i, *_: (i, 0)),
                      wspec, bspec(D_FF), wspec, bspec(D_FF), wspec, bspec(D_MODEL)],
            out_specs=pl.BlockSpec((M, D_MODEL), lambda i, *_: (i, 0)),
            scratch_shapes=[pltpu.VMEM((2, 3, D_MODEL, D_FF), F32),
                            pltpu.VMEM((D_MODEL, D_FF), BF16),
                            pltpu.VMEM((D_MODEL, D_FF), BF16),
                            pltpu.VMEM((D_FF, D_MODEL), BF16),
                            pltpu.SemaphoreType.DMA((2, 3))]),
        out_shape=jax.ShapeDtypeStruct((n_pad, D_MODEL), F32),
        compiler_params=_params(("arbitrary",), VMEM_LIMIT),
        name="experts",
    )(blk_exp, blk_rows, blk_slot, blk_next, x_rows, w_g, b_g, w_u, b_u, w_d, b_d)


def _final_kernel(toff_ref, eoff_ref, n_ref, trows_ref, x1_ref, posw_ref, p_ref, gpg_ref, wpg_ref, wpp_ref, gpp_ref,
                  gfin_ref, y_ref, o_ref, buf, xmid, sems, *, nt):
    tm = x1_ref.shape[0]
    step = pl.program_id(0)
    j = jnp.minimum(step, nt - 1)

    def tile_runs(t, act):
        slot = t % 2

        def body(e, c):
            r = t * N_EXPERTS + e
            t0, s0 = toff_ref[r], eoff_ref[r]
            _run_pieces(n_ref[r], tm // RUN_ALIGN, lambda done, size: act(pltpu.make_async_copy(
                y_ref.at[pl.ds(s0 + done, size)], buf.at[slot, pl.ds(t0 + done, size)], sems.at[slot])))
            return c

        lax.fori_loop(0, N_EXPERTS, body, 0, unroll=RUN_UNROLL)

    @pl.when(step == 0)
    def _():
        buf[...] = jnp.zeros(buf.shape, buf.dtype)
        xmid[...] = jnp.zeros(xmid.shape, xmid.dtype)
        tile_runs(0, lambda cp: cp.start())

    @pl.when(step + 1 < nt)
    def _():
        tile_runs(step + 1, lambda cp: cp.start())

    @pl.when(step < nt)
    def _():
        _run_pieces(trows_ref[j], buf.shape[1], lambda done, size: pltpu.make_async_copy(
            y_ref.at[pl.ds(0, size)], buf.at[j % 2, pl.ds(0, size)], sems.at[j % 2]).wait())

    x = xmid[...]
    pp = jnp.dot(p_ref[...].astype(BF16), wpp_ref[...], preferred_element_type=F32)
    pg = _sigmoid(jnp.dot(_rms(x, gpg_ref[...]).astype(BF16), wpg_ref[...], preferred_element_type=F32))
    x = x + pg * _rms(pp, gpp_ref[...])
    o_ref[...] = _rms(x, gfin_ref[...])

    posw_t = posw_ref[...].T
    comb_t = _onehot_bands(posw_t[:TOP_K], posw_t[TOP_K:2 * TOP_K].astype(BF16))
    sure = TOP_K * tm
    sure_g = sure // RUN_ALIGN
    head = buf[j % 2, :sure_g].reshape(sure, D_MODEL)
    tail = buf[j % 2, sure_g:].reshape(TILE_ROWS - sure, D_MODEL)
    live = lax.broadcasted_iota(jnp.int32, tail.shape, 0) < trows_ref[j] * RUN_ALIGN - sure
    y = jnp.concatenate([head, jnp.where(live, tail, 0.0)], axis=0).astype(BF16)
    xmid[...] = x1_ref[...] + lax.dot_general(comb_t, y, TN, preferred_element_type=F32)


def _final(tile_off, expert_off, run_n, tile_rows, x1, y_rows, posw, p2, g_pg, w_pg, w_pp, g_pp, g_fin, tm):
    T = x1.shape[0]
    nt = T // tm
    full = lambda shape: pl.BlockSpec(shape, lambda i, *_: (0, 0))
    this = lambda w: pl.BlockSpec((tm, w), lambda i, *_: (jnp.minimum(i, nt - 1), 0))
    prev = lambda w: pl.BlockSpec((tm, w), lambda i, *_: (jnp.maximum(i - 1, 0), 0))
    return pl.pallas_call(
        functools.partial(_final_kernel, nt=nt),
        grid_spec=pltpu.PrefetchScalarGridSpec(
            num_scalar_prefetch=4,
            grid=(nt + 1,),
            in_specs=[this(D_MODEL), this(LANES), prev(PLE_DIM),
                      full((1, D_MODEL)), full((D_MODEL, D_MODEL)), full((PLE_DIM, D_MODEL)),
                      full((1, D_MODEL)), full((1, D_MODEL)),
                      pl.BlockSpec(memory_space=pl.ANY)],
            out_specs=prev(D_MODEL),
            scratch_shapes=[pltpu.VMEM((2, TILE_ROWS // RUN_ALIGN, RUN_ALIGN, D_MODEL), F32),
                            pltpu.VMEM((tm, D_MODEL), F32),
                            pltpu.SemaphoreType.DMA((2,))]),
        out_shape=jax.ShapeDtypeStruct((T, D_MODEL), F32),
        compiler_params=_params(("arbitrary",), VMEM_LIMIT),
        name="final",
    )(tile_off, expert_off, run_n, tile_rows, x1, posw, p2, g_pg, w_pg, w_pp, g_pp, g_fin,
      y_rows.reshape(-1, RUN_ALIGN, D_MODEL))


def _split_w_in(w_in):
    sizes = (GLA_QK, GLA_QK, GLA_V, GLA_V, GLA_RANK, MOBA_W, MOBA_W, MOBA_W, D_MODEL, D_MODEL)
    offs = [0]
    for s in sizes:
        offs.append(offs[-1] + s)
    main = jnp.concatenate([w_in[:, :offs[4]], w_in[:, offs[5]:]], axis=1).astype(BF16)
    alow = jnp.pad(w_in[:, offs[4]:offs[5]], ((0, 0), (0, LANES - GLA_RANK))).astype(BF16)
    return main, alow


def _layer(x2, p2, bias, B, S, g_mix, w_in, w_a2, b_a2, g_gla_out, w_up_gla, w_up_moba, w_o, g_ffn, w_router,
           b_router, w_e_gate, b_e_gate, w_e_up, b_e_up, w_e_down, b_e_down, g_ple_gate, w_ple_gate, w_ple_proj,
           g_ple_proj, g_final):
    T = B * S
    row = lambda v: v.reshape(1, -1).astype(F32)
    w_main, w_alow = _split_w_in(w_in)
    colscale = jnp.ones((D_PROJ,), F32)
    colscale = colscale.at[OFF_QA:OFF_QA + GLA_QK].set(GLA_DK ** -0.5)
    colscale = colscale.at[OFF_QB:OFF_QB + MOBA_W].set(MOBA_HD ** -0.5 * LOG2E)
    w_a2p = jnp.pad(w_a2, ((0, LANES - GLA_RANK), (0, 0))).astype(BF16)
    proj, glog = _inproj(x2, row(g_mix), w_main, colscale.reshape(1, -1), w_alow, w_a2p, row(b_a2))

    o_a = _gla(proj, glog, row(g_gla_out), B, S)
    o_b = _moba(proj, bias, B, S)

    w_r = jnp.pad(w_router.astype(F32), ((0, 0), (0, LANES - N_EXPERTS)))
    w_r_hi = w_r.astype(BF16)
    w_r2 = jnp.concatenate([w_r_hi, (w_r - w_r_hi.astype(F32)).astype(BF16)], axis=1)
    b_r = jnp.pad(b_router.astype(F32), (0, LANES - N_EXPERTS)).reshape(1, -1)
    tm = TOKEN_TILE
    nt = T // tm
    x1, h2, logits = _merge(o_a, o_b, proj, x2, w_up_gla.astype(BF16), w_up_moba.astype(BF16),
                            w_o.astype(BF16), row(g_ffn), w_r2, b_r, MERGE_TILE)
    posw, cnt_t, carry = _router(logits, tm)

    M = EXPERT_BLOCK
    A = nt * TILE_ROWS
    n_pad = (-(-A // M)) * M + N_EXPERTS * M
    n_blk = n_pad // M
    cnt_t = cnt_t[:, 0, :N_EXPERTS].astype(jnp.int32)
    carry = carry[:, 0, :N_EXPERTS].astype(jnp.int32)
    counts = carry[-1] + cnt_t[-1]
    padded = (counts + M - 1) // M * M
    pad_end = jnp.cumsum(padded)
    pad_start = pad_end - padded
    blk_exp = jnp.minimum(jnp.sum(pad_end[None, :] <= (jnp.arange(n_blk, dtype=jnp.int32) * M)[:, None], axis=1),
                          N_EXPERTS - 1).astype(jnp.int32)
    n_used = (pad_end[-1:] // M).astype(jnp.int32)
    blk_start = jnp.arange(n_blk, dtype=jnp.int32) * M
    eids = jnp.arange(N_EXPERTS, dtype=jnp.int32)

    def per_block(per_expert):
        return jnp.sum(jnp.where(blk_exp[:, None] == eids[None, :], per_expert[None, :], 0), axis=1).astype(jnp.int32)

    blk_rows = jnp.clip(per_block(pad_start + counts) - blk_start, 0, M)
    blk_rows = jnp.where(blk_start < pad_end[-1], (blk_rows + EXPERT_SUB - 1) // EXPERT_SUB * EXPERT_SUB, 0)
    has_rows = counts > 0
    blk_slot = per_block((jnp.cumsum(has_rows) - 1) % 2)
    later = jnp.where((eids[None, :] > eids[:, None]) & has_rows[None, :], eids[None, :], N_EXPERTS)
    next_exp = jnp.min(later, axis=1)
    blk_next = per_block(jnp.where(next_exp < N_EXPERTS, next_exp, -1))
    G = RUN_ALIGN
    tile_off = (jnp.cumsum(cnt_t, axis=1) - cnt_t).reshape(-1) // G
    expert_off = (carry + pad_start[None, :]).reshape(-1) // G
    run_n = cnt_t.reshape(-1) // G
    tile_rows = jnp.sum(cnt_t, axis=1) // G
    x_rows = _dispatch(h2, posw, n_pad, tile_off, expert_off, run_n, tile_rows, (pad_start + counts) // G,
                       (padded - counts) // G, n_used, tm)
    y_rows = _experts(blk_exp, blk_rows.astype(jnp.int32), blk_slot, blk_next, x_rows, n_pad,
                      w_e_gate, b_e_gate.reshape(N_EXPERTS, 1, D_FF),
                      w_e_up, b_e_up.reshape(N_EXPERTS, 1, D_FF), w_e_down,
                      b_e_down.reshape(N_EXPERTS, 1, D_MODEL))
    return _final(tile_off, expert_off, run_n, tile_rows, x1, y_rows, posw, p2, row(g_ple_gate),
                  w_ple_gate.astype(BF16), w_ple_proj.astype(BF16), row(g_ple_proj), row(g_final), tm)


def kernel(x, p, rel_bias, g_mix, w_in, w_a2, b_a2, g_gla_out, w_up_gla, w_up_moba, w_o, g_ffn, w_router, b_router,
           w_e_gate, b_e_gate, w_e_up, b_e_up, w_e_down, b_e_down, g_ple_gate, w_ple_gate, w_ple_proj, g_ple_proj,
           g_final):
    B, S, D = x.shape
    assert D == D_MODEL and S % MOBA_BLOCK == 0 and S % GLA_CHUNK == 0 and p.shape[0] == 1
    bias = _bias_tiles(rel_bias, S // MOBA_BLOCK)
    out = _layer(x.reshape(B * S, D), p[0].reshape(B * S, PLE_DIM), bias, B, S,
                 g_mix[0], w_in[0], w_a2[0], b_a2[0], g_gla_out[0], w_up_gla[0], w_up_moba[0], w_o[0], g_ffn[0],
                 w_router[0], b_router[0], w_e_gate[0], b_e_gate[0], w_e_up[0], b_e_up[0], w_e_down[0],
                 b_e_down[0], g_ple_gate[0], w_ple_gate[0], w_ple_proj[0], g_ple_proj[0], g_final)
    return out.reshape(B, S, D)
```

```python
import functools
import math

import jax
import jax.numpy as jnp
from jax import lax
from jax.experimental import pallas as pl
from jax.experimental.pallas import tpu as pltpu

F32 = jnp.float32
BF16 = jnp.bfloat16

D_MODEL = 1024
PLE_DIM = 256
GLA_HEADS = 4
GLA_DK = 128
GLA_DV = 256
GLA_RANK = 16
GLA_TAU = 16.0
GLA_QK = GLA_HEADS * GLA_DK
GLA_V = GLA_HEADS * GLA_DV
MOBA_HEADS = 8
MOBA_HD = 128
MOBA_BLOCK = 256
MOBA_TOPK = 3
MOBA_W = MOBA_HEADS * MOBA_HD
REL_BUCKETS = 32
REL_MAX_DIST = 4096
N_EXPERTS = 32
TOP_K = 4
D_FF = 1024
SWIGLU_LIMIT = 7.0
SWIGLU_ALPHA = 1.702
EPS = 1e-6

LANES = 128
NEG = -1e30
LOG2E = math.log2(math.e)
VMEM_LIMIT = 56 * 1024 * 1024

OFF_QA, OFF_KA, OFF_VA, OFF_RA = 0, 512, 1024, 2048
OFF_QB, OFF_KB, OFF_VB, OFF_GA, OFF_GB = 3072, 4096, 5120, 6144, 7168
D_PROJ = 8192

GLA_CHUNK = 128
EXPERT_BLOCK = 512
EXPERT_SUB = 128
TOKEN_TILE = 256
RUN_ALIGN = 8
TILE_ROWS = TOP_K * TOKEN_TILE + N_EXPERTS * RUN_ALIGN
RUN_UNROLL = 8
MERGE_TILE = 512

NT = (((1,), (1,)), ((), ()))
TN = (((0,), (0,)), ((), ()))


def _params(sem, vmem=None):
    return pltpu.CompilerParams(dimension_semantics=sem, vmem_limit_bytes=vmem)


def _rms(x, g):
    return x * lax.rsqrt(jnp.mean(x * x, axis=-1, keepdims=True) + EPS) * g


def _sigmoid(x):
    return 1.0 / (1.0 + jnp.exp(-x))


def _bucket_of(n):
    max_exact = REL_BUCKETS // 2
    if n < max_exact:
        return n
    return min(max_exact + int(math.log(n / max_exact) / math.log(REL_MAX_DIST / max_exact)
                               * (REL_BUCKETS - max_exact)), REL_BUCKETS - 1)


def _bias_kernel(tab_ref, bkt_ref, o_ref, *, nblk):
    strip = 8
    m = pl.program_id(0)

    def fill(lo, hi):
        def body(s, carry):
            r0 = pl.multiple_of(s * strip, strip)
            b = bkt_ref[0, pl.ds(r0, strip), :]
            accs = [jnp.zeros(b.shape, F32) for _ in range(MOBA_HEADS)]
            for bb in range(lo, hi + 1):
                hit = b == bb
                for h in range(MOBA_HEADS):
                    accs[h] = jnp.where(hit, tab_ref[h, bb], accs[h])
            for h in range(MOBA_HEADS):
                o_ref[h, 0, pl.ds(r0, strip), :] = jnp.where(b < 0, NEG, accs[h])
            return carry

        lax.fori_loop(0, MOBA_BLOCK // strip, body, 0)

    for mm in range(nblk):
        lo = max(_bucket_of(max(mm * MOBA_BLOCK - (MOBA_BLOCK - 1), 0)) - 1, 0)
        hi = min(_bucket_of(mm * MOBA_BLOCK + MOBA_BLOCK - 1) + 1, REL_BUCKETS - 1)
        pl.when(m == mm)(functools.partial(fill, lo, hi))


def _bias_tiles(rel_bias, nblk):
    i = jnp.arange(MOBA_BLOCK, dtype=jnp.int32)
    dist = (jnp.arange(nblk, dtype=jnp.int32)[:, None, None] * MOBA_BLOCK + i[None, :, None] - i[None, None, :])
    n = jnp.maximum(dist, 0)
    max_exact = REL_BUCKETS // 2
    nf = jnp.maximum(n, 1).astype(F32)
    large = max_exact + (jnp.log(nf / max_exact) / math.log(REL_MAX_DIST / max_exact)
                         * (REL_BUCKETS - max_exact)).astype(jnp.int32)
    large = jnp.minimum(large, REL_BUCKETS - 1)
    bkt = jnp.where(dist < 0, -1, jnp.where(n < max_exact, n, large)).astype(jnp.int32)
    tab = rel_bias.astype(F32).T * LOG2E
    return pl.pallas_call(
        functools.partial(_bias_kernel, nblk=nblk),
        grid=(nblk,),
        in_specs=[pl.BlockSpec(memory_space=pltpu.SMEM),
                  pl.BlockSpec((1, MOBA_BLOCK, MOBA_BLOCK), lambda m: (m, 0, 0))],
        out_specs=pl.BlockSpec((MOBA_HEADS, 1, MOBA_BLOCK, MOBA_BLOCK), lambda m: (0, m, 0, 0)),
        out_shape=jax.ShapeDtypeStruct((MOBA_HEADS, nblk, MOBA_BLOCK, MOBA_BLOCK), F32),
        compiler_params=_params(("arbitrary",)),
        name="bias_tiles",
    )(tab, bkt)


def _inproj_kernel(x0_ref, xn_ref, g_ref, w_ref, cs_ref, wal_ref, wa2_ref, ba2_ref, o_ref, glog_ref, h_scr, al_scr,
                   *, n_j):
    i, j = pl.program_id(0), pl.program_id(1)
    tm = xn_ref.shape[0]
    rs = tm // n_j

    def gate_logits(a_low):
        a = jnp.dot(a_low, wa2_ref[...], preferred_element_type=F32) + ba2_ref[...]
        log_sig = jnp.minimum(a, 0.0) - jnp.log1p(jnp.exp(-jnp.abs(a)))
        return log_sig * (1.0 / GLA_TAU)

    def slice_rows(behind):
        return pl.multiple_of(((j + n_j - behind) % n_j) * rs, rs)

    @pl.when((i == 0) & (j == 0))
    def _():
        h0 = _rms(x0_ref[...], g_ref[...]).astype(BF16)
        h_scr[0] = h0
        a_low0 = jnp.dot(h0, wal_ref[...], preferred_element_type=F32).astype(BF16)
        glog_ref[...] = gate_logits(a_low0)
        al_scr[...] = a_low0[(n_j - 2) * rs:(n_j - 1) * rs, :]

    glog_ref[pl.ds(slice_rows(2), rs), :] = gate_logits(al_scr[...])
    slot = jnp.where(j == 0, i, i + 1) % 2
    h_lag = h_scr[slot, pl.ds(slice_rows(1), rs), :]
    al_scr[...] = jnp.dot(h_lag, wal_ref[...], preferred_element_type=F32).astype(BF16)

    acc = jnp.dot(h_scr[i % 2], w_ref[...], preferred_element_type=F32)
    o_ref[...] = (acc * cs_ref[...]).astype(BF16)

    h_scr[(i + 1) % 2, pl.ds(slice_rows(0), rs), :] = _rms(xn_ref[pl.ds(slice_rows(0), rs), :],
                                                             g_ref[...]).astype(BF16)


def _inproj(x2, g_mix, w_main, colscale, w_alow, w_a2p, b_a2, tm=1024, tn=2048):
    T = x2.shape[0]
    n_i, n_j = T // tm, D_PROJ // tn
    assert n_i >= 2 and tm % (8 * n_j) == 0
    return pl.pallas_call(
        functools.partial(_inproj_kernel, n_j=n_j),
        grid=(n_i, n_j),
        in_specs=[pl.BlockSpec((tm, D_MODEL), lambda i, j: (0, 0)),
                  pl.BlockSpec((tm, D_MODEL), lambda i, j: (jnp.minimum(i + 1, n_i - 1), 0)),
                  pl.BlockSpec((1, D_MODEL), lambda i, j: (0, 0)),
                  pl.BlockSpec((D_MODEL, tn), lambda i, j: (0, j)),
                  pl.BlockSpec((1, tn), lambda i, j: (0, j)),
                  pl.BlockSpec((D_MODEL, LANES), lambda i, j: (0, 0)),
                  pl.BlockSpec((LANES, GLA_QK), lambda i, j: (0, 0)),
                  pl.BlockSpec((1, GLA_QK), lambda i, j: (0, 0))],
        out_specs=[pl.BlockSpec((tm, tn), lambda i, j: (i, j)),
                   pl.BlockSpec((tm, GLA_QK), lambda i, j: (jnp.minimum(i + jnp.minimum(j // 2, 1), n_i - 1), 0))],
        out_shape=[jax.ShapeDtypeStruct((T, D_PROJ), BF16),
                   jax.ShapeDtypeStruct((T, GLA_QK), F32)],
        scratch_shapes=[pltpu.VMEM((2, tm, D_MODEL), BF16), pltpu.VMEM((tm // n_j, LANES), BF16)],
        compiler_params=_params(("arbitrary", "arbitrary"), VMEM_LIMIT),
        name="inproj",
    )(x2, x2, g_mix, w_main, colscale, w_alow, w_a2p, b_a2)


def _gla_kernel(q_ref, k_ref, v_ref, r_ref, g_ref, gout_ref, o_ref, st_ref):
    C = GLA_CHUNK

    @pl.when(pl.program_id(1) == 0)
    def _():
        st_ref[...] = jnp.zeros(st_ref.shape, F32)

    nb = q_ref.shape[0]
    row = lax.broadcasted_iota(jnp.int32, (C, C), 0)
    col = lax.broadcasted_iota(jnp.int32, (C, C), 1)
    causal = col <= row
    ltri = causal.astype(BF16)
    mid = C // 2
    pairs = [(b, h) for b in range(nb) for h in range(GLA_HEADS)]
    ks = lambda h: slice(h * GLA_DK, (h + 1) * GLA_DK)
    vs = lambda h: slice(h * GLA_DV, (h + 1) * GLA_DV)
    gate = {}
    for b, h in pairs:
        r = r_ref[b, :, vs(h)].astype(F32)
        gate[b, h] = r * _sigmoid(r)
    G = []
    for b in range(nb):
        g = g_ref[b]
        g_hi = g.astype(BF16)
        g_lo = (g - g_hi.astype(F32)).astype(BF16)
        G.append(jnp.dot(ltri, g_hi, preferred_element_type=F32) + jnp.dot(ltri, g_lo, preferred_element_type=F32))
    Gh = {(b, h): G[b][:, ks(h)] for b, h in pairs}
    qh = {(b, h): q_ref[b, :, ks(h)].astype(F32) for b, h in pairs}
    kh = {(b, h): k_ref[b, :, ks(h)].astype(F32) for b, h in pairs}
    g_mid = {p: Gh[p][mid:mid + 1, :] for p in pairs}
    g_last = {p: Gh[p][C - 1:C, :] for p in pairs}
    A = {p: lax.dot_general((qh[p] * jnp.exp(Gh[p] - g_mid[p])).astype(BF16),
                            (kh[p] * jnp.exp(g_mid[p] - Gh[p])).astype(BF16), NT, preferred_element_type=F32)
         for p in pairs}
    st = {p: st_ref[p[0], p[1]] for p in pairs}
    inter = {p: lax.dot_general((qh[p] * jnp.exp(Gh[p])).astype(BF16), st[p].astype(BF16), NT,
                                preferred_element_type=F32) for p in pairs}
    for b, h in pairs:
        p = (b, h)
        k_d = (kh[p] * jnp.exp(g_last[p] - Gh[p])).astype(BF16)
        st_ref[b, h] = jnp.exp(g_last[p]) * st[p] + lax.dot_general(v_ref[b, :, vs(h)], k_d, TN,
                                                                     preferred_element_type=F32)
    intra = {(b, h): jnp.dot(jnp.where(causal, A[b, h], 0.0).astype(BF16), v_ref[b, :, vs(h)],
                             preferred_element_type=F32) for b, h in pairs}
    for b, h in pairs:
        o = inter[b, h] + intra[b, h]
        o_ref[b, :, vs(h)] = (_rms(o, gout_ref[...]) * gate[b, h]).astype(BF16)


def _gla(proj, glog, g_gla_out, B, S, nb=8):
    C = GLA_CHUNK
    assert B % nb == 0
    proj3 = proj.reshape(B, S, D_PROJ)
    glog3 = glog.reshape(B, S, GLA_QK)
    spec = lambda w, off: pl.BlockSpec((nb, C, w), lambda b, c: (b, c, off // w))
    out = pl.pallas_call(
        _gla_kernel,
        grid=(B // nb, S // C),
        in_specs=[spec(GLA_QK, OFF_QA), spec(GLA_QK, OFF_KA), spec(GLA_V, OFF_VA), spec(GLA_V, OFF_RA),
                  spec(GLA_QK, 0), pl.BlockSpec((1, GLA_DV), lambda b, c: (0, 0))],
        out_specs=spec(GLA_V, 0),
        out_shape=jax.ShapeDtypeStruct((B, S, GLA_V), BF16),
        scratch_shapes=[pltpu.VMEM((nb, GLA_HEADS, GLA_DV, GLA_DK), F32)],
        compiler_params=_params(("arbitrary", "arbitrary")),
        name="gla",
    )(proj3, proj3, proj3, proj3, glog3, g_gla_out)
    return out.reshape(B * S, GLA_V)


def _moba_kernel(q_ref, k_ref, v_ref, bias_ref, o_ref, ka_scr, va_scr, lg_scr, mx_scr, qa_scr, *, nblk):
    BLK, HD = MOBA_BLOCK, MOBA_HD
    S = nblk * BLK
    units = range(q_ref.shape[0])

    @pl.when((pl.program_id(0) == 0) & (pl.program_id(1) == 0))
    def _():
        blk = lax.broadcasted_iota(jnp.int32, (S, HD), 0) // BLK
        lane = lax.broadcasted_iota(jnp.int32, (S, HD), 1)
        for u in units:
            ka_scr[u, :, HD:] = (lane == blk).astype(BF16)
            va_scr[u, :, HD:] = (lane == 0).astype(BF16)

    n_plain = min(MOBA_TOPK + 1, nblk)
    n_late = nblk - n_plain
    n_tiles = nblk * (nblk + 1) // 2

    def tile_id(cc, j):
        return cc * (cc + 1) // 2 + j

    def rows(cc):
        return slice(cc * BLK, (cc + 1) * BLK)

    def stage1(u, cc, q_in, keys):
        mx = None
        for j in range(cc + 1):
            lg = lax.dot_general(q_in, keys(j), NT, preferred_element_type=F32) + bias_ref[0, cc - j]
            lg_scr[u * n_tiles + tile_id(cc, j)] = lg
            t = jnp.maximum(lg[:, :LANES], lg[:, LANES:])
            mx = t if mx is None else jnp.maximum(mx, t)
            if j == cc:
                mx_scr[u * nblk + cc] = mx
            yield

    def stage2(u, cc):
        m = mx_scr[u * nblk + cc].max(axis=-1, keepdims=True)
        acc = jnp.zeros((BLK, 2 * HD), F32)
        for j in range(cc + 1):
            p = jnp.exp2(lg_scr[u * n_tiles + tile_id(cc, j)] - m).astype(BF16)
            acc = acc + jnp.dot(p, va_scr[u, rows(j), :], preferred_element_type=F32)
            if j == cc:
                o_ref[u, rows(cc), :] = (acc[:, :HD] / acc[:, HD:HD + 1]).astype(BF16)
            yield

    def drain(gen):
        for _ in gen:
            pass

    def interleave(main, side, n_main, n_side):
        side_steps = (s for g in side for s in g)
        done = 0
        for i, _ in enumerate(s for g in main for s in g):
            assert n_main > 0
            want = (i + 1) * n_side // n_main
            while done < want and next(side_steps, "end") != "end":
                done += 1
        drain(side_steps)

    plain = {u: [stage1(u, cc, q_ref[u, rows(cc), :], lambda j, u=u: k_ref[u, rows(j), :]) for cc in range(n_plain)]
             for u in units}
    if n_late:
        pens = []
        for u in units:
            next(plain[u][0])
            ksum = [k_ref[u, rows(j), :].astype(F32).reshape(BLK // 8, 8, HD).sum(axis=0).sum(axis=0, keepdims=True)
                    for j in range(nblk)]
            kmean = jnp.concatenate(ksum, axis=0) * (1.0 / BLK)
            km_hi = kmean.astype(BF16)
            km_lo = (kmean - km_hi.astype(F32)).astype(BF16)
            km2 = jnp.concatenate([km_hi, km_lo], axis=0)
            for cc in range(n_plain, nblk):
                q = q_ref[u, rows(cc), :]
                s2 = lax.dot_general(km2, q, NT, preferred_element_type=F32)
                pens.append((u, cc, q, s2[:nblk] + s2[nblk:]))
        for u in units:
            for g in plain[u][:2]:
                drain(g)
        for u, cc, q, s in pens:
            ji = lax.broadcasted_iota(jnp.int32, s.shape, 0)
            cnt = jnp.zeros(s.shape, F32)
            for jp in range(cc):
                sj = s[jp:jp + 1, :]
                beats = (sj > s) | ((sj == s) & (jp < ji))
                cnt = cnt + beats.astype(F32)
            pen = jnp.where((ji < cc) & (cnt >= MOBA_TOPK), NEG, 0.0)
            pen_t = jnp.concatenate([pen, jnp.zeros((HD - nblk, BLK), F32)], axis=0).T
            qa_scr[u * n_late + cc - n_plain] = jnp.concatenate([q, pen_t.astype(BF16)], axis=1)
    for u in units:
        ka_scr[u, :, :HD] = k_ref[u]
        va_scr[u, :, :HD] = v_ref[u]
    for u in units:
        for g in plain[u]:
            drain(g)

    @pl.when(pl.program_id(0) >= 0)
    def _():
        late = [stage1(u, cc, qa_scr[u * n_late + cc - n_plain], lambda j, u=u: ka_scr[u, rows(j), :])
                for u in units for cc in range(n_plain, nblk)]
        early = [stage2(u, cc) for u in units for cc in range(n_plain)]
        if late:
            interleave(late, early, len(units) * (n_tiles - tile_id(n_plain, 0)), len(units) * tile_id(n_plain, 0))
        else:
            for g in early:
                drain(g)

    @pl.when(pl.program_id(0) >= 0)
    def _():
        for u in units:
            for cc in range(n_plain, nblk):
                drain(stage2(u, cc))


def _moba(proj, bias, B, S, nb=2):
    BLK = MOBA_BLOCK
    nblk = S // BLK
    H = MOBA_HEADS
    assert nblk <= MOBA_HD and B % nb == 0
    proj3 = proj.reshape(B, S, D_PROJ)
    spec = lambda off: pl.BlockSpec((nb, S, MOBA_HD), lambda h, b: (b, 0, off // MOBA_HD + h))
    out = pl.pallas_call(
        functools.partial(_moba_kernel, nblk=nblk),
        grid=(H, B // nb),
        in_specs=[spec(OFF_QB), spec(OFF_KB), spec(OFF_VB),
                  pl.BlockSpec((1, nblk, BLK, BLK), lambda h, b: (h, 0, 0, 0))],
        out_specs=spec(0),
        out_shape=jax.ShapeDtypeStruct((B, S, MOBA_W), BF16),
        scratch_shapes=[pltpu.VMEM((nb, S, 2 * MOBA_HD), BF16), pltpu.VMEM((nb, S, 2 * MOBA_HD), BF16),
                        pltpu.VMEM((nb * nblk * (nblk + 1) // 2, BLK, BLK), F32),
                        pltpu.VMEM((nb * nblk, BLK, LANES), F32),
                        pltpu.VMEM((nb * max(nblk - MOBA_TOPK - 1, 1), BLK, 2 * MOBA_HD), BF16)],
        compiler_params=_params(("arbitrary", "arbitrary"), VMEM_LIMIT),
        name="moba",
    )(proj3, proj3, proj3, bias)
    return out.reshape(B * S, MOBA_W)


def _pack_exact(lo, hi):
    lo_b = lax.bitcast_convert_type(lo, jnp.uint32)
    hi_b = lax.bitcast_convert_type(hi, jnp.uint32)
    return (lo_b >> 16) | (hi_b & jnp.uint32(0xFFFF0000))


def _unpack(w):
    lo = lax.bitcast_convert_type(w << 16, F32)
    hi = lax.bitcast_convert_type(w & jnp.uint32(0xFFFF0000), F32)
    return lo.astype(BF16), hi.astype(BF16)


def _merge_kernel(oa_ref, ob_ref, ga_ref, gb_ref, x_ref, wua_ref, wub_ref, wo_ref, gffn_ref, wr_ref, br_ref,
                  x1_ref, h2_ref, lg_ref):
    tm = x_ref.shape[0]
    u_a = jnp.dot(oa_ref[...], wua_ref[...], preferred_element_type=F32)
    u_b = jnp.dot(ob_ref[...], wub_ref[...], preferred_element_type=F32)
    y = _sigmoid(ga_ref[...].astype(F32)) * u_a + _sigmoid(gb_ref[...].astype(F32)) * u_b
    x1 = x_ref[...] + jnp.dot(y.astype(BF16), wo_ref[...], preferred_element_type=F32)
    x1_ref[...] = x1
    h2 = _rms(x1, gffn_ref[...])
    h_hi = h2.astype(BF16)
    h2_ref[...] = h_hi
    h_lo = (h2 - h_hi.astype(F32)).astype(BF16)
    r = jnp.dot(jnp.concatenate([h_hi, h_lo], axis=0), wr_ref[...], preferred_element_type=F32)
    lg_ref[...] = r[:tm, :LANES] + r[:tm, LANES:] + r[tm:, :LANES] + br_ref[...]


def _merge(o_a, o_b, proj, x2, w_ua, w_ub, w_o, g_ffn, w_r2, b_r, tm):
    T = x2.shape[0]
    full = lambda shape: pl.BlockSpec(shape, lambda i: (0, 0))
    rowblk = lambda w: pl.BlockSpec((tm, w), lambda i: (i, 0))
    return pl.pallas_call(
        _merge_kernel,
        grid=(T // tm,),
        in_specs=[rowblk(GLA_V), rowblk(MOBA_W),
                  pl.BlockSpec((tm, D_MODEL), lambda i: (i, OFF_GA // D_MODEL)),
                  pl.BlockSpec((tm, D_MODEL), lambda i: (i, OFF_GB // D_MODEL)),
                  rowblk(D_MODEL),
                  full((GLA_V, D_MODEL)), full((MOBA_W, D_MODEL)), full((D_MODEL, D_MODEL)),
                  full((1, D_MODEL)), full((D_MODEL, 2 * LANES)), full((1, LANES))],
        out_specs=[rowblk(D_MODEL), rowblk(D_MODEL), rowblk(LANES)],
        out_shape=[jax.ShapeDtypeStruct((T, D_MODEL), F32),
                   jax.ShapeDtypeStruct((T, D_MODEL), BF16),
                   jax.ShapeDtypeStruct((T, LANES), F32)],
        compiler_params=_params(("arbitrary",), VMEM_LIMIT),
        name="merge",
    )(o_a, o_b, proj, proj, x2, w_ua, w_ub, w_o, g_ffn, w_r2, b_r)


def _router_kernel(lg_ref, posw_ref, cnt_ref, carry_ref, cnt_scr, *, tm):
    rows = lg_ref.shape[0]

    @pl.when(pl.program_id(0) == 0)
    def _():
        cnt_scr[...] = jnp.zeros(cnt_scr.shape, F32)

    lane = lax.broadcasted_iota(jnp.int32, (rows, LANES), 1)
    lane_f = lane.astype(F32)
    work = jnp.where(lane < N_EXPERTS, lg_ref[...], NEG)
    vals, hots = [], []
    for _ in range(TOP_K):
        mx = work.max(axis=-1, keepdims=True)
        idx = jnp.min(jnp.where(work == mx, lane_f, float(LANES)), axis=-1, keepdims=True)
        hot = lane_f == idx
        vals.append(mx)
        hots.append(hot)
        work = jnp.where(hot, 2.0 * NEG, work)
    exps = [jnp.exp(v - vals[0]) for v in vals]
    den = exps[0] + exps[1] + exps[2] + exps[3]
    sel = jnp.zeros((rows, LANES), F32)
    for hot in hots:
        sel = sel + hot.astype(F32)
    row = lax.broadcasted_iota(jnp.int32, (tm, tm), 0)
    col = lax.broadcasted_iota(jnp.int32, (tm, tm), 1)
    below = (col < row).astype(BF16)
    er = lax.broadcasted_iota(jnp.int32, (LANES, LANES), 0)
    ec = lax.broadcasted_iota(jnp.int32, (LANES, LANES), 1)
    before = (er < ec).astype(F32)
    pos_parts = []
    for t in range(rows // tm):
        sel_t = sel[t * tm:(t + 1) * tm]
        local_rank = jnp.dot(below, sel_t.astype(BF16), preferred_element_type=F32)
        cnt_t = sel_t.sum(axis=0, keepdims=True)
        cnt_t = jnp.floor((cnt_t + (RUN_ALIGN - 1.0)) * (1.0 / RUN_ALIGN)) * RUN_ALIGN
        tile_off = jnp.dot(jnp.broadcast_to(cnt_t, (8, LANES)), before, preferred_element_type=F32,
                           precision=lax.Precision.HIGHEST)[0:1]
        pos_parts.append(local_rank + tile_off)
        carry_ref[t] = cnt_scr[...]
        cnt_ref[t] = cnt_t
        cnt_scr[...] = cnt_scr[...] + cnt_t
    pos_all = jnp.concatenate(pos_parts, axis=0)
    posw = jnp.zeros((rows, LANES), F32)
    for k in range(TOP_K):
        pk = jnp.sum(jnp.where(hots[k], pos_all, 0.0), axis=-1, keepdims=True)
        posw = jnp.where(lane == k, pk, posw)
        posw = jnp.where(lane == TOP_K + k, exps[k] / den, posw)
    posw_ref[...] = posw


def _router(logits, tm, tiles_per_step=4):
    T = logits.shape[0]
    nt = T // tm
    rows = tm * tiles_per_step
    tilerow = pl.BlockSpec((tiles_per_step, 1, LANES), lambda i: (i, 0, 0))
    return pl.pallas_call(
        functools.partial(_router_kernel, tm=tm),
        grid=(T // rows,),
        in_specs=[pl.BlockSpec((rows, LANES), lambda i: (i, 0))],
        out_specs=[pl.BlockSpec((rows, LANES), lambda i: (i, 0)), tilerow, tilerow],
        out_shape=[jax.ShapeDtypeStruct((T, LANES), F32),
                   jax.ShapeDtypeStruct((nt, 1, LANES), F32),
                   jax.ShapeDtypeStruct((nt, 1, LANES), F32)],
        scratch_shapes=[pltpu.VMEM((1, LANES), F32)],
        compiler_params=_params(("arbitrary",)),
        name="router",
    )(logits)


def _run_pieces(n, max_n, fn):
    for b in reversed(range(max_n.bit_length())):
        size = 1 << b
        done = n & ~((2 << b) - 1)

        @pl.when((n & size) != 0)
        def _():
            fn(done, size)


def _onehot_bands(pos, val):
    band = 256
    assert TILE_ROWS % band == 0
    n_k, n_tok = pos.shape
    pos_a = jnp.floor(pos * (1.0 / band))
    pos_b = pos - band * pos_a
    sub = lax.broadcasted_iota(jnp.int32, (band, n_tok), 0).astype(F32).astype(BF16)
    zero = jnp.zeros((band, n_tok), BF16)
    bands = []
    for a in range(TILE_ROWS // band):
        want = jnp.where(pos_a == a, pos_b, -1.0).astype(BF16)
        hit = zero
        for k in range(n_k):
            row = jnp.broadcast_to(want[k:k + 1, :], (band, n_tok))
            fill = jnp.ones((band, n_tok), BF16) if val is None else jnp.broadcast_to(val[k:k + 1, :], (band, n_tok))
            hit = hit + jnp.where(sub == row, fill, zero)
        bands.append(hit)
    return jnp.concatenate(bands, axis=0)


def _dispatch_kernel(toff_ref, eoff_ref, n_ref, trows_ref, zoff_ref, zn_ref, tail_ref, h2_ref, posw_ref, x_ref,
                     buf, zbuf, sems, *, nt):
    tm = h2_ref.shape[0]
    half = D_MODEL // 2
    j = pl.program_id(0)
    zgroups = zbuf.shape[0]
    tile_groups = buf.shape[1]

    def tile_runs(t, act):
        slot = t % 2

        def body(e, c):
            r = t * N_EXPERTS + e
            t0, d0 = toff_ref[r], eoff_ref[r]
            _run_pieces(n_ref[r], tm // RUN_ALIGN, lambda done, size: act(pltpu.make_async_copy(
                buf.at[slot, pl.ds(t0 + done, size)], x_ref.at[pl.ds(d0 + done, size)], sems.at[slot])))
            return c

        lax.fori_loop(0, N_EXPERTS, body, 0, unroll=RUN_UNROLL)

    def zero_fill(act):
        def body(e, c):
            d0 = zoff_ref[e]
            _run_pieces(zn_ref[e], zgroups, lambda done, size: act(pltpu.make_async_copy(
                zbuf.at[pl.ds(0, size)], x_ref.at[pl.ds(d0 + done, size)], sems.at[2])))
            return c

        lax.fori_loop(0, N_EXPERTS, body, 0)

        def tail(i, c):
            act(pltpu.make_async_copy(zbuf, x_ref.at[pl.ds(i * zgroups, zgroups)], sems.at[2]))
            return c

        lax.fori_loop(tail_ref[0], x_ref.shape[0] // zgroups, tail, 0)

    start = lambda cp: cp.start()
    wait = lambda cp: cp.wait()

    def wait_tile(t):
        slot = t % 2
        _run_pieces(trows_ref[t], tile_groups, lambda done, size: pltpu.make_async_copy(
            buf.at[slot, pl.ds(0, size)], x_ref.at[pl.ds(0, size)], sems.at[slot]).wait())

    @pl.when(j == 0)
    def _():
        zbuf[...] = jnp.zeros(zbuf.shape, zbuf.dtype)
        zero_fill(start)

    @pl.when(j >= 2)
    def _():
        wait_tile(j - 2)

    perm = _onehot_bands(posw_ref[...].T[:TOP_K], None)
    xs = jnp.dot(perm, h2_ref[...], preferred_element_type=F32)
    buf[j % 2] = _pack_exact(xs[:, :half], xs[:, half:]).reshape(tile_groups, RUN_ALIGN, half)
    tile_runs(j, start)

    @pl.when(j == nt - 1)
    def _():
        if nt >= 2:
            wait_tile(j - 1)
        wait_tile(j)
        zero_fill(wait)


def _dispatch(h2, posw, n_rows, tile_off, expert_off, run_n, tile_rows, zoff, zn, tail, tm):
    T = h2.shape[0]
    nt = T // tm
    G = RUN_ALIGN
    assert n_rows % EXPERT_BLOCK == 0 and EXPERT_BLOCK % G == 0 and TILE_ROWS % G == 0
    x_rows = pl.pallas_call(
        functools.partial(_dispatch_kernel, nt=nt),
        grid_spec=pltpu.PrefetchScalarGridSpec(
            num_scalar_prefetch=7,
            grid=(nt,),
            in_specs=[pl.BlockSpec((tm, D_MODEL), lambda i, *_: (i, 0)),
                      pl.BlockSpec((tm, LANES), lambda i, *_: (i, 0))],
            out_specs=pl.BlockSpec(memory_space=pl.ANY),
            scratch_shapes=[pltpu.VMEM((2, TILE_ROWS // G, G, D_MODEL // 2), jnp.uint32),
                            pltpu.VMEM((EXPERT_BLOCK // G, G, D_MODEL // 2), jnp.uint32),
                            pltpu.SemaphoreType.DMA((3,))]),
        out_shape=jax.ShapeDtypeStruct((n_rows // G, G, D_MODEL // 2), jnp.uint32),
        compiler_params=_params(("arbitrary",), VMEM_LIMIT),
        name="dispatch",
    )(tile_off, expert_off, run_n, tile_rows, zoff, zn, tail, h2, posw)
    return x_rows.reshape(n_rows, D_MODEL // 2)


def _expert_kernel(be_ref, rows_ref, slot_ref, next_ref, x_ref, wg_ref, bg_ref, wu_ref, bu_ref, wd_ref, bd_ref, y_ref,
                   w_in, wg_s, wu_s, wd_s, sems):
    i = pl.program_id(0)
    half = D_MODEL // 2
    M = x_ref.shape[0]
    rows = rows_ref[i]
    e = be_ref[i]
    prev = be_ref[jnp.maximum(i - 1, 0)]

    def weight_copies(expert, slot):
        return [pltpu.make_async_copy(w_hbm.at[expert], w_in.at[slot, k], sems.at[slot, k])
                for k, w_hbm in enumerate((wg_ref, wu_ref, wd_ref))]

    @pl.when((rows > 0) & ((i == 0) | (e != prev)))
    def _():
        slot = slot_ref[i]

        @pl.when(i == 0)
        def _():
            for cp in weight_copies(e, slot):
                cp.start()

        nxt = next_ref[i]

        @pl.when(nxt >= 0)
        def _():
            for cp in weight_copies(nxt, 1 - slot):
                cp.start()

        for cp in weight_copies(e, slot):
            cp.wait()
        for k, dst in enumerate((wg_s, wu_s, wd_s)):
            dst[...] = w_in[slot, k].astype(BF16)

    def compute(r):
        x_lo, x_hi = _unpack(x_ref[:r, :])

        def proj_in(w_s, b_ref):
            return (jnp.dot(x_lo, w_s[:half, :], preferred_element_type=F32)
                    + jnp.dot(x_hi, w_s[half:, :], preferred_element_type=F32) + b_ref[0])

        gate = jnp.minimum(proj_in(wg_s, bg_ref), SWIGLU_LIMIT)
        up = jnp.clip(proj_in(wu_s, bu_ref), -SWIGLU_LIMIT, SWIGLU_LIMIT)
        glu = gate * _sigmoid(gate * SWIGLU_ALPHA)
        act = ((up + 1.0) * glu).astype(BF16)
        y = jnp.dot(act, wd_s[...], preferred_element_type=F32) + bd_ref[0]
        y_ref[:r, :] = y
        if r < M:
            y_ref[r:, :] = jnp.zeros((M - r, D_MODEL), y_ref.dtype)

    for r in range(EXPERT_SUB, M + 1, EXPERT_SUB):
        pl.when(rows == r)(functools.partial(compute, r))

    @pl.when(rows == 0)
    def _():
        y_ref[...] = jnp.zeros(y_ref.shape, y_ref.dtype)


def _experts(blk_exp, blk_rows, blk_slot, blk_next, x_rows, n_pad, w_g, b_g, w_u, b_u, w_d, b_d):
    M = EXPERT_BLOCK
    assert D_FF == D_MODEL
    bspec = lambda n: pl.BlockSpec((1, 1, n), lambda i, be, *_: (be[i], 0, 0))
    wspec = pl.BlockSpec(memory_space=pl.ANY)
    return pl.pallas_call(
        _expert_kernel,
        grid_spec=pltpu.PrefetchScalarGridSpec(
            num_scalar_prefetch=4,
            grid=(n_pad // M,),
            in_specs=[pl.BlockSpec((M, D_MODEL // 2), lambda i, *_: (i, 0)),
                      wspec, bspec(D_FF), wspec, bspec(D_FF), wspec, bspec(D_MODEL)],
            out_specs=pl.BlockSpec((M, D_MODEL), lambda i, *_: (i, 0)),
            scratch_shapes=[pltpu.VMEM((2, 3, D_MODEL, D_FF), F32),
                            pltpu.VMEM((D_MODEL, D_FF), BF16),
                            pltpu.VMEM((D_MODEL, D_FF), BF16),
                            pltpu.VMEM((D_FF, D_MODEL), BF16),
                            pltpu.SemaphoreType.DMA((2, 3))]),
        out_shape=jax.ShapeDtypeStruct((n_pad, D_MODEL), F32),
        compiler_params=_params(("arbitrary",), VMEM_LIMIT),
        name="experts",
    )(blk_exp, blk_rows, blk_slot, blk_next, x_rows, w_g, b_g, w_u, b_u, w_d, b_d)


def _final_kernel(toff_ref, eoff_ref, n_ref, trows_ref, x1_ref, posw_ref, p_ref, gpg_ref, wpg_ref, wpp_ref, gpp_ref,
                  gfin_ref, y_ref, o_ref, buf, xmid, sems, *, nt):
    tm = x1_ref.shape[0]
    step = pl.program_id(0)
    j = jnp.minimum(step, nt - 1)

    def tile_runs(t, act):
        slot = t % 2

        def body(e, c):
            r = t * N_EXPERTS + e
            t0, s0 = toff_ref[r], eoff_ref[r]
            _run_pieces(n_ref[r], tm // RUN_ALIGN, lambda done, size: act(pltpu.make_async_copy(
                y_ref.at[pl.ds(s0 + done, size)], buf.at[slot, pl.ds(t0 + done, size)], sems.at[slot])))
            return c

        lax.fori_loop(0, N_EXPERTS, body, 0, unroll=RUN_UNROLL)

    @pl.when(step == 0)
    def _():
        buf[...] = jnp.zeros(buf.shape, buf.dtype)
        xmid[...] = jnp.zeros(xmid.shape, xmid.dtype)
        tile_runs(0, lambda cp: cp.start())

    @pl.when(step + 1 < nt)
    def _():
        tile_runs(step + 1, lambda cp: cp.start())

    @pl.when(step < nt)
    def _():
        _run_pieces(trows_ref[j], buf.shape[1], lambda done, size: pltpu.make_async_copy(
            y_ref.at[pl.ds(0, size)], buf.at[j % 2, pl.ds(0, size)], sems.at[j % 2]).wait())

    x = xmid[...]
    pp = jnp.dot(p_ref[...].astype(BF16), wpp_ref[...], preferred_element_type=F32)
    pg = _sigmoid(jnp.dot(_rms(x, gpg_ref[...]).astype(BF16), wpg_ref[...], preferred_element_type=F32))
    x = x + pg * _rms(pp, gpp_ref[...])
    o_ref[...] = _rms(x, gfin_ref[...])

    posw_t = posw_ref[...].T
    comb_t = _onehot_bands(posw_t[:TOP_K], posw_t[TOP_K:2 * TOP_K].astype(BF16))
    sure = TOP_K * tm
    sure_g = sure // RUN_ALIGN
    head = buf[j % 2, :sure_g].reshape(sure, D_MODEL)
    tail = buf[j % 2, sure_g:].reshape(TILE_ROWS - sure, D_MODEL)
    live = lax.broadcasted_iota(jnp.int32, tail.shape, 0) < trows_ref[j] * RUN_ALIGN - sure
    y = jnp.concatenate([head, jnp.where(live, tail, 0.0)], axis=0).astype(BF16)
    xmid[...] = x1_ref[...] + lax.dot_general(comb_t, y, TN, preferred_element_type=F32)


def _final(tile_off, expert_off, run_n, tile_rows, x1, y_rows, posw, p2, g_pg, w_pg, w_pp, g_pp, g_fin, tm):
    T = x1.shape[0]
    nt = T // tm
    full = lambda shape: pl.BlockSpec(shape, lambda i, *_: (0, 0))
    this = lambda w: pl.BlockSpec((tm, w), lambda i, *_: (jnp.minimum(i, nt - 1), 0))
    prev = lambda w: pl.BlockSpec((tm, w), lambda i, *_: (jnp.maximum(i - 1, 0), 0))
    return pl.pallas_call(
        functools.partial(_final_kernel, nt=nt),
        grid_spec=pltpu.PrefetchScalarGridSpec(
            num_scalar_prefetch=4,
            grid=(nt + 1,),
            in_specs=[this(D_MODEL), this(LANES), prev(PLE_DIM),
                      full((1, D_MODEL)), full((D_MODEL, D_MODEL)), full((PLE_DIM, D_MODEL)),
                      full((1, D_MODEL)), full((1, D_MODEL)),
                      pl.BlockSpec(memory_space=pl.ANY)],
            out_specs=prev(D_MODEL),
            scratch_shapes=[pltpu.VMEM((2, TILE_ROWS // RUN_ALIGN, RUN_ALIGN, D_MODEL), F32),
                            pltpu.VMEM((tm, D_MODEL), F32),
                            pltpu.SemaphoreType.DMA((2,))]),
        out_shape=jax.ShapeDtypeStruct((T, D_MODEL), F32),
        compiler_params=_params(("arbitrary",), VMEM_LIMIT),
        name="final",
    )(tile_off, expert_off, run_n, tile_rows, x1, posw, p2, g_pg, w_pg, w_pp, g_pp, g_fin,
      y_rows.reshape(-1, RUN_ALIGN, D_MODEL))


def _split_w_in(w_in):
    sizes = (GLA_QK, GLA_QK, GLA_V, GLA_V, GLA_RANK, MOBA_W, MOBA_W, MOBA_W, D_MODEL, D_MODEL)
    offs = [0]
    for s in sizes:
        offs.append(offs[-1] + s)
    main = jnp.concatenate([w_in[:, :offs[4]], w_in[:, offs[5]:]], axis=1).astype(BF16)
    alow = jnp.pad(w_in[:, offs[4]:offs[5]], ((0, 0), (0, LANES - GLA_RANK))).astype(BF16)
    return main, alow


def _layer(x2, p2, bias, B, S, g_mix, w_in, w_a2, b_a2, g_gla_out, w_up_gla, w_up_moba, w_o, g_ffn, w_router,
           b_router, w_e_gate, b_e_gate, w_e_up, b_e_up, w_e_down, b_e_down, g_ple_gate, w_ple_gate, w_ple_proj,
           g_ple_proj, g_final):
    T = B * S
    row = lambda v: v.reshape(1, -1).astype(F32)
    w_main, w_alow = _split_w_in(w_in)
    colscale = jnp.ones((D_PROJ,), F32)
    colscale = colscale.at[OFF_QA:OFF_QA + GLA_QK].set(GLA_DK ** -0.5)
    colscale = colscale.at[OFF_QB:OFF_QB + MOBA_W].set(MOBA_HD ** -0.5 * LOG2E)
    w_a2p = jnp.pad(w_a2, ((0, LANES - GLA_RANK), (0, 0))).astype(BF16)
    proj, glog = _inproj(x2, row(g_mix), w_main, colscale.reshape(1, -1), w_alow, w_a2p, row(b_a2))

    o_a = _gla(proj, glog, row(g_gla_out), B, S)
    o_b = _moba(proj, bias, B, S)

    w_r = jnp.pad(w_router.astype(F32), ((0, 0), (0, LANES - N_EXPERTS)))
    w_r_hi = w_r.astype(BF16)
    w_r2 = jnp.concatenate([w_r_hi, (w_r - w_r_hi.astype(F32)).astype(BF16)], axis=1)
    b_r = jnp.pad(b_router.astype(F32), (0, LANES - N_EXPERTS)).reshape(1, -1)
    tm = TOKEN_TILE
    nt = T // tm
    x1, h2, logits = _merge(o_a, o_b, proj, x2, w_up_gla.astype(BF16), w_up_moba.astype(BF16),
                            w_o.astype(BF16), row(g_ffn), w_r2, b_r, MERGE_TILE)
    posw, cnt_t, carry = _router(logits, tm)

    M = EXPERT_BLOCK
    A = nt * TILE_ROWS
    n_pad = (-(-A // M)) * M + N_EXPERTS * M
    n_blk = n_pad // M
    cnt_t = cnt_t[:, 0, :N_EXPERTS].astype(jnp.int32)
    carry = carry[:, 0, :N_EXPERTS].astype(jnp.int32)
    counts = carry[-1] + cnt_t[-1]
    padded = (counts + M - 1) // M * M
    pad_end = jnp.cumsum(padded)
    pad_start = pad_end - padded
    blk_exp = jnp.minimum(jnp.sum(pad_end[None, :] <= (jnp.arange(n_blk, dtype=jnp.int32) * M)[:, None], axis=1),
                          N_EXPERTS - 1).astype(jnp.int32)
    n_used = (pad_end[-1:] // M).astype(jnp.int32)
    blk_start = jnp.arange(n_blk, dtype=jnp.int32) * M
    eids = jnp.arange(N_EXPERTS, dtype=jnp.int32)

    def per_block(per_expert):
        return jnp.sum(jnp.where(blk_exp[:, None] == eids[None, :], per_expert[None, :], 0), axis=1).astype(jnp.int32)

    blk_rows = jnp.clip(per_block(pad_start + counts) - blk_start, 0, M)
    blk_rows = jnp.where(blk_start < pad_end[-1], (blk_rows + EXPERT_SUB - 1) // EXPERT_SUB * EXPERT_SUB, 0)
    has_rows = counts > 0
    blk_slot = per_block((jnp.cumsum(has_rows) - 1) % 2)
    later = jnp.where((eids[None, :] > eids[:, None]) & has_rows[None, :], eids[None, :], N_EXPERTS)
    next_exp = jnp.min(later, axis=1)
    blk_next = per_block(jnp.where(next_exp < N_EXPERTS, next_exp, -1))
    G = RUN_ALIGN
    tile_off = (jnp.cumsum(cnt_t, axis=1) - cnt_t).reshape(-1) // G
    expert_off = (carry + pad_start[None, :]).reshape(-1) // G
    run_n = cnt_t.reshape(-1) // G
    tile_rows = jnp.sum(cnt_t, axis=1) // G
    x_rows = _dispatch(h2, posw, n_pad, tile_off, expert_off, run_n, tile_rows, (pad_start + counts) // G,
                       (padded - counts) // G, n_used, tm)
    y_rows = _experts(blk_exp, blk_rows.astype(jnp.int32), blk_slot, blk_next, x_rows, n_pad,
                      w_e_gate, b_e_gate.reshape(N_EXPERTS, 1, D_FF),
                      w_e_up, b_e_up.reshape(N_EXPERTS, 1, D_FF), w_e_down,
                      b_e_down.reshape(N_EXPERTS, 1, D_MODEL))
    return _final(tile_off, expert_off, run_n, tile_rows, x1, y_rows, posw, p2, row(g_ple_gate),
                  w_ple_gate.astype(BF16), w_ple_proj.astype(BF16), row(g_ple_proj), row(g_final), tm)


def kernel(x, p, rel_bias, g_mix, w_in, w_a2, b_a2, g_gla_out, w_up_gla, w_up_moba, w_o, g_ffn, w_router, b_router,
           w_e_gate, b_e_gate, w_e_up, b_e_up, w_e_down, b_e_down, g_ple_gate, w_ple_gate, w_ple_proj, g_ple_proj,
           g_final):
    B, S, D = x.shape
    assert D == D_MODEL and S % MOBA_BLOCK == 0 and S % GLA_CHUNK == 0 and p.shape[0] == 1
    bias = _bias_tiles(rel_bias, S // MOBA_BLOCK)
    out = _layer(x.reshape(B * S, D), p[0].reshape(B * S, PLE_DIM), bias, B, S,
                 g_mix[0], w_in[0], w_a2[0], b_a2[0], g_gla_out[0], w_up_gla[0], w_up_moba[0], w_o[0], g_ffn[0],
                 w_router[0], b_router[0], w_e_gate[0], b_e_gate[0], w_e_up[0], b_e_up[0], w_e_down[0],
                 b_e_down[0], g_ple_gate[0], w_ple_gate[0], w_ple_proj[0], g_ple_proj[0], g_final)
    return out.reshape(B, S, D)
```

```python
import functools
import math

import jax
import jax.numpy as jnp
from jax import lax
from jax.experimental import pallas as pl
from jax.experimental.pallas import tpu as pltpu

F32 = jnp.float32
BF16 = jnp.bfloat16

D_MODEL = 1024
PLE_DIM = 256
GLA_HEADS = 4
GLA_DK = 128
GLA_DV = 256
GLA_RANK = 16
GLA_TAU = 16.0
GLA_QK = GLA_HEADS * GLA_DK
GLA_V = GLA_HEADS * GLA_DV
MOBA_HEADS = 8
MOBA_HD = 128
MOBA_BLOCK = 256
MOBA_TOPK = 3
MOBA_W = MOBA_HEADS * MOBA_HD
REL_BUCKETS = 32
REL_MAX_DIST = 4096
N_EXPERTS = 32
TOP_K = 4
D_FF = 1024
SWIGLU_LIMIT = 7.0
SWIGLU_ALPHA = 1.702
EPS = 1e-6

LANES = 128
NEG = -1e30
LOG2E = math.log2(math.e)
VMEM_LIMIT = 56 * 1024 * 1024

OFF_QA, OFF_KA, OFF_VA, OFF_RA = 0, 512, 1024, 2048
OFF_QB, OFF_KB, OFF_VB, OFF_GA, OFF_GB = 3072, 4096, 5120, 6144, 7168
D_PROJ = 8192

GLA_CHUNK = 128
EXPERT_BLOCK = 512
EXPERT_SUB = 128
TOKEN_TILE = 256
RUN_ALIGN = 8
TILE_ROWS = TOP_K * TOKEN_TILE + N_EXPERTS * RUN_ALIGN
RUN_UNROLL = 8
MERGE_TILE = 512
FINAL_SLOTS = 3

NT = (((1,), (1,)), ((), ()))
TN = (((0,), (0,)), ((), ()))


def _params(sem, vmem=None):
    return pltpu.CompilerParams(dimension_semantics=sem, vmem_limit_bytes=vmem)


def _rms(x, g):
    return x * lax.rsqrt(jnp.mean(x * x, axis=-1, keepdims=True) + EPS) * g


def _sigmoid(x):
    return 1.0 / (1.0 + jnp.exp(-x))


def _bucket_of(n):
    max_exact = REL_BUCKETS // 2
    if n < max_exact:
        return n
    return min(max_exact + int(math.log(n / max_exact) / math.log(REL_MAX_DIST / max_exact)
                               * (REL_BUCKETS - max_exact)), REL_BUCKETS - 1)


def _bias_kernel(tab_ref, bkt_ref, o_ref, *, nblk):
    strip = 8
    m = pl.program_id(0)

    def fill(lo, hi):
        def body(s, carry):
            r0 = pl.multiple_of(s * strip, strip)
            b = bkt_ref[0, pl.ds(r0, strip), :]
            accs = [jnp.zeros(b.shape, F32) for _ in range(MOBA_HEADS)]
            for bb in range(lo, hi + 1):
                hit = b == bb
                for h in range(MOBA_HEADS):
                    accs[h] = jnp.where(hit, tab_ref[h, bb], accs[h])
            for h in range(MOBA_HEADS):
                o_ref[h, 0, pl.ds(r0, strip), :] = jnp.where(b < 0, NEG, accs[h])
            return carry

        lax.fori_loop(0, MOBA_BLOCK // strip, body, 0)

    for mm in range(nblk):
        lo = max(_bucket_of(max(mm * MOBA_BLOCK - (MOBA_BLOCK - 1), 0)) - 1, 0)
        hi = min(_bucket_of(mm * MOBA_BLOCK + MOBA_BLOCK - 1) + 1, REL_BUCKETS - 1)
        pl.when(m == mm)(functools.partial(fill, lo, hi))


def _bias_tiles(rel_bias, nblk):
    i = jnp.arange(MOBA_BLOCK, dtype=jnp.int32)
    dist = (jnp.arange(nblk, dtype=jnp.int32)[:, None, None] * MOBA_BLOCK + i[None, :, None] - i[None, None, :])
    n = jnp.maximum(dist, 0)
    max_exact = REL_BUCKETS // 2
    nf = jnp.maximum(n, 1).astype(F32)
    large = max_exact + (jnp.log(nf / max_exact) / math.log(REL_MAX_DIST / max_exact)
                         * (REL_BUCKETS - max_exact)).astype(jnp.int32)
    large = jnp.minimum(large, REL_BUCKETS - 1)
    bkt = jnp.where(dist < 0, -1, jnp.where(n < max_exact, n, large)).astype(jnp.int32)
    tab = rel_bias.astype(F32).T * LOG2E
    return pl.pallas_call(
        functools.partial(_bias_kernel, nblk=nblk),
        grid=(nblk,),
        in_specs=[pl.BlockSpec(memory_space=pltpu.SMEM),
                  pl.BlockSpec((1, MOBA_BLOCK, MOBA_BLOCK), lambda m: (m, 0, 0))],
        out_specs=pl.BlockSpec((MOBA_HEADS, 1, MOBA_BLOCK, MOBA_BLOCK), lambda m: (0, m, 0, 0)),
        out_shape=jax.ShapeDtypeStruct((MOBA_HEADS, nblk, MOBA_BLOCK, MOBA_BLOCK), F32),
        compiler_params=_params(("arbitrary",)),
        name="bias_tiles",
    )(tab, bkt)


def _inproj_kernel(x0_ref, xn_ref, g_ref, w_ref, cs_ref, wal_ref, wa2_ref, ba2_ref, o_ref, glog_ref, h_scr, al_scr,
                   *, n_j):
    i, j = pl.program_id(0), pl.program_id(1)
    tm = xn_ref.shape[0]
    rs = tm // n_j

    def gate_logits(a_low):
        a = jnp.dot(a_low, wa2_ref[...], preferred_element_type=F32) + ba2_ref[...]
        log_sig = jnp.minimum(a, 0.0) - jnp.log1p(jnp.exp(-jnp.abs(a)))
        return log_sig * (1.0 / GLA_TAU)

    def slice_rows(behind):
        return pl.multiple_of(((j + n_j - behind) % n_j) * rs, rs)

    @pl.when((i == 0) & (j == 0))
    def _():
        h0 = _rms(x0_ref[...], g_ref[...]).astype(BF16)
        h_scr[0] = h0
        a_low0 = jnp.dot(h0, wal_ref[...], preferred_element_type=F32).astype(BF16)
        glog_ref[...] = gate_logits(a_low0)
        al_scr[...] = a_low0[(n_j - 2) * rs:(n_j - 1) * rs, :]

    glog_ref[pl.ds(slice_rows(2), rs), :] = gate_logits(al_scr[...])
    slot = jnp.where(j == 0, i, i + 1) % 2
    h_lag = h_scr[slot, pl.ds(slice_rows(1), rs), :]
    al_scr[...] = jnp.dot(h_lag, wal_ref[...], preferred_element_type=F32).astype(BF16)

    acc = jnp.dot(h_scr[i % 2], w_ref[...], preferred_element_type=F32)
    o_ref[...] = (acc * cs_ref[...]).astype(BF16)

    h_scr[(i + 1) % 2, pl.ds(slice_rows(0), rs), :] = _rms(xn_ref[pl.ds(slice_rows(0), rs), :],
                                                             g_ref[...]).astype(BF16)


def _inproj(x2, g_mix, w_main, colscale, w_alow, w_a2p, b_a2, tm=1024, tn=2048):
    T = x2.shape[0]
    n_i, n_j = T // tm, D_PROJ // tn
    assert n_i >= 2 and tm % (8 * n_j) == 0
    return pl.pallas_call(
        functools.partial(_inproj_kernel, n_j=n_j),
        grid=(n_i, n_j),
        in_specs=[pl.BlockSpec((tm, D_MODEL), lambda i, j: (0, 0)),
                  pl.BlockSpec((tm, D_MODEL), lambda i, j: (jnp.minimum(i + 1, n_i - 1), 0)),
                  pl.BlockSpec((1, D_MODEL), lambda i, j: (0, 0)),
                  pl.BlockSpec((D_MODEL, tn), lambda i, j: (0, j)),
                  pl.BlockSpec((1, tn), lambda i, j: (0, j)),
                  pl.BlockSpec((D_MODEL, LANES), lambda i, j: (0, 0)),
                  pl.BlockSpec((LANES, GLA_QK), lambda i, j: (0, 0)),
                  pl.BlockSpec((1, GLA_QK), lambda i, j: (0, 0))],
        out_specs=[pl.BlockSpec((tm, tn), lambda i, j: (i, j)),
                   pl.BlockSpec((tm, GLA_QK), lambda i, j: (jnp.minimum(i + jnp.minimum(j // 2, 1), n_i - 1), 0))],
        out_shape=[jax.ShapeDtypeStruct((T, D_PROJ), BF16),
                   jax.ShapeDtypeStruct((T, GLA_QK), F32)],
        scratch_shapes=[pltpu.VMEM((2, tm, D_MODEL), BF16), pltpu.VMEM((tm // n_j, LANES), BF16)],
        compiler_params=_params(("arbitrary", "arbitrary"), VMEM_LIMIT),
        name="inproj",
    )(x2, x2, g_mix, w_main, colscale, w_alow, w_a2p, b_a2)


def _gla_kernel(q_ref, k_ref, v_ref, r_ref, g_ref, gout_ref, o_ref, st_ref):
    C = GLA_CHUNK

    @pl.when(pl.program_id(1) == 0)
    def _():
        st_ref[...] = jnp.zeros(st_ref.shape, F32)

    nb = q_ref.shape[0]
    row = lax.broadcasted_iota(jnp.int32, (C, C), 0)
    col = lax.broadcasted_iota(jnp.int32, (C, C), 1)
    causal = col <= row
    ltri = causal.astype(BF16)
    mid = C // 2
    pairs = [(b, h) for b in range(nb) for h in range(GLA_HEADS)]
    ks = lambda h: slice(h * GLA_DK, (h + 1) * GLA_DK)
    vs = lambda h: slice(h * GLA_DV, (h + 1) * GLA_DV)
    gate = {}
    for b, h in pairs:
        r = r_ref[b, :, vs(h)].astype(F32)
        gate[b, h] = r * _sigmoid(r)
    G = []
    for b in range(nb):
        g = g_ref[b]
        g_hi = g.astype(BF16)
        g_lo = (g - g_hi.astype(F32)).astype(BF16)
        G.append(jnp.dot(ltri, g_hi, preferred_element_type=F32) + jnp.dot(ltri, g_lo, preferred_element_type=F32))
    Gh = {(b, h): G[b][:, ks(h)] for b, h in pairs}
    qh = {(b, h): q_ref[b, :, ks(h)].astype(F32) for b, h in pairs}
    kh = {(b, h): k_ref[b, :, ks(h)].astype(F32) for b, h in pairs}
    g_mid = {p: Gh[p][mid:mid + 1, :] for p in pairs}
    g_last = {p: Gh[p][C - 1:C, :] for p in pairs}
    A = {p: lax.dot_general((qh[p] * jnp.exp(Gh[p] - g_mid[p])).astype(BF16),
                            (kh[p] * jnp.exp(g_mid[p] - Gh[p])).astype(BF16), NT, preferred_element_type=F32)
         for p in pairs}
    st = {p: st_ref[p[0], p[1]] for p in pairs}
    inter = {p: lax.dot_general((qh[p] * jnp.exp(Gh[p])).astype(BF16), st[p].astype(BF16), NT,
                                preferred_element_type=F32) for p in pairs}
    for b, h in pairs:
        p = (b, h)
        k_d = (kh[p] * jnp.exp(g_last[p] - Gh[p])).astype(BF16)
        st_ref[b, h] = jnp.exp(g_last[p]) * st[p] + lax.dot_general(v_ref[b, :, vs(h)], k_d, TN,
                                                                     preferred_element_type=F32)
    intra = {(b, h): jnp.dot(jnp.where(causal, A[b, h], 0.0).astype(BF16), v_ref[b, :, vs(h)],
                             preferred_element_type=F32) for b, h in pairs}
    for b, h in pairs:
        o = inter[b, h] + intra[b, h]
        o_ref[b, :, vs(h)] = (_rms(o, gout_ref[...]) * gate[b, h]).astype(BF16)


def _gla(proj, glog, g_gla_out, B, S, nb=4):
    C = GLA_CHUNK
    assert B % nb == 0
    proj3 = proj.reshape(B, S, D_PROJ)
    glog3 = glog.reshape(B, S, GLA_QK)
    spec = lambda w, off: pl.BlockSpec((nb, C, w), lambda b, c: (b, c, off // w))
    out = pl.pallas_call(
        _gla_kernel,
        grid=(B // nb, S // C),
        in_specs=[spec(GLA_QK, OFF_QA), spec(GLA_QK, OFF_KA), spec(GLA_V, OFF_VA), spec(GLA_V, OFF_RA),
                  spec(GLA_QK, 0), pl.BlockSpec((1, GLA_DV), lambda b, c: (0, 0))],
        out_specs=spec(GLA_V, 0),
        out_shape=jax.ShapeDtypeStruct((B, S, GLA_V), BF16),
        scratch_shapes=[pltpu.VMEM((nb, GLA_HEADS, GLA_DV, GLA_DK), F32)],
        compiler_params=_params(("arbitrary", "arbitrary")),
        name="gla",
    )(proj3, proj3, proj3, proj3, glog3, g_gla_out)
    return out.reshape(B * S, GLA_V)


def _moba_kernel(q_ref, k_ref, v_ref, bias_ref, o_ref, ka_scr, va_scr, lg_scr, mx_scr, qa_scr, *, nblk):
    BLK, HD = MOBA_BLOCK, MOBA_HD
    S = nblk * BLK
    units = range(q_ref.shape[0])

    @pl.when((pl.program_id(0) == 0) & (pl.program_id(1) == 0))
    def _():
        blk = lax.broadcasted_iota(jnp.int32, (S, HD), 0) // BLK
        lane = lax.broadcasted_iota(jnp.int32, (S, HD), 1)
        for u in units:
            ka_scr[u, :, HD:] = (lane == blk).astype(BF16)
            va_scr[u, :, HD:] = (lane == 0).astype(BF16)

    n_plain = min(MOBA_TOPK + 1, nblk)
    n_late = nblk - n_plain
    n_tiles = nblk * (nblk + 1) // 2

    def tile_id(cc, j):
        return cc * (cc + 1) // 2 + j

    def rows(cc):
        return slice(cc * BLK, (cc + 1) * BLK)

    def stage1(u, cc, q_in, keys):
        mx = None
        for j in range(cc + 1):
            lg = lax.dot_general(q_in, keys(j), NT, preferred_element_type=F32) + bias_ref[0, cc - j]
            lg_scr[u * n_tiles + tile_id(cc, j)] = lg
            t = jnp.maximum(lg[:, :LANES], lg[:, LANES:])
            mx = t if mx is None else jnp.maximum(mx, t)
            if j == cc:
                mx_scr[u * nblk + cc] = mx
            yield

    def stage2(u, cc):
        m = mx_scr[u * nblk + cc].max(axis=-1, keepdims=True)
        acc = jnp.zeros((BLK, 2 * HD), F32)
        for j in range(cc + 1):
            p = jnp.exp2(lg_scr[u * n_tiles + tile_id(cc, j)] - m).astype(BF16)
            acc = acc + jnp.dot(p, va_scr[u, rows(j), :], preferred_element_type=F32)
            if j == cc:
                o_ref[u, rows(cc), :] = (acc[:, :HD] / acc[:, HD:HD + 1]).astype(BF16)
            yield

    def drain(gen):
        for _ in gen:
            pass

    def interleave(main, side, n_main, n_side):
        side_steps = (s for g in side for s in g)
        done = 0
        for i, _ in enumerate(s for g in main for s in g):
            assert n_main > 0
            want = (i + 1) * n_side // n_main
            while done < want and next(side_steps, "end") != "end":
                done += 1
        drain(side_steps)

    plain = {u: [stage1(u, cc, q_ref[u, rows(cc), :], lambda j, u=u: k_ref[u, rows(j), :]) for cc in range(n_plain)]
             for u in units}
    if n_late:
        pens = []
        for u in units:
            next(plain[u][0])
            ksum = [k_ref[u, rows(j), :].astype(F32).reshape(BLK // 8, 8, HD).sum(axis=0).sum(axis=0, keepdims=True)
                    for j in range(nblk)]
            kmean = jnp.concatenate(ksum, axis=0) * (1.0 / BLK)
            km_hi = kmean.astype(BF16)
            km_lo = (kmean - km_hi.astype(F32)).astype(BF16)
            km2 = jnp.concatenate([km_hi, km_lo], axis=0)
            for cc in range(n_plain, nblk):
                q = q_ref[u, rows(cc), :]
                s2 = lax.dot_general(km2, q, NT, preferred_element_type=F32)
                pens.append((u, cc, q, s2[:nblk] + s2[nblk:]))
        for u in units:
            for g in plain[u][:2]:
                drain(g)
        for u, cc, q, s in pens:
            ji = lax.broadcasted_iota(jnp.int32, s.shape, 0)
            cnt = jnp.zeros(s.shape, F32)
            for jp in range(cc):
                sj = s[jp:jp + 1, :]
                beats = (sj > s) | ((sj == s) & (jp < ji))
                cnt = cnt + beats.astype(F32)
            pen = jnp.where((ji < cc) & (cnt >= MOBA_TOPK), NEG, 0.0)
            pen_t = jnp.concatenate([pen, jnp.zeros((HD - nblk, BLK), F32)], axis=0).T
            qa_scr[u * n_late + cc - n_plain] = jnp.concatenate([q, pen_t.astype(BF16)], axis=1)
    for u in units:
        ka_scr[u, :, :HD] = k_ref[u]
        va_scr[u, :, :HD] = v_ref[u]
    for u in units:
        for g in plain[u]:
            drain(g)

    @pl.when(pl.program_id(0) >= 0)
    def _():
        late = [stage1(u, cc, qa_scr[u * n_late + cc - n_plain], lambda j, u=u: ka_scr[u, rows(j), :])
                for u in units for cc in range(n_plain, nblk)]
        early = [stage2(u, cc) for u in units for cc in range(n_plain)]
        if late:
            interleave(late, early, len(units) * (n_tiles - tile_id(n_plain, 0)), len(units) * tile_id(n_plain, 0))
        else:
            for g in early:
                drain(g)

    @pl.when(pl.program_id(0) >= 0)
    def _():
        for u in units:
            for cc in range(n_plain, nblk):
                drain(stage2(u, cc))


def _moba(proj, bias, B, S, nb=2):
    BLK = MOBA_BLOCK
    nblk = S // BLK
    H = MOBA_HEADS
    assert nblk <= MOBA_HD and B % nb == 0
    proj3 = proj.reshape(B, S, D_PROJ)
    spec = lambda off: pl.BlockSpec((nb, S, MOBA_HD), lambda h, b: (b, 0, off // MOBA_HD + h))
    out = pl.pallas_call(
        functools.partial(_moba_kernel, nblk=nblk),
        grid=(H, B // nb),
        in_specs=[spec(OFF_QB), spec(OFF_KB), spec(OFF_VB),
                  pl.BlockSpec((1, nblk, BLK, BLK), lambda h, b: (h, 0, 0, 0))],
        out_specs=spec(0),
        out_shape=jax.ShapeDtypeStruct((B, S, MOBA_W), BF16),
        scratch_shapes=[pltpu.VMEM((nb, S, 2 * MOBA_HD), BF16), pltpu.VMEM((nb, S, 2 * MOBA_HD), BF16),
                        pltpu.VMEM((nb * nblk * (nblk + 1) // 2, BLK, BLK), F32),
                        pltpu.VMEM((nb * nblk, BLK, LANES), F32),
                        pltpu.VMEM((nb * max(nblk - MOBA_TOPK - 1, 1), BLK, 2 * MOBA_HD), BF16)],
        compiler_params=_params(("arbitrary", "arbitrary"), VMEM_LIMIT),
        name="moba",
    )(proj3, proj3, proj3, bias)
    return out.reshape(B * S, MOBA_W)


def _pack_exact(lo, hi):
    lo_b = lax.bitcast_convert_type(lo, jnp.uint32)
    hi_b = lax.bitcast_convert_type(hi, jnp.uint32)
    return (lo_b >> 16) | (hi_b & jnp.uint32(0xFFFF0000))


def _unpack(w):
    lo = lax.bitcast_convert_type(w << 16, F32)
    hi = lax.bitcast_convert_type(w & jnp.uint32(0xFFFF0000), F32)
    return lo.astype(BF16), hi.astype(BF16)


def _merge_kernel(oa_ref, ob_ref, ga_ref, gb_ref, x_ref, wua_ref, wub_ref, wo_ref, gffn_ref, wr_ref, br_ref,
                  x1_ref, h2_ref, lg_ref):
    tm = x_ref.shape[0]
    u_a = jnp.dot(oa_ref[...], wua_ref[...], preferred_element_type=F32)
    u_b = jnp.dot(ob_ref[...], wub_ref[...], preferred_element_type=F32)
    y = _sigmoid(ga_ref[...].astype(F32)) * u_a + _sigmoid(gb_ref[...].astype(F32)) * u_b
    x1 = x_ref[...] + jnp.dot(y.astype(BF16), wo_ref[...], preferred_element_type=F32)
    x1_ref[...] = x1
    h2 = _rms(x1, gffn_ref[...])
    h_hi = h2.astype(BF16)
    h2_ref[...] = h_hi
    h_lo = (h2 - h_hi.astype(F32)).astype(BF16)
    r = jnp.dot(jnp.concatenate([h_hi, h_lo], axis=0), wr_ref[...], preferred_element_type=F32)
    lg_ref[...] = r[:tm, :LANES] + r[:tm, LANES:] + r[tm:, :LANES] + br_ref[...]


def _merge(o_a, o_b, proj, x2, w_ua, w_ub, w_o, g_ffn, w_r2, b_r, tm):
    T = x2.shape[0]
    full = lambda shape: pl.BlockSpec(shape, lambda i: (0, 0))
    rowblk = lambda w: pl.BlockSpec((tm, w), lambda i: (i, 0))
    return pl.pallas_call(
        _merge_kernel,
        grid=(T // tm,),
        in_specs=[rowblk(GLA_V), rowblk(MOBA_W),
                  pl.BlockSpec((tm, D_MODEL), lambda i: (i, OFF_GA // D_MODEL)),
                  pl.BlockSpec((tm, D_MODEL), lambda i: (i, OFF_GB // D_MODEL)),
                  rowblk(D_MODEL),
                  full((GLA_V, D_MODEL)), full((MOBA_W, D_MODEL)), full((D_MODEL, D_MODEL)),
                  full((1, D_MODEL)), full((D_MODEL, 2 * LANES)), full((1, LANES))],
        out_specs=[rowblk(D_MODEL), rowblk(D_MODEL), rowblk(LANES)],
        out_shape=[jax.ShapeDtypeStruct((T, D_MODEL), F32),
                   jax.ShapeDtypeStruct((T, D_MODEL), BF16),
                   jax.ShapeDtypeStruct((T, LANES), F32)],
        compiler_params=_params(("arbitrary",), VMEM_LIMIT),
        name="merge",
    )(o_a, o_b, proj, proj, x2, w_ua, w_ub, w_o, g_ffn, w_r2, b_r)


def _router_kernel(lg_ref, posw_ref, cnt_ref, carry_ref, cnt_scr, *, tm):
    rows = lg_ref.shape[0]

    @pl.when(pl.program_id(0) == 0)
    def _():
        cnt_scr[...] = jnp.zeros(cnt_scr.shape, F32)

    lane = lax.broadcasted_iota(jnp.int32, (rows, LANES), 1)
    lane_f = lane.astype(F32)
    work = jnp.where(lane < N_EXPERTS, lg_ref[...], NEG)
    vals, hots = [], []
    for _ in range(TOP_K):
        mx = work.max(axis=-1, keepdims=True)
        idx = jnp.min(jnp.where(work == mx, lane_f, float(LANES)), axis=-1, keepdims=True)
        hot = lane_f == idx
        vals.append(mx)
        hots.append(hot)
        work = jnp.where(hot, 2.0 * NEG, work)
    exps = [jnp.exp(v - vals[0]) for v in vals]
    den = exps[0] + exps[1] + exps[2] + exps[3]
    sel = jnp.zeros((rows, LANES), F32)
    for hot in hots:
        sel = sel + hot.astype(F32)
    row = lax.broadcasted_iota(jnp.int32, (tm, tm), 0)
    col = lax.broadcasted_iota(jnp.int32, (tm, tm), 1)
    below = (col < row).astype(BF16)
    er = lax.broadcasted_iota(jnp.int32, (LANES, LANES), 0)
    ec = lax.broadcasted_iota(jnp.int32, (LANES, LANES), 1)
    before = (er < ec).astype(F32)
    pos_parts = []
    for t in range(rows // tm):
        sel_t = sel[t * tm:(t + 1) * tm]
        local_rank = jnp.dot(below, sel_t.astype(BF16), preferred_element_type=F32)
        cnt_t = sel_t.sum(axis=0, keepdims=True)
        cnt_t = jnp.floor((cnt_t + (RUN_ALIGN - 1.0)) * (1.0 / RUN_ALIGN)) * RUN_ALIGN
        tile_off = jnp.dot(jnp.broadcast_to(cnt_t, (8, LANES)), before, preferred_element_type=F32,
                           precision=lax.Precision.HIGHEST)[0:1]
        pos_parts.append(local_rank + tile_off)
        carry_ref[t] = cnt_scr[...]
        cnt_ref[t] = cnt_t
        cnt_scr[...] = cnt_scr[...] + cnt_t
    pos_all = jnp.concatenate(pos_parts, axis=0)
    posw = jnp.zeros((rows, LANES), F32)
    for k in range(TOP_K):
        pk = jnp.sum(jnp.where(hots[k], pos_all, 0.0), axis=-1, keepdims=True)
        posw = jnp.where(lane == k, pk, posw)
        posw = jnp.where(lane == TOP_K + k, exps[k] / den, posw)
    posw_ref[...] = posw


def _router(logits, tm, tiles_per_step=4):
    T = logits.shape[0]
    nt = T // tm
    rows = tm * tiles_per_step
    tilerow = pl.BlockSpec((tiles_per_step, 1, LANES), lambda i: (i, 0, 0))
    return pl.pallas_call(
        functools.partial(_router_kernel, tm=tm),
        grid=(T // rows,),
        in_specs=[pl.BlockSpec((rows, LANES), lambda i: (i, 0))],
        out_specs=[pl.BlockSpec((rows, LANES), lambda i: (i, 0)), tilerow, tilerow],
        out_shape=[jax.ShapeDtypeStruct((T, LANES), F32),
                   jax.ShapeDtypeStruct((nt, 1, LANES), F32),
                   jax.ShapeDtypeStruct((nt, 1, LANES), F32)],
        scratch_shapes=[pltpu.VMEM((1, LANES), F32)],
        compiler_params=_params(("arbitrary",)),
        name="router",
    )(logits)


def _run_pieces(n, max_n, fn):
    for b in reversed(range(max_n.bit_length())):
        size = 1 << b
        done = n & ~((2 << b) - 1)

        @pl.when((n & size) != 0)
        def _():
            fn(done, size)


def _onehot_bands(pos, val):
    band = 256
    assert TILE_ROWS % band == 0
    n_k, n_tok = pos.shape
    pos_a = jnp.floor(pos * (1.0 / band))
    pos_b = pos - band * pos_a
    sub = lax.broadcasted_iota(jnp.int32, (band, n_tok), 0).astype(F32).astype(BF16)
    zero = jnp.zeros((band, n_tok), BF16)
    bands = []
    for a in range(TILE_ROWS // band):
        want = jnp.where(pos_a == a, pos_b, -1.0).astype(BF16)
        hit = zero
        for k in range(n_k):
            row = jnp.broadcast_to(want[k:k + 1, :], (band, n_tok))
            fill = jnp.ones((band, n_tok), BF16) if val is None else jnp.broadcast_to(val[k:k + 1, :], (band, n_tok))
            hit = hit + jnp.where(sub == row, fill, zero)
        bands.append(hit)
    return jnp.concatenate(bands, axis=0)


def _dispatch_kernel(toff_ref, eoff_ref, n_ref, trows_ref, zoff_ref, zn_ref, tail_ref, h2_ref, posw_ref, x_ref,
                     buf, zbuf, sems, *, nt):
    tm = h2_ref.shape[0]
    half = D_MODEL // 2
    j = pl.program_id(0)
    zgroups = zbuf.shape[0]
    tile_groups = buf.shape[1]

    def tile_runs(t, act):
        slot = t % 2

        def body(e, c):
            r = t * N_EXPERTS + e
            t0, d0 = toff_ref[r], eoff_ref[r]
            _run_pieces(n_ref[r], tm // RUN_ALIGN, lambda done, size: act(pltpu.make_async_copy(
                buf.at[slot, pl.ds(t0 + done, size)], x_ref.at[pl.ds(d0 + done, size)], sems.at[slot])))
            return c

        lax.fori_loop(0, N_EXPERTS, body, 0, unroll=RUN_UNROLL)

    def zero_fill(act):
        def body(e, c):
            d0 = zoff_ref[e]
            _run_pieces(zn_ref[e], zgroups, lambda done, size: act(pltpu.make_async_copy(
                zbuf.at[pl.ds(0, size)], x_ref.at[pl.ds(d0 + done, size)], sems.at[2])))
            return c

        lax.fori_loop(0, N_EXPERTS, body, 0)

        def tail(i, c):
            act(pltpu.make_async_copy(zbuf, x_ref.at[pl.ds(i * zgroups, zgroups)], sems.at[2]))
            return c

        lax.fori_loop(tail_ref[0], x_ref.shape[0] // zgroups, tail, 0)

    start = lambda cp: cp.start()
    wait = lambda cp: cp.wait()

    def wait_tile(t):
        slot = t % 2
        _run_pieces(trows_ref[t], tile_groups, lambda done, size: pltpu.make_async_copy(
            buf.at[slot, pl.ds(0, size)], x_ref.at[pl.ds(0, size)], sems.at[slot]).wait())

    @pl.when(j == 0)
    def _():
        zbuf[...] = jnp.zeros(zbuf.shape, zbuf.dtype)
        zero_fill(start)

    @pl.when(j >= 2)
    def _():
        wait_tile(j - 2)

    perm = _onehot_bands(posw_ref[...].T[:TOP_K], None)
    xs = jnp.dot(perm, h2_ref[...], preferred_element_type=F32)
    buf[j % 2] = _pack_exact(xs[:, :half], xs[:, half:]).reshape(tile_groups, RUN_ALIGN, half)
    tile_runs(j, start)

    @pl.when(j == nt - 1)
    def _():
        if nt >= 2:
            wait_tile(j - 1)
        wait_tile(j)
        zero_fill(wait)


def _dispatch(h2, posw, n_rows, tile_off, expert_off, run_n, tile_rows, zoff, zn, tail, tm):
    T = h2.shape[0]
    nt = T // tm
    G = RUN_ALIGN
    assert n_rows % EXPERT_BLOCK == 0 and EXPERT_BLOCK % G == 0 and TILE_ROWS % G == 0
    x_rows = pl.pallas_call(
        functools.partial(_dispatch_kernel, nt=nt),
        grid_spec=pltpu.PrefetchScalarGridSpec(
            num_scalar_prefetch=7,
            grid=(nt,),
            in_specs=[pl.BlockSpec((tm, D_MODEL), lambda i, *_: (i, 0)),
                      pl.BlockSpec((tm, LANES), lambda i, *_: (i, 0))],
            out_specs=pl.BlockSpec(memory_space=pl.ANY),
            scratch_shapes=[pltpu.VMEM((2, TILE_ROWS // G, G, D_MODEL // 2), jnp.uint32),
                            pltpu.VMEM((EXPERT_BLOCK // G, G, D_MODEL // 2), jnp.uint32),
                            pltpu.SemaphoreType.DMA((3,))]),
        out_shape=jax.ShapeDtypeStruct((n_rows // G, G, D_MODEL // 2), jnp.uint32),
        compiler_params=_params(("arbitrary",), VMEM_LIMIT),
        name="dispatch",
    )(tile_off, expert_off, run_n, tile_rows, zoff, zn, tail, h2, posw)
    return x_rows.reshape(n_rows, D_MODEL // 2)


def _expert_kernel(be_ref, rows_ref, slot_ref, next_ref, x_ref, wg_ref, bg_ref, wu_ref, bu_ref, wd_ref, bd_ref, y_ref,
                   w_in, wg_s, wu_s, wd_s, sems):
    i = pl.program_id(0)
    half = D_MODEL // 2
    M = x_ref.shape[0]
    rows = rows_ref[i]
    e = be_ref[i]
    prev = be_ref[jnp.maximum(i - 1, 0)]

    def weight_copies(expert, slot):
        return [pltpu.make_async_copy(w_hbm.at[expert], w_in.at[slot, k], sems.at[slot, k])
                for k, w_hbm in enumerate((wg_ref, wu_ref, wd_ref))]

    @pl.when((rows > 0) & ((i == 0) | (e != prev)))
    def _():
        slot = slot_ref[i]

        @pl.when(i == 0)
        def _():
            for cp in weight_copies(e, slot):
                cp.start()

        nxt = next_ref[i]

        @pl.when(nxt >= 0)
        def _():
            for cp in weight_copies(nxt, 1 - slot):
                cp.start()

        for cp in weight_copies(e, slot):
            cp.wait()
        for k, dst in enumerate((wg_s, wu_s, wd_s)):
            dst[...] = w_in[slot, k].astype(BF16)

    def compute(r):
        x_lo, x_hi = _unpack(x_ref[:r, :])

        def proj_in(w_s, b_ref):
            return (jnp.dot(x_lo, w_s[:half, :], preferred_element_type=F32)
                    + jnp.dot(x_hi, w_s[half:, :], preferred_element_type=F32) + b_ref[0])

        gate = jnp.minimum(proj_in(wg_s, bg_ref), SWIGLU_LIMIT)
        up = jnp.clip(proj_in(wu_s, bu_ref), -SWIGLU_LIMIT, SWIGLU_LIMIT)
        glu = gate * _sigmoid(gate * SWIGLU_ALPHA)
        act = ((up + 1.0) * glu).astype(BF16)
        y = jnp.dot(act, wd_s[...], preferred_element_type=F32) + bd_ref[0]
        y_ref[:r, :] = y
        if r < M:
            y_ref[r:, :] = jnp.zeros((M - r, D_MODEL), y_ref.dtype)

    for r in range(EXPERT_SUB, M + 1, EXPERT_SUB):
        pl.when(rows == r)(functools.partial(compute, r))

    @pl.when(rows == 0)
    def _():
        y_ref[...] = jnp.zeros(y_ref.shape, y_ref.dtype)


def _experts(blk_exp, blk_rows, blk_slot, blk_next, x_rows, n_pad, w_g, b_g, w_u, b_u, w_d, b_d):
    M = EXPERT_BLOCK
    assert D_FF == D_MODEL
    bspec = lambda n: pl.BlockSpec((1, 1, n), lambda i, be, *_: (be[i], 0, 0))
    wspec = pl.BlockSpec(memory_space=pl.ANY)
    return pl.pallas_call(
        _expert_kernel,
        grid_spec=pltpu.PrefetchScalarGridSpec(
            num_scalar_prefetch=4,
            grid=(n_pad // M,),
            in_specs=[pl.BlockSpec((M, D_MODEL // 2), lambda i, *_: (i, 0)),
                      wspec, bspec(D_FF), wspec, bspec(D_FF), wspec, bspec(D_MODEL)],
            out_specs=pl.BlockSpec((M, D_MODEL), lambda i, *_: (i, 0)),
            scratch_shapes=[pltpu.VMEM((2, 3, D_MODEL, D_FF), F32),
                            pltpu.VMEM((D_MODEL, D_FF), BF16),
                            pltpu.VMEM((D_MODEL, D_FF), BF16),
                            pltpu.VMEM((D_FF, D_MODEL), BF16),
                            pltpu.SemaphoreType.DMA((2, 3))]),
        out_shape=jax.ShapeDtypeStruct((n_pad, D_MODEL), F32),
        compiler_params=_params(("arbitrary",), VMEM_LIMIT),
        name="experts",
    )(blk_exp, blk_rows, blk_slot, blk_next, x_rows, w_g, b_g, w_u, b_u, w_d, b_d)


def _final_kernel(toff_ref, eoff_ref, n_ref, trows_ref, x1_ref, posw_ref, p_ref, gpg_ref, wpg_ref, wpp_ref, gpp_ref,
                  gfin_ref, y_ref, o_ref, buf, xmid, sems, *, nt):
    tm = x1_ref.shape[0]
    step = pl.program_id(0)
    j = jnp.minimum(step, nt - 1)

    n_slots = buf.shape[0]
    cur = j % n_slots

    def tile_runs(t, act):
        slot = t % n_slots

        def body(e, c):
            r = t * N_EXPERTS + e
            t0, s0 = toff_ref[r], eoff_ref[r]
            _run_pieces(n_ref[r], tm // RUN_ALIGN, lambda done, size: act(pltpu.make_async_copy(
                y_ref.at[pl.ds(s0 + done, size)], buf.at[slot, pl.ds(t0 + done, size)], sems.at[slot])))
            return c

        lax.fori_loop(0, N_EXPERTS, body, 0, unroll=RUN_UNROLL)

    @pl.when(step == 0)
    def _():
        buf[...] = jnp.zeros(buf.shape, buf.dtype)
        xmid[...] = jnp.zeros(xmid.shape, xmid.dtype)
        for t in range(min(n_slots - 1, nt)):
            tile_runs(t, lambda cp: cp.start())

    @pl.when(step + n_slots - 1 < nt)
    def _():
        tile_runs(step + n_slots - 1, lambda cp: cp.start())

    @pl.when(step < nt)
    def _():
        _run_pieces(trows_ref[j], buf.shape[1], lambda done, size: pltpu.make_async_copy(
            y_ref.at[pl.ds(0, size)], buf.at[cur, pl.ds(0, size)], sems.at[cur]).wait())

    x = xmid[...]
    pp = jnp.dot(p_ref[...].astype(BF16), wpp_ref[...], preferred_element_type=F32)
    pg = _sigmoid(jnp.dot(_rms(x, gpg_ref[...]).astype(BF16), wpg_ref[...], preferred_element_type=F32))
    x = x + pg * _rms(pp, gpp_ref[...])
    o_ref[...] = _rms(x, gfin_ref[...])

    posw_t = posw_ref[...].T
    comb_t = _onehot_bands(posw_t[:TOP_K], posw_t[TOP_K:2 * TOP_K].astype(BF16))
    sure = TOP_K * tm
    sure_g = sure // RUN_ALIGN
    head = buf[cur, :sure_g].reshape(sure, D_MODEL)
    tail = buf[cur, sure_g:].reshape(TILE_ROWS - sure, D_MODEL)
    live = lax.broadcasted_iota(jnp.int32, tail.shape, 0) < trows_ref[j] * RUN_ALIGN - sure
    y = jnp.concatenate([head, jnp.where(live, tail, 0.0)], axis=0).astype(BF16)
    xmid[...] = x1_ref[...] + lax.dot_general(comb_t, y, TN, preferred_element_type=F32)


def _final(tile_off, expert_off, run_n, tile_rows, x1, y_rows, posw, p2, g_pg, w_pg, w_pp, g_pp, g_fin, tm):
    T = x1.shape[0]
    nt = T // tm
    full = lambda shape: pl.BlockSpec(shape, lambda i, *_: (0, 0))
    this = lambda w: pl.BlockSpec((tm, w), lambda i, *_: (jnp.minimum(i, nt - 1), 0))
    prev = lambda w: pl.BlockSpec((tm, w), lambda i, *_: (jnp.maximum(i - 1, 0), 0))
    return pl.pallas_call(
        functools.partial(_final_kernel, nt=nt),
        grid_spec=pltpu.PrefetchScalarGridSpec(
            num_scalar_prefetch=4,
            grid=(nt + 1,),
            in_specs=[this(D_MODEL), this(LANES), prev(PLE_DIM),
                      full((1, D_MODEL)), full((D_MODEL, D_MODEL)), full((PLE_DIM, D_MODEL)),
                      full((1, D_MODEL)), full((1, D_MODEL)),
                      pl.BlockSpec(memory_space=pl.ANY)],
            out_specs=prev(D_MODEL),
            scratch_shapes=[pltpu.VMEM((FINAL_SLOTS, TILE_ROWS // RUN_ALIGN, RUN_ALIGN, D_MODEL), F32),
                            pltpu.VMEM((tm, D_MODEL), F32),
                            pltpu.SemaphoreType.DMA((FINAL_SLOTS,))]),
        out_shape=jax.ShapeDtypeStruct((T, D_MODEL), F32),
        compiler_params=_params(("arbitrary",), VMEM_LIMIT),
        name="final",
    )(tile_off, expert_off, run_n, tile_rows, x1, posw, p2, g_pg, w_pg, w_pp, g_pp, g_fin,
      y_rows.reshape(-1, RUN_ALIGN, D_MODEL))


def _split_w_in(w_in):
    sizes = (GLA_QK, GLA_QK, GLA_V, GLA_V, GLA_RANK, MOBA_W, MOBA_W, MOBA_W, D_MODEL, D_MODEL)
    offs = [0]
    for s in sizes:
        offs.append(offs[-1] + s)
    main = jnp.concatenate([w_in[:, :offs[4]], w_in[:, offs[5]:]], axis=1).astype(BF16)
    alow = jnp.pad(w_in[:, offs[4]:offs[5]], ((0, 0), (0, LANES - GLA_RANK))).astype(BF16)
    return main, alow


def _layer(x2, p2, bias, B, S, g_mix, w_in, w_a2, b_a2, g_gla_out, w_up_gla, w_up_moba, w_o, g_ffn, w_router,
           b_router, w_e_gate, b_e_gate, w_e_up, b_e_up, w_e_down, b_e_down, g_ple_gate, w_ple_gate, w_ple_proj,
           g_ple_proj, g_final):
    T = B * S
    row = lambda v: v.reshape(1, -1).astype(F32)
    w_main, w_alow = _split_w_in(w_in)
    colscale = jnp.ones((D_PROJ,), F32)
    colscale = colscale.at[OFF_QA:OFF_QA + GLA_QK].set(GLA_DK ** -0.5)
    colscale = colscale.at[OFF_QB:OFF_QB + MOBA_W].set(MOBA_HD ** -0.5 * LOG2E)
    w_a2p = jnp.pad(w_a2, ((0, LANES - GLA_RANK), (0, 0))).astype(BF16)
    proj, glog = _inproj(x2, row(g_mix), w_main, colscale.reshape(1, -1), w_alow, w_a2p, row(b_a2))

    o_a = _gla(proj, glog, row(g_gla_out), B, S)
    o_b = _moba(proj, bias, B, S)

    w_r = jnp.pad(w_router.astype(F32), ((0, 0), (0, LANES - N_EXPERTS)))
    w_r_hi = w_r.astype(BF16)
    w_r2 = jnp.concatenate([w_r_hi, (w_r - w_r_hi.astype(F32)).astype(BF16)], axis=1)
    b_r = jnp.pad(b_router.astype(F32), (0, LANES - N_EXPERTS)).reshape(1, -1)
    tm = TOKEN_TILE
    nt = T // tm
    x1, h2, logits = _merge(o_a, o_b, proj, x2, w_up_gla.astype(BF16), w_up_moba.astype(BF16),
                            w_o.astype(BF16), row(g_ffn), w_r2, b_r, MERGE_TILE)
    posw, cnt_t, carry = _router(logits, tm)

    M = EXPERT_BLOCK
    A = nt * TILE_ROWS
    n_pad = (-(-A // M)) * M + N_EXPERTS * M
    n_blk = n_pad // M
    cnt_t = cnt_t[:, 0, :N_EXPERTS].astype(jnp.int32)
    carry = carry[:, 0, :N_EXPERTS].astype(jnp.int32)
    counts = carry[-1] + cnt_t[-1]
    padded = (counts + M - 1) // M * M
    pad_end = jnp.cumsum(padded)
    pad_start = pad_end - padded
    blk_exp = jnp.minimum(jnp.sum(pad_end[None, :] <= (jnp.arange(n_blk, dtype=jnp.int32) * M)[:, None], axis=1),
                          N_EXPERTS - 1).astype(jnp.int32)
    n_used = (pad_end[-1:] // M).astype(jnp.int32)
    blk_start = jnp.arange(n_blk, dtype=jnp.int32) * M
    eids = jnp.arange(N_EXPERTS, dtype=jnp.int32)

    def per_block(per_expert):
        return jnp.sum(jnp.where(blk_exp[:, None] == eids[None, :], per_expert[None, :], 0), axis=1).astype(jnp.int32)

    blk_rows = jnp.clip(per_block(pad_start + counts) - blk_start, 0, M)
    blk_rows = jnp.where(blk_start < pad_end[-1], (blk_rows + EXPERT_SUB - 1) // EXPERT_SUB * EXPERT_SUB, 0)
    has_rows = counts > 0
    blk_slot = per_block((jnp.cumsum(has_rows) - 1) % 2)
    later = jnp.where((eids[None, :] > eids[:, None]) & has_rows[None, :], eids[None, :], N_EXPERTS)
    next_exp = jnp.min(later, axis=1)
    blk_next = per_block(jnp.where(next_exp < N_EXPERTS, next_exp, -1))
    G = RUN_ALIGN
    tile_off = (jnp.cumsum(cnt_t, axis=1) - cnt_t).reshape(-1) // G
    expert_off = (carry + pad_start[None, :]).reshape(-1) // G
    run_n = cnt_t.reshape(-1) // G
    tile_rows = jnp.sum(cnt_t, axis=1) // G
    x_rows = _dispatch(h2, posw, n_pad, tile_off, expert_off, run_n, tile_rows, (pad_start + counts) // G,
                       (padded - counts) // G, n_used, tm)
    y_rows = _experts(blk_exp, blk_rows.astype(jnp.int32), blk_slot, blk_next, x_rows, n_pad,
                      w_e_gate, b_e_gate.reshape(N_EXPERTS, 1, D_FF),
                      w_e_up, b_e_up.reshape(N_EXPERTS, 1, D_FF), w_e_down,
                      b_e_down.reshape(N_EXPERTS, 1, D_MODEL))
    return _final(tile_off, expert_off, run_n, tile_rows, x1, y_rows, posw, p2, row(g_ple_gate),
                  w_ple_gate.astype(BF16), w_ple_proj.astype(BF16), row(g_ple_proj), row(g_final), tm)


def kernel(x, p, rel_bias, g_mix, w_in, w_a2, b_a2, g_gla_out, w_up_gla, w_up_moba, w_o, g_ffn, w_router, b_router,
           w_e_gate, b_e_gate, w_e_up, b_e_up, w_e_down, b_e_down, g_ple_gate, w_ple_gate, w_ple_proj, g_ple_proj,
           g_final):
    B, S, D = x.shape
    assert D == D_MODEL and S % MOBA_BLOCK == 0 and S % GLA_CHUNK == 0 and p.shape[0] == 1
    bias = _bias_tiles(rel_bias, S // MOBA_BLOCK)
    out = _layer(x.reshape(B * S, D), p[0].reshape(B * S, PLE_DIM), bias, B, S,
                 g_mix[0], w_in[0], w_a2[0], b_a2[0], g_gla_out[0], w_up_gla[0], w_up_moba[0], w_o[0], g_ffn[0],
                 w_router[0], b_router[0], w_e_gate[0], b_e_gate[0], w_e_up[0], b_e_up[0], w_e_down[0],
                 b_e_down[0], g_ple_gate[0], w_ple_gate[0], w_ple_proj[0], g_ple_proj[0], g_final)
    return out.reshape(B, S, D)
```

```python
import functools
import math

import jax
import jax.numpy as jnp
from jax import lax
from jax.experimental import pallas as pl
from jax.experimental.pallas import tpu as pltpu

F32 = jnp.float32
BF16 = jnp.bfloat16

D_MODEL = 1024
PLE_DIM = 256
GLA_HEADS = 4
GLA_DK = 128
GLA_DV = 256
GLA_RANK = 16
GLA_TAU = 16.0
GLA_QK = GLA_HEADS * GLA_DK
GLA_V = GLA_HEADS * GLA_DV
MOBA_HEADS = 8
MOBA_HD = 128
MOBA_BLOCK = 256
MOBA_TOPK = 3
MOBA_W = MOBA_HEADS * MOBA_HD
REL_BUCKETS = 32
REL_MAX_DIST = 4096
N_EXPERTS = 32
TOP_K = 4
D_FF = 1024
SWIGLU_LIMIT = 7.0
SWIGLU_ALPHA = 1.702
EPS = 1e-6

LANES = 128
NEG = -1e30
LOG2E = math.log2(math.e)
VMEM_LIMIT = 56 * 1024 * 1024

OFF_QA, OFF_KA, OFF_VA, OFF_RA = 0, 512, 1024, 2048
OFF_QB, OFF_KB, OFF_VB, OFF_GA, OFF_GB = 3072, 4096, 5120, 6144, 7168
D_PROJ = 8192

GLA_CHUNK = 128
EXPERT_BLOCK = 512
EXPERT_SUB = 128
TOKEN_TILE = 256
RUN_ALIGN = 8
TILE_ROWS = TOP_K * TOKEN_TILE + N_EXPERTS * RUN_ALIGN
RUN_UNROLL = 8
MERGE_TILE = 1024

NT = (((1,), (1,)), ((), ()))
TN = (((0,), (0,)), ((), ()))


def _params(sem, vmem=None):
    return pltpu.CompilerParams(dimension_semantics=sem, vmem_limit_bytes=vmem)


def _rms(x, g):
    return x * lax.rsqrt(jnp.mean(x * x, axis=-1, keepdims=True) + EPS) * g


def _sigmoid(x):
    return 1.0 / (1.0 + jnp.exp(-x))


def _bucket_of(n):
    max_exact = REL_BUCKETS // 2
    if n < max_exact:
        return n
    return min(max_exact + int(math.log(n / max_exact) / math.log(REL_MAX_DIST / max_exact)
                               * (REL_BUCKETS - max_exact)), REL_BUCKETS - 1)


def _bias_kernel(tab_ref, bkt_ref, o_ref, *, nblk):
    strip = 8
    m = pl.program_id(0)

    def fill(lo, hi):
        def body(s, carry):
            r0 = pl.multiple_of(s * strip, strip)
            b = bkt_ref[0, pl.ds(r0, strip), :]
            accs = [jnp.zeros(b.shape, F32) for _ in range(MOBA_HEADS)]
            for bb in range(lo, hi + 1):
                hit = b == bb
                for h in range(MOBA_HEADS):
                    accs[h] = jnp.where(hit, tab_ref[h, bb], accs[h])
            for h in range(MOBA_HEADS):
                o_ref[h, 0, pl.ds(r0, strip), :] = jnp.where(b < 0, NEG, accs[h])
            return carry

        lax.fori_loop(0, MOBA_BLOCK // strip, body, 0)

    for mm in range(nblk):
        lo = max(_bucket_of(max(mm * MOBA_BLOCK - (MOBA_BLOCK - 1), 0)) - 1, 0)
        hi = min(_bucket_of(mm * MOBA_BLOCK + MOBA_BLOCK - 1) + 1, REL_BUCKETS - 1)
        pl.when(m == mm)(functools.partial(fill, lo, hi))


def _bias_tiles(rel_bias, nblk):
    i = jnp.arange(MOBA_BLOCK, dtype=jnp.int32)
    dist = (jnp.arange(nblk, dtype=jnp.int32)[:, None, None] * MOBA_BLOCK + i[None, :, None] - i[None, None, :])
    n = jnp.maximum(dist, 0)
    max_exact = REL_BUCKETS // 2
    nf = jnp.maximum(n, 1).astype(F32)
    large = max_exact + (jnp.log(nf / max_exact) / math.log(REL_MAX_DIST / max_exact)
                         * (REL_BUCKETS - max_exact)).astype(jnp.int32)
    large = jnp.minimum(large, REL_BUCKETS - 1)
    bkt = jnp.where(dist < 0, -1, jnp.where(n < max_exact, n, large)).astype(jnp.int32)
    tab = rel_bias.astype(F32).T * LOG2E
    return pl.pallas_call(
        functools.partial(_bias_kernel, nblk=nblk),
        grid=(nblk,),
        in_specs=[pl.BlockSpec(memory_space=pltpu.SMEM),
                  pl.BlockSpec((1, MOBA_BLOCK, MOBA_BLOCK), lambda m: (m, 0, 0))],
        out_specs=pl.BlockSpec((MOBA_HEADS, 1, MOBA_BLOCK, MOBA_BLOCK), lambda m: (0, m, 0, 0)),
        out_shape=jax.ShapeDtypeStruct((MOBA_HEADS, nblk, MOBA_BLOCK, MOBA_BLOCK), F32),
        compiler_params=_params(("arbitrary",)),
        name="bias_tiles",
    )(tab, bkt)


def _inproj_kernel(x0_ref, xn_ref, g_ref, w_ref, cs_ref, wal_ref, wa2_ref, ba2_ref, o_ref, glog_ref, h_scr, al_scr,
                   *, n_j):
    i, j = pl.program_id(0), pl.program_id(1)
    tm = xn_ref.shape[0]
    rs = tm // n_j

    def gate_logits(a_low):
        a = jnp.dot(a_low, wa2_ref[...], preferred_element_type=F32) + ba2_ref[...]
        log_sig = jnp.minimum(a, 0.0) - jnp.log1p(jnp.exp(-jnp.abs(a)))
        return log_sig * (1.0 / GLA_TAU)

    def slice_rows(behind):
        return pl.multiple_of(((j + n_j - behind) % n_j) * rs, rs)

    @pl.when((i == 0) & (j == 0))
    def _():
        h0 = _rms(x0_ref[...], g_ref[...]).astype(BF16)
        h_scr[0] = h0
        a_low0 = jnp.dot(h0, wal_ref[...], preferred_element_type=F32).astype(BF16)
        glog_ref[...] = gate_logits(a_low0)
        al_scr[...] = a_low0[(n_j - 2) * rs:(n_j - 1) * rs, :]

    glog_ref[pl.ds(slice_rows(2), rs), :] = gate_logits(al_scr[...])
    slot = jnp.where(j == 0, i, i + 1) % 2
    h_lag = h_scr[slot, pl.ds(slice_rows(1), rs), :]
    al_scr[...] = jnp.dot(h_lag, wal_ref[...], preferred_element_type=F32).astype(BF16)

    acc = jnp.dot(h_scr[i % 2], w_ref[...], preferred_element_type=F32)
    o_ref[...] = (acc * cs_ref[...]).astype(BF16)

    h_scr[(i + 1) % 2, pl.ds(slice_rows(0), rs), :] = _rms(xn_ref[pl.ds(slice_rows(0), rs), :],
                                                             g_ref[...]).astype(BF16)


def _inproj(x2, g_mix, w_main, colscale, w_alow, w_a2p, b_a2, tm=1024, tn=2048):
    T = x2.shape[0]
    n_i, n_j = T // tm, D_PROJ // tn
    assert n_i >= 2 and tm % (8 * n_j) == 0
    return pl.pallas_call(
        functools.partial(_inproj_kernel, n_j=n_j),
        grid=(n_i, n_j),
        in_specs=[pl.BlockSpec((tm, D_MODEL), lambda i, j: (0, 0)),
                  pl.BlockSpec((tm, D_MODEL), lambda i, j: (jnp.minimum(i + 1, n_i - 1), 0)),
                  pl.BlockSpec((1, D_MODEL), lambda i, j: (0, 0)),
                  pl.BlockSpec((D_MODEL, tn), lambda i, j: (0, j)),
                  pl.BlockSpec((1, tn), lambda i, j: (0, j)),
                  pl.BlockSpec((D_MODEL, LANES), lambda i, j: (0, 0)),
                  pl.BlockSpec((LANES, GLA_QK), lambda i, j: (0, 0)),
                  pl.BlockSpec((1, GLA_QK), lambda i, j: (0, 0))],
        out_specs=[pl.BlockSpec((tm, tn), lambda i, j: (i, j)),
                   pl.BlockSpec((tm, GLA_QK), lambda i, j: (jnp.minimum(i + jnp.minimum(j // 2, 1), n_i - 1), 0))],
        out_shape=[jax.ShapeDtypeStruct((T, D_PROJ), BF16),
                   jax.ShapeDtypeStruct((T, GLA_QK), F32)],
        scratch_shapes=[pltpu.VMEM((2, tm, D_MODEL), BF16), pltpu.VMEM((tm // n_j, LANES), BF16)],
        compiler_params=_params(("arbitrary", "arbitrary"), VMEM_LIMIT),
        name="inproj",
    )(x2, x2, g_mix, w_main, colscale, w_alow, w_a2p, b_a2)


def _gla_kernel(q_ref, k_ref, v_ref, r_ref, g_ref, gout_ref, o_ref, st_ref):
    C = GLA_CHUNK

    @pl.when(pl.program_id(1) == 0)
    def _():
        st_ref[...] = jnp.zeros(st_ref.shape, F32)

    nb = q_ref.shape[0]
    row = lax.broadcasted_iota(jnp.int32, (C, C), 0)
    col = lax.broadcasted_iota(jnp.int32, (C, C), 1)
    causal = col <= row
    ltri = causal.astype(BF16)
    mid = C // 2
    pairs = [(b, h) for b in range(nb) for h in range(GLA_HEADS)]
    ks = lambda h: slice(h * GLA_DK, (h + 1) * GLA_DK)
    vs = lambda h: slice(h * GLA_DV, (h + 1) * GLA_DV)
    gate = {}
    for b, h in pairs:
        r = r_ref[b, :, vs(h)].astype(F32)
        gate[b, h] = r * _sigmoid(r)
    G = []
    for b in range(nb):
        g = g_ref[b]
        g_hi = g.astype(BF16)
        g_lo = (g - g_hi.astype(F32)).astype(BF16)
        G.append(jnp.dot(ltri, g_hi, preferred_element_type=F32) + jnp.dot(ltri, g_lo, preferred_element_type=F32))
    Gh = {(b, h): G[b][:, ks(h)] for b, h in pairs}
    qh = {(b, h): q_ref[b, :, ks(h)].astype(F32) for b, h in pairs}
    kh = {(b, h): k_ref[b, :, ks(h)].astype(F32) for b, h in pairs}
    g_mid = {p: Gh[p][mid:mid + 1, :] for p in pairs}
    g_last = {p: Gh[p][C - 1:C, :] for p in pairs}
    A = {p: lax.dot_general((qh[p] * jnp.exp(Gh[p] - g_mid[p])).astype(BF16),
                            (kh[p] * jnp.exp(g_mid[p] - Gh[p])).astype(BF16), NT, preferred_element_type=F32)
         for p in pairs}
    st = {p: st_ref[p[0], p[1]] for p in pairs}
    inter = {p: lax.dot_general((qh[p] * jnp.exp(Gh[p])).astype(BF16), st[p].astype(BF16), NT,
                                preferred_element_type=F32) for p in pairs}
    for b, h in pairs:
        p = (b, h)
        k_d = (kh[p] * jnp.exp(g_last[p] - Gh[p])).astype(BF16)
        st_ref[b, h] = jnp.exp(g_last[p]) * st[p] + lax.dot_general(v_ref[b, :, vs(h)], k_d, TN,
                                                                     preferred_element_type=F32)
    intra = {(b, h): jnp.dot(jnp.where(causal, A[b, h], 0.0).astype(BF16), v_ref[b, :, vs(h)],
                             preferred_element_type=F32) for b, h in pairs}
    for b, h in pairs:
        o = inter[b, h] + intra[b, h]
        o_ref[b, :, vs(h)] = (_rms(o, gout_ref[...]) * gate[b, h]).astype(BF16)


def _gla(proj, glog, g_gla_out, B, S, nb=4):
    C = GLA_CHUNK
    assert B % nb == 0
    proj3 = proj.reshape(B, S, D_PROJ)
    glog3 = glog.reshape(B, S, GLA_QK)
    spec = lambda w, off: pl.BlockSpec((nb, C, w), lambda b, c: (b, c, off // w))
    out = pl.pallas_call(
        _gla_kernel,
        grid=(B // nb, S // C),
        in_specs=[spec(GLA_QK, OFF_QA), spec(GLA_QK, OFF_KA), spec(GLA_V, OFF_VA), spec(GLA_V, OFF_RA),
                  spec(GLA_QK, 0), pl.BlockSpec((1, GLA_DV), lambda b, c: (0, 0))],
        out_specs=spec(GLA_V, 0),
        out_shape=jax.ShapeDtypeStruct((B, S, GLA_V), BF16),
        scratch_shapes=[pltpu.VMEM((nb, GLA_HEADS, GLA_DV, GLA_DK), F32)],
        compiler_params=_params(("arbitrary", "arbitrary")),
        name="gla",
    )(proj3, proj3, proj3, proj3, glog3, g_gla_out)
    return out.reshape(B * S, GLA_V)


def _moba_kernel(q_ref, k_ref, v_ref, bias_ref, o_ref, ka_scr, va_scr, lg_scr, mx_scr, qa_scr, *, nblk):
    BLK, HD = MOBA_BLOCK, MOBA_HD
    S = nblk * BLK
    units = range(q_ref.shape[0])

    @pl.when((pl.program_id(0) == 0) & (pl.program_id(1) == 0))
    def _():
        blk = lax.broadcasted_iota(jnp.int32, (S, HD), 0) // BLK
        lane = lax.broadcasted_iota(jnp.int32, (S, HD), 1)
        for u in units:
            ka_scr[u, :, HD:] = (lane == blk).astype(BF16)
            va_scr[u, :, HD:] = (lane == 0).astype(BF16)

    n_plain = min(MOBA_TOPK + 1, nblk)
    n_late = nblk - n_plain
    n_tiles = nblk * (nblk + 1) // 2

    def tile_id(cc, j):
        return cc * (cc + 1) // 2 + j

    def rows(cc):
        return slice(cc * BLK, (cc + 1) * BLK)

    def stage1(u, cc, q_in, keys):
        mx = None
        for j in range(cc + 1):
            lg = lax.dot_general(q_in, keys(j), NT, preferred_element_type=F32) + bias_ref[0, cc - j]
            lg_scr[u * n_tiles + tile_id(cc, j)] = lg
            t = jnp.maximum(lg[:, :LANES], lg[:, LANES:])
            mx = t if mx is None else jnp.maximum(mx, t)
            if j == cc:
                mx_scr[u * nblk + cc] = mx
            yield

    def stage2(u, cc):
        m = mx_scr[u * nblk + cc].max(axis=-1, keepdims=True)
        acc = jnp.zeros((BLK, 2 * HD), F32)
        for j in range(cc + 1):
            p = jnp.exp2(lg_scr[u * n_tiles + tile_id(cc, j)] - m).astype(BF16)
            acc = acc + jnp.dot(p, va_scr[u, rows(j), :], preferred_element_type=F32)
            if j == cc:
                o_ref[u, rows(cc), :] = (acc[:, :HD] / acc[:, HD:HD + 1]).astype(BF16)
            yield

    def drain(gen):
        for _ in gen:
            pass

    def interleave(main, side, n_main, n_side):
        side_steps = (s for g in side for s in g)
        done = 0
        for i, _ in enumerate(s for g in main for s in g):
            assert n_main > 0
            want = (i + 1) * n_side // n_main
            while done < want and next(side_steps, "end") != "end":
                done += 1
        drain(side_steps)

    plain = {u: [stage1(u, cc, q_ref[u, rows(cc), :], lambda j, u=u: k_ref[u, rows(j), :]) for cc in range(n_plain)]
             for u in units}
    if n_late:
        pens = []
        for u in units:
            next(plain[u][0])
            ksum = [k_ref[u, rows(j), :].astype(F32).reshape(BLK // 8, 8, HD).sum(axis=0).sum(axis=0, keepdims=True)
                    for j in range(nblk)]
            kmean = jnp.concatenate(ksum, axis=0) * (1.0 / BLK)
            km_hi = kmean.astype(BF16)
            km_lo = (kmean - km_hi.astype(F32)).astype(BF16)
            km2 = jnp.concatenate([km_hi, km_lo], axis=0)
            for cc in range(n_plain, nblk):
                q = q_ref[u, rows(cc), :]
                s2 = lax.dot_general(km2, q, NT, preferred_element_type=F32)
                pens.append((u, cc, q, s2[:nblk] + s2[nblk:]))
        for u in units:
            for g in plain[u][:2]:
                drain(g)
        for u, cc, q, s in pens:
            ji = lax.broadcasted_iota(jnp.int32, s.shape, 0)
            cnt = jnp.zeros(s.shape, F32)
            for jp in range(cc):
                sj = s[jp:jp + 1, :]
                beats = (sj > s) | ((sj == s) & (jp < ji))
                cnt = cnt + beats.astype(F32)
            pen = jnp.where((ji < cc) & (cnt >= MOBA_TOPK), NEG, 0.0)
            pen_t = jnp.concatenate([pen, jnp.zeros((HD - nblk, BLK), F32)], axis=0).T
            qa_scr[u * n_late + cc - n_plain] = jnp.concatenate([q, pen_t.astype(BF16)], axis=1)
    for u in units:
        ka_scr[u, :, :HD] = k_ref[u]
        va_scr[u, :, :HD] = v_ref[u]
    for u in units:
        for g in plain[u]:
            drain(g)

    @pl.when(pl.program_id(0) >= 0)
    def _():
        late = [stage1(u, cc, qa_scr[u * n_late + cc - n_plain], lambda j, u=u: ka_scr[u, rows(j), :])
                for u in units for cc in range(n_plain, nblk)]
        early = [stage2(u, cc) for u in units for cc in range(n_plain)]
        if late:
            interleave(late, early, len(units) * (n_tiles - tile_id(n_plain, 0)), len(units) * tile_id(n_plain, 0))
        else:
            for g in early:
                drain(g)

    @pl.when(pl.program_id(0) >= 0)
    def _():
        for u in units:
            for cc in range(n_plain, nblk):
                drain(stage2(u, cc))


def _moba(proj, bias, B, S, nb=2):
    BLK = MOBA_BLOCK
    nblk = S // BLK
    H = MOBA_HEADS
    assert nblk <= MOBA_HD and B % nb == 0
    proj3 = proj.reshape(B, S, D_PROJ)
    spec = lambda off: pl.BlockSpec((nb, S, MOBA_HD), lambda h, b: (b, 0, off // MOBA_HD + h))
    out = pl.pallas_call(
        functools.partial(_moba_kernel, nblk=nblk),
        grid=(H, B // nb),
        in_specs=[spec(OFF_QB), spec(OFF_KB), spec(OFF_VB),
                  pl.BlockSpec((1, nblk, BLK, BLK), lambda h, b: (h, 0, 0, 0))],
        out_specs=spec(0),
        out_shape=jax.ShapeDtypeStruct((B, S, MOBA_W), BF16),
        scratch_shapes=[pltpu.VMEM((nb, S, 2 * MOBA_HD), BF16), pltpu.VMEM((nb, S, 2 * MOBA_HD), BF16),
                        pltpu.VMEM((nb * nblk * (nblk + 1) // 2, BLK, BLK), F32),
                        pltpu.VMEM((nb * nblk, BLK, LANES), F32),
                        pltpu.VMEM((nb * max(nblk - MOBA_TOPK - 1, 1), BLK, 2 * MOBA_HD), BF16)],
        compiler_params=_params(("arbitrary", "arbitrary"), VMEM_LIMIT),
        name="moba",
    )(proj3, proj3, proj3, bias)
    return out.reshape(B * S, MOBA_W)


def _pack_exact(lo, hi):
    lo_b = lax.bitcast_convert_type(lo, jnp.uint32)
    hi_b = lax.bitcast_convert_type(hi, jnp.uint32)
    return (lo_b >> 16) | (hi_b & jnp.uint32(0xFFFF0000))


def _unpack(w):
    lo = lax.bitcast_convert_type(w << 16, F32)
    hi = lax.bitcast_convert_type(w & jnp.uint32(0xFFFF0000), F32)
    return lo.astype(BF16), hi.astype(BF16)


def _merge_kernel(oa_ref, ob_ref, ga_ref, gb_ref, x_ref, wua_ref, wub_ref, wo_ref, gffn_ref, wr_ref, br_ref,
                  x1_ref, h2_ref, lg_ref):
    tm = x_ref.shape[0]
    u_a = jnp.dot(oa_ref[...], wua_ref[...], preferred_element_type=F32)
    u_b = jnp.dot(ob_ref[...], wub_ref[...], preferred_element_type=F32)
    y = _sigmoid(ga_ref[...].astype(F32)) * u_a + _sigmoid(gb_ref[...].astype(F32)) * u_b
    x1 = x_ref[...] + jnp.dot(y.astype(BF16), wo_ref[...], preferred_element_type=F32)
    x1_ref[...] = x1
    h2 = _rms(x1, gffn_ref[...])
    h_hi = h2.astype(BF16)
    h2_ref[...] = h_hi
    h_lo = (h2 - h_hi.astype(F32)).astype(BF16)
    r = jnp.dot(jnp.concatenate([h_hi, h_lo], axis=0), wr_ref[...], preferred_element_type=F32)
    lg_ref[...] = r[:tm, :LANES] + r[:tm, LANES:] + r[tm:, :LANES] + br_ref[...]


def _merge(o_a, o_b, proj, x2, w_ua, w_ub, w_o, g_ffn, w_r2, b_r, tm):
    T = x2.shape[0]
    full = lambda shape: pl.BlockSpec(shape, lambda i: (0, 0))
    rowblk = lambda w: pl.BlockSpec((tm, w), lambda i: (i, 0))
    return pl.pallas_call(
        _merge_kernel,
        grid=(T // tm,),
        in_specs=[rowblk(GLA_V), rowblk(MOBA_W),
                  pl.BlockSpec((tm, D_MODEL), lambda i: (i, OFF_GA // D_MODEL)),
                  pl.BlockSpec((tm, D_MODEL), lambda i: (i, OFF_GB // D_MODEL)),
                  rowblk(D_MODEL),
                  full((GLA_V, D_MODEL)), full((MOBA_W, D_MODEL)), full((D_MODEL, D_MODEL)),
                  full((1, D_MODEL)), full((D_MODEL, 2 * LANES)), full((1, LANES))],
        out_specs=[rowblk(D_MODEL), rowblk(D_MODEL), rowblk(LANES)],
        out_shape=[jax.ShapeDtypeStruct((T, D_MODEL), F32),
                   jax.ShapeDtypeStruct((T, D_MODEL), BF16),
                   jax.ShapeDtypeStruct((T, LANES), F32)],
        compiler_params=_params(("arbitrary",), VMEM_LIMIT),
        name="merge",
    )(o_a, o_b, proj, proj, x2, w_ua, w_ub, w_o, g_ffn, w_r2, b_r)


def _router_kernel(lg_ref, posw_ref, cnt_ref, carry_ref, cnt_scr, *, tm):
    rows = lg_ref.shape[0]

    @pl.when(pl.program_id(0) == 0)
    def _():
        cnt_scr[...] = jnp.zeros(cnt_scr.shape, F32)

    lane = lax.broadcasted_iota(jnp.int32, (rows, LANES), 1)
    lane_f = lane.astype(F32)
    work = jnp.where(lane < N_EXPERTS, lg_ref[...], NEG)
    vals, hots = [], []
    for _ in range(TOP_K):
        mx = work.max(axis=-1, keepdims=True)
        idx = jnp.min(jnp.where(work == mx, lane_f, float(LANES)), axis=-1, keepdims=True)
        hot = lane_f == idx
        vals.append(mx)
        hots.append(hot)
        work = jnp.where(hot, 2.0 * NEG, work)
    exps = [jnp.exp(v - vals[0]) for v in vals]
    den = exps[0] + exps[1] + exps[2] + exps[3]
    sel = jnp.zeros((rows, LANES), F32)
    for hot in hots:
        sel = sel + hot.astype(F32)
    row = lax.broadcasted_iota(jnp.int32, (tm, tm), 0)
    col = lax.broadcasted_iota(jnp.int32, (tm, tm), 1)
    below = (col < row).astype(BF16)
    er = lax.broadcasted_iota(jnp.int32, (LANES, LANES), 0)
    ec = lax.broadcasted_iota(jnp.int32, (LANES, LANES), 1)
    before = (er < ec).astype(F32)
    pos_parts = []
    for t in range(rows // tm):
        sel_t = sel[t * tm:(t + 1) * tm]
        local_rank = jnp.dot(below, sel_t.astype(BF16), preferred_element_type=F32)
        cnt_t = sel_t.sum(axis=0, keepdims=True)
        cnt_t = jnp.floor((cnt_t + (RUN_ALIGN - 1.0)) * (1.0 / RUN_ALIGN)) * RUN_ALIGN
        tile_off = jnp.dot(jnp.broadcast_to(cnt_t, (8, LANES)), before, preferred_element_type=F32,
                           precision=lax.Precision.HIGHEST)[0:1]
        pos_parts.append(local_rank + tile_off)
        carry_ref[t] = cnt_scr[...]
        cnt_ref[t] = cnt_t
        cnt_scr[...] = cnt_scr[...] + cnt_t
    pos_all = jnp.concatenate(pos_parts, axis=0)
    posw = jnp.zeros((rows, LANES), F32)
    for k in range(TOP_K):
        pk = jnp.sum(jnp.where(hots[k], pos_all, 0.0), axis=-1, keepdims=True)
        posw = jnp.where(lane == k, pk, posw)
        posw = jnp.where(lane == TOP_K + k, exps[k] / den, posw)
    posw_ref[...] = posw


def _router(logits, tm, tiles_per_step=4):
    T = logits.shape[0]
    nt = T // tm
    rows = tm * tiles_per_step
    tilerow = pl.BlockSpec((tiles_per_step, 1, LANES), lambda i: (i, 0, 0))
    return pl.pallas_call(
        functools.partial(_router_kernel, tm=tm),
        grid=(T // rows,),
        in_specs=[pl.BlockSpec((rows, LANES), lambda i: (i, 0))],
        out_specs=[pl.BlockSpec((rows, LANES), lambda i: (i, 0)), tilerow, tilerow],
        out_shape=[jax.ShapeDtypeStruct((T, LANES), F32),
                   jax.ShapeDtypeStruct((nt, 1, LANES), F32),
                   jax.ShapeDtypeStruct((nt, 1, LANES), F32)],
        scratch_shapes=[pltpu.VMEM((1, LANES), F32)],
        compiler_params=_params(("arbitrary",)),
        name="router",
    )(logits)


def _run_pieces(n, max_n, fn):
    for b in reversed(range(max_n.bit_length())):
        size = 1 << b
        done = n & ~((2 << b) - 1)

        @pl.when((n & size) != 0)
        def _():
            fn(done, size)


def _onehot_bands(pos, val):
    band = 256
    assert TILE_ROWS % band == 0
    n_k, n_tok = pos.shape
    pos_a = jnp.floor(pos * (1.0 / band))
    pos_b = pos - band * pos_a
    sub = lax.broadcasted_iota(jnp.int32, (band, n_tok), 0).astype(F32).astype(BF16)
    zero = jnp.zeros((band, n_tok), BF16)
    bands = []
    for a in range(TILE_ROWS // band):
        want = jnp.where(pos_a == a, pos_b, -1.0).astype(BF16)
        hit = zero
        for k in range(n_k):
            row = jnp.broadcast_to(want[k:k + 1, :], (band, n_tok))
            fill = jnp.ones((band, n_tok), BF16) if val is None else jnp.broadcast_to(val[k:k + 1, :], (band, n_tok))
            hit = hit + jnp.where(sub == row, fill, zero)
        bands.append(hit)
    return jnp.concatenate(bands, axis=0)


def _dispatch_kernel(toff_ref, eoff_ref, n_ref, trows_ref, zoff_ref, zn_ref, tail_ref, h2_ref, posw_ref, x_ref,
                     buf, zbuf, sems, *, nt):
    tm = h2_ref.shape[0]
    half = D_MODEL // 2
    j = pl.program_id(0)
    zgroups = zbuf.shape[0]
    tile_groups = buf.shape[1]

    def tile_runs(t, act):
        slot = t % 2

        def body(e, c):
            r = t * N_EXPERTS + e
            t0, d0 = toff_ref[r], eoff_ref[r]
            _run_pieces(n_ref[r], tm // RUN_ALIGN, lambda done, size: act(pltpu.make_async_copy(
                buf.at[slot, pl.ds(t0 + done, size)], x_ref.at[pl.ds(d0 + done, size)], sems.at[slot])))
            return c

        lax.fori_loop(0, N_EXPERTS, body, 0, unroll=RUN_UNROLL)

    def zero_fill(act):
        def body(e, c):
            d0 = zoff_ref[e]
            _run_pieces(zn_ref[e], zgroups, lambda done, size: act(pltpu.make_async_copy(
                zbuf.at[pl.ds(0, size)], x_ref.at[pl.ds(d0 + done, size)], sems.at[2])))
            return c

        lax.fori_loop(0, N_EXPERTS, body, 0)

        def tail(i, c):
            act(pltpu.make_async_copy(zbuf, x_ref.at[pl.ds(i * zgroups, zgroups)], sems.at[2]))
            return c

        lax.fori_loop(tail_ref[0], x_ref.shape[0] // zgroups, tail, 0)

    start = lambda cp: cp.start()
    wait = lambda cp: cp.wait()

    def wait_tile(t):
        slot = t % 2
        _run_pieces(trows_ref[t], tile_groups, lambda done, size: pltpu.make_async_copy(
            buf.at[slot, pl.ds(0, size)], x_ref.at[pl.ds(0, size)], sems.at[slot]).wait())

    @pl.when(j == 0)
    def _():
        zbuf[...] = jnp.zeros(zbuf.shape, zbuf.dtype)
        zero_fill(start)

    @pl.when(j >= 2)
    def _():
        wait_tile(j - 2)

    perm = _onehot_bands(posw_ref[...].T[:TOP_K], None)
    xs = jnp.dot(perm, h2_ref[...], preferred_element_type=F32)
    buf[j % 2] = _pack_exact(xs[:, :half], xs[:, half:]).reshape(tile_groups, RUN_ALIGN, half)
    tile_runs(j, start)

    @pl.when(j == nt - 1)
    def _():
        if nt >= 2:
            wait_tile(j - 1)
        wait_tile(j)
        zero_fill(wait)


def _dispatch(h2, posw, n_rows, tile_off, expert_off, run_n, tile_rows, zoff, zn, tail, tm):
    T = h2.shape[0]
    nt = T // tm
    G = RUN_ALIGN
    assert n_rows % EXPERT_BLOCK == 0 and EXPERT_BLOCK % G == 0 and TILE_ROWS % G == 0
    x_rows = pl.pallas_call(
        functools.partial(_dispatch_kernel, nt=nt),
        grid_spec=pltpu.PrefetchScalarGridSpec(
            num_scalar_prefetch=7,
            grid=(nt,),
            in_specs=[pl.BlockSpec((tm, D_MODEL), lambda i, *_: (i, 0)),
                      pl.BlockSpec((tm, LANES), lambda i, *_: (i, 0))],
            out_specs=pl.BlockSpec(memory_space=pl.ANY),
            scratch_shapes=[pltpu.VMEM((2, TILE_ROWS // G, G, D_MODEL // 2), jnp.uint32),
                            pltpu.VMEM((EXPERT_BLOCK // G, G, D_MODEL // 2), jnp.uint32),
                            pltpu.SemaphoreType.DMA((3,))]),
        out_shape=jax.ShapeDtypeStruct((n_rows // G, G, D_MODEL // 2), jnp.uint32),
        compiler_params=_params(("arbitrary",), VMEM_LIMIT),
        name="dispatch",
    )(tile_off, expert_off, run_n, tile_rows, zoff, zn, tail, h2, posw)
    return x_rows.reshape(n_rows, D_MODEL // 2)


def _expert_kernel(be_ref, rows_ref, slot_ref, next_ref, x_ref, wg_ref, bg_ref, wu_ref, bu_ref, wd_ref, bd_ref, y_ref,
                   w_in, wg_s, wu_s, wd_s, sems):
    i = pl.program_id(0)
    half = D_MODEL // 2
    M = x_ref.shape[0]
    rows = rows_ref[i]
    e = be_ref[i]
    prev = be_ref[jnp.maximum(i - 1, 0)]

    def weight_copies(expert, slot):
        return [pltpu.make_async_copy(w_hbm.at[expert], w_in.at[slot, k], sems.at[slot, k])
                for k, w_hbm in enumerate((wg_ref, wu_ref, wd_ref))]

    @pl.when((rows > 0) & ((i == 0) | (e != prev)))
    def _():
        slot = slot_ref[i]

        @pl.when(i == 0)
        def _():
            for cp in weight_copies(e, slot):
                cp.start()

        nxt = next_ref[i]

        @pl.when(nxt >= 0)
        def _():
            for cp in weight_copies(nxt, 1 - slot):
                cp.start()

        for cp in weight_copies(e, slot):
            cp.wait()
        for k, dst in enumerate((wg_s, wu_s, wd_s)):
            dst[...] = w_in[slot, k].astype(BF16)

    def compute(r):
        x_lo, x_hi = _unpack(x_ref[:r, :])

        def proj_in(w_s, b_ref):
            return (jnp.dot(x_lo, w_s[:half, :], preferred_element_type=F32)
                    + jnp.dot(x_hi, w_s[half:, :], preferred_element_type=F32) + b_ref[0])

        gate = jnp.minimum(proj_in(wg_s, bg_ref), SWIGLU_LIMIT)
        up = jnp.clip(proj_in(wu_s, bu_ref), -SWIGLU_LIMIT, SWIGLU_LIMIT)
        glu = gate * _sigmoid(gate * SWIGLU_ALPHA)
        act = ((up + 1.0) * glu).astype(BF16)
        y = jnp.dot(act, wd_s[...], preferred_element_type=F32) + bd_ref[0]
        y_ref[:r, :] = y
        if r < M:
            y_ref[r:, :] = jnp.zeros((M - r, D_MODEL), y_ref.dtype)

    for r in range(EXPERT_SUB, M + 1, EXPERT_SUB):
        pl.when(rows == r)(functools.partial(compute, r))

    @pl.when(rows == 0)
    def _():
        y_ref[...] = jnp.zeros(y_ref.shape, y_ref.dtype)


def _experts(blk_exp, blk_rows, blk_slot, blk_next, x_rows, n_pad, w_g, b_g, w_u, b_u, w_d, b_d):
    M = EXPERT_BLOCK
    assert D_FF == D_MODEL
    bspec = lambda n: pl.BlockSpec((1, 1, n), lambda i, be, *_: (be[i], 0, 0))
    wspec = pl.BlockSpec(memory_space=pl.ANY)
    return pl.pallas_call(
        _expert_kernel,
        grid_spec=pltpu.PrefetchScalarGridSpec(
            num_scalar_prefetch=4,
            grid=(n_pad // M,),
            in_specs=[pl.BlockSpec((M, D_MODEL // 2), lambda i, *_: (i, 0)),
                      wspec, bspec(D_FF), wspec, bspec(D_FF), wspec, bspec(D_MODEL)],
            out_specs=pl.BlockSpec((M, D_MODEL), lambda i, *_: (i, 0)),
            scratch_shapes=[pltpu.VMEM((2, 3, D_MODEL, D_FF), F32),
                            pltpu.VMEM((D_MODEL, D_FF), BF16),
                            pltpu.VMEM((D_MODEL, D_FF), BF16),
                            pltpu.VMEM((D_FF, D_MODEL), BF16),
                            pltpu.SemaphoreType.DMA((2, 3))]),
        out_shape=jax.ShapeDtypeStruct((n_pad, D_MODEL), F32),
        compiler_params=_params(("arbitrary",), VMEM_LIMIT),
        name="experts",
    )(blk_exp, blk_rows, blk_slot, blk_next, x_rows, w_g, b_g, w_u, b_u, w_d, b_d)


def _final_kernel(toff_ref, eoff_ref, n_ref, trows_ref, x1_ref, posw_ref, p_ref, gpg_ref, wpg_ref, wpp_ref, gpp_ref,
                  gfin_ref, y_ref, o_ref, buf, xmid, sems, *, nt):
    tm = x1_ref.shape[0]
    step = pl.program_id(0)
    j = jnp.minimum(step, nt - 1)

    def tile_runs(t, act):
        slot = t % 2

        def body(e, c):
            r = t * N_EXPERTS + e
            t0, s0 = toff_ref[r], eoff_ref[r]
            _run_pieces(n_ref[r], tm // RUN_ALIGN, lambda done, size: act(pltpu.make_async_copy(
                y_ref.at[pl.ds(s0 + done, size)], buf.at[slot, pl.ds(t0 + done, size)], sems.at[slot])))
            return c

        lax.fori_loop(0, N_EXPERTS, body, 0, unroll=RUN_UNROLL)

    @pl.when(step == 0)
    def _():
        buf[...] = jnp.zeros(buf.shape, buf.dtype)
        xmid[...] = jnp.zeros(xmid.shape, xmid.dtype)
        tile_runs(0, lambda cp: cp.start())

    @pl.when(step + 1 < nt)
    def _():
        tile_runs(step + 1, lambda cp: cp.start())

    @pl.when(step < nt)
    def _():
        _run_pieces(trows_ref[j], buf.shape[1], lambda done, size: pltpu.make_async_copy(
            y_ref.at[pl.ds(0, size)], buf.at[j % 2, pl.ds(0, size)], sems.at[j % 2]).wait())

    x = xmid[...]
    pp = jnp.dot(p_ref[...].astype(BF16), wpp_ref[...], preferred_element_type=F32)
    pg = _sigmoid(jnp.dot(_rms(x, gpg_ref[...]).astype(BF16), wpg_ref[...], preferred_element_type=F32))
    x = x + pg * _rms(pp, gpp_ref[...])
    o_ref[...] = _rms(x, gfin_ref[...])

    posw_t = posw_ref[...].T
    comb_t = _onehot_bands(posw_t[:TOP_K], posw_t[TOP_K:2 * TOP_K].astype(BF16))
    sure = TOP_K * tm
    sure_g = sure // RUN_ALIGN
    head = buf[j % 2, :sure_g].reshape(sure, D_MODEL)
    tail = buf[j % 2, sure_g:].reshape(TILE_ROWS - sure, D_MODEL)
    live = lax.broadcasted_iota(jnp.int32, tail.shape, 0) < trows_ref[j] * RUN_ALIGN - sure
    y = jnp.concatenate([head, jnp.where(live, tail, 0.0)], axis=0).astype(BF16)
    xmid[...] = x1_ref[...] + lax.dot_general(comb_t, y, TN, preferred_element_type=F32)


def _final(tile_off, expert_off, run_n, tile_rows, x1, y_rows, posw, p2, g_pg, w_pg, w_pp, g_pp, g_fin, tm):
    T = x1.shape[0]
    nt = T // tm
    full = lambda shape: pl.BlockSpec(shape, lambda i, *_: (0, 0))
    this = lambda w: pl.BlockSpec((tm, w), lambda i, *_: (jnp.minimum(i, nt - 1), 0))
    prev = lambda w: pl.BlockSpec((tm, w), lambda i, *_: (jnp.maximum(i - 1, 0), 0))
    return pl.pallas_call(
        functools.partial(_final_kernel, nt=nt),
        grid_spec=pltpu.PrefetchScalarGridSpec(
            num_scalar_prefetch=4,
            grid=(nt + 1,),
            in_specs=[this(D_MODEL), this(LANES), prev(PLE_DIM),
                      full((1, D_MODEL)), full((D_MODEL, D_MODEL)), full((PLE_DIM, D_MODEL)),
                      full((1, D_MODEL)), full((1, D_MODEL)),
                      pl.BlockSpec(memory_space=pl.ANY)],
            out_specs=prev(D_MODEL),
            scratch_shapes=[pltpu.VMEM((2, TILE_ROWS // RUN_ALIGN, RUN_ALIGN, D_MODEL), F32),
                            pltpu.VMEM((tm, D_MODEL), F32),
                            pltpu.SemaphoreType.DMA((2,))]),
        out_shape=jax.ShapeDtypeStruct((T, D_MODEL), F32),
        compiler_params=_params(("arbitrary",), VMEM_LIMIT),
        name="final",
    )(tile_off, expert_off, run_n, tile_rows, x1, posw, p2, g_pg, w_pg, w_pp, g_pp, g_fin,
      y_rows.reshape(-1, RUN_ALIGN, D_MODEL))


def _split_w_in(w_in):
    sizes = (GLA_QK, GLA_QK, GLA_V, GLA_V, GLA_RANK, MOBA_W, MOBA_W, MOBA_W, D_MODEL, D_MODEL)
    offs = [0]
    for s in sizes:
        offs.append(offs[-1] + s)
    main = jnp.concatenate([w_in[:, :offs[4]], w_in[:, offs[5]:]], axis=1).astype(BF16)
    alow = jnp.pad(w_in[:, offs[4]:offs[5]], ((0, 0), (0, LANES - GLA_RANK))).astype(BF16)
    return main, alow


def _layer(x2, p2, bias, B, S, g_mix, w_in, w_a2, b_a2, g_gla_out, w_up_gla, w_up_moba, w_o, g_ffn, w_router,
           b_router, w_e_gate, b_e_gate, w_e_up, b_e_up, w_e_down, b_e_down, g_ple_gate, w_ple_gate, w_ple_proj,
           g_ple_proj, g_final):
    T = B * S
    row = lambda v: v.reshape(1, -1).astype(F32)
    w_main, w_alow = _split_w_in(w_in)
    colscale = jnp.ones((D_PROJ,), F32)
    colscale = colscale.at[OFF_QA:OFF_QA + GLA_QK].set(GLA_DK ** -0.5)
    colscale = colscale.at[OFF_QB:OFF_QB + MOBA_W].set(MOBA_HD ** -0.5 * LOG2E)
    w_a2p = jnp.pad(w_a2, ((0, LANES - GLA_RANK), (0, 0))).astype(BF16)
    proj, glog = _inproj(x2, row(g_mix), w_main, colscale.reshape(1, -1), w_alow, w_a2p, row(b_a2))

    o_a = _gla(proj, glog, row(g_gla_out), B, S)
    o_b = _moba(proj, bias, B, S)

    w_r = jnp.pad(w_router.astype(F32), ((0, 0), (0, LANES - N_EXPERTS)))
    w_r_hi = w_r.astype(BF16)
    w_r2 = jnp.concatenate([w_r_hi, (w_r - w_r_hi.astype(F32)).astype(BF16)], axis=1)
    b_r = jnp.pad(b_router.astype(F32), (0, LANES - N_EXPERTS)).reshape(1, -1)
    tm = TOKEN_TILE
    nt = T // tm
    x1, h2, logits = _merge(o_a, o_b, proj, x2, w_up_gla.astype(BF16), w_up_moba.astype(BF16),
                            w_o.astype(BF16), row(g_ffn), w_r2, b_r, MERGE_TILE)
    posw, cnt_t, carry = _router(logits, tm)

    M = EXPERT_BLOCK
    A = nt * TILE_ROWS
    n_pad = (-(-A // M)) * M + N_EXPERTS * M
    n_blk = n_pad // M
    cnt_t = cnt_t[:, 0, :N_EXPERTS].astype(jnp.int32)
    carry = carry[:, 0, :N_EXPERTS].astype(jnp.int32)
    counts = carry[-1] + cnt_t[-1]
    padded = (counts + M - 1) // M * M
    pad_end = jnp.cumsum(padded)
    pad_start = pad_end - padded
    blk_exp = jnp.minimum(jnp.sum(pad_end[None, :] <= (jnp.arange(n_blk, dtype=jnp.int32) * M)[:, None], axis=1),
                          N_EXPERTS - 1).astype(jnp.int32)
    n_used = (pad_end[-1:] // M).astype(jnp.int32)
    blk_start = jnp.arange(n_blk, dtype=jnp.int32) * M
    eids = jnp.arange(N_EXPERTS, dtype=jnp.int32)

    def per_block(per_expert):
        return jnp.sum(jnp.where(blk_exp[:, None] == eids[None, :], per_expert[None, :], 0), axis=1).astype(jnp.int32)

    blk_rows = jnp.clip(per_block(pad_start + counts) - blk_start, 0, M)
    blk_rows = jnp.where(blk_start < pad_end[-1], (blk_rows + EXPERT_SUB - 1) // EXPERT_SUB * EXPERT_SUB, 0)
    has_rows = counts > 0
    blk_slot = per_block((jnp.cumsum(has_rows) - 1) % 2)
    later = jnp.where((eids[None, :] > eids[:, None]) & has_rows[None, :], eids[None, :], N_EXPERTS)
    next_exp = jnp.min(later, axis=1)
    blk_next = per_block(jnp.where(next_exp < N_EXPERTS, next_exp, -1))
    G = RUN_ALIGN
    tile_off = (jnp.cumsum(cnt_t, axis=1) - cnt_t).reshape(-1) // G
    expert_off = (carry + pad_start[None, :]).reshape(-1) // G
    run_n = cnt_t.reshape(-1) // G
    tile_rows = jnp.sum(cnt_t, axis=1) // G
    x_rows = _dispatch(h2, posw, n_pad, tile_off, expert_off, run_n, tile_rows, (pad_start + counts) // G,
                       (padded - counts) // G, n_used, tm)
    y_rows = _experts(blk_exp, blk_rows.astype(jnp.int32), blk_slot, blk_next, x_rows, n_pad,
                      w_e_gate, b_e_gate.reshape(N_EXPERTS, 1, D_FF),
                      w_e_up, b_e_up.reshape(N_EXPERTS, 1, D_FF), w_e_down,
                      b_e_down.reshape(N_EXPERTS, 1, D_MODEL))
    return _final(tile_off, expert_off, run_n, tile_rows, x1, y_rows, posw, p2, row(g_ple_gate),
                  w_ple_gate.astype(BF16), w_ple_proj.astype(BF16), row(g_ple_proj), row(g_final), tm)


def kernel(x, p, rel_bias, g_mix, w_in, w_a2, b_a2, g_gla_out, w_up_gla, w_up_moba, w_o, g_ffn, w_router, b_router,
           w_e_gate, b_e_gate, w_e_up, b_e_up, w_e_down, b_e_down, g_ple_gate, w_ple_gate, w_ple_proj, g_ple_proj,
           g_final):
    B, S, D = x.shape
    assert D == D_MODEL and S % MOBA_BLOCK == 0 and S % GLA_CHUNK == 0 and p.shape[0] == 1
    bias = _bias_tiles(rel_bias, S // MOBA_BLOCK)
    out = _layer(x.reshape(B * S, D), p[0].reshape(B * S, PLE_DIM), bias, B, S,
                 g_mix[0], w_in[0], w_a2[0], b_a2[0], g_gla_out[0], w_up_gla[0], w_up_moba[0], w_o[0], g_ffn[0],
                 w_router[0], b_router[0], w_e_gate[0], b_e_gate[0], w_e_up[0], b_e_up[0], w_e_down[0],
                 b_e_down[0], g_ple_gate[0], w_ple_gate[0], w_ple_proj[0], g_ple_proj[0], g_final)
    return out.reshape(B, S, D)
```
